```python
import jax
import jax.numpy as jnp
from jax import lax
import numpy as np

D_MODEL = 1024
BATCH = 8
SEQ = 2048
DEPTH = 1

N_HEADS = 16
HEAD_DIM = 64
N_KV = 4
GROUP = N_HEADS // N_KV
ROT_DIM = HEAD_DIM // 4
ROPE_THETA = 500000.0
CMP_LEN = 32
CMP_STRIDE = 16
CMP_HIDDEN = 2 * HEAD_DIM
SEL_LEN = 64
N_SEL = 8
WINDOW = 512
Q_BLOCK = 128
SEL_Q_BLOCK = 64
POOL_WINDOWS = (2, 4, 8, 16)
POOL_WIDTH = D_MODEL // 2
POOL_GROUP = POOL_WIDTH // len(POOL_WINDOWS)
D_FF = 2816
EPS = 1e-6
NEG_INF = -1e30
FORCE_SCORE = 1e4
Q_WIDTH = N_HEADS * HEAD_DIM
KV_WIDTH = N_KV * HEAD_DIM
IN_WIDTHS = (Q_WIDTH, KV_WIDTH, KV_WIDTH, KV_WIDTH, KV_WIDTH, KV_WIDTH, KV_WIDTH, 3 * N_HEADS, POOL_WIDTH, 2 * D_MODEL)
IN_TOTAL = sum(IN_WIDTHS)

kernel_name = 'hybrid_nsa_pool_macaron_layer'


def rms_norm(x, g):
    xf = x.astype(jnp.float32)
    y = xf * lax.rsqrt(jnp.mean(xf * xf, axis=-1, keepdims=True) + EPS)
    return (y * g.astype(jnp.float32)).astype(x.dtype)


def swiglu(h, w_gate, w_up, w_down):
    return (jax.nn.silu(h @ w_gate) * (h @ w_up)) @ w_down


def partial_rope(x, pos):
    half = ROT_DIM // 2
    inv = ROPE_THETA ** (-jnp.arange(half, dtype=jnp.float32) * (2.0 / ROT_DIM))
    ang = pos.astype(jnp.float32)[..., None] * inv
    ang = ang.reshape(ang.shape[:2] + (1,) * (x.ndim - 3) + (half,))
    cos, sin = jnp.cos(ang), jnp.sin(ang)
    xr = x[..., :ROT_DIM].astype(jnp.float32)
    x1, x2 = xr[..., :half], xr[..., half:]
    rot = jnp.concatenate([x1 * cos - x2 * sin, x2 * cos + x1 * sin], axis=-1).astype(x.dtype)
    return jnp.concatenate([rot, x[..., ROT_DIM:]], axis=-1)


def compress_blocks(kv, pe, w1, w2):
    B, S = kv.shape[:2]
    n_cmp = (S - CMP_LEN) // CMP_STRIDE + 1
    idx = jnp.arange(n_cmp)[:, None] * CMP_STRIDE + jnp.arange(CMP_LEN)[None, :]
    blk = kv[:, idx] + pe[:, None, :]
    blk = jnp.moveaxis(blk, 3, 2).reshape(B, n_cmp, N_KV, CMP_LEN * HEAD_DIM)
    return jax.nn.gelu(blk @ w1) @ w2


def compressed_attention(q, kc, vc):
    S = q.shape[1]
    n = kc.shape[1]
    s = jnp.einsum('btgrd,bngd->btgrn', q, kc).astype(jnp.float32) * HEAD_DIM ** -0.5
    blk_end = jnp.arange(n) * CMP_STRIDE + CMP_LEN - 1
    valid = (blk_end[None, :] <= jnp.arange(S)[:, None])[None, :, None, None, :]
    p = jax.nn.softmax(jnp.where(valid, s, NEG_INF), axis=-1)
    p = jnp.where(valid, p, 0.0)
    o = jnp.einsum('btgrn,bngd->btgrd', p.astype(vc.dtype), vc)
    return o, p


def select_blocks(p_cmp, S):
    n_cmp = p_cmp.shape[-1]
    n_slc = S // SEL_LEN
    c0 = jnp.arange(n_cmp) * CMP_STRIDE
    s0 = jnp.arange(n_slc) * SEL_LEN
    ov = jnp.minimum(c0[:, None] + CMP_LEN, s0[None, :] + SEL_LEN) - jnp.maximum(c0[:, None], s0[None, :])
    overlap = jnp.clip(ov, 0).astype(jnp.float32) / CMP_LEN
    imp = jnp.einsum('btgrn,nj->btgj', p_cmp, overlap)
    t = jnp.arange(S)[:, None]
    blk = jnp.arange(n_slc)[None, :]
    forced = ((blk == t // SEL_LEN) | (blk == 0))[None, :, None, :]
    causal = (blk * SEL_LEN <= t)[None, :, None, :]
    imp = jnp.where(causal, jnp.where(forced, FORCE_SCORE, imp), NEG_INF)
    _, idx = lax.top_k(imp, min(N_SEL, n_slc))
    return idx


def selected_attention(q, ks, vs, idx):
    B, S, G, R, hd = q.shape
    n_slc = S // SEL_LEN
    k_top = idx.shape[-1]
    kb = ks.reshape(B, n_slc, SEL_LEN, G, hd).transpose(0, 3, 1, 2, 4)
    vb = vs.reshape(B, n_slc, SEL_LEN, G, hd).transpose(0, 3, 1, 2, 4)
    nq = S // SEL_Q_BLOCK
    qc = q.reshape(B, nq, SEL_Q_BLOCK, G, R, hd).transpose(1, 0, 2, 3, 4, 5)
    ic = idx.reshape(B, nq, SEL_Q_BLOCK, G, k_top).transpose(1, 0, 2, 3, 4)
    b_ix = jnp.arange(B)[:, None, None, None]
    g_ix = jnp.arange(G)[None, None, :, None]

    def chunk(args):
        qi, ii, c = args
        kg = kb[b_ix, g_ix, ii]
        vg = vb[b_ix, g_ix, ii]
        s = jnp.einsum('btgrd,btgkld->btgrkl', qi, kg).astype(jnp.float32) * HEAD_DIM ** -0.5
        t = c * SEL_Q_BLOCK + jnp.arange(SEL_Q_BLOCK)
        kpos = ii[..., None] * SEL_LEN + jnp.arange(SEL_LEN)
        mask = (kpos <= t[None, :, None, None, None])[:, :, :, None]
        s = jnp.where(mask, s, NEG_INF).reshape(B, SEL_Q_BLOCK, G, R, k_top * SEL_LEN)
        p = jax.nn.softmax(s, axis=-1).reshape(B, SEL_Q_BLOCK, G, R, k_top, SEL_LEN)
        return jnp.einsum('btgrkl,btgkld->btgrd', p.astype(vg.dtype), vg)

    o = lax.map(chunk, (qc, ic, jnp.arange(nq)))
    return o.transpose(1, 0, 2, 3, 4, 5).reshape(B, S, G, R, hd)


def window_attention(q, kw, vw):
    B, S, G, R, hd = q.shape
    nq = S // Q_BLOCK
    band = Q_BLOCK + WINDOW
    kp = jnp.pad(kw, ((0, 0), (WINDOW, 0), (0, 0), (0, 0)))
    vp = jnp.pad(vw, ((0, 0), (WINDOW, 0), (0, 0), (0, 0)))
    qc = q.reshape(B, nq, Q_BLOCK, G, R, hd).transpose(1, 0, 2, 3, 4, 5)

    def blk(args):
        qi, c = args
        start = c * Q_BLOCK
        kk = lax.dynamic_slice_in_dim(kp, start, band, axis=1)
        vv = lax.dynamic_slice_in_dim(vp, start, band, axis=1)
        s = jnp.einsum('btgrd,bsgd->btgrs', qi, kk).astype(jnp.float32) * HEAD_DIM ** -0.5
        t = start + jnp.arange(Q_BLOCK)
        kpos = start - WINDOW + jnp.arange(band)
        mask = (kpos[None, :] <= t[:, None]) & (kpos[None, :] > t[:, None] - WINDOW) & (kpos[None, :] >= 0)
        p = jax.nn.softmax(jnp.where(mask[None, :, None, None, :], s, NEG_INF), axis=-1)
        return jnp.einsum('btgrs,bsgd->btgrd', p.astype(vv.dtype), vv)

    o = lax.map(blk, (qc, jnp.arange(nq)))
    return o.transpose(1, 0, 2, 3, 4, 5).reshape(B, S, G, R, hd)


def pool_mixer(u, w_grp, scale):
    S = u.shape[1]
    uf = u.astype(jnp.float32)
    c = jnp.pad(jnp.cumsum(uf, axis=1), ((0, 0), (1, 0), (0, 0)))
    t1 = jnp.arange(1, S + 1)
    outs = []
    for gi, w in enumerate(POOL_WINDOWS):
        sl = slice(gi * POOL_GROUP, (gi + 1) * POOL_GROUP)
        cg = c[..., sl]
        lo = jnp.maximum(t1 - w, 0)
        cnt = jnp.minimum(t1, w).astype(jnp.float32)
        outs.append((cg[:, 1:] - cg[:, lo]) / cnt[None, :, None] - uf[..., sl])
    pooled = jnp.stack(outs, axis=2).astype(u.dtype)
    y = jnp.einsum('bsgc,gcd->bsgd', pooled, w_grp).reshape(u.shape)
    return y * scale


def token_mixer(h, positions, w_in, cmp_pe_k, cmp_w1_k, cmp_w2_k, cmp_pe_v, cmp_w1_v, cmp_w2_v,
                w_attn_branch, pool_w, pool_scale, w_pool_branch, w_out):
    B, S, _ = h.shape
    offs = [int(v) for v in np.cumsum(IN_WIDTHS)[:-1]]
    q, kc, vc, ks, vs, kw, vw, nsa_g, pool_in, merge_g = jnp.split(h @ w_in, offs, axis=-1)
    q = partial_rope(q.reshape(B, S, N_KV, GROUP, HEAD_DIM), positions)
    kv_shape = (B, S, N_KV, HEAD_DIM)
    ks = partial_rope(ks.reshape(kv_shape), positions)
    kw = partial_rope(kw.reshape(kv_shape), positions)
    n_cmp = (S - CMP_LEN) // CMP_STRIDE + 1
    pos_cmp = positions[:, jnp.arange(n_cmp) * CMP_STRIDE + CMP_LEN - 1]
    kcmp = partial_rope(compress_blocks(kc.reshape(kv_shape), cmp_pe_k, cmp_w1_k, cmp_w2_k), pos_cmp)
    vcmp = compress_blocks(vc.reshape(kv_shape), cmp_pe_v, cmp_w1_v, cmp_w2_v)
    o_cmp, p_cmp = compressed_attention(q, kcmp, vcmp)
    idx = select_blocks(p_cmp, S)
    o_sel = selected_attention(q, ks, vs.reshape(kv_shape), idx)
    o_win = window_attention(q, kw, vw.reshape(kv_shape))
    g = jax.nn.sigmoid(nsa_g.reshape(B, S, 3, N_KV, GROUP, 1))
    o_nsa = (g[:, :, 0] * o_cmp + g[:, :, 1] * o_sel + g[:, :, 2] * o_win).reshape(B, S, Q_WIDTH)
    o_pool = pool_mixer(pool_in, pool_w, pool_scale)
    g_attn, g_pool = jnp.split(jax.nn.sigmoid(merge_g), 2, axis=-1)
    y = g_attn * (o_nsa @ w_attn_branch) + g_pool * (o_pool @ w_pool_branch)
    return y @ w_out


def setup_inputs(seed: int = 0) -> dict:
    key = jax.random.key(seed)
    k = jax.random.split(key, 32)
    f32 = jnp.float32

    def w(kk, shape, fan_in):
        return jax.random.normal(kk, (DEPTH,) + shape, f32) * fan_in ** -0.5

    def gain(kk, n):
        return 1.0 + 0.05 * jax.random.normal(kk, (DEPTH, n), f32)

    x = jax.random.normal(k[0], (BATCH, SEQ, D_MODEL), f32)
    offsets = jax.random.randint(k[1], (BATCH, 1), 0, 4096, dtype=jnp.int32)
    positions = offsets + jnp.arange(SEQ, dtype=jnp.int32)[None, :]
    cmp_in = CMP_LEN * HEAD_DIM
    return {
        'x': x,
        'positions': positions,
        'g_ffn1_pre': gain(k[2], D_MODEL),
        'w_ffn1_gate': w(k[3], (D_MODEL, D_FF), D_MODEL),
        'w_ffn1_up': w(k[4], (D_MODEL, D_FF), D_MODEL),
        'w_ffn1_down': w(k[5], (D_FF, D_MODEL), D_FF),
        'g_ffn1_post': gain(k[6], D_MODEL),
        'g_mix_pre': gain(k[7], D_MODEL),
        'w_in': w(k[8], (D_MODEL, IN_TOTAL), D_MODEL),
        'cmp_pe_k': 0.5 * jax.random.normal(k[9], (DEPTH, CMP_LEN, HEAD_DIM), f32),
        'cmp_w1_k': w(k[10], (cmp_in, CMP_HIDDEN), cmp_in),
        'cmp_w2_k': w(k[11], (CMP_HIDDEN, HEAD_DIM), CMP_HIDDEN),
        'cmp_pe_v': 0.5 * jax.random.normal(k[12], (DEPTH, CMP_LEN, HEAD_DIM), f32),
        'cmp_w1_v': w(k[13], (cmp_in, CMP_HIDDEN), cmp_in),
        'cmp_w2_v': w(k[14], (CMP_HIDDEN, HEAD_DIM), CMP_HIDDEN),
        'w_attn_branch': w(k[15], (Q_WIDTH, D_MODEL), Q_WIDTH),
        'pool_w': w(k[16], (len(POOL_WINDOWS), POOL_GROUP, POOL_GROUP), POOL_GROUP),
        'pool_scale': 1.0 + 0.1 * jax.random.normal(k[17], (DEPTH, POOL_WIDTH), f32),
        'w_pool_branch': w(k[18], (POOL_WIDTH, D_MODEL), POOL_WIDTH),
        'w_out': w(k[19], (D_MODEL, D_MODEL), D_MODEL),
        'g_mix_post': gain(k[20], D_MODEL),
        'g_ffn2_pre': gain(k[21], D_MODEL),
        'w_ffn2_gate': w(k[22], (D_MODEL, D_FF), D_MODEL),
        'w_ffn2_up': w(k[23], (D_MODEL, D_FF), D_MODEL),
        'w_ffn2_down': w(k[24], (D_FF, D_MODEL), D_FF),
        'g_ffn2_post': gain(k[25], D_MODEL),
    }


def reference(x, positions, g_ffn1_pre, w_ffn1_gate, w_ffn1_up, w_ffn1_down, g_ffn1_post,
              g_mix_pre, w_in, cmp_pe_k, cmp_w1_k, cmp_w2_k, cmp_pe_v, cmp_w1_v, cmp_w2_v,
              w_attn_branch, pool_w, pool_scale, w_pool_branch, w_out, g_mix_post,
              g_ffn2_pre, w_ffn2_gate, w_ffn2_up, w_ffn2_down, g_ffn2_post):
    for l in range(DEPTH):
        h = swiglu(rms_norm(x, g_ffn1_pre[l]), w_ffn1_gate[l], w_ffn1_up[l], w_ffn1_down[l])
        x = x + 0.5 * rms_norm(h, g_ffn1_post[l])
        h = token_mixer(rms_norm(x, g_mix_pre[l]), positions, w_in[l],
                        cmp_pe_k[l], cmp_w1_k[l], cmp_w2_k[l], cmp_pe_v[l], cmp_w1_v[l], cmp_w2_v[l],
                        w_attn_branch[l], pool_w[l], pool_scale[l], w_pool_branch[l], w_out[l])
        x = x + rms_norm(h, g_mix_post[l])
        h = swiglu(rms_norm(x, g_ffn2_pre[l]), w_ffn2_gate[l], w_ffn2_up[l], w_ffn2_down[l])
        x = x + 0.5 * rms_norm(h, g_ffn2_post[l])
    return x
```

```python
import functools

import jax
import jax.numpy as jnp
from jax import lax
from jax.experimental import pallas as pl
from jax.experimental.pallas import tpu as pltpu

F32 = jnp.float32
BF16 = jnp.bfloat16

D_MODEL = 1024
N_HEADS = 16
HEAD_DIM = 64
N_KV = 4
GROUP = N_HEADS // N_KV
ROT_DIM = HEAD_DIM // 4
ROT_HALF = ROT_DIM // 2
ROPE_THETA = 500000.0
CMP_LEN = 32
CMP_STRIDE = 16
CMP_HIDDEN = 2 * HEAD_DIM
SEL_LEN = 64
SEL_SHIFT = SEL_LEN.bit_length() - 1
N_SEL = 8
WINDOW = 512
POOL_WINDOWS = (2, 4, 8, 16)
POOL_WIDTH = D_MODEL // 2
POOL_GROUP = POOL_WIDTH // len(POOL_WINDOWS)
D_FF = 2816
EPS = 1e-6
NEG_INF = -1e30
FORCE_SCORE = 1e4
Q_WIDTH = N_HEADS * HEAD_DIM
KV_WIDTH = N_KV * HEAD_DIM
N_GATES = 3 * N_HEADS

LANES = 128
V7X_VMEM_BYTES = 64 * 1024 * 1024

SLOT = LANES
ONE_LANE = HEAD_DIM
BLK_LANE0 = HEAD_DIM
TOK_TILE = 512
TQ = 128
KT = 128
FF_CHUNK = 256
N_ROPE_TAB = 3


def _vmem_limit(block_bytes, scratch_bytes=0):
    need = 2 * block_bytes + scratch_bytes
    return int(min(V7X_VMEM_BYTES - (4 << 20), max(2 * need, 32 << 20)))


def _nbytes(shape, dtype):
    n = 1
    for s in shape:
        n *= s
    return n * jnp.dtype(dtype).itemsize


def _rms(xf, g):
    return xf * lax.rsqrt(jnp.mean(xf * xf, axis=-1, keepdims=True) + EPS) * g


def _dot(a, b):
    return jnp.dot(a, b, preferred_element_type=F32)


def _dot_nt(a, b, precision=None):
    return lax.dot_general(a, b, (((1,), (1,)), ((), ())), precision=precision,
                           preferred_element_type=F32)


def _rope_trig_kernel(pos_ref, inv_ref, cos_ref, sin_ref):
    ang = pos_ref[...].astype(F32) * inv_ref[...]
    cos_ref[...] = jnp.cos(ang)
    sin_ref[...] = jnp.sin(ang)


def _rope_tables(pos_flat):
    n = pos_flat.shape[0]
    inv = ROPE_THETA ** (-jnp.arange(ROT_HALF, dtype=F32) * (2.0 / ROT_DIM))
    cos8, sin8 = pl.pallas_call(
        _rope_trig_kernel,
        out_shape=(jax.ShapeDtypeStruct((ROT_HALF, n), F32),) * 2,
        name="rope_trig",
    )(pos_flat.reshape(1, n), inv.reshape(ROT_HALF, 1))
    cos8, sin8 = cos8.T, sin8.T
    ones = jnp.ones((n, SLOT - ROT_DIM), F32)
    zeros = jnp.zeros((n, SLOT - ROT_HALF), F32)
    cos_t = jnp.concatenate([cos8, cos8, ones], axis=1)
    sin_a = jnp.concatenate([-sin8, zeros], axis=1)
    sin_b = jnp.concatenate([jnp.zeros((n, ROT_HALF), F32), sin8, zeros[:, ROT_HALF:]], axis=1)
    return jnp.concatenate([cos_t, sin_a, sin_b], axis=1)


def _rope_slot(y, tab):
    cos_t = tab[:, 0:SLOT]
    sin_a = tab[:, SLOT:2 * SLOT]
    sin_b = tab[:, 2 * SLOT:3 * SLOT]
    up = pltpu.roll(y, SLOT - ROT_HALF, axis=1)
    down = pltpu.roll(y, ROT_HALF, axis=1)
    return y * cos_t + up * sin_a + down * sin_b


def _ffn_kernel(x_ref, gpre_ref, wg_ref, wu_ref, wd_ref, gpost_ref, o_ref, acc_ref):
    x = x_ref[...]
    hn = _rms(x, gpre_ref[...]).astype(BF16)
    for c in range(D_FF // FF_CHUNK):
        sl = slice(c * FF_CHUNK, (c + 1) * FF_CHUNK)
        g = _dot(hn, wg_ref[:, sl])
        u = _dot(hn, wu_ref[:, sl])
        a = (g * jax.nn.sigmoid(g) * u).astype(BF16)
        d = _dot(a, wd_ref[sl, :])
        if c == 0:
            acc_ref[...] = d
        else:
            acc_ref[...] += d
    o_ref[...] = x + 0.5 * _rms(acc_ref[...], gpost_ref[...])


def _ffn(x, g_pre, w_gate, w_up, w_down, g_post):
    t = x.shape[0]
    row = lambda i: (i, 0)
    fixed = lambda i: (0, 0)
    blocks = (2 * _nbytes((TOK_TILE, D_MODEL), F32) + 3 * _nbytes((D_MODEL, D_FF), BF16))
    return pl.pallas_call(
        _ffn_kernel,
        grid=(t // TOK_TILE,),
        in_specs=[
            pl.BlockSpec((TOK_TILE, D_MODEL), row),
            pl.BlockSpec((1, D_MODEL), fixed),
            pl.BlockSpec((D_MODEL, D_FF), fixed),
            pl.BlockSpec((D_MODEL, D_FF), fixed),
            pl.BlockSpec((D_FF, D_MODEL), fixed),
            pl.BlockSpec((1, D_MODEL), fixed),
        ],
        out_specs=pl.BlockSpec((TOK_TILE, D_MODEL), row),
        out_shape=jax.ShapeDtypeStruct((t, D_MODEL), F32),
        scratch_shapes=[pltpu.VMEM((TOK_TILE, D_MODEL), F32)],
        compiler_params=pltpu.CompilerParams(
            dimension_semantics=("arbitrary",),
            vmem_limit_bytes=_vmem_limit(blocks, _nbytes((TOK_TILE, D_MODEL), F32))),
        name="ffn",
    )(x, g_pre.reshape(1, -1), w_gate.astype(BF16), w_up.astype(BF16), w_down.astype(BF16),
      g_post.reshape(1, -1))


W_Q_PAD = N_HEADS * SLOT
W_KV_PAD = N_KV * SLOT
W_CMP = 2 * KV_WIDTH
OFF_Q = 0
OFF_KS = OFF_Q + W_Q_PAD
OFF_VS = OFF_KS + W_KV_PAD
OFF_KW = OFF_VS + W_KV_PAD
OFF_VW = OFF_KW + W_KV_PAD
OFF_CMP = OFF_VW + W_KV_PAD
OFF_GATE = OFF_CMP + W_CMP
OFF_POOL = OFF_GATE + SLOT
IN_PAD_TOTAL = OFF_POOL + POOL_WIDTH


def _in_proj_kernel(x_ref, g_ref, w_ref, tab_ref, q_ref, ks_ref, vs_ref, kw_ref, vw_ref,
                    cmp_ref, gate_ref, pool_ref, *, seq_len):
    hn = _rms(x_ref[...], g_ref[...]).astype(BF16)
    tab = tab_ref[...]
    rows = x_ref.shape[0]
    lane = lax.broadcasted_iota(jnp.int32, (rows, SLOT), 1)
    tile_tok0 = (pl.program_id(0) * rows) % seq_len
    tok = tile_tok0 + lax.broadcasted_iota(jnp.int32, (rows, SLOT), 0)
    blk_onehot = (lane - BLK_LANE0 == tok >> SEL_SHIFT).astype(F32)
    one_lane = (lane == ONE_LANE).astype(F32)

    def slot(off, j):
        return _dot(hn, w_ref[:, off + j * SLOT:off + (j + 1) * SLOT])

    for h in range(N_HEADS):
        q_ref[:, h * SLOT:(h + 1) * SLOT] = _rope_slot(slot(OFF_Q, h), tab).astype(BF16)
    for g in range(N_KV):
        sl = slice(g * SLOT, (g + 1) * SLOT)
        ks_ref[:, sl] = (_rope_slot(slot(OFF_KS, g), tab) + blk_onehot).astype(BF16)
        vs_ref[:, sl] = (slot(OFF_VS, g) + one_lane).astype(BF16)
        kw_ref[:, sl] = _rope_slot(slot(OFF_KW, g), tab).astype(BF16)
        vw_ref[:, sl] = (slot(OFF_VW, g) + one_lane).astype(BF16)
    for j in range(W_CMP // SLOT):
        cmp_ref[:, j * SLOT:(j + 1) * SLOT] = slot(OFF_CMP, j).astype(BF16)
    gate_ref[...] = slot(OFF_GATE, 0)
    for j in range(POOL_WIDTH // SLOT):
        pool_ref[:, j * SLOT:(j + 1) * SLOT] = slot(OFF_POOL, j)


def _pad_slots(w, n_slots):
    k = w.shape[0]
    w = w.reshape(k, n_slots, HEAD_DIM)
    w = jnp.pad(w, ((0, 0), (0, 0), (0, SLOT - HEAD_DIM)))
    return w.reshape(k, n_slots * SLOT)


def _in_proj(x1, g_pre, w_in, tab, seq_len):
    t = x1.shape[0]
    o = 0
    parts = {}
    for name, width in (("q", Q_WIDTH), ("kc", KV_WIDTH), ("vc", KV_WIDTH), ("ks", KV_WIDTH),
                        ("vs", KV_WIDTH), ("kw", KV_WIDTH), ("vw", KV_WIDTH), ("gate", N_GATES),
                        ("pool", POOL_WIDTH), ("merge", 2 * D_MODEL)):
        parts[name] = w_in[:, o:o + width]
        o += width
    w_all = jnp.concatenate([
        _pad_slots(parts["q"] * HEAD_DIM ** -0.5, N_HEADS),
        _pad_slots(parts["ks"], N_KV), _pad_slots(parts["vs"], N_KV),
        _pad_slots(parts["kw"], N_KV), _pad_slots(parts["vw"], N_KV),
        parts["kc"], parts["vc"],
        jnp.pad(parts["gate"], ((0, 0), (0, SLOT - N_GATES))),
        parts["pool"],
    ], axis=1).astype(BF16)
    assert w_all.shape[1] == IN_PAD_TOTAL
    row = lambda i: (i, 0)
    fixed = lambda i: (0, 0)
    outs = (
        (W_Q_PAD, BF16), (W_KV_PAD, BF16), (W_KV_PAD, BF16), (W_KV_PAD, BF16), (W_KV_PAD, BF16),
        (W_CMP, BF16), (SLOT, F32), (POOL_WIDTH, F32))
    blocks = (_nbytes((TOK_TILE, D_MODEL), F32) + _nbytes((D_MODEL, IN_PAD_TOTAL), BF16)
              + _nbytes((TOK_TILE, N_ROPE_TAB * SLOT), F32)
              + sum(_nbytes((TOK_TILE, w), d) for w, d in outs))
    res = pl.pallas_call(
        functools.partial(_in_proj_kernel, seq_len=seq_len),
        grid=(t // TOK_TILE,),
        in_specs=[
            pl.BlockSpec((TOK_TILE, D_MODEL), row),
            pl.BlockSpec((1, D_MODEL), fixed),
            pl.BlockSpec((D_MODEL, IN_PAD_TOTAL), fixed),
            pl.BlockSpec((TOK_TILE, N_ROPE_TAB * SLOT), row),
        ],
        out_specs=[pl.BlockSpec((TOK_TILE, w), row) for w, _ in outs],
        out_shape=[jax.ShapeDtypeStruct((t, w), d) for w, d in outs],
        compiler_params=pltpu.CompilerParams(
            dimension_semantics=("arbitrary",), vmem_limit_bytes=_vmem_limit(blocks)),
        name="in_proj",
    )(x1, g_pre.reshape(1, -1), w_all, tab)
    return res, parts["merge"]


CHUNK_TOK = CMP_STRIDE
HID_PAD = N_KV * CMP_HIDDEN


def _compress_kernel(x_ref, w1k_ref, w1v_ref, pek_ref, pev_ref, w1k_raw_ref, w1v_raw_ref,
                     w2k_ref, w2v_ref, tab_ref, kc_ref, vc_ref, acck_ref, accv_ref):
    l = pl.program_id(0)
    xk = x_ref[:, 0:KV_WIDTH]
    xv = x_ref[:, KV_WIDTH:2 * KV_WIDTH]
    dk = _dot(xk, w1k_ref[0])
    dv = _dot(xv, w1v_ref[0])

    @pl.when(l == 0)
    def _():
        acck_ref[...] = dk
        accv_ref[...] = dv

    @pl.when(l > 0)
    def _():
        acck_ref[...] += dk
        accv_ref[...] += dv

    @pl.when(l == CHUNK_TOK - 1)
    def _():
        rows = acck_ref.shape[0]

        def finish(acc_ref, pe_ref, w1_raw_ref, w2_ref):
            bias = _dot(pe_ref[...], w1_raw_ref[...])[0:1, :]
            bias = jnp.concatenate([bias] * N_KV, axis=1)
            first = acc_ref[:, 0:HID_PAD]
            second = pltpu.roll(acc_ref[:, HID_PAD:2 * HID_PAD], rows - 1, axis=0)
            hid = jax.nn.gelu(first + second + bias).astype(BF16)
            return _dot(hid, w2_ref[...])

        kc = finish(acck_ref, pek_ref, w1k_raw_ref, w2k_ref)
        vc = finish(accv_ref, pev_ref, w1v_raw_ref, w2v_ref)
        tab = tab_ref[...]
        for g in range(N_KV):
            sl = slice(g * SLOT, (g + 1) * SLOT)
            kc_ref[:, sl] = _rope_slot(kc[:, sl], tab).astype(BF16)
            vc_ref[:, sl] = vc[:, sl].astype(BF16)


def _block_diag(w, n):
    eye = jnp.eye(n, dtype=w.dtype)
    out = jnp.einsum("ij,...ab->...iajb", eye, w)
    return out.reshape(w.shape[:-2] + (n * w.shape[-2], n * w.shape[-1]))


def _compress_weights(w1, w2):
    w1 = w1.reshape(2, CHUNK_TOK, HEAD_DIM, CMP_HIDDEN)
    bd = _block_diag(w1, N_KV)
    w1_tok = jnp.concatenate([bd[0], bd[1]], axis=-1).astype(BF16)
    w2_pad = jnp.pad(w2, ((0, 0), (0, SLOT - HEAD_DIM)))
    return w1_tok, _block_diag(w2_pad, N_KV).astype(BF16)


def _compress(cmp_in, cmp_pe_k, cmp_w1_k, cmp_w2_k, cmp_pe_v, cmp_w1_v, cmp_w2_v, tab_cmp):
    t = cmp_in.shape[0]
    rows = t // CHUNK_TOK
    x = cmp_in.reshape(rows, CHUNK_TOK * W_CMP)
    w1k, w2k = _compress_weights(cmp_w1_k, cmp_w2_k)
    w1v, w2v = _compress_weights(cmp_w1_v, cmp_w2_v)
    sub = 8
    pek = jnp.broadcast_to(cmp_pe_k.reshape(1, -1), (sub, CMP_LEN * HEAD_DIM)).astype(BF16)
    pev = jnp.broadcast_to(cmp_pe_v.reshape(1, -1), (sub, CMP_LEN * HEAD_DIM)).astype(BF16)
    fixed = lambda l: (0, 0)
    blocks = (_nbytes((rows, W_CMP), BF16) + 2 * _nbytes((KV_WIDTH, 2 * HID_PAD), BF16)
              + 2 * _nbytes((CMP_LEN * HEAD_DIM, CMP_HIDDEN), BF16)
              + 2 * _nbytes((HID_PAD, W_KV_PAD), BF16)
              + _nbytes((rows, N_ROPE_TAB * SLOT), F32) + 2 * _nbytes((rows, W_KV_PAD), BF16))
    scratch = 2 * _nbytes((rows, 2 * HID_PAD), F32)
    return pl.pallas_call(
        _compress_kernel,
        grid=(CHUNK_TOK,),
        in_specs=[
            pl.BlockSpec((rows, W_CMP), lambda l: (0, l)),
            pl.BlockSpec((1, KV_WIDTH, 2 * HID_PAD), lambda l: (l, 0, 0)),
            pl.BlockSpec((1, KV_WIDTH, 2 * HID_PAD), lambda l: (l, 0, 0)),
            pl.BlockSpec((sub, CMP_LEN * HEAD_DIM), fixed),
            pl.BlockSpec((sub, CMP_LEN * HEAD_DIM), fixed),
            pl.BlockSpec((CMP_LEN * HEAD_DIM, CMP_HIDDEN), fixed),
            pl.BlockSpec((CMP_LEN * HEAD_DIM, CMP_HIDDEN), fixed),
            pl.BlockSpec((HID_PAD, W_KV_PAD), fixed),
            pl.BlockSpec((HID_PAD, W_KV_PAD), fixed),
            pl.BlockSpec((rows, N_ROPE_TAB * SLOT), fixed),
        ],
        out_specs=[pl.BlockSpec((rows, W_KV_PAD), fixed)] * 2,
        out_shape=[jax.ShapeDtypeStruct((rows, W_KV_PAD), BF16)] * 2,
        scratch_shapes=[pltpu.VMEM((rows, 2 * HID_PAD), F32)] * 2,
        compiler_params=pltpu.CompilerParams(
            dimension_semantics=("arbitrary",), vmem_limit_bytes=_vmem_limit(blocks, scratch)),
        name="compress",
    )(x, w1k, w1v, pek, pev, cmp_w1_k.astype(BF16), cmp_w1_v.astype(BF16), w2k, w2v, tab_cmp)


N_SLC = None


def _nsa_kernel(q_ref, gate_ref, kc_ref, vc_ref, ks_ref, vs_ref, kw_ref, vw_ref, ovt_ref,
                o_ref, qg_ref, m_ref, acc_ref, *, n_slc):
    i = pl.program_id(1)
    start = i * TQ
    rows_g = GROUP * TQ
    t_col = start + lax.broadcasted_iota(jnp.int32, (TQ, 1), 0)
    gates = jax.nn.sigmoid(gate_ref[...])
    n_cmp_idx = lax.broadcasted_iota(jnp.int32, (TQ, kc_ref.shape[1]), 1)
    cmp_valid = n_cmp_idx * CMP_STRIDE + (CMP_LEN - 1) <= t_col
    t_row = start + lax.broadcasted_iota(jnp.int32, (n_slc, TQ), 1)
    blk = lax.broadcasted_iota(jnp.int32, (n_slc, TQ), 0)
    forced = (blk == t_row >> SEL_SHIFT) | (blk == 0)
    causal_blk = blk * SEL_LEN <= t_row
    r_in = lax.broadcasted_iota(jnp.int32, (rows_g, KT), 0) & (TQ - 1)
    c_in = lax.broadcasted_iota(jnp.int32, (rows_g, KT), 1)
    diag_mask = c_in <= r_in
    far_mask = c_in > r_in
    data_lane = lax.broadcasted_iota(jnp.int32, (TQ, SLOT), 1) < HEAD_DIM

    def flash_step(k_tile, v_tile, mask):
        s = _dot_nt(qg_ref[...], k_tile)
        if mask is not None:
            s = jnp.where(mask, s, NEG_INF)
        m_old = m_ref[...]
        m_new = jnp.maximum(m_old, jnp.max(s, axis=-1, keepdims=True))
        p = jnp.exp(s - m_new)
        acc_ref[...] = jnp.exp(m_old - m_new) * acc_ref[...] + _dot(p.astype(BF16), v_tile)
        m_ref[...] = m_new

    def flash_init():
        m_ref[...] = jnp.full(m_ref.shape, NEG_INF, F32)
        acc_ref[...] = jnp.zeros(acc_ref.shape, F32)

    def flash_result():
        acc = acc_ref[...]
        return acc / acc[:, ONE_LANE:ONE_LANE + 1]

    for g in range(N_KV):
        gs = slice(g * SLOT, (g + 1) * SLOT)
        kc_g = kc_ref[0, :, gs]
        vc_g = vc_ref[0, :, gs]
        p_sum = jnp.zeros((TQ, kc_ref.shape[1]), F32)
        o_cmp = []
        for r in range(GROUP):
            h = g * GROUP + r
            s = _dot_nt(q_ref[:, h * SLOT:(h + 1) * SLOT], kc_g)
            s = jnp.where(cmp_valid, s, NEG_INF)
            p = jnp.exp(s - jnp.max(s, axis=-1, keepdims=True))
            p = jnp.where(cmp_valid, p, 0.0)
            den = jnp.sum(p, axis=-1, keepdims=True)
            p = p / jnp.where(den > 0.0, den, 1.0)
            p_sum = p_sum + p
            o_cmp.append(_dot(p.astype(BF16), vc_g))
        imp_t = _dot_nt(ovt_ref[...], p_sum, precision=lax.Precision.HIGHEST)
        imp = imp_t[BLK_LANE0:BLK_LANE0 + n_slc, :]
        score = jnp.where(causal_blk, jnp.where(forced, FORCE_SCORE, imp), NEG_INF)
        rank = jnp.zeros((n_slc, TQ), jnp.int32)
        for c in range(n_slc):
            other = score[c:c + 1, :]
            beats = (other > score) | ((other == score) & (c < blk))
            rank = rank + beats.astype(jnp.int32)
        selected = causal_blk & (rank < N_SEL)
        bias = jnp.where(selected, 0.0, NEG_INF)
        bias_t = jnp.concatenate([jnp.zeros((BLK_LANE0, TQ), F32), bias,
                                  jnp.zeros((SLOT - BLK_LANE0 - n_slc, TQ), F32)], axis=0)
        bias_q = bias_t.T
        for r in range(GROUP):
            h = g * GROUP + r
            qh = q_ref[:, h * SLOT:(h + 1) * SLOT].astype(F32)
            qg_ref[r * TQ:(r + 1) * TQ, :] = (qh + bias_q).astype(BF16)
        flash_init()

        def sel_body(kt, carry):
            off = pl.multiple_of(kt * KT, KT)
            flash_step(ks_ref[0, pl.ds(off, KT), gs], vs_ref[0, pl.ds(off, KT), gs], None)
            return carry

        lax.fori_loop(0, i, sel_body, 0)
        off_d = pl.multiple_of(start, KT)
        flash_step(ks_ref[0, pl.ds(off_d, KT), gs], vs_ref[0, pl.ds(off_d, KT), gs], diag_mask)
        o_sel = flash_result()
        flash_init()
        n_back = WINDOW // KT
        for d in range(n_back, 0, -1):
            mask = far_mask if d == n_back else None

            @pl.when(i >= d)
            def _(d=d, mask=mask):
                off = pl.multiple_of((i - d) * KT, KT)
                flash_step(kw_ref[0, pl.ds(off, KT), gs], vw_ref[0, pl.ds(off, KT), gs], mask)

        flash_step(kw_ref[0, pl.ds(off_d, KT), gs], vw_ref[0, pl.ds(off_d, KT), gs], diag_mask)
        o_win = flash_result()
        for r2 in range(GROUP // 2):
            pair = []
            for r in (2 * r2, 2 * r2 + 1):
                h = g * GROUP + r
                rs = slice(r * TQ, (r + 1) * TQ)
                o_h = (gates[:, h:h + 1] * o_cmp[r]
                       + gates[:, N_HEADS + h:N_HEADS + h + 1] * o_sel[rs]
                       + gates[:, 2 * N_HEADS + h:2 * N_HEADS + h + 1] * o_win[rs])
                pair.append(jnp.where(data_lane, o_h, 0.0))
            both = pair[0] + pltpu.roll(pair[1], HEAD_DIM, axis=1)
            slot_idx = (g * GROUP) // 2 + r2
            o_ref[:, slot_idx * SLOT:(slot_idx + 1) * SLOT] = both.astype(BF16)


def _overlap_t(n_cmp_pad, n_slc):
    c0 = jnp.arange(n_cmp_pad) * CMP_STRIDE
    s0 = jnp.arange(n_slc) * SEL_LEN
    ov = jnp.minimum(c0[None, :] + CMP_LEN, s0[:, None] + SEL_LEN) - jnp.maximum(c0[None, :], s0[:, None])
    ov = jnp.clip(ov, 0).astype(F32) / CMP_LEN
    return jnp.pad(ov, ((BLK_LANE0, SLOT - BLK_LANE0 - n_slc), (0, 0)))


def _nsa(q, gate, kc, vc, ks, vs, kw, vw, batch, seq_len):
    n_slc = seq_len // SEL_LEN
    n_cmp_pad = kc.shape[1]
    assert n_slc <= SLOT - BLK_LANE0 and n_slc % 8 == 0
    ovt = _overlap_t(n_cmp_pad, n_slc)
    nq = seq_len // TQ
    qrow = lambda b, i: (b * nq + i, 0)
    per_b = lambda b, i: (b, 0, 0)
    fixed = lambda b, i: (0, 0)
    kv3 = lambda a: a.reshape(batch, seq_len, W_KV_PAD)
    blocks = (_nbytes((TQ, W_Q_PAD), BF16) + _nbytes((TQ, SLOT), F32)
              + 2 * _nbytes((n_cmp_pad, W_KV_PAD), BF16) + 4 * _nbytes((seq_len, W_KV_PAD), BF16)
              + _nbytes((SLOT, n_cmp_pad), F32) + _nbytes((TQ, Q_WIDTH), BF16))
    scratch = (_nbytes((GROUP * TQ, SLOT), BF16) + 2 * _nbytes((GROUP * TQ, SLOT), F32))
    return pl.pallas_call(
        functools.partial(_nsa_kernel, n_slc=n_slc),
        grid=(batch, nq),
        in_specs=[
            pl.BlockSpec((TQ, W_Q_PAD), qrow),
            pl.BlockSpec((TQ, SLOT), qrow),
            pl.BlockSpec((1, n_cmp_pad, W_KV_PAD), per_b),
            pl.BlockSpec((1, n_cmp_pad, W_KV_PAD), per_b),
            pl.BlockSpec((1, seq_len, W_KV_PAD), per_b),
            pl.BlockSpec((1, seq_len, W_KV_PAD), per_b),
            pl.BlockSpec((1, seq_len, W_KV_PAD), per_b),
            pl.BlockSpec((1, seq_len, W_KV_PAD), per_b),
            pl.BlockSpec((SLOT, n_cmp_pad), fixed),
        ],
        out_specs=pl.BlockSpec((TQ, Q_WIDTH), qrow),
        out_shape=jax.ShapeDtypeStruct((batch * seq_len, Q_WIDTH), BF16),
        scratch_shapes=[
            pltpu.VMEM((GROUP * TQ, SLOT), BF16),
            pltpu.VMEM((GROUP * TQ, 1), F32),
            pltpu.VMEM((GROUP * TQ, SLOT), F32),
        ],
        compiler_params=pltpu.CompilerParams(
            dimension_semantics=("arbitrary", "arbitrary"),
            vmem_limit_bytes=_vmem_limit(blocks, scratch)),
        name="nsa",
    )(q, gate, kc.reshape(batch, n_cmp_pad, W_KV_PAD), vc.reshape(batch, n_cmp_pad, W_KV_PAD),
      kv3(ks), kv3(vs), kv3(kw), kv3(vw), ovt)


def _pool_kernel(u_ref, w_ref, scale_ref, o_ref):
    seq = u_ref.shape[0]
    t = lax.broadcasted_iota(jnp.int32, (seq, POOL_GROUP), 0)

    def shifted(x, k):
        return jnp.where(t >= k, pltpu.roll(x, k, axis=0), 0.0)

    for gi, w in enumerate(POOL_WINDOWS):
        sl = slice(gi * POOL_GROUP, (gi + 1) * POOL_GROUP)
        x = u_ref[:, sl]
        wsum = x
        span = 1
        while span < w:
            wsum = wsum + shifted(wsum, span)
            span *= 2
        cnt = jnp.minimum(t + 1, w).astype(F32)
        pooled = (wsum / cnt - x).astype(BF16)
        o_ref[:, sl] = (_dot(pooled, w_ref[gi]) * scale_ref[:, sl]).astype(BF16)


def _pool(pool_in, pool_w, pool_scale, batch, seq_len):
    for w in POOL_WINDOWS:
        assert w & (w - 1) == 0
    blocks = (_nbytes((seq_len, POOL_WIDTH), F32) + _nbytes(pool_w.shape, BF16)
              + _nbytes((seq_len, POOL_WIDTH), BF16))
    return pl.pallas_call(
        _pool_kernel,
        grid=(batch,),
        in_specs=[
            pl.BlockSpec((seq_len, POOL_WIDTH), lambda b: (b, 0)),
            pl.BlockSpec(pool_w.shape, lambda b: (0, 0, 0)),
            pl.BlockSpec((1, POOL_WIDTH), lambda b: (0, 0)),
        ],
        out_specs=pl.BlockSpec((seq_len, POOL_WIDTH), lambda b: (b, 0)),
        out_shape=jax.ShapeDtypeStruct((batch * seq_len, POOL_WIDTH), BF16),
        compiler_params=pltpu.CompilerParams(
            dimension_semantics=("arbitrary",), vmem_limit_bytes=_vmem_limit(blocks)),
        name="pool",
    )(pool_in, pool_w.astype(BF16), pool_scale.reshape(1, -1))


def _merge_kernel(x_ref, gpre_ref, wm_ref, nsa_ref, wa_ref, pool_ref, wp_ref, wo_ref, gpost_ref,
                  o_ref):
    x = x_ref[...]
    hn = _rms(x, gpre_ref[...]).astype(BF16)
    g_attn = jax.nn.sigmoid(_dot(hn, wm_ref[:, 0:D_MODEL]))
    g_pool = jax.nn.sigmoid(_dot(hn, wm_ref[:, D_MODEL:2 * D_MODEL]))
    y = g_attn * _dot(nsa_ref[...], wa_ref[...]) + g_pool * _dot(pool_ref[...], wp_ref[...])
    h = _dot(y.astype(BF16), wo_ref[...])
    o_ref[...] = x + _rms(h, gpost_ref[...])


def _merge(x1, g_pre, w_merge, o_nsa, w_attn, o_pool, w_pool, w_out, g_post):
    t = x1.shape[0]
    row = lambda i: (i, 0)
    fixed = lambda i: (0, 0)
    blocks = (2 * _nbytes((TOK_TILE, D_MODEL), F32) + _nbytes((D_MODEL, 2 * D_MODEL), BF16)
              + _nbytes((TOK_TILE, Q_WIDTH), BF16) + _nbytes((Q_WIDTH, D_MODEL), BF16)
              + _nbytes((TOK_TILE, POOL_WIDTH), BF16) + _nbytes((POOL_WIDTH, D_MODEL), BF16)
              + _nbytes((D_MODEL, D_MODEL), BF16))
    return pl.pallas_call(
        _merge_kernel,
        grid=(t // TOK_TILE,),
        in_specs=[
            pl.BlockSpec((TOK_TILE, D_MODEL), row),
            pl.BlockSpec((1, D_MODEL), fixed),
            pl.BlockSpec((D_MODEL, 2 * D_MODEL), fixed),
            pl.BlockSpec((TOK_TILE, Q_WIDTH), row),
            pl.BlockSpec((Q_WIDTH, D_MODEL), fixed),
            pl.BlockSpec((TOK_TILE, POOL_WIDTH), row),
            pl.BlockSpec((POOL_WIDTH, D_MODEL), fixed),
            pl.BlockSpec((D_MODEL, D_MODEL), fixed),
            pl.BlockSpec((1, D_MODEL), fixed),
        ],
        out_specs=pl.BlockSpec((TOK_TILE, D_MODEL), row),
        out_shape=jax.ShapeDtypeStruct((t, D_MODEL), F32),
        compiler_params=pltpu.CompilerParams(
            dimension_semantics=("arbitrary",), vmem_limit_bytes=_vmem_limit(blocks)),
        name="merge",
    )(x1, g_pre.reshape(1, -1), w_merge.astype(BF16), o_nsa, w_attn.astype(BF16), o_pool,
      w_pool.astype(BF16), w_out.astype(BF16), g_post.reshape(1, -1))


def _layer(x, tab, tab_cmp, batch, seq_len, g_ffn1_pre, w_ffn1_gate, w_ffn1_up, w_ffn1_down,
           g_ffn1_post, g_mix_pre, w_in, cmp_pe_k, cmp_w1_k, cmp_w2_k, cmp_pe_v, cmp_w1_v, cmp_w2_v,
           w_attn_branch, pool_w, pool_scale, w_pool_branch, w_out, g_mix_post, g_ffn2_pre,
           w_ffn2_gate, w_ffn2_up, w_ffn2_down, g_ffn2_post):
    x1 = _ffn(x, g_ffn1_pre, w_ffn1_gate, w_ffn1_up, w_ffn1_down, g_ffn1_post)
    (q, ks, vs, kw, vw, cmp_in, gate, pool_in), w_merge = _in_proj(x1, g_mix_pre, w_in, tab, seq_len)
    kc, vc = _compress(cmp_in, cmp_pe_k, cmp_w1_k, cmp_w2_k, cmp_pe_v, cmp_w1_v, cmp_w2_v, tab_cmp)
    n_cmp_pad = seq_len // CMP_STRIDE
    kc = kc.reshape(batch, n_cmp_pad, W_KV_PAD)
    vc = vc.reshape(batch, n_cmp_pad, W_KV_PAD)
    o_nsa = _nsa(q, gate, kc, vc, ks, vs, kw, vw, batch, seq_len)
    o_pool = _pool(pool_in, pool_w, pool_scale, batch, seq_len)
    x2 = _merge(x1, g_mix_pre, w_merge, o_nsa, w_attn_branch, o_pool, w_pool_branch, w_out,
                g_mix_post)
    return _ffn(x2, g_ffn2_pre, w_ffn2_gate, w_ffn2_up, w_ffn2_down, g_ffn2_post)


def kernel(x, positions, g_ffn1_pre, w_ffn1_gate, w_ffn1_up, w_ffn1_down, g_ffn1_post, g_mix_pre, w_in, cmp_pe_k, cmp_w1_k, cmp_w2_k, cmp_pe_v, cmp_w1_v, cmp_w2_v, w_attn_branch, pool_w, pool_scale, w_pool_branch, w_out, g_mix_post, g_ffn2_pre, w_ffn2_gate, w_ffn2_up, w_ffn2_down, g_ffn2_post):
    batch, seq_len, d_model = x.shape
    assert d_model == D_MODEL and seq_len % TOK_TILE == 0 and seq_len % TQ == 0
    assert TQ == KT and TQ % SEL_LEN == 0 and seq_len % CMP_STRIDE == 0
    t = batch * seq_len
    tab = _rope_tables(positions.reshape(t))
    n_cmp_pad = seq_len // CMP_STRIDE
    last_tok = jnp.minimum(jnp.arange(n_cmp_pad) * CMP_STRIDE + CMP_LEN - 1, seq_len - 1)
    tab_cmp = tab.reshape(batch, seq_len, -1)[:, last_tok].reshape(batch * n_cmp_pad, -1)
    xf = x.reshape(t, D_MODEL)
    per_layer = (g_ffn1_pre, w_ffn1_gate, w_ffn1_up, w_ffn1_down, g_ffn1_post, g_mix_pre, w_in,
                 cmp_pe_k, cmp_w1_k, cmp_w2_k, cmp_pe_v, cmp_w1_v, cmp_w2_v, w_attn_branch, pool_w,
                 pool_scale, w_pool_branch, w_out, g_mix_post, g_ffn2_pre, w_ffn2_gate, w_ffn2_up,
                 w_ffn2_down, g_ffn2_post)
    for l in range(g_ffn1_pre.shape[0]):
        xf = _layer(xf, tab, tab_cmp, batch, seq_len, *(p[l] for p in per_layer))
    return xf.reshape(batch, seq_len, D_MODEL)
```

```python
import functools

import jax
import jax.numpy as jnp
from jax import lax
from jax.experimental import pallas as pl
from jax.experimental.pallas import tpu as pltpu

F32 = jnp.float32
BF16 = jnp.bfloat16

D_MODEL = 1024
N_HEADS = 16
HEAD_DIM = 64
N_KV = 4
GROUP = N_HEADS // N_KV
ROT_DIM = HEAD_DIM // 4
ROT_HALF = ROT_DIM // 2
ROPE_THETA = 500000.0
CMP_LEN = 32
CMP_STRIDE = 16
CMP_HIDDEN = 2 * HEAD_DIM
SEL_LEN = 64
SEL_SHIFT = SEL_LEN.bit_length() - 1
N_SEL = 8
WINDOW = 512
POOL_WINDOWS = (2, 4, 8, 16)
POOL_WIDTH = D_MODEL // 2
POOL_GROUP = POOL_WIDTH // len(POOL_WINDOWS)
D_FF = 2816
EPS = 1e-6
NEG_INF = -1e30
FORCE_SCORE = 1e4
Q_WIDTH = N_HEADS * HEAD_DIM
KV_WIDTH = N_KV * HEAD_DIM
N_GATES = 3 * N_HEADS

LANES = 128
V7X_VMEM_BYTES = 64 * 1024 * 1024

SLOT = LANES
ONE_LANE = HEAD_DIM
BLK_LANE0 = HEAD_DIM
TOK_TILE = 512
TQ = 256
KT = 256
FF_CHUNK = 256
N_ROPE_TAB = 3


def _vmem_limit(block_bytes, scratch_bytes=0):
    need = 2 * block_bytes + scratch_bytes
    return int(min(V7X_VMEM_BYTES - (4 << 20), max(2 * need, 32 << 20)))


def _nbytes(shape, dtype):
    n = 1
    for s in shape:
        n *= s
    return n * jnp.dtype(dtype).itemsize


def _rms(xf, g):
    return xf * lax.rsqrt(jnp.mean(xf * xf, axis=-1, keepdims=True) + EPS) * g


def _dot(a, b):
    return jnp.dot(a, b, preferred_element_type=F32)


def _dot_nt(a, b, precision=None):
    return lax.dot_general(a, b, (((1,), (1,)), ((), ())), precision=precision,
                           preferred_element_type=F32)


def _rope_trig_kernel(pos_ref, inv_ref, cos_ref, sin_ref):
    ang = pos_ref[...].astype(F32) * inv_ref[...]
    cos_ref[...] = jnp.cos(ang)
    sin_ref[...] = jnp.sin(ang)


def _rope_tables(pos_flat):
    n = pos_flat.shape[0]
    inv = ROPE_THETA ** (-jnp.arange(ROT_HALF, dtype=F32) * (2.0 / ROT_DIM))
    cos8, sin8 = pl.pallas_call(
        _rope_trig_kernel,
        out_shape=(jax.ShapeDtypeStruct((ROT_HALF, n), F32),) * 2,
        name="rope_trig",
    )(pos_flat.reshape(1, n), inv.reshape(ROT_HALF, 1))
    cos8, sin8 = cos8.T, sin8.T
    ones = jnp.ones((n, SLOT - ROT_DIM), F32)
    zeros = jnp.zeros((n, SLOT - ROT_HALF), F32)
    cos_t = jnp.concatenate([cos8, cos8, ones], axis=1)
    sin_a = jnp.concatenate([-sin8, zeros], axis=1)
    sin_b = jnp.concatenate([jnp.zeros((n, ROT_HALF), F32), sin8, zeros[:, ROT_HALF:]], axis=1)
    return jnp.concatenate([cos_t, sin_a, sin_b], axis=1)


def _rope_slot(y, tab):
    cos_t = tab[:, 0:SLOT]
    sin_a = tab[:, SLOT:2 * SLOT]
    sin_b = tab[:, 2 * SLOT:3 * SLOT]
    up = pltpu.roll(y, SLOT - ROT_HALF, axis=1)
    down = pltpu.roll(y, ROT_HALF, axis=1)
    return y * cos_t + up * sin_a + down * sin_b


def _ffn_kernel(x_ref, gpre_ref, wg_ref, wu_ref, wd_ref, gpost_ref, o_ref, acc_ref):
    x = x_ref[...]
    hn = _rms(x, gpre_ref[...]).astype(BF16)
    for c in range(D_FF // FF_CHUNK):
        sl = slice(c * FF_CHUNK, (c + 1) * FF_CHUNK)
        g = _dot(hn, wg_ref[:, sl])
        u = _dot(hn, wu_ref[:, sl])
        a = (g * jax.nn.sigmoid(g) * u).astype(BF16)
        d = _dot(a, wd_ref[sl, :])
        if c == 0:
            acc_ref[...] = d
        else:
            acc_ref[...] += d
    o_ref[...] = x + 0.5 * _rms(acc_ref[...], gpost_ref[...])


def _ffn(x, g_pre, w_gate, w_up, w_down, g_post):
    t = x.shape[0]
    row = lambda i: (i, 0)
    fixed = lambda i: (0, 0)
    blocks = (2 * _nbytes((TOK_TILE, D_MODEL), F32) + 3 * _nbytes((D_MODEL, D_FF), BF16))
    return pl.pallas_call(
        _ffn_kernel,
        grid=(t // TOK_TILE,),
        in_specs=[
            pl.BlockSpec((TOK_TILE, D_MODEL), row),
            pl.BlockSpec((1, D_MODEL), fixed),
            pl.BlockSpec((D_MODEL, D_FF), fixed),
            pl.BlockSpec((D_MODEL, D_FF), fixed),
            pl.BlockSpec((D_FF, D_MODEL), fixed),
            pl.BlockSpec((1, D_MODEL), fixed),
        ],
        out_specs=pl.BlockSpec((TOK_TILE, D_MODEL), row),
        out_shape=jax.ShapeDtypeStruct((t, D_MODEL), F32),
        scratch_shapes=[pltpu.VMEM((TOK_TILE, D_MODEL), F32)],
        compiler_params=pltpu.CompilerParams(
            dimension_semantics=("arbitrary",),
            vmem_limit_bytes=_vmem_limit(blocks, _nbytes((TOK_TILE, D_MODEL), F32))),
        name="ffn",
    )(x, g_pre.reshape(1, -1), w_gate.astype(BF16), w_up.astype(BF16), w_down.astype(BF16),
      g_post.reshape(1, -1))


W_Q_PAD = N_HEADS * SLOT
W_KV_PAD = N_KV * SLOT
W_CMP = 2 * KV_WIDTH
OFF_Q = 0
OFF_KS = OFF_Q + W_Q_PAD
OFF_VS = OFF_KS + W_KV_PAD
OFF_KW = OFF_VS + W_KV_PAD
OFF_VW = OFF_KW + W_KV_PAD
OFF_CMP = OFF_VW + W_KV_PAD
OFF_GATE = OFF_CMP + W_CMP
OFF_POOL = OFF_GATE + SLOT
IN_PAD_TOTAL = OFF_POOL + POOL_WIDTH


def _in_proj_kernel(x_ref, g_ref, w_ref, tab_ref, q_ref, ks_ref, vs_ref, kw_ref, vw_ref,
                    cmp_ref, gate_ref, pool_ref, *, seq_len):
    hn = _rms(x_ref[...], g_ref[...]).astype(BF16)
    tab = tab_ref[...]
    rows = x_ref.shape[0]
    lane = lax.broadcasted_iota(jnp.int32, (rows, SLOT), 1)
    tile_tok0 = (pl.program_id(0) * rows) % seq_len
    tok = tile_tok0 + lax.broadcasted_iota(jnp.int32, (rows, SLOT), 0)
    blk_onehot = (lane - BLK_LANE0 == tok >> SEL_SHIFT).astype(F32)
    one_lane = (lane == ONE_LANE).astype(F32)

    def slot(off, j):
        return _dot(hn, w_ref[:, off + j * SLOT:off + (j + 1) * SLOT])

    for h in range(N_HEADS):
        q_ref[:, h * SLOT:(h + 1) * SLOT] = _rope_slot(slot(OFF_Q, h), tab).astype(BF16)
    for g in range(N_KV):
        sl = slice(g * SLOT, (g + 1) * SLOT)
        ks_ref[:, sl] = (_rope_slot(slot(OFF_KS, g), tab) + blk_onehot).astype(BF16)
        vs_ref[:, sl] = (slot(OFF_VS, g) + one_lane).astype(BF16)
        kw_ref[:, sl] = _rope_slot(slot(OFF_KW, g), tab).astype(BF16)
        vw_ref[:, sl] = (slot(OFF_VW, g) + one_lane).astype(BF16)
    for j in range(W_CMP // SLOT):
        cmp_ref[:, j * SLOT:(j + 1) * SLOT] = slot(OFF_CMP, j).astype(BF16)
    gate_ref[...] = slot(OFF_GATE, 0)
    for j in range(POOL_WIDTH // SLOT):
        pool_ref[:, j * SLOT:(j + 1) * SLOT] = slot(OFF_POOL, j)


def _pad_slots(w, n_slots):
    k = w.shape[0]
    w = w.reshape(k, n_slots, HEAD_DIM)
    w = jnp.pad(w, ((0, 0), (0, 0), (0, SLOT - HEAD_DIM)))
    return w.reshape(k, n_slots * SLOT)


def _in_proj(x1, g_pre, w_in, tab, seq_len):
    t = x1.shape[0]
    o = 0
    parts = {}
    for name, width in (("q", Q_WIDTH), ("kc", KV_WIDTH), ("vc", KV_WIDTH), ("ks", KV_WIDTH),
                        ("vs", KV_WIDTH), ("kw", KV_WIDTH), ("vw", KV_WIDTH), ("gate", N_GATES),
                        ("pool", POOL_WIDTH), ("merge", 2 * D_MODEL)):
        parts[name] = w_in[:, o:o + width]
        o += width
    w_all = jnp.concatenate([
        _pad_slots(parts["q"] * HEAD_DIM ** -0.5, N_HEADS),
        _pad_slots(parts["ks"], N_KV), _pad_slots(parts["vs"], N_KV),
        _pad_slots(parts["kw"], N_KV), _pad_slots(parts["vw"], N_KV),
        parts["kc"], parts["vc"],
        jnp.pad(parts["gate"], ((0, 0), (0, SLOT - N_GATES))),
        parts["pool"],
    ], axis=1).astype(BF16)
    assert w_all.shape[1] == IN_PAD_TOTAL
    row = lambda i: (i, 0)
    fixed = lambda i: (0, 0)
    outs = (
        (W_Q_PAD, BF16), (W_KV_PAD, BF16), (W_KV_PAD, BF16), (W_KV_PAD, BF16), (W_KV_PAD, BF16),
        (W_CMP, BF16), (SLOT, F32), (POOL_WIDTH, F32))
    blocks = (_nbytes((TOK_TILE, D_MODEL), F32) + _nbytes((D_MODEL, IN_PAD_TOTAL), BF16)
              + _nbytes((TOK_TILE, N_ROPE_TAB * SLOT), F32)
              + sum(_nbytes((TOK_TILE, w), d) for w, d in outs))
    res = pl.pallas_call(
        functools.partial(_in_proj_kernel, seq_len=seq_len),
        grid=(t // TOK_TILE,),
        in_specs=[
            pl.BlockSpec((TOK_TILE, D_MODEL), row),
            pl.BlockSpec((1, D_MODEL), fixed),
            pl.BlockSpec((D_MODEL, IN_PAD_TOTAL), fixed),
            pl.BlockSpec((TOK_TILE, N_ROPE_TAB * SLOT), row),
        ],
        out_specs=[pl.BlockSpec((TOK_TILE, w), row) for w, _ in outs],
        out_shape=[jax.ShapeDtypeStruct((t, w), d) for w, d in outs],
        compiler_params=pltpu.CompilerParams(
            dimension_semantics=("arbitrary",), vmem_limit_bytes=_vmem_limit(blocks)),
        name="in_proj",
    )(x1, g_pre.reshape(1, -1), w_all, tab)
    return res, parts["merge"]


CHUNK_TOK = CMP_STRIDE
HID_PAD = N_KV * CMP_HIDDEN


def _compress_kernel(x_ref, w1k_ref, w1v_ref, pek_ref, pev_ref, w1k_raw_ref, w1v_raw_ref,
                     w2k_ref, w2v_ref, tab_ref, kc_ref, vc_ref, acck_ref, accv_ref):
    l = pl.program_id(0)
    xk = x_ref[:, 0:KV_WIDTH]
    xv = x_ref[:, KV_WIDTH:2 * KV_WIDTH]
    dk = _dot(xk, w1k_ref[0])
    dv = _dot(xv, w1v_ref[0])

    @pl.when(l == 0)
    def _():
        acck_ref[...] = dk
        accv_ref[...] = dv

    @pl.when(l > 0)
    def _():
        acck_ref[...] += dk
        accv_ref[...] += dv

    @pl.when(l == CHUNK_TOK - 1)
    def _():
        rows = acck_ref.shape[0]

        def finish(acc_ref, pe_ref, w1_raw_ref, w2_ref):
            bias = _dot(pe_ref[...], w1_raw_ref[...])[0:1, :]
            bias = jnp.concatenate([bias] * N_KV, axis=1)
            first = acc_ref[:, 0:HID_PAD]
            second = pltpu.roll(acc_ref[:, HID_PAD:2 * HID_PAD], rows - 1, axis=0)
            hid = jax.nn.gelu(first + second + bias).astype(BF16)
            return _dot(hid, w2_ref[...])

        kc = finish(acck_ref, pek_ref, w1k_raw_ref, w2k_ref)
        vc = finish(accv_ref, pev_ref, w1v_raw_ref, w2v_ref)
        tab = tab_ref[...]
        for g in range(N_KV):
            sl = slice(g * SLOT, (g + 1) * SLOT)
            kc_ref[:, sl] = _rope_slot(kc[:, sl], tab).astype(BF16)
            vc_ref[:, sl] = vc[:, sl].astype(BF16)


def _block_diag(w, n):
    eye = jnp.eye(n, dtype=w.dtype)
    out = jnp.einsum("ij,...ab->...iajb", eye, w)
    return out.reshape(w.shape[:-2] + (n * w.shape[-2], n * w.shape[-1]))


def _compress_weights(w1, w2):
    w1 = w1.reshape(2, CHUNK_TOK, HEAD_DIM, CMP_HIDDEN)
    bd = _block_diag(w1, N_KV)
    w1_tok = jnp.concatenate([bd[0], bd[1]], axis=-1).astype(BF16)
    w2_pad = jnp.pad(w2, ((0, 0), (0, SLOT - HEAD_DIM)))
    return w1_tok, _block_diag(w2_pad, N_KV).astype(BF16)


def _compress(cmp_in, cmp_pe_k, cmp_w1_k, cmp_w2_k, cmp_pe_v, cmp_w1_v, cmp_w2_v, tab_cmp):
    t = cmp_in.shape[0]
    rows = t // CHUNK_TOK
    x = cmp_in.reshape(rows, CHUNK_TOK * W_CMP)
    w1k, w2k = _compress_weights(cmp_w1_k, cmp_w2_k)
    w1v, w2v = _compress_weights(cmp_w1_v, cmp_w2_v)
    sub = 8
    pek = jnp.broadcast_to(cmp_pe_k.reshape(1, -1), (sub, CMP_LEN * HEAD_DIM)).astype(BF16)
    pev = jnp.broadcast_to(cmp_pe_v.reshape(1, -1), (sub, CMP_LEN * HEAD_DIM)).astype(BF16)
    fixed = lambda l: (0, 0)
    blocks = (_nbytes((rows, W_CMP), BF16) + 2 * _nbytes((KV_WIDTH, 2 * HID_PAD), BF16)
              + 2 * _nbytes((CMP_LEN * HEAD_DIM, CMP_HIDDEN), BF16)
              + 2 * _nbytes((HID_PAD, W_KV_PAD), BF16)
              + _nbytes((rows, N_ROPE_TAB * SLOT), F32) + 2 * _nbytes((rows, W_KV_PAD), BF16))
    scratch = 2 * _nbytes((rows, 2 * HID_PAD), F32)
    return pl.pallas_call(
        _compress_kernel,
        grid=(CHUNK_TOK,),
        in_specs=[
            pl.BlockSpec((rows, W_CMP), lambda l: (0, l)),
            pl.BlockSpec((1, KV_WIDTH, 2 * HID_PAD), lambda l: (l, 0, 0)),
            pl.BlockSpec((1, KV_WIDTH, 2 * HID_PAD), lambda l: (l, 0, 0)),
            pl.BlockSpec((sub, CMP_LEN * HEAD_DIM), fixed),
            pl.BlockSpec((sub, CMP_LEN * HEAD_DIM), fixed),
            pl.BlockSpec((CMP_LEN * HEAD_DIM, CMP_HIDDEN), fixed),
            pl.BlockSpec((CMP_LEN * HEAD_DIM, CMP_HIDDEN), fixed),
            pl.BlockSpec((HID_PAD, W_KV_PAD), fixed),
            pl.BlockSpec((HID_PAD, W_KV_PAD), fixed),
            pl.BlockSpec((rows, N_ROPE_TAB * SLOT), fixed),
        ],
        out_specs=[pl.BlockSpec((rows, W_KV_PAD), fixed)] * 2,
        out_shape=[jax.ShapeDtypeStruct((rows, W_KV_PAD), BF16)] * 2,
        scratch_shapes=[pltpu.VMEM((rows, 2 * HID_PAD), F32)] * 2,
        compiler_params=pltpu.CompilerParams(
            dimension_semantics=("arbitrary",), vmem_limit_bytes=_vmem_limit(blocks, scratch)),
        name="compress",
    )(x, w1k, w1v, pek, pev, cmp_w1_k.astype(BF16), cmp_w1_v.astype(BF16), w2k, w2v, tab_cmp)


def _nsa_kernel(q_ref, gate_ref, kc_ref, vc_ref, ks_ref, vs_ref, kw_ref, vw_ref, ovt_ref,
                o_ref, qg_ref, m_ref, acc_ref, mix_ref, *, n_slc):
    i = pl.program_id(1)
    start = i * TQ
    rows_g = GROUP * TQ
    n_cmp = kc_ref.shape[1]
    t_col = start + lax.broadcasted_iota(jnp.int32, (TQ, 1), 0)
    gates = jax.nn.sigmoid(gate_ref[...])
    n_cmp_idx = lax.broadcasted_iota(jnp.int32, (TQ, n_cmp), 1)
    cmp_valid = n_cmp_idx * CMP_STRIDE + (CMP_LEN - 1) <= t_col
    t_row = start + lax.broadcasted_iota(jnp.int32, (n_slc, TQ), 1)
    blk = lax.broadcasted_iota(jnp.int32, (n_slc, TQ), 0)
    forced = (blk == t_row >> SEL_SHIFT) | (blk == 0)
    causal_blk = blk * SEL_LEN <= t_row
    r_in = lax.broadcasted_iota(jnp.int32, (rows_g, KT), 0) & (TQ - 1)
    c_in = lax.broadcasted_iota(jnp.int32, (rows_g, KT), 1)
    diag_mask = c_in <= r_in
    far_mask = c_in > r_in
    data_lane = lax.broadcasted_iota(jnp.int32, (TQ, SLOT), 1) < HEAD_DIM

    def gate_col(branch, h):
        c = branch * N_HEADS + h
        return gates[:, c:c + 1]

    def flash_init():
        m_ref[...] = jnp.full(m_ref.shape, NEG_INF, F32)
        acc_ref[...] = jnp.zeros(acc_ref.shape, F32)

    def flash_step(g, k_ref, v_ref, off, mask):
        gs = slice(g * SLOT, (g + 1) * SLOT)
        s = _dot_nt(qg_ref[g], k_ref[0, pl.ds(off, KT), gs])
        if mask is not None:
            s = jnp.where(mask, s, NEG_INF)
        chunks = [s[:, c * LANES:(c + 1) * LANES] for c in range(KT // LANES)]
        m_old = m_ref[g]
        m_new = jnp.maximum(m_old, jnp.max(functools.reduce(jnp.maximum, chunks), axis=-1,
                                           keepdims=True))
        p = jnp.concatenate([jnp.exp(c - m_new) for c in chunks], axis=1).astype(BF16)
        acc_ref[g] = jnp.exp(m_old - m_new) * acc_ref[g] + _dot(p, v_ref[0, pl.ds(off, KT), gs])
        m_ref[g] = m_new

    def flash_mix(branch, first):
        for g in range(N_KV):
            acc = acc_ref[g]
            res = acc / acc[:, ONE_LANE:ONE_LANE + 1]
            for r in range(GROUP):
                h = g * GROUP + r
                term = gate_col(branch, h) * res[r * TQ:(r + 1) * TQ]
                mix_ref[h] = term if first else mix_ref[h] + term

    for g in range(N_KV):
        gs = slice(g * SLOT, (g + 1) * SLOT)
        kc_g = kc_ref[0, :, gs]
        vc_g = vc_ref[0, :, gs]
        p_sum = jnp.zeros((TQ, n_cmp), F32)
        for r in range(GROUP):
            h = g * GROUP + r
            s = _dot_nt(q_ref[:, h * SLOT:(h + 1) * SLOT], kc_g)
            s = jnp.where(cmp_valid, s, NEG_INF)
            p = jnp.exp(s - jnp.max(s, axis=-1, keepdims=True))
            p = jnp.where(cmp_valid, p, 0.0)
            den = jnp.sum(p, axis=-1, keepdims=True)
            p = p / jnp.where(den > 0.0, den, 1.0)
            p_sum = p_sum + p
            mix_ref[h] = gate_col(0, h) * _dot(p.astype(BF16), vc_g)
        imp_t = _dot_nt(ovt_ref[...], p_sum, precision=lax.Precision.HIGHEST)
        imp = imp_t[BLK_LANE0:BLK_LANE0 + n_slc, :]
        score = jnp.where(causal_blk, jnp.where(forced, FORCE_SCORE, imp), NEG_INF)
        rank = jnp.zeros((n_slc, TQ), jnp.int32)
        for c in range(n_slc):
            other = score[c:c + 1, :]
            beats = (other > score) | ((other == score) & (c < blk))
            rank = rank + beats.astype(jnp.int32)
        selected = causal_blk & (rank < N_SEL)
        bias = jnp.where(selected, 0.0, NEG_INF)
        bias_t = jnp.concatenate([jnp.zeros((BLK_LANE0, TQ), F32), bias,
                                  jnp.zeros((SLOT - BLK_LANE0 - n_slc, TQ), F32)], axis=0)
        bias_q = bias_t.T
        for r in range(GROUP):
            h = g * GROUP + r
            qh = q_ref[:, h * SLOT:(h + 1) * SLOT].astype(F32)
            qg_ref[g, r * TQ:(r + 1) * TQ, :] = (qh + bias_q).astype(BF16)

    off_diag = pl.multiple_of(start, KT)
    flash_init()

    def sel_body(kt, carry):
        off = pl.multiple_of(kt * KT, KT)
        for g in range(N_KV):
            flash_step(g, ks_ref, vs_ref, off, None)
        return carry

    lax.fori_loop(0, i, sel_body, 0)
    for g in range(N_KV):
        flash_step(g, ks_ref, vs_ref, off_diag, diag_mask)
    flash_mix(1, first=False)
    flash_init()
    n_back = WINDOW // KT
    for d in range(n_back, 0, -1):
        mask = far_mask if d == n_back else None

        @pl.when(i >= d)
        def _(d=d, mask=mask):
            off = pl.multiple_of((i - d) * KT, KT)
            for g in range(N_KV):
                flash_step(g, kw_ref, vw_ref, off, mask)

    for g in range(N_KV):
        flash_step(g, kw_ref, vw_ref, off_diag, diag_mask)
    flash_mix(2, first=False)
    for h2 in range(N_HEADS // 2):
        even = jnp.where(data_lane, mix_ref[2 * h2], 0.0)
        odd = jnp.where(data_lane, mix_ref[2 * h2 + 1], 0.0)
        o_ref[:, h2 * SLOT:(h2 + 1) * SLOT] = (even + pltpu.roll(odd, HEAD_DIM, axis=1)).astype(BF16)


def _overlap_t(n_cmp_pad, n_slc):
    c0 = jnp.arange(n_cmp_pad) * CMP_STRIDE
    s0 = jnp.arange(n_slc) * SEL_LEN
    ov = jnp.minimum(c0[None, :] + CMP_LEN, s0[:, None] + SEL_LEN) - jnp.maximum(c0[None, :], s0[:, None])
    ov = jnp.clip(ov, 0).astype(F32) / CMP_LEN
    return jnp.pad(ov, ((BLK_LANE0, SLOT - BLK_LANE0 - n_slc), (0, 0)))


def _nsa(q, gate, kc, vc, ks, vs, kw, vw, batch, seq_len):
    n_slc = seq_len // SEL_LEN
    n_cmp_pad = kc.shape[1]
    assert n_slc <= SLOT - BLK_LANE0 and n_slc % 8 == 0
    ovt = _overlap_t(n_cmp_pad, n_slc)
    nq = seq_len // TQ
    qrow = lambda b, i: (b * nq + i, 0)
    per_b = lambda b, i: (b, 0, 0)
    fixed = lambda b, i: (0, 0)
    kv3 = lambda a: a.reshape(batch, seq_len, W_KV_PAD)
    blocks = (_nbytes((TQ, W_Q_PAD), BF16) + _nbytes((TQ, SLOT), F32)
              + 2 * _nbytes((n_cmp_pad, W_KV_PAD), BF16) + 4 * _nbytes((seq_len, W_KV_PAD), BF16)
              + _nbytes((SLOT, n_cmp_pad), F32) + _nbytes((TQ, Q_WIDTH), BF16))
    scratch_shapes = [
        pltpu.VMEM((N_KV, GROUP * TQ, SLOT), BF16),
        pltpu.VMEM((N_KV, GROUP * TQ, LANES), F32),
        pltpu.VMEM((N_KV, GROUP * TQ, SLOT), F32),
        pltpu.VMEM((N_HEADS, TQ, SLOT), F32),
    ]
    scratch = sum(_nbytes(s.shape, s.dtype) for s in scratch_shapes)
    return pl.pallas_call(
        functools.partial(_nsa_kernel, n_slc=n_slc),
        grid=(batch, nq),
        in_specs=[
            pl.BlockSpec((TQ, W_Q_PAD), qrow),
            pl.BlockSpec((TQ, SLOT), qrow),
            pl.BlockSpec((1, n_cmp_pad, W_KV_PAD), per_b),
            pl.BlockSpec((1, n_cmp_pad, W_KV_PAD), per_b),
            pl.BlockSpec((1, seq_len, W_KV_PAD), per_b),
            pl.BlockSpec((1, seq_len, W_KV_PAD), per_b),
            pl.BlockSpec((1, seq_len, W_KV_PAD), per_b),
            pl.BlockSpec((1, seq_len, W_KV_PAD), per_b),
            pl.BlockSpec((SLOT, n_cmp_pad), fixed),
        ],
        out_specs=pl.BlockSpec((TQ, Q_WIDTH), qrow),
        out_shape=jax.ShapeDtypeStruct((batch * seq_len, Q_WIDTH), BF16),
        scratch_shapes=scratch_shapes,
        compiler_params=pltpu.CompilerParams(
            dimension_semantics=("arbitrary", "arbitrary"),
            vmem_limit_bytes=_vmem_limit(blocks, scratch)),
        name="nsa",
    )(q, gate, kc.reshape(batch, n_cmp_pad, W_KV_PAD), vc.reshape(batch, n_cmp_pad, W_KV_PAD),
      kv3(ks), kv3(vs), kv3(kw), kv3(vw), ovt)


def _pool_kernel(u_ref, w_ref, scale_ref, o_ref):
    seq = u_ref.shape[0]
    t = lax.broadcasted_iota(jnp.int32, (seq, POOL_GROUP), 0)

    def shifted(x, k):
        return jnp.where(t >= k, pltpu.roll(x, k, axis=0), 0.0)

    for gi, w in enumerate(POOL_WINDOWS):
        sl = slice(gi * POOL_GROUP, (gi + 1) * POOL_GROUP)
        x = u_ref[:, sl]
        wsum = x
        span = 1
        while span < w:
            wsum = wsum + shifted(wsum, span)
            span *= 2
        cnt = jnp.minimum(t + 1, w).astype(F32)
        pooled = (wsum / cnt - x).astype(BF16)
        o_ref[:, sl] = (_dot(pooled, w_ref[gi]) * scale_ref[:, sl]).astype(BF16)


def _pool(pool_in, pool_w, pool_scale, batch, seq_len):
    for w in POOL_WINDOWS:
        assert w & (w - 1) == 0
    blocks = (_nbytes((seq_len, POOL_WIDTH), F32) + _nbytes(pool_w.shape, BF16)
              + _nbytes((seq_len, POOL_WIDTH), BF16))
    return pl.pallas_call(
        _pool_kernel,
        grid=(batch,),
        in_specs=[
            pl.BlockSpec((seq_len, POOL_WIDTH), lambda b: (b, 0)),
            pl.BlockSpec(pool_w.shape, lambda b: (0, 0, 0)),
            pl.BlockSpec((1, POOL_WIDTH), lambda b: (0, 0)),
        ],
        out_specs=pl.BlockSpec((seq_len, POOL_WIDTH), lambda b: (b, 0)),
        out_shape=jax.ShapeDtypeStruct((batch * seq_len, POOL_WIDTH), BF16),
        compiler_params=pltpu.CompilerParams(
            dimension_semantics=("arbitrary",), vmem_limit_bytes=_vmem_limit(blocks)),
        name="pool",
    )(pool_in, pool_w.astype(BF16), pool_scale.reshape(1, -1))


def _merge_kernel(x_ref, gpre_ref, wm_ref, nsa_ref, wa_ref, pool_ref, wp_ref, wo_ref, gpost_ref,
                  o_ref):
    x = x_ref[...]
    hn = _rms(x, gpre_ref[...]).astype(BF16)
    g_attn = jax.nn.sigmoid(_dot(hn, wm_ref[:, 0:D_MODEL]))
    g_pool = jax.nn.sigmoid(_dot(hn, wm_ref[:, D_MODEL:2 * D_MODEL]))
    y = g_attn * _dot(nsa_ref[...], wa_ref[...]) + g_pool * _dot(pool_ref[...], wp_ref[...])
    h = _dot(y.astype(BF16), wo_ref[...])
    o_ref[...] = x + _rms(h, gpost_ref[...])


def _merge(x1, g_pre, w_merge, o_nsa, w_attn, o_pool, w_pool, w_out, g_post):
    t = x1.shape[0]
    row = lambda i: (i, 0)
    fixed = lambda i: (0, 0)
    blocks = (2 * _nbytes((TOK_TILE, D_MODEL), F32) + _nbytes((D_MODEL, 2 * D_MODEL), BF16)
              + _nbytes((TOK_TILE, Q_WIDTH), BF16) + _nbytes((Q_WIDTH, D_MODEL), BF16)
              + _nbytes((TOK_TILE, POOL_WIDTH), BF16) + _nbytes((POOL_WIDTH, D_MODEL), BF16)
              + _nbytes((D_MODEL, D_MODEL), BF16))
    return pl.pallas_call(
        _merge_kernel,
        grid=(t // TOK_TILE,),
        in_specs=[
            pl.BlockSpec((TOK_TILE, D_MODEL), row),
            pl.BlockSpec((1, D_MODEL), fixed),
            pl.BlockSpec((D_MODEL, 2 * D_MODEL), fixed),
            pl.BlockSpec((TOK_TILE, Q_WIDTH), row),
            pl.BlockSpec((Q_WIDTH, D_MODEL), fixed),
            pl.BlockSpec((TOK_TILE, POOL_WIDTH), row),
            pl.BlockSpec((POOL_WIDTH, D_MODEL), fixed),
            pl.BlockSpec((D_MODEL, D_MODEL), fixed),
            pl.BlockSpec((1, D_MODEL), fixed),
        ],
        out_specs=pl.BlockSpec((TOK_TILE, D_MODEL), row),
        out_shape=jax.ShapeDtypeStruct((t, D_MODEL), F32),
        compiler_params=pltpu.CompilerParams(
            dimension_semantics=("arbitrary",), vmem_limit_bytes=_vmem_limit(blocks)),
        name="merge",
    )(x1, g_pre.reshape(1, -1), w_merge.astype(BF16), o_nsa, w_attn.astype(BF16), o_pool,
      w_pool.astype(BF16), w_out.astype(BF16), g_post.reshape(1, -1))


def _layer(x, tab, tab_cmp, batch, seq_len, g_ffn1_pre, w_ffn1_gate, w_ffn1_up, w_ffn1_down,
           g_ffn1_post, g_mix_pre, w_in, cmp_pe_k, cmp_w1_k, cmp_w2_k, cmp_pe_v, cmp_w1_v, cmp_w2_v,
           w_attn_branch, pool_w, pool_scale, w_pool_branch, w_out, g_mix_post, g_ffn2_pre,
           w_ffn2_gate, w_ffn2_up, w_ffn2_down, g_ffn2_post):
    x1 = _ffn(x, g_ffn1_pre, w_ffn1_gate, w_ffn1_up, w_ffn1_down, g_ffn1_post)
    (q, ks, vs, kw, vw, cmp_in, gate, pool_in), w_merge = _in_proj(x1, g_mix_pre, w_in, tab, seq_len)
    kc, vc = _compress(cmp_in, cmp_pe_k, cmp_w1_k, cmp_w2_k, cmp_pe_v, cmp_w1_v, cmp_w2_v, tab_cmp)
    n_cmp_pad = seq_len // CMP_STRIDE
    kc = kc.reshape(batch, n_cmp_pad, W_KV_PAD)
    vc = vc.reshape(batch, n_cmp_pad, W_KV_PAD)
    o_nsa = _nsa(q, gate, kc, vc, ks, vs, kw, vw, batch, seq_len)
    o_pool = _pool(pool_in, pool_w, pool_scale, batch, seq_len)
    x2 = _merge(x1, g_mix_pre, w_merge, o_nsa, w_attn_branch, o_pool, w_pool_branch, w_out,
                g_mix_post)
    return _ffn(x2, g_ffn2_pre, w_ffn2_gate, w_ffn2_up, w_ffn2_down, g_ffn2_post)


def kernel(x, positions, g_ffn1_pre, w_ffn1_gate, w_ffn1_up, w_ffn1_down, g_ffn1_post, g_mix_pre, w_in, cmp_pe_k, cmp_w1_k, cmp_w2_k, cmp_pe_v, cmp_w1_v, cmp_w2_v, w_attn_branch, pool_w, pool_scale, w_pool_branch, w_out, g_mix_post, g_ffn2_pre, w_ffn2_gate, w_ffn2_up, w_ffn2_down, g_ffn2_post):
    batch, seq_len, d_model = x.shape
    assert d_model == D_MODEL and seq_len % TOK_TILE == 0 and seq_len % TQ == 0
    assert TQ == KT and TQ % SEL_LEN == 0 and seq_len % CMP_STRIDE == 0
    t = batch * seq_len
    tab = _rope_tables(positions.reshape(t))
    n_cmp_pad = seq_len // CMP_STRIDE
    last_tok = jnp.minimum(jnp.arange(n_cmp_pad) * CMP_STRIDE + CMP_LEN - 1, seq_len - 1)
    tab_cmp = tab.reshape(batch, seq_len, -1)[:, last_tok].reshape(batch * n_cmp_pad, -1)
    xf = x.reshape(t, D_MODEL)
    per_layer = (g_ffn1_pre, w_ffn1_gate, w_ffn1_up, w_ffn1_down, g_ffn1_post, g_mix_pre, w_in,
                 cmp_pe_k, cmp_w1_k, cmp_w2_k, cmp_pe_v, cmp_w1_v, cmp_w2_v, w_attn_branch, pool_w,
                 pool_scale, w_pool_branch, w_out, g_mix_post, g_ffn2_pre, w_ffn2_gate, w_ffn2_up,
                 w_ffn2_down, g_ffn2_post)
    for l in range(g_ffn1_pre.shape[0]):
        xf = _layer(xf, tab, tab_cmp, batch, seq_len, *(p[l] for p in per_layer))
    return xf.reshape(batch, seq_len, D_MODEL)
```

```python
import functools

import jax
import jax.numpy as jnp
from jax import lax
from jax.experimental import pallas as pl
from jax.experimental.pallas import tpu as pltpu

F32 = jnp.float32
BF16 = jnp.bfloat16

D_MODEL = 1024
N_HEADS = 16
HEAD_DIM = 64
N_KV = 4
GROUP = N_HEADS // N_KV
ROT_DIM = HEAD_DIM // 4
ROT_HALF = ROT_DIM // 2
ROPE_THETA = 500000.0
CMP_LEN = 32
CMP_STRIDE = 16
CMP_HIDDEN = 2 * HEAD_DIM
SEL_LEN = 64
SEL_SHIFT = SEL_LEN.bit_length() - 1
N_SEL = 8
WINDOW = 512
POOL_WINDOWS = (2, 4, 8, 16)
POOL_WIDTH = D_MODEL // 2
POOL_GROUP = POOL_WIDTH // len(POOL_WINDOWS)
D_FF = 2816
EPS = 1e-6
NEG_INF = -1e30
FORCE_SCORE = 1e4
Q_WIDTH = N_HEADS * HEAD_DIM
KV_WIDTH = N_KV * HEAD_DIM
N_GATES = 3 * N_HEADS

LANES = 128
V7X_VMEM_BYTES = 64 * 1024 * 1024

SLOT = LANES
ONE_LANE = HEAD_DIM
BLK_LANE0 = HEAD_DIM
TOK_TILE = 512
TQ = 256
KT = 256
CHAIN_COLS = 256
PIPE_DEPTH = 3
FF_CHUNK = 256
N_ROPE_TAB = 3


def _vmem_limit(block_bytes, scratch_bytes=0):
    need = 2 * block_bytes + scratch_bytes
    return int(min(V7X_VMEM_BYTES - (4 << 20), max(2 * need, 32 << 20)))


def _nbytes(shape, dtype):
    n = 1
    for s in shape:
        n *= s
    return n * jnp.dtype(dtype).itemsize


def _rms(xf, g):
    return xf * lax.rsqrt(jnp.mean(xf * xf, axis=-1, keepdims=True) + EPS) * g


def _dot(a, b):
    return jnp.dot(a, b, preferred_element_type=F32)


def _dot_nt(a, b, precision=None):
    return lax.dot_general(a, b, (((1,), (1,)), ((), ())), precision=precision,
                           preferred_element_type=F32)


def _rope_trig_kernel(pos_ref, inv_ref, cos_ref, sin_ref):
    ang = pos_ref[...].astype(F32) * inv_ref[...]
    cos_ref[...] = jnp.cos(ang)
    sin_ref[...] = jnp.sin(ang)


def _rope_tables(pos_flat):
    n = pos_flat.shape[0]
    inv = ROPE_THETA ** (-jnp.arange(ROT_HALF, dtype=F32) * (2.0 / ROT_DIM))
    cos8, sin8 = pl.pallas_call(
        _rope_trig_kernel,
        out_shape=(jax.ShapeDtypeStruct((ROT_HALF, n), F32),) * 2,
        name="rope_trig",
    )(pos_flat.reshape(1, n), inv.reshape(ROT_HALF, 1))
    cos8, sin8 = cos8.T, sin8.T
    ones = jnp.ones((n, SLOT - ROT_DIM), F32)
    zeros = jnp.zeros((n, SLOT - ROT_HALF), F32)
    cos_t = jnp.concatenate([cos8, cos8, ones], axis=1)
    sin_a = jnp.concatenate([-sin8, zeros], axis=1)
    sin_b = jnp.concatenate([jnp.zeros((n, ROT_HALF), F32), sin8, zeros[:, ROT_HALF:]], axis=1)
    return jnp.concatenate([cos_t, sin_a, sin_b], axis=1)


def _rope_slot(y, tab):
    cos_t = tab[:, 0:SLOT]
    sin_a = tab[:, SLOT:2 * SLOT]
    sin_b = tab[:, 2 * SLOT:3 * SLOT]
    up = pltpu.roll(y, SLOT - ROT_HALF, axis=1)
    down = pltpu.roll(y, ROT_HALF, axis=1)
    return y * cos_t + up * sin_a + down * sin_b


def _ffn_kernel(x_ref, gpre_ref, wg_ref, wu_ref, wd_ref, gpost_ref, o_ref, acc_ref):
    x = x_ref[...]
    hn = _rms(x, gpre_ref[...]).astype(BF16)
    for c in range(D_FF // FF_CHUNK):
        sl = slice(c * FF_CHUNK, (c + 1) * FF_CHUNK)
        g = _dot(hn, wg_ref[:, sl])
        u = _dot(hn, wu_ref[:, sl])
        a = (g * jax.nn.sigmoid(g) * u).astype(BF16)
        d = _dot(a, wd_ref[sl, :])
        if c == 0:
            acc_ref[...] = d
        else:
            acc_ref[...] += d
    o_ref[...] = x + 0.5 * _rms(acc_ref[...], gpost_ref[...])


def _ffn(x, g_pre, w_gate, w_up, w_down, g_post):
    t = x.shape[0]
    row = lambda i: (i, 0)
    fixed = lambda i: (0, 0)
    blocks = (2 * _nbytes((TOK_TILE, D_MODEL), F32) + 3 * _nbytes((D_MODEL, D_FF), BF16))
    return pl.pallas_call(
        _ffn_kernel,
        grid=(t // TOK_TILE,),
        in_specs=[
            pl.BlockSpec((TOK_TILE, D_MODEL), row),
            pl.BlockSpec((1, D_MODEL), fixed),
            pl.BlockSpec((D_MODEL, D_FF), fixed),
            pl.BlockSpec((D_MODEL, D_FF), fixed),
            pl.BlockSpec((D_FF, D_MODEL), fixed),
            pl.BlockSpec((1, D_MODEL), fixed),
        ],
        out_specs=pl.BlockSpec((TOK_TILE, D_MODEL), row),
        out_shape=jax.ShapeDtypeStruct((t, D_MODEL), F32),
        scratch_shapes=[pltpu.VMEM((TOK_TILE, D_MODEL), F32)],
        compiler_params=pltpu.CompilerParams(
            dimension_semantics=("arbitrary",),
            vmem_limit_bytes=_vmem_limit(blocks, _nbytes((TOK_TILE, D_MODEL), F32))),
        name="ffn",
    )(x, g_pre.reshape(1, -1), w_gate.astype(BF16), w_up.astype(BF16), w_down.astype(BF16),
      g_post.reshape(1, -1))


W_Q_PAD = N_HEADS * SLOT
W_KV_PAD = N_KV * SLOT
W_CMP = 2 * KV_WIDTH
OFF_Q = 0
OFF_KS = OFF_Q + W_Q_PAD
OFF_KW = OFF_KS + W_KV_PAD
OFF_CMP = OFF_KW + W_KV_PAD
OFF_GATE = OFF_CMP + W_CMP
OFF_POOL = OFF_GATE + SLOT
IN_PAD_TOTAL = OFF_POOL + POOL_WIDTH


def _in_proj_kernel(x_ref, g_ref, w_ref, wvt_ref, tab_ref, q_ref, ks_ref, kw_ref, vst_ref, vwt_ref,
                    cmp_ref, gate_ref, pool_ref, *, seq_len):
    hn = _rms(x_ref[...], g_ref[...]).astype(BF16)
    tab = tab_ref[...]
    rows = x_ref.shape[0]
    lane = lax.broadcasted_iota(jnp.int32, (rows, SLOT), 1)
    tile_tok0 = (pl.program_id(0) * rows) % seq_len
    tok = tile_tok0 + lax.broadcasted_iota(jnp.int32, (rows, SLOT), 0)
    blk_onehot = (lane - BLK_LANE0 == tok >> SEL_SHIFT).astype(F32)

    def slot(off, j):
        return _dot(hn, w_ref[:, off + j * SLOT:off + (j + 1) * SLOT])

    for h in range(N_HEADS):
        q_ref[:, h * SLOT:(h + 1) * SLOT] = _rope_slot(slot(OFF_Q, h), tab).astype(BF16)
    for g in range(N_KV):
        sl = slice(g * SLOT, (g + 1) * SLOT)
        ks_ref[:, sl] = (_rope_slot(slot(OFF_KS, g), tab) + blk_onehot).astype(BF16)
        kw_ref[:, sl] = _rope_slot(slot(OFF_KW, g), tab).astype(BF16)
    one_row = ((lax.broadcasted_iota(jnp.int32, (W_KV_PAD, rows), 0) & (SLOT - 1)) == ONE_LANE).astype(F32)
    for vt_ref, r0 in ((vst_ref, 0), (vwt_ref, W_KV_PAD)):
        vt = (_dot_nt(wvt_ref[r0:r0 + W_KV_PAD, :], hn) + one_row).astype(BF16)
        for j in range(rows // KT):
            vt_ref[j] = vt[:, j * KT:(j + 1) * KT]
    for j in range(W_CMP // SLOT):
        cmp_ref[:, j * SLOT:(j + 1) * SLOT] = slot(OFF_CMP, j).astype(BF16)
    gate_ref[...] = slot(OFF_GATE, 0)
    for j in range(POOL_WIDTH // SLOT):
        pool_ref[:, j * SLOT:(j + 1) * SLOT] = slot(OFF_POOL, j)


def _pad_slots(w, n_slots):
    k = w.shape[0]
    w = w.reshape(k, n_slots, HEAD_DIM)
    w = jnp.pad(w, ((0, 0), (0, 0), (0, SLOT - HEAD_DIM)))
    return w.reshape(k, n_slots * SLOT)


def _in_proj(x1, g_pre, w_in, tab, seq_len):
    t = x1.shape[0]
    o = 0
    parts = {}
    for name, width in (("q", Q_WIDTH), ("kc", KV_WIDTH), ("vc", KV_WIDTH), ("ks", KV_WIDTH),
                        ("vs", KV_WIDTH), ("kw", KV_WIDTH), ("vw", KV_WIDTH), ("gate", N_GATES),
                        ("pool", POOL_WIDTH), ("merge", 2 * D_MODEL)):
        parts[name] = w_in[:, o:o + width]
        o += width
    w_all = jnp.concatenate([
        _pad_slots(parts["q"] * HEAD_DIM ** -0.5, N_HEADS),
        _pad_slots(parts["ks"], N_KV), _pad_slots(parts["kw"], N_KV),
        parts["kc"], parts["vc"],
        jnp.pad(parts["gate"], ((0, 0), (0, SLOT - N_GATES))),
        parts["pool"],
    ], axis=1).astype(BF16)
    assert w_all.shape[1] == IN_PAD_TOTAL
    w_vt = jnp.concatenate([_pad_slots(parts["vs"], N_KV), _pad_slots(parts["vw"], N_KV)],
                           axis=1).T.astype(BF16)
    row = lambda i: (i, 0)
    fixed = lambda i: (0, 0)
    tiles = TOK_TILE // KT
    row_outs = {0: (W_Q_PAD, BF16), 1: (W_KV_PAD, BF16), 2: (W_KV_PAD, BF16),
                5: (W_CMP, BF16), 6: (SLOT, F32), 7: (POOL_WIDTH, F32)}
    out_specs, out_shape = [], []
    for k in range(8):
        if k in row_outs:
            w, d = row_outs[k]
            out_specs.append(pl.BlockSpec((TOK_TILE, w), row))
            out_shape.append(jax.ShapeDtypeStruct((t, w), d))
        else:
            out_specs.append(pl.BlockSpec((tiles, W_KV_PAD, KT), lambda i: (i, 0, 0)))
            out_shape.append(jax.ShapeDtypeStruct((t // KT, W_KV_PAD, KT), BF16))
    blocks = (_nbytes((TOK_TILE, D_MODEL), F32) + _nbytes((D_MODEL, IN_PAD_TOTAL), BF16)
              + _nbytes(w_vt.shape, BF16) + _nbytes((TOK_TILE, N_ROPE_TAB * SLOT), F32)
              + sum(_nbytes((TOK_TILE, w), d) for w, d in row_outs.values())
              + 2 * _nbytes((TOK_TILE, W_KV_PAD), BF16))
    res = pl.pallas_call(
        functools.partial(_in_proj_kernel, seq_len=seq_len),
        grid=(t // TOK_TILE,),
        in_specs=[
            pl.BlockSpec((TOK_TILE, D_MODEL), row),
            pl.BlockSpec((1, D_MODEL), fixed),
            pl.BlockSpec((D_MODEL, IN_PAD_TOTAL), fixed),
            pl.BlockSpec(w_vt.shape, fixed),
            pl.BlockSpec((TOK_TILE, N_ROPE_TAB * SLOT), row),
        ],
        out_specs=out_specs,
        out_shape=out_shape,
        compiler_params=pltpu.CompilerParams(
            dimension_semantics=("arbitrary",), vmem_limit_bytes=_vmem_limit(blocks)),
        name="in_proj",
    )(x1, g_pre.reshape(1, -1), w_all, w_vt, tab)
    return res, parts["merge"]


CHUNK_TOK = CMP_STRIDE
HID_PAD = N_KV * CMP_HIDDEN


def _compress_kernel(x_ref, w1k_ref, w1v_ref, pek_ref, pev_ref, w1k_raw_ref, w1v_raw_ref,
                     w2k_ref, w2v_ref, tab_ref, kc_ref, vc_ref, acck_ref, accv_ref):
    l = pl.program_id(0)
    xk = x_ref[:, 0:KV_WIDTH]
    xv = x_ref[:, KV_WIDTH:2 * KV_WIDTH]
    dk = _dot(xk, w1k_ref[0])
    dv = _dot(xv, w1v_ref[0])

    @pl.when(l == 0)
    def _():
        acck_ref[...] = dk
        accv_ref[...] = dv

    @pl.when(l > 0)
    def _():
        acck_ref[...] += dk
        accv_ref[...] += dv

    @pl.when(l == CHUNK_TOK - 1)
    def _():
        rows = acck_ref.shape[0]

        def finish(acc_ref, pe_ref, w1_raw_ref, w2_ref):
            bias = _dot(pe_ref[...], w1_raw_ref[...])[0:1, :]
            bias = jnp.concatenate([bias] * N_KV, axis=1)
            first = acc_ref[:, 0:HID_PAD]
            second = pltpu.roll(acc_ref[:, HID_PAD:2 * HID_PAD], rows - 1, axis=0)
            hid = jax.nn.gelu(first + second + bias).astype(BF16)
            return _dot(hid, w2_ref[...])

        kc = finish(acck_ref, pek_ref, w1k_raw_ref, w2k_ref)
        vc = finish(accv_ref, pev_ref, w1v_raw_ref, w2v_ref)
        tab = tab_ref[...]
        for g in range(N_KV):
            sl = slice(g * SLOT, (g + 1) * SLOT)
            kc_ref[:, sl] = _rope_slot(kc[:, sl], tab).astype(BF16)
            vc_ref[:, sl] = vc[:, sl].astype(BF16)


def _block_diag(w, n):
    eye = jnp.eye(n, dtype=w.dtype)
    out = jnp.einsum("ij,...ab->...iajb", eye, w)
    return out.reshape(w.shape[:-2] + (n * w.shape[-2], n * w.shape[-1]))


def _compress_weights(w1, w2):
    w1 = w1.reshape(2, CHUNK_TOK, HEAD_DIM, CMP_HIDDEN)
    bd = _block_diag(w1, N_KV)
    w1_tok = jnp.concatenate([bd[0], bd[1]], axis=-1).astype(BF16)
    w2_pad = jnp.pad(w2, ((0, 0), (0, SLOT - HEAD_DIM)))
    return w1_tok, _block_diag(w2_pad, N_KV).astype(BF16)


def _compress(cmp_in, cmp_pe_k, cmp_w1_k, cmp_w2_k, cmp_pe_v, cmp_w1_v, cmp_w2_v, tab_cmp):
    t = cmp_in.shape[0]
    rows = t // CHUNK_TOK
    x = cmp_in.reshape(rows, CHUNK_TOK * W_CMP)
    w1k, w2k = _compress_weights(cmp_w1_k, cmp_w2_k)
    w1v, w2v = _compress_weights(cmp_w1_v, cmp_w2_v)
    sub = 8
    pek = jnp.broadcast_to(cmp_pe_k.reshape(1, -1), (sub, CMP_LEN * HEAD_DIM)).astype(BF16)
    pev = jnp.broadcast_to(cmp_pe_v.reshape(1, -1), (sub, CMP_LEN * HEAD_DIM)).astype(BF16)
    fixed = lambda l: (0, 0)
    blocks = (_nbytes((rows, W_CMP), BF16) + 2 * _nbytes((KV_WIDTH, 2 * HID_PAD), BF16)
              + 2 * _nbytes((CMP_LEN * HEAD_DIM, CMP_HIDDEN), BF16)
              + 2 * _nbytes((HID_PAD, W_KV_PAD), BF16)
              + _nbytes((rows, N_ROPE_TAB * SLOT), F32) + 2 * _nbytes((rows, W_KV_PAD), BF16))
    scratch = 2 * _nbytes((rows, 2 * HID_PAD), F32)
    return pl.pallas_call(
        _compress_kernel,
        grid=(CHUNK_TOK,),
        in_specs=[
            pl.BlockSpec((rows, W_CMP), lambda l: (0, l)),
            pl.BlockSpec((1, KV_WIDTH, 2 * HID_PAD), lambda l: (l, 0, 0)),
            pl.BlockSpec((1, KV_WIDTH, 2 * HID_PAD), lambda l: (l, 0, 0)),
            pl.BlockSpec((sub, CMP_LEN * HEAD_DIM), fixed),
            pl.BlockSpec((sub, CMP_LEN * HEAD_DIM), fixed),
            pl.BlockSpec((CMP_LEN * HEAD_DIM, CMP_HIDDEN), fixed),
            pl.BlockSpec((CMP_LEN * HEAD_DIM, CMP_HIDDEN), fixed),
            pl.BlockSpec((HID_PAD, W_KV_PAD), fixed),
            pl.BlockSpec((HID_PAD, W_KV_PAD), fixed),
            pl.BlockSpec((rows, N_ROPE_TAB * SLOT), fixed),
        ],
        out_specs=[pl.BlockSpec((rows, W_KV_PAD), fixed)] * 2,
        out_shape=[jax.ShapeDtypeStruct((rows, W_KV_PAD), BF16)] * 2,
        scratch_shapes=[pltpu.VMEM((rows, 2 * HID_PAD), F32)] * 2,
        compiler_params=pltpu.CompilerParams(
            dimension_semantics=("arbitrary",), vmem_limit_bytes=_vmem_limit(blocks, scratch)),
        name="compress",
    )(x, w1k, w1v, pek, pev, cmp_w1_k.astype(BF16), cmp_w1_v.astype(BF16), w2k, w2v, tab_cmp)


def _nsa_kernel(q_ref, gate_ref, kc_ref, vc_ref, ks_ref, vst_ref, kw_ref, vwt_ref, ov_ref,
                o_ref, qg_ref, mix_ref, *state_refs, n_slc):
    i = pl.program_id(1)
    start = i * TQ
    cols_g = GROUP * TQ
    n_cmp = kc_ref.shape[1]
    gates_t = jax.nn.sigmoid(gate_ref[...]).T
    t_cmp = start + lax.broadcasted_iota(jnp.int32, (n_cmp, TQ), 1)
    n_idx = lax.broadcasted_iota(jnp.int32, (n_cmp, TQ), 0)
    cmp_valid = n_idx * CMP_STRIDE + (CMP_LEN - 1) <= t_cmp
    t_row = start + lax.broadcasted_iota(jnp.int32, (n_slc, TQ), 1)
    blk = lax.broadcasted_iota(jnp.int32, (n_slc, TQ), 0)
    forced = (blk == t_row >> SEL_SHIFT) | (blk == 0)
    causal_blk = blk * SEL_LEN <= t_row
    key_in = lax.broadcasted_iota(jnp.int32, (KT, cols_g), 0)
    qry_in = lax.broadcasted_iota(jnp.int32, (KT, cols_g), 1) & (TQ - 1)
    diag_mask = key_in <= qry_in
    far_mask = key_in > qry_in

    def gate_row(branch, h):
        c = branch * N_HEADS + h
        return gates_t[c:c + 1, :]

    chains_g = cols_g // CHAIN_COLS
    m_refs, acc_refs = state_refs[:N_KV * chains_g], state_refs[N_KV * chains_g:]

    def flash_init():
        for m_ref, acc_ref in zip(m_refs, acc_refs):
            m_ref[...] = jnp.full(m_ref.shape, NEG_INF, F32)
            acc_ref[...] = jnp.zeros(acc_ref.shape, F32)

    def flash_tile(k_ref, vt_ref, kt, mask):
        off = pl.multiple_of(kt * KT, KT)
        n_chains = N_KV * chains_g
        scores = {}
        for step in range(n_chains + PIPE_DEPTH):
            if step < n_chains:
                g, c = divmod(step, chains_g)
                k_tile = k_ref[0, pl.ds(off, KT), g * SLOT:(g + 1) * SLOT]
                cs = slice(c * CHAIN_COLS, (c + 1) * CHAIN_COLS)
                s = _dot_nt(k_tile, qg_ref[g, cs, :])
                if mask is not None:
                    s = jnp.where(mask[:, cs], s, NEG_INF)
                scores[step] = s
            done = step - PIPE_DEPTH
            if done >= 0:
                g = done // chains_g
                m_ref, acc_ref = m_refs[done], acc_refs[done]
                s = scores.pop(done)
                m_old = m_ref[...]
                m_new = jnp.maximum(m_old, jnp.max(s, axis=0, keepdims=True))
                p = jnp.exp(s - m_new).astype(BF16)
                pv = _dot(vt_ref[kt, g * SLOT:(g + 1) * SLOT, :], p)
                acc_ref[...] = jnp.exp(m_old - m_new) * acc_ref[...] + pv
                m_ref[...] = m_new

    def flash_mix(branch):
        for g in range(N_KV):
            for r in range(GROUP):
                h = g * GROUP + r
                c, c0 = divmod(r * TQ, CHAIN_COLS)
                acc = acc_refs[g * chains_g + c][:, c0:c0 + TQ]
                scale = gate_row(branch, h) * (1.0 / acc[ONE_LANE:ONE_LANE + 1, :])
                mix_ref[h] = mix_ref[h] + scale * acc

    for g in range(N_KV):
        gs = slice(g * SLOT, (g + 1) * SLOT)
        kc_g = kc_ref[0, :, gs]
        vct_g = vc_ref[0, :, gs].astype(F32).T.astype(BF16)
        p_sum = jnp.zeros((n_cmp, TQ), F32)
        for r in range(GROUP):
            h = g * GROUP + r
            s = _dot_nt(kc_g, q_ref[:, h * SLOT:(h + 1) * SLOT])
            s = jnp.where(cmp_valid, s, NEG_INF)
            p = jnp.exp(s - jnp.max(s, axis=0, keepdims=True))
            p = jnp.where(cmp_valid, p, 0.0)
            den = jnp.sum(p, axis=0, keepdims=True)
            p = p / jnp.where(den > 0.0, den, 1.0)
            p_sum = p_sum + p
            mix_ref[h] = gate_row(0, h) * _dot(vct_g, p.astype(BF16))
        imp = jnp.dot(ov_ref[...], p_sum, precision=lax.Precision.HIGHEST,
                      preferred_element_type=F32)
        score = jnp.where(causal_blk, jnp.where(forced, FORCE_SCORE, imp), NEG_INF)
        rank = jnp.zeros((n_slc, TQ), jnp.int32)
        for c in range(n_slc):
            other = score[c:c + 1, :]
            beats = (other > score) | ((other == score) & (c < blk))
            rank = rank + beats.astype(jnp.int32)
        selected = causal_blk & (rank < N_SEL)
        bias = jnp.where(selected, 0.0, NEG_INF)
        bias_t = jnp.concatenate([jnp.zeros((BLK_LANE0, TQ), F32), bias,
                                  jnp.zeros((SLOT - BLK_LANE0 - n_slc, TQ), F32)], axis=0)
        bias_q = bias_t.T
        for r in range(GROUP):
            h = g * GROUP + r
            qh = q_ref[:, h * SLOT:(h + 1) * SLOT].astype(F32)
            qg_ref[g, r * TQ:(r + 1) * TQ, :] = (qh + bias_q).astype(BF16)

    flash_init()

    def sel_body(kt, carry):
        flash_tile(ks_ref, vst_ref, kt, None)
        return carry

    lax.fori_loop(0, i, sel_body, 0)
    flash_tile(ks_ref, vst_ref, i, diag_mask)
    flash_mix(1)
    flash_init()
    n_back = WINDOW // KT
    for d in range(n_back, 0, -1):
        mask = far_mask if d == n_back else None

        @pl.when(i >= d)
        def _(d=d, mask=mask):
            flash_tile(kw_ref, vwt_ref, i - d, mask)

    flash_tile(kw_ref, vwt_ref, i, diag_mask)
    flash_mix(2)
    for h2 in range(N_HEADS // 2):
        pair = jnp.concatenate([mix_ref[2 * h2, 0:HEAD_DIM, :], mix_ref[2 * h2 + 1, 0:HEAD_DIM, :]],
                               axis=0)
        o_ref[:, h2 * SLOT:(h2 + 1) * SLOT] = pair.T.astype(BF16)


def _overlap(n_cmp_pad, n_slc):
    c0 = jnp.arange(n_cmp_pad) * CMP_STRIDE
    s0 = jnp.arange(n_slc) * SEL_LEN
    ov = jnp.minimum(c0[None, :] + CMP_LEN, s0[:, None] + SEL_LEN) - jnp.maximum(c0[None, :], s0[:, None])
    return jnp.clip(ov, 0).astype(F32) / CMP_LEN


def _nsa(q, gate, kc, vc, ks, vst, kw, vwt, batch, seq_len):
    n_slc = seq_len // SEL_LEN
    n_cmp_pad = kc.shape[1]
    assert n_slc <= SLOT - BLK_LANE0 and n_slc % 8 == 0
    ov = _overlap(n_cmp_pad, n_slc)
    nq = seq_len // TQ
    n_kt = seq_len // KT
    qrow = lambda b, i: (b * nq + i, 0)
    per_b = lambda b, i: (b, 0, 0)
    fixed = lambda b, i: (0, 0)
    kv3 = lambda a: a.reshape(batch, seq_len, W_KV_PAD)
    blocks = (_nbytes((TQ, W_Q_PAD), BF16) + _nbytes((TQ, SLOT), F32)
              + 2 * _nbytes((n_cmp_pad, W_KV_PAD), BF16) + 4 * _nbytes((seq_len, W_KV_PAD), BF16)
              + _nbytes(ov.shape, F32) + _nbytes((TQ, Q_WIDTH), BF16))
    assert CHAIN_COLS % TQ == 0 and (GROUP * TQ) % CHAIN_COLS == 0
    n_chains = N_KV * GROUP * TQ // CHAIN_COLS
    scratch_shapes = (
        [pltpu.VMEM((N_KV, GROUP * TQ, SLOT), BF16),
         pltpu.VMEM((N_HEADS, SLOT, TQ), F32)]
        + [pltpu.VMEM((1, CHAIN_COLS), F32)] * n_chains
        + [pltpu.VMEM((SLOT, CHAIN_COLS), F32)] * n_chains)
    scratch = sum(_nbytes(s.shape, s.dtype) for s in scratch_shapes)
    return pl.pallas_call(
        functools.partial(_nsa_kernel, n_slc=n_slc),
        grid=(batch, nq),
        in_specs=[
            pl.BlockSpec((TQ, W_Q_PAD), qrow),
            pl.BlockSpec((TQ, SLOT), qrow),
            pl.BlockSpec((1, n_cmp_pad, W_KV_PAD), per_b),
            pl.BlockSpec((1, n_cmp_pad, W_KV_PAD), per_b),
            pl.BlockSpec((1, seq_len, W_KV_PAD), per_b),
            pl.BlockSpec((n_kt, W_KV_PAD, KT), per_b),
            pl.BlockSpec((1, seq_len, W_KV_PAD), per_b),
            pl.BlockSpec((n_kt, W_KV_PAD, KT), per_b),
            pl.BlockSpec(ov.shape, fixed),
        ],
        out_specs=pl.BlockSpec((TQ, Q_WIDTH), qrow),
        out_shape=jax.ShapeDtypeStruct((batch * seq_len, Q_WIDTH), BF16),
        scratch_shapes=scratch_shapes,
        compiler_params=pltpu.CompilerParams(
            dimension_semantics=("arbitrary", "arbitrary"),
            vmem_limit_bytes=_vmem_limit(blocks, scratch)),
        name="nsa",
    )(q, gate, kc.reshape(batch, n_cmp_pad, W_KV_PAD), vc.reshape(batch, n_cmp_pad, W_KV_PAD),
      kv3(ks), vst, kv3(kw), vwt, ov)


def _pool_kernel(u_ref, w_ref, scale_ref, o_ref):
    seq = u_ref.shape[0]
    t = lax.broadcasted_iota(jnp.int32, (seq, POOL_GROUP), 0)

    def shifted(x, k):
        return jnp.where(t >= k, pltpu.roll(x, k, axis=0), 0.0)

    for gi, w in enumerate(POOL_WINDOWS):
        sl = slice(gi * POOL_GROUP, (gi + 1) * POOL_GROUP)
        x = u_ref[:, sl]
        wsum = x
        span = 1
        while span < w:
            wsum = wsum + shifted(wsum, span)
            span *= 2
        cnt = jnp.minimum(t + 1, w).astype(F32)
        pooled = (wsum / cnt - x).astype(BF16)
        o_ref[:, sl] = (_dot(pooled, w_ref[gi]) * scale_ref[:, sl]).astype(BF16)


def _pool(pool_in, pool_w, pool_scale, batch, seq_len):
    for w in POOL_WINDOWS:
        assert w & (w - 1) == 0
    blocks = (_nbytes((seq_len, POOL_WIDTH), F32) + _nbytes(pool_w.shape, BF16)
              + _nbytes((seq_len, POOL_WIDTH), BF16))
    return pl.pallas_call(
        _pool_kernel,
        grid=(batch,),
        in_specs=[
            pl.BlockSpec((seq_len, POOL_WIDTH), lambda b: (b, 0)),
            pl.BlockSpec(pool_w.shape, lambda b: (0, 0, 0)),
            pl.BlockSpec((1, POOL_WIDTH), lambda b: (0, 0)),
        ],
        out_specs=pl.BlockSpec((seq_len, POOL_WIDTH), lambda b: (b, 0)),
        out_shape=jax.ShapeDtypeStruct((batch * seq_len, POOL_WIDTH), BF16),
        compiler_params=pltpu.CompilerParams(
            dimension_semantics=("arbitrary",), vmem_limit_bytes=_vmem_limit(blocks)),
        name="pool",
    )(pool_in, pool_w.astype(BF16), pool_scale.reshape(1, -1))


def _merge_kernel(x_ref, gpre_ref, wm_ref, nsa_ref, wa_ref, pool_ref, wp_ref, wo_ref, gpost_ref,
                  o_ref):
    x = x_ref[...]
    hn = _rms(x, gpre_ref[...]).astype(BF16)
    g_attn = jax.nn.sigmoid(_dot(hn, wm_ref[:, 0:D_MODEL]))
    g_pool = jax.nn.sigmoid(_dot(hn, wm_ref[:, D_MODEL:2 * D_MODEL]))
    y = g_attn * _dot(nsa_ref[...], wa_ref[...]) + g_pool * _dot(pool_ref[...], wp_ref[...])
    h = _dot(y.astype(BF16), wo_ref[...])
    o_ref[...] = x + _rms(h, gpost_ref[...])


def _merge(x1, g_pre, w_merge, o_nsa, w_attn, o_pool, w_pool, w_out, g_post):
    t = x1.shape[0]
    row = lambda i: (i, 0)
    fixed = lambda i: (0, 0)
    blocks = (2 * _nbytes((TOK_TILE, D_MODEL), F32) + _nbytes((D_MODEL, 2 * D_MODEL), BF16)
              + _nbytes((TOK_TILE, Q_WIDTH), BF16) + _nbytes((Q_WIDTH, D_MODEL), BF16)
              + _nbytes((TOK_TILE, POOL_WIDTH), BF16) + _nbytes((POOL_WIDTH, D_MODEL), BF16)
              + _nbytes((D_MODEL, D_MODEL), BF16))
    return pl.pallas_call(
        _merge_kernel,
        grid=(t // TOK_TILE,),
        in_specs=[
            pl.BlockSpec((TOK_TILE, D_MODEL), row),
            pl.BlockSpec((1, D_MODEL), fixed),
            pl.BlockSpec((D_MODEL, 2 * D_MODEL), fixed),
            pl.BlockSpec((TOK_TILE, Q_WIDTH), row),
            pl.BlockSpec((Q_WIDTH, D_MODEL), fixed),
            pl.BlockSpec((TOK_TILE, POOL_WIDTH), row),
            pl.BlockSpec((POOL_WIDTH, D_MODEL), fixed),
            pl.BlockSpec((D_MODEL, D_MODEL), fixed),
            pl.BlockSpec((1, D_MODEL), fixed),
        ],
        out_specs=pl.BlockSpec((TOK_TILE, D_MODEL), row),
        out_shape=jax.ShapeDtypeStruct((t, D_MODEL), F32),
        compiler_params=pltpu.CompilerParams(
            dimension_semantics=("arbitrary",), vmem_limit_bytes=_vmem_limit(blocks)),
        name="merge",
    )(x1, g_pre.reshape(1, -1), w_merge.astype(BF16), o_nsa, w_attn.astype(BF16), o_pool,
      w_pool.astype(BF16), w_out.astype(BF16), g_post.reshape(1, -1))


def _layer(x, tab, tab_cmp, batch, seq_len, g_ffn1_pre, w_ffn1_gate, w_ffn1_up, w_ffn1_down,
           g_ffn1_post, g_mix_pre, w_in, cmp_pe_k, cmp_w1_k, cmp_w2_k, cmp_pe_v, cmp_w1_v, cmp_w2_v,
           w_attn_branch, pool_w, pool_scale, w_pool_branch, w_out, g_mix_post, g_ffn2_pre,
           w_ffn2_gate, w_ffn2_up, w_ffn2_down, g_ffn2_post):
    x1 = _ffn(x, g_ffn1_pre, w_ffn1_gate, w_ffn1_up, w_ffn1_down, g_ffn1_post)
    (q, ks, kw, vst, vwt, cmp_in, gate, pool_in), w_merge = _in_proj(x1, g_mix_pre, w_in, tab, seq_len)
    kc, vc = _compress(cmp_in, cmp_pe_k, cmp_w1_k, cmp_w2_k, cmp_pe_v, cmp_w1_v, cmp_w2_v, tab_cmp)
    n_cmp_pad = seq_len // CMP_STRIDE
    kc = kc.reshape(batch, n_cmp_pad, W_KV_PAD)
    vc = vc.reshape(batch, n_cmp_pad, W_KV_PAD)
    o_nsa = _nsa(q, gate, kc, vc, ks, vst, kw, vwt, batch, seq_len)
    o_pool = _pool(pool_in, pool_w, pool_scale, batch, seq_len)
    x2 = _merge(x1, g_mix_pre, w_merge, o_nsa, w_attn_branch, o_pool, w_pool_branch, w_out,
                g_mix_post)
    return _ffn(x2, g_ffn2_pre, w_ffn2_gate, w_ffn2_up, w_ffn2_down, g_ffn2_post)


def kernel(x, positions, g_ffn1_pre, w_ffn1_gate, w_ffn1_up, w_ffn1_down, g_ffn1_post, g_mix_pre, w_in, cmp_pe_k, cmp_w1_k, cmp_w2_k, cmp_pe_v, cmp_w1_v, cmp_w2_v, w_attn_branch, pool_w, pool_scale, w_pool_branch, w_out, g_mix_post, g_ffn2_pre, w_ffn2_gate, w_ffn2_up, w_ffn2_down, g_ffn2_post):
    batch, seq_len, d_model = x.shape
    assert d_model == D_MODEL and seq_len % TOK_TILE == 0 and seq_len % TQ == 0
    assert TQ == KT and TQ % SEL_LEN == 0 and seq_len % CMP_STRIDE == 0
    t = batch * seq_len
    tab = _rope_tables(positions.reshape(t))
    n_cmp_pad = seq_len // CMP_STRIDE
    last_tok = jnp.minimum(jnp.arange(n_cmp_pad) * CMP_STRIDE + CMP_LEN - 1, seq_len - 1)
    tab_cmp = tab.reshape(batch, seq_len, -1)[:, last_tok].reshape(batch * n_cmp_pad, -1)
    xf = x.reshape(t, D_MODEL)
    per_layer = (g_ffn1_pre, w_ffn1_gate, w_ffn1_up, w_ffn1_down, g_ffn1_post, g_mix_pre, w_in,
                 cmp_pe_k, cmp_w1_k, cmp_w2_k, cmp_pe_v, cmp_w1_v, cmp_w2_v, w_attn_branch, pool_w,
                 pool_scale, w_pool_branch, w_out, g_mix_post, g_ffn2_pre, w_ffn2_gate, w_ffn2_up,
                 w_ffn2_down, g_ffn2_post)
    for l in range(g_ffn1_pre.shape[0]):
        xf = _layer(xf, tab, tab_cmp, batch, seq_len, *(p[l] for p in per_layer))
    return xf.reshape(batch, seq_len, D_MODEL)
```

```python
import functools
import math

import jax
import jax.numpy as jnp
from jax import lax
from jax.experimental import pallas as pl
from jax.experimental.pallas import tpu as pltpu

F32 = jnp.float32
BF16 = jnp.bfloat16

D_MODEL = 1024
N_HEADS = 16
HEAD_DIM = 64
N_KV = 4
GROUP = N_HEADS // N_KV
ROT_DIM = HEAD_DIM // 4
ROT_HALF = ROT_DIM // 2
ROPE_THETA = 500000.0
CMP_LEN = 32
CMP_STRIDE = 16
CMP_HIDDEN = 2 * HEAD_DIM
SEL_LEN = 64
SEL_SHIFT = SEL_LEN.bit_length() - 1
N_SEL = 8
WINDOW = 512
POOL_WINDOWS = (2, 4, 8, 16)
POOL_WIDTH = D_MODEL // 2
POOL_GROUP = POOL_WIDTH // len(POOL_WINDOWS)
D_FF = 2816
EPS = 1e-6
NEG_INF = -1e30
FORCE_SCORE = 1e4
Q_WIDTH = N_HEADS * HEAD_DIM
KV_WIDTH = N_KV * HEAD_DIM
N_GATES = 3 * N_HEADS
LOG2_E = math.log2(math.e)

LANES = 128
BF16_SUBLANES = 16
V7X_VMEM_BYTES = 64 * 1024 * 1024

SLOT = LANES
BLK_LANE0 = HEAD_DIM
VT_ROWS = HEAD_DIM + BF16_SUBLANES
ONE_ROW = HEAD_DIM
TOK_TILE = 512
TQ = 256
KT = 256
CHAIN_COLS = 256
PIPE_DEPTH = 3
FF_CHUNK = 256
N_ROPE_TAB = 2


def _vmem_limit(block_bytes, scratch_bytes=0):
    need = 2 * block_bytes + scratch_bytes
    return int(min(V7X_VMEM_BYTES - (4 << 20), max(2 * need, 32 << 20)))


def _nbytes(shape, dtype):
    n = 1
    for s in shape:
        n *= s
    return n * jnp.dtype(dtype).itemsize


def _rms(xf, g):
    return xf * lax.rsqrt(jnp.mean(xf * xf, axis=-1, keepdims=True) + EPS) * g


def _dot(a, b):
    return jnp.dot(a, b, preferred_element_type=F32)


def _dot_nt(a, b):
    return lax.dot_general(a, b, (((1,), (1,)), ((), ())), preferred_element_type=F32)


def _rope_trig_kernel(pos_ref, inv_ref, cos_ref, sin_ref):
    ang = pos_ref[...].astype(F32) * inv_ref[...]
    cos_ref[...] = jnp.cos(ang)
    sin_ref[...] = jnp.sin(ang)


def _rope_tables(pos_flat):
    n = pos_flat.shape[0]
    inv = ROPE_THETA ** (-jnp.arange(ROT_HALF, dtype=F32) * (2.0 / ROT_DIM))
    cos8, sin8 = pl.pallas_call(
        _rope_trig_kernel,
        out_shape=(jax.ShapeDtypeStruct((ROT_HALF, n), F32),) * 2,
        name="rope_trig",
    )(pos_flat.reshape(1, n), inv.reshape(ROT_HALF, 1))
    cos8, sin8 = cos8.T, sin8.T
    rest = HEAD_DIM - ROT_DIM
    cos_h = jnp.concatenate([cos8, cos8, jnp.ones((n, rest), F32)], axis=1)
    sin_h = jnp.concatenate([-sin8, sin8, jnp.zeros((n, rest), F32)], axis=1)
    reps = SLOT // HEAD_DIM
    return jnp.concatenate([cos_h] * reps + [sin_h] * reps, axis=1)


def _rope_slot(y, tab):
    cos_t = tab[:, 0:SLOT]
    sin_t = tab[:, SLOT:2 * SLOT]
    lane = lax.broadcasted_iota(jnp.int32, y.shape, 1) & (HEAD_DIM - 1)
    up = pltpu.roll(y, SLOT - ROT_HALF, axis=1)
    down = pltpu.roll(y, ROT_HALF, axis=1)
    return y * cos_t + jnp.where(lane < ROT_HALF, up, down) * sin_t


def _ffn_kernel(x_ref, gpre_ref, wg_ref, wu_ref, wd_ref, gpost_ref, o_ref, acc_ref):
    x = x_ref[...]
    hn = _rms(x, gpre_ref[...]).astype(BF16)
    for c in range(D_FF // FF_CHUNK):
        sl = slice(c * FF_CHUNK, (c + 1) * FF_CHUNK)
        g = _dot(hn, wg_ref[:, sl])
        u = _dot(hn, wu_ref[:, sl])
        a = (g * jax.nn.sigmoid(g) * u).astype(BF16)
        d = _dot(a, wd_ref[sl, :])
        if c == 0:
            acc_ref[...] = d
        else:
            acc_ref[...] += d
    o_ref[...] = x + 0.5 * _rms(acc_ref[...], gpost_ref[...])


def _ffn(x, g_pre, w_gate, w_up, w_down, g_post):
    t = x.shape[0]
    row = lambda i: (i, 0)
    fixed = lambda i: (0, 0)
    blocks = (2 * _nbytes((TOK_TILE, D_MODEL), F32) + 3 * _nbytes((D_MODEL, D_FF), BF16))
    return pl.pallas_call(
        _ffn_kernel,
        grid=(t // TOK_TILE,),
        in_specs=[
            pl.BlockSpec((TOK_TILE, D_MODEL), row),
            pl.BlockSpec((1, D_MODEL), fixed),
            pl.BlockSpec((D_MODEL, D_FF), fixed),
            pl.BlockSpec((D_MODEL, D_FF), fixed),
            pl.BlockSpec((D_FF, D_MODEL), fixed),
            pl.BlockSpec((1, D_MODEL), fixed),
        ],
        out_specs=pl.BlockSpec((TOK_TILE, D_MODEL), row),
        out_shape=jax.ShapeDtypeStruct((t, D_MODEL), F32),
        scratch_shapes=[pltpu.VMEM((TOK_TILE, D_MODEL), F32)],
        compiler_params=pltpu.CompilerParams(
            dimension_semantics=("arbitrary",),
            vmem_limit_bytes=_vmem_limit(blocks, _nbytes((TOK_TILE, D_MODEL), F32))),
        name="ffn",
    )(x, g_pre.reshape(1, -1), w_gate.astype(BF16), w_up.astype(BF16), w_down.astype(BF16),
      g_post.reshape(1, -1))


OFF_Q = 0
OFF_CMP = OFF_Q + Q_WIDTH
OFF_KS = OFF_CMP + 2 * KV_WIDTH
OFF_VS = OFF_KS + KV_WIDTH
OFF_KW = OFF_VS + KV_WIDTH
OFF_VW = OFF_KW + KV_WIDTH
OFF_GATE = OFF_VW + KV_WIDTH
OFF_POOL = OFF_GATE + N_GATES
OFF_MERGE = OFF_POOL + POOL_WIDTH
W_CMP = 2 * KV_WIDTH
W_GP = SLOT + POOL_WIDTH
CHUNK_TOK = CMP_STRIDE
VT_SLAB = N_KV * VT_ROWS


def _in_proj_kernel(x_ref, g_ref, w_ref, wgp_ref, wvt_ref, tab_ref, q_ref, ks_ref, kw_ref, vst_ref,
                    vwt_ref, cmp_ref, gate_ref, pool_ref, cmp_scr):
    hn = _rms(x_ref[...], g_ref[...]).astype(BF16)
    tab = tab_ref[...]
    rows = x_ref.shape[0]
    wide = 4 * SLOT
    tab_q = tab * (HEAD_DIM ** -0.5 * LOG2_E)
    for c in range(Q_WIDTH // wide):
        y = _dot(hn, w_ref[:, OFF_Q + c * wide:OFF_Q + (c + 1) * wide])
        for j in range(wide // SLOT):
            sl = slice(c * wide + j * SLOT, c * wide + (j + 1) * SLOT)
            q_ref[:, sl] = _rope_slot(y[:, j * SLOT:(j + 1) * SLOT], tab_q).astype(BF16)
    for k_ref, off in ((ks_ref, OFF_KS), (kw_ref, OFF_KW)):
        y = _dot(hn, w_ref[:, off:off + KV_WIDTH])
        for j in range(KV_WIDTH // SLOT):
            sl = slice(j * SLOT, (j + 1) * SLOT)
            k_ref[:, sl] = _rope_slot(y[:, sl], tab).astype(BF16)
    y = _dot(hn, w_ref[:, OFF_CMP:OFF_CMP + W_CMP])
    for j in range(W_CMP // LANES):
        cmp_scr[j] = y[:, j * LANES:(j + 1) * LANES]
    for l in range(CHUNK_TOK):
        for j in range(W_CMP // LANES):
            c0 = l * W_CMP + j * LANES
            cmp_ref[:, c0:c0 + LANES] = (
                cmp_scr[j, pl.ds(l, rows // CHUNK_TOK, stride=CHUNK_TOK), :].astype(BF16))
    y = _dot(hn, wgp_ref[...])
    gate_ref[...] = y[:, 0:SLOT]
    pool_ref[...] = y[:, SLOT:W_GP]
    aux = (lax.broadcasted_iota(jnp.int32, (VT_ROWS - HEAD_DIM, KT), 0) == 0).astype(BF16)
    for vt_ref, r0 in ((vst_ref, 0), (vwt_ref, KV_WIDTH)):
        vt = _dot_nt(wvt_ref[r0:r0 + KV_WIDTH, :], hn).astype(BF16)
        for j in range(rows // KT):
            for g in range(N_KV):
                vt_ref[j, g * VT_ROWS:g * VT_ROWS + HEAD_DIM, :] = (
                    vt[g * HEAD_DIM:(g + 1) * HEAD_DIM, j * KT:(j + 1) * KT])
                vt_ref[j, g * VT_ROWS + HEAD_DIM:(g + 1) * VT_ROWS, :] = aux


def _in_proj(x1, g_pre, w_in, tab):
    t = x1.shape[0]
    w_main = w_in[:, :OFF_GATE].astype(BF16)
    w_gp = jnp.concatenate([jnp.pad(w_in[:, OFF_GATE:OFF_POOL], ((0, 0), (0, SLOT - N_GATES))),
                            w_in[:, OFF_POOL:OFF_MERGE]], axis=1).astype(BF16)
    w_vt = jnp.concatenate([w_in[:, OFF_VS:OFF_VS + KV_WIDTH], w_in[:, OFF_VW:OFF_VW + KV_WIDTH]],
                           axis=1).T.astype(BF16)
    row = lambda i: (i, 0)
    fixed = lambda i: (0, 0)
    tiles = TOK_TILE // KT
    chunk_rows = TOK_TILE // CHUNK_TOK
    slab = pl.BlockSpec((tiles, VT_SLAB, KT), lambda i: (i, 0, 0))
    slab_shape = jax.ShapeDtypeStruct((t // KT, VT_SLAB, KT), BF16)
    out_specs = [
        pl.BlockSpec((TOK_TILE, Q_WIDTH), row), pl.BlockSpec((TOK_TILE, KV_WIDTH), row),
        pl.BlockSpec((TOK_TILE, KV_WIDTH), row), slab, slab,
        pl.BlockSpec((chunk_rows, CHUNK_TOK * W_CMP), row),
        pl.BlockSpec((TOK_TILE, SLOT), row), pl.BlockSpec((TOK_TILE, POOL_WIDTH), row)]
    out_shape = [
        jax.ShapeDtypeStruct((t, Q_WIDTH), BF16), jax.ShapeDtypeStruct((t, KV_WIDTH), BF16),
        jax.ShapeDtypeStruct((t, KV_WIDTH), BF16), slab_shape, slab_shape,
        jax.ShapeDtypeStruct((t // CHUNK_TOK, CHUNK_TOK * W_CMP), BF16),
        jax.ShapeDtypeStruct((t, SLOT), F32), jax.ShapeDtypeStruct((t, POOL_WIDTH), F32)]
    blocks = (_nbytes((TOK_TILE, D_MODEL), F32) + _nbytes(w_main.shape, BF16)
              + _nbytes(w_gp.shape, BF16) + _nbytes(w_vt.shape, BF16)
              + _nbytes((TOK_TILE, N_ROPE_TAB * SLOT), F32)
              + _nbytes((TOK_TILE, Q_WIDTH + 2 * KV_WIDTH + W_CMP), BF16)
              + 2 * _nbytes((tiles, VT_SLAB, KT), BF16) + _nbytes((TOK_TILE, SLOT + POOL_WIDTH), F32))
    res = pl.pallas_call(
        _in_proj_kernel,
        grid=(t // TOK_TILE,),
        in_specs=[
            pl.BlockSpec((TOK_TILE, D_MODEL), row),
            pl.BlockSpec((1, D_MODEL), fixed),
            pl.BlockSpec(w_main.shape, fixed),
            pl.BlockSpec(w_gp.shape, fixed),
            pl.BlockSpec(w_vt.shape, fixed),
            pl.BlockSpec((TOK_TILE, N_ROPE_TAB * SLOT), row),
        ],
        out_specs=out_specs,
        out_shape=out_shape,
        scratch_shapes=[pltpu.VMEM((W_CMP // LANES, TOK_TILE, LANES), F32)],
        compiler_params=pltpu.CompilerParams(
            dimension_semantics=("arbitrary",),
            vmem_limit_bytes=_vmem_limit(blocks, _nbytes((TOK_TILE, W_CMP), F32))),
        name="in_proj",
    )(x1, g_pre.reshape(1, -1), w_main, w_gp, w_vt, tab)
    return res, w_in[:, OFF_MERGE:]


def _compress_kernel(x_ref, w1k_ref, w1v_ref, pek_ref, pev_ref, w1k_raw_ref, w1v_raw_ref,
                     w2k_ref, w2v_ref, tab_ref, kc_ref, vc_ref, acck_ref, accv_ref):
    l = pl.program_id(0)

    @pl.when(l == 0)
    def _():
        acck_ref[...] = jnp.zeros(acck_ref.shape, F32)
        accv_ref[...] = jnp.zeros(accv_ref.shape, F32)

    for acc_ref, w1_ref, c0 in ((acck_ref, w1k_ref, 0), (accv_ref, w1v_ref, KV_WIDTH)):
        for g in range(N_KV):
            acc_ref[g] += _dot(x_ref[:, c0 + g * HEAD_DIM:c0 + (g + 1) * HEAD_DIM], w1_ref[0])

    @pl.when(l == CHUNK_TOK - 1)
    def _():
        rows = acck_ref.shape[1]
        tab = tab_ref[...]
        for acc_ref, pe_ref, w1_raw_ref, w2_ref, o_ref, rope in (
                (acck_ref, pek_ref, w1k_raw_ref, w2k_ref, kc_ref, True),
                (accv_ref, pev_ref, w1v_raw_ref, w2v_ref, vc_ref, False)):
            bias = _dot(pe_ref[...], w1_raw_ref[...])[0:1, :]
            for g in range(N_KV):
                first = acc_ref[g, :, 0:CMP_HIDDEN]
                second = pltpu.roll(acc_ref[g, :, CMP_HIDDEN:2 * CMP_HIDDEN], rows - 1, axis=0)
                hid = jax.nn.gelu(first + second + bias).astype(BF16)
                out = _dot(hid, w2_ref[...])
                if rope:
                    out = _rope_slot(out, tab)
                o_ref[:, g * SLOT:(g + 1) * SLOT] = out.astype(BF16)


def _compress_weights(w1, w2):
    w1 = w1.reshape(2, CHUNK_TOK, HEAD_DIM, CMP_HIDDEN)
    w1_tok = jnp.concatenate([w1[0], w1[1]], axis=-1).astype(BF16)
    return w1_tok, jnp.pad(w2, ((0, 0), (0, SLOT - HEAD_DIM))).astype(BF16)


def _compress(cmp_rows, cmp_pe_k, cmp_w1_k, cmp_w2_k, cmp_pe_v, cmp_w1_v, cmp_w2_v, tab_cmp):
    rows = cmp_rows.shape[0]
    w1k, w2k = _compress_weights(cmp_w1_k, cmp_w2_k)
    w1v, w2v = _compress_weights(cmp_w1_v, cmp_w2_v)
    sub = 8
    pek = jnp.broadcast_to(cmp_pe_k.reshape(1, -1), (sub, CMP_LEN * HEAD_DIM)).astype(BF16)
    pev = jnp.broadcast_to(cmp_pe_v.reshape(1, -1), (sub, CMP_LEN * HEAD_DIM)).astype(BF16)
    fixed = lambda l: (0, 0)
    w_kv_pad = N_KV * SLOT
    blocks = (_nbytes((rows, W_CMP), BF16) + 2 * _nbytes((HEAD_DIM, 2 * CMP_HIDDEN), BF16)
              + 2 * _nbytes((sub + CMP_LEN * HEAD_DIM, CMP_HIDDEN), BF16)
              + 2 * _nbytes((CMP_HIDDEN, SLOT), BF16)
              + _nbytes((rows, N_ROPE_TAB * SLOT), F32) + 2 * _nbytes((rows, w_kv_pad), BF16))
    acc = pltpu.VMEM((N_KV, rows, 2 * CMP_HIDDEN), F32)
    return pl.pallas_call(
        _compress_kernel,
        grid=(CHUNK_TOK,),
        in_specs=[
            pl.BlockSpec((rows, W_CMP), lambda l: (0, l)),
            pl.BlockSpec((1, HEAD_DIM, 2 * CMP_HIDDEN), lambda l: (l, 0, 0)),
            pl.BlockSpec((1, HEAD_DIM, 2 * CMP_HIDDEN), lambda l: (l, 0, 0)),
            pl.BlockSpec((sub, CMP_LEN * HEAD_DIM), fixed),
            pl.BlockSpec((sub, CMP_LEN * HEAD_DIM), fixed),
            pl.BlockSpec((CMP_LEN * HEAD_DIM, CMP_HIDDEN), fixed),
            pl.BlockSpec((CMP_LEN * HEAD_DIM, CMP_HIDDEN), fixed),
            pl.BlockSpec((CMP_HIDDEN, SLOT), fixed),
            pl.BlockSpec((CMP_HIDDEN, SLOT), fixed),
            pl.BlockSpec((rows, N_ROPE_TAB * SLOT), fixed),
        ],
        out_specs=[pl.BlockSpec((rows, w_kv_pad), fixed)] * 2,
        out_shape=[jax.ShapeDtypeStruct((rows, w_kv_pad), BF16)] * 2,
        scratch_shapes=[acc, acc],
        compiler_params=pltpu.CompilerParams(
            dimension_semantics=("arbitrary",),
            vmem_limit_bytes=_vmem_limit(blocks, 2 * _nbytes(acc.shape, F32))),
        name="compress",
    )(cmp_rows, w1k, w1v, pek, pev, cmp_w1_k.astype(BF16), cmp_w1_v.astype(BF16), w2k, w2v, tab_cmp)


def _nsa_kernel(q_ref, gate_ref, kc_ref, vc_ref, ks_ref, vst_ref, kw_ref, vwt_ref, ov_ref,
                o_ref, ksp_ref, kwp_ref, qg_ref, mix_ref, *state_refs, n_slc):
    i = pl.program_id(1)
    start = i * TQ
    seq = ks_ref.shape[1]
    cols_g = GROUP * TQ
    chains_g = cols_g // CHAIN_COLS
    n_chains = N_KV * chains_g
    n_cmp = kc_ref.shape[1]
    sel_state = (state_refs[0:n_chains], state_refs[n_chains:2 * n_chains])
    win_state = (state_refs[2 * n_chains:3 * n_chains], state_refs[3 * n_chains:4 * n_chains])
    lane_q = lax.broadcasted_iota(jnp.int32, (TQ, SLOT), 1)

    def head_slot(x_ref, rows, j):
        two = x_ref[rows, (j // 2) * SLOT:(j // 2 + 1) * SLOT].astype(F32)
        return pltpu.roll(two, HEAD_DIM, axis=1) if j % 2 else two

    @pl.when(i == 0)
    def _():
        for c in range(seq // KT):
            rows = pl.ds(c * KT, KT)
            lane = lax.broadcasted_iota(jnp.int32, (KT, SLOT), 1)
            key_blk = (c * KT + lax.broadcasted_iota(jnp.int32, (KT, SLOT), 0)) >> SEL_SHIFT
            onehot = (lane - BLK_LANE0 == key_blk).astype(F32)
            for g in range(N_KV):
                gs = slice(g * SLOT, (g + 1) * SLOT)
                ksp_ref[rows, gs] = jnp.where(lane < HEAD_DIM, head_slot(ks_ref.at[0], rows, g),
                                              onehot).astype(BF16)
                kwp_ref[rows, gs] = jnp.where(lane < HEAD_DIM, head_slot(kw_ref.at[0], rows, g),
                                              0.0).astype(BF16)

    for h in range(N_HEADS):
        g, r = divmod(h, GROUP)
        qg_ref[g, r * TQ:(r + 1) * TQ, :] = jnp.where(
            lane_q < HEAD_DIM, head_slot(q_ref, slice(None), h), 0.0).astype(BF16)

    gates_t = jax.nn.sigmoid(gate_ref[...]).T
    t_cmp = start + lax.broadcasted_iota(jnp.int32, (n_cmp, TQ), 1)
    n_idx = lax.broadcasted_iota(jnp.int32, (n_cmp, TQ), 0)
    cmp_valid = n_idx * CMP_STRIDE + (CMP_LEN - 1) <= t_cmp
    t_row = start + lax.broadcasted_iota(jnp.int32, (n_slc, TQ), 1)
    blk = lax.broadcasted_iota(jnp.int32, (n_slc, TQ), 0)
    forced = (blk == t_row >> SEL_SHIFT) | (blk == 0)
    causal_blk = blk * SEL_LEN <= t_row
    key_in = lax.broadcasted_iota(jnp.int32, (KT, CHAIN_COLS), 0)
    qry_in = lax.broadcasted_iota(jnp.int32, (KT, CHAIN_COLS), 1) & (TQ - 1)
    diag_mask = key_in <= qry_in
    far_mask = key_in > qry_in

    def gate_row(branch, h):
        c = branch * N_HEADS + h
        return gates_t[c:c + 1, :]

    cmp_scores = []
    for h in range(N_HEADS):
        g, r = divmod(h, GROUP)
        cmp_scores.append(_dot_nt(kc_ref[0, :, g * SLOT:(g + 1) * SLOT],
                                  qg_ref[g, r * TQ:(r + 1) * TQ, :]))
    for g in range(N_KV):
        vct_g = vc_ref[0, :, g * SLOT:(g + 1) * SLOT].astype(F32).T[0:HEAD_DIM].astype(BF16)
        p_sum = jnp.zeros((n_cmp, TQ), F32)
        for r in range(GROUP):
            h = g * GROUP + r
            s = jnp.where(cmp_valid, cmp_scores[h], NEG_INF)
            p = jnp.exp2(s - jnp.max(s, axis=0, keepdims=True))
            p = jnp.where(cmp_valid, p, 0.0)
            den = jnp.sum(p, axis=0, keepdims=True)
            p = p / jnp.where(den > 0.0, den, 1.0)
            p_sum = p_sum + p
            mix_ref[h] = gate_row(0, h) * _dot(vct_g, p.astype(BF16))
        imp = jnp.dot(ov_ref[...], p_sum, precision=lax.Precision.HIGHEST,
                      preferred_element_type=F32)
        score = jnp.where(causal_blk, jnp.where(forced, FORCE_SCORE, imp), NEG_INF)
        rank = jnp.zeros((n_slc, TQ), jnp.int32)
        for c in range(n_slc):
            other = score[c:c + 1, :]
            beats = (other > score) | ((other == score) & (c < blk))
            rank = rank + beats.astype(jnp.int32)
        selected = causal_blk & (rank < N_SEL)
        bias = jnp.where(selected, 0.0, NEG_INF)
        bias_t = jnp.concatenate([jnp.zeros((BLK_LANE0, TQ), F32), bias,
                                  jnp.zeros((SLOT - BLK_LANE0 - n_slc, TQ), F32)], axis=0)
        bias_q = bias_t.T.astype(BF16)
        for r in range(GROUP):
            rs = slice(r * TQ, (r + 1) * TQ)
            qg_ref[g, rs, :] = jnp.where(lane_q < HEAD_DIM, qg_ref[g, rs, :], bias_q)

    def flash_init(state):
        for m_ref, acc_ref in zip(*state):
            m_ref[...] = jnp.full(m_ref.shape, NEG_INF, F32)
            acc_ref[...] = jnp.zeros(acc_ref.shape, F32)

    def flash_tiles(tiles):
        jobs = [(tile, c) for tile in tiles for c in range(n_chains)]
        scores = {}
        for step in range(len(jobs) + PIPE_DEPTH):
            if step < len(jobs):
                (kp_ref, _, kt, mask, _), chain = jobs[step]
                g, c = divmod(chain, chains_g)
                off = pl.multiple_of(kt * KT, KT)
                s = _dot_nt(kp_ref[pl.ds(off, KT), g * SLOT:(g + 1) * SLOT],
                            qg_ref[g, c * CHAIN_COLS:(c + 1) * CHAIN_COLS, :])
                scores[step] = s if mask is None else jnp.where(mask, s, NEG_INF)
            done = step - PIPE_DEPTH
            if done >= 0:
                (_, vt_ref, kt, _, (m_refs, acc_refs)), chain = jobs[done]
                g = chain // chains_g
                m_ref, acc_ref = m_refs[chain], acc_refs[chain]
                s = scores.pop(done)
                m_old = m_ref[...]
                m_new = jnp.maximum(m_old, jnp.max(s, axis=0, keepdims=True))
                p = jnp.exp2(s - m_new).astype(BF16)
                pv = _dot(vt_ref[kt, g * VT_ROWS:(g + 1) * VT_ROWS, :], p)
                acc_ref[...] = jnp.exp2(m_old - m_new) * acc_ref[...] + pv
                m_ref[...] = m_new

    def flash_mix(branch, state):
        _, acc_refs = state
        for h in range(N_HEADS):
            g, r = divmod(h, GROUP)
            c, c0 = divmod(r * TQ, CHAIN_COLS)
            acc = acc_refs[g * chains_g + c][:, c0:c0 + TQ]
            scale = gate_row(branch, h) * (1.0 / acc[ONE_ROW:ONE_ROW + 1, :])
            mix_ref[h] = mix_ref[h] + scale * acc[0:HEAD_DIM]

    flash_init(sel_state)
    flash_init(win_state)

    def sel_body(kt, carry):
        flash_tiles([(ksp_ref, vst_ref, kt, None, sel_state)])
        return carry

    lax.fori_loop(0, i, sel_body, 0)

    n_back = WINDOW // KT
    sel_diag = (ksp_ref, vst_ref, i, diag_mask, sel_state)
    win_diag = (kwp_ref, vwt_ref, i, diag_mask, win_state)

    def win_back(d):
        return (kwp_ref, vwt_ref, i - d, far_mask if d == n_back else None, win_state)

    for have in range(n_back + 1):
        cond = (i == have) if have < n_back else (i >= have)

        @pl.when(cond)
        def _(have=have):
            flash_tiles([sel_diag] + [win_back(d) for d in range(have, 0, -1)] + [win_diag])

    flash_mix(1, sel_state)
    flash_mix(2, win_state)
    for h2 in range(N_HEADS // 2):
        pair = jnp.concatenate([mix_ref[2 * h2], mix_ref[2 * h2 + 1]], axis=0)
        o_ref[:, h2 * SLOT:(h2 + 1) * SLOT] = pair.T.astype(BF16)


def _overlap(n_cmp_pad, n_slc):
    c0 = jnp.arange(n_cmp_pad) * CMP_STRIDE
    s0 = jnp.arange(n_slc) * SEL_LEN
    ov = jnp.minimum(c0[None, :] + CMP_LEN, s0[:, None] + SEL_LEN) - jnp.maximum(c0[None, :], s0[:, None])
    return jnp.clip(ov, 0).astype(F32) / CMP_LEN


def _nsa(q, gate, kc, vc, ks, vst, kw, vwt, batch, seq_len):
    n_slc = seq_len // SEL_LEN
    n_cmp_pad = kc.shape[1]
    w_kv_pad = N_KV * SLOT
    assert n_slc <= SLOT - BLK_LANE0 and n_slc % 8 == 0
    assert CHAIN_COLS % TQ == 0 and (GROUP * TQ) % CHAIN_COLS == 0
    ov = _overlap(n_cmp_pad, n_slc)
    nq = seq_len // TQ
    n_kt = seq_len // KT
    qrow = lambda b, i: (b * nq + i, 0)
    per_b = lambda b, i: (b, 0, 0)
    fixed = lambda b, i: (0, 0)
    blocks = (_nbytes((TQ, Q_WIDTH), BF16) + _nbytes((TQ, SLOT), F32)
              + 2 * _nbytes((n_cmp_pad, w_kv_pad), BF16) + 2 * _nbytes((seq_len, KV_WIDTH), BF16)
              + 2 * _nbytes((n_kt, VT_SLAB, KT), BF16)
              + _nbytes(ov.shape, F32) + _nbytes((TQ, Q_WIDTH), BF16))
    n_chains = N_KV * GROUP * TQ // CHAIN_COLS
    run_max = [pltpu.VMEM((1, CHAIN_COLS), F32)] * n_chains
    run_acc = [pltpu.VMEM((VT_ROWS, CHAIN_COLS), F32)] * n_chains
    scratch_shapes = (
        [pltpu.VMEM((seq_len, w_kv_pad), BF16),
         pltpu.VMEM((seq_len, w_kv_pad), BF16),
         pltpu.VMEM((N_KV, GROUP * TQ, SLOT), BF16),
         pltpu.VMEM((N_HEADS, HEAD_DIM, TQ), F32)]
        + run_max + run_acc + run_max + run_acc)
    scratch = sum(_nbytes(s.shape, s.dtype) for s in scratch_shapes)
    return pl.pallas_call(
        functools.partial(_nsa_kernel, n_slc=n_slc),
        grid=(batch, nq),
        in_specs=[
            pl.BlockSpec((TQ, Q_WIDTH), qrow),
            pl.BlockSpec((TQ, SLOT), qrow),
            pl.BlockSpec((1, n_cmp_pad, w_kv_pad), per_b),
            pl.BlockSpec((1, n_cmp_pad, w_kv_pad), per_b),
            pl.BlockSpec((1, seq_len, KV_WIDTH), per_b),
            pl.BlockSpec((n_kt, VT_SLAB, KT), per_b),
            pl.BlockSpec((1, seq_len, KV_WIDTH), per_b),
            pl.BlockSpec((n_kt, VT_SLAB, KT), per_b),
            pl.BlockSpec(ov.shape, fixed),
        ],
        out_specs=pl.BlockSpec((TQ, Q_WIDTH), qrow),
        out_shape=jax.ShapeDtypeStruct((batch * seq_len, Q_WIDTH), BF16),
        scratch_shapes=scratch_shapes,
        compiler_params=pltpu.CompilerParams(
            dimension_semantics=("arbitrary", "arbitrary"),
            vmem_limit_bytes=_vmem_limit(blocks, scratch)),
        name="nsa",
    )(q, gate, kc.reshape(batch, n_cmp_pad, w_kv_pad), vc.reshape(batch, n_cmp_pad, w_kv_pad),
      ks.reshape(batch, seq_len, KV_WIDTH), vst, kw.reshape(batch, seq_len, KV_WIDTH), vwt, ov)


def _pool_kernel(u_ref, w_ref, scale_ref, o_ref):
    seq = u_ref.shape[0]
    t = lax.broadcasted_iota(jnp.int32, (seq, POOL_GROUP), 0)

    def shifted(x, k):
        return jnp.where(t >= k, pltpu.roll(x, k, axis=0), 0.0)

    for gi, w in enumerate(POOL_WINDOWS):
        sl = slice(gi * POOL_GROUP, (gi + 1) * POOL_GROUP)
        x = u_ref[:, sl]
        wsum = x
        span = 1
        while span < w:
            wsum = wsum + shifted(wsum, span)
            span *= 2
        cnt = jnp.minimum(t + 1, w).astype(F32)
        pooled = (wsum / cnt - x).astype(BF16)
        o_ref[:, sl] = (_dot(pooled, w_ref[gi]) * scale_ref[:, sl]).astype(BF16)


def _pool(pool_in, pool_w, pool_scale, batch, seq_len):
    for w in POOL_WINDOWS:
        assert w & (w - 1) == 0
    blocks = (_nbytes((seq_len, POOL_WIDTH), F32) + _nbytes(pool_w.shape, BF16)
              + _nbytes((seq_len, POOL_WIDTH), BF16))
    return pl.pallas_call(
        _pool_kernel,
        grid=(batch,),
        in_specs=[
            pl.BlockSpec((seq_len, POOL_WIDTH), lambda b: (b, 0)),
            pl.BlockSpec(pool_w.shape, lambda b: (0, 0, 0)),
            pl.BlockSpec((1, POOL_WIDTH), lambda b: (0, 0)),
        ],
        out_specs=pl.BlockSpec((seq_len, POOL_WIDTH), lambda b: (b, 0)),
        out_shape=jax.ShapeDtypeStruct((batch * seq_len, POOL_WIDTH), BF16),
        compiler_params=pltpu.CompilerParams(
            dimension_semantics=("arbitrary",), vmem_limit_bytes=_vmem_limit(blocks)),
        name="pool",
    )(pool_in, pool_w.astype(BF16), pool_scale.reshape(1, -1))


def _merge_kernel(x_ref, gpre_ref, wm_ref, nsa_ref, wa_ref, pool_ref, wp_ref, wo_ref, gpost_ref,
                  o_ref):
    x = x_ref[...]
    hn = _rms(x, gpre_ref[...]).astype(BF16)
    g_attn = jax.nn.sigmoid(_dot(hn, wm_ref[:, 0:D_MODEL]))
    g_pool = jax.nn.sigmoid(_dot(hn, wm_ref[:, D_MODEL:2 * D_MODEL]))
    y = g_attn * _dot(nsa_ref[...], wa_ref[...]) + g_pool * _dot(pool_ref[...], wp_ref[...])
    h = _dot(y.astype(BF16), wo_ref[...])
    o_ref[...] = x + _rms(h, gpost_ref[...])


def _merge(x1, g_pre, w_merge, o_nsa, w_attn, o_pool, w_pool, w_out, g_post):
    t = x1.shape[0]
    row = lambda i: (i, 0)
    fixed = lambda i: (0, 0)
    blocks = (2 * _nbytes((TOK_TILE, D_MODEL), F32) + _nbytes((D_MODEL, 2 * D_MODEL), BF16)
              + _nbytes((TOK_TILE, Q_WIDTH), BF16) + _nbytes((Q_WIDTH, D_MODEL), BF16)
              + _nbytes((TOK_TILE, POOL_WIDTH), BF16) + _nbytes((POOL_WIDTH, D_MODEL), BF16)
              + _nbytes((D_MODEL, D_MODEL), BF16))
    return pl.pallas_call(
        _merge_kernel,
        grid=(t // TOK_TILE,),
        in_specs=[
            pl.BlockSpec((TOK_TILE, D_MODEL), row),
            pl.BlockSpec((1, D_MODEL), fixed),
            pl.BlockSpec((D_MODEL, 2 * D_MODEL), fixed),
            pl.BlockSpec((TOK_TILE, Q_WIDTH), row),
            pl.BlockSpec((Q_WIDTH, D_MODEL), fixed),
            pl.BlockSpec((TOK_TILE, POOL_WIDTH), row),
            pl.BlockSpec((POOL_WIDTH, D_MODEL), fixed),
            pl.BlockSpec((D_MODEL, D_MODEL), fixed),
            pl.BlockSpec((1, D_MODEL), fixed),
        ],
        out_specs=pl.BlockSpec((TOK_TILE, D_MODEL), row),
        out_shape=jax.ShapeDtypeStruct((t, D_MODEL), F32),
        compiler_params=pltpu.CompilerParams(
            dimension_semantics=("arbitrary",), vmem_limit_bytes=_vmem_limit(blocks)),
        name="merge",
    )(x1, g_pre.reshape(1, -1), w_merge.astype(BF16), o_nsa, w_attn.astype(BF16), o_pool,
      w_pool.astype(BF16), w_out.astype(BF16), g_post.reshape(1, -1))


def _layer(x, tab, tab_cmp, batch, seq_len, g_ffn1_pre, w_ffn1_gate, w_ffn1_up, w_ffn1_down,
           g_ffn1_post, g_mix_pre, w_in, cmp_pe_k, cmp_w1_k, cmp_w2_k, cmp_pe_v, cmp_w1_v, cmp_w2_v,
           w_attn_branch, pool_w, pool_scale, w_pool_branch, w_out, g_mix_post, g_ffn2_pre,
           w_ffn2_gate, w_ffn2_up, w_ffn2_down, g_ffn2_post):
    x1 = _ffn(x, g_ffn1_pre, w_ffn1_gate, w_ffn1_up, w_ffn1_down, g_ffn1_post)
    (q, ks, kw, vst, vwt, cmp_rows, gate, pool_in), w_merge = _in_proj(x1, g_mix_pre, w_in, tab)
    kc, vc = _compress(cmp_rows, cmp_pe_k, cmp_w1_k, cmp_w2_k, cmp_pe_v, cmp_w1_v, cmp_w2_v, tab_cmp)
    n_cmp_pad = seq_len // CMP_STRIDE
    kc = kc.reshape(batch, n_cmp_pad, N_KV * SLOT)
    vc = vc.reshape(batch, n_cmp_pad, N_KV * SLOT)
    o_nsa = _nsa(q, gate, kc, vc, ks, vst, kw, vwt, batch, seq_len)
    o_pool = _pool(pool_in, pool_w, pool_scale, batch, seq_len)
    x2 = _merge(x1, g_mix_pre, w_merge, o_nsa, w_attn_branch, o_pool, w_pool_branch, w_out,
                g_mix_post)
    return _ffn(x2, g_ffn2_pre, w_ffn2_gate, w_ffn2_up, w_ffn2_down, g_ffn2_post)


def kernel(x, positions, g_ffn1_pre, w_ffn1_gate, w_ffn1_up, w_ffn1_down, g_ffn1_post, g_mix_pre, w_in, cmp_pe_k, cmp_w1_k, cmp_w2_k, cmp_pe_v, cmp_w1_v, cmp_w2_v, w_attn_branch, pool_w, pool_scale, w_pool_branch, w_out, g_mix_post, g_ffn2_pre, w_ffn2_gate, w_ffn2_up, w_ffn2_down, g_ffn2_post):
    batch, seq_len, d_model = x.shape
    assert d_model == D_MODEL and seq_len % TOK_TILE == 0 and seq_len % TQ == 0
    assert TQ == KT and TQ % SEL_LEN == 0 and TOK_TILE % KT == 0 and WINDOW % KT == 0
    assert seq_len % CMP_STRIDE == 0 and CMP_LEN == 2 * CMP_STRIDE
    t = batch * seq_len
    tab = _rope_tables(positions.reshape(t))
    n_cmp = (seq_len - CMP_LEN) // CMP_STRIDE + 1
    n_cmp_pad = seq_len // CMP_STRIDE
    tab_cmp = tab.reshape(batch, seq_len, -1)[:, CMP_LEN - 1::CMP_STRIDE]
    tab_cmp = jnp.pad(tab_cmp, ((0, 0), (0, n_cmp_pad - n_cmp), (0, 0))).reshape(batch * n_cmp_pad, -1)
    xf = x.reshape(t, D_MODEL)
    per_layer = (g_ffn1_pre, w_ffn1_gate, w_ffn1_up, w_ffn1_down, g_ffn1_post, g_mix_pre, w_in,
                 cmp_pe_k, cmp_w1_k, cmp_w2_k, cmp_pe_v, cmp_w1_v, cmp_w2_v, w_attn_branch, pool_w,
                 pool_scale, w_pool_branch, w_out, g_mix_post, g_ffn2_pre, w_ffn2_gate, w_ffn2_up,
                 w_ffn2_down, g_ffn2_post)
    for l in range(g_ffn1_pre.shape[0]):
        xf = _layer(xf, tab, tab_cmp, batch, seq_len, *(p[l] for p in per_layer))
    return xf.reshape(batch, seq_len, D_MODEL)
```

```python
import functools
import math

import jax
import jax.numpy as jnp
from jax import lax
from jax.experimental import pallas as pl
from jax.experimental.pallas import tpu as pltpu

F32 = jnp.float32
BF16 = jnp.bfloat16

D_MODEL = 1024
N_HEADS = 16
HEAD_DIM = 64
N_KV = 4
GROUP = N_HEADS // N_KV
ROT_DIM = HEAD_DIM // 4
ROT_HALF = ROT_DIM // 2
ROPE_THETA = 500000.0
CMP_LEN = 32
CMP_STRIDE = 16
CMP_HIDDEN = 2 * HEAD_DIM
SEL_LEN = 64
SEL_SHIFT = SEL_LEN.bit_length() - 1
N_SEL = 8
WINDOW = 512
POOL_WINDOWS = (2, 4, 8, 16)
POOL_WIDTH = D_MODEL // 2
POOL_GROUP = POOL_WIDTH // len(POOL_WINDOWS)
D_FF = 2816
EPS = 1e-6
NEG_INF = -1e30
FORCE_SCORE = 1e4
Q_WIDTH = N_HEADS * HEAD_DIM
KV_WIDTH = N_KV * HEAD_DIM
N_GATES = 3 * N_HEADS
LOG2_E = math.log2(math.e)

LANES = 128
BF16_SUBLANES = 16
V7X_VMEM_BYTES = 64 * 1024 * 1024

SLOT = LANES
BLK_LANE0 = HEAD_DIM
VT_ROWS = HEAD_DIM + BF16_SUBLANES
ONE_ROW = HEAD_DIM
TOK_TILE = 512
TQ = 256
KT = 256
CHAIN_COLS = 256
PIPE_DEPTH = 5
FF_CHUNK = 256
N_ROPE_TAB = 2


def _vmem_limit(block_bytes, scratch_bytes=0):
    need = 2 * block_bytes + scratch_bytes
    return int(min(V7X_VMEM_BYTES - (4 << 20), max(2 * need, 32 << 20)))


def _nbytes(shape, dtype):
    n = 1
    for s in shape:
        n *= s
    return n * jnp.dtype(dtype).itemsize


def _rms(xf, g):
    return xf * lax.rsqrt(jnp.mean(xf * xf, axis=-1, keepdims=True) + EPS) * g


def _dot(a, b):
    return jnp.dot(a, b, preferred_element_type=F32)


def _dot_nt(a, b):
    return lax.dot_general(a, b, (((1,), (1,)), ((), ())), preferred_element_type=F32)


def _rope_trig_kernel(pos_ref, inv_ref, tab_ref):
    tn = pos_ref.shape[1]
    ang = pos_ref[...].astype(F32) * inv_ref[...]
    c, s = jnp.cos(ang), jnp.sin(ang)
    rest = HEAD_DIM - ROT_DIM
    reps = SLOT // HEAD_DIM
    cos_rows = jnp.concatenate([c, c, jnp.ones((rest, tn), F32)] * reps, axis=0)
    sin_rows = jnp.concatenate([-s, s, jnp.zeros((rest, tn), F32)] * reps, axis=0)
    for j in range(tn // LANES):
        cs = slice(j * LANES, (j + 1) * LANES)
        tab_ref[cs, 0:SLOT] = cos_rows[:, cs].T
        tab_ref[cs, SLOT:2 * SLOT] = sin_rows[:, cs].T


def _rope_tables(pos_flat, tile):
    n = pos_flat.shape[0]
    inv = ROPE_THETA ** (-jnp.arange(ROT_HALF, dtype=F32) * (2.0 / ROT_DIM))
    return pl.pallas_call(
        _rope_trig_kernel,
        grid=(n // tile,),
        in_specs=[pl.BlockSpec((1, tile), lambda i: (0, i)),
                  pl.BlockSpec((ROT_HALF, 1), lambda i: (0, 0))],
        out_specs=pl.BlockSpec((tile, N_ROPE_TAB * SLOT), lambda i: (i, 0)),
        out_shape=jax.ShapeDtypeStruct((n, N_ROPE_TAB * SLOT), F32),
        compiler_params=pltpu.CompilerParams(dimension_semantics=("arbitrary",)),
        name="rope_trig",
    )(pos_flat.reshape(1, n), inv.reshape(ROT_HALF, 1))


def _rope_slot(y, tab):
    cos_t = tab[:, 0:SLOT]
    sin_t = tab[:, SLOT:2 * SLOT]
    lane = lax.broadcasted_iota(jnp.int32, y.shape, 1) & (HEAD_DIM - 1)
    up = pltpu.roll(y, SLOT - ROT_HALF, axis=1)
    down = pltpu.roll(y, ROT_HALF, axis=1)
    return y * cos_t + jnp.where(lane < ROT_HALF, up, down) * sin_t


def _ffn_kernel(x_ref, gpre_ref, wg_ref, wu_ref, wd_ref, gpost_ref, o_ref, acc_ref):
    x = x_ref[...]
    hn = _rms(x, gpre_ref[...]).astype(BF16)
    for c in range(D_FF // FF_CHUNK):
        sl = slice(c * FF_CHUNK, (c + 1) * FF_CHUNK)
        g = _dot(hn, wg_ref[:, sl])
        u = _dot(hn, wu_ref[:, sl])
        a = (g * jax.nn.sigmoid(g) * u).astype(BF16)
        d = _dot(a, wd_ref[sl, :])
        if c == 0:
            acc_ref[...] = d
        else:
            acc_ref[...] += d
    o_ref[...] = x + 0.5 * _rms(acc_ref[...], gpost_ref[...])


def _ffn(x, g_pre, w_gate, w_up, w_down, g_post):
    t = x.shape[0]
    row = lambda i: (i, 0)
    fixed = lambda i: (0, 0)
    blocks = (2 * _nbytes((TOK_TILE, D_MODEL), F32) + 3 * _nbytes((D_MODEL, D_FF), BF16))
    return pl.pallas_call(
        _ffn_kernel,
        grid=(t // TOK_TILE,),
        in_specs=[
            pl.BlockSpec((TOK_TILE, D_MODEL), row),
            pl.BlockSpec((1, D_MODEL), fixed),
            pl.BlockSpec((D_MODEL, D_FF), fixed),
            pl.BlockSpec((D_MODEL, D_FF), fixed),
            pl.BlockSpec((D_FF, D_MODEL), fixed),
            pl.BlockSpec((1, D_MODEL), fixed),
        ],
        out_specs=pl.BlockSpec((TOK_TILE, D_MODEL), row),
        out_shape=jax.ShapeDtypeStruct((t, D_MODEL), F32),
        scratch_shapes=[pltpu.VMEM((TOK_TILE, D_MODEL), F32)],
        compiler_params=pltpu.CompilerParams(
            dimension_semantics=("arbitrary",),
            vmem_limit_bytes=_vmem_limit(blocks, _nbytes((TOK_TILE, D_MODEL), F32))),
        name="ffn",
    )(x, g_pre.reshape(1, -1), w_gate.astype(BF16), w_up.astype(BF16), w_down.astype(BF16),
      g_post.reshape(1, -1))


OFF_Q = 0
OFF_CMP = OFF_Q + Q_WIDTH
OFF_KS = OFF_CMP + 2 * KV_WIDTH
OFF_VS = OFF_KS + KV_WIDTH
OFF_KW = OFF_VS + KV_WIDTH
OFF_VW = OFF_KW + KV_WIDTH
OFF_GATE = OFF_VW + KV_WIDTH
OFF_POOL = OFF_GATE + N_GATES
OFF_MERGE = OFF_POOL + POOL_WIDTH
W_CMP = 2 * KV_WIDTH
W_GP = SLOT + POOL_WIDTH
CHUNK_TOK = CMP_STRIDE
VT_SLAB = N_KV * VT_ROWS


def _in_proj_kernel(x_ref, g_ref, w_ref, wgp_ref, wvt_ref, tab_ref, q_ref, ks_ref, kw_ref, vst_ref,
                    vwt_ref, cmp_ref, gate_ref, pool_ref, cmp_scr):
    hn = _rms(x_ref[...], g_ref[...]).astype(BF16)
    tab = tab_ref[...]
    rows = x_ref.shape[0]
    wide = 4 * SLOT
    tab_q = tab * (HEAD_DIM ** -0.5 * LOG2_E)
    for c in range(Q_WIDTH // wide):
        y = _dot(hn, w_ref[:, OFF_Q + c * wide:OFF_Q + (c + 1) * wide])
        for j in range(wide // SLOT):
            sl = slice(c * wide + j * SLOT, c * wide + (j + 1) * SLOT)
            q_ref[:, sl] = _rope_slot(y[:, j * SLOT:(j + 1) * SLOT], tab_q).astype(BF16)
    for k_ref, off in ((ks_ref, OFF_KS), (kw_ref, OFF_KW)):
        y = _dot(hn, w_ref[:, off:off + KV_WIDTH])
        for j in range(KV_WIDTH // SLOT):
            sl = slice(j * SLOT, (j + 1) * SLOT)
            k_ref[:, sl] = _rope_slot(y[:, sl], tab).astype(BF16)
    y = _dot(hn, w_ref[:, OFF_CMP:OFF_CMP + W_CMP])
    for j in range(W_CMP // LANES):
        cmp_scr[j] = y[:, j * LANES:(j + 1) * LANES]
    for l in range(CHUNK_TOK):
        for j in range(W_CMP // LANES):
            c0 = l * W_CMP + j * LANES
            cmp_ref[:, c0:c0 + LANES] = (
                cmp_scr[j, pl.ds(l, rows // CHUNK_TOK, stride=CHUNK_TOK), :].astype(BF16))
    y = _dot(hn, wgp_ref[...])
    gate_ref[...] = y[:, 0:SLOT]
    pool_ref[...] = y[:, SLOT:W_GP]
    aux = (lax.broadcasted_iota(jnp.int32, (VT_ROWS - HEAD_DIM, KT), 0) == 0).astype(BF16)
    for vt_ref, r0 in ((vst_ref, 0), (vwt_ref, KV_WIDTH)):
        vt = _dot_nt(wvt_ref[r0:r0 + KV_WIDTH, :], hn).astype(BF16)
        for j in range(rows // KT):
            for g in range(N_KV):
                vt_ref[j, g * VT_ROWS:g * VT_ROWS + HEAD_DIM, :] = (
                    vt[g * HEAD_DIM:(g + 1) * HEAD_DIM, j * KT:(j + 1) * KT])
                vt_ref[j, g * VT_ROWS + HEAD_DIM:(g + 1) * VT_ROWS, :] = aux


def _in_proj(x1, g_pre, w_in, tab):
    t = x1.shape[0]
    w_main = w_in[:, :OFF_GATE].astype(BF16)
    w_gp = jnp.concatenate([jnp.pad(w_in[:, OFF_GATE:OFF_POOL], ((0, 0), (0, SLOT - N_GATES))),
                            w_in[:, OFF_POOL:OFF_MERGE]], axis=1).astype(BF16)
    w_vt = jnp.concatenate([w_in[:, OFF_VS:OFF_VS + KV_WIDTH], w_in[:, OFF_VW:OFF_VW + KV_WIDTH]],
                           axis=1).T.astype(BF16)
    row = lambda i: (i, 0)
    fixed = lambda i: (0, 0)
    tiles = TOK_TILE // KT
    chunk_rows = TOK_TILE // CHUNK_TOK
    slab = pl.BlockSpec((tiles, VT_SLAB, KT), lambda i: (i, 0, 0))
    slab_shape = jax.ShapeDtypeStruct((t // KT, VT_SLAB, KT), BF16)
    out_specs = [
        pl.BlockSpec((TOK_TILE, Q_WIDTH), row), pl.BlockSpec((TOK_TILE, KV_WIDTH), row),
        pl.BlockSpec((TOK_TILE, KV_WIDTH), row), slab, slab,
        pl.BlockSpec((chunk_rows, CHUNK_TOK * W_CMP), row),
        pl.BlockSpec((TOK_TILE, SLOT), row), pl.BlockSpec((TOK_TILE, POOL_WIDTH), row)]
    out_shape = [
        jax.ShapeDtypeStruct((t, Q_WIDTH), BF16), jax.ShapeDtypeStruct((t, KV_WIDTH), BF16),
        jax.ShapeDtypeStruct((t, KV_WIDTH), BF16), slab_shape, slab_shape,
        jax.ShapeDtypeStruct((t // CHUNK_TOK, CHUNK_TOK * W_CMP), BF16),
        jax.ShapeDtypeStruct((t, SLOT), F32), jax.ShapeDtypeStruct((t, POOL_WIDTH), F32)]
    blocks = (_nbytes((TOK_TILE, D_MODEL), F32) + _nbytes(w_main.shape, BF16)
              + _nbytes(w_gp.shape, BF16) + _nbytes(w_vt.shape, BF16)
              + _nbytes((TOK_TILE, N_ROPE_TAB * SLOT), F32)
              + _nbytes((TOK_TILE, Q_WIDTH + 2 * KV_WIDTH + W_CMP), BF16)
              + 2 * _nbytes((tiles, VT_SLAB, KT), BF16) + _nbytes((TOK_TILE, SLOT + POOL_WIDTH), F32))
    res = pl.pallas_call(
        _in_proj_kernel,
        grid=(t // TOK_TILE,),
        in_specs=[
            pl.BlockSpec((TOK_TILE, D_MODEL), row),
            pl.BlockSpec((1, D_MODEL), fixed),
            pl.BlockSpec(w_main.shape, fixed),
            pl.BlockSpec(w_gp.shape, fixed),
            pl.BlockSpec(w_vt.shape, fixed),
            pl.BlockSpec((TOK_TILE, N_ROPE_TAB * SLOT), row),
        ],
        out_specs=out_specs,
        out_shape=out_shape,
        scratch_shapes=[pltpu.VMEM((W_CMP // LANES, TOK_TILE, LANES), F32)],
        compiler_params=pltpu.CompilerParams(
            dimension_semantics=("arbitrary",),
            vmem_limit_bytes=_vmem_limit(blocks, _nbytes((TOK_TILE, W_CMP), F32))),
        name="in_proj",
    )(x1, g_pre.reshape(1, -1), w_main, w_gp, w_vt, tab)
    return res, w_in[:, OFF_MERGE:]


def _compress_kernel(x_ref, w1k_ref, w1v_ref, pek_ref, pev_ref, w1k_raw_ref, w1v_raw_ref,
                     w2k_ref, w2v_ref, tab_ref, kc_ref, vc_ref, acck_ref, accv_ref):
    l = pl.program_id(0)

    @pl.when(l == 0)
    def _():
        acck_ref[...] = jnp.zeros(acck_ref.shape, F32)
        accv_ref[...] = jnp.zeros(accv_ref.shape, F32)

    for acc_ref, w1_ref, c0 in ((acck_ref, w1k_ref, 0), (accv_ref, w1v_ref, KV_WIDTH)):
        for g in range(N_KV):
            acc_ref[g] += _dot(x_ref[:, c0 + g * HEAD_DIM:c0 + (g + 1) * HEAD_DIM], w1_ref[0])

    @pl.when(l == CHUNK_TOK - 1)
    def _():
        rows = acck_ref.shape[1]
        tab = tab_ref[...]
        for acc_ref, pe_ref, w1_raw_ref, w2_ref, o_ref, rope in (
                (acck_ref, pek_ref, w1k_raw_ref, w2k_ref, kc_ref, True),
                (accv_ref, pev_ref, w1v_raw_ref, w2v_ref, vc_ref, False)):
            bias = _dot(pe_ref[...], w1_raw_ref[...])[0:1, :]
            for g in range(N_KV):
                first = acc_ref[g, :, 0:CMP_HIDDEN]
                second = pltpu.roll(acc_ref[g, :, CMP_HIDDEN:2 * CMP_HIDDEN], rows - 1, axis=0)
                hid = jax.nn.gelu(first + second + bias).astype(BF16)
                out = _dot(hid, w2_ref[...])
                if rope:
                    out = _rope_slot(out, tab)
                o_ref[:, g * SLOT:(g + 1) * SLOT] = out.astype(BF16)


def _compress_weights(w1, w2):
    w1 = w1.reshape(2, CHUNK_TOK, HEAD_DIM, CMP_HIDDEN)
    w1_tok = jnp.concatenate([w1[0], w1[1]], axis=-1).astype(BF16)
    return w1_tok, jnp.pad(w2, ((0, 0), (0, SLOT - HEAD_DIM))).astype(BF16)


def _compress(cmp_rows, cmp_pe_k, cmp_w1_k, cmp_w2_k, cmp_pe_v, cmp_w1_v, cmp_w2_v, tab_cmp):
    rows = cmp_rows.shape[0]
    w1k, w2k = _compress_weights(cmp_w1_k, cmp_w2_k)
    w1v, w2v = _compress_weights(cmp_w1_v, cmp_w2_v)
    sub = 8
    pek = jnp.broadcast_to(cmp_pe_k.reshape(1, -1), (sub, CMP_LEN * HEAD_DIM)).astype(BF16)
    pev = jnp.broadcast_to(cmp_pe_v.reshape(1, -1), (sub, CMP_LEN * HEAD_DIM)).astype(BF16)
    fixed = lambda l: (0, 0)
    w_kv_pad = N_KV * SLOT
    blocks = (_nbytes((rows, W_CMP), BF16) + 2 * _nbytes((HEAD_DIM, 2 * CMP_HIDDEN), BF16)
              + 2 * _nbytes((sub + CMP_LEN * HEAD_DIM, CMP_HIDDEN), BF16)
              + 2 * _nbytes((CMP_HIDDEN, SLOT), BF16)
              + _nbytes((rows, N_ROPE_TAB * SLOT), F32) + 2 * _nbytes((rows, w_kv_pad), BF16))
    acc = pltpu.VMEM((N_KV, rows, 2 * CMP_HIDDEN), F32)
    return pl.pallas_call(
        _compress_kernel,
        grid=(CHUNK_TOK,),
        in_specs=[
            pl.BlockSpec((rows, W_CMP), lambda l: (0, l)),
            pl.BlockSpec((1, HEAD_DIM, 2 * CMP_HIDDEN), lambda l: (l, 0, 0)),
            pl.BlockSpec((1, HEAD_DIM, 2 * CMP_HIDDEN), lambda l: (l, 0, 0)),
            pl.BlockSpec((sub, CMP_LEN * HEAD_DIM), fixed),
            pl.BlockSpec((sub, CMP_LEN * HEAD_DIM), fixed),
            pl.BlockSpec((CMP_LEN * HEAD_DIM, CMP_HIDDEN), fixed),
            pl.BlockSpec((CMP_LEN * HEAD_DIM, CMP_HIDDEN), fixed),
            pl.BlockSpec((CMP_HIDDEN, SLOT), fixed),
            pl.BlockSpec((CMP_HIDDEN, SLOT), fixed),
            pl.BlockSpec((rows, N_ROPE_TAB * SLOT), fixed),
        ],
        out_specs=[pl.BlockSpec((rows, w_kv_pad), fixed)] * 2,
        out_shape=[jax.ShapeDtypeStruct((rows, w_kv_pad), BF16)] * 2,
        scratch_shapes=[acc, acc],
        compiler_params=pltpu.CompilerParams(
            dimension_semantics=("arbitrary",),
            vmem_limit_bytes=_vmem_limit(blocks, 2 * _nbytes(acc.shape, F32))),
        name="compress",
    )(cmp_rows, w1k, w1v, pek, pev, cmp_w1_k.astype(BF16), cmp_w1_v.astype(BF16), w2k, w2v, tab_cmp)


def _nsa_kernel(q_ref, gate_ref, kc_ref, vc_ref, ks_ref, vst_ref, kw_ref, vwt_ref, ov_ref,
                o_ref, ksp_ref, kwp_ref, qg_ref, mix_ref, *state_refs, n_slc):
    i = pl.program_id(1)
    start = i * TQ
    seq = ks_ref.shape[1]
    cols_g = GROUP * TQ
    chains_g = cols_g // CHAIN_COLS
    n_chains = N_KV * chains_g
    n_cmp = kc_ref.shape[1]
    sel_state = (state_refs[0:n_chains], state_refs[n_chains:2 * n_chains])
    win_state = (state_refs[2 * n_chains:3 * n_chains], state_refs[3 * n_chains:4 * n_chains])
    lane_q = lax.broadcasted_iota(jnp.int32, (TQ, SLOT), 1)

    def head_slot(x_ref, rows, j):
        two = x_ref[rows, (j // 2) * SLOT:(j // 2 + 1) * SLOT].astype(F32)
        return pltpu.roll(two, HEAD_DIM, axis=1) if j % 2 else two

    @pl.when(i == 0)
    def _():
        for c in range(seq // KT):
            rows = pl.ds(c * KT, KT)
            lane = lax.broadcasted_iota(jnp.int32, (KT, SLOT), 1)
            key_blk = (c * KT + lax.broadcasted_iota(jnp.int32, (KT, SLOT), 0)) >> SEL_SHIFT
            onehot = (lane - BLK_LANE0 == key_blk).astype(F32)
            for g in range(N_KV):
                gs = slice(g * SLOT, (g + 1) * SLOT)
                ksp_ref[rows, gs] = jnp.where(lane < HEAD_DIM, head_slot(ks_ref.at[0], rows, g),
                                              onehot).astype(BF16)
                kwp_ref[rows, gs] = jnp.where(lane < HEAD_DIM, head_slot(kw_ref.at[0], rows, g),
                                              0.0).astype(BF16)

    for h in range(N_HEADS):
        g, r = divmod(h, GROUP)
        qg_ref[g, r * TQ:(r + 1) * TQ, :] = jnp.where(
            lane_q < HEAD_DIM, head_slot(q_ref, slice(None), h), 0.0).astype(BF16)

    gates_t = jax.nn.sigmoid(gate_ref[...]).T
    t_cmp = start + lax.broadcasted_iota(jnp.int32, (n_cmp, TQ), 1)
    n_idx = lax.broadcasted_iota(jnp.int32, (n_cmp, TQ), 0)
    cmp_valid = n_idx * CMP_STRIDE + (CMP_LEN - 1) <= t_cmp
    any_cmp = start + lax.broadcasted_iota(jnp.int32, (1, TQ), 1) >= CMP_LEN - 1
    t_row = start + lax.broadcasted_iota(jnp.int32, (n_slc, TQ), 1)
    blk = lax.broadcasted_iota(jnp.int32, (n_slc, TQ), 0)
    forced = (blk == t_row >> SEL_SHIFT) | (blk == 0)
    causal_blk = blk * SEL_LEN <= t_row
    key_in = lax.broadcasted_iota(jnp.int32, (KT, CHAIN_COLS), 0)
    qry_in = lax.broadcasted_iota(jnp.int32, (KT, CHAIN_COLS), 1) & (TQ - 1)
    diag_mask = key_in <= qry_in
    far_mask = key_in > qry_in

    def gate_row(branch, h):
        c = branch * N_HEADS + h
        return gates_t[c:c + 1, :]

    cmp_scores = []
    for h in range(N_HEADS):
        g, r = divmod(h, GROUP)
        cmp_scores.append(_dot_nt(kc_ref[0, :, g * SLOT:(g + 1) * SLOT],
                                  qg_ref[g, r * TQ:(r + 1) * TQ, :]))
    for g in range(N_KV):
        vct_g = vc_ref[0, :, g * SLOT:(g + 1) * SLOT].astype(F32).T[0:HEAD_DIM].astype(BF16)
        p_sum = jnp.zeros((n_cmp, TQ), F32)
        for r in range(GROUP):
            h = g * GROUP + r
            s = jnp.where(cmp_valid, cmp_scores[h], NEG_INF)
            p = jnp.exp2(s - jnp.max(s, axis=0, keepdims=True))
            inv = jnp.where(any_cmp, 1.0 / jnp.sum(p, axis=0, keepdims=True), 0.0)
            p = p * inv
            p_sum = p_sum + p
            mix_ref[h] = gate_row(0, h) * _dot(vct_g, p.astype(BF16))
        imp = jnp.dot(ov_ref[...], p_sum, precision=lax.Precision.HIGHEST,
                      preferred_element_type=F32)
        score = jnp.where(causal_blk, jnp.where(forced, FORCE_SCORE, imp), NEG_INF)
        sub = 8
        idx8 = lax.broadcasted_iota(jnp.int32, (sub, TQ), 0)
        rows8 = [score[k * sub:(k + 1) * sub] for k in range(n_slc // sub)]
        ranks = [jnp.zeros((sub, TQ), jnp.int32) for _ in rows8]
        for c in range(n_slc):
            other = score[c:c + 1, :]
            for k, mine in enumerate(rows8):
                if k * sub > c:
                    beats = other >= mine
                elif (k + 1) * sub - 1 <= c:
                    beats = other > mine
                else:
                    beats = (other > mine) | ((other == mine) & (idx8 > c - k * sub))
                ranks[k] = ranks[k] + beats.astype(jnp.int32)
        selected = causal_blk & (jnp.concatenate(ranks, axis=0) < N_SEL)
        bias = jnp.where(selected, 0.0, NEG_INF)
        bias_t = jnp.concatenate([jnp.zeros((BLK_LANE0, TQ), F32), bias,
                                  jnp.zeros((SLOT - BLK_LANE0 - n_slc, TQ), F32)], axis=0)
        bias_q = bias_t.T.astype(BF16)
        for r in range(GROUP):
            rs = slice(r * TQ, (r + 1) * TQ)
            qg_ref[g, rs, :] = jnp.where(lane_q < HEAD_DIM, qg_ref[g, rs, :], bias_q)

    def flash_init(state):
        for m_ref, acc_ref in zip(*state):
            m_ref[...] = jnp.full(m_ref.shape, NEG_INF, F32)
            acc_ref[...] = jnp.zeros(acc_ref.shape, F32)

    def flash_tiles(tiles):
        jobs = [(tile, c) for tile in tiles for c in range(n_chains)]
        scores = {}
        for step in range(len(jobs) + PIPE_DEPTH):
            if step < len(jobs):
                (kp_ref, _, kt, mask, _), chain = jobs[step]
                g, c = divmod(chain, chains_g)
                off = pl.multiple_of(kt * KT, KT)
                s = _dot_nt(kp_ref[pl.ds(off, KT), g * SLOT:(g + 1) * SLOT],
                            qg_ref[g, c * CHAIN_COLS:(c + 1) * CHAIN_COLS, :])
                scores[step] = s if mask is None else jnp.where(mask, s, NEG_INF)
            done = step - PIPE_DEPTH
            if done >= 0:
                (_, vt_ref, kt, _, (m_refs, acc_refs)), chain = jobs[done]
                g = chain // chains_g
                m_ref, acc_ref = m_refs[chain], acc_refs[chain]
                s = scores.pop(done)
                m_old = m_ref[...]
                m_new = jnp.maximum(m_old, jnp.max(s, axis=0, keepdims=True))
                p = jnp.exp2(s - m_new).astype(BF16)
                pv = _dot(vt_ref[kt, g * VT_ROWS:(g + 1) * VT_ROWS, :], p)
                acc_ref[...] = jnp.exp2(m_old - m_new) * acc_ref[...] + pv
                m_ref[...] = m_new

    def flash_mix(branch, state):
        _, acc_refs = state
        for h in range(N_HEADS):
            g, r = divmod(h, GROUP)
            c, c0 = divmod(r * TQ, CHAIN_COLS)
            acc = acc_refs[g * chains_g + c][:, c0:c0 + TQ]
            scale = gate_row(branch, h) * (1.0 / acc[ONE_ROW:ONE_ROW + 1, :])
            mix_ref[h] = mix_ref[h] + scale * acc[0:HEAD_DIM]

    flash_init(sel_state)
    flash_init(win_state)

    def sel_past(kt):
        return (ksp_ref, vst_ref, kt, None, sel_state)

    def sel_pair(j, carry):
        flash_tiles([sel_past(2 * j), sel_past(2 * j + 1)])
        return carry

    lax.fori_loop(0, i >> 1, sel_pair, 0)

    @pl.when(i & 1 == 1)
    def _():
        flash_tiles([sel_past(i - 1)])

    n_back = WINDOW // KT
    sel_diag = (ksp_ref, vst_ref, i, diag_mask, sel_state)
    win_diag = (kwp_ref, vwt_ref, i, diag_mask, win_state)

    def win_back(d):
        return (kwp_ref, vwt_ref, i - d, far_mask if d == n_back else None, win_state)

    for have in range(n_back + 1):
        cond = (i == have) if have < n_back else (i >= have)

        @pl.when(cond)
        def _(have=have):
            flash_tiles([sel_diag] + [win_back(d) for d in range(have, 0, -1)] + [win_diag])

    flash_mix(1, sel_state)
    flash_mix(2, win_state)
    for h2 in range(N_HEADS // 2):
        pair = jnp.concatenate([mix_ref[2 * h2], mix_ref[2 * h2 + 1]], axis=0)
        o_ref[:, h2 * SLOT:(h2 + 1) * SLOT] = pair.T.astype(BF16)


def _overlap(n_cmp_pad, n_slc):
    c0 = jnp.arange(n_cmp_pad) * CMP_STRIDE
    s0 = jnp.arange(n_slc) * SEL_LEN
    ov = jnp.minimum(c0[None, :] + CMP_LEN, s0[:, None] + SEL_LEN) - jnp.maximum(c0[None, :], s0[:, None])
    return jnp.clip(ov, 0).astype(F32) / CMP_LEN


def _nsa(q, gate, kc, vc, ks, vst, kw, vwt, batch, seq_len):
    n_slc = seq_len // SEL_LEN
    n_cmp_pad = kc.shape[1]
    w_kv_pad = N_KV * SLOT
    assert n_slc <= SLOT - BLK_LANE0 and n_slc % 8 == 0
    assert CHAIN_COLS % TQ == 0 and (GROUP * TQ) % CHAIN_COLS == 0
    ov = _overlap(n_cmp_pad, n_slc)
    nq = seq_len // TQ
    n_kt = seq_len // KT
    qrow = lambda b, i: (b * nq + i, 0)
    per_b = lambda b, i: (b, 0, 0)
    fixed = lambda b, i: (0, 0)
    blocks = (_nbytes((TQ, Q_WIDTH), BF16) + _nbytes((TQ, SLOT), F32)
              + 2 * _nbytes((n_cmp_pad, w_kv_pad), BF16) + 2 * _nbytes((seq_len, KV_WIDTH), BF16)
              + 2 * _nbytes((n_kt, VT_SLAB, KT), BF16)
              + _nbytes(ov.shape, F32) + _nbytes((TQ, Q_WIDTH), BF16))
    n_chains = N_KV * GROUP * TQ // CHAIN_COLS
    run_max = [pltpu.VMEM((1, CHAIN_COLS), F32)] * n_chains
    run_acc = [pltpu.VMEM((VT_ROWS, CHAIN_COLS), F32)] * n_chains
    scratch_shapes = (
        [pltpu.VMEM((seq_len, w_kv_pad), BF16),
         pltpu.VMEM((seq_len, w_kv_pad), BF16),
         pltpu.VMEM((N_KV, GROUP * TQ, SLOT), BF16),
         pltpu.VMEM((N_HEADS, HEAD_DIM, TQ), F32)]
        + run_max + run_acc + run_max + run_acc)
    scratch = sum(_nbytes(s.shape, s.dtype) for s in scratch_shapes)
    return pl.pallas_call(
        functools.partial(_nsa_kernel, n_slc=n_slc),
        grid=(batch, nq),
        in_specs=[
            pl.BlockSpec((TQ, Q_WIDTH), qrow),
            pl.BlockSpec((TQ, SLOT), qrow),
            pl.BlockSpec((1, n_cmp_pad, w_kv_pad), per_b),
            pl.BlockSpec((1, n_cmp_pad, w_kv_pad), per_b),
            pl.BlockSpec((1, seq_len, KV_WIDTH), per_b),
            pl.BlockSpec((n_kt, VT_SLAB, KT), per_b),
            pl.BlockSpec((1, seq_len, KV_WIDTH), per_b),
            pl.BlockSpec((n_kt, VT_SLAB, KT), per_b),
            pl.BlockSpec(ov.shape, fixed),
        ],
        out_specs=pl.BlockSpec((TQ, Q_WIDTH), qrow),
        out_shape=jax.ShapeDtypeStruct((batch * seq_len, Q_WIDTH), BF16),
        scratch_shapes=scratch_shapes,
        compiler_params=pltpu.CompilerParams(
            dimension_semantics=("arbitrary", "arbitrary"),
            vmem_limit_bytes=_vmem_limit(blocks, scratch)),
        name="nsa",
    )(q, gate, kc.reshape(batch, n_cmp_pad, w_kv_pad), vc.reshape(batch, n_cmp_pad, w_kv_pad),
      ks.reshape(batch, seq_len, KV_WIDTH), vst, kw.reshape(batch, seq_len, KV_WIDTH), vwt, ov)


def _pool_kernel(u_ref, w_ref, scale_ref, o_ref):
    seq = u_ref.shape[0]
    t = lax.broadcasted_iota(jnp.int32, (seq, POOL_GROUP), 0)

    def shifted(x, k):
        return jnp.where(t >= k, pltpu.roll(x, k, axis=0), 0.0)

    for gi, w in enumerate(POOL_WINDOWS):
        sl = slice(gi * POOL_GROUP, (gi + 1) * POOL_GROUP)
        x = u_ref[:, sl]
        wsum = x
        span = 1
        while span < w:
            wsum = wsum + shifted(wsum, span)
            span *= 2
        cnt = jnp.minimum(t + 1, w).astype(F32)
        pooled = (wsum / cnt - x).astype(BF16)
        o_ref[:, sl] = (_dot(pooled, w_ref[gi]) * scale_ref[:, sl]).astype(BF16)


def _pool(pool_in, pool_w, pool_scale, batch, seq_len):
    for w in POOL_WINDOWS:
        assert w & (w - 1) == 0
    blocks = (_nbytes((seq_len, POOL_WIDTH), F32) + _nbytes(pool_w.shape, BF16)
              + _nbytes((seq_len, POOL_WIDTH), BF16))
    return pl.pallas_call(
        _pool_kernel,
        grid=(batch,),
        in_specs=[
            pl.BlockSpec((seq_len, POOL_WIDTH), lambda b: (b, 0)),
            pl.BlockSpec(pool_w.shape, lambda b: (0, 0, 0)),
            pl.BlockSpec((1, POOL_WIDTH), lambda b: (0, 0)),
        ],
        out_specs=pl.BlockSpec((seq_len, POOL_WIDTH), lambda b: (b, 0)),
        out_shape=jax.ShapeDtypeStruct((batch * seq_len, POOL_WIDTH), BF16),
        compiler_params=pltpu.CompilerParams(
            dimension_semantics=("arbitrary",), vmem_limit_bytes=_vmem_limit(blocks)),
        name="pool",
    )(pool_in, pool_w.astype(BF16), pool_scale.reshape(1, -1))


def _merge_kernel(x_ref, gpre_ref, wm_ref, nsa_ref, wa_ref, pool_ref, wp_ref, wo_ref, gpost_ref,
                  o_ref):
    x = x_ref[...]
    hn = _rms(x, gpre_ref[...]).astype(BF16)
    g_attn = jax.nn.sigmoid(_dot(hn, wm_ref[:, 0:D_MODEL]))
    g_pool = jax.nn.sigmoid(_dot(hn, wm_ref[:, D_MODEL:2 * D_MODEL]))
    y = g_attn * _dot(nsa_ref[...], wa_ref[...]) + g_pool * _dot(pool_ref[...], wp_ref[...])
    h = _dot(y.astype(BF16), wo_ref[...])
    o_ref[...] = x + _rms(h, gpost_ref[...])


def _merge(x1, g_pre, w_merge, o_nsa, w_attn, o_pool, w_pool, w_out, g_post):
    t = x1.shape[0]
    row = lambda i: (i, 0)
    fixed = lambda i: (0, 0)
    blocks = (2 * _nbytes((TOK_TILE, D_MODEL), F32) + _nbytes((D_MODEL, 2 * D_MODEL), BF16)
              + _nbytes((TOK_TILE, Q_WIDTH), BF16) + _nbytes((Q_WIDTH, D_MODEL), BF16)
              + _nbytes((TOK_TILE, POOL_WIDTH), BF16) + _nbytes((POOL_WIDTH, D_MODEL), BF16)
              + _nbytes((D_MODEL, D_MODEL), BF16))
    return pl.pallas_call(
        _merge_kernel,
        grid=(t // TOK_TILE,),
        in_specs=[
            pl.BlockSpec((TOK_TILE, D_MODEL), row),
            pl.BlockSpec((1, D_MODEL), fixed),
            pl.BlockSpec((D_MODEL, 2 * D_MODEL), fixed),
            pl.BlockSpec((TOK_TILE, Q_WIDTH), row),
            pl.BlockSpec((Q_WIDTH, D_MODEL), fixed),
            pl.BlockSpec((TOK_TILE, POOL_WIDTH), row),
            pl.BlockSpec((POOL_WIDTH, D_MODEL), fixed),
            pl.BlockSpec((D_MODEL, D_MODEL), fixed),
            pl.BlockSpec((1, D_MODEL), fixed),
        ],
        out_specs=pl.BlockSpec((TOK_TILE, D_MODEL), row),
        out_shape=jax.ShapeDtypeStruct((t, D_MODEL), F32),
        compiler_params=pltpu.CompilerParams(
            dimension_semantics=("arbitrary",), vmem_limit_bytes=_vmem_limit(blocks)),
        name="merge",
    )(x1, g_pre.reshape(1, -1), w_merge.astype(BF16), o_nsa, w_attn.astype(BF16), o_pool,
      w_pool.astype(BF16), w_out.astype(BF16), g_post.reshape(1, -1))


def _layer(x, tab, tab_cmp, batch, seq_len, g_ffn1_pre, w_ffn1_gate, w_ffn1_up, w_ffn1_down,
           g_ffn1_post, g_mix_pre, w_in, cmp_pe_k, cmp_w1_k, cmp_w2_k, cmp_pe_v, cmp_w1_v, cmp_w2_v,
           w_attn_branch, pool_w, pool_scale, w_pool_branch, w_out, g_mix_post, g_ffn2_pre,
           w_ffn2_gate, w_ffn2_up, w_ffn2_down, g_ffn2_post):
    x1 = _ffn(x, g_ffn1_pre, w_ffn1_gate, w_ffn1_up, w_ffn1_down, g_ffn1_post)
    (q, ks, kw, vst, vwt, cmp_rows, gate, pool_in), w_merge = _in_proj(x1, g_mix_pre, w_in, tab)
    kc, vc = _compress(cmp_rows, cmp_pe_k, cmp_w1_k, cmp_w2_k, cmp_pe_v, cmp_w1_v, cmp_w2_v, tab_cmp)
    n_cmp_pad = seq_len // CMP_STRIDE
    kc = kc.reshape(batch, n_cmp_pad, N_KV * SLOT)
    vc = vc.reshape(batch, n_cmp_pad, N_KV * SLOT)
    o_nsa = _nsa(q, gate, kc, vc, ks, vst, kw, vwt, batch, seq_len)
    o_pool = _pool(pool_in, pool_w, pool_scale, batch, seq_len)
    x2 = _merge(x1, g_mix_pre, w_merge, o_nsa, w_attn_branch, o_pool, w_pool_branch, w_out,
                g_mix_post)
    return _ffn(x2, g_ffn2_pre, w_ffn2_gate, w_ffn2_up, w_ffn2_down, g_ffn2_post)


def kernel(x, positions, g_ffn1_pre, w_ffn1_gate, w_ffn1_up, w_ffn1_down, g_ffn1_post, g_mix_pre, w_in, cmp_pe_k, cmp_w1_k, cmp_w2_k, cmp_pe_v, cmp_w1_v, cmp_w2_v, w_attn_branch, pool_w, pool_scale, w_pool_branch, w_out, g_mix_post, g_ffn2_pre, w_ffn2_gate, w_ffn2_up, w_ffn2_down, g_ffn2_post):
    batch, seq_len, d_model = x.shape
    assert d_model == D_MODEL and seq_len % TOK_TILE == 0 and seq_len % TQ == 0
    assert TQ == KT and TQ % SEL_LEN == 0 and TOK_TILE % KT == 0 and WINDOW % KT == 0
    assert seq_len % CMP_STRIDE == 0 and CMP_LEN == 2 * CMP_STRIDE
    t = batch * seq_len
    tab = _rope_tables(positions.reshape(t), seq_len)
    n_cmp = (seq_len - CMP_LEN) // CMP_STRIDE + 1
    n_cmp_pad = seq_len // CMP_STRIDE
    tab_cmp = tab.reshape(batch, seq_len, -1)[:, CMP_LEN - 1::CMP_STRIDE]
    tab_cmp = jnp.pad(tab_cmp, ((0, 0), (0, n_cmp_pad - n_cmp), (0, 0))).reshape(batch * n_cmp_pad, -1)
    xf = x.reshape(t, D_MODEL)
    per_layer = (g_ffn1_pre, w_ffn1_gate, w_ffn1_up, w_ffn1_down, g_ffn1_post, g_mix_pre, w_in,
                 cmp_pe_k, cmp_w1_k, cmp_w2_k, cmp_pe_v, cmp_w1_v, cmp_w2_v, w_attn_branch, pool_w,
                 pool_scale, w_pool_branch, w_out, g_mix_post, g_ffn2_pre, w_ffn2_gate, w_ffn2_up,
                 w_ffn2_down, g_ffn2_post)
    for l in range(g_ffn1_pre.shape[0]):
        xf = _layer(xf, tab, tab_cmp, batch, seq_len, *(p[l] for p in per_layer))
    return xf.reshape(batch, seq_len, D_MODEL)
```

```python
import functools
import math

import jax
import jax.numpy as jnp
from jax import lax
from jax.experimental import pallas as pl
from jax.experimental.pallas import tpu as pltpu

F32 = jnp.float32
BF16 = jnp.bfloat16

D_MODEL = 1024
N_HEADS = 16
HEAD_DIM = 64
N_KV = 4
GROUP = N_HEADS // N_KV
ROT_DIM = HEAD_DIM // 4
ROT_HALF = ROT_DIM // 2
ROPE_THETA = 500000.0
CMP_LEN = 32
CMP_STRIDE = 16
CMP_HIDDEN = 2 * HEAD_DIM
SEL_LEN = 64
SEL_SHIFT = SEL_LEN.bit_length() - 1
N_SEL = 8
WINDOW = 512
POOL_WINDOWS = (2, 4, 8, 16)
POOL_WIDTH = D_MODEL // 2
POOL_GROUP = POOL_WIDTH // len(POOL_WINDOWS)
D_FF = 2816
EPS = 1e-6
NEG_INF = -1e30
FORCE_SCORE = 1e4
Q_WIDTH = N_HEADS * HEAD_DIM
KV_WIDTH = N_KV * HEAD_DIM
N_GATES = 3 * N_HEADS
LOG2_E = math.log2(math.e)

LANES = 128
BF16_SUBLANES = 16
V7X_VMEM_BYTES = 64 * 1024 * 1024

SLOT = LANES
BLK_LANE0 = HEAD_DIM
VT_ROWS = HEAD_DIM + BF16_SUBLANES
ONE_ROW = HEAD_DIM
TOK_TILE = 512
TQ = 256
KT = 256
CHAIN_COLS = 256
PIPE_DEPTH = 5
FF_CHUNK = 256
FFN_SPLIT = 2
N_ROPE_TAB = 2


def _vmem_limit(block_bytes, scratch_bytes=0):
    need = 2 * block_bytes + scratch_bytes
    return int(min(V7X_VMEM_BYTES - (4 << 20), max(2 * need, 32 << 20)))


def _nbytes(shape, dtype):
    n = 1
    for s in shape:
        n *= s
    return n * jnp.dtype(dtype).itemsize


def _rms(xf, g):
    return xf * lax.rsqrt(jnp.mean(xf * xf, axis=-1, keepdims=True) + EPS) * g


def _dot(a, b):
    return jnp.dot(a, b, preferred_element_type=F32)


def _dot_nt(a, b):
    return lax.dot_general(a, b, (((1,), (1,)), ((), ())), preferred_element_type=F32)


def _rope_trig_kernel(pos_ref, inv_ref, tab_ref):
    tn = pos_ref.shape[1]
    ang = pos_ref[...].astype(F32) * inv_ref[...]
    c, s = jnp.cos(ang), jnp.sin(ang)
    rest = HEAD_DIM - ROT_DIM
    reps = SLOT // HEAD_DIM
    cos_rows = jnp.concatenate([c, c, jnp.ones((rest, tn), F32)] * reps, axis=0)
    sin_rows = jnp.concatenate([-s, s, jnp.zeros((rest, tn), F32)] * reps, axis=0)
    for j in range(tn // LANES):
        cs = slice(j * LANES, (j + 1) * LANES)
        tab_ref[cs, 0:SLOT] = cos_rows[:, cs].T
        tab_ref[cs, SLOT:2 * SLOT] = sin_rows[:, cs].T


def _rope_tables(pos_flat, tile):
    n = pos_flat.shape[0]
    inv = ROPE_THETA ** (-jnp.arange(ROT_HALF, dtype=F32) * (2.0 / ROT_DIM))
    return pl.pallas_call(
        _rope_trig_kernel,
        grid=(n // tile,),
        in_specs=[pl.BlockSpec((1, tile), lambda i: (0, i)),
                  pl.BlockSpec((ROT_HALF, 1), lambda i: (0, 0))],
        out_specs=pl.BlockSpec((tile, N_ROPE_TAB * SLOT), lambda i: (i, 0)),
        out_shape=jax.ShapeDtypeStruct((n, N_ROPE_TAB * SLOT), F32),
        compiler_params=pltpu.CompilerParams(dimension_semantics=("arbitrary",)),
        name="rope_trig",
    )(pos_flat.reshape(1, n), inv.reshape(ROT_HALF, 1))


def _rope_slot(y, tab):
    cos_t = tab[:, 0:SLOT]
    sin_t = tab[:, SLOT:2 * SLOT]
    lane = lax.broadcasted_iota(jnp.int32, y.shape, 1) & (HEAD_DIM - 1)
    up = pltpu.roll(y, SLOT - ROT_HALF, axis=1)
    down = pltpu.roll(y, ROT_HALF, axis=1)
    return y * cos_t + jnp.where(lane < ROT_HALF, up, down) * sin_t


def _ffn_kernel(x_ref, gpre_ref, wg_ref, wu_ref, wd_ref, gpost_ref, o_ref, acc_ref):
    half = x_ref.shape[0] // FFN_SPLIT
    parts = [slice(k * half, (k + 1) * half) for k in range(FFN_SPLIT)]
    hn = [_rms(x_ref[rs, :], gpre_ref[...]).astype(BF16) for rs in parts]
    for k, rs in enumerate(parts):
        for c in range(D_FF // FF_CHUNK):
            sl = slice(c * FF_CHUNK, (c + 1) * FF_CHUNK)
            g = _dot(hn[k], wg_ref[:, sl])
            u = _dot(hn[k], wu_ref[:, sl])
            a = (g * jax.nn.sigmoid(g) * u).astype(BF16)
            d = _dot(a, wd_ref[sl, :])
            if c == 0:
                acc_ref[rs, :] = d
            else:
                acc_ref[rs, :] += d
        o_ref[rs, :] = x_ref[rs, :] + 0.5 * _rms(acc_ref[rs, :], gpost_ref[...])


def _ffn(x, g_pre, w_gate, w_up, w_down, g_post):
    t = x.shape[0]
    tile = FFN_SPLIT * TOK_TILE
    row = lambda i: (i, 0)
    fixed = lambda i: (0, 0)
    once = pl.Buffered(1)
    weights = 3 * _nbytes((D_MODEL, D_FF), BF16)
    blocks = 2 * _nbytes((tile, D_MODEL), F32)
    return pl.pallas_call(
        _ffn_kernel,
        grid=(t // tile,),
        in_specs=[
            pl.BlockSpec((tile, D_MODEL), row),
            pl.BlockSpec((1, D_MODEL), fixed),
            pl.BlockSpec((D_MODEL, D_FF), fixed, pipeline_mode=once),
            pl.BlockSpec((D_MODEL, D_FF), fixed, pipeline_mode=once),
            pl.BlockSpec((D_FF, D_MODEL), fixed, pipeline_mode=once),
            pl.BlockSpec((1, D_MODEL), fixed),
        ],
        out_specs=pl.BlockSpec((tile, D_MODEL), row),
        out_shape=jax.ShapeDtypeStruct((t, D_MODEL), F32),
        scratch_shapes=[pltpu.VMEM((tile, D_MODEL), F32)],
        compiler_params=pltpu.CompilerParams(
            dimension_semantics=("arbitrary",),
            vmem_limit_bytes=_vmem_limit(blocks, weights + _nbytes((tile, D_MODEL), F32))),
        name="ffn",
    )(x, g_pre.reshape(1, -1), w_gate.astype(BF16), w_up.astype(BF16), w_down.astype(BF16),
      g_post.reshape(1, -1))


OFF_Q = 0
OFF_CMP = OFF_Q + Q_WIDTH
OFF_KS = OFF_CMP + 2 * KV_WIDTH
OFF_VS = OFF_KS + KV_WIDTH
OFF_KW = OFF_VS + KV_WIDTH
OFF_VW = OFF_KW + KV_WIDTH
OFF_GATE = OFF_VW + KV_WIDTH
OFF_POOL = OFF_GATE + N_GATES
OFF_MERGE = OFF_POOL + POOL_WIDTH
W_CMP = 2 * KV_WIDTH
W_GP = SLOT + POOL_WIDTH
CHUNK_TOK = CMP_STRIDE
VT_SLAB = N_KV * VT_ROWS


def _in_proj_kernel(x_ref, g_ref, w_ref, wgp_ref, wvt_ref, tab_ref, q_ref, ks_ref, kw_ref, vst_ref,
                    vwt_ref, cmp_ref, gate_ref, pool_ref, cmp_scr):
    hn = _rms(x_ref[...], g_ref[...]).astype(BF16)
    tab = tab_ref[...]
    rows = x_ref.shape[0]
    wide = 4 * SLOT
    tab_q = tab * (HEAD_DIM ** -0.5 * LOG2_E)
    for c in range(Q_WIDTH // wide):
        y = _dot(hn, w_ref[:, OFF_Q + c * wide:OFF_Q + (c + 1) * wide])
        for j in range(wide // SLOT):
            sl = slice(c * wide + j * SLOT, c * wide + (j + 1) * SLOT)
            q_ref[:, sl] = _rope_slot(y[:, j * SLOT:(j + 1) * SLOT], tab_q).astype(BF16)
    for k_ref, off in ((ks_ref, OFF_KS), (kw_ref, OFF_KW)):
        y = _dot(hn, w_ref[:, off:off + KV_WIDTH])
        for j in range(KV_WIDTH // SLOT):
            sl = slice(j * SLOT, (j + 1) * SLOT)
            k_ref[:, sl] = _rope_slot(y[:, sl], tab).astype(BF16)
    y = _dot(hn, w_ref[:, OFF_CMP:OFF_CMP + W_CMP])
    for j in range(W_CMP // LANES):
        cmp_scr[j] = y[:, j * LANES:(j + 1) * LANES]
    for l in range(CHUNK_TOK):
        for j in range(W_CMP // LANES):
            c0 = l * W_CMP + j * LANES
            cmp_ref[:, c0:c0 + LANES] = (
                cmp_scr[j, pl.ds(l, rows // CHUNK_TOK, stride=CHUNK_TOK), :].astype(BF16))
    y = _dot(hn, wgp_ref[...])
    gate_ref[...] = y[:, 0:SLOT]
    pool_ref[...] = y[:, SLOT:W_GP]
    aux = (lax.broadcasted_iota(jnp.int32, (VT_ROWS - HEAD_DIM, KT), 0) == 0).astype(BF16)
    for vt_ref, r0 in ((vst_ref, 0), (vwt_ref, KV_WIDTH)):
        vt = _dot_nt(wvt_ref[r0:r0 + KV_WIDTH, :], hn).astype(BF16)
        for j in range(rows // KT):
            for g in range(N_KV):
                vt_ref[j, g * VT_ROWS:g * VT_ROWS + HEAD_DIM, :] = (
                    vt[g * HEAD_DIM:(g + 1) * HEAD_DIM, j * KT:(j + 1) * KT])
                vt_ref[j, g * VT_ROWS + HEAD_DIM:(g + 1) * VT_ROWS, :] = aux


def _in_proj(x1, g_pre, w_in, tab):
    t = x1.shape[0]
    w_main = w_in[:, :OFF_GATE].astype(BF16)
    w_gp = jnp.concatenate([jnp.pad(w_in[:, OFF_GATE:OFF_POOL], ((0, 0), (0, SLOT - N_GATES))),
                            w_in[:, OFF_POOL:OFF_MERGE]], axis=1).astype(BF16)
    w_vt = jnp.concatenate([w_in[:, OFF_VS:OFF_VS + KV_WIDTH], w_in[:, OFF_VW:OFF_VW + KV_WIDTH]],
                           axis=1).T.astype(BF16)
    row = lambda i: (i, 0)
    fixed = lambda i: (0, 0)
    tiles = TOK_TILE // KT
    chunk_rows = TOK_TILE // CHUNK_TOK
    slab = pl.BlockSpec((tiles, VT_SLAB, KT), lambda i: (i, 0, 0))
    slab_shape = jax.ShapeDtypeStruct((t // KT, VT_SLAB, KT), BF16)
    out_specs = [
        pl.BlockSpec((TOK_TILE, Q_WIDTH), row), pl.BlockSpec((TOK_TILE, KV_WIDTH), row),
        pl.BlockSpec((TOK_TILE, KV_WIDTH), row), slab, slab,
        pl.BlockSpec((chunk_rows, CHUNK_TOK * W_CMP), row),
        pl.BlockSpec((TOK_TILE, SLOT), row), pl.BlockSpec((TOK_TILE, POOL_WIDTH), row)]
    out_shape = [
        jax.ShapeDtypeStruct((t, Q_WIDTH), BF16), jax.ShapeDtypeStruct((t, KV_WIDTH), BF16),
        jax.ShapeDtypeStruct((t, KV_WIDTH), BF16), slab_shape, slab_shape,
        jax.ShapeDtypeStruct((t // CHUNK_TOK, CHUNK_TOK * W_CMP), BF16),
        jax.ShapeDtypeStruct((t, SLOT), F32), jax.ShapeDtypeStruct((t, POOL_WIDTH), F32)]
    blocks = (_nbytes((TOK_TILE, D_MODEL), F32) + _nbytes(w_main.shape, BF16)
              + _nbytes(w_gp.shape, BF16) + _nbytes(w_vt.shape, BF16)
              + _nbytes((TOK_TILE, N_ROPE_TAB * SLOT), F32)
              + _nbytes((TOK_TILE, Q_WIDTH + 2 * KV_WIDTH + W_CMP), BF16)
              + 2 * _nbytes((tiles, VT_SLAB, KT), BF16) + _nbytes((TOK_TILE, SLOT + POOL_WIDTH), F32))
    res = pl.pallas_call(
        _in_proj_kernel,
        grid=(t // TOK_TILE,),
        in_specs=[
            pl.BlockSpec((TOK_TILE, D_MODEL), row),
            pl.BlockSpec((1, D_MODEL), fixed),
            pl.BlockSpec(w_main.shape, fixed),
            pl.BlockSpec(w_gp.shape, fixed),
            pl.BlockSpec(w_vt.shape, fixed),
            pl.BlockSpec((TOK_TILE, N_ROPE_TAB * SLOT), row),
        ],
        out_specs=out_specs,
        out_shape=out_shape,
        scratch_shapes=[pltpu.VMEM((W_CMP // LANES, TOK_TILE, LANES), F32)],
        compiler_params=pltpu.CompilerParams(
            dimension_semantics=("arbitrary",),
            vmem_limit_bytes=_vmem_limit(blocks, _nbytes((TOK_TILE, W_CMP), F32))),
        name="in_proj",
    )(x1, g_pre.reshape(1, -1), w_main, w_gp, w_vt, tab)
    return res, w_in[:, OFF_MERGE:]


def _compress_kernel(x_ref, w1k_ref, w1v_ref, pek_ref, pev_ref, w1k_raw_ref, w1v_raw_ref,
                     w2k_ref, w2v_ref, tab_ref, kc_ref, vc_ref, acck_ref, accv_ref):
    l = pl.program_id(0)

    @pl.when(l == 0)
    def _():
        acck_ref[...] = jnp.zeros(acck_ref.shape, F32)
        accv_ref[...] = jnp.zeros(accv_ref.shape, F32)

    for acc_ref, w1_ref, c0 in ((acck_ref, w1k_ref, 0), (accv_ref, w1v_ref, KV_WIDTH)):
        for g in range(N_KV):
            acc_ref[g] += _dot(x_ref[:, c0 + g * HEAD_DIM:c0 + (g + 1) * HEAD_DIM], w1_ref[0])

    @pl.when(l == CHUNK_TOK - 1)
    def _():
        rows = acck_ref.shape[1]
        tab = tab_ref[...]
        for acc_ref, pe_ref, w1_raw_ref, w2_ref, o_ref, rope in (
                (acck_ref, pek_ref, w1k_raw_ref, w2k_ref, kc_ref, True),
                (accv_ref, pev_ref, w1v_raw_ref, w2v_ref, vc_ref, False)):
            bias = _dot(pe_ref[...], w1_raw_ref[...])[0:1, :]
            for g in range(N_KV):
                first = acc_ref[g, :, 0:CMP_HIDDEN]
                second = pltpu.roll(acc_ref[g, :, CMP_HIDDEN:2 * CMP_HIDDEN], rows - 1, axis=0)
                hid = jax.nn.gelu(first + second + bias).astype(BF16)
                out = _dot(hid, w2_ref[...])
                if rope:
                    out = _rope_slot(out, tab)
                o_ref[:, g * SLOT:(g + 1) * SLOT] = out.astype(BF16)


def _compress_weights(w1, w2):
    w1 = w1.reshape(2, CHUNK_TOK, HEAD_DIM, CMP_HIDDEN)
    w1_tok = jnp.concatenate([w1[0], w1[1]], axis=-1).astype(BF16)
    return w1_tok, jnp.pad(w2, ((0, 0), (0, SLOT - HEAD_DIM))).astype(BF16)


def _compress(cmp_rows, cmp_pe_k, cmp_w1_k, cmp_w2_k, cmp_pe_v, cmp_w1_v, cmp_w2_v, tab_cmp):
    rows = cmp_rows.shape[0]
    w1k, w2k = _compress_weights(cmp_w1_k, cmp_w2_k)
    w1v, w2v = _compress_weights(cmp_w1_v, cmp_w2_v)
    sub = 8
    pek = jnp.broadcast_to(cmp_pe_k.reshape(1, -1), (sub, CMP_LEN * HEAD_DIM)).astype(BF16)
    pev = jnp.broadcast_to(cmp_pe_v.reshape(1, -1), (sub, CMP_LEN * HEAD_DIM)).astype(BF16)
    fixed = lambda l: (0, 0)
    w_kv_pad = N_KV * SLOT
    blocks = (_nbytes((rows, W_CMP), BF16) + 2 * _nbytes((HEAD_DIM, 2 * CMP_HIDDEN), BF16)
              + 2 * _nbytes((sub + CMP_LEN * HEAD_DIM, CMP_HIDDEN), BF16)
              + 2 * _nbytes((CMP_HIDDEN, SLOT), BF16)
              + _nbytes((rows, N_ROPE_TAB * SLOT), F32) + 2 * _nbytes((rows, w_kv_pad), BF16))
    acc = pltpu.VMEM((N_KV, rows, 2 * CMP_HIDDEN), F32)
    return pl.pallas_call(
        _compress_kernel,
        grid=(CHUNK_TOK,),
        in_specs=[
            pl.BlockSpec((rows, W_CMP), lambda l: (0, l)),
            pl.BlockSpec((1, HEAD_DIM, 2 * CMP_HIDDEN), lambda l: (l, 0, 0)),
            pl.BlockSpec((1, HEAD_DIM, 2 * CMP_HIDDEN), lambda l: (l, 0, 0)),
            pl.BlockSpec((sub, CMP_LEN * HEAD_DIM), fixed),
            pl.BlockSpec((sub, CMP_LEN * HEAD_DIM), fixed),
            pl.BlockSpec((CMP_LEN * HEAD_DIM, CMP_HIDDEN), fixed),
            pl.BlockSpec((CMP_LEN * HEAD_DIM, CMP_HIDDEN), fixed),
            pl.BlockSpec((CMP_HIDDEN, SLOT), fixed),
            pl.BlockSpec((CMP_HIDDEN, SLOT), fixed),
            pl.BlockSpec((rows, N_ROPE_TAB * SLOT), fixed),
        ],
        out_specs=[pl.BlockSpec((rows, w_kv_pad), fixed)] * 2,
        out_shape=[jax.ShapeDtypeStruct((rows, w_kv_pad), BF16)] * 2,
        scratch_shapes=[acc, acc],
        compiler_params=pltpu.CompilerParams(
            dimension_semantics=("arbitrary",),
            vmem_limit_bytes=_vmem_limit(blocks, 2 * _nbytes(acc.shape, F32))),
        name="compress",
    )(cmp_rows, w1k, w1v, pek, pev, cmp_w1_k.astype(BF16), cmp_w1_v.astype(BF16), w2k, w2v, tab_cmp)


def _nsa_kernel(q_ref, gate_ref, kc_ref, vc_ref, ks_ref, vst_ref, kw_ref, vwt_ref, ov_ref,
                o_ref, ksp_ref, kwp_ref, qg_ref, mix_ref, *state_refs, n_slc):
    i = pl.program_id(1)
    start = i * TQ
    seq = ks_ref.shape[1]
    cols_g = GROUP * TQ
    chains_g = cols_g // CHAIN_COLS
    n_chains = N_KV * chains_g
    n_cmp = kc_ref.shape[1]
    sel_state = (state_refs[0:n_chains], state_refs[n_chains:2 * n_chains])
    win_state = (state_refs[2 * n_chains:3 * n_chains], state_refs[3 * n_chains:4 * n_chains])
    lane_q = lax.broadcasted_iota(jnp.int32, (TQ, SLOT), 1)

    def head_slot(x_ref, rows, j):
        two = x_ref[rows, (j // 2) * SLOT:(j // 2 + 1) * SLOT].astype(F32)
        return pltpu.roll(two, HEAD_DIM, axis=1) if j % 2 else two

    @pl.when(i == 0)
    def _():
        for c in range(seq // KT):
            rows = pl.ds(c * KT, KT)
            lane = lax.broadcasted_iota(jnp.int32, (KT, SLOT), 1)
            key_blk = (c * KT + lax.broadcasted_iota(jnp.int32, (KT, SLOT), 0)) >> SEL_SHIFT
            onehot = (lane - BLK_LANE0 == key_blk).astype(F32)
            for g in range(N_KV):
                gs = slice(g * SLOT, (g + 1) * SLOT)
                ksp_ref[rows, gs] = jnp.where(lane < HEAD_DIM, head_slot(ks_ref.at[0], rows, g),
                                              onehot).astype(BF16)
                kwp_ref[rows, gs] = jnp.where(lane < HEAD_DIM, head_slot(kw_ref.at[0], rows, g),
                                              0.0).astype(BF16)

    for h in range(N_HEADS):
        g, r = divmod(h, GROUP)
        qg_ref[g, r * TQ:(r + 1) * TQ, :] = jnp.where(
            lane_q < HEAD_DIM, head_slot(q_ref, slice(None), h), 0.0).astype(BF16)

    gates_t = jax.nn.sigmoid(gate_ref[...]).T
    any_cmp = start + lax.broadcasted_iota(jnp.int32, (1, TQ), 1) >= CMP_LEN - 1
    key_in = lax.broadcasted_iota(jnp.int32, (KT, CHAIN_COLS), 0)
    qry_in = lax.broadcasted_iota(jnp.int32, (KT, CHAIN_COLS), 1) & (TQ - 1)
    diag_mask = key_in <= qry_in
    far_mask = key_in > qry_in

    def gate_row(branch, h):
        c = branch * N_HEADS + h
        return gates_t[c:c + 1, :]

    sub = 8
    cmp_per_sub = sub * SEL_LEN // CMP_STRIDE

    def compressed_and_select(ng):
        n_blk, n_c = ng * sub, ng * cmp_per_sub
        t_cmp = start + lax.broadcasted_iota(jnp.int32, (n_c, TQ), 1)
        n_idx = lax.broadcasted_iota(jnp.int32, (n_c, TQ), 0)
        cmp_valid = n_idx * CMP_STRIDE + (CMP_LEN - 1) <= t_cmp
        t_row = start + lax.broadcasted_iota(jnp.int32, (n_blk, TQ), 1)
        blk = lax.broadcasted_iota(jnp.int32, (n_blk, TQ), 0)
        forced = (blk == t_row >> SEL_SHIFT) | (blk == 0)
        causal_blk = blk * SEL_LEN <= t_row
        idx8 = lax.broadcasted_iota(jnp.int32, (sub, TQ), 0)
        pad_c = jnp.zeros((n_cmp - n_c, TQ), F32)
        cmp_scores = []
        for h in range(N_HEADS):
            g, r = divmod(h, GROUP)
            cmp_scores.append(_dot_nt(kc_ref[0, 0:n_c, g * SLOT:(g + 1) * SLOT],
                                      qg_ref[g, r * TQ:(r + 1) * TQ, :]))
        for g in range(N_KV):
            vct_g = vc_ref[0, :, g * SLOT:(g + 1) * SLOT].astype(F32).T[0:HEAD_DIM].astype(BF16)
            p_sum = jnp.zeros((n_c, TQ), F32)
            for r in range(GROUP):
                h = g * GROUP + r
                s = jnp.where(cmp_valid, cmp_scores[h], NEG_INF)
                p = jnp.exp2(s - jnp.max(s, axis=0, keepdims=True))
                inv = jnp.where(any_cmp, 1.0 / jnp.sum(p, axis=0, keepdims=True), 0.0)
                p = p * inv
                p_sum = p_sum + p
                p_all = jnp.concatenate([p, pad_c], axis=0) if ng * cmp_per_sub < n_cmp else p
                mix_ref[h] = gate_row(0, h) * _dot(vct_g, p_all.astype(BF16))
            p_sum_all = jnp.concatenate([p_sum, pad_c], axis=0) if ng * cmp_per_sub < n_cmp else p_sum
            imp = jnp.dot(ov_ref[0:n_blk, :], p_sum_all, precision=lax.Precision.HIGHEST,
                          preferred_element_type=F32)
            score = jnp.where(causal_blk, jnp.where(forced, FORCE_SCORE, imp), NEG_INF)
            rows8 = [score[k * sub:(k + 1) * sub] for k in range(ng)]
            ranks = [jnp.zeros((sub, TQ), jnp.int32) for _ in rows8]
            for c in range(n_blk):
                other = score[c:c + 1, :]
                for k, mine in enumerate(rows8):
                    if k * sub > c:
                        beats = other >= mine
                    elif (k + 1) * sub - 1 <= c:
                        beats = other > mine
                    else:
                        beats = (other > mine) | ((other == mine) & (idx8 > c - k * sub))
                    ranks[k] = ranks[k] + beats.astype(jnp.int32)
            rank = jnp.concatenate(ranks, axis=0) if ng > 1 else ranks[0]
            bias = jnp.where(causal_blk & (rank < N_SEL), 0.0, NEG_INF)
            bias_t = jnp.concatenate([jnp.zeros((BLK_LANE0, TQ), F32), bias,
                                      jnp.zeros((SLOT - BLK_LANE0 - n_blk, TQ), F32)], axis=0)
            bias_q = bias_t.T.astype(BF16)
            for r in range(GROUP):
                rs = slice(r * TQ, (r + 1) * TQ)
                qg_ref[g, rs, :] = jnp.where(lane_q < HEAD_DIM, qg_ref[g, rs, :], bias_q)

    ng_max = n_slc // sub
    ng_now = ((i + 1) * TQ + sub * SEL_LEN - 1) // (sub * SEL_LEN)
    for ng in range(1, ng_max + 1):
        pl.when(ng_now == ng)(functools.partial(compressed_and_select, ng))

    def flash_init(state):
        for m_ref, acc_ref in zip(*state):
            m_ref[...] = jnp.full(m_ref.shape, NEG_INF, F32)
            acc_ref[...] = jnp.zeros(acc_ref.shape, F32)

    def flash_tiles(tiles):
        jobs = [(tile, c) for tile in tiles for c in range(n_chains)]
        scores = {}
        for step in range(len(jobs) + PIPE_DEPTH):
            if step < len(jobs):
                (kp_ref, _, kt, mask, _), chain = jobs[step]
                g, c = divmod(chain, chains_g)
                off = pl.multiple_of(kt * KT, KT)
                s = _dot_nt(kp_ref[pl.ds(off, KT), g * SLOT:(g + 1) * SLOT],
                            qg_ref[g, c * CHAIN_COLS:(c + 1) * CHAIN_COLS, :])
                scores[step] = s if mask is None else jnp.where(mask, s, NEG_INF)
            done = step - PIPE_DEPTH
            if done >= 0:
                (_, vt_ref, kt, _, (m_refs, acc_refs)), chain = jobs[done]
                g = chain // chains_g
                m_ref, acc_ref = m_refs[chain], acc_refs[chain]
                s = scores.pop(done)
                m_old = m_ref[...]
                m_new = jnp.maximum(m_old, jnp.max(s, axis=0, keepdims=True))
                p = jnp.exp2(s - m_new).astype(BF16)
                pv = _dot(vt_ref[kt, g * VT_ROWS:(g + 1) * VT_ROWS, :], p)
                acc_ref[...] = jnp.exp2(m_old - m_new) * acc_ref[...] + pv
                m_ref[...] = m_new

    def flash_mix(branch, state):
        _, acc_refs = state
        for h in range(N_HEADS):
            g, r = divmod(h, GROUP)
            c, c0 = divmod(r * TQ, CHAIN_COLS)
            acc = acc_refs[g * chains_g + c][:, c0:c0 + TQ]
            scale = gate_row(branch, h) * (1.0 / acc[ONE_ROW:ONE_ROW + 1, :])
            mix_ref[h] = mix_ref[h] + scale * acc[0:HEAD_DIM]

    flash_init(sel_state)
    flash_init(win_state)

    def sel_past(kt):
        return (ksp_ref, vst_ref, kt, None, sel_state)

    def sel_pair(j, carry):
        flash_tiles([sel_past(2 * j), sel_past(2 * j + 1)])
        return carry

    lax.fori_loop(0, i >> 1, sel_pair, 0)

    @pl.when(i & 1 == 1)
    def _():
        flash_tiles([sel_past(i - 1)])

    n_back = WINDOW // KT
    sel_diag = (ksp_ref, vst_ref, i, diag_mask, sel_state)
    win_diag = (kwp_ref, vwt_ref, i, diag_mask, win_state)

    def win_back(d):
        return (kwp_ref, vwt_ref, i - d, far_mask if d == n_back else None, win_state)

    for have in range(n_back + 1):
        cond = (i == have) if have < n_back else (i >= have)

        @pl.when(cond)
        def _(have=have):
            flash_tiles([sel_diag] + [win_back(d) for d in range(have, 0, -1)] + [win_diag])

    flash_mix(1, sel_state)
    flash_mix(2, win_state)
    for h2 in range(N_HEADS // 2):
        pair = jnp.concatenate([mix_ref[2 * h2], mix_ref[2 * h2 + 1]], axis=0)
        o_ref[:, h2 * SLOT:(h2 + 1) * SLOT] = pair.T.astype(BF16)


def _overlap(n_cmp_pad, n_slc):
    c0 = jnp.arange(n_cmp_pad) * CMP_STRIDE
    s0 = jnp.arange(n_slc) * SEL_LEN
    ov = jnp.minimum(c0[None, :] + CMP_LEN, s0[:, None] + SEL_LEN) - jnp.maximum(c0[None, :], s0[:, None])
    return jnp.clip(ov, 0).astype(F32) / CMP_LEN


def _nsa(q, gate, kc, vc, ks, vst, kw, vwt, batch, seq_len):
    n_slc = seq_len // SEL_LEN
    n_cmp_pad = kc.shape[1]
    w_kv_pad = N_KV * SLOT
    assert n_slc <= SLOT - BLK_LANE0 and n_slc % 8 == 0
    assert CHAIN_COLS % TQ == 0 and (GROUP * TQ) % CHAIN_COLS == 0
    ov = _overlap(n_cmp_pad, n_slc)
    nq = seq_len // TQ
    n_kt = seq_len // KT
    qrow = lambda b, i: (b * nq + i, 0)
    per_b = lambda b, i: (b, 0, 0)
    fixed = lambda b, i: (0, 0)
    blocks = (_nbytes((TQ, Q_WIDTH), BF16) + _nbytes((TQ, SLOT), F32)
              + 2 * _nbytes((n_cmp_pad, w_kv_pad), BF16) + 2 * _nbytes((seq_len, KV_WIDTH), BF16)
              + 2 * _nbytes((n_kt, VT_SLAB, KT), BF16)
              + _nbytes(ov.shape, F32) + _nbytes((TQ, Q_WIDTH), BF16))
    n_chains = N_KV * GROUP * TQ // CHAIN_COLS
    run_max = [pltpu.VMEM((1, CHAIN_COLS), F32)] * n_chains
    run_acc = [pltpu.VMEM((VT_ROWS, CHAIN_COLS), F32)] * n_chains
    scratch_shapes = (
        [pltpu.VMEM((seq_len, w_kv_pad), BF16),
         pltpu.VMEM((seq_len, w_kv_pad), BF16),
         pltpu.VMEM((N_KV, GROUP * TQ, SLOT), BF16),
         pltpu.VMEM((N_HEADS, HEAD_DIM, TQ), F32)]
        + run_max + run_acc + run_max + run_acc)
    scratch = sum(_nbytes(s.shape, s.dtype) for s in scratch_shapes)
    return pl.pallas_call(
        functools.partial(_nsa_kernel, n_slc=n_slc),
        grid=(batch, nq),
        in_specs=[
            pl.BlockSpec((TQ, Q_WIDTH), qrow),
            pl.BlockSpec((TQ, SLOT), qrow),
            pl.BlockSpec((1, n_cmp_pad, w_kv_pad), per_b),
            pl.BlockSpec((1, n_cmp_pad, w_kv_pad), per_b),
            pl.BlockSpec((1, seq_len, KV_WIDTH), per_b),
            pl.BlockSpec((n_kt, VT_SLAB, KT), per_b),
            pl.BlockSpec((1, seq_len, KV_WIDTH), per_b),
            pl.BlockSpec((n_kt, VT_SLAB, KT), per_b),
            pl.BlockSpec(ov.shape, fixed),
        ],
        out_specs=pl.BlockSpec((TQ, Q_WIDTH), qrow),
        out_shape=jax.ShapeDtypeStruct((batch * seq_len, Q_WIDTH), BF16),
        scratch_shapes=scratch_shapes,
        compiler_params=pltpu.CompilerParams(
            dimension_semantics=("arbitrary", "arbitrary"),
            vmem_limit_bytes=_vmem_limit(blocks, scratch)),
        name="nsa",
    )(q, gate, kc.reshape(batch, n_cmp_pad, w_kv_pad), vc.reshape(batch, n_cmp_pad, w_kv_pad),
      ks.reshape(batch, seq_len, KV_WIDTH), vst, kw.reshape(batch, seq_len, KV_WIDTH), vwt, ov)


def _pool_kernel(u_ref, w_ref, scale_ref, o_ref):
    seq = u_ref.shape[0]
    t = lax.broadcasted_iota(jnp.int32, (seq, POOL_GROUP), 0)

    def shifted(x, k):
        return jnp.where(t >= k, pltpu.roll(x, k, axis=0), 0.0)

    for gi, w in enumerate(POOL_WINDOWS):
        sl = slice(gi * POOL_GROUP, (gi + 1) * POOL_GROUP)
        x = u_ref[:, sl]
        wsum = x
        span = 1
        while span < w:
            wsum = wsum + shifted(wsum, span)
            span *= 2
        cnt = jnp.minimum(t + 1, w).astype(F32)
        pooled = (wsum / cnt - x).astype(BF16)
        o_ref[:, sl] = (_dot(pooled, w_ref[gi]) * scale_ref[:, sl]).astype(BF16)


def _pool(pool_in, pool_w, pool_scale, batch, seq_len):
    for w in POOL_WINDOWS:
        assert w & (w - 1) == 0
    blocks = (_nbytes((seq_len, POOL_WIDTH), F32) + _nbytes(pool_w.shape, BF16)
              + _nbytes((seq_len, POOL_WIDTH), BF16))
    return pl.pallas_call(
        _pool_kernel,
        grid=(batch,),
        in_specs=[
            pl.BlockSpec((seq_len, POOL_WIDTH), lambda b: (b, 0)),
            pl.BlockSpec(pool_w.shape, lambda b: (0, 0, 0)),
            pl.BlockSpec((1, POOL_WIDTH), lambda b: (0, 0)),
        ],
        out_specs=pl.BlockSpec((seq_len, POOL_WIDTH), lambda b: (b, 0)),
        out_shape=jax.ShapeDtypeStruct((batch * seq_len, POOL_WIDTH), BF16),
        compiler_params=pltpu.CompilerParams(
            dimension_semantics=("arbitrary",), vmem_limit_bytes=_vmem_limit(blocks)),
        name="pool",
    )(pool_in, pool_w.astype(BF16), pool_scale.reshape(1, -1))


def _merge_kernel(x_ref, gpre_ref, wm_ref, nsa_ref, wa_ref, pool_ref, wp_ref, wo_ref, gpost_ref,
                  o_ref):
    x = x_ref[...]
    hn = _rms(x, gpre_ref[...]).astype(BF16)
    g_attn = jax.nn.sigmoid(_dot(hn, wm_ref[:, 0:D_MODEL]))
    g_pool = jax.nn.sigmoid(_dot(hn, wm_ref[:, D_MODEL:2 * D_MODEL]))
    y = g_attn * _dot(nsa_ref[...], wa_ref[...]) + g_pool * _dot(pool_ref[...], wp_ref[...])
    h = _dot(y.astype(BF16), wo_ref[...])
    o_ref[...] = x + _rms(h, gpost_ref[...])


def _merge(x1, g_pre, w_merge, o_nsa, w_attn, o_pool, w_pool, w_out, g_post):
    t = x1.shape[0]
    row = lambda i: (i, 0)
    fixed = lambda i: (0, 0)
    blocks = (2 * _nbytes((TOK_TILE, D_MODEL), F32) + _nbytes((D_MODEL, 2 * D_MODEL), BF16)
              + _nbytes((TOK_TILE, Q_WIDTH), BF16) + _nbytes((Q_WIDTH, D_MODEL), BF16)
              + _nbytes((TOK_TILE, POOL_WIDTH), BF16) + _nbytes((POOL_WIDTH, D_MODEL), BF16)
              + _nbytes((D_MODEL, D_MODEL), BF16))
    return pl.pallas_call(
        _merge_kernel,
        grid=(t // TOK_TILE,),
        in_specs=[
            pl.BlockSpec((TOK_TILE, D_MODEL), row),
            pl.BlockSpec((1, D_MODEL), fixed),
            pl.BlockSpec((D_MODEL, 2 * D_MODEL), fixed),
            pl.BlockSpec((TOK_TILE, Q_WIDTH), row),
            pl.BlockSpec((Q_WIDTH, D_MODEL), fixed),
            pl.BlockSpec((TOK_TILE, POOL_WIDTH), row),
            pl.BlockSpec((POOL_WIDTH, D_MODEL), fixed),
            pl.BlockSpec((D_MODEL, D_MODEL), fixed),
            pl.BlockSpec((1, D_MODEL), fixed),
        ],
        out_specs=pl.BlockSpec((TOK_TILE, D_MODEL), row),
        out_shape=jax.ShapeDtypeStruct((t, D_MODEL), F32),
        compiler_params=pltpu.CompilerParams(
            dimension_semantics=("arbitrary",), vmem_limit_bytes=_vmem_limit(blocks)),
        name="merge",
    )(x1, g_pre.reshape(1, -1), w_merge.astype(BF16), o_nsa, w_attn.astype(BF16), o_pool,
      w_pool.astype(BF16), w_out.astype(BF16), g_post.reshape(1, -1))


def _layer(x, tab, tab_cmp, batch, seq_len, g_ffn1_pre, w_ffn1_gate, w_ffn1_up, w_ffn1_down,
           g_ffn1_post, g_mix_pre, w_in, cmp_pe_k, cmp_w1_k, cmp_w2_k, cmp_pe_v, cmp_w1_v, cmp_w2_v,
           w_attn_branch, pool_w, pool_scale, w_pool_branch, w_out, g_mix_post, g_ffn2_pre,
           w_ffn2_gate, w_ffn2_up, w_ffn2_down, g_ffn2_post):
    x1 = _ffn(x, g_ffn1_pre, w_ffn1_gate, w_ffn1_up, w_ffn1_down, g_ffn1_post)
    (q, ks, kw, vst, vwt, cmp_rows, gate, pool_in), w_merge = _in_proj(x1, g_mix_pre, w_in, tab)
    kc, vc = _compress(cmp_rows, cmp_pe_k, cmp_w1_k, cmp_w2_k, cmp_pe_v, cmp_w1_v, cmp_w2_v, tab_cmp)
    n_cmp_pad = seq_len // CMP_STRIDE
    kc = kc.reshape(batch, n_cmp_pad, N_KV * SLOT)
    vc = vc.reshape(batch, n_cmp_pad, N_KV * SLOT)
    o_nsa = _nsa(q, gate, kc, vc, ks, vst, kw, vwt, batch, seq_len)
    o_pool = _pool(pool_in, pool_w, pool_scale, batch, seq_len)
    x2 = _merge(x1, g_mix_pre, w_merge, o_nsa, w_attn_branch, o_pool, w_pool_branch, w_out,
                g_mix_post)
    return _ffn(x2, g_ffn2_pre, w_ffn2_gate, w_ffn2_up, w_ffn2_down, g_ffn2_post)


def kernel(x, positions, g_ffn1_pre, w_ffn1_gate, w_ffn1_up, w_ffn1_down, g_ffn1_post, g_mix_pre, w_in, cmp_pe_k, cmp_w1_k, cmp_w2_k, cmp_pe_v, cmp_w1_v, cmp_w2_v, w_attn_branch, pool_w, pool_scale, w_pool_branch, w_out, g_mix_post, g_ffn2_pre, w_ffn2_gate, w_ffn2_up, w_ffn2_down, g_ffn2_post):
    batch, seq_len, d_model = x.shape
    assert d_model == D_MODEL and seq_len % TOK_TILE == 0 and seq_len % TQ == 0
    assert TQ == KT and TQ % SEL_LEN == 0 and TOK_TILE % KT == 0 and WINDOW % KT == 0
    assert seq_len % CMP_STRIDE == 0 and CMP_LEN == 2 * CMP_STRIDE
    t = batch * seq_len
    tab = _rope_tables(positions.reshape(t), seq_len)
    n_cmp = (seq_len - CMP_LEN) // CMP_STRIDE + 1
    n_cmp_pad = seq_len // CMP_STRIDE
    pos_cmp = jnp.pad(positions[:, CMP_LEN - 1::CMP_STRIDE], ((0, 0), (0, n_cmp_pad - n_cmp)))
    tab_cmp = _rope_tables(pos_cmp.reshape(batch * n_cmp_pad), batch * n_cmp_pad)
    xf = x.reshape(t, D_MODEL)
    per_layer = (g_ffn1_pre, w_ffn1_gate, w_ffn1_up, w_ffn1_down, g_ffn1_post, g_mix_pre, w_in,
                 cmp_pe_k, cmp_w1_k, cmp_w2_k, cmp_pe_v, cmp_w1_v, cmp_w2_v, w_attn_branch, pool_w,
                 pool_scale, w_pool_branch, w_out, g_mix_post, g_ffn2_pre, w_ffn2_gate, w_ffn2_up,
                 w_ffn2_down, g_ffn2_post)
    for l in range(g_ffn1_pre.shape[0]):
        xf = _layer(xf, tab, tab_cmp, batch, seq_len, *(p[l] for p in per_layer))
    return xf.reshape(batch, seq_len, D_MODEL)
```

```python
import functools
import math

import jax
import jax.numpy as jnp
from jax import lax
from jax.experimental import pallas as pl
from jax.experimental.pallas import tpu as pltpu

F32 = jnp.float32
BF16 = jnp.bfloat16

D_MODEL = 1024
N_HEADS = 16
HEAD_DIM = 64
N_KV = 4
GROUP = N_HEADS // N_KV
ROT_DIM = HEAD_DIM // 4
ROT_HALF = ROT_DIM // 2
ROPE_THETA = 500000.0
CMP_LEN = 32
CMP_STRIDE = 16
CMP_HIDDEN = 2 * HEAD_DIM
SEL_LEN = 64
SEL_SHIFT = SEL_LEN.bit_length() - 1
N_SEL = 8
WINDOW = 512
POOL_WINDOWS = (2, 4, 8, 16)
POOL_WIDTH = D_MODEL // 2
POOL_GROUP = POOL_WIDTH // len(POOL_WINDOWS)
D_FF = 2816
EPS = 1e-6
NEG_INF = -1e30
FORCE_SCORE = 1e4
Q_WIDTH = N_HEADS * HEAD_DIM
KV_WIDTH = N_KV * HEAD_DIM
N_GATES = 3 * N_HEADS
LOG2_E = math.log2(math.e)

LANES = 128
BF16_SUBLANES = 16
V7X_VMEM_BYTES = 64 * 1024 * 1024

SLOT = LANES
BLK_LANE0 = HEAD_DIM
VT_ROWS = HEAD_DIM + BF16_SUBLANES
ONE_ROW = HEAD_DIM
TOK_TILE = 512
TQ = 256
KT = 256
CHAIN_COLS = 256
PIPE_DEPTH = 5
FF_CHUNK = 256
FFN_SPLIT = 1
N_ROPE_TAB = 2


def _vmem_limit(block_bytes, scratch_bytes=0):
    need = 2 * block_bytes + scratch_bytes
    return int(min(V7X_VMEM_BYTES - (4 << 20), max(2 * need, 32 << 20)))


def _nbytes(shape, dtype):
    n = 1
    for s in shape:
        n *= s
    return n * jnp.dtype(dtype).itemsize


def _rms(xf, g):
    return xf * lax.rsqrt(jnp.mean(xf * xf, axis=-1, keepdims=True) + EPS) * g


def _dot(a, b):
    return jnp.dot(a, b, preferred_element_type=F32)


def _dot_nt(a, b):
    return lax.dot_general(a, b, (((1,), (1,)), ((), ())), preferred_element_type=F32)


def _rope_trig_kernel(pos_ref, inv_ref, tab_ref):
    tn = pos_ref.shape[1]
    ang = pos_ref[...].astype(F32) * inv_ref[...]
    c, s = jnp.cos(ang), jnp.sin(ang)
    rest = HEAD_DIM - ROT_DIM
    reps = SLOT // HEAD_DIM
    cos_rows = jnp.concatenate([c, c, jnp.ones((rest, tn), F32)] * reps, axis=0)
    sin_rows = jnp.concatenate([-s, s, jnp.zeros((rest, tn), F32)] * reps, axis=0)
    for j in range(tn // LANES):
        cs = slice(j * LANES, (j + 1) * LANES)
        tab_ref[cs, 0:SLOT] = cos_rows[:, cs].T
        tab_ref[cs, SLOT:2 * SLOT] = sin_rows[:, cs].T


def _rope_tables(pos_flat, tile):
    n = pos_flat.shape[0]
    inv = ROPE_THETA ** (-jnp.arange(ROT_HALF, dtype=F32) * (2.0 / ROT_DIM))
    return pl.pallas_call(
        _rope_trig_kernel,
        grid=(n // tile,),
        in_specs=[pl.BlockSpec((1, tile), lambda i: (0, i)),
                  pl.BlockSpec((ROT_HALF, 1), lambda i: (0, 0))],
        out_specs=pl.BlockSpec((tile, N_ROPE_TAB * SLOT), lambda i: (i, 0)),
        out_shape=jax.ShapeDtypeStruct((n, N_ROPE_TAB * SLOT), F32),
        compiler_params=pltpu.CompilerParams(dimension_semantics=("arbitrary",)),
        name="rope_trig",
    )(pos_flat.reshape(1, n), inv.reshape(ROT_HALF, 1))


def _rope_slot(y, tab):
    cos_t = tab[:, 0:SLOT]
    sin_t = tab[:, SLOT:2 * SLOT]
    lane = lax.broadcasted_iota(jnp.int32, y.shape, 1) & (HEAD_DIM - 1)
    up = pltpu.roll(y, SLOT - ROT_HALF, axis=1)
    down = pltpu.roll(y, ROT_HALF, axis=1)
    return y * cos_t + jnp.where(lane < ROT_HALF, up, down) * sin_t


def _ffn_kernel(x_ref, gpre_ref, wg_ref, wu_ref, wd_ref, gpost_ref, o_ref, acc_ref):
    half = x_ref.shape[0] // FFN_SPLIT
    parts = [slice(k * half, (k + 1) * half) for k in range(FFN_SPLIT)]
    hn = [_rms(x_ref[rs, :], gpre_ref[...]) for rs in parts]
    for k, rs in enumerate(parts):
        for c in range(D_FF // FF_CHUNK):
            sl = slice(c * FF_CHUNK, (c + 1) * FF_CHUNK)
            g = _dot(hn[k], wg_ref[:, sl])
            u = _dot(hn[k], wu_ref[:, sl])
            a = g * jax.nn.sigmoid(g) * u
            d = _dot(a, wd_ref[sl, :])
            if c == 0:
                acc_ref[rs, :] = d
            else:
                acc_ref[rs, :] += d
        o_ref[rs, :] = x_ref[rs, :] + 0.5 * _rms(acc_ref[rs, :], gpost_ref[...])


def _ffn(x, g_pre, w_gate, w_up, w_down, g_post):
    t = x.shape[0]
    tile = FFN_SPLIT * TOK_TILE
    row = lambda i: (i, 0)
    fixed = lambda i: (0, 0)
    once = pl.Buffered(1)
    weights = 3 * _nbytes((D_MODEL, D_FF), F32)
    blocks = 2 * _nbytes((tile, D_MODEL), F32)
    return pl.pallas_call(
        _ffn_kernel,
        grid=(t // tile,),
        in_specs=[
            pl.BlockSpec((tile, D_MODEL), row),
            pl.BlockSpec((1, D_MODEL), fixed),
            pl.BlockSpec((D_MODEL, D_FF), fixed, pipeline_mode=once),
            pl.BlockSpec((D_MODEL, D_FF), fixed, pipeline_mode=once),
            pl.BlockSpec((D_FF, D_MODEL), fixed, pipeline_mode=once),
            pl.BlockSpec((1, D_MODEL), fixed),
        ],
        out_specs=pl.BlockSpec((tile, D_MODEL), row),
        out_shape=jax.ShapeDtypeStruct((t, D_MODEL), F32),
        scratch_shapes=[pltpu.VMEM((tile, D_MODEL), F32)],
        compiler_params=pltpu.CompilerParams(
            dimension_semantics=("arbitrary",),
            vmem_limit_bytes=_vmem_limit(blocks, weights + _nbytes((tile, D_MODEL), F32))),
        name="ffn",
    )(x, g_pre.reshape(1, -1), w_gate, w_up, w_down, g_post.reshape(1, -1))


OFF_Q = 0
OFF_CMP = OFF_Q + Q_WIDTH
OFF_KS = OFF_CMP + 2 * KV_WIDTH
OFF_VS = OFF_KS + KV_WIDTH
OFF_KW = OFF_VS + KV_WIDTH
OFF_VW = OFF_KW + KV_WIDTH
OFF_GATE = OFF_VW + KV_WIDTH
OFF_POOL = OFF_GATE + N_GATES
OFF_MERGE = OFF_POOL + POOL_WIDTH
W_CMP = 2 * KV_WIDTH
W_GP = SLOT + POOL_WIDTH
CHUNK_TOK = CMP_STRIDE
VT_SLAB = N_KV * VT_ROWS


def _in_proj_kernel(x_ref, g_ref, w_ref, wgp_ref, tab_ref, q_ref, ks_ref, kw_ref, vst_ref,
                    vwt_ref, cmp_ref, gate_ref, pool_ref, cmp_scr):
    hn = _rms(x_ref[...], g_ref[...])
    tab = tab_ref[...]
    rows = x_ref.shape[0]
    wide = 4 * SLOT
    tab_q = tab * (HEAD_DIM ** -0.5 * LOG2_E)
    for c in range(Q_WIDTH // wide):
        y = _dot(hn, w_ref[:, OFF_Q + c * wide:OFF_Q + (c + 1) * wide])
        for j in range(wide // SLOT):
            sl = slice(c * wide + j * SLOT, c * wide + (j + 1) * SLOT)
            q_ref[:, sl] = _rope_slot(y[:, j * SLOT:(j + 1) * SLOT], tab_q).astype(BF16)
    for k_ref, off in ((ks_ref, OFF_KS), (kw_ref, OFF_KW)):
        y = _dot(hn, w_ref[:, off:off + KV_WIDTH])
        for j in range(KV_WIDTH // SLOT):
            sl = slice(j * SLOT, (j + 1) * SLOT)
            k_ref[:, sl] = _rope_slot(y[:, sl], tab).astype(BF16)
    y = _dot(hn, w_ref[:, OFF_CMP:OFF_CMP + W_CMP])
    for j in range(W_CMP // LANES):
        cmp_scr[j] = y[:, j * LANES:(j + 1) * LANES]
    for l in range(CHUNK_TOK):
        for j in range(W_CMP // LANES):
            c0 = l * W_CMP + j * LANES
            cmp_ref[:, c0:c0 + LANES] = (
                cmp_scr[j, pl.ds(l, rows // CHUNK_TOK, stride=CHUNK_TOK), :].astype(BF16))
    y = _dot(hn, wgp_ref[...])
    gate_ref[...] = y[:, 0:SLOT]
    pool_ref[...] = y[:, SLOT:W_GP]
    aux = (lax.broadcasted_iota(jnp.int32, (VT_ROWS - HEAD_DIM, KT), 0) == 0).astype(BF16)
    for vt_ref, off in ((vst_ref, OFF_VS), (vwt_ref, OFF_VW)):
        vt = _dot(hn, w_ref[:, off:off + KV_WIDTH]).T.astype(BF16)
        for j in range(rows // KT):
            for g in range(N_KV):
                vt_ref[j, g * VT_ROWS:g * VT_ROWS + HEAD_DIM, :] = (
                    vt[g * HEAD_DIM:(g + 1) * HEAD_DIM, j * KT:(j + 1) * KT])
                vt_ref[j, g * VT_ROWS + HEAD_DIM:(g + 1) * VT_ROWS, :] = aux


def _in_proj(x1, g_pre, w_in, tab):
    t = x1.shape[0]
    w_gp = jnp.concatenate([jnp.pad(w_in[:, OFF_GATE:OFF_POOL], ((0, 0), (0, SLOT - N_GATES))),
                            w_in[:, OFF_POOL:OFF_MERGE]], axis=1)
    row = lambda i: (i, 0)
    fixed = lambda i: (0, 0)
    once = pl.Buffered(1)
    tiles = TOK_TILE // KT
    chunk_rows = TOK_TILE // CHUNK_TOK
    slab = pl.BlockSpec((tiles, VT_SLAB, KT), lambda i: (i, 0, 0))
    slab_shape = jax.ShapeDtypeStruct((t // KT, VT_SLAB, KT), BF16)
    out_specs = [
        pl.BlockSpec((TOK_TILE, Q_WIDTH), row), pl.BlockSpec((TOK_TILE, KV_WIDTH), row),
        pl.BlockSpec((TOK_TILE, KV_WIDTH), row), slab, slab,
        pl.BlockSpec((chunk_rows, CHUNK_TOK * W_CMP), row),
        pl.BlockSpec((TOK_TILE, SLOT), row), pl.BlockSpec((TOK_TILE, POOL_WIDTH), row)]
    out_shape = [
        jax.ShapeDtypeStruct((t, Q_WIDTH), BF16), jax.ShapeDtypeStruct((t, KV_WIDTH), BF16),
        jax.ShapeDtypeStruct((t, KV_WIDTH), BF16), slab_shape, slab_shape,
        jax.ShapeDtypeStruct((t // CHUNK_TOK, CHUNK_TOK * W_CMP), BF16),
        jax.ShapeDtypeStruct((t, SLOT), F32), jax.ShapeDtypeStruct((t, POOL_WIDTH), F32)]
    weights = _nbytes(w_in.shape, F32) + _nbytes(w_gp.shape, F32)
    blocks = (_nbytes((TOK_TILE, D_MODEL), F32) + _nbytes((TOK_TILE, N_ROPE_TAB * SLOT), F32)
              + _nbytes((TOK_TILE, Q_WIDTH + 2 * KV_WIDTH + W_CMP), BF16)
              + 2 * _nbytes((tiles, VT_SLAB, KT), BF16) + _nbytes((TOK_TILE, SLOT + POOL_WIDTH), F32))
    res = pl.pallas_call(
        _in_proj_kernel,
        grid=(t // TOK_TILE,),
        in_specs=[
            pl.BlockSpec((TOK_TILE, D_MODEL), row),
            pl.BlockSpec((1, D_MODEL), fixed),
            pl.BlockSpec(w_in.shape, fixed, pipeline_mode=once),
            pl.BlockSpec(w_gp.shape, fixed, pipeline_mode=once),
            pl.BlockSpec((TOK_TILE, N_ROPE_TAB * SLOT), row),
        ],
        out_specs=out_specs,
        out_shape=out_shape,
        scratch_shapes=[pltpu.VMEM((W_CMP // LANES, TOK_TILE, LANES), F32)],
        compiler_params=pltpu.CompilerParams(
            dimension_semantics=("arbitrary",),
            vmem_limit_bytes=_vmem_limit(blocks, weights + _nbytes((TOK_TILE, W_CMP), F32))),
        name="in_proj",
    )(x1, g_pre.reshape(1, -1), w_in, w_gp, tab)
    return res, w_in[:, OFF_MERGE:]


def _compress_kernel(x_ref, w1k_ref, w1v_ref, pek_ref, pev_ref, w1k_raw_ref, w1v_raw_ref,
                     w2k_ref, w2v_ref, tab_ref, kc_ref, vc_ref, acck_ref, accv_ref):
    l = pl.program_id(0)

    @pl.when(l == 0)
    def _():
        acck_ref[...] = jnp.zeros(acck_ref.shape, F32)
        accv_ref[...] = jnp.zeros(accv_ref.shape, F32)

    for acc_ref, w1_ref, c0 in ((acck_ref, w1k_ref, 0), (accv_ref, w1v_ref, KV_WIDTH)):
        for g in range(N_KV):
            acc_ref[g] += _dot(x_ref[:, c0 + g * HEAD_DIM:c0 + (g + 1) * HEAD_DIM], w1_ref[0])

    @pl.when(l == CHUNK_TOK - 1)
    def _():
        rows = acck_ref.shape[1]
        tab = tab_ref[...]
        for acc_ref, pe_ref, w1_raw_ref, w2_ref, o_ref, rope in (
                (acck_ref, pek_ref, w1k_raw_ref, w2k_ref, kc_ref, True),
                (accv_ref, pev_ref, w1v_raw_ref, w2v_ref, vc_ref, False)):
            bias = _dot(pe_ref[...], w1_raw_ref[...])[0:1, :]
            for g in range(N_KV):
                first = acc_ref[g, :, 0:CMP_HIDDEN]
                second = pltpu.roll(acc_ref[g, :, CMP_HIDDEN:2 * CMP_HIDDEN], rows - 1, axis=0)
                hid = jax.nn.gelu(first + second + bias).astype(BF16)
                out = _dot(hid, w2_ref[...])
                if rope:
                    out = _rope_slot(out, tab)
                o_ref[:, g * SLOT:(g + 1) * SLOT] = out.astype(BF16)


def _compress_weights(w1, w2):
    w1 = w1.reshape(2, CHUNK_TOK, HEAD_DIM, CMP_HIDDEN)
    w1_tok = jnp.concatenate([w1[0], w1[1]], axis=-1).astype(BF16)
    return w1_tok, jnp.pad(w2, ((0, 0), (0, SLOT - HEAD_DIM))).astype(BF16)


def _compress(cmp_rows, cmp_pe_k, cmp_w1_k, cmp_w2_k, cmp_pe_v, cmp_w1_v, cmp_w2_v, tab_cmp):
    rows = cmp_rows.shape[0]
    w1k, w2k = _compress_weights(cmp_w1_k, cmp_w2_k)
    w1v, w2v = _compress_weights(cmp_w1_v, cmp_w2_v)
    sub = 8
    pek = jnp.broadcast_to(cmp_pe_k.reshape(1, -1), (sub, CMP_LEN * HEAD_DIM)).astype(BF16)
    pev = jnp.broadcast_to(cmp_pe_v.reshape(1, -1), (sub, CMP_LEN * HEAD_DIM)).astype(BF16)
    fixed = lambda l: (0, 0)
    w_kv_pad = N_KV * SLOT
    blocks = (_nbytes((rows, W_CMP), BF16) + 2 * _nbytes((HEAD_DIM, 2 * CMP_HIDDEN), BF16)
              + 2 * _nbytes((sub + CMP_LEN * HEAD_DIM, CMP_HIDDEN), BF16)
              + 2 * _nbytes((CMP_HIDDEN, SLOT), BF16)
              + _nbytes((rows, N_ROPE_TAB * SLOT), F32) + 2 * _nbytes((rows, w_kv_pad), BF16))
    acc = pltpu.VMEM((N_KV, rows, 2 * CMP_HIDDEN), F32)
    return pl.pallas_call(
        _compress_kernel,
        grid=(CHUNK_TOK,),
        in_specs=[
            pl.BlockSpec((rows, W_CMP), lambda l: (0, l)),
            pl.BlockSpec((1, HEAD_DIM, 2 * CMP_HIDDEN), lambda l: (l, 0, 0)),
            pl.BlockSpec((1, HEAD_DIM, 2 * CMP_HIDDEN), lambda l: (l, 0, 0)),
            pl.BlockSpec((sub, CMP_LEN * HEAD_DIM), fixed),
            pl.BlockSpec((sub, CMP_LEN * HEAD_DIM), fixed),
            pl.BlockSpec((CMP_LEN * HEAD_DIM, CMP_HIDDEN), fixed),
            pl.BlockSpec((CMP_LEN * HEAD_DIM, CMP_HIDDEN), fixed),
            pl.BlockSpec((CMP_HIDDEN, SLOT), fixed),
            pl.BlockSpec((CMP_HIDDEN, SLOT), fixed),
            pl.BlockSpec((rows, N_ROPE_TAB * SLOT), fixed),
        ],
        out_specs=[pl.BlockSpec((rows, w_kv_pad), fixed)] * 2,
        out_shape=[jax.ShapeDtypeStruct((rows, w_kv_pad), BF16)] * 2,
        scratch_shapes=[acc, acc],
        compiler_params=pltpu.CompilerParams(
            dimension_semantics=("arbitrary",),
            vmem_limit_bytes=_vmem_limit(blocks, 2 * _nbytes(acc.shape, F32))),
        name="compress",
    )(cmp_rows, w1k, w1v, pek, pev, cmp_w1_k.astype(BF16), cmp_w1_v.astype(BF16), w2k, w2v, tab_cmp)


def _nsa_kernel(q_ref, gate_ref, kc_ref, vc_ref, ks_ref, vst_ref, kw_ref, vwt_ref, ov_ref,
                o_ref, ksp_ref, kwp_ref, qg_ref, mix_ref, *state_refs, n_slc):
    i = pl.program_id(1)
    start = i * TQ
    seq = ks_ref.shape[1]
    cols_g = GROUP * TQ
    chains_g = cols_g // CHAIN_COLS
    n_chains = N_KV * chains_g
    n_cmp = kc_ref.shape[1]
    sel_state = (state_refs[0:n_chains], state_refs[n_chains:2 * n_chains])
    win_state = (state_refs[2 * n_chains:3 * n_chains], state_refs[3 * n_chains:4 * n_chains])
    lane_q = lax.broadcasted_iota(jnp.int32, (TQ, SLOT), 1)

    def head_slot(x_ref, rows, j):
        two = x_ref[rows, (j // 2) * SLOT:(j // 2 + 1) * SLOT].astype(F32)
        return pltpu.roll(two, HEAD_DIM, axis=1) if j % 2 else two

    @pl.when(i == 0)
    def _():
        for c in range(seq // KT):
            rows = pl.ds(c * KT, KT)
            lane = lax.broadcasted_iota(jnp.int32, (KT, SLOT), 1)
            key_blk = (c * KT + lax.broadcasted_iota(jnp.int32, (KT, SLOT), 0)) >> SEL_SHIFT
            onehot = (lane - BLK_LANE0 == key_blk).astype(F32)
            for g in range(N_KV):
                gs = slice(g * SLOT, (g + 1) * SLOT)
                ksp_ref[rows, gs] = jnp.where(lane < HEAD_DIM, head_slot(ks_ref.at[0], rows, g),
                                              onehot).astype(BF16)
                kwp_ref[rows, gs] = jnp.where(lane < HEAD_DIM, head_slot(kw_ref.at[0], rows, g),
                                              0.0).astype(BF16)

    for h in range(N_HEADS):
        g, r = divmod(h, GROUP)
        qg_ref[g, r * TQ:(r + 1) * TQ, :] = jnp.where(
            lane_q < HEAD_DIM, head_slot(q_ref, slice(None), h), 0.0).astype(BF16)

    gates_t = jax.nn.sigmoid(gate_ref[...]).T
    any_cmp = start + lax.broadcasted_iota(jnp.int32, (1, TQ), 1) >= CMP_LEN - 1
    key_in = lax.broadcasted_iota(jnp.int32, (KT, CHAIN_COLS), 0)
    qry_in = lax.broadcasted_iota(jnp.int32, (KT, CHAIN_COLS), 1) & (TQ - 1)
    diag_mask = key_in <= qry_in
    far_mask = key_in > qry_in

    def gate_row(branch, h):
        c = branch * N_HEADS + h
        return gates_t[c:c + 1, :]

    sub = 8
    cmp_per_sub = sub * SEL_LEN // CMP_STRIDE

    def compressed_and_select(ng):
        n_blk, n_c = ng * sub, ng * cmp_per_sub
        t_cmp = start + lax.broadcasted_iota(jnp.int32, (n_c, TQ), 1)
        n_idx = lax.broadcasted_iota(jnp.int32, (n_c, TQ), 0)
        cmp_valid = n_idx * CMP_STRIDE + (CMP_LEN - 1) <= t_cmp
        t_row = start + lax.broadcasted_iota(jnp.int32, (n_blk, TQ), 1)
        blk = lax.broadcasted_iota(jnp.int32, (n_blk, TQ), 0)
        forced = (blk == t_row >> SEL_SHIFT) | (blk == 0)
        causal_blk = blk * SEL_LEN <= t_row
        idx8 = lax.broadcasted_iota(jnp.int32, (sub, TQ), 0)
        pad_c = jnp.zeros((n_cmp - n_c, TQ), F32)
        cmp_scores = []
        for h in range(N_HEADS):
            g, r = divmod(h, GROUP)
            cmp_scores.append(_dot_nt(kc_ref[0, 0:n_c, g * SLOT:(g + 1) * SLOT],
                                      qg_ref[g, r * TQ:(r + 1) * TQ, :]))
        for g in range(N_KV):
            vct_g = vc_ref[0, :, g * SLOT:(g + 1) * SLOT].astype(F32).T[0:HEAD_DIM].astype(BF16)
            p_sum = jnp.zeros((n_c, TQ), F32)
            for r in range(GROUP):
                h = g * GROUP + r
                s = jnp.where(cmp_valid, cmp_scores[h], NEG_INF)
                p = jnp.exp2(s - jnp.max(s, axis=0, keepdims=True))
                inv = jnp.where(any_cmp, 1.0 / jnp.sum(p, axis=0, keepdims=True), 0.0)
                p = p * inv
                p_sum = p_sum + p
                p_all = jnp.concatenate([p, pad_c], axis=0) if ng * cmp_per_sub < n_cmp else p
                mix_ref[h] = gate_row(0, h) * _dot(vct_g, p_all.astype(BF16))
            p_sum_all = jnp.concatenate([p_sum, pad_c], axis=0) if ng * cmp_per_sub < n_cmp else p_sum
            imp = jnp.dot(ov_ref[0:n_blk, :], p_sum_all, precision=lax.Precision.HIGHEST,
                          preferred_element_type=F32)
            score = jnp.where(causal_blk, jnp.where(forced, FORCE_SCORE, imp), NEG_INF)
            rows8 = [score[k * sub:(k + 1) * sub] for k in range(ng)]
            ranks = [jnp.zeros((sub, TQ), jnp.int32) for _ in rows8]
            for c in range(n_blk):
                other = score[c:c + 1, :]
                for k, mine in enumerate(rows8):
                    if k * sub > c:
                        beats = other >= mine
                    elif (k + 1) * sub - 1 <= c:
                        beats = other > mine
                    else:
                        beats = (other > mine) | ((other == mine) & (idx8 > c - k * sub))
                    ranks[k] = ranks[k] + beats.astype(jnp.int32)
            rank = jnp.concatenate(ranks, axis=0) if ng > 1 else ranks[0]
            bias = jnp.where(causal_blk & (rank < N_SEL), 0.0, NEG_INF)
            bias_t = jnp.concatenate([jnp.zeros((BLK_LANE0, TQ), F32), bias,
                                      jnp.zeros((SLOT - BLK_LANE0 - n_blk, TQ), F32)], axis=0)
            bias_q = bias_t.T.astype(BF16)
            for r in range(GROUP):
                rs = slice(r * TQ, (r + 1) * TQ)
                qg_ref[g, rs, :] = jnp.where(lane_q < HEAD_DIM, qg_ref[g, rs, :], bias_q)

    ng_max = n_slc // sub
    ng_now = ((i + 1) * TQ + sub * SEL_LEN - 1) // (sub * SEL_LEN)
    for ng in range(1, ng_max + 1):
        pl.when(ng_now == ng)(functools.partial(compressed_and_select, ng))

    def flash_init(state):
        for m_ref, acc_ref in zip(*state):
            m_ref[...] = jnp.full(m_ref.shape, NEG_INF, F32)
            acc_ref[...] = jnp.zeros(acc_ref.shape, F32)

    def flash_tiles(tiles):
        jobs = [(tile, c) for tile in tiles for c in range(n_chains)]
        scores = {}
        for step in range(len(jobs) + PIPE_DEPTH):
            if step < len(jobs):
                (kp_ref, _, kt, mask, _), chain = jobs[step]
                g, c = divmod(chain, chains_g)
                off = pl.multiple_of(kt * KT, KT)
                s = _dot_nt(kp_ref[pl.ds(off, KT), g * SLOT:(g + 1) * SLOT],
                            qg_ref[g, c * CHAIN_COLS:(c + 1) * CHAIN_COLS, :])
                scores[step] = s if mask is None else jnp.where(mask, s, NEG_INF)
            done = step - PIPE_DEPTH
            if done >= 0:
                (_, vt_ref, kt, _, (m_refs, acc_refs)), chain = jobs[done]
                g = chain // chains_g
                m_ref, acc_ref = m_refs[chain], acc_refs[chain]
                s = scores.pop(done)
                m_old = m_ref[...]
                m_new = jnp.maximum(m_old, jnp.max(s, axis=0, keepdims=True))
                p = jnp.exp2(s - m_new).astype(BF16)
                pv = _dot(vt_ref[kt, g * VT_ROWS:(g + 1) * VT_ROWS, :], p)
                acc_ref[...] = jnp.exp2(m_old - m_new) * acc_ref[...] + pv
                m_ref[...] = m_new

    def flash_mix(branch, state):
        _, acc_refs = state
        for h in range(N_HEADS):
            g, r = divmod(h, GROUP)
            c, c0 = divmod(r * TQ, CHAIN_COLS)
            acc = acc_refs[g * chains_g + c][:, c0:c0 + TQ]
            scale = gate_row(branch, h) * (1.0 / acc[ONE_ROW:ONE_ROW + 1, :])
            mix_ref[h] = mix_ref[h] + scale * acc[0:HEAD_DIM]

    flash_init(sel_state)
    flash_init(win_state)

    def sel_past(kt):
        return (ksp_ref, vst_ref, kt, None, sel_state)

    def sel_pair(j, carry):
        flash_tiles([sel_past(2 * j), sel_past(2 * j + 1)])
        return carry

    lax.fori_loop(0, i >> 1, sel_pair, 0)

    @pl.when(i & 1 == 1)
    def _():
        flash_tiles([sel_past(i - 1)])

    n_back = WINDOW // KT
    sel_diag = (ksp_ref, vst_ref, i, diag_mask, sel_state)
    win_diag = (kwp_ref, vwt_ref, i, diag_mask, win_state)

    def win_back(d):
        return (kwp_ref, vwt_ref, i - d, far_mask if d == n_back else None, win_state)

    for have in range(n_back + 1):
        cond = (i == have) if have < n_back else (i >= have)

        @pl.when(cond)
        def _(have=have):
            flash_tiles([sel_diag] + [win_back(d) for d in range(have, 0, -1)] + [win_diag])

    flash_mix(1, sel_state)
    flash_mix(2, win_state)
    for h2 in range(N_HEADS // 2):
        pair = jnp.concatenate([mix_ref[2 * h2], mix_ref[2 * h2 + 1]], axis=0)
        o_ref[:, h2 * SLOT:(h2 + 1) * SLOT] = pair.T.astype(BF16)


def _overlap(n_cmp_pad, n_slc):
    c0 = jnp.arange(n_cmp_pad) * CMP_STRIDE
    s0 = jnp.arange(n_slc) * SEL_LEN
    ov = jnp.minimum(c0[None, :] + CMP_LEN, s0[:, None] + SEL_LEN) - jnp.maximum(c0[None, :], s0[:, None])
    return jnp.clip(ov, 0).astype(F32) / CMP_LEN


def _nsa(q, gate, kc, vc, ks, vst, kw, vwt, batch, seq_len):
    n_slc = seq_len // SEL_LEN
    n_cmp_pad = kc.shape[1]
    w_kv_pad = N_KV * SLOT
    assert n_slc <= SLOT - BLK_LANE0 and n_slc % 8 == 0
    assert CHAIN_COLS % TQ == 0 and (GROUP * TQ) % CHAIN_COLS == 0
    ov = _overlap(n_cmp_pad, n_slc)
    nq = seq_len // TQ
    n_kt = seq_len // KT
    qrow = lambda b, i: (b * nq + i, 0)
    per_b = lambda b, i: (b, 0, 0)
    fixed = lambda b, i: (0, 0)
    blocks = (_nbytes((TQ, Q_WIDTH), BF16) + _nbytes((TQ, SLOT), F32)
              + 2 * _nbytes((n_cmp_pad, w_kv_pad), BF16) + 2 * _nbytes((seq_len, KV_WIDTH), BF16)
              + 2 * _nbytes((n_kt, VT_SLAB, KT), BF16)
              + _nbytes(ov.shape, F32) + _nbytes((TQ, Q_WIDTH), BF16))
    n_chains = N_KV * GROUP * TQ // CHAIN_COLS
    run_max = [pltpu.VMEM((1, CHAIN_COLS), F32)] * n_chains
    run_acc = [pltpu.VMEM((VT_ROWS, CHAIN_COLS), F32)] * n_chains
    scratch_shapes = (
        [pltpu.VMEM((seq_len, w_kv_pad), BF16),
         pltpu.VMEM((seq_len, w_kv_pad), BF16),
         pltpu.VMEM((N_KV, GROUP * TQ, SLOT), BF16),
         pltpu.VMEM((N_HEADS, HEAD_DIM, TQ), F32)]
        + run_max + run_acc + run_max + run_acc)
    scratch = sum(_nbytes(s.shape, s.dtype) for s in scratch_shapes)
    return pl.pallas_call(
        functools.partial(_nsa_kernel, n_slc=n_slc),
        grid=(batch, nq),
        in_specs=[
            pl.BlockSpec((TQ, Q_WIDTH), qrow),
            pl.BlockSpec((TQ, SLOT), qrow),
            pl.BlockSpec((1, n_cmp_pad, w_kv_pad), per_b),
            pl.BlockSpec((1, n_cmp_pad, w_kv_pad), per_b),
            pl.BlockSpec((1, seq_len, KV_WIDTH), per_b),
            pl.BlockSpec((n_kt, VT_SLAB, KT), per_b),
            pl.BlockSpec((1, seq_len, KV_WIDTH), per_b),
            pl.BlockSpec((n_kt, VT_SLAB, KT), per_b),
            pl.BlockSpec(ov.shape, fixed),
        ],
        out_specs=pl.BlockSpec((TQ, Q_WIDTH), qrow),
        out_shape=jax.ShapeDtypeStruct((batch * seq_len, Q_WIDTH), BF16),
        scratch_shapes=scratch_shapes,
        compiler_params=pltpu.CompilerParams(
            dimension_semantics=("arbitrary", "arbitrary"),
            vmem_limit_bytes=_vmem_limit(blocks, scratch)),
        name="nsa",
    )(q, gate, kc.reshape(batch, n_cmp_pad, w_kv_pad), vc.reshape(batch, n_cmp_pad, w_kv_pad),
      ks.reshape(batch, seq_len, KV_WIDTH), vst, kw.reshape(batch, seq_len, KV_WIDTH), vwt, ov)


def _pool_kernel(u_ref, w_ref, scale_ref, o_ref):
    seq = u_ref.shape[0]
    t = lax.broadcasted_iota(jnp.int32, (seq, POOL_GROUP), 0)

    def shifted(x, k):
        return jnp.where(t >= k, pltpu.roll(x, k, axis=0), 0.0)

    for gi, w in enumerate(POOL_WINDOWS):
        sl = slice(gi * POOL_GROUP, (gi + 1) * POOL_GROUP)
        x = u_ref[:, sl]
        wsum = x
        span = 1
        while span < w:
            wsum = wsum + shifted(wsum, span)
            span *= 2
        cnt = jnp.minimum(t + 1, w).astype(F32)
        pooled = (wsum / cnt - x).astype(BF16)
        o_ref[:, sl] = (_dot(pooled, w_ref[gi]) * scale_ref[:, sl]).astype(BF16)


def _pool(pool_in, pool_w, pool_scale, batch, seq_len):
    for w in POOL_WINDOWS:
        assert w & (w - 1) == 0
    blocks = (_nbytes((seq_len, POOL_WIDTH), F32) + _nbytes(pool_w.shape, BF16)
              + _nbytes((seq_len, POOL_WIDTH), BF16))
    return pl.pallas_call(
        _pool_kernel,
        grid=(batch,),
        in_specs=[
            pl.BlockSpec((seq_len, POOL_WIDTH), lambda b: (b, 0)),
            pl.BlockSpec(pool_w.shape, lambda b: (0, 0, 0)),
            pl.BlockSpec((1, POOL_WIDTH), lambda b: (0, 0)),
        ],
        out_specs=pl.BlockSpec((seq_len, POOL_WIDTH), lambda b: (b, 0)),
        out_shape=jax.ShapeDtypeStruct((batch * seq_len, POOL_WIDTH), BF16),
        compiler_params=pltpu.CompilerParams(
            dimension_semantics=("arbitrary",), vmem_limit_bytes=_vmem_limit(blocks)),
        name="pool",
    )(pool_in, pool_w.astype(BF16), pool_scale.reshape(1, -1))


def _merge_kernel(x_ref, gpre_ref, wm_ref, nsa_ref, wa_ref, pool_ref, wp_ref, wo_ref, gpost_ref,
                  o_ref):
    x = x_ref[...]
    hn = _rms(x, gpre_ref[...])
    g_attn = jax.nn.sigmoid(_dot(hn, wm_ref[:, 0:D_MODEL]))
    g_pool = jax.nn.sigmoid(_dot(hn, wm_ref[:, D_MODEL:2 * D_MODEL]))
    y = (g_attn * _dot(nsa_ref[...].astype(F32), wa_ref[...])
         + g_pool * _dot(pool_ref[...].astype(F32), wp_ref[...]))
    h = _dot(y, wo_ref[...])
    o_ref[...] = x + _rms(h, gpost_ref[...])


def _merge(x1, g_pre, w_merge, o_nsa, w_attn, o_pool, w_pool, w_out, g_post):
    t = x1.shape[0]
    row = lambda i: (i, 0)
    fixed = lambda i: (0, 0)
    once = pl.Buffered(1)
    weights = (_nbytes((D_MODEL, 2 * D_MODEL), F32) + _nbytes((Q_WIDTH, D_MODEL), F32)
               + _nbytes((POOL_WIDTH, D_MODEL), F32) + _nbytes((D_MODEL, D_MODEL), F32))
    blocks = (2 * _nbytes((TOK_TILE, D_MODEL), F32) + _nbytes((TOK_TILE, Q_WIDTH), BF16)
              + _nbytes((TOK_TILE, POOL_WIDTH), BF16))
    return pl.pallas_call(
        _merge_kernel,
        grid=(t // TOK_TILE,),
        in_specs=[
            pl.BlockSpec((TOK_TILE, D_MODEL), row),
            pl.BlockSpec((1, D_MODEL), fixed),
            pl.BlockSpec((D_MODEL, 2 * D_MODEL), fixed, pipeline_mode=once),
            pl.BlockSpec((TOK_TILE, Q_WIDTH), row),
            pl.BlockSpec((Q_WIDTH, D_MODEL), fixed, pipeline_mode=once),
            pl.BlockSpec((TOK_TILE, POOL_WIDTH), row),
            pl.BlockSpec((POOL_WIDTH, D_MODEL), fixed, pipeline_mode=once),
            pl.BlockSpec((D_MODEL, D_MODEL), fixed, pipeline_mode=once),
            pl.BlockSpec((1, D_MODEL), fixed),
        ],
        out_specs=pl.BlockSpec((TOK_TILE, D_MODEL), row),
        out_shape=jax.ShapeDtypeStruct((t, D_MODEL), F32),
        compiler_params=pltpu.CompilerParams(
            dimension_semantics=("arbitrary",), vmem_limit_bytes=_vmem_limit(blocks, weights)),
        name="merge",
    )(x1, g_pre.reshape(1, -1), w_merge, o_nsa, w_attn, o_pool, w_pool, w_out, g_post.reshape(1, -1))


def _layer(x, tab, tab_cmp, batch, seq_len, g_ffn1_pre, w_ffn1_gate, w_ffn1_up, w_ffn1_down,
           g_ffn1_post, g_mix_pre, w_in, cmp_pe_k, cmp_w1_k, cmp_w2_k, cmp_pe_v, cmp_w1_v, cmp_w2_v,
           w_attn_branch, pool_w, pool_scale, w_pool_branch, w_out, g_mix_post, g_ffn2_pre,
           w_ffn2_gate, w_ffn2_up, w_ffn2_down, g_ffn2_post):
    x1 = _ffn(x, g_ffn1_pre, w_ffn1_gate, w_ffn1_up, w_ffn1_down, g_ffn1_post)
    (q, ks, kw, vst, vwt, cmp_rows, gate, pool_in), w_merge = _in_proj(x1, g_mix_pre, w_in, tab)
    kc, vc = _compress(cmp_rows, cmp_pe_k, cmp_w1_k, cmp_w2_k, cmp_pe_v, cmp_w1_v, cmp_w2_v, tab_cmp)
    n_cmp_pad = seq_len // CMP_STRIDE
    kc = kc.reshape(batch, n_cmp_pad, N_KV * SLOT)
    vc = vc.reshape(batch, n_cmp_pad, N_KV * SLOT)
    o_nsa = _nsa(q, gate, kc, vc, ks, vst, kw, vwt, batch, seq_len)
    o_pool = _pool(pool_in, pool_w, pool_scale, batch, seq_len)
    x2 = _merge(x1, g_mix_pre, w_merge, o_nsa, w_attn_branch, o_pool, w_pool_branch, w_out,
                g_mix_post)
    return _ffn(x2, g_ffn2_pre, w_ffn2_gate, w_ffn2_up, w_ffn2_down, g_ffn2_post)


def kernel(x, positions, g_ffn1_pre, w_ffn1_gate, w_ffn1_up, w_ffn1_down, g_ffn1_post, g_mix_pre, w_in, cmp_pe_k, cmp_w1_k, cmp_w2_k, cmp_pe_v, cmp_w1_v, cmp_w2_v, w_attn_branch, pool_w, pool_scale, w_pool_branch, w_out, g_mix_post, g_ffn2_pre, w_ffn2_gate, w_ffn2_up, w_ffn2_down, g_ffn2_post):
    batch, seq_len, d_model = x.shape
    assert d_model == D_MODEL and seq_len % TOK_TILE == 0 and seq_len % TQ == 0
    assert TQ == KT and TQ % SEL_LEN == 0 and TOK_TILE % KT == 0 and WINDOW % KT == 0
    assert seq_len % CMP_STRIDE == 0 and CMP_LEN == 2 * CMP_STRIDE
    t = batch * seq_len
    tab = _rope_tables(positions.reshape(t), seq_len)
    n_cmp = (seq_len - CMP_LEN) // CMP_STRIDE + 1
    n_cmp_pad = seq_len // CMP_STRIDE
    pos_cmp = jnp.pad(positions[:, CMP_LEN - 1::CMP_STRIDE], ((0, 0), (0, n_cmp_pad - n_cmp)))
    tab_cmp = _rope_tables(pos_cmp.reshape(batch * n_cmp_pad), batch * n_cmp_pad)
    xf = x.reshape(t, D_MODEL)
    per_layer = (g_ffn1_pre, w_ffn1_gate, w_ffn1_up, w_ffn1_down, g_ffn1_post, g_mix_pre, w_in,
                 cmp_pe_k, cmp_w1_k, cmp_w2_k, cmp_pe_v, cmp_w1_v, cmp_w2_v, w_attn_branch, pool_w,
                 pool_scale, w_pool_branch, w_out, g_mix_post, g_ffn2_pre, w_ffn2_gate, w_ffn2_up,
                 w_ffn2_down, g_ffn2_post)
    for l in range(g_ffn1_pre.shape[0]):
        xf = _layer(xf, tab, tab_cmp, batch, seq_len, *(p[l] for p in per_layer))
    return xf.reshape(batch, seq_len, D_MODEL)
```

```python
import functools
import math

import jax
import jax.numpy as jnp
from jax import lax
from jax.experimental import pallas as pl
from jax.experimental.pallas import tpu as pltpu

F32 = jnp.float32
BF16 = jnp.bfloat16

D_MODEL = 1024
N_HEADS = 16
HEAD_DIM = 64
N_KV = 4
GROUP = N_HEADS // N_KV
ROT_DIM = HEAD_DIM // 4
ROT_HALF = ROT_DIM // 2
ROPE_THETA = 500000.0
CMP_LEN = 32
CMP_STRIDE = 16
CMP_HIDDEN = 2 * HEAD_DIM
SEL_LEN = 64
SEL_SHIFT = SEL_LEN.bit_length() - 1
N_SEL = 8
WINDOW = 512
POOL_WINDOWS = (2, 4, 8, 16)
POOL_WIDTH = D_MODEL // 2
POOL_GROUP = POOL_WIDTH // len(POOL_WINDOWS)
D_FF = 2816
EPS = 1e-6
NEG_INF = -1e30
FORCE_SCORE = 1e4
Q_WIDTH = N_HEADS * HEAD_DIM
KV_WIDTH = N_KV * HEAD_DIM
N_GATES = 3 * N_HEADS
LOG2_E = math.log2(math.e)

LANES = 128
BF16_SUBLANES = 16
V7X_VMEM_BYTES = 64 * 1024 * 1024

SLOT = LANES
BLK_LANE0 = HEAD_DIM
VT_ROWS = HEAD_DIM + BF16_SUBLANES
ONE_ROW = HEAD_DIM
TOK_TILE = 512
TQ = 256
KT = 256
CHAIN_COLS = 256
PIPE_DEPTH = 5
FF_CHUNK = 256
FFN_SPLIT = 1
N_ROPE_TAB = 2


def _vmem_limit(block_bytes, scratch_bytes=0):
    need = 2 * block_bytes + scratch_bytes
    return int(min(V7X_VMEM_BYTES - (4 << 20), max(2 * need, 32 << 20)))


def _nbytes(shape, dtype):
    n = 1
    for s in shape:
        n *= s
    return n * jnp.dtype(dtype).itemsize


def _rms(xf, g):
    return xf * lax.rsqrt(jnp.mean(xf * xf, axis=-1, keepdims=True) + EPS) * g


def _dot(a, b):
    return jnp.dot(a, b, preferred_element_type=F32)


def _dot_nt(a, b):
    return lax.dot_general(a, b, (((1,), (1,)), ((), ())), preferred_element_type=F32)


def _rope_trig_kernel(pos_ref, inv_ref, tab_ref):
    tn = pos_ref.shape[1]
    ang = pos_ref[...].astype(F32) * inv_ref[...]
    c, s = jnp.cos(ang), jnp.sin(ang)
    rest = HEAD_DIM - ROT_DIM
    reps = SLOT // HEAD_DIM
    cos_rows = jnp.concatenate([c, c, jnp.ones((rest, tn), F32)] * reps, axis=0)
    sin_rows = jnp.concatenate([-s, s, jnp.zeros((rest, tn), F32)] * reps, axis=0)
    for j in range(tn // LANES):
        cs = slice(j * LANES, (j + 1) * LANES)
        tab_ref[cs, 0:SLOT] = cos_rows[:, cs].T
        tab_ref[cs, SLOT:2 * SLOT] = sin_rows[:, cs].T


def _rope_tables(pos_flat, tile):
    n = pos_flat.shape[0]
    inv = ROPE_THETA ** (-jnp.arange(ROT_HALF, dtype=F32) * (2.0 / ROT_DIM))
    return pl.pallas_call(
        _rope_trig_kernel,
        grid=(n // tile,),
        in_specs=[pl.BlockSpec((1, tile), lambda i: (0, i)),
                  pl.BlockSpec((ROT_HALF, 1), lambda i: (0, 0))],
        out_specs=pl.BlockSpec((tile, N_ROPE_TAB * SLOT), lambda i: (i, 0)),
        out_shape=jax.ShapeDtypeStruct((n, N_ROPE_TAB * SLOT), F32),
        compiler_params=pltpu.CompilerParams(dimension_semantics=("arbitrary",)),
        name="rope_trig",
    )(pos_flat.reshape(1, n), inv.reshape(ROT_HALF, 1))


def _rope_slot(y, tab):
    cos_t = tab[:, 0:SLOT]
    sin_t = tab[:, SLOT:2 * SLOT]
    lane = lax.broadcasted_iota(jnp.int32, y.shape, 1) & (HEAD_DIM - 1)
    up = pltpu.roll(y, SLOT - ROT_HALF, axis=1)
    down = pltpu.roll(y, ROT_HALF, axis=1)
    return y * cos_t + jnp.where(lane < ROT_HALF, up, down) * sin_t


def _ffn_kernel(x_ref, gpre_ref, wg_ref, wu_ref, wd_ref, gpost_ref, o_ref, acc_ref):
    half = x_ref.shape[0] // FFN_SPLIT
    parts = [slice(k * half, (k + 1) * half) for k in range(FFN_SPLIT)]
    hn = [_rms(x_ref[rs, :], gpre_ref[...]) for rs in parts]
    for k, rs in enumerate(parts):
        for c in range(D_FF // FF_CHUNK):
            sl = slice(c * FF_CHUNK, (c + 1) * FF_CHUNK)
            g = _dot(hn[k], wg_ref[:, sl])
            u = _dot(hn[k], wu_ref[:, sl])
            a = g * jax.nn.sigmoid(g) * u
            d = _dot(a, wd_ref[sl, :])
            if c == 0:
                acc_ref[rs, :] = d
            else:
                acc_ref[rs, :] += d
        o_ref[rs, :] = x_ref[rs, :] + 0.5 * _rms(acc_ref[rs, :], gpost_ref[...])


def _ffn(x, g_pre, w_gate, w_up, w_down, g_post):
    t = x.shape[0]
    tile = FFN_SPLIT * TOK_TILE
    row = lambda i: (i, 0)
    fixed = lambda i: (0, 0)
    once = pl.Buffered(1)
    weights = 3 * _nbytes((D_MODEL, D_FF), F32)
    blocks = 2 * _nbytes((tile, D_MODEL), F32)
    return pl.pallas_call(
        _ffn_kernel,
        grid=(t // tile,),
        in_specs=[
            pl.BlockSpec((tile, D_MODEL), row),
            pl.BlockSpec((1, D_MODEL), fixed),
            pl.BlockSpec((D_MODEL, D_FF), fixed, pipeline_mode=once),
            pl.BlockSpec((D_MODEL, D_FF), fixed, pipeline_mode=once),
            pl.BlockSpec((D_FF, D_MODEL), fixed, pipeline_mode=once),
            pl.BlockSpec((1, D_MODEL), fixed),
        ],
        out_specs=pl.BlockSpec((tile, D_MODEL), row),
        out_shape=jax.ShapeDtypeStruct((t, D_MODEL), F32),
        scratch_shapes=[pltpu.VMEM((tile, D_MODEL), F32)],
        compiler_params=pltpu.CompilerParams(
            dimension_semantics=("arbitrary",),
            vmem_limit_bytes=_vmem_limit(blocks, weights + _nbytes((tile, D_MODEL), F32))),
        name="ffn",
    )(x, g_pre.reshape(1, -1), w_gate, w_up, w_down, g_post.reshape(1, -1))


OFF_Q = 0
OFF_CMP = OFF_Q + Q_WIDTH
OFF_KS = OFF_CMP + 2 * KV_WIDTH
OFF_VS = OFF_KS + KV_WIDTH
OFF_KW = OFF_VS + KV_WIDTH
OFF_VW = OFF_KW + KV_WIDTH
OFF_GATE = OFF_VW + KV_WIDTH
OFF_POOL = OFF_GATE + N_GATES
OFF_MERGE = OFF_POOL + POOL_WIDTH
W_CMP = 2 * KV_WIDTH
CHUNK_TOK = CMP_STRIDE
VT_SLAB = N_KV * VT_ROWS


def _in_proj_kernel(x_ref, g_ref, w_ref, tab_ref, q_ref, ks_ref, kw_ref, vst_ref,
                    vwt_ref, cmp_ref, gate_ref, pool_ref, cmp_scr):
    hn = _rms(x_ref[...], g_ref[...])
    tab = tab_ref[...]
    rows = x_ref.shape[0]
    wide = 4 * SLOT
    tab_q = tab * (HEAD_DIM ** -0.5 * LOG2_E)
    for c in range(Q_WIDTH // wide):
        y = _dot_nt(hn, w_ref[OFF_Q + c * wide:OFF_Q + (c + 1) * wide, :])
        for j in range(wide // SLOT):
            sl = slice(c * wide + j * SLOT, c * wide + (j + 1) * SLOT)
            q_ref[:, sl] = _rope_slot(y[:, j * SLOT:(j + 1) * SLOT], tab_q).astype(BF16)
    for k_ref, off in ((ks_ref, OFF_KS), (kw_ref, OFF_KW)):
        y = _dot_nt(hn, w_ref[off:off + KV_WIDTH, :])
        for j in range(KV_WIDTH // SLOT):
            sl = slice(j * SLOT, (j + 1) * SLOT)
            k_ref[:, sl] = _rope_slot(y[:, sl], tab).astype(BF16)
    y = _dot_nt(hn, w_ref[OFF_CMP:OFF_CMP + W_CMP, :])
    for j in range(W_CMP // LANES):
        cmp_scr[j] = y[:, j * LANES:(j + 1) * LANES]
    for l in range(CHUNK_TOK):
        for j in range(W_CMP // LANES):
            c0 = l * W_CMP + j * LANES
            cmp_ref[:, c0:c0 + LANES] = (
                cmp_scr[j, pl.ds(l, rows // CHUNK_TOK, stride=CHUNK_TOK), :].astype(BF16))
    gate_ref[...] = _dot_nt(hn, w_ref[OFF_GATE:OFF_GATE + SLOT, :])
    pool_ref[...] = _dot_nt(hn, w_ref[OFF_POOL:OFF_MERGE, :])
    aux = (lax.broadcasted_iota(jnp.int32, (VT_ROWS - HEAD_DIM, KT), 0) == 0).astype(BF16)
    for vt_ref, off in ((vst_ref, OFF_VS), (vwt_ref, OFF_VW)):
        vt = _dot_nt(w_ref[off:off + KV_WIDTH, :], hn).astype(BF16)
        for j in range(rows // KT):
            for g in range(N_KV):
                vt_ref[j, g * VT_ROWS:g * VT_ROWS + HEAD_DIM, :] = (
                    vt[g * HEAD_DIM:(g + 1) * HEAD_DIM, j * KT:(j + 1) * KT])
                vt_ref[j, g * VT_ROWS + HEAD_DIM:(g + 1) * VT_ROWS, :] = aux


def _in_proj(x1, g_pre, w_in_t, tab):
    t = x1.shape[0]
    row = lambda i: (i, 0)
    fixed = lambda i: (0, 0)
    once = pl.Buffered(1)
    tiles = TOK_TILE // KT
    chunk_rows = TOK_TILE // CHUNK_TOK
    slab = pl.BlockSpec((tiles, VT_SLAB, KT), lambda i: (i, 0, 0))
    slab_shape = jax.ShapeDtypeStruct((t // KT, VT_SLAB, KT), BF16)
    out_specs = [
        pl.BlockSpec((TOK_TILE, Q_WIDTH), row), pl.BlockSpec((TOK_TILE, KV_WIDTH), row),
        pl.BlockSpec((TOK_TILE, KV_WIDTH), row), slab, slab,
        pl.BlockSpec((chunk_rows, CHUNK_TOK * W_CMP), row),
        pl.BlockSpec((TOK_TILE, SLOT), row), pl.BlockSpec((TOK_TILE, POOL_WIDTH), row)]
    out_shape = [
        jax.ShapeDtypeStruct((t, Q_WIDTH), BF16), jax.ShapeDtypeStruct((t, KV_WIDTH), BF16),
        jax.ShapeDtypeStruct((t, KV_WIDTH), BF16), slab_shape, slab_shape,
        jax.ShapeDtypeStruct((t // CHUNK_TOK, CHUNK_TOK * W_CMP), BF16),
        jax.ShapeDtypeStruct((t, SLOT), F32), jax.ShapeDtypeStruct((t, POOL_WIDTH), F32)]
    weights = _nbytes(w_in_t.shape, F32)
    blocks = (_nbytes((TOK_TILE, D_MODEL), F32) + _nbytes((TOK_TILE, N_ROPE_TAB * SLOT), F32)
              + _nbytes((TOK_TILE, Q_WIDTH + 2 * KV_WIDTH + W_CMP), BF16)
              + 2 * _nbytes((tiles, VT_SLAB, KT), BF16) + _nbytes((TOK_TILE, SLOT + POOL_WIDTH), F32))
    res = pl.pallas_call(
        _in_proj_kernel,
        grid=(t // TOK_TILE,),
        in_specs=[
            pl.BlockSpec((TOK_TILE, D_MODEL), row),
            pl.BlockSpec((1, D_MODEL), fixed),
            pl.BlockSpec(w_in_t.shape, fixed, pipeline_mode=once),
            pl.BlockSpec((TOK_TILE, N_ROPE_TAB * SLOT), row),
        ],
        out_specs=out_specs,
        out_shape=out_shape,
        scratch_shapes=[pltpu.VMEM((W_CMP // LANES, TOK_TILE, LANES), F32)],
        compiler_params=pltpu.CompilerParams(
            dimension_semantics=("arbitrary",),
            vmem_limit_bytes=_vmem_limit(blocks, weights + _nbytes((TOK_TILE, W_CMP), F32))),
        name="in_proj",
    )(x1, g_pre.reshape(1, -1), w_in_t, tab)
    return res


def _compress_kernel(x_ref, w1k_ref, w1v_ref, pek_ref, pev_ref, w1k_raw_ref, w1v_raw_ref,
                     w2k_ref, w2v_ref, tab_ref, kc_ref, vc_ref, acck_ref, accv_ref):
    l = pl.program_id(0)

    @pl.when(l == 0)
    def _():
        acck_ref[...] = jnp.zeros(acck_ref.shape, F32)
        accv_ref[...] = jnp.zeros(accv_ref.shape, F32)

    for acc_ref, w1_ref, c0 in ((acck_ref, w1k_ref, 0), (accv_ref, w1v_ref, KV_WIDTH)):
        for g in range(N_KV):
            acc_ref[g] += _dot(x_ref[:, c0 + g * HEAD_DIM:c0 + (g + 1) * HEAD_DIM], w1_ref[0])

    @pl.when(l == CHUNK_TOK - 1)
    def _():
        rows = acck_ref.shape[1]
        tab = tab_ref[...]
        for acc_ref, pe_ref, w1_raw_ref, w2_ref, o_ref, rope in (
                (acck_ref, pek_ref, w1k_raw_ref, w2k_ref, kc_ref, True),
                (accv_ref, pev_ref, w1v_raw_ref, w2v_ref, vc_ref, False)):
            bias = _dot(pe_ref[...], w1_raw_ref[...])[0:1, :]
            for g in range(N_KV):
                first = acc_ref[g, :, 0:CMP_HIDDEN]
                second = pltpu.roll(acc_ref[g, :, CMP_HIDDEN:2 * CMP_HIDDEN], rows - 1, axis=0)
                hid = jax.nn.gelu(first + second + bias).astype(BF16)
                out = _dot(hid, w2_ref[...])
                if rope:
                    out = _rope_slot(out, tab)
                o_ref[:, g * SLOT:(g + 1) * SLOT] = out.astype(BF16)


def _compress_weights(w1, w2):
    w1 = w1.reshape(2, CHUNK_TOK, HEAD_DIM, CMP_HIDDEN)
    w1_tok = jnp.concatenate([w1[0], w1[1]], axis=-1).astype(BF16)
    return w1_tok, jnp.pad(w2, ((0, 0), (0, SLOT - HEAD_DIM))).astype(BF16)


def _compress(cmp_rows, cmp_pe_k, cmp_w1_k, cmp_w2_k, cmp_pe_v, cmp_w1_v, cmp_w2_v, tab_cmp):
    rows = cmp_rows.shape[0]
    w1k, w2k = _compress_weights(cmp_w1_k, cmp_w2_k)
    w1v, w2v = _compress_weights(cmp_w1_v, cmp_w2_v)
    sub = 8
    pek = jnp.broadcast_to(cmp_pe_k.reshape(1, -1), (sub, CMP_LEN * HEAD_DIM)).astype(BF16)
    pev = jnp.broadcast_to(cmp_pe_v.reshape(1, -1), (sub, CMP_LEN * HEAD_DIM)).astype(BF16)
    fixed = lambda l: (0, 0)
    w_kv_pad = N_KV * SLOT
    blocks = (_nbytes((rows, W_CMP), BF16) + 2 * _nbytes((HEAD_DIM, 2 * CMP_HIDDEN), BF16)
              + 2 * _nbytes((sub + CMP_LEN * HEAD_DIM, CMP_HIDDEN), BF16)
              + 2 * _nbytes((CMP_HIDDEN, SLOT), BF16)
              + _nbytes((rows, N_ROPE_TAB * SLOT), F32) + 2 * _nbytes((rows, w_kv_pad), BF16))
    acc = pltpu.VMEM((N_KV, rows, 2 * CMP_HIDDEN), F32)
    return pl.pallas_call(
        _compress_kernel,
        grid=(CHUNK_TOK,),
        in_specs=[
            pl.BlockSpec((rows, W_CMP), lambda l: (0, l)),
            pl.BlockSpec((1, HEAD_DIM, 2 * CMP_HIDDEN), lambda l: (l, 0, 0)),
            pl.BlockSpec((1, HEAD_DIM, 2 * CMP_HIDDEN), lambda l: (l, 0, 0)),
            pl.BlockSpec((sub, CMP_LEN * HEAD_DIM), fixed),
            pl.BlockSpec((sub, CMP_LEN * HEAD_DIM), fixed),
            pl.BlockSpec((CMP_LEN * HEAD_DIM, CMP_HIDDEN), fixed),
            pl.BlockSpec((CMP_LEN * HEAD_DIM, CMP_HIDDEN), fixed),
            pl.BlockSpec((CMP_HIDDEN, SLOT), fixed),
            pl.BlockSpec((CMP_HIDDEN, SLOT), fixed),
            pl.BlockSpec((rows, N_ROPE_TAB * SLOT), fixed),
        ],
        out_specs=[pl.BlockSpec((rows, w_kv_pad), fixed)] * 2,
        out_shape=[jax.ShapeDtypeStruct((rows, w_kv_pad), BF16)] * 2,
        scratch_shapes=[acc, acc],
        compiler_params=pltpu.CompilerParams(
            dimension_semantics=("arbitrary",),
            vmem_limit_bytes=_vmem_limit(blocks, 2 * _nbytes(acc.shape, F32))),
        name="compress",
    )(cmp_rows, w1k, w1v, pek, pev, cmp_w1_k.astype(BF16), cmp_w1_v.astype(BF16), w2k, w2v, tab_cmp)


def _nsa_kernel(q_ref, gate_ref, kc_ref, vc_ref, ks_ref, vst_ref, kw_ref, vwt_ref, ov_ref,
                o_ref, ksp_ref, kwp_ref, qg_ref, mix_ref, *state_refs, n_slc):
    i = pl.program_id(1)
    start = i * TQ
    seq = ks_ref.shape[1]
    cols_g = GROUP * TQ
    chains_g = cols_g // CHAIN_COLS
    n_chains = N_KV * chains_g
    n_cmp = kc_ref.shape[1]
    sel_state = (state_refs[0:n_chains], state_refs[n_chains:2 * n_chains])
    win_state = (state_refs[2 * n_chains:3 * n_chains], state_refs[3 * n_chains:4 * n_chains])
    lane_q = lax.broadcasted_iota(jnp.int32, (TQ, SLOT), 1)

    def head_slot(x_ref, rows, j):
        two = x_ref[rows, (j // 2) * SLOT:(j // 2 + 1) * SLOT].astype(F32)
        return pltpu.roll(two, HEAD_DIM, axis=1) if j % 2 else two

    @pl.when(i == 0)
    def _():
        for c in range(seq // KT):
            rows = pl.ds(c * KT, KT)
            lane = lax.broadcasted_iota(jnp.int32, (KT, SLOT), 1)
            key_blk = (c * KT + lax.broadcasted_iota(jnp.int32, (KT, SLOT), 0)) >> SEL_SHIFT
            onehot = (lane - BLK_LANE0 == key_blk).astype(F32)
            for g in range(N_KV):
                gs = slice(g * SLOT, (g + 1) * SLOT)
                ksp_ref[rows, gs] = jnp.where(lane < HEAD_DIM, head_slot(ks_ref.at[0], rows, g),
                                              onehot).astype(BF16)
                kwp_ref[rows, gs] = jnp.where(lane < HEAD_DIM, head_slot(kw_ref.at[0], rows, g),
                                              0.0).astype(BF16)

    for h in range(N_HEADS):
        g, r = divmod(h, GROUP)
        qg_ref[g, r * TQ:(r + 1) * TQ, :] = jnp.where(
            lane_q < HEAD_DIM, head_slot(q_ref, slice(None), h), 0.0).astype(BF16)

    gates_t = jax.nn.sigmoid(gate_ref[...]).T
    any_cmp = start + lax.broadcasted_iota(jnp.int32, (1, TQ), 1) >= CMP_LEN - 1
    key_in = lax.broadcasted_iota(jnp.int32, (KT, CHAIN_COLS), 0)
    qry_in = lax.broadcasted_iota(jnp.int32, (KT, CHAIN_COLS), 1) & (TQ - 1)
    diag_mask = key_in <= qry_in
    far_mask = key_in > qry_in

    def gate_row(branch, h):
        c = branch * N_HEADS + h
        return gates_t[c:c + 1, :]

    sub = 8
    cmp_per_sub = sub * SEL_LEN // CMP_STRIDE

    def compressed_and_select(ng):
        n_blk, n_c = ng * sub, ng * cmp_per_sub
        t_cmp = start + lax.broadcasted_iota(jnp.int32, (n_c, TQ), 1)
        n_idx = lax.broadcasted_iota(jnp.int32, (n_c, TQ), 0)
        cmp_valid = n_idx * CMP_STRIDE + (CMP_LEN - 1) <= t_cmp
        t_row = start + lax.broadcasted_iota(jnp.int32, (n_blk, TQ), 1)
        blk = lax.broadcasted_iota(jnp.int32, (n_blk, TQ), 0)
        forced = (blk == t_row >> SEL_SHIFT) | (blk == 0)
        causal_blk = blk * SEL_LEN <= t_row
        idx8 = lax.broadcasted_iota(jnp.int32, (sub, TQ), 0)
        pad_c = jnp.zeros((n_cmp - n_c, TQ), F32)
        cmp_scores = []
        for h in range(N_HEADS):
            g, r = divmod(h, GROUP)
            cmp_scores.append(_dot_nt(kc_ref[0, 0:n_c, g * SLOT:(g + 1) * SLOT],
                                      qg_ref[g, r * TQ:(r + 1) * TQ, :]))
        for g in range(N_KV):
            vct_g = vc_ref[0, :, g * SLOT:(g + 1) * SLOT].astype(F32).T[0:HEAD_DIM].astype(BF16)
            p_sum = jnp.zeros((n_c, TQ), F32)
            for r in range(GROUP):
                h = g * GROUP + r
                s = jnp.where(cmp_valid, cmp_scores[h], NEG_INF)
                p = jnp.exp2(s - jnp.max(s, axis=0, keepdims=True))
                inv = jnp.where(any_cmp, 1.0 / jnp.sum(p, axis=0, keepdims=True), 0.0)
                p = p * inv
                p_sum = p_sum + p
                p_all = jnp.concatenate([p, pad_c], axis=0) if ng * cmp_per_sub < n_cmp else p
                mix_ref[h] = gate_row(0, h) * _dot(vct_g, p_all.astype(BF16))
            p_sum_all = jnp.concatenate([p_sum, pad_c], axis=0) if ng * cmp_per_sub < n_cmp else p_sum
            imp = jnp.dot(ov_ref[0:n_blk, :], p_sum_all, precision=lax.Precision.HIGHEST,
                          preferred_element_type=F32)
            score = jnp.where(causal_blk, jnp.where(forced, FORCE_SCORE, imp), NEG_INF)
            rows8 = [score[k * sub:(k + 1) * sub] for k in range(ng)]
            ranks = [jnp.zeros((sub, TQ), jnp.int32) for _ in rows8]
            for c in range(n_blk):
                other = score[c:c + 1, :]
                for k, mine in enumerate(rows8):
                    if k * sub > c:
                        beats = other >= mine
                    elif (k + 1) * sub - 1 <= c:
                        beats = other > mine
                    else:
                        beats = (other > mine) | ((other == mine) & (idx8 > c - k * sub))
                    ranks[k] = ranks[k] + beats.astype(jnp.int32)
            rank = jnp.concatenate(ranks, axis=0) if ng > 1 else ranks[0]
            bias = jnp.where(causal_blk & (rank < N_SEL), 0.0, NEG_INF)
            bias_t = jnp.concatenate([jnp.zeros((BLK_LANE0, TQ), F32), bias,
                                      jnp.zeros((SLOT - BLK_LANE0 - n_blk, TQ), F32)], axis=0)
            bias_q = bias_t.T.astype(BF16)
            for r in range(GROUP):
                rs = slice(r * TQ, (r + 1) * TQ)
                qg_ref[g, rs, :] = jnp.where(lane_q < HEAD_DIM, qg_ref[g, rs, :], bias_q)

    ng_max = n_slc // sub
    ng_now = ((i + 1) * TQ + sub * SEL_LEN - 1) // (sub * SEL_LEN)
    for ng in range(1, ng_max + 1):
        pl.when(ng_now == ng)(functools.partial(compressed_and_select, ng))

    def flash_init(state):
        for m_ref, acc_ref in zip(*state):
            m_ref[...] = jnp.full(m_ref.shape, NEG_INF, F32)
            acc_ref[...] = jnp.zeros(acc_ref.shape, F32)

    def flash_tiles(tiles):
        jobs = [(tile, c) for tile in tiles for c in range(n_chains)]
        scores = {}
        for step in range(len(jobs) + PIPE_DEPTH):
            if step < len(jobs):
                (kp_ref, _, kt, mask, _), chain = jobs[step]
                g, c = divmod(chain, chains_g)
                off = pl.multiple_of(kt * KT, KT)
                s = _dot_nt(kp_ref[pl.ds(off, KT), g * SLOT:(g + 1) * SLOT],
                            qg_ref[g, c * CHAIN_COLS:(c + 1) * CHAIN_COLS, :])
                scores[step] = s if mask is None else jnp.where(mask, s, NEG_INF)
            done = step - PIPE_DEPTH
            if done >= 0:
                (_, vt_ref, kt, _, (m_refs, acc_refs)), chain = jobs[done]
                g = chain // chains_g
                m_ref, acc_ref = m_refs[chain], acc_refs[chain]
                s = scores.pop(done)
                m_old = m_ref[...]
                m_new = jnp.maximum(m_old, jnp.max(s, axis=0, keepdims=True))
                p = jnp.exp2(s - m_new).astype(BF16)
                pv = _dot(vt_ref[kt, g * VT_ROWS:(g + 1) * VT_ROWS, :], p)
                acc_ref[...] = jnp.exp2(m_old - m_new) * acc_ref[...] + pv
                m_ref[...] = m_new

    def flash_mix(branch, state):
        _, acc_refs = state
        for h in range(N_HEADS):
            g, r = divmod(h, GROUP)
            c, c0 = divmod(r * TQ, CHAIN_COLS)
            acc = acc_refs[g * chains_g + c][:, c0:c0 + TQ]
            scale = gate_row(branch, h) * (1.0 / acc[ONE_ROW:ONE_ROW + 1, :])
            mix_ref[h] = mix_ref[h] + scale * acc[0:HEAD_DIM]

    flash_init(sel_state)
    flash_init(win_state)

    def sel_past(kt):
        return (ksp_ref, vst_ref, kt, None, sel_state)

    def sel_pair(j, carry):
        flash_tiles([sel_past(2 * j), sel_past(2 * j + 1)])
        return carry

    lax.fori_loop(0, i >> 1, sel_pair, 0)

    @pl.when(i & 1 == 1)
    def _():
        flash_tiles([sel_past(i - 1)])

    n_back = WINDOW // KT
    sel_diag = (ksp_ref, vst_ref, i, diag_mask, sel_state)
    win_diag = (kwp_ref, vwt_ref, i, diag_mask, win_state)

    def win_back(d):
        return (kwp_ref, vwt_ref, i - d, far_mask if d == n_back else None, win_state)

    for have in range(n_back + 1):
        cond = (i == have) if have < n_back else (i >= have)

        @pl.when(cond)
        def _(have=have):
            flash_tiles([sel_diag] + [win_back(d) for d in range(have, 0, -1)] + [win_diag])

    flash_mix(1, sel_state)
    flash_mix(2, win_state)
    for h2 in range(N_HEADS // 2):
        pair = jnp.concatenate([mix_ref[2 * h2], mix_ref[2 * h2 + 1]], axis=0)
        o_ref[:, h2 * SLOT:(h2 + 1) * SLOT] = pair.T.astype(BF16)


def _overlap(n_cmp_pad, n_slc):
    c0 = jnp.arange(n_cmp_pad) * CMP_STRIDE
    s0 = jnp.arange(n_slc) * SEL_LEN
    ov = jnp.minimum(c0[None, :] + CMP_LEN, s0[:, None] + SEL_LEN) - jnp.maximum(c0[None, :], s0[:, None])
    return jnp.clip(ov, 0).astype(F32) / CMP_LEN


def _nsa(q, gate, kc, vc, ks, vst, kw, vwt, batch, seq_len):
    n_slc = seq_len // SEL_LEN
    n_cmp_pad = kc.shape[1]
    w_kv_pad = N_KV * SLOT
    assert n_slc <= SLOT - BLK_LANE0 and n_slc % 8 == 0
    assert CHAIN_COLS % TQ == 0 and (GROUP * TQ) % CHAIN_COLS == 0
    ov = _overlap(n_cmp_pad, n_slc)
    nq = seq_len // TQ
    n_kt = seq_len // KT
    qrow = lambda b, i: (b * nq + i, 0)
    per_b = lambda b, i: (b, 0, 0)
    fixed = lambda b, i: (0, 0)
    blocks = (_nbytes((TQ, Q_WIDTH), BF16) + _nbytes((TQ, SLOT), F32)
              + 2 * _nbytes((n_cmp_pad, w_kv_pad), BF16) + 2 * _nbytes((seq_len, KV_WIDTH), BF16)
              + 2 * _nbytes((n_kt, VT_SLAB, KT), BF16)
              + _nbytes(ov.shape, F32) + _nbytes((TQ, Q_WIDTH), BF16))
    n_chains = N_KV * GROUP * TQ // CHAIN_COLS
    run_max = [pltpu.VMEM((1, CHAIN_COLS), F32)] * n_chains
    run_acc = [pltpu.VMEM((VT_ROWS, CHAIN_COLS), F32)] * n_chains
    scratch_shapes = (
        [pltpu.VMEM((seq_len, w_kv_pad), BF16),
         pltpu.VMEM((seq_len, w_kv_pad), BF16),
         pltpu.VMEM((N_KV, GROUP * TQ, SLOT), BF16),
         pltpu.VMEM((N_HEADS, HEAD_DIM, TQ), F32)]
        + run_max + run_acc + run_max + run_acc)
    scratch = sum(_nbytes(s.shape, s.dtype) for s in scratch_shapes)
    return pl.pallas_call(
        functools.partial(_nsa_kernel, n_slc=n_slc),
        grid=(batch, nq),
        in_specs=[
            pl.BlockSpec((TQ, Q_WIDTH), qrow),
            pl.BlockSpec((TQ, SLOT), qrow),
            pl.BlockSpec((1, n_cmp_pad, w_kv_pad), per_b),
            pl.BlockSpec((1, n_cmp_pad, w_kv_pad), per_b),
            pl.BlockSpec((1, seq_len, KV_WIDTH), per_b),
            pl.BlockSpec((n_kt, VT_SLAB, KT), per_b),
            pl.BlockSpec((1, seq_len, KV_WIDTH), per_b),
            pl.BlockSpec((n_kt, VT_SLAB, KT), per_b),
            pl.BlockSpec(ov.shape, fixed),
        ],
        out_specs=pl.BlockSpec((TQ, Q_WIDTH), qrow),
        out_shape=jax.ShapeDtypeStruct((batch * seq_len, Q_WIDTH), BF16),
        scratch_shapes=scratch_shapes,
        compiler_params=pltpu.CompilerParams(
            dimension_semantics=("arbitrary", "arbitrary"),
            vmem_limit_bytes=_vmem_limit(blocks, scratch)),
        name="nsa",
    )(q, gate, kc.reshape(batch, n_cmp_pad, w_kv_pad), vc.reshape(batch, n_cmp_pad, w_kv_pad),
      ks.reshape(batch, seq_len, KV_WIDTH), vst, kw.reshape(batch, seq_len, KV_WIDTH), vwt, ov)


def _pool_kernel(u_ref, w_ref, scale_ref, o_ref):
    seq = u_ref.shape[0]
    t = lax.broadcasted_iota(jnp.int32, (seq, POOL_GROUP), 0)

    def shifted(x, k):
        return jnp.where(t >= k, pltpu.roll(x, k, axis=0), 0.0)

    for gi, w in enumerate(POOL_WINDOWS):
        sl = slice(gi * POOL_GROUP, (gi + 1) * POOL_GROUP)
        x = u_ref[:, sl]
        wsum = x
        span = 1
        while span < w:
            wsum = wsum + shifted(wsum, span)
            span *= 2
        cnt = jnp.minimum(t + 1, w).astype(F32)
        pooled = (wsum / cnt - x).astype(BF16)
        o_ref[:, sl] = (_dot(pooled, w_ref[gi]) * scale_ref[:, sl]).astype(BF16)


def _pool(pool_in, pool_w, pool_scale, batch, seq_len):
    for w in POOL_WINDOWS:
        assert w & (w - 1) == 0
    blocks = (_nbytes((seq_len, POOL_WIDTH), F32) + _nbytes(pool_w.shape, BF16)
              + _nbytes((seq_len, POOL_WIDTH), BF16))
    return pl.pallas_call(
        _pool_kernel,
        grid=(batch,),
        in_specs=[
            pl.BlockSpec((seq_len, POOL_WIDTH), lambda b: (b, 0)),
            pl.BlockSpec(pool_w.shape, lambda b: (0, 0, 0)),
            pl.BlockSpec((1, POOL_WIDTH), lambda b: (0, 0)),
        ],
        out_specs=pl.BlockSpec((seq_len, POOL_WIDTH), lambda b: (b, 0)),
        out_shape=jax.ShapeDtypeStruct((batch * seq_len, POOL_WIDTH), BF16),
        compiler_params=pltpu.CompilerParams(
            dimension_semantics=("arbitrary",), vmem_limit_bytes=_vmem_limit(blocks)),
        name="pool",
    )(pool_in, pool_w.astype(BF16), pool_scale.reshape(1, -1))


def _merge_kernel(x_ref, gpre_ref, wt_ref, nsa_ref, wa_ref, pool_ref, wp_ref, wo_ref, gpost_ref,
                  o_ref):
    x = x_ref[...]
    hn = _rms(x, gpre_ref[...])
    g_attn = jax.nn.sigmoid(_dot_nt(hn, wt_ref[OFF_MERGE:OFF_MERGE + D_MODEL, :]))
    g_pool = jax.nn.sigmoid(_dot_nt(hn, wt_ref[OFF_MERGE + D_MODEL:OFF_MERGE + 2 * D_MODEL, :]))
    y = (g_attn * _dot(nsa_ref[...].astype(F32), wa_ref[...])
         + g_pool * _dot(pool_ref[...].astype(F32), wp_ref[...]))
    h = _dot(y, wo_ref[...])
    o_ref[...] = x + _rms(h, gpost_ref[...])


def _merge(x1, g_pre, w_in_t, o_nsa, w_attn, o_pool, w_pool, w_out, g_post):
    t = x1.shape[0]
    row = lambda i: (i, 0)
    fixed = lambda i: (0, 0)
    once = pl.Buffered(1)
    weights = (_nbytes(w_in_t.shape, F32) + _nbytes((Q_WIDTH, D_MODEL), F32)
               + _nbytes((POOL_WIDTH, D_MODEL), F32) + _nbytes((D_MODEL, D_MODEL), F32))
    blocks = (2 * _nbytes((TOK_TILE, D_MODEL), F32) + _nbytes((TOK_TILE, Q_WIDTH), BF16)
              + _nbytes((TOK_TILE, POOL_WIDTH), BF16))
    return pl.pallas_call(
        _merge_kernel,
        grid=(t // TOK_TILE,),
        in_specs=[
            pl.BlockSpec((TOK_TILE, D_MODEL), row),
            pl.BlockSpec((1, D_MODEL), fixed),
            pl.BlockSpec(w_in_t.shape, fixed, pipeline_mode=once),
            pl.BlockSpec((TOK_TILE, Q_WIDTH), row),
            pl.BlockSpec((Q_WIDTH, D_MODEL), fixed, pipeline_mode=once),
            pl.BlockSpec((TOK_TILE, POOL_WIDTH), row),
            pl.BlockSpec((POOL_WIDTH, D_MODEL), fixed, pipeline_mode=once),
            pl.BlockSpec((D_MODEL, D_MODEL), fixed, pipeline_mode=once),
            pl.BlockSpec((1, D_MODEL), fixed),
        ],
        out_specs=pl.BlockSpec((TOK_TILE, D_MODEL), row),
        out_shape=jax.ShapeDtypeStruct((t, D_MODEL), F32),
        compiler_params=pltpu.CompilerParams(
            dimension_semantics=("arbitrary",), vmem_limit_bytes=_vmem_limit(blocks, weights)),
        name="merge",
    )(x1, g_pre.reshape(1, -1), w_in_t, o_nsa, w_attn, o_pool, w_pool, w_out, g_post.reshape(1, -1))


def _layer(x, tab, tab_cmp, batch, seq_len, g_ffn1_pre, w_ffn1_gate, w_ffn1_up, w_ffn1_down,
           g_ffn1_post, g_mix_pre, w_in, cmp_pe_k, cmp_w1_k, cmp_w2_k, cmp_pe_v, cmp_w1_v, cmp_w2_v,
           w_attn_branch, pool_w, pool_scale, w_pool_branch, w_out, g_mix_post, g_ffn2_pre,
           w_ffn2_gate, w_ffn2_up, w_ffn2_down, g_ffn2_post):
    x1 = _ffn(x, g_ffn1_pre, w_ffn1_gate, w_ffn1_up, w_ffn1_down, g_ffn1_post)
    w_in_t = w_in.T
    q, ks, kw, vst, vwt, cmp_rows, gate, pool_in = _in_proj(x1, g_mix_pre, w_in_t, tab)
    kc, vc = _compress(cmp_rows, cmp_pe_k, cmp_w1_k, cmp_w2_k, cmp_pe_v, cmp_w1_v, cmp_w2_v, tab_cmp)
    n_cmp_pad = seq_len // CMP_STRIDE
    kc = kc.reshape(batch, n_cmp_pad, N_KV * SLOT)
    vc = vc.reshape(batch, n_cmp_pad, N_KV * SLOT)
    o_nsa = _nsa(q, gate, kc, vc, ks, vst, kw, vwt, batch, seq_len)
    o_pool = _pool(pool_in, pool_w, pool_scale, batch, seq_len)
    x2 = _merge(x1, g_mix_pre, w_in_t, o_nsa, w_attn_branch, o_pool, w_pool_branch, w_out,
                g_mix_post)
    return _ffn(x2, g_ffn2_pre, w_ffn2_gate, w_ffn2_up, w_ffn2_down, g_ffn2_post)


def kernel(x, positions, g_ffn1_pre, w_ffn1_gate, w_ffn1_up, w_ffn1_down, g_ffn1_post, g_mix_pre, w_in, cmp_pe_k, cmp_w1_k, cmp_w2_k, cmp_pe_v, cmp_w1_v, cmp_w2_v, w_attn_branch, pool_w, pool_scale, w_pool_branch, w_out, g_mix_post, g_ffn2_pre, w_ffn2_gate, w_ffn2_up, w_ffn2_down, g_ffn2_post):
    batch, seq_len, d_model = x.shape
    assert d_model == D_MODEL and seq_len % TOK_TILE == 0 and seq_len % TQ == 0
    assert TQ == KT and TQ % SEL_LEN == 0 and TOK_TILE % KT == 0 and WINDOW % KT == 0
    assert seq_len % CMP_STRIDE == 0 and CMP_LEN == 2 * CMP_STRIDE
    t = batch * seq_len
    tab = _rope_tables(positions.reshape(t), seq_len)
    n_cmp = (seq_len - CMP_LEN) // CMP_STRIDE + 1
    n_cmp_pad = seq_len // CMP_STRIDE
    pos_cmp = jnp.pad(positions[:, CMP_LEN - 1::CMP_STRIDE], ((0, 0), (0, n_cmp_pad - n_cmp)))
    tab_cmp = _rope_tables(pos_cmp.reshape(batch * n_cmp_pad), batch * n_cmp_pad)
    xf = x.reshape(t, D_MODEL)
    per_layer = (g_ffn1_pre, w_ffn1_gate, w_ffn1_up, w_ffn1_down, g_ffn1_post, g_mix_pre, w_in,
                 cmp_pe_k, cmp_w1_k, cmp_w2_k, cmp_pe_v, cmp_w1_v, cmp_w2_v, w_attn_branch, pool_w,
                 pool_scale, w_pool_branch, w_out, g_mix_post, g_ffn2_pre, w_ffn2_gate, w_ffn2_up,
                 w_ffn2_down, g_ffn2_post)
    for l in range(g_ffn1_pre.shape[0]):
        xf = _layer(xf, tab, tab_cmp, batch, seq_len, *(p[l] for p in per_layer))
    return xf.reshape(batch, seq_len, D_MODEL)
```

```python
import functools
import math

import jax
import jax.numpy as jnp
from jax import lax
from jax.experimental import pallas as pl
from jax.experimental.pallas import tpu as pltpu

F32 = jnp.float32
BF16 = jnp.bfloat16

D_MODEL = 1024
N_HEADS = 16
HEAD_DIM = 64
N_KV = 4
GROUP = N_HEADS // N_KV
ROT_DIM = HEAD_DIM // 4
ROT_HALF = ROT_DIM // 2
ROPE_THETA = 500000.0
CMP_LEN = 32
CMP_STRIDE = 16
CMP_HIDDEN = 2 * HEAD_DIM
SEL_LEN = 64
SEL_SHIFT = SEL_LEN.bit_length() - 1
N_SEL = 8
WINDOW = 512
POOL_WINDOWS = (2, 4, 8, 16)
POOL_WIDTH = D_MODEL // 2
POOL_GROUP = POOL_WIDTH // len(POOL_WINDOWS)
D_FF = 2816
EPS = 1e-6
NEG_INF = -1e30
FORCE_SCORE = 1e4
Q_WIDTH = N_HEADS * HEAD_DIM
KV_WIDTH = N_KV * HEAD_DIM
N_GATES = 3 * N_HEADS
LOG2_E = math.log2(math.e)

LANES = 128
BF16_SUBLANES = 16
V7X_VMEM_BYTES = 64 * 1024 * 1024

SLOT = LANES
BLK_LANE0 = HEAD_DIM
VT_ROWS = HEAD_DIM + BF16_SUBLANES
ONE_ROW = HEAD_DIM
TOK_TILE = 512
TQ = 256
KT = 256
HALF = TQ // 2
CHAIN_COLS = 256
PIPE_DEPTH = 5
FF_CHUNK = 256
IN_SPLIT = 2
N_ROPE_TAB = 2


def _vmem_limit(block_bytes, scratch_bytes=0):
    need = 2 * block_bytes + scratch_bytes
    return int(min(V7X_VMEM_BYTES - (4 << 20), max(2 * need, 32 << 20)))


def _nbytes(shape, dtype):
    n = 1
    for s in shape:
        n *= s
    return n * jnp.dtype(dtype).itemsize


def _rms(xf, g):
    return xf * lax.rsqrt(jnp.mean(xf * xf, axis=-1, keepdims=True) + EPS) * g


def _dot(a, b):
    return jnp.dot(a, b, preferred_element_type=F32)


def _dot_nt(a, b):
    return lax.dot_general(a, b, (((1,), (1,)), ((), ())), preferred_element_type=F32)


def _rope_trig_kernel(pos_ref, inv_ref, tab_ref):
    tn = pos_ref.shape[1]
    ang = pos_ref[...].astype(F32) * inv_ref[...]
    c, s = jnp.cos(ang), jnp.sin(ang)
    rest = HEAD_DIM - ROT_DIM
    reps = SLOT // HEAD_DIM
    cos_rows = jnp.concatenate([c, c, jnp.ones((rest, tn), F32)] * reps, axis=0)
    sin_rows = jnp.concatenate([-s, s, jnp.zeros((rest, tn), F32)] * reps, axis=0)
    for j in range(tn // LANES):
        cs = slice(j * LANES, (j + 1) * LANES)
        tab_ref[cs, 0:SLOT] = cos_rows[:, cs].T
        tab_ref[cs, SLOT:2 * SLOT] = sin_rows[:, cs].T


def _rope_tables(pos_flat, tile):
    n = pos_flat.shape[0]
    inv = ROPE_THETA ** (-jnp.arange(ROT_HALF, dtype=F32) * (2.0 / ROT_DIM))
    return pl.pallas_call(
        _rope_trig_kernel,
        grid=(n // tile,),
        in_specs=[pl.BlockSpec((1, tile), lambda i: (0, i)),
                  pl.BlockSpec((ROT_HALF, 1), lambda i: (0, 0))],
        out_specs=pl.BlockSpec((tile, N_ROPE_TAB * SLOT), lambda i: (i, 0)),
        out_shape=jax.ShapeDtypeStruct((n, N_ROPE_TAB * SLOT), F32),
        compiler_params=pltpu.CompilerParams(dimension_semantics=("arbitrary",)),
        name="rope_trig",
    )(pos_flat.reshape(1, n), inv.reshape(ROT_HALF, 1))


def _rope_slot(y, tab):
    cos_t = tab[:, 0:SLOT]
    sin_t = tab[:, SLOT:2 * SLOT]
    lane = lax.broadcasted_iota(jnp.int32, y.shape, 1) & (HEAD_DIM - 1)
    up = pltpu.roll(y, SLOT - ROT_HALF, axis=1)
    down = pltpu.roll(y, ROT_HALF, axis=1)
    return y * cos_t + jnp.where(lane < ROT_HALF, up, down) * sin_t


def _ffn_kernel(x_ref, gpre_ref, wg_ref, wu_ref, wd_ref, gpost_ref, o_ref, acc_ref):
    x = x_ref[...]
    hn = _rms(x, gpre_ref[...])
    for c in range(D_FF // FF_CHUNK):
        sl = slice(c * FF_CHUNK, (c + 1) * FF_CHUNK)
        g = _dot(hn, wg_ref[:, sl])
        u = _dot(hn, wu_ref[:, sl])
        d = _dot(g * jax.nn.sigmoid(g) * u, wd_ref[sl, :])
        if c == 0:
            acc_ref[...] = d
        else:
            acc_ref[...] += d
    o_ref[...] = x + 0.5 * _rms(acc_ref[...], gpost_ref[...])


def _ffn(x, g_pre, w_gate, w_up, w_down, g_post):
    t = x.shape[0]
    tile = TOK_TILE
    row = lambda i: (i, 0)
    fixed = lambda i: (0, 0)
    once = pl.Buffered(1)
    weights = 3 * _nbytes((D_MODEL, D_FF), F32)
    blocks = 2 * _nbytes((tile, D_MODEL), F32)
    return pl.pallas_call(
        _ffn_kernel,
        grid=(t // tile,),
        in_specs=[
            pl.BlockSpec((tile, D_MODEL), row),
            pl.BlockSpec((1, D_MODEL), fixed),
            pl.BlockSpec((D_MODEL, D_FF), fixed, pipeline_mode=once),
            pl.BlockSpec((D_MODEL, D_FF), fixed, pipeline_mode=once),
            pl.BlockSpec((D_FF, D_MODEL), fixed, pipeline_mode=once),
            pl.BlockSpec((1, D_MODEL), fixed),
        ],
        out_specs=pl.BlockSpec((tile, D_MODEL), row),
        out_shape=jax.ShapeDtypeStruct((t, D_MODEL), F32),
        scratch_shapes=[pltpu.VMEM((tile, D_MODEL), F32)],
        compiler_params=pltpu.CompilerParams(
            dimension_semantics=("arbitrary",),
            vmem_limit_bytes=_vmem_limit(blocks, weights + _nbytes((tile, D_MODEL), F32))),
        name="ffn",
    )(x, g_pre.reshape(1, -1), w_gate, w_up, w_down, g_post.reshape(1, -1))


OFF_Q = 0
OFF_CMP = OFF_Q + Q_WIDTH
OFF_KS = OFF_CMP + 2 * KV_WIDTH
OFF_VS = OFF_KS + KV_WIDTH
OFF_KW = OFF_VS + KV_WIDTH
OFF_VW = OFF_KW + KV_WIDTH
OFF_GATE = OFF_VW + KV_WIDTH
OFF_POOL = OFF_GATE + N_GATES
OFF_MERGE = OFF_POOL + POOL_WIDTH
W_CMP = 2 * KV_WIDTH
CHUNK_TOK = CMP_STRIDE
VT_SLAB = N_KV * VT_ROWS


def _in_proj_kernel(x_ref, g_ref, w_ref, tab_ref, q_ref, ks_ref, kw_ref, vst_ref,
                    vwt_ref, cmp_ref, gate_ref, pool_ref, cmp_scr):
    rows = x_ref.shape[0] // IN_SPLIT
    wide = 4 * SLOT
    aux = (lax.broadcasted_iota(jnp.int32, (VT_ROWS - HEAD_DIM, KT), 0) == 0).astype(BF16)

    def part(k):
        rs = slice(k * rows, (k + 1) * rows)
        hn = _rms(x_ref[rs, :], g_ref[...])
        tab = tab_ref[rs, :]
        tab_q = tab * (HEAD_DIM ** -0.5 * LOG2_E)
        for c in range(Q_WIDTH // wide):
            y = _dot_nt(hn, w_ref[OFF_Q + c * wide:OFF_Q + (c + 1) * wide, :])
            for j in range(wide // SLOT):
                sl = slice(c * wide + j * SLOT, c * wide + (j + 1) * SLOT)
                q_ref[rs, sl] = _rope_slot(y[:, j * SLOT:(j + 1) * SLOT], tab_q).astype(BF16)
        for k_ref, off in ((ks_ref, OFF_KS), (kw_ref, OFF_KW)):
            y = _dot_nt(hn, w_ref[off:off + KV_WIDTH, :])
            for j in range(KV_WIDTH // SLOT):
                sl = slice(j * SLOT, (j + 1) * SLOT)
                k_ref[rs, sl] = _rope_slot(y[:, sl], tab).astype(BF16)
        y = _dot_nt(hn, w_ref[OFF_CMP:OFF_CMP + W_CMP, :])
        for j in range(W_CMP // LANES):
            cmp_scr[j, rs, :] = y[:, j * LANES:(j + 1) * LANES]
        chunk_rows = rows // CHUNK_TOK
        for l in range(CHUNK_TOK):
            for j in range(W_CMP // LANES):
                c0 = l * W_CMP + j * LANES
                cmp_ref[k * chunk_rows:(k + 1) * chunk_rows, c0:c0 + LANES] = (
                    cmp_scr[j, pl.ds(k * rows + l, chunk_rows, stride=CHUNK_TOK), :].astype(BF16))
        gate_ref[rs, :] = _dot_nt(hn, w_ref[OFF_GATE:OFF_GATE + SLOT, :])
        pool_ref[rs, :] = _dot_nt(hn, w_ref[OFF_POOL:OFF_MERGE, :])
        tiles = rows // KT
        for vt_ref, off in ((vst_ref, OFF_VS), (vwt_ref, OFF_VW)):
            vt = _dot_nt(w_ref[off:off + KV_WIDTH, :], hn).astype(BF16)
            for j in range(tiles):
                for g in range(N_KV):
                    vt_ref[k * tiles + j, g * VT_ROWS:g * VT_ROWS + HEAD_DIM, :] = (
                        vt[g * HEAD_DIM:(g + 1) * HEAD_DIM, j * KT:(j + 1) * KT])
                    vt_ref[k * tiles + j, g * VT_ROWS + HEAD_DIM:(g + 1) * VT_ROWS, :] = aux

    for k in range(IN_SPLIT):
        part(k)


def _in_proj(x1, g_pre, w_in_t, tab):
    t = x1.shape[0]
    row = lambda i: (i, 0)
    fixed = lambda i: (0, 0)
    once = pl.Buffered(1)
    tile = IN_SPLIT * TOK_TILE
    tiles = tile // KT
    chunk_rows = tile // CHUNK_TOK
    slab = pl.BlockSpec((tiles, VT_SLAB, KT), lambda i: (i, 0, 0))
    slab_shape = jax.ShapeDtypeStruct((t // KT, VT_SLAB, KT), BF16)
    out_specs = [
        pl.BlockSpec((tile, Q_WIDTH), row), pl.BlockSpec((tile, KV_WIDTH), row),
        pl.BlockSpec((tile, KV_WIDTH), row), slab, slab,
        pl.BlockSpec((chunk_rows, CHUNK_TOK * W_CMP), row),
        pl.BlockSpec((tile, SLOT), row), pl.BlockSpec((tile, POOL_WIDTH), row)]
    out_shape = [
        jax.ShapeDtypeStruct((t, Q_WIDTH), BF16), jax.ShapeDtypeStruct((t, KV_WIDTH), BF16),
        jax.ShapeDtypeStruct((t, KV_WIDTH), BF16), slab_shape, slab_shape,
        jax.ShapeDtypeStruct((t // CHUNK_TOK, CHUNK_TOK * W_CMP), BF16),
        jax.ShapeDtypeStruct((t, SLOT), F32), jax.ShapeDtypeStruct((t, POOL_WIDTH), F32)]
    weights = _nbytes(w_in_t.shape, F32)
    blocks = (_nbytes((tile, D_MODEL), F32) + _nbytes((tile, N_ROPE_TAB * SLOT), F32)
              + _nbytes((tile, Q_WIDTH + 2 * KV_WIDTH + W_CMP), BF16)
              + 2 * _nbytes((tiles, VT_SLAB, KT), BF16) + _nbytes((tile, SLOT + POOL_WIDTH), F32))
    res = pl.pallas_call(
        _in_proj_kernel,
        grid=(t // tile,),
        in_specs=[
            pl.BlockSpec((tile, D_MODEL), row),
            pl.BlockSpec((1, D_MODEL), fixed),
            pl.BlockSpec(w_in_t.shape, fixed, pipeline_mode=once),
            pl.BlockSpec((tile, N_ROPE_TAB * SLOT), row),
        ],
        out_specs=out_specs,
        out_shape=out_shape,
        scratch_shapes=[pltpu.VMEM((W_CMP // LANES, tile, LANES), F32)],
        compiler_params=pltpu.CompilerParams(
            dimension_semantics=("arbitrary",),
            vmem_limit_bytes=_vmem_limit(blocks, weights + _nbytes((tile, W_CMP), F32))),
        name="in_proj",
    )(x1, g_pre.reshape(1, -1), w_in_t, tab)
    return res


def _compress_kernel(x_ref, w1k_ref, w1v_ref, pek_ref, pev_ref, w1k_raw_ref, w1v_raw_ref,
                     w2k_ref, w2v_ref, tab_ref, kc_ref, vc_ref, acck_ref, accv_ref):
    l = pl.program_id(0)

    @pl.when(l == 0)
    def _():
        acck_ref[...] = jnp.zeros(acck_ref.shape, F32)
        accv_ref[...] = jnp.zeros(accv_ref.shape, F32)

    for acc_ref, w1_ref, c0 in ((acck_ref, w1k_ref, 0), (accv_ref, w1v_ref, KV_WIDTH)):
        for g in range(N_KV):
            acc_ref[g] += _dot(x_ref[:, c0 + g * HEAD_DIM:c0 + (g + 1) * HEAD_DIM], w1_ref[0])

    @pl.when(l == CHUNK_TOK - 1)
    def _():
        rows = acck_ref.shape[1]
        tab = tab_ref[...]
        for acc_ref, pe_ref, w1_raw_ref, w2_ref, o_ref, rope in (
                (acck_ref, pek_ref, w1k_raw_ref, w2k_ref, kc_ref, True),
                (accv_ref, pev_ref, w1v_raw_ref, w2v_ref, vc_ref, False)):
            bias = _dot(pe_ref[...], w1_raw_ref[...])[0:1, :]
            for g in range(N_KV):
                first = acc_ref[g, :, 0:CMP_HIDDEN]
                second = pltpu.roll(acc_ref[g, :, CMP_HIDDEN:2 * CMP_HIDDEN], rows - 1, axis=0)
                hid = jax.nn.gelu(first + second + bias).astype(BF16)
                out = _dot(hid, w2_ref[...])
                if rope:
                    out = _rope_slot(out, tab)
                o_ref[:, g * SLOT:(g + 1) * SLOT] = out.astype(BF16)


def _compress_weights(w1, w2):
    w1 = w1.reshape(2, CHUNK_TOK, HEAD_DIM, CMP_HIDDEN)
    w1_tok = jnp.concatenate([w1[0], w1[1]], axis=-1).astype(BF16)
    return w1_tok, jnp.pad(w2, ((0, 0), (0, SLOT - HEAD_DIM))).astype(BF16)


def _compress(cmp_rows, cmp_pe_k, cmp_w1_k, cmp_w2_k, cmp_pe_v, cmp_w1_v, cmp_w2_v, tab_cmp):
    rows = cmp_rows.shape[0]
    w1k, w2k = _compress_weights(cmp_w1_k, cmp_w2_k)
    w1v, w2v = _compress_weights(cmp_w1_v, cmp_w2_v)
    sub = 8
    pek = jnp.broadcast_to(cmp_pe_k.reshape(1, -1), (sub, CMP_LEN * HEAD_DIM)).astype(BF16)
    pev = jnp.broadcast_to(cmp_pe_v.reshape(1, -1), (sub, CMP_LEN * HEAD_DIM)).astype(BF16)
    fixed = lambda l: (0, 0)
    w_kv_pad = N_KV * SLOT
    blocks = (_nbytes((rows, W_CMP), BF16) + 2 * _nbytes((HEAD_DIM, 2 * CMP_HIDDEN), BF16)
              + 2 * _nbytes((sub + CMP_LEN * HEAD_DIM, CMP_HIDDEN), BF16)
              + 2 * _nbytes((CMP_HIDDEN, SLOT), BF16)
              + _nbytes((rows, N_ROPE_TAB * SLOT), F32) + 2 * _nbytes((rows, w_kv_pad), BF16))
    acc = pltpu.VMEM((N_KV, rows, 2 * CMP_HIDDEN), F32)
    return pl.pallas_call(
        _compress_kernel,
        grid=(CHUNK_TOK,),
        in_specs=[
            pl.BlockSpec((rows, W_CMP), lambda l: (0, l)),
            pl.BlockSpec((1, HEAD_DIM, 2 * CMP_HIDDEN), lambda l: (l, 0, 0)),
            pl.BlockSpec((1, HEAD_DIM, 2 * CMP_HIDDEN), lambda l: (l, 0, 0)),
            pl.BlockSpec((sub, CMP_LEN * HEAD_DIM), fixed),
            pl.BlockSpec((sub, CMP_LEN * HEAD_DIM), fixed),
            pl.BlockSpec((CMP_LEN * HEAD_DIM, CMP_HIDDEN), fixed),
            pl.BlockSpec((CMP_LEN * HEAD_DIM, CMP_HIDDEN), fixed),
            pl.BlockSpec((CMP_HIDDEN, SLOT), fixed),
            pl.BlockSpec((CMP_HIDDEN, SLOT), fixed),
            pl.BlockSpec((rows, N_ROPE_TAB * SLOT), fixed),
        ],
        out_specs=[pl.BlockSpec((rows, w_kv_pad), fixed)] * 2,
        out_shape=[jax.ShapeDtypeStruct((rows, w_kv_pad), BF16)] * 2,
        scratch_shapes=[acc, acc],
        compiler_params=pltpu.CompilerParams(
            dimension_semantics=("arbitrary",),
            vmem_limit_bytes=_vmem_limit(blocks, 2 * _nbytes(acc.shape, F32))),
        name="compress",
    )(cmp_rows, w1k, w1v, pek, pev, cmp_w1_k.astype(BF16), cmp_w1_v.astype(BF16), w2k, w2v, tab_cmp)


def _nsa_kernel(q_ref, gate_ref, kc_ref, vc_ref, ks_ref, vst_ref, kw_ref, vwt_ref, ov_ref,
                o_ref, ksp_ref, kwp_ref, qg_ref, mix_ref, *state_refs, n_slc):
    i = pl.program_id(1)
    start = i * TQ
    seq = ks_ref.shape[1]
    cols_g = GROUP * TQ
    chains_g = cols_g // CHAIN_COLS
    n_chains = N_KV * chains_g
    n_cmp = kc_ref.shape[1]
    sel_state = (state_refs[0:n_chains], state_refs[n_chains:2 * n_chains])
    win_state = (state_refs[2 * n_chains:3 * n_chains], state_refs[3 * n_chains:4 * n_chains])
    lane_q = lax.broadcasted_iota(jnp.int32, (TQ, SLOT), 1)
    lane_h = lax.broadcasted_iota(jnp.int32, (HALF, SLOT), 1)

    def head_slot(x_ref, rows, j):
        two = x_ref[rows, (j // 2) * SLOT:(j // 2 + 1) * SLOT].astype(F32)
        return pltpu.roll(two, HEAD_DIM, axis=1) if j % 2 else two

    @pl.when(i == 0)
    def _():
        for c in range(seq // KT):
            rows = pl.ds(c * KT, KT)
            lane = lax.broadcasted_iota(jnp.int32, (KT, SLOT), 1)
            key_blk = (c * KT + lax.broadcasted_iota(jnp.int32, (KT, SLOT), 0)) >> SEL_SHIFT
            onehot = (lane - BLK_LANE0 == key_blk).astype(F32)
            for g in range(N_KV):
                gs = slice(g * SLOT, (g + 1) * SLOT)
                ksp_ref[rows, gs] = jnp.where(lane < HEAD_DIM, head_slot(ks_ref.at[0], rows, g),
                                              onehot).astype(BF16)
                kwp_ref[rows, gs] = jnp.where(lane < HEAD_DIM, head_slot(kw_ref.at[0], rows, g),
                                              0.0).astype(BF16)

    def q_rows(r, x):
        return slice((x * GROUP + r) * HALF, (x * GROUP + r + 1) * HALF)

    def chain_half(c):
        return c * CHAIN_COLS // (GROUP * HALF)

    for h in range(N_HEADS):
        g, r = divmod(h, GROUP)
        q_pad = jnp.where(lane_q < HEAD_DIM, head_slot(q_ref, slice(None), h), 0.0).astype(BF16)
        for x in range(2):
            qg_ref[g, q_rows(r, x), :] = q_pad[x * HALF:(x + 1) * HALF]

    gates_t = jax.nn.sigmoid(gate_ref[...]).T
    any_cmp = start + lax.broadcasted_iota(jnp.int32, (1, TQ), 1) >= CMP_LEN - 1
    key_h = lax.broadcasted_iota(jnp.int32, (HALF, CHAIN_COLS), 0)
    qry_h = lax.broadcasted_iota(jnp.int32, (HALF, CHAIN_COLS), 1) & (HALF - 1)
    tri_le, tri_gt = key_h <= qry_h, key_h > qry_h
    diag_plan = ((slice(0, HALF), tri_le, slice(0, HALF)), (slice(0, KT), tri_le, slice(HALF, KT)))
    far_plan = ((slice(0, KT), tri_gt, slice(0, HALF)), (slice(HALF, KT), tri_gt, slice(0, HALF)))

    def gate_row(branch, h):
        c = branch * N_HEADS + h
        return gates_t[c:c + 1, :]

    sub = 8
    cmp_per_sub = sub * SEL_LEN // CMP_STRIDE

    def compressed_and_select(ng):
        n_blk, n_c = ng * sub, ng * cmp_per_sub
        t_cmp = start + lax.broadcasted_iota(jnp.int32, (n_c, TQ), 1)
        n_idx = lax.broadcasted_iota(jnp.int32, (n_c, TQ), 0)
        cmp_valid = n_idx * CMP_STRIDE + (CMP_LEN - 1) <= t_cmp
        t_row = start + lax.broadcasted_iota(jnp.int32, (n_blk, TQ), 1)
        blk = lax.broadcasted_iota(jnp.int32, (n_blk, TQ), 0)
        forced = (blk == t_row >> SEL_SHIFT) | (blk == 0)
        causal_blk = blk * SEL_LEN <= t_row
        idx8 = lax.broadcasted_iota(jnp.int32, (sub, TQ), 0)
        pad_c = jnp.zeros((n_cmp - n_c, TQ), F32)
        cmp_scores = []
        for h in range(N_HEADS):
            g, r = divmod(h, GROUP)
            q_h = jnp.concatenate([qg_ref[g, q_rows(r, 0), :], qg_ref[g, q_rows(r, 1), :]], axis=0)
            cmp_scores.append(_dot_nt(kc_ref[0, 0:n_c, g * SLOT:(g + 1) * SLOT], q_h))
        for g in range(N_KV):
            vct_g = vc_ref[0, :, g * SLOT:(g + 1) * SLOT].astype(F32).T[0:HEAD_DIM].astype(BF16)
            p_sum = jnp.zeros((n_c, TQ), F32)
            for r in range(GROUP):
                h = g * GROUP + r
                s = jnp.where(cmp_valid, cmp_scores[h], NEG_INF)
                p = jnp.exp2(s - jnp.max(s, axis=0, keepdims=True))
                inv = jnp.where(any_cmp, 1.0 / jnp.sum(p, axis=0, keepdims=True), 0.0)
                p = p * inv
                p_sum = p_sum + p
                p_all = jnp.concatenate([p, pad_c], axis=0) if ng * cmp_per_sub < n_cmp else p
                mix_ref[h] = gate_row(0, h) * _dot(vct_g, p_all.astype(BF16))
            p_sum_all = jnp.concatenate([p_sum, pad_c], axis=0) if ng * cmp_per_sub < n_cmp else p_sum
            imp = jnp.dot(ov_ref[0:n_blk, :], p_sum_all, precision=lax.Precision.HIGHEST,
                          preferred_element_type=F32)
            score = jnp.where(causal_blk, jnp.where(forced, FORCE_SCORE, imp), NEG_INF)
            rows8 = [score[k * sub:(k + 1) * sub] for k in range(ng)]
            ranks = [jnp.zeros((sub, TQ), jnp.int32) for _ in rows8]
            for c in range(n_blk):
                other = score[c:c + 1, :]
                for k, mine in enumerate(rows8):
                    if k * sub > c:
                        beats = other >= mine
                    elif (k + 1) * sub - 1 <= c:
                        beats = other > mine
                    else:
                        beats = (other > mine) | ((other == mine) & (idx8 > c - k * sub))
                    ranks[k] = ranks[k] + beats.astype(jnp.int32)
            rank = jnp.concatenate(ranks, axis=0) if ng > 1 else ranks[0]
            bias = jnp.where(causal_blk & (rank < N_SEL), 0.0, NEG_INF)
            bias_t = jnp.concatenate([jnp.zeros((BLK_LANE0, TQ), F32), bias,
                                      jnp.zeros((SLOT - BLK_LANE0 - n_blk, TQ), F32)], axis=0)
            bias_q = bias_t.T.astype(BF16)
            for r in range(GROUP):
                for x in range(2):
                    rs = q_rows(r, x)
                    qg_ref[g, rs, :] = jnp.where(lane_h < HEAD_DIM, qg_ref[g, rs, :],
                                                 bias_q[x * HALF:(x + 1) * HALF])

    ng_max = n_slc // sub
    ng_now = ((i + 1) * TQ + sub * SEL_LEN - 1) // (sub * SEL_LEN)
    for ng in range(1, ng_max + 1):
        pl.when(ng_now == ng)(functools.partial(compressed_and_select, ng))

    def flash_init(state):
        for m_ref, acc_ref in zip(*state):
            m_ref[...] = jnp.full(m_ref.shape, NEG_INF, F32)
            acc_ref[...] = jnp.zeros(acc_ref.shape, F32)

    def flash_tiles(tiles):
        jobs = [(tile, c) for tile in tiles for c in range(n_chains)]

        def key_plan(job):
            (_, _, _, plan, _), chain = job
            return (slice(0, KT), None, None) if plan is None else plan[chain_half(chain % chains_g)]

        scores = {}
        for step in range(len(jobs) + PIPE_DEPTH):
            if step < len(jobs):
                (kp_ref, _, kt, _, _), chain = jobs[step]
                g, c = divmod(chain, chains_g)
                keys, mask, rows = key_plan(jobs[step])
                off = pl.multiple_of(kt * KT + keys.start, HALF)
                s = _dot_nt(kp_ref[pl.ds(off, keys.stop - keys.start), g * SLOT:(g + 1) * SLOT],
                            qg_ref[g, c * CHAIN_COLS:(c + 1) * CHAIN_COLS, :])
                if mask is not None:
                    parts = [s[0:rows.start], jnp.where(mask, s[rows], NEG_INF), s[rows.stop:]]
                    s = jnp.concatenate([p for p in parts if p.shape[0]], axis=0)
                scores[step] = s
            done = step - PIPE_DEPTH
            if done >= 0:
                (_, vt_ref, kt, _, (m_refs, acc_refs)), chain = jobs[done]
                g = chain // chains_g
                keys = key_plan(jobs[done])[0]
                m_ref, acc_ref = m_refs[chain], acc_refs[chain]
                s = scores.pop(done)
                m_old = m_ref[...]
                m_new = jnp.maximum(m_old, jnp.max(s, axis=0, keepdims=True))
                p = jnp.exp2(s - m_new).astype(BF16)
                pv = _dot(vt_ref[kt, g * VT_ROWS:(g + 1) * VT_ROWS, keys], p)
                acc_ref[...] = jnp.exp2(m_old - m_new) * acc_ref[...] + pv
                m_ref[...] = m_new

    def flash_mix(branch, state):
        _, acc_refs = state
        for h in range(N_HEADS):
            g, r = divmod(h, GROUP)
            gate = gate_row(branch, h)
            for x in range(2):
                c, c0 = divmod(q_rows(r, x).start, CHAIN_COLS)
                acc = acc_refs[g * chains_g + c][:, c0:c0 + HALF]
                qs = slice(x * HALF, (x + 1) * HALF)
                scale = gate[:, qs] * (1.0 / acc[ONE_ROW:ONE_ROW + 1, :])
                mix_ref[h, :, qs] = mix_ref[h, :, qs] + scale * acc[0:HEAD_DIM]

    flash_init(sel_state)
    flash_init(win_state)

    def sel_past(kt):
        return (ksp_ref, vst_ref, kt, None, sel_state)

    def sel_pair(j, carry):
        flash_tiles([sel_past(2 * j), sel_past(2 * j + 1)])
        return carry

    lax.fori_loop(0, i >> 1, sel_pair, 0)

    @pl.when(i & 1 == 1)
    def _():
        flash_tiles([sel_past(i - 1)])

    n_back = WINDOW // KT
    sel_diag = (ksp_ref, vst_ref, i, diag_plan, sel_state)
    win_diag = (kwp_ref, vwt_ref, i, diag_plan, win_state)

    def win_back(d):
        return (kwp_ref, vwt_ref, i - d, far_plan if d == n_back else None, win_state)

    for have in range(n_back + 1):
        cond = (i == have) if have < n_back else (i >= have)

        @pl.when(cond)
        def _(have=have):
            flash_tiles([sel_diag] + [win_back(d) for d in range(have, 0, -1)] + [win_diag])

    flash_mix(1, sel_state)
    flash_mix(2, win_state)
    for h2 in range(N_HEADS // 2):
        pair = jnp.concatenate([mix_ref[2 * h2], mix_ref[2 * h2 + 1]], axis=0)
        o_ref[:, h2 * SLOT:(h2 + 1) * SLOT] = pair.T.astype(BF16)


def _overlap(n_cmp_pad, n_slc):
    c0 = jnp.arange(n_cmp_pad) * CMP_STRIDE
    s0 = jnp.arange(n_slc) * SEL_LEN
    ov = jnp.minimum(c0[None, :] + CMP_LEN, s0[:, None] + SEL_LEN) - jnp.maximum(c0[None, :], s0[:, None])
    return jnp.clip(ov, 0).astype(F32) / CMP_LEN


def _nsa(q, gate, kc, vc, ks, vst, kw, vwt, batch, seq_len):
    n_slc = seq_len // SEL_LEN
    n_cmp_pad = kc.shape[1]
    w_kv_pad = N_KV * SLOT
    assert n_slc <= SLOT - BLK_LANE0 and n_slc % 8 == 0
    assert CHAIN_COLS % HALF == 0 and (GROUP * HALF) % CHAIN_COLS == 0 and KT == 2 * HALF
    ov = _overlap(n_cmp_pad, n_slc)
    nq = seq_len // TQ
    n_kt = seq_len // KT
    qrow = lambda b, i: (b * nq + i, 0)
    per_b = lambda b, i: (b, 0, 0)
    fixed = lambda b, i: (0, 0)
    blocks = (_nbytes((TQ, Q_WIDTH), BF16) + _nbytes((TQ, SLOT), F32)
              + 2 * _nbytes((n_cmp_pad, w_kv_pad), BF16) + 2 * _nbytes((seq_len, KV_WIDTH), BF16)
              + 2 * _nbytes((n_kt, VT_SLAB, KT), BF16)
              + _nbytes(ov.shape, F32) + _nbytes((TQ, Q_WIDTH), BF16))
    n_chains = N_KV * GROUP * TQ // CHAIN_COLS
    run_max = [pltpu.VMEM((1, CHAIN_COLS), F32)] * n_chains
    run_acc = [pltpu.VMEM((VT_ROWS, CHAIN_COLS), F32)] * n_chains
    scratch_shapes = (
        [pltpu.VMEM((seq_len, w_kv_pad), BF16),
         pltpu.VMEM((seq_len, w_kv_pad), BF16),
         pltpu.VMEM((N_KV, GROUP * TQ, SLOT), BF16),
         pltpu.VMEM((N_HEADS, HEAD_DIM, TQ), F32)]
        + run_max + run_acc + run_max + run_acc)
    scratch = sum(_nbytes(s.shape, s.dtype) for s in scratch_shapes)
    return pl.pallas_call(
        functools.partial(_nsa_kernel, n_slc=n_slc),
        grid=(batch, nq),
        in_specs=[
            pl.BlockSpec((TQ, Q_WIDTH), qrow),
            pl.BlockSpec((TQ, SLOT), qrow),
            pl.BlockSpec((1, n_cmp_pad, w_kv_pad), per_b),
            pl.BlockSpec((1, n_cmp_pad, w_kv_pad), per_b),
            pl.BlockSpec((1, seq_len, KV_WIDTH), per_b),
            pl.BlockSpec((n_kt, VT_SLAB, KT), per_b),
            pl.BlockSpec((1, seq_len, KV_WIDTH), per_b),
            pl.BlockSpec((n_kt, VT_SLAB, KT), per_b),
            pl.BlockSpec(ov.shape, fixed),
        ],
        out_specs=pl.BlockSpec((TQ, Q_WIDTH), qrow),
        out_shape=jax.ShapeDtypeStruct((batch * seq_len, Q_WIDTH), BF16),
        scratch_shapes=scratch_shapes,
        compiler_params=pltpu.CompilerParams(
            dimension_semantics=("arbitrary", "arbitrary"),
            vmem_limit_bytes=_vmem_limit(blocks, scratch)),
        name="nsa",
    )(q, gate, kc.reshape(batch, n_cmp_pad, w_kv_pad), vc.reshape(batch, n_cmp_pad, w_kv_pad),
      ks.reshape(batch, seq_len, KV_WIDTH), vst, kw.reshape(batch, seq_len, KV_WIDTH), vwt, ov)


def _pool_kernel(u_ref, w_ref, scale_ref, o_ref):
    seq = u_ref.shape[0]
    t = lax.broadcasted_iota(jnp.int32, (seq, POOL_GROUP), 0)

    def shifted(x, k):
        return jnp.where(t >= k, pltpu.roll(x, k, axis=0), 0.0)

    for gi, w in enumerate(POOL_WINDOWS):
        sl = slice(gi * POOL_GROUP, (gi + 1) * POOL_GROUP)
        x = u_ref[:, sl]
        wsum = x
        span = 1
        while span < w:
            wsum = wsum + shifted(wsum, span)
            span *= 2
        cnt = jnp.minimum(t + 1, w).astype(F32)
        pooled = (wsum / cnt - x).astype(BF16)
        o_ref[:, sl] = (_dot(pooled, w_ref[gi]) * scale_ref[:, sl]).astype(BF16)


def _pool(pool_in, pool_w, pool_scale, batch, seq_len):
    for w in POOL_WINDOWS:
        assert w & (w - 1) == 0
    blocks = (_nbytes((seq_len, POOL_WIDTH), F32) + _nbytes(pool_w.shape, BF16)
              + _nbytes((seq_len, POOL_WIDTH), BF16))
    return pl.pallas_call(
        _pool_kernel,
        grid=(batch,),
        in_specs=[
            pl.BlockSpec((seq_len, POOL_WIDTH), lambda b: (b, 0)),
            pl.BlockSpec(pool_w.shape, lambda b: (0, 0, 0)),
            pl.BlockSpec((1, POOL_WIDTH), lambda b: (0, 0)),
        ],
        out_specs=pl.BlockSpec((seq_len, POOL_WIDTH), lambda b: (b, 0)),
        out_shape=jax.ShapeDtypeStruct((batch * seq_len, POOL_WIDTH), BF16),
        compiler_params=pltpu.CompilerParams(
            dimension_semantics=("arbitrary",), vmem_limit_bytes=_vmem_limit(blocks)),
        name="pool",
    )(pool_in, pool_w.astype(BF16), pool_scale.reshape(1, -1))


def _merge_kernel(x_ref, gpre_ref, wt_ref, nsa_ref, wa_ref, pool_ref, wp_ref, wo_ref, gpost_ref,
                  o_ref):
    x = x_ref[...]
    hn = _rms(x, gpre_ref[...])
    g_attn = jax.nn.sigmoid(_dot_nt(hn, wt_ref[OFF_MERGE:OFF_MERGE + D_MODEL, :]))
    g_pool = jax.nn.sigmoid(_dot_nt(hn, wt_ref[OFF_MERGE + D_MODEL:OFF_MERGE + 2 * D_MODEL, :]))
    y = (g_attn * _dot(nsa_ref[...].astype(F32), wa_ref[...])
         + g_pool * _dot(pool_ref[...].astype(F32), wp_ref[...]))
    h = _dot(y, wo_ref[...])
    o_ref[...] = x + _rms(h, gpost_ref[...])


def _merge(x1, g_pre, w_in_t, o_nsa, w_attn, o_pool, w_pool, w_out, g_post):
    t = x1.shape[0]
    row = lambda i: (i, 0)
    fixed = lambda i: (0, 0)
    once = pl.Buffered(1)
    weights = (_nbytes(w_in_t.shape, F32) + _nbytes((Q_WIDTH, D_MODEL), F32)
               + _nbytes((POOL_WIDTH, D_MODEL), F32) + _nbytes((D_MODEL, D_MODEL), F32))
    blocks = (2 * _nbytes((TOK_TILE, D_MODEL), F32) + _nbytes((TOK_TILE, Q_WIDTH), BF16)
              + _nbytes((TOK_TILE, POOL_WIDTH), BF16))
    return pl.pallas_call(
        _merge_kernel,
        grid=(t // TOK_TILE,),
        in_specs=[
            pl.BlockSpec((TOK_TILE, D_MODEL), row),
            pl.BlockSpec((1, D_MODEL), fixed),
            pl.BlockSpec(w_in_t.shape, fixed, pipeline_mode=once),
            pl.BlockSpec((TOK_TILE, Q_WIDTH), row),
            pl.BlockSpec((Q_WIDTH, D_MODEL), fixed, pipeline_mode=once),
            pl.BlockSpec((TOK_TILE, POOL_WIDTH), row),
            pl.BlockSpec((POOL_WIDTH, D_MODEL), fixed, pipeline_mode=once),
            pl.BlockSpec((D_MODEL, D_MODEL), fixed, pipeline_mode=once),
            pl.BlockSpec((1, D_MODEL), fixed),
        ],
        out_specs=pl.BlockSpec((TOK_TILE, D_MODEL), row),
        out_shape=jax.ShapeDtypeStruct((t, D_MODEL), F32),
        compiler_params=pltpu.CompilerParams(
            dimension_semantics=("arbitrary",), vmem_limit_bytes=_vmem_limit(blocks, weights)),
        name="merge",
    )(x1, g_pre.reshape(1, -1), w_in_t, o_nsa, w_attn, o_pool, w_pool, w_out, g_post.reshape(1, -1))


def _layer(x, tab, tab_cmp, batch, seq_len, g_ffn1_pre, w_ffn1_gate, w_ffn1_up, w_ffn1_down,
           g_ffn1_post, g_mix_pre, w_in, cmp_pe_k, cmp_w1_k, cmp_w2_k, cmp_pe_v, cmp_w1_v, cmp_w2_v,
           w_attn_branch, pool_w, pool_scale, w_pool_branch, w_out, g_mix_post, g_ffn2_pre,
           w_ffn2_gate, w_ffn2_up, w_ffn2_down, g_ffn2_post):
    x1 = _ffn(x, g_ffn1_pre, w_ffn1_gate, w_ffn1_up, w_ffn1_down, g_ffn1_post)
    w_in_t = w_in.T
    q, ks, kw, vst, vwt, cmp_rows, gate, pool_in = _in_proj(x1, g_mix_pre, w_in_t, tab)
    kc, vc = _compress(cmp_rows, cmp_pe_k, cmp_w1_k, cmp_w2_k, cmp_pe_v, cmp_w1_v, cmp_w2_v, tab_cmp)
    n_cmp_pad = seq_len // CMP_STRIDE
    kc = kc.reshape(batch, n_cmp_pad, N_KV * SLOT)
    vc = vc.reshape(batch, n_cmp_pad, N_KV * SLOT)
    o_nsa = _nsa(q, gate, kc, vc, ks, vst, kw, vwt, batch, seq_len)
    o_pool = _pool(pool_in, pool_w, pool_scale, batch, seq_len)
    x2 = _merge(x1, g_mix_pre, w_in_t, o_nsa, w_attn_branch, o_pool, w_pool_branch, w_out,
                g_mix_post)
    return _ffn(x2, g_ffn2_pre, w_ffn2_gate, w_ffn2_up, w_ffn2_down, g_ffn2_post)


def kernel(x, positions, g_ffn1_pre, w_ffn1_gate, w_ffn1_up, w_ffn1_down, g_ffn1_post, g_mix_pre, w_in, cmp_pe_k, cmp_w1_k, cmp_w2_k, cmp_pe_v, cmp_w1_v, cmp_w2_v, w_attn_branch, pool_w, pool_scale, w_pool_branch, w_out, g_mix_post, g_ffn2_pre, w_ffn2_gate, w_ffn2_up, w_ffn2_down, g_ffn2_post):
    batch, seq_len, d_model = x.shape
    assert d_model == D_MODEL and seq_len % TOK_TILE == 0 and seq_len % TQ == 0
    assert TQ == KT and TQ % SEL_LEN == 0 and TOK_TILE % KT == 0 and WINDOW % KT == 0
    assert seq_len % CMP_STRIDE == 0 and CMP_LEN == 2 * CMP_STRIDE
    t = batch * seq_len
    tab = _rope_tables(positions.reshape(t), seq_len)
    n_cmp = (seq_len - CMP_LEN) // CMP_STRIDE + 1
    n_cmp_pad = seq_len // CMP_STRIDE
    pos_cmp = jnp.pad(positions[:, CMP_LEN - 1::CMP_STRIDE], ((0, 0), (0, n_cmp_pad - n_cmp)))
    tab_cmp = _rope_tables(pos_cmp.reshape(batch * n_cmp_pad), batch * n_cmp_pad)
    xf = x.reshape(t, D_MODEL)
    per_layer = (g_ffn1_pre, w_ffn1_gate, w_ffn1_up, w_ffn1_down, g_ffn1_post, g_mix_pre, w_in,
                 cmp_pe_k, cmp_w1_k, cmp_w2_k, cmp_pe_v, cmp_w1_v, cmp_w2_v, w_attn_branch, pool_w,
                 pool_scale, w_pool_branch, w_out, g_mix_post, g_ffn2_pre, w_ffn2_gate, w_ffn2_up,
                 w_ffn2_down, g_ffn2_post)
    for l in range(g_ffn1_pre.shape[0]):
        xf = _layer(xf, tab, tab_cmp, batch, seq_len, *(p[l] for p in per_layer))
    return xf.reshape(batch, seq_len, D_MODEL)
```

```python
import functools
import math

import jax
import jax.numpy as jnp
from jax import lax
from jax.experimental import pallas as pl
from jax.experimental.pallas import tpu as pltpu

F32 = jnp.float32
BF16 = jnp.bfloat16

D_MODEL = 1024
N_HEADS = 16
HEAD_DIM = 64
N_KV = 4
GROUP = N_HEADS // N_KV
ROT_DIM = HEAD_DIM // 4
ROT_HALF = ROT_DIM // 2
ROPE_THETA = 500000.0
CMP_LEN = 32
CMP_STRIDE = 16
CMP_HIDDEN = 2 * HEAD_DIM
SEL_LEN = 64
SEL_SHIFT = SEL_LEN.bit_length() - 1
N_SEL = 8
WINDOW = 512
POOL_WINDOWS = (2, 4, 8, 16)
POOL_WIDTH = D_MODEL // 2
POOL_GROUP = POOL_WIDTH // len(POOL_WINDOWS)
D_FF = 2816
EPS = 1e-6
NEG_INF = -1e30
FORCE_SCORE = 1e4
Q_WIDTH = N_HEADS * HEAD_DIM
KV_WIDTH = N_KV * HEAD_DIM
N_GATES = 3 * N_HEADS
LOG2_E = math.log2(math.e)

LANES = 128
BF16_SUBLANES = 16
V7X_VMEM_BYTES = 64 * 1024 * 1024

SLOT = LANES
BLK_LANE0 = HEAD_DIM
VT_ROWS = HEAD_DIM + BF16_SUBLANES
ONE_ROW = HEAD_DIM
TOK_TILE = 512
TQ = 256
KT = 256
HALF = TQ // 2
CHAIN_COLS = 256
PIPE_DEPTH = 7
FF_CHUNK = 256
IN_SPLIT = 2
N_ROPE_TAB = 2


def _vmem_limit(block_bytes, scratch_bytes=0):
    need = 2 * block_bytes + scratch_bytes
    return int(min(V7X_VMEM_BYTES - (4 << 20), max(2 * need, 32 << 20)))


def _nbytes(shape, dtype):
    n = 1
    for s in shape:
        n *= s
    return n * jnp.dtype(dtype).itemsize


def _rms(xf, g):
    return xf * lax.rsqrt(jnp.mean(xf * xf, axis=-1, keepdims=True) + EPS) * g


def _dot(a, b):
    return jnp.dot(a, b, preferred_element_type=F32)


def _dot_nt(a, b):
    return lax.dot_general(a, b, (((1,), (1,)), ((), ())), preferred_element_type=F32)


def _rope_trig_kernel(pos_ref, inv_ref, tab_ref):
    tn = pos_ref.shape[1]
    ang = pos_ref[...].astype(F32) * inv_ref[...]
    c, s = jnp.cos(ang), jnp.sin(ang)
    rest = HEAD_DIM - ROT_DIM
    reps = SLOT // HEAD_DIM
    cos_rows = jnp.concatenate([c, c, jnp.ones((rest, tn), F32)] * reps, axis=0)
    sin_rows = jnp.concatenate([-s, s, jnp.zeros((rest, tn), F32)] * reps, axis=0)
    for j in range(tn // LANES):
        cs = slice(j * LANES, (j + 1) * LANES)
        tab_ref[cs, 0:SLOT] = cos_rows[:, cs].T
        tab_ref[cs, SLOT:2 * SLOT] = sin_rows[:, cs].T


def _rope_tables(pos_flat, tile):
    n = pos_flat.shape[0]
    inv = ROPE_THETA ** (-jnp.arange(ROT_HALF, dtype=F32) * (2.0 / ROT_DIM))
    return pl.pallas_call(
        _rope_trig_kernel,
        grid=(n // tile,),
        in_specs=[pl.BlockSpec((1, tile), lambda i: (0, i)),
                  pl.BlockSpec((ROT_HALF, 1), lambda i: (0, 0))],
        out_specs=pl.BlockSpec((tile, N_ROPE_TAB * SLOT), lambda i: (i, 0)),
        out_shape=jax.ShapeDtypeStruct((n, N_ROPE_TAB * SLOT), F32),
        compiler_params=pltpu.CompilerParams(dimension_semantics=("arbitrary",)),
        name="rope_trig",
    )(pos_flat.reshape(1, n), inv.reshape(ROT_HALF, 1))


def _rope_slot(y, tab):
    cos_t = tab[:, 0:SLOT]
    sin_t = tab[:, SLOT:2 * SLOT]
    lane = lax.broadcasted_iota(jnp.int32, y.shape, 1) & (HEAD_DIM - 1)
    up = pltpu.roll(y, SLOT - ROT_HALF, axis=1)
    down = pltpu.roll(y, ROT_HALF, axis=1)
    return y * cos_t + jnp.where(lane < ROT_HALF, up, down) * sin_t


def _ffn_kernel(x_ref, gpre_ref, wg_ref, wu_ref, wd_ref, gpost_ref, o_ref, acc_ref):
    x = x_ref[...]
    hn = _rms(x, gpre_ref[...])
    for c in range(D_FF // FF_CHUNK):
        sl = slice(c * FF_CHUNK, (c + 1) * FF_CHUNK)
        g = _dot(hn, wg_ref[:, sl])
        u = _dot(hn, wu_ref[:, sl])
        d = _dot(g * jax.nn.sigmoid(g) * u, wd_ref[sl, :])
        if c == 0:
            acc_ref[...] = d
        else:
            acc_ref[...] += d
    o_ref[...] = x + 0.5 * _rms(acc_ref[...], gpost_ref[...])


def _ffn(x, g_pre, w_gate, w_up, w_down, g_post):
    t = x.shape[0]
    tile = TOK_TILE
    row = lambda i: (i, 0)
    fixed = lambda i: (0, 0)
    once = pl.Buffered(1)
    weights = 3 * _nbytes((D_MODEL, D_FF), F32)
    blocks = 2 * _nbytes((tile, D_MODEL), F32)
    return pl.pallas_call(
        _ffn_kernel,
        grid=(t // tile,),
        in_specs=[
            pl.BlockSpec((tile, D_MODEL), row),
            pl.BlockSpec((1, D_MODEL), fixed),
            pl.BlockSpec((D_MODEL, D_FF), fixed, pipeline_mode=once),
            pl.BlockSpec((D_MODEL, D_FF), fixed, pipeline_mode=once),
            pl.BlockSpec((D_FF, D_MODEL), fixed, pipeline_mode=once),
            pl.BlockSpec((1, D_MODEL), fixed),
        ],
        out_specs=pl.BlockSpec((tile, D_MODEL), row),
        out_shape=jax.ShapeDtypeStruct((t, D_MODEL), F32),
        scratch_shapes=[pltpu.VMEM((tile, D_MODEL), F32)],
        compiler_params=pltpu.CompilerParams(
            dimension_semantics=("arbitrary",),
            vmem_limit_bytes=_vmem_limit(blocks, weights + _nbytes((tile, D_MODEL), F32))),
        name="ffn",
    )(x, g_pre.reshape(1, -1), w_gate, w_up, w_down, g_post.reshape(1, -1))


OFF_Q = 0
OFF_CMP = OFF_Q + Q_WIDTH
OFF_KS = OFF_CMP + 2 * KV_WIDTH
OFF_VS = OFF_KS + KV_WIDTH
OFF_KW = OFF_VS + KV_WIDTH
OFF_VW = OFF_KW + KV_WIDTH
OFF_GATE = OFF_VW + KV_WIDTH
OFF_POOL = OFF_GATE + N_GATES
OFF_MERGE = OFF_POOL + POOL_WIDTH
W_CMP = 2 * KV_WIDTH
CHUNK_TOK = CMP_STRIDE
VT_SLAB = N_KV * VT_ROWS


def _in_proj_kernel(x_ref, g_ref, w_ref, tab_ref, q_ref, ks_ref, kw_ref, vst_ref,
                    vwt_ref, cmp_ref, gate_ref, pool_ref, cmp_scr):
    rows = x_ref.shape[0] // IN_SPLIT
    wide = 4 * SLOT
    aux = (lax.broadcasted_iota(jnp.int32, (VT_ROWS - HEAD_DIM, KT), 0) == 0).astype(BF16)

    def part(k):
        rs = slice(k * rows, (k + 1) * rows)
        hn = _rms(x_ref[rs, :], g_ref[...])
        tab = tab_ref[rs, :]
        tab_q = tab * (HEAD_DIM ** -0.5 * LOG2_E)
        for c in range(Q_WIDTH // wide):
            y = _dot_nt(hn, w_ref[OFF_Q + c * wide:OFF_Q + (c + 1) * wide, :])
            for j in range(wide // SLOT):
                sl = slice(c * wide + j * SLOT, c * wide + (j + 1) * SLOT)
                q_ref[rs, sl] = _rope_slot(y[:, j * SLOT:(j + 1) * SLOT], tab_q).astype(BF16)
        for k_ref, off in ((ks_ref, OFF_KS), (kw_ref, OFF_KW)):
            y = _dot_nt(hn, w_ref[off:off + KV_WIDTH, :])
            for j in range(KV_WIDTH // SLOT):
                sl = slice(j * SLOT, (j + 1) * SLOT)
                k_ref[rs, sl] = _rope_slot(y[:, sl], tab).astype(BF16)
        y = _dot_nt(hn, w_ref[OFF_CMP:OFF_CMP + W_CMP, :])
        for j in range(W_CMP // LANES):
            cmp_scr[j, rs, :] = y[:, j * LANES:(j + 1) * LANES]
        chunk_rows = rows // CHUNK_TOK
        for l in range(CHUNK_TOK):
            for j in range(W_CMP // LANES):
                c0 = l * W_CMP + j * LANES
                cmp_ref[k * chunk_rows:(k + 1) * chunk_rows, c0:c0 + LANES] = (
                    cmp_scr[j, pl.ds(k * rows + l, chunk_rows, stride=CHUNK_TOK), :].astype(BF16))
        gate_ref[rs, :] = _dot_nt(hn, w_ref[OFF_GATE:OFF_GATE + SLOT, :])
        pool_ref[rs, :] = _dot_nt(hn, w_ref[OFF_POOL:OFF_MERGE, :])
        tiles = rows // KT
        for vt_ref, off in ((vst_ref, OFF_VS), (vwt_ref, OFF_VW)):
            vt = _dot_nt(w_ref[off:off + KV_WIDTH, :], hn).astype(BF16)
            for j in range(tiles):
                for g in range(N_KV):
                    vt_ref[k * tiles + j, g * VT_ROWS:g * VT_ROWS + HEAD_DIM, :] = (
                        vt[g * HEAD_DIM:(g + 1) * HEAD_DIM, j * KT:(j + 1) * KT])
                    vt_ref[k * tiles + j, g * VT_ROWS + HEAD_DIM:(g + 1) * VT_ROWS, :] = aux

    for k in range(IN_SPLIT):
        part(k)


def _in_proj(x1, g_pre, w_in_t, tab):
    t = x1.shape[0]
    row = lambda i: (i, 0)
    fixed = lambda i: (0, 0)
    once = pl.Buffered(1)
    tile = IN_SPLIT * TOK_TILE
    tiles = tile // KT
    chunk_rows = tile // CHUNK_TOK
    slab = pl.BlockSpec((tiles, VT_SLAB, KT), lambda i: (i, 0, 0))
    slab_shape = jax.ShapeDtypeStruct((t // KT, VT_SLAB, KT), BF16)
    out_specs = [
        pl.BlockSpec((tile, Q_WIDTH), row), pl.BlockSpec((tile, KV_WIDTH), row),
        pl.BlockSpec((tile, KV_WIDTH), row), slab, slab,
        pl.BlockSpec((chunk_rows, CHUNK_TOK * W_CMP), row),
        pl.BlockSpec((tile, SLOT), row), pl.BlockSpec((tile, POOL_WIDTH), row)]
    out_shape = [
        jax.ShapeDtypeStruct((t, Q_WIDTH), BF16), jax.ShapeDtypeStruct((t, KV_WIDTH), BF16),
        jax.ShapeDtypeStruct((t, KV_WIDTH), BF16), slab_shape, slab_shape,
        jax.ShapeDtypeStruct((t // CHUNK_TOK, CHUNK_TOK * W_CMP), BF16),
        jax.ShapeDtypeStruct((t, SLOT), F32), jax.ShapeDtypeStruct((t, POOL_WIDTH), F32)]
    weights = _nbytes(w_in_t.shape, F32)
    blocks = (_nbytes((tile, D_MODEL), F32) + _nbytes((tile, N_ROPE_TAB * SLOT), F32)
              + _nbytes((tile, Q_WIDTH + 2 * KV_WIDTH + W_CMP), BF16)
              + 2 * _nbytes((tiles, VT_SLAB, KT), BF16) + _nbytes((tile, SLOT + POOL_WIDTH), F32))
    res = pl.pallas_call(
        _in_proj_kernel,
        grid=(t // tile,),
        in_specs=[
            pl.BlockSpec((tile, D_MODEL), row),
            pl.BlockSpec((1, D_MODEL), fixed),
            pl.BlockSpec(w_in_t.shape, fixed, pipeline_mode=once),
            pl.BlockSpec((tile, N_ROPE_TAB * SLOT), row),
        ],
        out_specs=out_specs,
        out_shape=out_shape,
        scratch_shapes=[pltpu.VMEM((W_CMP // LANES, tile, LANES), F32)],
        compiler_params=pltpu.CompilerParams(
            dimension_semantics=("arbitrary",),
            vmem_limit_bytes=_vmem_limit(blocks, weights + _nbytes((tile, W_CMP), F32))),
        name="in_proj",
    )(x1, g_pre.reshape(1, -1), w_in_t, tab)
    return res


def _compress_kernel(x_ref, w1k_ref, w1v_ref, pek_ref, pev_ref, w1k_raw_ref, w1v_raw_ref,
                     w2k_ref, w2v_ref, tab_ref, kc_ref, vc_ref, acck_ref, accv_ref):
    l = pl.program_id(0)

    @pl.when(l == 0)
    def _():
        acck_ref[...] = jnp.zeros(acck_ref.shape, F32)
        accv_ref[...] = jnp.zeros(accv_ref.shape, F32)

    for acc_ref, w1_ref, c0 in ((acck_ref, w1k_ref, 0), (accv_ref, w1v_ref, KV_WIDTH)):
        for g in range(N_KV):
            acc_ref[g] += _dot(x_ref[:, c0 + g * HEAD_DIM:c0 + (g + 1) * HEAD_DIM], w1_ref[0])

    @pl.when(l == CHUNK_TOK - 1)
    def _():
        rows = acck_ref.shape[1]
        tab = tab_ref[...]
        for acc_ref, pe_ref, w1_raw_ref, w2_ref, o_ref, rope in (
                (acck_ref, pek_ref, w1k_raw_ref, w2k_ref, kc_ref, True),
                (accv_ref, pev_ref, w1v_raw_ref, w2v_ref, vc_ref, False)):
            bias = _dot(pe_ref[...], w1_raw_ref[...])[0:1, :]
            for g in range(N_KV):
                first = acc_ref[g, :, 0:CMP_HIDDEN]
                second = pltpu.roll(acc_ref[g, :, CMP_HIDDEN:2 * CMP_HIDDEN], rows - 1, axis=0)
                hid = jax.nn.gelu(first + second + bias).astype(BF16)
                out = _dot(hid, w2_ref[...])
                if rope:
                    out = _rope_slot(out, tab)
                o_ref[:, g * SLOT:(g + 1) * SLOT] = out.astype(BF16)


def _compress_weights(w1, w2):
    w1 = w1.reshape(2, CHUNK_TOK, HEAD_DIM, CMP_HIDDEN)
    w1_tok = jnp.concatenate([w1[0], w1[1]], axis=-1).astype(BF16)
    return w1_tok, jnp.pad(w2, ((0, 0), (0, SLOT - HEAD_DIM))).astype(BF16)


def _compress(cmp_rows, cmp_pe_k, cmp_w1_k, cmp_w2_k, cmp_pe_v, cmp_w1_v, cmp_w2_v, tab_cmp):
    rows = cmp_rows.shape[0]
    w1k, w2k = _compress_weights(cmp_w1_k, cmp_w2_k)
    w1v, w2v = _compress_weights(cmp_w1_v, cmp_w2_v)
    sub = 8
    pek = jnp.broadcast_to(cmp_pe_k.reshape(1, -1), (sub, CMP_LEN * HEAD_DIM)).astype(BF16)
    pev = jnp.broadcast_to(cmp_pe_v.reshape(1, -1), (sub, CMP_LEN * HEAD_DIM)).astype(BF16)
    fixed = lambda l: (0, 0)
    w_kv_pad = N_KV * SLOT
    blocks = (_nbytes((rows, W_CMP), BF16) + 2 * _nbytes((HEAD_DIM, 2 * CMP_HIDDEN), BF16)
              + 2 * _nbytes((sub + CMP_LEN * HEAD_DIM, CMP_HIDDEN), BF16)
              + 2 * _nbytes((CMP_HIDDEN, SLOT), BF16)
              + _nbytes((rows, N_ROPE_TAB * SLOT), F32) + 2 * _nbytes((rows, w_kv_pad), BF16))
    acc = pltpu.VMEM((N_KV, rows, 2 * CMP_HIDDEN), F32)
    return pl.pallas_call(
        _compress_kernel,
        grid=(CHUNK_TOK,),
        in_specs=[
            pl.BlockSpec((rows, W_CMP), lambda l: (0, l)),
            pl.BlockSpec((1, HEAD_DIM, 2 * CMP_HIDDEN), lambda l: (l, 0, 0)),
            pl.BlockSpec((1, HEAD_DIM, 2 * CMP_HIDDEN), lambda l: (l, 0, 0)),
            pl.BlockSpec((sub, CMP_LEN * HEAD_DIM), fixed),
            pl.BlockSpec((sub, CMP_LEN * HEAD_DIM), fixed),
            pl.BlockSpec((CMP_LEN * HEAD_DIM, CMP_HIDDEN), fixed),
            pl.BlockSpec((CMP_LEN * HEAD_DIM, CMP_HIDDEN), fixed),
            pl.BlockSpec((CMP_HIDDEN, SLOT), fixed),
            pl.BlockSpec((CMP_HIDDEN, SLOT), fixed),
            pl.BlockSpec((rows, N_ROPE_TAB * SLOT), fixed),
        ],
        out_specs=[pl.BlockSpec((rows, w_kv_pad), fixed)] * 2,
        out_shape=[jax.ShapeDtypeStruct((rows, w_kv_pad), BF16)] * 2,
        scratch_shapes=[acc, acc],
        compiler_params=pltpu.CompilerParams(
            dimension_semantics=("arbitrary",),
            vmem_limit_bytes=_vmem_limit(blocks, 2 * _nbytes(acc.shape, F32))),
        name="compress",
    )(cmp_rows, w1k, w1v, pek, pev, cmp_w1_k.astype(BF16), cmp_w1_v.astype(BF16), w2k, w2v, tab_cmp)


def _nsa_kernel(q_ref, gate_ref, kc_ref, vc_ref, ks_ref, vst_ref, kw_ref, vwt_ref, ov_ref,
                o_ref, ksp_ref, kwp_ref, qg_ref, mix_ref, *state_refs, n_slc):
    i = pl.program_id(1)
    start = i * TQ
    seq = ks_ref.shape[1]
    cols_g = GROUP * TQ
    chains_g = cols_g // CHAIN_COLS
    n_chains = N_KV * chains_g
    n_cmp = kc_ref.shape[1]
    sel_state = (state_refs[0:n_chains], state_refs[n_chains:2 * n_chains])
    win_state = (state_refs[2 * n_chains:3 * n_chains], state_refs[3 * n_chains:4 * n_chains])
    lane_q = lax.broadcasted_iota(jnp.int32, (TQ, SLOT), 1)
    lane_h = lax.broadcasted_iota(jnp.int32, (HALF, SLOT), 1)

    def head_slot(x_ref, rows, j):
        two = x_ref[rows, (j // 2) * SLOT:(j // 2 + 1) * SLOT].astype(F32)
        return pltpu.roll(two, HEAD_DIM, axis=1) if j % 2 else two

    @pl.when(i == 0)
    def _():
        for c in range(seq // KT):
            rows = pl.ds(c * KT, KT)
            lane = lax.broadcasted_iota(jnp.int32, (KT, SLOT), 1)
            key_blk = (c * KT + lax.broadcasted_iota(jnp.int32, (KT, SLOT), 0)) >> SEL_SHIFT
            onehot = (lane - BLK_LANE0 == key_blk).astype(F32)
            for g in range(N_KV):
                gs = slice(g * SLOT, (g + 1) * SLOT)
                ksp_ref[rows, gs] = jnp.where(lane < HEAD_DIM, head_slot(ks_ref.at[0], rows, g),
                                              onehot).astype(BF16)
                kwp_ref[rows, gs] = jnp.where(lane < HEAD_DIM, head_slot(kw_ref.at[0], rows, g),
                                              0.0).astype(BF16)

    def q_rows(r, x):
        return slice((x * GROUP + r) * HALF, (x * GROUP + r + 1) * HALF)

    def chain_half(c):
        return c * CHAIN_COLS // (GROUP * HALF)

    for h in range(N_HEADS):
        g, r = divmod(h, GROUP)
        q_pad = jnp.where(lane_q < HEAD_DIM, head_slot(q_ref, slice(None), h), 0.0).astype(BF16)
        for x in range(2):
            qg_ref[g, q_rows(r, x), :] = q_pad[x * HALF:(x + 1) * HALF]

    gates_t = jax.nn.sigmoid(gate_ref[...]).T
    any_cmp = start + lax.broadcasted_iota(jnp.int32, (1, TQ), 1) >= CMP_LEN - 1
    key_h = lax.broadcasted_iota(jnp.int32, (HALF, CHAIN_COLS), 0)
    qry_h = lax.broadcasted_iota(jnp.int32, (HALF, CHAIN_COLS), 1) & (HALF - 1)
    tri_le, tri_gt = key_h <= qry_h, key_h > qry_h
    diag_plan = ((slice(0, HALF), tri_le, slice(0, HALF)), (slice(0, KT), tri_le, slice(HALF, KT)))
    far_plan = ((slice(0, KT), tri_gt, slice(0, HALF)), (slice(HALF, KT), tri_gt, slice(0, HALF)))

    def gate_row(branch, h):
        c = branch * N_HEADS + h
        return gates_t[c:c + 1, :]

    sub = 8
    cmp_per_sub = sub * SEL_LEN // CMP_STRIDE

    def compressed_and_select(ng):
        n_blk, n_c = ng * sub, ng * cmp_per_sub
        t_cmp = start + lax.broadcasted_iota(jnp.int32, (n_c, TQ), 1)
        n_idx = lax.broadcasted_iota(jnp.int32, (n_c, TQ), 0)
        cmp_valid = n_idx * CMP_STRIDE + (CMP_LEN - 1) <= t_cmp
        t_row = start + lax.broadcasted_iota(jnp.int32, (n_blk, TQ), 1)
        blk = lax.broadcasted_iota(jnp.int32, (n_blk, TQ), 0)
        forced = (blk == t_row >> SEL_SHIFT) | (blk == 0)
        causal_blk = blk * SEL_LEN <= t_row
        idx8 = lax.broadcasted_iota(jnp.int32, (sub, TQ), 0)
        pad_c = jnp.zeros((n_cmp - n_c, TQ), F32)
        cmp_scores = []
        for h in range(N_HEADS):
            g, r = divmod(h, GROUP)
            q_h = jnp.concatenate([qg_ref[g, q_rows(r, 0), :], qg_ref[g, q_rows(r, 1), :]], axis=0)
            cmp_scores.append(_dot_nt(kc_ref[0, 0:n_c, g * SLOT:(g + 1) * SLOT], q_h))
        for g in range(N_KV):
            vct_g = vc_ref[0, :, g * SLOT:(g + 1) * SLOT].astype(F32).T[0:HEAD_DIM].astype(BF16)
            p_sum = jnp.zeros((n_c, TQ), F32)
            for r in range(GROUP):
                h = g * GROUP + r
                s = jnp.where(cmp_valid, cmp_scores[h], NEG_INF)
                p = jnp.exp2(s - jnp.max(s, axis=0, keepdims=True))
                inv = jnp.where(any_cmp, 1.0 / jnp.sum(p, axis=0, keepdims=True), 0.0)
                p = p * inv
                p_sum = p_sum + p
                p_all = jnp.concatenate([p, pad_c], axis=0) if ng * cmp_per_sub < n_cmp else p
                mix_ref[h] = gate_row(0, h) * _dot(vct_g, p_all.astype(BF16))
            p_sum_all = jnp.concatenate([p_sum, pad_c], axis=0) if ng * cmp_per_sub < n_cmp else p_sum
            imp = jnp.dot(ov_ref[0:n_blk, :], p_sum_all, precision=lax.Precision.HIGHEST,
                          preferred_element_type=F32)
            score = jnp.where(causal_blk, jnp.where(forced, FORCE_SCORE, imp), NEG_INF)
            rows8 = [score[k * sub:(k + 1) * sub] for k in range(ng)]
            ranks = [jnp.zeros((sub, TQ), jnp.int32) for _ in rows8]
            for c in range(n_blk):
                other = score[c:c + 1, :]
                for k, mine in enumerate(rows8):
                    if k * sub > c:
                        beats = other >= mine
                    elif (k + 1) * sub - 1 <= c:
                        beats = other > mine
                    else:
                        beats = (other > mine) | ((other == mine) & (idx8 > c - k * sub))
                    ranks[k] = ranks[k] + beats.astype(jnp.int32)
            rank = jnp.concatenate(ranks, axis=0) if ng > 1 else ranks[0]
            bias = jnp.where(causal_blk & (rank < N_SEL), 0.0, NEG_INF)
            bias_t = jnp.concatenate([jnp.zeros((BLK_LANE0, TQ), F32), bias,
                                      jnp.zeros((SLOT - BLK_LANE0 - n_blk, TQ), F32)], axis=0)
            bias_q = bias_t.T.astype(BF16)
            for r in range(GROUP):
                for x in range(2):
                    rs = q_rows(r, x)
                    qg_ref[g, rs, :] = jnp.where(lane_h < HEAD_DIM, qg_ref[g, rs, :],
                                                 bias_q[x * HALF:(x + 1) * HALF])

    ng_max = n_slc // sub
    ng_now = ((i + 1) * TQ + sub * SEL_LEN - 1) // (sub * SEL_LEN)
    for ng in range(1, ng_max + 1):
        pl.when(ng_now == ng)(functools.partial(compressed_and_select, ng))

    def flash_init(state):
        for m_ref, acc_ref in zip(*state):
            m_ref[...] = jnp.full(m_ref.shape, NEG_INF, F32)
            acc_ref[...] = jnp.zeros(acc_ref.shape, F32)

    def flash_tiles(tiles):
        jobs = [(tile, c) for tile in tiles for c in range(n_chains)]

        def key_plan(job):
            (_, _, _, plan, _), chain = job
            return (slice(0, KT), None, None) if plan is None else plan[chain_half(chain % chains_g)]

        scores = {}
        for step in range(len(jobs) + PIPE_DEPTH):
            if step < len(jobs):
                (kp_ref, _, kt, _, _), chain = jobs[step]
                g, c = divmod(chain, chains_g)
                keys, mask, rows = key_plan(jobs[step])
                off = pl.multiple_of(kt * KT + keys.start, HALF)
                s = _dot_nt(kp_ref[pl.ds(off, keys.stop - keys.start), g * SLOT:(g + 1) * SLOT],
                            qg_ref[g, c * CHAIN_COLS:(c + 1) * CHAIN_COLS, :])
                if mask is not None:
                    parts = [s[0:rows.start], jnp.where(mask, s[rows], NEG_INF), s[rows.stop:]]
                    s = jnp.concatenate([p for p in parts if p.shape[0]], axis=0)
                scores[step] = s.astype(BF16)
            done = step - PIPE_DEPTH
            if done >= 0:
                (_, vt_ref, kt, _, (m_refs, acc_refs)), chain = jobs[done]
                g = chain // chains_g
                keys = key_plan(jobs[done])[0]
                m_ref, acc_ref = m_refs[chain], acc_refs[chain]
                s = scores.pop(done)
                m_old = m_ref[...]
                packed = [s[r * BF16_SUBLANES:(r + 1) * BF16_SUBLANES]
                          for r in range(s.shape[0] // BF16_SUBLANES)]
                m_tile = functools.reduce(jnp.maximum, packed).astype(F32)
                m_new = jnp.maximum(m_old, jnp.max(m_tile, axis=0, keepdims=True))
                p = jnp.exp2(s - m_new.astype(BF16))
                pv = _dot(vt_ref[kt, g * VT_ROWS:(g + 1) * VT_ROWS, keys], p)
                acc_ref[...] = jnp.exp2(m_old - m_new) * acc_ref[...] + pv
                m_ref[...] = m_new

    def flash_mix(branch, state):
        _, acc_refs = state
        for h in range(N_HEADS):
            g, r = divmod(h, GROUP)
            gate = gate_row(branch, h)
            for x in range(2):
                c, c0 = divmod(q_rows(r, x).start, CHAIN_COLS)
                acc = acc_refs[g * chains_g + c][:, c0:c0 + HALF]
                qs = slice(x * HALF, (x + 1) * HALF)
                scale = gate[:, qs] * (1.0 / acc[ONE_ROW:ONE_ROW + 1, :])
                mix_ref[h, :, qs] = mix_ref[h, :, qs] + scale * acc[0:HEAD_DIM]

    flash_init(sel_state)
    flash_init(win_state)

    def sel_past(kt):
        return (ksp_ref, vst_ref, kt, None, sel_state)

    def sel_pair(j, carry):
        flash_tiles([sel_past(2 * j), sel_past(2 * j + 1)])
        return carry

    lax.fori_loop(0, i >> 1, sel_pair, 0)

    @pl.when(i & 1 == 1)
    def _():
        flash_tiles([sel_past(i - 1)])

    n_back = WINDOW // KT
    sel_diag = (ksp_ref, vst_ref, i, diag_plan, sel_state)
    win_diag = (kwp_ref, vwt_ref, i, diag_plan, win_state)

    def win_back(d):
        return (kwp_ref, vwt_ref, i - d, far_plan if d == n_back else None, win_state)

    for have in range(n_back + 1):
        cond = (i == have) if have < n_back else (i >= have)

        @pl.when(cond)
        def _(have=have):
            flash_tiles([sel_diag] + [win_back(d) for d in range(have, 0, -1)] + [win_diag])

    flash_mix(1, sel_state)
    flash_mix(2, win_state)
    for h2 in range(N_HEADS // 2):
        pair = jnp.concatenate([mix_ref[2 * h2], mix_ref[2 * h2 + 1]], axis=0)
        o_ref[:, h2 * SLOT:(h2 + 1) * SLOT] = pair.T.astype(BF16)


def _overlap(n_cmp_pad, n_slc):
    c0 = jnp.arange(n_cmp_pad) * CMP_STRIDE
    s0 = jnp.arange(n_slc) * SEL_LEN
    ov = jnp.minimum(c0[None, :] + CMP_LEN, s0[:, None] + SEL_LEN) - jnp.maximum(c0[None, :], s0[:, None])
    return jnp.clip(ov, 0).astype(F32) / CMP_LEN


def _nsa(q, gate, kc, vc, ks, vst, kw, vwt, batch, seq_len):
    n_slc = seq_len // SEL_LEN
    n_cmp_pad = kc.shape[1]
    w_kv_pad = N_KV * SLOT
    assert n_slc <= SLOT - BLK_LANE0 and n_slc % 8 == 0
    assert CHAIN_COLS % HALF == 0 and (GROUP * HALF) % CHAIN_COLS == 0 and KT == 2 * HALF
    ov = _overlap(n_cmp_pad, n_slc)
    nq = seq_len // TQ
    n_kt = seq_len // KT
    qrow = lambda b, i: (b * nq + i, 0)
    per_b = lambda b, i: (b, 0, 0)
    fixed = lambda b, i: (0, 0)
    blocks = (_nbytes((TQ, Q_WIDTH), BF16) + _nbytes((TQ, SLOT), F32)
              + 2 * _nbytes((n_cmp_pad, w_kv_pad), BF16) + 2 * _nbytes((seq_len, KV_WIDTH), BF16)
              + 2 * _nbytes((n_kt, VT_SLAB, KT), BF16)
              + _nbytes(ov.shape, F32) + _nbytes((TQ, Q_WIDTH), BF16))
    n_chains = N_KV * GROUP * TQ // CHAIN_COLS
    run_max = [pltpu.VMEM((1, CHAIN_COLS), F32)] * n_chains
    run_acc = [pltpu.VMEM((VT_ROWS, CHAIN_COLS), F32)] * n_chains
    scratch_shapes = (
        [pltpu.VMEM((seq_len, w_kv_pad), BF16),
         pltpu.VMEM((seq_len, w_kv_pad), BF16),
         pltpu.VMEM((N_KV, GROUP * TQ, SLOT), BF16),
         pltpu.VMEM((N_HEADS, HEAD_DIM, TQ), F32)]
        + run_max + run_acc + run_max + run_acc)
    scratch = sum(_nbytes(s.shape, s.dtype) for s in scratch_shapes)
    return pl.pallas_call(
        functools.partial(_nsa_kernel, n_slc=n_slc),
        grid=(batch, nq),
        in_specs=[
            pl.BlockSpec((TQ, Q_WIDTH), qrow),
            pl.BlockSpec((TQ, SLOT), qrow),
            pl.BlockSpec((1, n_cmp_pad, w_kv_pad), per_b),
            pl.BlockSpec((1, n_cmp_pad, w_kv_pad), per_b),
            pl.BlockSpec((1, seq_len, KV_WIDTH), per_b),
            pl.BlockSpec((n_kt, VT_SLAB, KT), per_b),
            pl.BlockSpec((1, seq_len, KV_WIDTH), per_b),
            pl.BlockSpec((n_kt, VT_SLAB, KT), per_b),
            pl.BlockSpec(ov.shape, fixed),
        ],
        out_specs=pl.BlockSpec((TQ, Q_WIDTH), qrow),
        out_shape=jax.ShapeDtypeStruct((batch * seq_len, Q_WIDTH), BF16),
        scratch_shapes=scratch_shapes,
        compiler_params=pltpu.CompilerParams(
            dimension_semantics=("arbitrary", "arbitrary"),
            vmem_limit_bytes=_vmem_limit(blocks, scratch)),
        name="nsa",
    )(q, gate, kc.reshape(batch, n_cmp_pad, w_kv_pad), vc.reshape(batch, n_cmp_pad, w_kv_pad),
      ks.reshape(batch, seq_len, KV_WIDTH), vst, kw.reshape(batch, seq_len, KV_WIDTH), vwt, ov)


def _pool_kernel(u_ref, w_ref, scale_ref, o_ref):
    seq = u_ref.shape[0]
    t = lax.broadcasted_iota(jnp.int32, (seq, POOL_GROUP), 0)

    def shifted(x, k):
        return jnp.where(t >= k, pltpu.roll(x, k, axis=0), 0.0)

    for gi, w in enumerate(POOL_WINDOWS):
        sl = slice(gi * POOL_GROUP, (gi + 1) * POOL_GROUP)
        x = u_ref[:, sl]
        wsum = x
        span = 1
        while span < w:
            wsum = wsum + shifted(wsum, span)
            span *= 2
        cnt = jnp.minimum(t + 1, w).astype(F32)
        pooled = (wsum / cnt - x).astype(BF16)
        o_ref[:, sl] = (_dot(pooled, w_ref[gi]) * scale_ref[:, sl]).astype(BF16)


def _pool(pool_in, pool_w, pool_scale, batch, seq_len):
    for w in POOL_WINDOWS:
        assert w & (w - 1) == 0
    blocks = (_nbytes((seq_len, POOL_WIDTH), F32) + _nbytes(pool_w.shape, BF16)
              + _nbytes((seq_len, POOL_WIDTH), BF16))
    return pl.pallas_call(
        _pool_kernel,
        grid=(batch,),
        in_specs=[
            pl.BlockSpec((seq_len, POOL_WIDTH), lambda b: (b, 0)),
            pl.BlockSpec(pool_w.shape, lambda b: (0, 0, 0)),
            pl.BlockSpec((1, POOL_WIDTH), lambda b: (0, 0)),
        ],
        out_specs=pl.BlockSpec((seq_len, POOL_WIDTH), lambda b: (b, 0)),
        out_shape=jax.ShapeDtypeStruct((batch * seq_len, POOL_WIDTH), BF16),
        compiler_params=pltpu.CompilerParams(
            dimension_semantics=("arbitrary",), vmem_limit_bytes=_vmem_limit(blocks)),
        name="pool",
    )(pool_in, pool_w.astype(BF16), pool_scale.reshape(1, -1))


def _merge_kernel(x_ref, gpre_ref, wt_ref, nsa_ref, wa_ref, pool_ref, wp_ref, wo_ref, gpost_ref,
                  o_ref):
    x = x_ref[...]
    hn = _rms(x, gpre_ref[...])
    g_attn = jax.nn.sigmoid(_dot_nt(hn, wt_ref[OFF_MERGE:OFF_MERGE + D_MODEL, :]))
    g_pool = jax.nn.sigmoid(_dot_nt(hn, wt_ref[OFF_MERGE + D_MODEL:OFF_MERGE + 2 * D_MODEL, :]))
    y = (g_attn * _dot(nsa_ref[...].astype(F32), wa_ref[...])
         + g_pool * _dot(pool_ref[...].astype(F32), wp_ref[...]))
    h = _dot(y, wo_ref[...])
    o_ref[...] = x + _rms(h, gpost_ref[...])


def _merge(x1, g_pre, w_in_t, o_nsa, w_attn, o_pool, w_pool, w_out, g_post):
    t = x1.shape[0]
    row = lambda i: (i, 0)
    fixed = lambda i: (0, 0)
    once = pl.Buffered(1)
    weights = (_nbytes(w_in_t.shape, F32) + _nbytes((Q_WIDTH, D_MODEL), F32)
               + _nbytes((POOL_WIDTH, D_MODEL), F32) + _nbytes((D_MODEL, D_MODEL), F32))
    blocks = (2 * _nbytes((TOK_TILE, D_MODEL), F32) + _nbytes((TOK_TILE, Q_WIDTH), BF16)
              + _nbytes((TOK_TILE, POOL_WIDTH), BF16))
    return pl.pallas_call(
        _merge_kernel,
        grid=(t // TOK_TILE,),
        in_specs=[
            pl.BlockSpec((TOK_TILE, D_MODEL), row),
            pl.BlockSpec((1, D_MODEL), fixed),
            pl.BlockSpec(w_in_t.shape, fixed, pipeline_mode=once),
            pl.BlockSpec((TOK_TILE, Q_WIDTH), row),
            pl.BlockSpec((Q_WIDTH, D_MODEL), fixed, pipeline_mode=once),
            pl.BlockSpec((TOK_TILE, POOL_WIDTH), row),
            pl.BlockSpec((POOL_WIDTH, D_MODEL), fixed, pipeline_mode=once),
            pl.BlockSpec((D_MODEL, D_MODEL), fixed, pipeline_mode=once),
            pl.BlockSpec((1, D_MODEL), fixed),
        ],
        out_specs=pl.BlockSpec((TOK_TILE, D_MODEL), row),
        out_shape=jax.ShapeDtypeStruct((t, D_MODEL), F32),
        compiler_params=pltpu.CompilerParams(
            dimension_semantics=("arbitrary",), vmem_limit_bytes=_vmem_limit(blocks, weights)),
        name="merge",
    )(x1, g_pre.reshape(1, -1), w_in_t, o_nsa, w_attn, o_pool, w_pool, w_out, g_post.reshape(1, -1))


def _layer(x, tab, tab_cmp, batch, seq_len, g_ffn1_pre, w_ffn1_gate, w_ffn1_up, w_ffn1_down,
           g_ffn1_post, g_mix_pre, w_in, cmp_pe_k, cmp_w1_k, cmp_w2_k, cmp_pe_v, cmp_w1_v, cmp_w2_v,
           w_attn_branch, pool_w, pool_scale, w_pool_branch, w_out, g_mix_post, g_ffn2_pre,
           w_ffn2_gate, w_ffn2_up, w_ffn2_down, g_ffn2_post):
    x1 = _ffn(x, g_ffn1_pre, w_ffn1_gate, w_ffn1_up, w_ffn1_down, g_ffn1_post)
    w_in_t = w_in.T
    q, ks, kw, vst, vwt, cmp_rows, gate, pool_in = _in_proj(x1, g_mix_pre, w_in_t, tab)
    kc, vc = _compress(cmp_rows, cmp_pe_k, cmp_w1_k, cmp_w2_k, cmp_pe_v, cmp_w1_v, cmp_w2_v, tab_cmp)
    n_cmp_pad = seq_len // CMP_STRIDE
    kc = kc.reshape(batch, n_cmp_pad, N_KV * SLOT)
    vc = vc.reshape(batch, n_cmp_pad, N_KV * SLOT)
    o_nsa = _nsa(q, gate, kc, vc, ks, vst, kw, vwt, batch, seq_len)
    o_pool = _pool(pool_in, pool_w, pool_scale, batch, seq_len)
    x2 = _merge(x1, g_mix_pre, w_in_t, o_nsa, w_attn_branch, o_pool, w_pool_branch, w_out,
                g_mix_post)
    return _ffn(x2, g_ffn2_pre, w_ffn2_gate, w_ffn2_up, w_ffn2_down, g_ffn2_post)


def kernel(x, positions, g_ffn1_pre, w_ffn1_gate, w_ffn1_up, w_ffn1_down, g_ffn1_post, g_mix_pre, w_in, cmp_pe_k, cmp_w1_k, cmp_w2_k, cmp_pe_v, cmp_w1_v, cmp_w2_v, w_attn_branch, pool_w, pool_scale, w_pool_branch, w_out, g_mix_post, g_ffn2_pre, w_ffn2_gate, w_ffn2_up, w_ffn2_down, g_ffn2_post):
    batch, seq_len, d_model = x.shape
    assert d_model == D_MODEL and seq_len % TOK_TILE == 0 and seq_len % TQ == 0
    assert TQ == KT and TQ % SEL_LEN == 0 and TOK_TILE % KT == 0 and WINDOW % KT == 0
    assert seq_len % CMP_STRIDE == 0 and CMP_LEN == 2 * CMP_STRIDE
    t = batch * seq_len
    tab = _rope_tables(positions.reshape(t), seq_len)
    n_cmp = (seq_len - CMP_LEN) // CMP_STRIDE + 1
    n_cmp_pad = seq_len // CMP_STRIDE
    pos_cmp = jnp.pad(positions[:, CMP_LEN - 1::CMP_STRIDE], ((0, 0), (0, n_cmp_pad - n_cmp)))
    tab_cmp = _rope_tables(pos_cmp.reshape(batch * n_cmp_pad), batch * n_cmp_pad)
    xf = x.reshape(t, D_MODEL)
    per_layer = (g_ffn1_pre, w_ffn1_gate, w_ffn1_up, w_ffn1_down, g_ffn1_post, g_mix_pre, w_in,
                 cmp_pe_k, cmp_w1_k, cmp_w2_k, cmp_pe_v, cmp_w1_v, cmp_w2_v, w_attn_branch, pool_w,
                 pool_scale, w_pool_branch, w_out, g_mix_post, g_ffn2_pre, w_ffn2_gate, w_ffn2_up,
                 w_ffn2_down, g_ffn2_post)
    for l in range(g_ffn1_pre.shape[0]):
        xf = _layer(xf, tab, tab_cmp, batch, seq_len, *(p[l] for p in per_layer))
    return xf.reshape(batch, seq_len, D_MODEL)
```

```python
import functools
import math

import jax
import jax.numpy as jnp
from jax import lax
from jax.experimental import pallas as pl
from jax.experimental.pallas import tpu as pltpu

F32 = jnp.float32
BF16 = jnp.bfloat16

D_MODEL = 1024
N_HEADS = 16
HEAD_DIM = 64
N_KV = 4
GROUP = N_HEADS // N_KV
ROT_DIM = HEAD_DIM // 4
ROT_HALF = ROT_DIM // 2
ROPE_THETA = 500000.0
CMP_LEN = 32
CMP_STRIDE = 16
CMP_HIDDEN = 2 * HEAD_DIM
SEL_LEN = 64
SEL_SHIFT = SEL_LEN.bit_length() - 1
N_SEL = 8
WINDOW = 512
POOL_WINDOWS = (2, 4, 8, 16)
POOL_WIDTH = D_MODEL // 2
POOL_GROUP = POOL_WIDTH // len(POOL_WINDOWS)
D_FF = 2816
EPS = 1e-6
NEG_INF = -1e30
FORCE_SCORE = 1e4
Q_WIDTH = N_HEADS * HEAD_DIM
KV_WIDTH = N_KV * HEAD_DIM
N_GATES = 3 * N_HEADS
LOG2_E = math.log2(math.e)

LANES = 128
BF16_SUBLANES = 16
V7X_VMEM_BYTES = 64 * 1024 * 1024

SLOT = LANES
BLK_LANE0 = HEAD_DIM
VT_ROWS = HEAD_DIM + BF16_SUBLANES
ONE_ROW = HEAD_DIM
TOK_TILE = 512
TQ = 256
KT = 256
HALF = TQ // 2
CHAIN_COLS = 256
PIPE_DEPTH = 10
FF_CHUNK = 256
IN_SPLIT = 2
N_ROPE_TAB = 2


def _vmem_limit(block_bytes, scratch_bytes=0):
    need = 2 * block_bytes + scratch_bytes
    return int(min(V7X_VMEM_BYTES - (4 << 20), max(2 * need, 32 << 20)))


def _nbytes(shape, dtype):
    n = 1
    for s in shape:
        n *= s
    return n * jnp.dtype(dtype).itemsize


def _rms(xf, g):
    return xf * lax.rsqrt(jnp.mean(xf * xf, axis=-1, keepdims=True) + EPS) * g


def _dot(a, b):
    return jnp.dot(a, b, preferred_element_type=F32)


def _dot_nt(a, b):
    return lax.dot_general(a, b, (((1,), (1,)), ((), ())), preferred_element_type=F32)


def _rope_trig_kernel(pos_ref, inv_ref, tab_ref):
    tn = pos_ref.shape[1]
    ang = pos_ref[...].astype(F32) * inv_ref[...]
    c, s = jnp.cos(ang), jnp.sin(ang)
    rest = HEAD_DIM - ROT_DIM
    reps = SLOT // HEAD_DIM
    cos_rows = jnp.concatenate([c, c, jnp.ones((rest, tn), F32)] * reps, axis=0)
    sin_rows = jnp.concatenate([-s, s, jnp.zeros((rest, tn), F32)] * reps, axis=0)
    for j in range(tn // LANES):
        cs = slice(j * LANES, (j + 1) * LANES)
        tab_ref[cs, 0:SLOT] = cos_rows[:, cs].T
        tab_ref[cs, SLOT:2 * SLOT] = sin_rows[:, cs].T


def _rope_tables(pos_flat, tile):
    n = pos_flat.shape[0]
    inv = ROPE_THETA ** (-jnp.arange(ROT_HALF, dtype=F32) * (2.0 / ROT_DIM))
    return pl.pallas_call(
        _rope_trig_kernel,
        grid=(n // tile,),
        in_specs=[pl.BlockSpec((1, tile), lambda i: (0, i)),
                  pl.BlockSpec((ROT_HALF, 1), lambda i: (0, 0))],
        out_specs=pl.BlockSpec((tile, N_ROPE_TAB * SLOT), lambda i: (i, 0)),
        out_shape=jax.ShapeDtypeStruct((n, N_ROPE_TAB * SLOT), F32),
        compiler_params=pltpu.CompilerParams(dimension_semantics=("arbitrary",)),
        name="rope_trig",
    )(pos_flat.reshape(1, n), inv.reshape(ROT_HALF, 1))


def _rope_slot(y, tab):
    cos_t = tab[:, 0:SLOT]
    sin_t = tab[:, SLOT:2 * SLOT]
    lane = lax.broadcasted_iota(jnp.int32, y.shape, 1) & (HEAD_DIM - 1)
    up = pltpu.roll(y, SLOT - ROT_HALF, axis=1)
    down = pltpu.roll(y, ROT_HALF, axis=1)
    return y * cos_t + jnp.where(lane < ROT_HALF, up, down) * sin_t


def _ffn_kernel(x_ref, gpre_ref, wg_ref, wu_ref, wd_ref, gpost_ref, o_ref, acc_ref):
    x = x_ref[...]
    hn = _rms(x, gpre_ref[...])
    for c in range(D_FF // FF_CHUNK):
        sl = slice(c * FF_CHUNK, (c + 1) * FF_CHUNK)
        g = _dot(hn, wg_ref[:, sl])
        u = _dot(hn, wu_ref[:, sl])
        d = _dot(g * jax.nn.sigmoid(g) * u, wd_ref[sl, :])
        if c == 0:
            acc_ref[...] = d
        else:
            acc_ref[...] += d
    o_ref[...] = x + 0.5 * _rms(acc_ref[...], gpost_ref[...])


def _ffn(x, g_pre, w_gate, w_up, w_down, g_post):
    t = x.shape[0]
    tile = TOK_TILE
    row = lambda i: (i, 0)
    fixed = lambda i: (0, 0)
    once = pl.Buffered(1)
    weights = 3 * _nbytes((D_MODEL, D_FF), F32)
    blocks = 2 * _nbytes((tile, D_MODEL), F32)
    return pl.pallas_call(
        _ffn_kernel,
        grid=(t // tile,),
        in_specs=[
            pl.BlockSpec((tile, D_MODEL), row),
            pl.BlockSpec((1, D_MODEL), fixed),
            pl.BlockSpec((D_MODEL, D_FF), fixed, pipeline_mode=once),
            pl.BlockSpec((D_MODEL, D_FF), fixed, pipeline_mode=once),
            pl.BlockSpec((D_FF, D_MODEL), fixed, pipeline_mode=once),
            pl.BlockSpec((1, D_MODEL), fixed),
        ],
        out_specs=pl.BlockSpec((tile, D_MODEL), row),
        out_shape=jax.ShapeDtypeStruct((t, D_MODEL), F32),
        scratch_shapes=[pltpu.VMEM((tile, D_MODEL), F32)],
        compiler_params=pltpu.CompilerParams(
            dimension_semantics=("arbitrary",),
            vmem_limit_bytes=_vmem_limit(blocks, weights + _nbytes((tile, D_MODEL), F32))),
        name="ffn",
    )(x, g_pre.reshape(1, -1), w_gate, w_up, w_down, g_post.reshape(1, -1))


OFF_Q = 0
OFF_CMP = OFF_Q + Q_WIDTH
OFF_KS = OFF_CMP + 2 * KV_WIDTH
OFF_VS = OFF_KS + KV_WIDTH
OFF_KW = OFF_VS + KV_WIDTH
OFF_VW = OFF_KW + KV_WIDTH
OFF_GATE = OFF_VW + KV_WIDTH
OFF_POOL = OFF_GATE + N_GATES
OFF_MERGE = OFF_POOL + POOL_WIDTH
W_CMP = 2 * KV_WIDTH
CHUNK_TOK = CMP_STRIDE
VT_SLAB = N_KV * VT_ROWS


def _in_proj_kernel(x_ref, g_ref, w_ref, tab_ref, q_ref, ks_ref, kw_ref, vst_ref,
                    vwt_ref, cmp_ref, gate_ref, pool_ref, cmp_scr):
    rows = x_ref.shape[0] // IN_SPLIT
    wide = 4 * SLOT
    aux = (lax.broadcasted_iota(jnp.int32, (VT_ROWS - HEAD_DIM, KT), 0) == 0).astype(BF16)

    def part(k):
        rs = slice(k * rows, (k + 1) * rows)
        hn = _rms(x_ref[rs, :], g_ref[...])
        tab = tab_ref[rs, :]
        tab_q = tab * (HEAD_DIM ** -0.5 * LOG2_E)
        for c in range(Q_WIDTH // wide):
            y = _dot_nt(hn, w_ref[OFF_Q + c * wide:OFF_Q + (c + 1) * wide, :])
            for j in range(wide // SLOT):
                sl = slice(c * wide + j * SLOT, c * wide + (j + 1) * SLOT)
                q_ref[rs, sl] = _rope_slot(y[:, j * SLOT:(j + 1) * SLOT], tab_q).astype(BF16)
        for k_ref, off in ((ks_ref, OFF_KS), (kw_ref, OFF_KW)):
            y = _dot_nt(hn, w_ref[off:off + KV_WIDTH, :])
            for j in range(KV_WIDTH // SLOT):
                sl = slice(j * SLOT, (j + 1) * SLOT)
                k_ref[rs, sl] = _rope_slot(y[:, sl], tab).astype(BF16)
        y = _dot_nt(hn, w_ref[OFF_CMP:OFF_CMP + W_CMP, :])
        for j in range(W_CMP // LANES):
            cmp_scr[j, rs, :] = y[:, j * LANES:(j + 1) * LANES]
        chunk_rows = rows // CHUNK_TOK
        for l in range(CHUNK_TOK):
            for j in range(W_CMP // LANES):
                c0 = l * W_CMP + j * LANES
                cmp_ref[k * chunk_rows:(k + 1) * chunk_rows, c0:c0 + LANES] = (
                    cmp_scr[j, pl.ds(k * rows + l, chunk_rows, stride=CHUNK_TOK), :].astype(BF16))
        gate_ref[rs, :] = _dot_nt(hn, w_ref[OFF_GATE:OFF_GATE + SLOT, :])
        pool_ref[rs, :] = _dot_nt(hn, w_ref[OFF_POOL:OFF_MERGE, :])
        tiles = rows // KT
        for vt_ref, off in ((vst_ref, OFF_VS), (vwt_ref, OFF_VW)):
            vt = _dot_nt(w_ref[off:off + KV_WIDTH, :], hn).astype(BF16)
            for j in range(tiles):
                for g in range(N_KV):
                    vt_ref[k * tiles + j, g * VT_ROWS:g * VT_ROWS + HEAD_DIM, :] = (
                        vt[g * HEAD_DIM:(g + 1) * HEAD_DIM, j * KT:(j + 1) * KT])
                    vt_ref[k * tiles + j, g * VT_ROWS + HEAD_DIM:(g + 1) * VT_ROWS, :] = aux

    for k in range(IN_SPLIT):
        part(k)


def _in_proj(x1, g_pre, w_in_t, tab):
    t = x1.shape[0]
    row = lambda i: (i, 0)
    fixed = lambda i: (0, 0)
    once = pl.Buffered(1)
    tile = IN_SPLIT * TOK_TILE
    tiles = tile // KT
    chunk_rows = tile // CHUNK_TOK
    slab = pl.BlockSpec((tiles, VT_SLAB, KT), lambda i: (i, 0, 0))
    slab_shape = jax.ShapeDtypeStruct((t // KT, VT_SLAB, KT), BF16)
    out_specs = [
        pl.BlockSpec((tile, Q_WIDTH), row), pl.BlockSpec((tile, KV_WIDTH), row),
        pl.BlockSpec((tile, KV_WIDTH), row), slab, slab,
        pl.BlockSpec((chunk_rows, CHUNK_TOK * W_CMP), row),
        pl.BlockSpec((tile, SLOT), row), pl.BlockSpec((tile, POOL_WIDTH), row)]
    out_shape = [
        jax.ShapeDtypeStruct((t, Q_WIDTH), BF16), jax.ShapeDtypeStruct((t, KV_WIDTH), BF16),
        jax.ShapeDtypeStruct((t, KV_WIDTH), BF16), slab_shape, slab_shape,
        jax.ShapeDtypeStruct((t // CHUNK_TOK, CHUNK_TOK * W_CMP), BF16),
        jax.ShapeDtypeStruct((t, SLOT), F32), jax.ShapeDtypeStruct((t, POOL_WIDTH), F32)]
    weights = _nbytes(w_in_t.shape, F32)
    blocks = (_nbytes((tile, D_MODEL), F32) + _nbytes((tile, N_ROPE_TAB * SLOT), F32)
              + _nbytes((tile, Q_WIDTH + 2 * KV_WIDTH + W_CMP), BF16)
              + 2 * _nbytes((tiles, VT_SLAB, KT), BF16) + _nbytes((tile, SLOT + POOL_WIDTH), F32))
    res = pl.pallas_call(
        _in_proj_kernel,
        grid=(t // tile,),
        in_specs=[
            pl.BlockSpec((tile, D_MODEL), row),
            pl.BlockSpec((1, D_MODEL), fixed),
            pl.BlockSpec(w_in_t.shape, fixed, pipeline_mode=once),
            pl.BlockSpec((tile, N_ROPE_TAB * SLOT), row),
        ],
        out_specs=out_specs,
        out_shape=out_shape,
        scratch_shapes=[pltpu.VMEM((W_CMP // LANES, tile, LANES), F32)],
        compiler_params=pltpu.CompilerParams(
            dimension_semantics=("arbitrary",),
            vmem_limit_bytes=_vmem_limit(blocks, weights + _nbytes((tile, W_CMP), F32))),
        name="in_proj",
    )(x1, g_pre.reshape(1, -1), w_in_t, tab)
    return res


def _compress_kernel(x_ref, w1k_ref, w1v_ref, pek_ref, pev_ref, w1k_raw_ref, w1v_raw_ref,
                     w2k_ref, w2v_ref, tab_ref, kc_ref, vc_ref, acck_ref, accv_ref):
    l = pl.program_id(0)

    @pl.when(l == 0)
    def _():
        acck_ref[...] = jnp.zeros(acck_ref.shape, F32)
        accv_ref[...] = jnp.zeros(accv_ref.shape, F32)

    for acc_ref, w1_ref, c0 in ((acck_ref, w1k_ref, 0), (accv_ref, w1v_ref, KV_WIDTH)):
        for g in range(N_KV):
            acc_ref[g] += _dot(x_ref[:, c0 + g * HEAD_DIM:c0 + (g + 1) * HEAD_DIM], w1_ref[0])

    @pl.when(l == CHUNK_TOK - 1)
    def _():
        rows = acck_ref.shape[1]
        tab = tab_ref[...]
        for acc_ref, pe_ref, w1_raw_ref, w2_ref, o_ref, rope in (
                (acck_ref, pek_ref, w1k_raw_ref, w2k_ref, kc_ref, True),
                (accv_ref, pev_ref, w1v_raw_ref, w2v_ref, vc_ref, False)):
            bias = _dot(pe_ref[...], w1_raw_ref[...])[0:1, :]
            for g in range(N_KV):
                first = acc_ref[g, :, 0:CMP_HIDDEN]
                second = pltpu.roll(acc_ref[g, :, CMP_HIDDEN:2 * CMP_HIDDEN], rows - 1, axis=0)
                hid = jax.nn.gelu(first + second + bias).astype(BF16)
                out = _dot(hid, w2_ref[...])
                if rope:
                    out = _rope_slot(out, tab)
                o_ref[:, g * SLOT:(g + 1) * SLOT] = out.astype(BF16)


def _compress_weights(w1, w2):
    w1 = w1.reshape(2, CHUNK_TOK, HEAD_DIM, CMP_HIDDEN)
    w1_tok = jnp.concatenate([w1[0], w1[1]], axis=-1).astype(BF16)
    return w1_tok, jnp.pad(w2, ((0, 0), (0, SLOT - HEAD_DIM))).astype(BF16)


def _compress(cmp_rows, cmp_pe_k, cmp_w1_k, cmp_w2_k, cmp_pe_v, cmp_w1_v, cmp_w2_v, tab_cmp):
    rows = cmp_rows.shape[0]
    w1k, w2k = _compress_weights(cmp_w1_k, cmp_w2_k)
    w1v, w2v = _compress_weights(cmp_w1_v, cmp_w2_v)
    sub = 8
    pek = jnp.broadcast_to(cmp_pe_k.reshape(1, -1), (sub, CMP_LEN * HEAD_DIM)).astype(BF16)
    pev = jnp.broadcast_to(cmp_pe_v.reshape(1, -1), (sub, CMP_LEN * HEAD_DIM)).astype(BF16)
    fixed = lambda l: (0, 0)
    w_kv_pad = N_KV * SLOT
    blocks = (_nbytes((rows, W_CMP), BF16) + 2 * _nbytes((HEAD_DIM, 2 * CMP_HIDDEN), BF16)
              + 2 * _nbytes((sub + CMP_LEN * HEAD_DIM, CMP_HIDDEN), BF16)
              + 2 * _nbytes((CMP_HIDDEN, SLOT), BF16)
              + _nbytes((rows, N_ROPE_TAB * SLOT), F32) + 2 * _nbytes((rows, w_kv_pad), BF16))
    acc = pltpu.VMEM((N_KV, rows, 2 * CMP_HIDDEN), F32)
    return pl.pallas_call(
        _compress_kernel,
        grid=(CHUNK_TOK,),
        in_specs=[
            pl.BlockSpec((rows, W_CMP), lambda l: (0, l)),
            pl.BlockSpec((1, HEAD_DIM, 2 * CMP_HIDDEN), lambda l: (l, 0, 0)),
            pl.BlockSpec((1, HEAD_DIM, 2 * CMP_HIDDEN), lambda l: (l, 0, 0)),
            pl.BlockSpec((sub, CMP_LEN * HEAD_DIM), fixed),
            pl.BlockSpec((sub, CMP_LEN * HEAD_DIM), fixed),
            pl.BlockSpec((CMP_LEN * HEAD_DIM, CMP_HIDDEN), fixed),
            pl.BlockSpec((CMP_LEN * HEAD_DIM, CMP_HIDDEN), fixed),
            pl.BlockSpec((CMP_HIDDEN, SLOT), fixed),
            pl.BlockSpec((CMP_HIDDEN, SLOT), fixed),
            pl.BlockSpec((rows, N_ROPE_TAB * SLOT), fixed),
        ],
        out_specs=[pl.BlockSpec((rows, w_kv_pad), fixed)] * 2,
        out_shape=[jax.ShapeDtypeStruct((rows, w_kv_pad), BF16)] * 2,
        scratch_shapes=[acc, acc],
        compiler_params=pltpu.CompilerParams(
            dimension_semantics=("arbitrary",),
            vmem_limit_bytes=_vmem_limit(blocks, 2 * _nbytes(acc.shape, F32))),
        name="compress",
    )(cmp_rows, w1k, w1v, pek, pev, cmp_w1_k.astype(BF16), cmp_w1_v.astype(BF16), w2k, w2v, tab_cmp)


def _nsa_kernel(q_ref, gate_ref, kc_ref, vc_ref, ks_ref, vst_ref, kw_ref, vwt_ref, ov_ref,
                o_ref, ksp_ref, kwp_ref, qg_ref, mix_ref, *state_refs, n_slc):
    i = pl.program_id(1)
    start = i * TQ
    seq = ks_ref.shape[1]
    cols_g = GROUP * TQ
    chains_g = cols_g // CHAIN_COLS
    n_chains = N_KV * chains_g
    n_cmp = kc_ref.shape[1]
    sel_state = (state_refs[0:n_chains], state_refs[n_chains:2 * n_chains])
    win_state = (state_refs[2 * n_chains:3 * n_chains], state_refs[3 * n_chains:4 * n_chains])
    lane_q = lax.broadcasted_iota(jnp.int32, (TQ, SLOT), 1)
    lane_h = lax.broadcasted_iota(jnp.int32, (HALF, SLOT), 1)

    def head_slot(x_ref, rows, j):
        two = x_ref[rows, (j // 2) * SLOT:(j // 2 + 1) * SLOT]
        if j % 2:
            words = pltpu.roll(pltpu.bitcast(two, jnp.uint32), HEAD_DIM, axis=1)
            two = pltpu.bitcast(words, BF16)
        return two

    @pl.when(i == 0)
    def _():
        for c in range(seq // KT):
            rows = pl.ds(c * KT, KT)
            lane = lax.broadcasted_iota(jnp.int32, (KT, SLOT), 1)
            key_blk = (c * KT + lax.broadcasted_iota(jnp.int32, (KT, SLOT), 0)) >> SEL_SHIFT
            onehot = jnp.where(lane - BLK_LANE0 == key_blk, 1.0, 0.0).astype(BF16)
            zeros = jnp.zeros((KT, SLOT), BF16)
            for g in range(N_KV):
                gs = slice(g * SLOT, (g + 1) * SLOT)
                ksp_ref[rows, gs] = jnp.where(lane < HEAD_DIM, head_slot(ks_ref.at[0], rows, g), onehot)
                kwp_ref[rows, gs] = jnp.where(lane < HEAD_DIM, head_slot(kw_ref.at[0], rows, g), zeros)

    def q_rows(r, x):
        return slice((x * GROUP + r) * HALF, (x * GROUP + r + 1) * HALF)

    def chain_half(c):
        return c * CHAIN_COLS // (GROUP * HALF)

    for h in range(N_HEADS):
        g, r = divmod(h, GROUP)
        q_pad = jnp.where(lane_q < HEAD_DIM, head_slot(q_ref, slice(None), h), jnp.zeros((TQ, SLOT), BF16))
        for x in range(2):
            qg_ref[g, q_rows(r, x), :] = q_pad[x * HALF:(x + 1) * HALF]

    gates_t = jax.nn.sigmoid(gate_ref[...]).T
    any_cmp = start + lax.broadcasted_iota(jnp.int32, (1, TQ), 1) >= CMP_LEN - 1
    key_h = lax.broadcasted_iota(jnp.int32, (HALF, CHAIN_COLS), 0)
    qry_h = lax.broadcasted_iota(jnp.int32, (HALF, CHAIN_COLS), 1) & (HALF - 1)
    tri_le, tri_gt = key_h <= qry_h, key_h > qry_h
    diag_plan = ((slice(0, HALF), tri_le, slice(0, HALF)), (slice(0, KT), tri_le, slice(HALF, KT)))
    far_plan = ((slice(0, KT), tri_gt, slice(0, HALF)), (slice(HALF, KT), tri_gt, slice(0, HALF)))

    def gate_row(branch, h):
        c = branch * N_HEADS + h
        return gates_t[c:c + 1, :]

    sub = 8
    cmp_per_sub = sub * SEL_LEN // CMP_STRIDE

    def compressed_and_select(ng):
        n_blk, n_c = ng * sub, ng * cmp_per_sub
        t_cmp = start + lax.broadcasted_iota(jnp.int32, (n_c, TQ), 1)
        n_idx = lax.broadcasted_iota(jnp.int32, (n_c, TQ), 0)
        cmp_valid = n_idx * CMP_STRIDE + (CMP_LEN - 1) <= t_cmp
        t_row = start + lax.broadcasted_iota(jnp.int32, (n_blk, TQ), 1)
        blk = lax.broadcasted_iota(jnp.int32, (n_blk, TQ), 0)
        forced = (blk == t_row >> SEL_SHIFT) | (blk == 0)
        causal_blk = blk * SEL_LEN <= t_row
        idx8 = lax.broadcasted_iota(jnp.int32, (sub, TQ), 0)
        pad_c = jnp.zeros((n_cmp - n_c, TQ), F32)
        cmp_scores = []
        for h in range(N_HEADS):
            g, r = divmod(h, GROUP)
            q_h = jnp.concatenate([qg_ref[g, q_rows(r, 0), :], qg_ref[g, q_rows(r, 1), :]], axis=0)
            cmp_scores.append(_dot_nt(kc_ref[0, 0:n_c, g * SLOT:(g + 1) * SLOT], q_h))
        for g in range(N_KV):
            vct_g = vc_ref[0, :, g * SLOT:(g + 1) * SLOT].astype(F32).T[0:HEAD_DIM].astype(BF16)
            p_sum = jnp.zeros((n_c, TQ), F32)
            for r in range(GROUP):
                h = g * GROUP + r
                s = jnp.where(cmp_valid, cmp_scores[h], NEG_INF)
                p = jnp.exp2(s - jnp.max(s, axis=0, keepdims=True))
                inv = jnp.where(any_cmp, 1.0 / jnp.sum(p, axis=0, keepdims=True), 0.0)
                p = p * inv
                p_sum = p_sum + p
                p_all = jnp.concatenate([p, pad_c], axis=0) if ng * cmp_per_sub < n_cmp else p
                mix_ref[h] = gate_row(0, h) * _dot(vct_g, p_all.astype(BF16))
            p_sum_all = jnp.concatenate([p_sum, pad_c], axis=0) if ng * cmp_per_sub < n_cmp else p_sum
            imp = jnp.dot(ov_ref[0:n_blk, :], p_sum_all, precision=lax.Precision.HIGHEST,
                          preferred_element_type=F32)
            score = jnp.where(causal_blk, jnp.where(forced, FORCE_SCORE, imp), NEG_INF)
            rows8 = [score[k * sub:(k + 1) * sub] for k in range(ng)]
            ranks = [jnp.zeros((sub, TQ), jnp.int32) for _ in rows8]
            for c in range(n_blk):
                other = score[c:c + 1, :]
                for k, mine in enumerate(rows8):
                    if k * sub > c:
                        beats = other >= mine
                    elif (k + 1) * sub - 1 <= c:
                        beats = other > mine
                    else:
                        beats = (other > mine) | ((other == mine) & (idx8 > c - k * sub))
                    ranks[k] = ranks[k] + beats.astype(jnp.int32)
            rank = jnp.concatenate(ranks, axis=0) if ng > 1 else ranks[0]
            bias = jnp.where(causal_blk & (rank < N_SEL), 0.0, NEG_INF)
            bias_t = jnp.concatenate([jnp.zeros((BLK_LANE0, TQ), F32), bias,
                                      jnp.zeros((SLOT - BLK_LANE0 - n_blk, TQ), F32)], axis=0)
            bias_q = bias_t.T.astype(BF16)
            for r in range(GROUP):
                for x in range(2):
                    rs = q_rows(r, x)
                    qg_ref[g, rs, :] = jnp.where(lane_h < HEAD_DIM, qg_ref[g, rs, :],
                                                 bias_q[x * HALF:(x + 1) * HALF])

    ng_max = n_slc // sub
    ng_now = ((i + 1) * TQ + sub * SEL_LEN - 1) // (sub * SEL_LEN)
    for ng in range(1, ng_max + 1):
        pl.when(ng_now == ng)(functools.partial(compressed_and_select, ng))

    def flash_init(state):
        for m_ref, acc_ref in zip(*state):
            m_ref[...] = jnp.full(m_ref.shape, NEG_INF, F32)
            acc_ref[...] = jnp.zeros(acc_ref.shape, F32)

    def flash_tiles(tiles):
        jobs = [(tile, c) for tile in tiles for c in range(n_chains)]

        def key_plan(job):
            (_, _, _, plan, _), chain = job
            return (slice(0, KT), None, None) if plan is None else plan[chain_half(chain % chains_g)]

        scores = {}
        for step in range(len(jobs) + PIPE_DEPTH):
            if step < len(jobs):
                (kp_ref, _, kt, _, _), chain = jobs[step]
                g, c = divmod(chain, chains_g)
                keys, mask, rows = key_plan(jobs[step])
                off = pl.multiple_of(kt * KT + keys.start, HALF)
                s = _dot_nt(kp_ref[pl.ds(off, keys.stop - keys.start), g * SLOT:(g + 1) * SLOT],
                            qg_ref[g, c * CHAIN_COLS:(c + 1) * CHAIN_COLS, :])
                if mask is not None:
                    parts = [s[0:rows.start], jnp.where(mask, s[rows], NEG_INF), s[rows.stop:]]
                    s = jnp.concatenate([p for p in parts if p.shape[0]], axis=0)
                scores[step] = s.astype(BF16)
            done = step - PIPE_DEPTH
            if done >= 0:
                (_, vt_ref, kt, _, (m_refs, acc_refs)), chain = jobs[done]
                g = chain // chains_g
                keys = key_plan(jobs[done])[0]
                m_ref, acc_ref = m_refs[chain], acc_refs[chain]
                s = scores.pop(done)
                m_old = m_ref[...]
                packed = [s[r * BF16_SUBLANES:(r + 1) * BF16_SUBLANES]
                          for r in range(s.shape[0] // BF16_SUBLANES)]
                m_tile = functools.reduce(jnp.maximum, packed).astype(F32)
                m_new = jnp.maximum(m_old, jnp.max(m_tile, axis=0, keepdims=True))
                p = jnp.exp2(s - m_new.astype(BF16))
                pv = _dot(vt_ref[kt, g * VT_ROWS:(g + 1) * VT_ROWS, keys], p)
                acc_ref[...] = jnp.exp2(m_old - m_new) * acc_ref[...] + pv
                m_ref[...] = m_new

    def flash_mix(branch, state):
        _, acc_refs = state
        for h in range(N_HEADS):
            g, r = divmod(h, GROUP)
            gate = gate_row(branch, h)
            for x in range(2):
                c, c0 = divmod(q_rows(r, x).start, CHAIN_COLS)
                acc = acc_refs[g * chains_g + c][:, c0:c0 + HALF]
                qs = slice(x * HALF, (x + 1) * HALF)
                scale = gate[:, qs] * (1.0 / acc[ONE_ROW:ONE_ROW + 1, :])
                mix_ref[h, :, qs] = mix_ref[h, :, qs] + scale * acc[0:HEAD_DIM]

    flash_init(sel_state)
    flash_init(win_state)

    def sel_past(kt):
        return (ksp_ref, vst_ref, kt, None, sel_state)

    def sel_pair(j, carry):
        flash_tiles([sel_past(2 * j), sel_past(2 * j + 1)])
        return carry

    lax.fori_loop(0, i >> 1, sel_pair, 0)

    @pl.when(i & 1 == 1)
    def _():
        flash_tiles([sel_past(i - 1)])

    n_back = WINDOW // KT
    sel_diag = (ksp_ref, vst_ref, i, diag_plan, sel_state)
    win_diag = (kwp_ref, vwt_ref, i, diag_plan, win_state)

    def win_back(d):
        return (kwp_ref, vwt_ref, i - d, far_plan if d == n_back else None, win_state)

    for have in range(n_back + 1):
        cond = (i == have) if have < n_back else (i >= have)

        @pl.when(cond)
        def _(have=have):
            flash_tiles([sel_diag] + [win_back(d) for d in range(have, 0, -1)] + [win_diag])

    flash_mix(1, sel_state)
    flash_mix(2, win_state)
    for h2 in range(N_HEADS // 2):
        pair = jnp.concatenate([mix_ref[2 * h2], mix_ref[2 * h2 + 1]], axis=0)
        o_ref[:, h2 * SLOT:(h2 + 1) * SLOT] = pair.T.astype(BF16)


def _overlap(n_cmp_pad, n_slc):
    c0 = jnp.arange(n_cmp_pad) * CMP_STRIDE
    s0 = jnp.arange(n_slc) * SEL_LEN
    ov = jnp.minimum(c0[None, :] + CMP_LEN, s0[:, None] + SEL_LEN) - jnp.maximum(c0[None, :], s0[:, None])
    return jnp.clip(ov, 0).astype(F32) / CMP_LEN


def _nsa(q, gate, kc, vc, ks, vst, kw, vwt, batch, seq_len):
    n_slc = seq_len // SEL_LEN
    n_cmp_pad = kc.shape[1]
    w_kv_pad = N_KV * SLOT
    assert n_slc <= SLOT - BLK_LANE0 and n_slc % 8 == 0
    assert CHAIN_COLS % HALF == 0 and (GROUP * HALF) % CHAIN_COLS == 0 and KT == 2 * HALF
    ov = _overlap(n_cmp_pad, n_slc)
    nq = seq_len // TQ
    n_kt = seq_len // KT
    qrow = lambda b, i: (b * nq + i, 0)
    per_b = lambda b, i: (b, 0, 0)
    fixed = lambda b, i: (0, 0)
    blocks = (_nbytes((TQ, Q_WIDTH), BF16) + _nbytes((TQ, SLOT), F32)
              + 2 * _nbytes((n_cmp_pad, w_kv_pad), BF16) + 2 * _nbytes((seq_len, KV_WIDTH), BF16)
              + 2 * _nbytes((n_kt, VT_SLAB, KT), BF16)
              + _nbytes(ov.shape, F32) + _nbytes((TQ, Q_WIDTH), BF16))
    n_chains = N_KV * GROUP * TQ // CHAIN_COLS
    run_max = [pltpu.VMEM((1, CHAIN_COLS), F32)] * n_chains
    run_acc = [pltpu.VMEM((VT_ROWS, CHAIN_COLS), F32)] * n_chains
    scratch_shapes = (
        [pltpu.VMEM((seq_len, w_kv_pad), BF16),
         pltpu.VMEM((seq_len, w_kv_pad), BF16),
         pltpu.VMEM((N_KV, GROUP * TQ, SLOT), BF16),
         pltpu.VMEM((N_HEADS, HEAD_DIM, TQ), F32)]
        + run_max + run_acc + run_max + run_acc)
    scratch = sum(_nbytes(s.shape, s.dtype) for s in scratch_shapes)
    return pl.pallas_call(
        functools.partial(_nsa_kernel, n_slc=n_slc),
        grid=(batch, nq),
        in_specs=[
            pl.BlockSpec((TQ, Q_WIDTH), qrow),
            pl.BlockSpec((TQ, SLOT), qrow),
            pl.BlockSpec((1, n_cmp_pad, w_kv_pad), per_b),
            pl.BlockSpec((1, n_cmp_pad, w_kv_pad), per_b),
            pl.BlockSpec((1, seq_len, KV_WIDTH), per_b),
            pl.BlockSpec((n_kt, VT_SLAB, KT), per_b),
            pl.BlockSpec((1, seq_len, KV_WIDTH), per_b),
            pl.BlockSpec((n_kt, VT_SLAB, KT), per_b),
            pl.BlockSpec(ov.shape, fixed),
        ],
        out_specs=pl.BlockSpec((TQ, Q_WIDTH), qrow),
        out_shape=jax.ShapeDtypeStruct((batch * seq_len, Q_WIDTH), BF16),
        scratch_shapes=scratch_shapes,
        compiler_params=pltpu.CompilerParams(
            dimension_semantics=("arbitrary", "arbitrary"),
            vmem_limit_bytes=_vmem_limit(blocks, scratch)),
        name="nsa",
    )(q, gate, kc.reshape(batch, n_cmp_pad, w_kv_pad), vc.reshape(batch, n_cmp_pad, w_kv_pad),
      ks.reshape(batch, seq_len, KV_WIDTH), vst, kw.reshape(batch, seq_len, KV_WIDTH), vwt, ov)


def _pool_kernel(u_ref, w_ref, scale_ref, o_ref):
    seq = u_ref.shape[0]
    t = lax.broadcasted_iota(jnp.int32, (seq, POOL_GROUP), 0)

    def shifted(x, k):
        return jnp.where(t >= k, pltpu.roll(x, k, axis=0), 0.0)

    for gi, w in enumerate(POOL_WINDOWS):
        sl = slice(gi * POOL_GROUP, (gi + 1) * POOL_GROUP)
        x = u_ref[:, sl]
        wsum = x
        span = 1
        while span < w:
            wsum = wsum + shifted(wsum, span)
            span *= 2
        cnt = jnp.minimum(t + 1, w).astype(F32)
        pooled = (wsum / cnt - x).astype(BF16)
        o_ref[:, sl] = (_dot(pooled, w_ref[gi]) * scale_ref[:, sl]).astype(BF16)


def _pool(pool_in, pool_w, pool_scale, batch, seq_len):
    for w in POOL_WINDOWS:
        assert w & (w - 1) == 0
    blocks = (_nbytes((seq_len, POOL_WIDTH), F32) + _nbytes(pool_w.shape, BF16)
              + _nbytes((seq_len, POOL_WIDTH), BF16))
    return pl.pallas_call(
        _pool_kernel,
        grid=(batch,),
        in_specs=[
            pl.BlockSpec((seq_len, POOL_WIDTH), lambda b: (b, 0)),
            pl.BlockSpec(pool_w.shape, lambda b: (0, 0, 0)),
            pl.BlockSpec((1, POOL_WIDTH), lambda b: (0, 0)),
        ],
        out_specs=pl.BlockSpec((seq_len, POOL_WIDTH), lambda b: (b, 0)),
        out_shape=jax.ShapeDtypeStruct((batch * seq_len, POOL_WIDTH), BF16),
        compiler_params=pltpu.CompilerParams(
            dimension_semantics=("arbitrary",), vmem_limit_bytes=_vmem_limit(blocks)),
        name="pool",
    )(pool_in, pool_w.astype(BF16), pool_scale.reshape(1, -1))


def _merge_kernel(x_ref, gpre_ref, wt_ref, nsa_ref, wa_ref, pool_ref, wp_ref, wo_ref, gpost_ref,
                  o_ref):
    x = x_ref[...]
    hn = _rms(x, gpre_ref[...])
    g_attn = jax.nn.sigmoid(_dot_nt(hn, wt_ref[OFF_MERGE:OFF_MERGE + D_MODEL, :]))
    g_pool = jax.nn.sigmoid(_dot_nt(hn, wt_ref[OFF_MERGE + D_MODEL:OFF_MERGE + 2 * D_MODEL, :]))
    y = (g_attn * _dot(nsa_ref[...].astype(F32), wa_ref[...])
         + g_pool * _dot(pool_ref[...].astype(F32), wp_ref[...]))
    h = _dot(y, wo_ref[...])
    o_ref[...] = x + _rms(h, gpost_ref[...])


def _merge(x1, g_pre, w_in_t, o_nsa, w_attn, o_pool, w_pool, w_out, g_post):
    t = x1.shape[0]
    row = lambda i: (i, 0)
    fixed = lambda i: (0, 0)
    once = pl.Buffered(1)
    weights = (_nbytes(w_in_t.shape, F32) + _nbytes((Q_WIDTH, D_MODEL), F32)
               + _nbytes((POOL_WIDTH, D_MODEL), F32) + _nbytes((D_MODEL, D_MODEL), F32))
    blocks = (2 * _nbytes((TOK_TILE, D_MODEL), F32) + _nbytes((TOK_TILE, Q_WIDTH), BF16)
              + _nbytes((TOK_TILE, POOL_WIDTH), BF16))
    return pl.pallas_call(
        _merge_kernel,
        grid=(t // TOK_TILE,),
        in_specs=[
            pl.BlockSpec((TOK_TILE, D_MODEL), row),
            pl.BlockSpec((1, D_MODEL), fixed),
            pl.BlockSpec(w_in_t.shape, fixed, pipeline_mode=once),
            pl.BlockSpec((TOK_TILE, Q_WIDTH), row),
            pl.BlockSpec((Q_WIDTH, D_MODEL), fixed, pipeline_mode=once),
            pl.BlockSpec((TOK_TILE, POOL_WIDTH), row),
            pl.BlockSpec((POOL_WIDTH, D_MODEL), fixed, pipeline_mode=once),
            pl.BlockSpec((D_MODEL, D_MODEL), fixed, pipeline_mode=once),
            pl.BlockSpec((1, D_MODEL), fixed),
        ],
        out_specs=pl.BlockSpec((TOK_TILE, D_MODEL), row),
        out_shape=jax.ShapeDtypeStruct((t, D_MODEL), F32),
        compiler_params=pltpu.CompilerParams(
            dimension_semantics=("arbitrary",), vmem_limit_bytes=_vmem_limit(blocks, weights)),
        name="merge",
    )(x1, g_pre.reshape(1, -1), w_in_t, o_nsa, w_attn, o_pool, w_pool, w_out, g_post.reshape(1, -1))


def _layer(x, tab, tab_cmp, batch, seq_len, g_ffn1_pre, w_ffn1_gate, w_ffn1_up, w_ffn1_down,
           g_ffn1_post, g_mix_pre, w_in, cmp_pe_k, cmp_w1_k, cmp_w2_k, cmp_pe_v, cmp_w1_v, cmp_w2_v,
           w_attn_branch, pool_w, pool_scale, w_pool_branch, w_out, g_mix_post, g_ffn2_pre,
           w_ffn2_gate, w_ffn2_up, w_ffn2_down, g_ffn2_post):
    x1 = _ffn(x, g_ffn1_pre, w_ffn1_gate, w_ffn1_up, w_ffn1_down, g_ffn1_post)
    w_in_t = w_in.T
    q, ks, kw, vst, vwt, cmp_rows, gate, pool_in = _in_proj(x1, g_mix_pre, w_in_t, tab)
    kc, vc = _compress(cmp_rows, cmp_pe_k, cmp_w1_k, cmp_w2_k, cmp_pe_v, cmp_w1_v, cmp_w2_v, tab_cmp)
    n_cmp_pad = seq_len // CMP_STRIDE
    kc = kc.reshape(batch, n_cmp_pad, N_KV * SLOT)
    vc = vc.reshape(batch, n_cmp_pad, N_KV * SLOT)
    o_nsa = _nsa(q, gate, kc, vc, ks, vst, kw, vwt, batch, seq_len)
    o_pool = _pool(pool_in, pool_w, pool_scale, batch, seq_len)
    x2 = _merge(x1, g_mix_pre, w_in_t, o_nsa, w_attn_branch, o_pool, w_pool_branch, w_out,
                g_mix_post)
    return _ffn(x2, g_ffn2_pre, w_ffn2_gate, w_ffn2_up, w_ffn2_down, g_ffn2_post)


def kernel(x, positions, g_ffn1_pre, w_ffn1_gate, w_ffn1_up, w_ffn1_down, g_ffn1_post, g_mix_pre, w_in, cmp_pe_k, cmp_w1_k, cmp_w2_k, cmp_pe_v, cmp_w1_v, cmp_w2_v, w_attn_branch, pool_w, pool_scale, w_pool_branch, w_out, g_mix_post, g_ffn2_pre, w_ffn2_gate, w_ffn2_up, w_ffn2_down, g_ffn2_post):
    batch, seq_len, d_model = x.shape
    assert d_model == D_MODEL and seq_len % TOK_TILE == 0 and seq_len % TQ == 0
    assert TQ == KT and TQ % SEL_LEN == 0 and TOK_TILE % KT == 0 and WINDOW % KT == 0
    assert seq_len % CMP_STRIDE == 0 and CMP_LEN == 2 * CMP_STRIDE
    t = batch * seq_len
    tab = _rope_tables(positions.reshape(t), seq_len)
    n_cmp = (seq_len - CMP_LEN) // CMP_STRIDE + 1
    n_cmp_pad = seq_len // CMP_STRIDE
    pos_cmp = jnp.pad(positions[:, CMP_LEN - 1::CMP_STRIDE], ((0, 0), (0, n_cmp_pad - n_cmp)))
    tab_cmp = _rope_tables(pos_cmp.reshape(batch * n_cmp_pad), batch * n_cmp_pad)
    xf = x.reshape(t, D_MODEL)
    per_layer = (g_ffn1_pre, w_ffn1_gate, w_ffn1_up, w_ffn1_down, g_ffn1_post, g_mix_pre, w_in,
                 cmp_pe_k, cmp_w1_k, cmp_w2_k, cmp_pe_v, cmp_w1_v, cmp_w2_v, w_attn_branch, pool_w,
                 pool_scale, w_pool_branch, w_out, g_mix_post, g_ffn2_pre, w_ffn2_gate, w_ffn2_up,
                 w_ffn2_down, g_ffn2_post)
    for l in range(g_ffn1_pre.shape[0]):
        xf = _layer(xf, tab, tab_cmp, batch, seq_len, *(p[l] for p in per_layer))
    return xf.reshape(batch, seq_len, D_MODEL)
```

```python
import functools
import math

import jax
import jax.numpy as jnp
from jax import lax
from jax.experimental import pallas as pl
from jax.experimental.pallas import tpu as pltpu

F32 = jnp.float32
BF16 = jnp.bfloat16

D_MODEL = 1024
N_HEADS = 16
HEAD_DIM = 64
N_KV = 4
GROUP = N_HEADS // N_KV
ROT_DIM = HEAD_DIM // 4
ROT_HALF = ROT_DIM // 2
ROPE_THETA = 500000.0
CMP_LEN = 32
CMP_STRIDE = 16
CMP_HIDDEN = 2 * HEAD_DIM
SEL_LEN = 64
SEL_SHIFT = SEL_LEN.bit_length() - 1
N_SEL = 8
WINDOW = 512
POOL_WINDOWS = (2, 4, 8, 16)
POOL_WIDTH = D_MODEL // 2
POOL_GROUP = POOL_WIDTH // len(POOL_WINDOWS)
D_FF = 2816
EPS = 1e-6
NEG_INF = -1e30
FORCE_SCORE = 1e4
Q_WIDTH = N_HEADS * HEAD_DIM
KV_WIDTH = N_KV * HEAD_DIM
N_GATES = 3 * N_HEADS
LOG2_E = math.log2(math.e)

LANES = 128
BF16_SUBLANES = 16
V7X_VMEM_BYTES = 64 * 1024 * 1024

SLOT = LANES
BLK_LANE0 = HEAD_DIM
VT_ROWS = HEAD_DIM + BF16_SUBLANES
ONE_ROW = HEAD_DIM
TOK_TILE = 512
TQ = 256
KT = 256
HALF = TQ // 2
CHAIN_COLS = 256
PIPE_DEPTH = 10
FF_CHUNK = 256
IN_SPLIT = 2
FFN_SPLIT = 2
N_ROPE_TAB = 2


def _vmem_limit(block_bytes, scratch_bytes=0):
    need = 2 * block_bytes + scratch_bytes
    return int(min(V7X_VMEM_BYTES - (4 << 20), max(2 * need, 32 << 20)))


def _nbytes(shape, dtype):
    n = 1
    for s in shape:
        n *= s
    return n * jnp.dtype(dtype).itemsize


def _rms(xf, g):
    return xf * lax.rsqrt(jnp.mean(xf * xf, axis=-1, keepdims=True) + EPS) * g


def _dot(a, b):
    return jnp.dot(a, b, preferred_element_type=F32)


def _dot_nt(a, b):
    return lax.dot_general(a, b, (((1,), (1,)), ((), ())), preferred_element_type=F32)


def _rope_trig_kernel(pos_ref, inv_ref, tab_ref):
    tn = pos_ref.shape[1]
    ang = pos_ref[...].astype(F32) * inv_ref[...]
    c, s = jnp.cos(ang), jnp.sin(ang)
    rest = HEAD_DIM - ROT_DIM
    reps = SLOT // HEAD_DIM
    cos_rows = jnp.concatenate([c, c, jnp.ones((rest, tn), F32)] * reps, axis=0)
    sin_rows = jnp.concatenate([-s, s, jnp.zeros((rest, tn), F32)] * reps, axis=0)
    for j in range(tn // LANES):
        cs = slice(j * LANES, (j + 1) * LANES)
        tab_ref[cs, 0:SLOT] = cos_rows[:, cs].T
        tab_ref[cs, SLOT:2 * SLOT] = sin_rows[:, cs].T


def _rope_tables(pos_flat, tile):
    n = pos_flat.shape[0]
    inv = ROPE_THETA ** (-jnp.arange(ROT_HALF, dtype=F32) * (2.0 / ROT_DIM))
    return pl.pallas_call(
        _rope_trig_kernel,
        grid=(n // tile,),
        in_specs=[pl.BlockSpec((1, tile), lambda i: (0, i)),
                  pl.BlockSpec((ROT_HALF, 1), lambda i: (0, 0))],
        out_specs=pl.BlockSpec((tile, N_ROPE_TAB * SLOT), lambda i: (i, 0)),
        out_shape=jax.ShapeDtypeStruct((n, N_ROPE_TAB * SLOT), F32),
        compiler_params=pltpu.CompilerParams(dimension_semantics=("arbitrary",)),
        name="rope_trig",
    )(pos_flat.reshape(1, n), inv.reshape(ROT_HALF, 1))


def _rope_slot(y, tab):
    cos_t = tab[:, 0:SLOT]
    sin_t = tab[:, SLOT:2 * SLOT]
    lane = lax.broadcasted_iota(jnp.int32, y.shape, 1) & (HEAD_DIM - 1)
    up = pltpu.roll(y, SLOT - ROT_HALF, axis=1)
    down = pltpu.roll(y, ROT_HALF, axis=1)
    return y * cos_t + jnp.where(lane < ROT_HALF, up, down) * sin_t


def _ffn_kernel(x_ref, gpre_ref, wg_ref, wu_ref, wd_ref, gpost_ref, o_ref):
    rows = x_ref.shape[0] // FFN_SPLIT
    parts = [slice(k * rows, (k + 1) * rows) for k in range(FFN_SPLIT)]
    hn = [_rms(x_ref[rs, :], gpre_ref[...]) for rs in parts]
    for k, rs in enumerate(parts):
        for c in range(D_FF // FF_CHUNK):
            sl = slice(c * FF_CHUNK, (c + 1) * FF_CHUNK)
            g = _dot(hn[k], wg_ref[:, sl])
            u = _dot(hn[k], wu_ref[:, sl])
            d = _dot(g * jax.nn.sigmoid(g) * u, wd_ref[sl, :])
            if c == 0:
                o_ref[rs, :] = d
            else:
                o_ref[rs, :] += d
        o_ref[rs, :] = x_ref[rs, :] + 0.5 * _rms(o_ref[rs, :], gpost_ref[...])


def _ffn(x, g_pre, w_gate, w_up, w_down, g_post):
    t = x.shape[0]
    tile = FFN_SPLIT * TOK_TILE
    row = lambda i: (i, 0)
    fixed = lambda i: (0, 0)
    once = pl.Buffered(1)
    weights = 3 * _nbytes((D_MODEL, D_FF), F32)
    blocks = 2 * _nbytes((tile, D_MODEL), F32)
    return pl.pallas_call(
        _ffn_kernel,
        grid=(t // tile,),
        in_specs=[
            pl.BlockSpec((tile, D_MODEL), row),
            pl.BlockSpec((1, D_MODEL), fixed),
            pl.BlockSpec((D_MODEL, D_FF), fixed, pipeline_mode=once),
            pl.BlockSpec((D_MODEL, D_FF), fixed, pipeline_mode=once),
            pl.BlockSpec((D_FF, D_MODEL), fixed, pipeline_mode=once),
            pl.BlockSpec((1, D_MODEL), fixed),
        ],
        out_specs=pl.BlockSpec((tile, D_MODEL), row),
        out_shape=jax.ShapeDtypeStruct((t, D_MODEL), F32),
        compiler_params=pltpu.CompilerParams(
            dimension_semantics=("arbitrary",), vmem_limit_bytes=_vmem_limit(blocks, weights)),
        name="ffn",
    )(x, g_pre.reshape(1, -1), w_gate, w_up, w_down, g_post.reshape(1, -1))


OFF_Q = 0
OFF_CMP = OFF_Q + Q_WIDTH
OFF_KS = OFF_CMP + 2 * KV_WIDTH
OFF_VS = OFF_KS + KV_WIDTH
OFF_KW = OFF_VS + KV_WIDTH
OFF_VW = OFF_KW + KV_WIDTH
OFF_GATE = OFF_VW + KV_WIDTH
OFF_POOL = OFF_GATE + N_GATES
OFF_MERGE = OFF_POOL + POOL_WIDTH
W_CMP = 2 * KV_WIDTH
CHUNK_TOK = CMP_STRIDE
VT_SLAB = N_KV * VT_ROWS


def _in_proj_kernel(x_ref, g_ref, w_ref, tab_ref, q_ref, ks_ref, kw_ref, vst_ref,
                    vwt_ref, cmp_ref, gate_ref, pool_ref, cmp_scr):
    rows = x_ref.shape[0] // IN_SPLIT
    wide = 4 * SLOT
    aux = (lax.broadcasted_iota(jnp.int32, (VT_ROWS - HEAD_DIM, KT), 0) == 0).astype(BF16)

    def part(k):
        rs = slice(k * rows, (k + 1) * rows)
        hn = _rms(x_ref[rs, :], g_ref[...])
        tab = tab_ref[rs, :]
        tab_q = tab * (HEAD_DIM ** -0.5 * LOG2_E)
        for c in range(Q_WIDTH // wide):
            y = _dot_nt(hn, w_ref[OFF_Q + c * wide:OFF_Q + (c + 1) * wide, :])
            for j in range(wide // SLOT):
                sl = slice(c * wide + j * SLOT, c * wide + (j + 1) * SLOT)
                q_ref[rs, sl] = _rope_slot(y[:, j * SLOT:(j + 1) * SLOT], tab_q).astype(BF16)
        for k_ref, off in ((ks_ref, OFF_KS), (kw_ref, OFF_KW)):
            y = _dot_nt(hn, w_ref[off:off + KV_WIDTH, :])
            for j in range(KV_WIDTH // SLOT):
                sl = slice(j * SLOT, (j + 1) * SLOT)
                k_ref[rs, sl] = _rope_slot(y[:, sl], tab).astype(BF16)
        y = _dot_nt(hn, w_ref[OFF_CMP:OFF_CMP + W_CMP, :])
        for j in range(W_CMP // LANES):
            cmp_scr[j, rs, :] = y[:, j * LANES:(j + 1) * LANES]
        chunk_rows = rows // CHUNK_TOK
        for l in range(CHUNK_TOK):
            for j in range(W_CMP // LANES):
                c0 = l * W_CMP + j * LANES
                cmp_ref[k * chunk_rows:(k + 1) * chunk_rows, c0:c0 + LANES] = (
                    cmp_scr[j, pl.ds(k * rows + l, chunk_rows, stride=CHUNK_TOK), :].astype(BF16))
        gate_ref[rs, :] = _dot_nt(hn, w_ref[OFF_GATE:OFF_GATE + SLOT, :])
        pool_ref[rs, :] = _dot_nt(hn, w_ref[OFF_POOL:OFF_MERGE, :])
        tiles = rows // KT
        for vt_ref, off in ((vst_ref, OFF_VS), (vwt_ref, OFF_VW)):
            vt = _dot_nt(w_ref[off:off + KV_WIDTH, :], hn).astype(BF16)
            for j in range(tiles):
                for g in range(N_KV):
                    vt_ref[k * tiles + j, g * VT_ROWS:g * VT_ROWS + HEAD_DIM, :] = (
                        vt[g * HEAD_DIM:(g + 1) * HEAD_DIM, j * KT:(j + 1) * KT])
                    vt_ref[k * tiles + j, g * VT_ROWS + HEAD_DIM:(g + 1) * VT_ROWS, :] = aux

    for k in range(IN_SPLIT):
        part(k)


def _in_proj(x1, g_pre, w_in_t, tab):
    t = x1.shape[0]
    row = lambda i: (i, 0)
    fixed = lambda i: (0, 0)
    once = pl.Buffered(1)
    tile = IN_SPLIT * TOK_TILE
    tiles = tile // KT
    chunk_rows = tile // CHUNK_TOK
    slab = pl.BlockSpec((tiles, VT_SLAB, KT), lambda i: (i, 0, 0))
    slab_shape = jax.ShapeDtypeStruct((t // KT, VT_SLAB, KT), BF16)
    out_specs = [
        pl.BlockSpec((tile, Q_WIDTH), row), pl.BlockSpec((tile, KV_WIDTH), row),
        pl.BlockSpec((tile, KV_WIDTH), row), slab, slab,
        pl.BlockSpec((chunk_rows, CHUNK_TOK * W_CMP), row),
        pl.BlockSpec((tile, SLOT), row), pl.BlockSpec((tile, POOL_WIDTH), row)]
    out_shape = [
        jax.ShapeDtypeStruct((t, Q_WIDTH), BF16), jax.ShapeDtypeStruct((t, KV_WIDTH), BF16),
        jax.ShapeDtypeStruct((t, KV_WIDTH), BF16), slab_shape, slab_shape,
        jax.ShapeDtypeStruct((t // CHUNK_TOK, CHUNK_TOK * W_CMP), BF16),
        jax.ShapeDtypeStruct((t, SLOT), F32), jax.ShapeDtypeStruct((t, POOL_WIDTH), F32)]
    weights = _nbytes(w_in_t.shape, F32)
    blocks = (_nbytes((tile, D_MODEL), F32) + _nbytes((tile, N_ROPE_TAB * SLOT), F32)
              + _nbytes((tile, Q_WIDTH + 2 * KV_WIDTH + W_CMP), BF16)
              + 2 * _nbytes((tiles, VT_SLAB, KT), BF16) + _nbytes((tile, SLOT + POOL_WIDTH), F32))
    res = pl.pallas_call(
        _in_proj_kernel,
        grid=(t // tile,),
        in_specs=[
            pl.BlockSpec((tile, D_MODEL), row),
            pl.BlockSpec((1, D_MODEL), fixed),
            pl.BlockSpec(w_in_t.shape, fixed, pipeline_mode=once),
            pl.BlockSpec((tile, N_ROPE_TAB * SLOT), row),
        ],
        out_specs=out_specs,
        out_shape=out_shape,
        scratch_shapes=[pltpu.VMEM((W_CMP // LANES, tile, LANES), F32)],
        compiler_params=pltpu.CompilerParams(
            dimension_semantics=("arbitrary",),
            vmem_limit_bytes=_vmem_limit(blocks, weights + _nbytes((tile, W_CMP), F32))),
        name="in_proj",
    )(x1, g_pre.reshape(1, -1), w_in_t, tab)
    return res


def _compress_kernel(x_ref, w1k_ref, w1v_ref, pek_ref, pev_ref, w1k_raw_ref, w1v_raw_ref,
                     w2k_ref, w2v_ref, tab_ref, kc_ref, vc_ref, acck_ref, accv_ref):
    l = pl.program_id(0)

    @pl.when(l == 0)
    def _():
        acck_ref[...] = jnp.zeros(acck_ref.shape, F32)
        accv_ref[...] = jnp.zeros(accv_ref.shape, F32)

    for acc_ref, w1_ref, c0 in ((acck_ref, w1k_ref, 0), (accv_ref, w1v_ref, KV_WIDTH)):
        for g in range(N_KV):
            acc_ref[g] += _dot(x_ref[:, c0 + g * HEAD_DIM:c0 + (g + 1) * HEAD_DIM], w1_ref[0])

    @pl.when(l == CHUNK_TOK - 1)
    def _():
        rows = acck_ref.shape[1]
        tab = tab_ref[...]
        for acc_ref, pe_ref, w1_raw_ref, w2_ref, o_ref, rope in (
                (acck_ref, pek_ref, w1k_raw_ref, w2k_ref, kc_ref, True),
                (accv_ref, pev_ref, w1v_raw_ref, w2v_ref, vc_ref, False)):
            bias = _dot(pe_ref[...], w1_raw_ref[...])[0:1, :]
            for g in range(N_KV):
                first = acc_ref[g, :, 0:CMP_HIDDEN]
                second = pltpu.roll(acc_ref[g, :, CMP_HIDDEN:2 * CMP_HIDDEN], rows - 1, axis=0)
                hid = jax.nn.gelu(first + second + bias).astype(BF16)
                out = _dot(hid, w2_ref[...])
                if rope:
                    out = _rope_slot(out, tab)
                o_ref[:, g * SLOT:(g + 1) * SLOT] = out.astype(BF16)


def _compress_weights(w1, w2):
    w1 = w1.reshape(2, CHUNK_TOK, HEAD_DIM, CMP_HIDDEN)
    w1_tok = jnp.concatenate([w1[0], w1[1]], axis=-1).astype(BF16)
    return w1_tok, jnp.pad(w2, ((0, 0), (0, SLOT - HEAD_DIM))).astype(BF16)


def _compress(cmp_rows, cmp_pe_k, cmp_w1_k, cmp_w2_k, cmp_pe_v, cmp_w1_v, cmp_w2_v, tab_cmp):
    rows = cmp_rows.shape[0]
    w1k, w2k = _compress_weights(cmp_w1_k, cmp_w2_k)
    w1v, w2v = _compress_weights(cmp_w1_v, cmp_w2_v)
    sub = 8
    pek = jnp.broadcast_to(cmp_pe_k.reshape(1, -1), (sub, CMP_LEN * HEAD_DIM)).astype(BF16)
    pev = jnp.broadcast_to(cmp_pe_v.reshape(1, -1), (sub, CMP_LEN * HEAD_DIM)).astype(BF16)
    fixed = lambda l: (0, 0)
    w_kv_pad = N_KV * SLOT
    blocks = (_nbytes((rows, W_CMP), BF16) + 2 * _nbytes((HEAD_DIM, 2 * CMP_HIDDEN), BF16)
              + 2 * _nbytes((sub + CMP_LEN * HEAD_DIM, CMP_HIDDEN), BF16)
              + 2 * _nbytes((CMP_HIDDEN, SLOT), BF16)
              + _nbytes((rows, N_ROPE_TAB * SLOT), F32) + 2 * _nbytes((rows, w_kv_pad), BF16))
    acc = pltpu.VMEM((N_KV, rows, 2 * CMP_HIDDEN), F32)
    return pl.pallas_call(
        _compress_kernel,
        grid=(CHUNK_TOK,),
        in_specs=[
            pl.BlockSpec((rows, W_CMP), lambda l: (0, l)),
            pl.BlockSpec((1, HEAD_DIM, 2 * CMP_HIDDEN), lambda l: (l, 0, 0)),
            pl.BlockSpec((1, HEAD_DIM, 2 * CMP_HIDDEN), lambda l: (l, 0, 0)),
            pl.BlockSpec((sub, CMP_LEN * HEAD_DIM), fixed),
            pl.BlockSpec((sub, CMP_LEN * HEAD_DIM), fixed),
            pl.BlockSpec((CMP_LEN * HEAD_DIM, CMP_HIDDEN), fixed),
            pl.BlockSpec((CMP_LEN * HEAD_DIM, CMP_HIDDEN), fixed),
            pl.BlockSpec((CMP_HIDDEN, SLOT), fixed),
            pl.BlockSpec((CMP_HIDDEN, SLOT), fixed),
            pl.BlockSpec((rows, N_ROPE_TAB * SLOT), fixed),
        ],
        out_specs=[pl.BlockSpec((rows, w_kv_pad), fixed)] * 2,
        out_shape=[jax.ShapeDtypeStruct((rows, w_kv_pad), BF16)] * 2,
        scratch_shapes=[acc, acc],
        compiler_params=pltpu.CompilerParams(
            dimension_semantics=("arbitrary",),
            vmem_limit_bytes=_vmem_limit(blocks, 2 * _nbytes(acc.shape, F32))),
        name="compress",
    )(cmp_rows, w1k, w1v, pek, pev, cmp_w1_k.astype(BF16), cmp_w1_v.astype(BF16), w2k, w2v, tab_cmp)


def _nsa_kernel(q_ref, gate_ref, kc_ref, vc_ref, ks_ref, vst_ref, kw_ref, vwt_ref, ov_ref,
                o_ref, ksp_ref, kwp_ref, qg_ref, mix_ref, *state_refs, n_slc):
    i = pl.program_id(1)
    start = i * TQ
    seq = ks_ref.shape[1]
    cols_g = GROUP * TQ
    chains_g = cols_g // CHAIN_COLS
    n_chains = N_KV * chains_g
    n_cmp = kc_ref.shape[1]
    sel_state = (state_refs[0:n_chains], state_refs[n_chains:2 * n_chains])
    win_state = (state_refs[2 * n_chains:3 * n_chains], state_refs[3 * n_chains:4 * n_chains])
    lane_q = lax.broadcasted_iota(jnp.int32, (TQ, SLOT), 1)
    lane_h = lax.broadcasted_iota(jnp.int32, (HALF, SLOT), 1)

    def head_slot(x_ref, rows, j):
        two = x_ref[rows, (j // 2) * SLOT:(j // 2 + 1) * SLOT]
        if j % 2:
            words = pltpu.roll(pltpu.bitcast(two, jnp.uint32), HEAD_DIM, axis=1)
            two = pltpu.bitcast(words, BF16)
        return two

    @pl.when(i == 0)
    def _():
        for c in range(seq // KT):
            rows = pl.ds(c * KT, KT)
            lane = lax.broadcasted_iota(jnp.int32, (KT, SLOT), 1)
            key_blk = (c * KT + lax.broadcasted_iota(jnp.int32, (KT, SLOT), 0)) >> SEL_SHIFT
            onehot = jnp.where(lane - BLK_LANE0 == key_blk, 1.0, 0.0).astype(BF16)
            zeros = jnp.zeros((KT, SLOT), BF16)
            for g in range(N_KV):
                gs = slice(g * SLOT, (g + 1) * SLOT)
                ksp_ref[rows, gs] = jnp.where(lane < HEAD_DIM, head_slot(ks_ref.at[0], rows, g), onehot)
                kwp_ref[rows, gs] = jnp.where(lane < HEAD_DIM, head_slot(kw_ref.at[0], rows, g), zeros)

    def q_rows(r, x):
        return slice((x * GROUP + r) * HALF, (x * GROUP + r + 1) * HALF)

    def chain_half(c):
        return c * CHAIN_COLS // (GROUP * HALF)

    for h in range(N_HEADS):
        g, r = divmod(h, GROUP)
        q_pad = jnp.where(lane_q < HEAD_DIM, head_slot(q_ref, slice(None), h), jnp.zeros((TQ, SLOT), BF16))
        for x in range(2):
            qg_ref[g, q_rows(r, x), :] = q_pad[x * HALF:(x + 1) * HALF]

    gates_t = jax.nn.sigmoid(gate_ref[...]).T
    any_cmp = start + lax.broadcasted_iota(jnp.int32, (1, TQ), 1) >= CMP_LEN - 1
    key_h = lax.broadcasted_iota(jnp.int32, (HALF, CHAIN_COLS), 0)
    qry_h = lax.broadcasted_iota(jnp.int32, (HALF, CHAIN_COLS), 1) & (HALF - 1)
    tri_le, tri_gt = key_h <= qry_h, key_h > qry_h
    diag_plan = ((slice(0, HALF), tri_le, slice(0, HALF)), (slice(0, KT), tri_le, slice(HALF, KT)))
    far_plan = ((slice(0, KT), tri_gt, slice(0, HALF)), (slice(HALF, KT), tri_gt, slice(0, HALF)))

    def gate_row(branch, h):
        c = branch * N_HEADS + h
        return gates_t[c:c + 1, :]

    sub = 8
    cmp_per_sub = sub * SEL_LEN // CMP_STRIDE

    def compressed_and_select(ng):
        n_blk, n_c = ng * sub, ng * cmp_per_sub
        t_cmp = start + lax.broadcasted_iota(jnp.int32, (n_c, TQ), 1)
        n_idx = lax.broadcasted_iota(jnp.int32, (n_c, TQ), 0)
        cmp_valid = n_idx * CMP_STRIDE + (CMP_LEN - 1) <= t_cmp
        t_row = start + lax.broadcasted_iota(jnp.int32, (n_blk, TQ), 1)
        blk = lax.broadcasted_iota(jnp.int32, (n_blk, TQ), 0)
        forced = (blk == t_row >> SEL_SHIFT) | (blk == 0)
        causal_blk = blk * SEL_LEN <= t_row
        idx8 = lax.broadcasted_iota(jnp.int32, (sub, TQ), 0)
        pad_c = jnp.zeros((n_cmp - n_c, TQ), F32)
        cmp_scores = []
        for h in range(N_HEADS):
            g, r = divmod(h, GROUP)
            q_h = jnp.concatenate([qg_ref[g, q_rows(r, 0), :], qg_ref[g, q_rows(r, 1), :]], axis=0)
            cmp_scores.append(_dot_nt(kc_ref[0, 0:n_c, g * SLOT:(g + 1) * SLOT], q_h))
        for g in range(N_KV):
            vct_g = vc_ref[0, :, g * SLOT:(g + 1) * SLOT].astype(F32).T[0:HEAD_DIM].astype(BF16)
            p_sum = jnp.zeros((n_c, TQ), F32)
            for r in range(GROUP):
                h = g * GROUP + r
                s = jnp.where(cmp_valid, cmp_scores[h], NEG_INF)
                p = jnp.exp2(s - jnp.max(s, axis=0, keepdims=True))
                inv = jnp.where(any_cmp, 1.0 / jnp.sum(p, axis=0, keepdims=True), 0.0)
                p = p * inv
                p_sum = p_sum + p
                p_all = jnp.concatenate([p, pad_c], axis=0) if ng * cmp_per_sub < n_cmp else p
                mix_ref[h] = gate_row(0, h) * _dot(vct_g, p_all.astype(BF16))
            p_sum_all = jnp.concatenate([p_sum, pad_c], axis=0) if ng * cmp_per_sub < n_cmp else p_sum
            imp = jnp.dot(ov_ref[0:n_blk, :], p_sum_all, precision=lax.Precision.HIGHEST,
                          preferred_element_type=F32)
            score = jnp.where(causal_blk, jnp.where(forced, FORCE_SCORE, imp), NEG_INF)
            rows8 = [score[k * sub:(k + 1) * sub] for k in range(ng)]
            ranks = [jnp.zeros((sub, TQ), jnp.int32) for _ in rows8]
            for c in range(n_blk):
                other = score[c:c + 1, :]
                for k, mine in enumerate(rows8):
                    if k * sub > c:
                        beats = other >= mine
                    elif (k + 1) * sub - 1 <= c:
                        beats = other > mine
                    else:
                        beats = (other > mine) | ((other == mine) & (idx8 > c - k * sub))
                    ranks[k] = ranks[k] + beats.astype(jnp.int32)
            rank = jnp.concatenate(ranks, axis=0) if ng > 1 else ranks[0]
            bias = jnp.where(causal_blk & (rank < N_SEL), 0.0, NEG_INF)
            bias_t = jnp.concatenate([jnp.zeros((BLK_LANE0, TQ), F32), bias,
                                      jnp.zeros((SLOT - BLK_LANE0 - n_blk, TQ), F32)], axis=0)
            bias_q = bias_t.T.astype(BF16)
            for r in range(GROUP):
                for x in range(2):
                    rs = q_rows(r, x)
                    qg_ref[g, rs, :] = jnp.where(lane_h < HEAD_DIM, qg_ref[g, rs, :],
                                                 bias_q[x * HALF:(x + 1) * HALF])

    ng_max = n_slc // sub
    ng_now = ((i + 1) * TQ + sub * SEL_LEN - 1) // (sub * SEL_LEN)
    for ng in range(1, ng_max + 1):
        pl.when(ng_now == ng)(functools.partial(compressed_and_select, ng))

    def flash_init(state):
        for m_ref, acc_ref in zip(*state):
            m_ref[...] = jnp.full(m_ref.shape, NEG_INF, F32)
            acc_ref[...] = jnp.zeros(acc_ref.shape, F32)

    def flash_tiles(tiles):
        jobs = [(tile, c) for tile in tiles for c in range(n_chains)]

        def key_plan(job):
            (_, _, _, plan, _), chain = job
            return (slice(0, KT), None, None) if plan is None else plan[chain_half(chain % chains_g)]

        scores = {}
        for step in range(len(jobs) + PIPE_DEPTH):
            if step < len(jobs):
                (kp_ref, _, kt, _, _), chain = jobs[step]
                g, c = divmod(chain, chains_g)
                keys, mask, rows = key_plan(jobs[step])
                off = pl.multiple_of(kt * KT + keys.start, HALF)
                s = _dot_nt(kp_ref[pl.ds(off, keys.stop - keys.start), g * SLOT:(g + 1) * SLOT],
                            qg_ref[g, c * CHAIN_COLS:(c + 1) * CHAIN_COLS, :])
                if mask is not None:
                    parts = [s[0:rows.start], jnp.where(mask, s[rows], NEG_INF), s[rows.stop:]]
                    s = jnp.concatenate([p for p in parts if p.shape[0]], axis=0)
                scores[step] = s.astype(BF16)
            done = step - PIPE_DEPTH
            if done >= 0:
                (_, vt_ref, kt, _, (m_refs, acc_refs)), chain = jobs[done]
                g = chain // chains_g
                keys = key_plan(jobs[done])[0]
                m_ref, acc_ref = m_refs[chain], acc_refs[chain]
                s = scores.pop(done)
                m_old = m_ref[...]
                packed = [s[r * BF16_SUBLANES:(r + 1) * BF16_SUBLANES]
                          for r in range(s.shape[0] // BF16_SUBLANES)]
                m_tile = functools.reduce(jnp.maximum, packed).astype(F32)
                m_new = jnp.maximum(m_old, jnp.max(m_tile, axis=0, keepdims=True))
                p = jnp.exp2(s - m_new.astype(BF16))
                pv = _dot(vt_ref[kt, g * VT_ROWS:(g + 1) * VT_ROWS, keys], p)
                acc_ref[...] = jnp.exp2(m_old - m_new) * acc_ref[...] + pv
                m_ref[...] = m_new

    def flash_mix(branch, state):
        _, acc_refs = state
        for h in range(N_HEADS):
            g, r = divmod(h, GROUP)
            gate = gate_row(branch, h)
            for x in range(2):
                c, c0 = divmod(q_rows(r, x).start, CHAIN_COLS)
                acc = acc_refs[g * chains_g + c][:, c0:c0 + HALF]
                qs = slice(x * HALF, (x + 1) * HALF)
                scale = gate[:, qs] * (1.0 / acc[ONE_ROW:ONE_ROW + 1, :])
                mix_ref[h, :, qs] = mix_ref[h, :, qs] + scale * acc[0:HEAD_DIM]

    flash_init(sel_state)
    flash_init(win_state)

    def sel_past(kt):
        return (ksp_ref, vst_ref, kt, None, sel_state)

    def sel_pair(j, carry):
        flash_tiles([sel_past(2 * j), sel_past(2 * j + 1)])
        return carry

    lax.fori_loop(0, i >> 1, sel_pair, 0)

    @pl.when(i & 1 == 1)
    def _():
        flash_tiles([sel_past(i - 1)])

    n_back = WINDOW // KT
    sel_diag = (ksp_ref, vst_ref, i, diag_plan, sel_state)
    win_diag = (kwp_ref, vwt_ref, i, diag_plan, win_state)

    def win_back(d):
        return (kwp_ref, vwt_ref, i - d, far_plan if d == n_back else None, win_state)

    for have in range(n_back + 1):
        cond = (i == have) if have < n_back else (i >= have)

        @pl.when(cond)
        def _(have=have):
            flash_tiles([sel_diag] + [win_back(d) for d in range(have, 0, -1)] + [win_diag])

    flash_mix(1, sel_state)
    flash_mix(2, win_state)
    for h2 in range(N_HEADS // 2):
        pair = jnp.concatenate([mix_ref[2 * h2], mix_ref[2 * h2 + 1]], axis=0)
        o_ref[:, h2 * SLOT:(h2 + 1) * SLOT] = pair.T.astype(BF16)


def _overlap(n_cmp_pad, n_slc):
    c0 = jnp.arange(n_cmp_pad) * CMP_STRIDE
    s0 = jnp.arange(n_slc) * SEL_LEN
    ov = jnp.minimum(c0[None, :] + CMP_LEN, s0[:, None] + SEL_LEN) - jnp.maximum(c0[None, :], s0[:, None])
    return jnp.clip(ov, 0).astype(F32) / CMP_LEN


def _nsa(q, gate, kc, vc, ks, vst, kw, vwt, batch, seq_len):
    n_slc = seq_len // SEL_LEN
    n_cmp_pad = kc.shape[1]
    w_kv_pad = N_KV * SLOT
    assert n_slc <= SLOT - BLK_LANE0 and n_slc % 8 == 0
    assert CHAIN_COLS % HALF == 0 and (GROUP * HALF) % CHAIN_COLS == 0 and KT == 2 * HALF
    ov = _overlap(n_cmp_pad, n_slc)
    nq = seq_len // TQ
    n_kt = seq_len // KT
    qrow = lambda b, i: (b * nq + i, 0)
    per_b = lambda b, i: (b, 0, 0)
    fixed = lambda b, i: (0, 0)
    blocks = (_nbytes((TQ, Q_WIDTH), BF16) + _nbytes((TQ, SLOT), F32)
              + 2 * _nbytes((n_cmp_pad, w_kv_pad), BF16) + 2 * _nbytes((seq_len, KV_WIDTH), BF16)
              + 2 * _nbytes((n_kt, VT_SLAB, KT), BF16)
              + _nbytes(ov.shape, F32) + _nbytes((TQ, Q_WIDTH), BF16))
    n_chains = N_KV * GROUP * TQ // CHAIN_COLS
    run_max = [pltpu.VMEM((1, CHAIN_COLS), F32)] * n_chains
    run_acc = [pltpu.VMEM((VT_ROWS, CHAIN_COLS), F32)] * n_chains
    scratch_shapes = (
        [pltpu.VMEM((seq_len, w_kv_pad), BF16),
         pltpu.VMEM((seq_len, w_kv_pad), BF16),
         pltpu.VMEM((N_KV, GROUP * TQ, SLOT), BF16),
         pltpu.VMEM((N_HEADS, HEAD_DIM, TQ), F32)]
        + run_max + run_acc + run_max + run_acc)
    scratch = sum(_nbytes(s.shape, s.dtype) for s in scratch_shapes)
    return pl.pallas_call(
        functools.partial(_nsa_kernel, n_slc=n_slc),
        grid=(batch, nq),
        in_specs=[
            pl.BlockSpec((TQ, Q_WIDTH), qrow),
            pl.BlockSpec((TQ, SLOT), qrow),
            pl.BlockSpec((1, n_cmp_pad, w_kv_pad), per_b),
            pl.BlockSpec((1, n_cmp_pad, w_kv_pad), per_b),
            pl.BlockSpec((1, seq_len, KV_WIDTH), per_b),
            pl.BlockSpec((n_kt, VT_SLAB, KT), per_b),
            pl.BlockSpec((1, seq_len, KV_WIDTH), per_b),
            pl.BlockSpec((n_kt, VT_SLAB, KT), per_b),
            pl.BlockSpec(ov.shape, fixed),
        ],
        out_specs=pl.BlockSpec((TQ, Q_WIDTH), qrow),
        out_shape=jax.ShapeDtypeStruct((batch * seq_len, Q_WIDTH), BF16),
        scratch_shapes=scratch_shapes,
        compiler_params=pltpu.CompilerParams(
            dimension_semantics=("arbitrary", "arbitrary"),
            vmem_limit_bytes=_vmem_limit(blocks, scratch)),
        name="nsa",
    )(q, gate, kc.reshape(batch, n_cmp_pad, w_kv_pad), vc.reshape(batch, n_cmp_pad, w_kv_pad),
      ks.reshape(batch, seq_len, KV_WIDTH), vst, kw.reshape(batch, seq_len, KV_WIDTH), vwt, ov)


def _pool_kernel(u_ref, w_ref, scale_ref, o_ref):
    seq = u_ref.shape[0]
    t = lax.broadcasted_iota(jnp.int32, (seq, POOL_GROUP), 0)

    def shifted(x, k):
        return jnp.where(t >= k, pltpu.roll(x, k, axis=0), 0.0)

    for gi, w in enumerate(POOL_WINDOWS):
        sl = slice(gi * POOL_GROUP, (gi + 1) * POOL_GROUP)
        x = u_ref[:, sl]
        wsum = x
        span = 1
        while span < w:
            wsum = wsum + shifted(wsum, span)
            span *= 2
        cnt = jnp.minimum(t + 1, w).astype(F32)
        pooled = (wsum / cnt - x).astype(BF16)
        o_ref[:, sl] = (_dot(pooled, w_ref[gi]) * scale_ref[:, sl]).astype(BF16)


def _pool(pool_in, pool_w, pool_scale, batch, seq_len):
    for w in POOL_WINDOWS:
        assert w & (w - 1) == 0
    blocks = (_nbytes((seq_len, POOL_WIDTH), F32) + _nbytes(pool_w.shape, BF16)
              + _nbytes((seq_len, POOL_WIDTH), BF16))
    return pl.pallas_call(
        _pool_kernel,
        grid=(batch,),
        in_specs=[
            pl.BlockSpec((seq_len, POOL_WIDTH), lambda b: (b, 0)),
            pl.BlockSpec(pool_w.shape, lambda b: (0, 0, 0)),
            pl.BlockSpec((1, POOL_WIDTH), lambda b: (0, 0)),
        ],
        out_specs=pl.BlockSpec((seq_len, POOL_WIDTH), lambda b: (b, 0)),
        out_shape=jax.ShapeDtypeStruct((batch * seq_len, POOL_WIDTH), BF16),
        compiler_params=pltpu.CompilerParams(
            dimension_semantics=("arbitrary",), vmem_limit_bytes=_vmem_limit(blocks)),
        name="pool",
    )(pool_in, pool_w.astype(BF16), pool_scale.reshape(1, -1))


def _merge_kernel(x_ref, gpre_ref, wt_ref, nsa_ref, wa_ref, pool_ref, wp_ref, wo_ref, gpost_ref,
                  o_ref):
    x = x_ref[...]
    hn = _rms(x, gpre_ref[...])
    g_attn = jax.nn.sigmoid(_dot_nt(hn, wt_ref[OFF_MERGE:OFF_MERGE + D_MODEL, :]))
    g_pool = jax.nn.sigmoid(_dot_nt(hn, wt_ref[OFF_MERGE + D_MODEL:OFF_MERGE + 2 * D_MODEL, :]))
    y = (g_attn * _dot(nsa_ref[...].astype(F32), wa_ref[...])
         + g_pool * _dot(pool_ref[...].astype(F32), wp_ref[...]))
    h = _dot(y, wo_ref[...])
    o_ref[...] = x + _rms(h, gpost_ref[...])


def _merge(x1, g_pre, w_in_t, o_nsa, w_attn, o_pool, w_pool, w_out, g_post):
    t = x1.shape[0]
    row = lambda i: (i, 0)
    fixed = lambda i: (0, 0)
    once = pl.Buffered(1)
    weights = (_nbytes(w_in_t.shape, F32) + _nbytes((Q_WIDTH, D_MODEL), F32)
               + _nbytes((POOL_WIDTH, D_MODEL), F32) + _nbytes((D_MODEL, D_MODEL), F32))
    blocks = (2 * _nbytes((TOK_TILE, D_MODEL), F32) + _nbytes((TOK_TILE, Q_WIDTH), BF16)
              + _nbytes((TOK_TILE, POOL_WIDTH), BF16))
    return pl.pallas_call(
        _merge_kernel,
        grid=(t // TOK_TILE,),
        in_specs=[
            pl.BlockSpec((TOK_TILE, D_MODEL), row),
            pl.BlockSpec((1, D_MODEL), fixed),
            pl.BlockSpec(w_in_t.shape, fixed, pipeline_mode=once),
            pl.BlockSpec((TOK_TILE, Q_WIDTH), row),
            pl.BlockSpec((Q_WIDTH, D_MODEL), fixed, pipeline_mode=once),
            pl.BlockSpec((TOK_TILE, POOL_WIDTH), row),
            pl.BlockSpec((POOL_WIDTH, D_MODEL), fixed, pipeline_mode=once),
            pl.BlockSpec((D_MODEL, D_MODEL), fixed, pipeline_mode=once),
            pl.BlockSpec((1, D_MODEL), fixed),
        ],
        out_specs=pl.BlockSpec((TOK_TILE, D_MODEL), row),
        out_shape=jax.ShapeDtypeStruct((t, D_MODEL), F32),
        compiler_params=pltpu.CompilerParams(
            dimension_semantics=("arbitrary",), vmem_limit_bytes=_vmem_limit(blocks, weights)),
        name="merge",
    )(x1, g_pre.reshape(1, -1), w_in_t, o_nsa, w_attn, o_pool, w_pool, w_out, g_post.reshape(1, -1))


def _layer(x, tab, tab_cmp, batch, seq_len, g_ffn1_pre, w_ffn1_gate, w_ffn1_up, w_ffn1_down,
           g_ffn1_post, g_mix_pre, w_in, cmp_pe_k, cmp_w1_k, cmp_w2_k, cmp_pe_v, cmp_w1_v, cmp_w2_v,
           w_attn_branch, pool_w, pool_scale, w_pool_branch, w_out, g_mix_post, g_ffn2_pre,
           w_ffn2_gate, w_ffn2_up, w_ffn2_down, g_ffn2_post):
    x1 = _ffn(x, g_ffn1_pre, w_ffn1_gate, w_ffn1_up, w_ffn1_down, g_ffn1_post)
    w_in_t = w_in.T
    q, ks, kw, vst, vwt, cmp_rows, gate, pool_in = _in_proj(x1, g_mix_pre, w_in_t, tab)
    kc, vc = _compress(cmp_rows, cmp_pe_k, cmp_w1_k, cmp_w2_k, cmp_pe_v, cmp_w1_v, cmp_w2_v, tab_cmp)
    n_cmp_pad = seq_len // CMP_STRIDE
    kc = kc.reshape(batch, n_cmp_pad, N_KV * SLOT)
    vc = vc.reshape(batch, n_cmp_pad, N_KV * SLOT)
    o_nsa = _nsa(q, gate, kc, vc, ks, vst, kw, vwt, batch, seq_len)
    o_pool = _pool(pool_in, pool_w, pool_scale, batch, seq_len)
    x2 = _merge(x1, g_mix_pre, w_in_t, o_nsa, w_attn_branch, o_pool, w_pool_branch, w_out,
                g_mix_post)
    return _ffn(x2, g_ffn2_pre, w_ffn2_gate, w_ffn2_up, w_ffn2_down, g_ffn2_post)


def kernel(x, positions, g_ffn1_pre, w_ffn1_gate, w_ffn1_up, w_ffn1_down, g_ffn1_post, g_mix_pre, w_in, cmp_pe_k, cmp_w1_k, cmp_w2_k, cmp_pe_v, cmp_w1_v, cmp_w2_v, w_attn_branch, pool_w, pool_scale, w_pool_branch, w_out, g_mix_post, g_ffn2_pre, w_ffn2_gate, w_ffn2_up, w_ffn2_down, g_ffn2_post):
    batch, seq_len, d_model = x.shape
    assert d_model == D_MODEL and seq_len % TOK_TILE == 0 and seq_len % TQ == 0
    assert TQ == KT and TQ % SEL_LEN == 0 and TOK_TILE % KT == 0 and WINDOW % KT == 0
    assert seq_len % CMP_STRIDE == 0 and CMP_LEN == 2 * CMP_STRIDE
    t = batch * seq_len
    tab = _rope_tables(positions.reshape(t), seq_len)
    n_cmp = (seq_len - CMP_LEN) // CMP_STRIDE + 1
    n_cmp_pad = seq_len // CMP_STRIDE
    pos_cmp = jnp.pad(positions[:, CMP_LEN - 1::CMP_STRIDE], ((0, 0), (0, n_cmp_pad - n_cmp)))
    tab_cmp = _rope_tables(pos_cmp.reshape(batch * n_cmp_pad), batch * n_cmp_pad)
    xf = x.reshape(t, D_MODEL)
    per_layer = (g_ffn1_pre, w_ffn1_gate, w_ffn1_up, w_ffn1_down, g_ffn1_post, g_mix_pre, w_in,
                 cmp_pe_k, cmp_w1_k, cmp_w2_k, cmp_pe_v, cmp_w1_v, cmp_w2_v, w_attn_branch, pool_w,
                 pool_scale, w_pool_branch, w_out, g_mix_post, g_ffn2_pre, w_ffn2_gate, w_ffn2_up,
                 w_ffn2_down, g_ffn2_post)
    for l in range(g_ffn1_pre.shape[0]):
        xf = _layer(xf, tab, tab_cmp, batch, seq_len, *(p[l] for p in per_layer))
    return xf.reshape(batch, seq_len, D_MODEL)
```

```python
import functools
import math

import jax
import jax.numpy as jnp
from jax import lax
from jax.experimental import pallas as pl
from jax.experimental.pallas import tpu as pltpu

F32 = jnp.float32
BF16 = jnp.bfloat16

D_MODEL = 1024
N_HEADS = 16
HEAD_DIM = 64
N_KV = 4
GROUP = N_HEADS // N_KV
ROT_DIM = HEAD_DIM // 4
ROT_HALF = ROT_DIM // 2
ROPE_THETA = 500000.0
CMP_LEN = 32
CMP_STRIDE = 16
CMP_HIDDEN = 2 * HEAD_DIM
SEL_LEN = 64
SEL_SHIFT = SEL_LEN.bit_length() - 1
N_SEL = 8
WINDOW = 512
POOL_WINDOWS = (2, 4, 8, 16)
POOL_WIDTH = D_MODEL // 2
POOL_GROUP = POOL_WIDTH // len(POOL_WINDOWS)
D_FF = 2816
EPS = 1e-6
NEG_INF = -1e30
FORCE_SCORE = 1e4
Q_WIDTH = N_HEADS * HEAD_DIM
KV_WIDTH = N_KV * HEAD_DIM
N_GATES = 3 * N_HEADS
LOG2_E = math.log2(math.e)

LANES = 128
BF16_SUBLANES = 16
V7X_VMEM_BYTES = 64 * 1024 * 1024

SLOT = LANES
BLK_LANE0 = HEAD_DIM
VT_ROWS = HEAD_DIM + BF16_SUBLANES
ONE_ROW = HEAD_DIM
TOK_TILE = 512
TQ = 256
KT = 256
HALF = TQ // 2
CHAIN_COLS = 256
PIPE_DEPTH = 10
FF_CHUNK = 256
IN_SPLIT = 2
FFN_SPLIT = 2
N_ROPE_TAB = 2


def _vmem_limit(block_bytes, scratch_bytes=0):
    need = 2 * block_bytes + scratch_bytes
    return int(min(V7X_VMEM_BYTES - (4 << 20), max(2 * need, 32 << 20)))


def _nbytes(shape, dtype):
    n = 1
    for s in shape:
        n *= s
    return n * jnp.dtype(dtype).itemsize


def _rms(xf, g):
    return xf * lax.rsqrt(jnp.mean(xf * xf, axis=-1, keepdims=True) + EPS) * g


def _dot(a, b):
    return jnp.dot(a, b, preferred_element_type=F32)


def _dot_nt(a, b):
    return lax.dot_general(a, b, (((1,), (1,)), ((), ())), preferred_element_type=F32)


def _rope_trig_kernel(pos_ref, inv_ref, tab_ref):
    tn = pos_ref.shape[1]
    ang = pos_ref[...].astype(F32) * inv_ref[...]
    c, s = jnp.cos(ang), jnp.sin(ang)
    rest = HEAD_DIM - ROT_DIM
    reps = SLOT // HEAD_DIM
    cos_rows = jnp.concatenate([c, c, jnp.ones((rest, tn), F32)] * reps, axis=0)
    sin_rows = jnp.concatenate([-s, s, jnp.zeros((rest, tn), F32)] * reps, axis=0)
    for j in range(tn // LANES):
        cs = slice(j * LANES, (j + 1) * LANES)
        tab_ref[cs, 0:SLOT] = cos_rows[:, cs].T
        tab_ref[cs, SLOT:2 * SLOT] = sin_rows[:, cs].T


def _rope_tables(pos_flat, tile):
    n = pos_flat.shape[0]
    inv = ROPE_THETA ** (-jnp.arange(ROT_HALF, dtype=F32) * (2.0 / ROT_DIM))
    return pl.pallas_call(
        _rope_trig_kernel,
        grid=(n // tile,),
        in_specs=[pl.BlockSpec((1, tile), lambda i: (0, i)),
                  pl.BlockSpec((ROT_HALF, 1), lambda i: (0, 0))],
        out_specs=pl.BlockSpec((tile, N_ROPE_TAB * SLOT), lambda i: (i, 0)),
        out_shape=jax.ShapeDtypeStruct((n, N_ROPE_TAB * SLOT), F32),
        compiler_params=pltpu.CompilerParams(dimension_semantics=("arbitrary",)),
        name="rope_trig",
    )(pos_flat.reshape(1, n), inv.reshape(ROT_HALF, 1))


def _rope_slot(y, tab):
    cos_t = tab[:, 0:SLOT]
    sin_t = tab[:, SLOT:2 * SLOT]
    lane = lax.broadcasted_iota(jnp.int32, y.shape, 1) & (HEAD_DIM - 1)
    up = pltpu.roll(y, SLOT - ROT_HALF, axis=1)
    down = pltpu.roll(y, ROT_HALF, axis=1)
    return y * cos_t + jnp.where(lane < ROT_HALF, up, down) * sin_t


def _ffn_kernel(x_ref, gpre_ref, wg_ref, wu_ref, wd_ref, gpost_ref, o_ref):
    rows = x_ref.shape[0] // FFN_SPLIT
    parts = [slice(k * rows, (k + 1) * rows) for k in range(FFN_SPLIT)]
    hn = [_rms(x_ref[rs, :], gpre_ref[...]) for rs in parts]
    for k, rs in enumerate(parts):
        for c in range(D_FF // FF_CHUNK):
            sl = slice(c * FF_CHUNK, (c + 1) * FF_CHUNK)
            g = _dot(hn[k], wg_ref[:, sl])
            u = _dot(hn[k], wu_ref[:, sl])
            d = _dot(g * jax.nn.sigmoid(g) * u, wd_ref[sl, :])
            if c == 0:
                o_ref[rs, :] = d
            else:
                o_ref[rs, :] += d
        o_ref[rs, :] = x_ref[rs, :] + 0.5 * _rms(o_ref[rs, :], gpost_ref[...])


def _ffn(x, g_pre, w_gate, w_up, w_down, g_post):
    t = x.shape[0]
    tile = FFN_SPLIT * TOK_TILE
    row = lambda i: (i, 0)
    fixed = lambda i: (0, 0)
    once = pl.Buffered(1)
    weights = 3 * _nbytes((D_MODEL, D_FF), F32)
    blocks = 2 * _nbytes((tile, D_MODEL), F32)
    return pl.pallas_call(
        _ffn_kernel,
        grid=(t // tile,),
        in_specs=[
            pl.BlockSpec((tile, D_MODEL), row),
            pl.BlockSpec((1, D_MODEL), fixed),
            pl.BlockSpec((D_MODEL, D_FF), fixed, pipeline_mode=once),
            pl.BlockSpec((D_MODEL, D_FF), fixed, pipeline_mode=once),
            pl.BlockSpec((D_FF, D_MODEL), fixed, pipeline_mode=once),
            pl.BlockSpec((1, D_MODEL), fixed),
        ],
        out_specs=pl.BlockSpec((tile, D_MODEL), row),
        out_shape=jax.ShapeDtypeStruct((t, D_MODEL), F32),
        compiler_params=pltpu.CompilerParams(
            dimension_semantics=("arbitrary",), vmem_limit_bytes=_vmem_limit(blocks, weights)),
        name="ffn",
    )(x, g_pre.reshape(1, -1), w_gate, w_up, w_down, g_post.reshape(1, -1))


OFF_Q = 0
OFF_CMP = OFF_Q + Q_WIDTH
OFF_KS = OFF_CMP + 2 * KV_WIDTH
OFF_VS = OFF_KS + KV_WIDTH
OFF_KW = OFF_VS + KV_WIDTH
OFF_VW = OFF_KW + KV_WIDTH
OFF_GATE = OFF_VW + KV_WIDTH
OFF_POOL = OFF_GATE + N_GATES
OFF_MERGE = OFF_POOL + POOL_WIDTH
W_CMP = 2 * KV_WIDTH
CHUNK_TOK = CMP_STRIDE
VT_SLAB = N_KV * VT_ROWS


def _in_proj_kernel(x_ref, g_ref, w_ref, tab_ref, q_ref, ks_ref, kw_ref, vst_ref,
                    vwt_ref, cmp_ref, gate_ref, pool_ref, cmp_scr):
    rows = x_ref.shape[0] // IN_SPLIT
    wide = 4 * SLOT
    aux = (lax.broadcasted_iota(jnp.int32, (VT_ROWS - HEAD_DIM, KT), 0) == 0).astype(BF16)

    def part(k):
        rs = slice(k * rows, (k + 1) * rows)
        hn = _rms(x_ref[rs, :], g_ref[...])
        tab = tab_ref[rs, :]
        tab_q = tab * (HEAD_DIM ** -0.5 * LOG2_E)
        for c in range(Q_WIDTH // wide):
            y = _dot_nt(hn, w_ref[OFF_Q + c * wide:OFF_Q + (c + 1) * wide, :])
            for j in range(wide // SLOT):
                sl = slice(c * wide + j * SLOT, c * wide + (j + 1) * SLOT)
                q_ref[rs, sl] = _rope_slot(y[:, j * SLOT:(j + 1) * SLOT], tab_q).astype(BF16)
        for k_ref, off in ((ks_ref, OFF_KS), (kw_ref, OFF_KW)):
            y = _dot_nt(hn, w_ref[off:off + KV_WIDTH, :])
            for j in range(KV_WIDTH // SLOT):
                sl = slice(j * SLOT, (j + 1) * SLOT)
                k_ref[rs, sl] = _rope_slot(y[:, sl], tab).astype(BF16)
        y = _dot_nt(hn, w_ref[OFF_CMP:OFF_CMP + W_CMP, :])
        for j in range(W_CMP // LANES):
            cmp_scr[j, rs, :] = y[:, j * LANES:(j + 1) * LANES]
        chunk_rows = rows // CHUNK_TOK
        for l in range(CHUNK_TOK):
            for j in range(W_CMP // LANES):
                c0 = l * W_CMP + j * LANES
                cmp_ref[k * chunk_rows:(k + 1) * chunk_rows, c0:c0 + LANES] = (
                    cmp_scr[j, pl.ds(k * rows + l, chunk_rows, stride=CHUNK_TOK), :].astype(BF16))
        gate_ref[rs, :] = _dot_nt(hn, w_ref[OFF_GATE:OFF_GATE + SLOT, :])
        pool_ref[rs, :] = _dot_nt(hn, w_ref[OFF_POOL:OFF_MERGE, :])
        tiles = rows // KT
        for vt_ref, off in ((vst_ref, OFF_VS), (vwt_ref, OFF_VW)):
            vt = _dot_nt(w_ref[off:off + KV_WIDTH, :], hn).astype(BF16)
            for j in range(tiles):
                for g in range(N_KV):
                    vt_ref[k * tiles + j, g * VT_ROWS:g * VT_ROWS + HEAD_DIM, :] = (
                        vt[g * HEAD_DIM:(g + 1) * HEAD_DIM, j * KT:(j + 1) * KT])
                    vt_ref[k * tiles + j, g * VT_ROWS + HEAD_DIM:(g + 1) * VT_ROWS, :] = aux

    for k in range(IN_SPLIT):
        part(k)


def _in_proj(x1, g_pre, w_in_t, tab):
    t = x1.shape[0]
    row = lambda i: (i, 0)
    fixed = lambda i: (0, 0)
    once = pl.Buffered(1)
    tile = IN_SPLIT * TOK_TILE
    tiles = tile // KT
    chunk_rows = tile // CHUNK_TOK
    slab = pl.BlockSpec((tiles, VT_SLAB, KT), lambda i: (i, 0, 0))
    slab_shape = jax.ShapeDtypeStruct((t // KT, VT_SLAB, KT), BF16)
    out_specs = [
        pl.BlockSpec((tile, Q_WIDTH), row), pl.BlockSpec((tile, KV_WIDTH), row),
        pl.BlockSpec((tile, KV_WIDTH), row), slab, slab,
        pl.BlockSpec((chunk_rows, CHUNK_TOK * W_CMP), row),
        pl.BlockSpec((tile, SLOT), row), pl.BlockSpec((tile, POOL_WIDTH), row)]
    out_shape = [
        jax.ShapeDtypeStruct((t, Q_WIDTH), BF16), jax.ShapeDtypeStruct((t, KV_WIDTH), BF16),
        jax.ShapeDtypeStruct((t, KV_WIDTH), BF16), slab_shape, slab_shape,
        jax.ShapeDtypeStruct((t // CHUNK_TOK, CHUNK_TOK * W_CMP), BF16),
        jax.ShapeDtypeStruct((t, SLOT), F32), jax.ShapeDtypeStruct((t, POOL_WIDTH), F32)]
    weights = _nbytes(w_in_t.shape, F32)
    blocks = (_nbytes((tile, D_MODEL), F32) + _nbytes((tile, N_ROPE_TAB * SLOT), F32)
              + _nbytes((tile, Q_WIDTH + 2 * KV_WIDTH + W_CMP), BF16)
              + 2 * _nbytes((tiles, VT_SLAB, KT), BF16) + _nbytes((tile, SLOT + POOL_WIDTH), F32))
    res = pl.pallas_call(
        _in_proj_kernel,
        grid=(t // tile,),
        in_specs=[
            pl.BlockSpec((tile, D_MODEL), row),
            pl.BlockSpec((1, D_MODEL), fixed),
            pl.BlockSpec(w_in_t.shape, fixed, pipeline_mode=once),
            pl.BlockSpec((tile, N_ROPE_TAB * SLOT), row),
        ],
        out_specs=out_specs,
        out_shape=out_shape,
        scratch_shapes=[pltpu.VMEM((W_CMP // LANES, tile, LANES), F32)],
        compiler_params=pltpu.CompilerParams(
            dimension_semantics=("arbitrary",),
            vmem_limit_bytes=_vmem_limit(blocks, weights + _nbytes((tile, W_CMP), F32))),
        name="in_proj",
    )(x1, g_pre.reshape(1, -1), w_in_t, tab)
    return res


def _compress_kernel(x_ref, w1k_ref, w1v_ref, pek_ref, pev_ref, w1k_raw_ref, w1v_raw_ref,
                     w2k_ref, w2v_ref, tab_ref, kc_ref, vc_ref, acck_ref, accv_ref):
    l = pl.program_id(0)

    @pl.when(l == 0)
    def _():
        acck_ref[...] = jnp.zeros(acck_ref.shape, F32)
        accv_ref[...] = jnp.zeros(accv_ref.shape, F32)

    for acc_ref, w1_ref, c0 in ((acck_ref, w1k_ref, 0), (accv_ref, w1v_ref, KV_WIDTH)):
        for g in range(N_KV):
            acc_ref[g] += _dot(x_ref[:, c0 + g * HEAD_DIM:c0 + (g + 1) * HEAD_DIM], w1_ref[0])

    @pl.when(l == CHUNK_TOK - 1)
    def _():
        rows = acck_ref.shape[1]
        tab = tab_ref[...]
        for acc_ref, pe_ref, w1_raw_ref, w2_ref, o_ref, rope in (
                (acck_ref, pek_ref, w1k_raw_ref, w2k_ref, kc_ref, True),
                (accv_ref, pev_ref, w1v_raw_ref, w2v_ref, vc_ref, False)):
            bias = _dot(pe_ref[...], w1_raw_ref[...])[0:1, :]
            for g in range(N_KV):
                first = acc_ref[g, :, 0:CMP_HIDDEN]
                second = pltpu.roll(acc_ref[g, :, CMP_HIDDEN:2 * CMP_HIDDEN], rows - 1, axis=0)
                hid = jax.nn.gelu(first + second + bias).astype(BF16)
                out = _dot(hid, w2_ref[...])
                if rope:
                    out = _rope_slot(out, tab)
                o_ref[:, g * SLOT:(g + 1) * SLOT] = out.astype(BF16)


def _compress_weights(w1, w2):
    w1 = w1.reshape(2, CHUNK_TOK, HEAD_DIM, CMP_HIDDEN)
    w1_tok = jnp.concatenate([w1[0], w1[1]], axis=-1).astype(BF16)
    return w1_tok, jnp.pad(w2, ((0, 0), (0, SLOT - HEAD_DIM))).astype(BF16)


def _compress(cmp_rows, cmp_pe_k, cmp_w1_k, cmp_w2_k, cmp_pe_v, cmp_w1_v, cmp_w2_v, tab_cmp):
    rows = cmp_rows.shape[0]
    w1k, w2k = _compress_weights(cmp_w1_k, cmp_w2_k)
    w1v, w2v = _compress_weights(cmp_w1_v, cmp_w2_v)
    sub = 8
    pek = jnp.broadcast_to(cmp_pe_k.reshape(1, -1), (sub, CMP_LEN * HEAD_DIM)).astype(BF16)
    pev = jnp.broadcast_to(cmp_pe_v.reshape(1, -1), (sub, CMP_LEN * HEAD_DIM)).astype(BF16)
    fixed = lambda l: (0, 0)
    w_kv_pad = N_KV * SLOT
    blocks = (_nbytes((rows, W_CMP), BF16) + 2 * _nbytes((HEAD_DIM, 2 * CMP_HIDDEN), BF16)
              + 2 * _nbytes((sub + CMP_LEN * HEAD_DIM, CMP_HIDDEN), BF16)
              + 2 * _nbytes((CMP_HIDDEN, SLOT), BF16)
              + _nbytes((rows, N_ROPE_TAB * SLOT), F32) + 2 * _nbytes((rows, w_kv_pad), BF16))
    acc = pltpu.VMEM((N_KV, rows, 2 * CMP_HIDDEN), F32)
    return pl.pallas_call(
        _compress_kernel,
        grid=(CHUNK_TOK,),
        in_specs=[
            pl.BlockSpec((rows, W_CMP), lambda l: (0, l)),
            pl.BlockSpec((1, HEAD_DIM, 2 * CMP_HIDDEN), lambda l: (l, 0, 0)),
            pl.BlockSpec((1, HEAD_DIM, 2 * CMP_HIDDEN), lambda l: (l, 0, 0)),
            pl.BlockSpec((sub, CMP_LEN * HEAD_DIM), fixed),
            pl.BlockSpec((sub, CMP_LEN * HEAD_DIM), fixed),
            pl.BlockSpec((CMP_LEN * HEAD_DIM, CMP_HIDDEN), fixed),
            pl.BlockSpec((CMP_LEN * HEAD_DIM, CMP_HIDDEN), fixed),
            pl.BlockSpec((CMP_HIDDEN, SLOT), fixed),
            pl.BlockSpec((CMP_HIDDEN, SLOT), fixed),
            pl.BlockSpec((rows, N_ROPE_TAB * SLOT), fixed),
        ],
        out_specs=[pl.BlockSpec((rows, w_kv_pad), fixed)] * 2,
        out_shape=[jax.ShapeDtypeStruct((rows, w_kv_pad), BF16)] * 2,
        scratch_shapes=[acc, acc],
        compiler_params=pltpu.CompilerParams(
            dimension_semantics=("arbitrary",),
            vmem_limit_bytes=_vmem_limit(blocks, 2 * _nbytes(acc.shape, F32))),
        name="compress",
    )(cmp_rows, w1k, w1v, pek, pev, cmp_w1_k.astype(BF16), cmp_w1_v.astype(BF16), w2k, w2v, tab_cmp)


def _nsa_kernel(q_ref, gate_ref, kc_ref, vc_ref, ks_ref, vst_ref, kw_ref, vwt_ref, ov_ref,
                o_ref, ksp_ref, kwp_ref, qg_ref, mix_ref, *state_refs):
    i = pl.program_id(1)
    seq = ks_ref.shape[1]
    cols_g = GROUP * TQ
    chains_g = cols_g // CHAIN_COLS
    n_chains = N_KV * chains_g
    n_cmp = kc_ref.shape[1]
    sel_state = (state_refs[0:n_chains], state_refs[n_chains:2 * n_chains])
    win_state = (state_refs[2 * n_chains:3 * n_chains], state_refs[3 * n_chains:4 * n_chains])
    lane_q = lax.broadcasted_iota(jnp.int32, (TQ, SLOT), 1)
    lane_h = lax.broadcasted_iota(jnp.int32, (HALF, SLOT), 1)

    def head_slot(x_ref, rows, j):
        two = x_ref[rows, (j // 2) * SLOT:(j // 2 + 1) * SLOT]
        if j % 2:
            words = pltpu.roll(pltpu.bitcast(two, jnp.uint32), HEAD_DIM, axis=1)
            two = pltpu.bitcast(words, BF16)
        return two

    @pl.when(i == 0)
    def _():
        for c in range(seq // KT):
            rows = pl.ds(c * KT, KT)
            lane = lax.broadcasted_iota(jnp.int32, (KT, SLOT), 1)
            key_blk = (c * KT + lax.broadcasted_iota(jnp.int32, (KT, SLOT), 0)) >> SEL_SHIFT
            onehot = jnp.where(lane - BLK_LANE0 == key_blk, 1.0, 0.0).astype(BF16)
            zeros = jnp.zeros((KT, SLOT), BF16)
            for g in range(N_KV):
                gs = slice(g * SLOT, (g + 1) * SLOT)
                ksp_ref[rows, gs] = jnp.where(lane < HEAD_DIM, head_slot(ks_ref.at[0], rows, g), onehot)
                kwp_ref[rows, gs] = jnp.where(lane < HEAD_DIM, head_slot(kw_ref.at[0], rows, g), zeros)

    def q_rows(r, x):
        return slice((x * GROUP + r) * HALF, (x * GROUP + r + 1) * HALF)

    def chain_half(c):
        return c * CHAIN_COLS // (GROUP * HALF)

    for h in range(N_HEADS):
        g, r = divmod(h, GROUP)
        q_pad = jnp.where(lane_q < HEAD_DIM, head_slot(q_ref, slice(None), h), jnp.zeros((TQ, SLOT), BF16))
        for x in range(2):
            qg_ref[g, q_rows(r, x), :] = q_pad[x * HALF:(x + 1) * HALF]

    gates_t = jax.nn.sigmoid(gate_ref[...]).T
    key_h = lax.broadcasted_iota(jnp.int32, (HALF, CHAIN_COLS), 0)
    qry_h = lax.broadcasted_iota(jnp.int32, (HALF, CHAIN_COLS), 1) & (HALF - 1)
    tri_le, tri_gt = key_h <= qry_h, key_h > qry_h
    diag_plan = ((slice(0, HALF), tri_le, slice(0, HALF)), (slice(0, KT), tri_le, slice(HALF, KT)))
    far_plan = ((slice(0, KT), tri_gt, slice(0, HALF)), (slice(HALF, KT), tri_gt, slice(0, HALF)))

    def gate_row(branch, h):
        c = branch * N_HEADS + h
        return gates_t[c:c + 1, :]

    sub = 8
    cmp_per_sub = sub * SEL_LEN // CMP_STRIDE

    def compressed_scores(ng):
        n_c = ng * cmp_per_sub
        cmp_scores = []
        for h in range(N_HEADS):
            g, r = divmod(h, GROUP)
            q_h = jnp.concatenate([qg_ref[g, q_rows(r, 0), :], qg_ref[g, q_rows(r, 1), :]], axis=0)
            cmp_scores.append(_dot_nt(kc_ref[0, 0:n_c, g * SLOT:(g + 1) * SLOT], q_h))
        return cmp_scores

    def compressed_and_select(ng, start, cmp_scores):
        n_blk, n_c = ng * sub, ng * cmp_per_sub
        t_cmp = start + lax.broadcasted_iota(jnp.int32, (n_c, TQ), 1)
        n_idx = lax.broadcasted_iota(jnp.int32, (n_c, TQ), 0)
        cmp_valid = n_idx * CMP_STRIDE + (CMP_LEN - 1) <= t_cmp
        any_cmp = start + lax.broadcasted_iota(jnp.int32, (1, TQ), 1) >= CMP_LEN - 1
        t_row = start + lax.broadcasted_iota(jnp.int32, (n_blk, TQ), 1)
        blk = lax.broadcasted_iota(jnp.int32, (n_blk, TQ), 0)
        forced = (blk == t_row >> SEL_SHIFT) | (blk == 0)
        causal_blk = blk * SEL_LEN <= t_row
        idx8 = lax.broadcasted_iota(jnp.int32, (sub, TQ), 0)
        pad_c = jnp.zeros((n_cmp - n_c, TQ), F32)
        for g in range(N_KV):
            vct_g = vc_ref[0, :, g * SLOT:(g + 1) * SLOT].astype(F32).T[0:HEAD_DIM].astype(BF16)
            p_sum = jnp.zeros((n_c, TQ), F32)
            for r in range(GROUP):
                h = g * GROUP + r
                s = jnp.where(cmp_valid, cmp_scores[h], NEG_INF)
                p = jnp.exp2(s - jnp.max(s, axis=0, keepdims=True))
                inv = jnp.where(any_cmp, 1.0 / jnp.sum(p, axis=0, keepdims=True), 0.0)
                p = p * inv
                p_sum = p_sum + p
                p_all = jnp.concatenate([p, pad_c], axis=0) if ng * cmp_per_sub < n_cmp else p
                mix_ref[h] = gate_row(0, h) * _dot(vct_g, p_all.astype(BF16))
            p_sum_all = jnp.concatenate([p_sum, pad_c], axis=0) if ng * cmp_per_sub < n_cmp else p_sum
            imp = jnp.dot(ov_ref[0:n_blk, :], p_sum_all, precision=lax.Precision.HIGHEST,
                          preferred_element_type=F32)
            score = jnp.where(causal_blk, jnp.where(forced, FORCE_SCORE, imp), NEG_INF)
            rows8 = [score[k * sub:(k + 1) * sub] for k in range(ng)]
            ranks = [jnp.zeros((sub, TQ), jnp.int32) for _ in rows8]
            for c in range(n_blk):
                other = score[c:c + 1, :]
                for k, mine in enumerate(rows8):
                    if k * sub > c:
                        beats = other >= mine
                    elif (k + 1) * sub - 1 <= c:
                        beats = other > mine
                    else:
                        beats = (other > mine) | ((other == mine) & (idx8 > c - k * sub))
                    ranks[k] = ranks[k] + beats.astype(jnp.int32)
            rank = jnp.concatenate(ranks, axis=0) if ng > 1 else ranks[0]
            bias = jnp.where(causal_blk & (rank < N_SEL), 0.0, NEG_INF)
            bias_t = jnp.concatenate([jnp.zeros((BLK_LANE0, TQ), F32), bias,
                                      jnp.zeros((SLOT - BLK_LANE0 - n_blk, TQ), F32)], axis=0)
            bias_q = bias_t.T.astype(BF16)
            for r in range(GROUP):
                for x in range(2):
                    rs = q_rows(r, x)
                    qg_ref[g, rs, :] = jnp.where(lane_h < HEAD_DIM, qg_ref[g, rs, :],
                                                 bias_q[x * HALF:(x + 1) * HALF])

    def flash_init(state):
        for m_ref, acc_ref in zip(*state):
            m_ref[...] = jnp.full(m_ref.shape, NEG_INF, F32)
            acc_ref[...] = jnp.zeros(acc_ref.shape, F32)

    def flash_tiles(tiles):
        jobs = [(tile, c) for tile in tiles for c in range(n_chains)]

        def key_plan(job):
            (_, _, _, plan, _), chain = job
            return (slice(0, KT), None, None) if plan is None else plan[chain_half(chain % chains_g)]

        scores = {}
        for step in range(len(jobs) + PIPE_DEPTH):
            if step < len(jobs):
                (kp_ref, _, kt, _, _), chain = jobs[step]
                g, c = divmod(chain, chains_g)
                keys, mask, rows = key_plan(jobs[step])
                off = kt * KT + keys.start
                s = _dot_nt(kp_ref[off:off + keys.stop - keys.start, g * SLOT:(g + 1) * SLOT],
                            qg_ref[g, c * CHAIN_COLS:(c + 1) * CHAIN_COLS, :])
                if mask is not None:
                    parts = [s[0:rows.start], jnp.where(mask, s[rows], NEG_INF), s[rows.stop:]]
                    s = jnp.concatenate([p for p in parts if p.shape[0]], axis=0)
                scores[step] = s.astype(BF16)
            done = step - PIPE_DEPTH
            if done >= 0:
                (_, vt_ref, kt, _, (m_refs, acc_refs)), chain = jobs[done]
                g = chain // chains_g
                keys = key_plan(jobs[done])[0]
                m_ref, acc_ref = m_refs[chain], acc_refs[chain]
                s = scores.pop(done)
                m_old = m_ref[...]
                packed = [s[r * BF16_SUBLANES:(r + 1) * BF16_SUBLANES]
                          for r in range(s.shape[0] // BF16_SUBLANES)]
                m_tile = functools.reduce(jnp.maximum, packed).astype(F32)
                m_new = jnp.maximum(m_old, jnp.max(m_tile, axis=0, keepdims=True))
                p = jnp.exp2(s - m_new.astype(BF16))
                pv = _dot(vt_ref[kt, g * VT_ROWS:(g + 1) * VT_ROWS, keys], p)
                acc_ref[...] = jnp.exp2(m_old - m_new) * acc_ref[...] + pv
                m_ref[...] = m_new

    def flash_mix(branch, state):
        _, acc_refs = state
        for h in range(N_HEADS):
            g, r = divmod(h, GROUP)
            gate = gate_row(branch, h)
            for x in range(2):
                c, c0 = divmod(q_rows(r, x).start, CHAIN_COLS)
                acc = acc_refs[g * chains_g + c][:, c0:c0 + HALF]
                qs = slice(x * HALF, (x + 1) * HALF)
                scale = gate[:, qs] * (1.0 / acc[ONE_ROW:ONE_ROW + 1, :])
                mix_ref[h, :, qs] = mix_ref[h, :, qs] + scale * acc[0:HEAD_DIM]

    flash_init(sel_state)
    flash_init(win_state)

    n_back = WINDOW // KT
    for k in range(seq // TQ):

        @pl.when(i == k)
        def _(k=k):
            ng = -(-(k + 1) * TQ // (sub * SEL_LEN))
            cmp_scores = compressed_scores(ng)
            win = [(kwp_ref, vwt_ref, k - d, far_plan if d == n_back else None, win_state)
                   for d in range(min(k, n_back), 0, -1)]
            flash_tiles(win + [(kwp_ref, vwt_ref, k, diag_plan, win_state)])
            compressed_and_select(ng, k * TQ, cmp_scores)
            sel = [(ksp_ref, vst_ref, kt, None, sel_state) for kt in range(k)]
            flash_tiles(sel + [(ksp_ref, vst_ref, k, diag_plan, sel_state)])

    flash_mix(1, sel_state)
    flash_mix(2, win_state)
    for h2 in range(N_HEADS // 2):
        pair = jnp.concatenate([mix_ref[2 * h2], mix_ref[2 * h2 + 1]], axis=0)
        o_ref[:, h2 * SLOT:(h2 + 1) * SLOT] = pair.T.astype(BF16)


def _overlap(n_cmp_pad, n_slc):
    c0 = jnp.arange(n_cmp_pad) * CMP_STRIDE
    s0 = jnp.arange(n_slc) * SEL_LEN
    ov = jnp.minimum(c0[None, :] + CMP_LEN, s0[:, None] + SEL_LEN) - jnp.maximum(c0[None, :], s0[:, None])
    return jnp.clip(ov, 0).astype(F32) / CMP_LEN


def _nsa(q, gate, kc, vc, ks, vst, kw, vwt, batch, seq_len):
    n_slc = seq_len // SEL_LEN
    n_cmp_pad = kc.shape[1]
    w_kv_pad = N_KV * SLOT
    assert n_slc <= SLOT - BLK_LANE0 and n_slc % 8 == 0
    assert CHAIN_COLS % HALF == 0 and (GROUP * HALF) % CHAIN_COLS == 0 and KT == 2 * HALF
    ov = _overlap(n_cmp_pad, n_slc)
    nq = seq_len // TQ
    n_kt = seq_len // KT
    qrow = lambda b, i: (b * nq + i, 0)
    per_b = lambda b, i: (b, 0, 0)
    fixed = lambda b, i: (0, 0)
    blocks = (_nbytes((TQ, Q_WIDTH), BF16) + _nbytes((TQ, SLOT), F32)
              + 2 * _nbytes((n_cmp_pad, w_kv_pad), BF16) + 2 * _nbytes((seq_len, KV_WIDTH), BF16)
              + 2 * _nbytes((n_kt, VT_SLAB, KT), BF16)
              + _nbytes(ov.shape, F32) + _nbytes((TQ, Q_WIDTH), BF16))
    n_chains = N_KV * GROUP * TQ // CHAIN_COLS
    run_max = [pltpu.VMEM((1, CHAIN_COLS), F32)] * n_chains
    run_acc = [pltpu.VMEM((VT_ROWS, CHAIN_COLS), F32)] * n_chains
    scratch_shapes = (
        [pltpu.VMEM((seq_len, w_kv_pad), BF16),
         pltpu.VMEM((seq_len, w_kv_pad), BF16),
         pltpu.VMEM((N_KV, GROUP * TQ, SLOT), BF16),
         pltpu.VMEM((N_HEADS, HEAD_DIM, TQ), F32)]
        + run_max + run_acc + run_max + run_acc)
    scratch = sum(_nbytes(s.shape, s.dtype) for s in scratch_shapes)
    return pl.pallas_call(
        _nsa_kernel,
        grid=(batch, nq),
        in_specs=[
            pl.BlockSpec((TQ, Q_WIDTH), qrow),
            pl.BlockSpec((TQ, SLOT), qrow),
            pl.BlockSpec((1, n_cmp_pad, w_kv_pad), per_b),
            pl.BlockSpec((1, n_cmp_pad, w_kv_pad), per_b),
            pl.BlockSpec((1, seq_len, KV_WIDTH), per_b),
            pl.BlockSpec((n_kt, VT_SLAB, KT), per_b),
            pl.BlockSpec((1, seq_len, KV_WIDTH), per_b),
            pl.BlockSpec((n_kt, VT_SLAB, KT), per_b),
            pl.BlockSpec(ov.shape, fixed),
        ],
        out_specs=pl.BlockSpec((TQ, Q_WIDTH), qrow),
        out_shape=jax.ShapeDtypeStruct((batch * seq_len, Q_WIDTH), BF16),
        scratch_shapes=scratch_shapes,
        compiler_params=pltpu.CompilerParams(
            dimension_semantics=("arbitrary", "arbitrary"),
            vmem_limit_bytes=_vmem_limit(blocks, scratch)),
        name="nsa",
    )(q, gate, kc.reshape(batch, n_cmp_pad, w_kv_pad), vc.reshape(batch, n_cmp_pad, w_kv_pad),
      ks.reshape(batch, seq_len, KV_WIDTH), vst, kw.reshape(batch, seq_len, KV_WIDTH), vwt, ov)


def _pool_kernel(u_ref, w_ref, scale_ref, o_ref):
    seq = u_ref.shape[0]
    t = lax.broadcasted_iota(jnp.int32, (seq, POOL_GROUP), 0)

    def shifted(x, k):
        return jnp.where(t >= k, pltpu.roll(x, k, axis=0), 0.0)

    for gi, w in enumerate(POOL_WINDOWS):
        sl = slice(gi * POOL_GROUP, (gi + 1) * POOL_GROUP)
        x = u_ref[:, sl]
        wsum = x
        span = 1
        while span < w:
            wsum = wsum + shifted(wsum, span)
            span *= 2
        cnt = jnp.minimum(t + 1, w).astype(F32)
        pooled = (wsum / cnt - x).astype(BF16)
        o_ref[:, sl] = (_dot(pooled, w_ref[gi]) * scale_ref[:, sl]).astype(BF16)


def _pool(pool_in, pool_w, pool_scale, batch, seq_len):
    for w in POOL_WINDOWS:
        assert w & (w - 1) == 0
    blocks = (_nbytes((seq_len, POOL_WIDTH), F32) + _nbytes(pool_w.shape, BF16)
              + _nbytes((seq_len, POOL_WIDTH), BF16))
    return pl.pallas_call(
        _pool_kernel,
        grid=(batch,),
        in_specs=[
            pl.BlockSpec((seq_len, POOL_WIDTH), lambda b: (b, 0)),
            pl.BlockSpec(pool_w.shape, lambda b: (0, 0, 0)),
            pl.BlockSpec((1, POOL_WIDTH), lambda b: (0, 0)),
        ],
        out_specs=pl.BlockSpec((seq_len, POOL_WIDTH), lambda b: (b, 0)),
        out_shape=jax.ShapeDtypeStruct((batch * seq_len, POOL_WIDTH), BF16),
        compiler_params=pltpu.CompilerParams(
            dimension_semantics=("arbitrary",), vmem_limit_bytes=_vmem_limit(blocks)),
        name="pool",
    )(pool_in, pool_w.astype(BF16), pool_scale.reshape(1, -1))


def _merge_kernel(x_ref, gpre_ref, wt_ref, nsa_ref, wa_ref, pool_ref, wp_ref, wo_ref, gpost_ref,
                  o_ref):
    x = x_ref[...]
    hn = _rms(x, gpre_ref[...])
    g_attn = jax.nn.sigmoid(_dot_nt(hn, wt_ref[OFF_MERGE:OFF_MERGE + D_MODEL, :]))
    g_pool = jax.nn.sigmoid(_dot_nt(hn, wt_ref[OFF_MERGE + D_MODEL:OFF_MERGE + 2 * D_MODEL, :]))
    y = (g_attn * _dot(nsa_ref[...].astype(F32), wa_ref[...])
         + g_pool * _dot(pool_ref[...].astype(F32), wp_ref[...]))
    h = _dot(y, wo_ref[...])
    o_ref[...] = x + _rms(h, gpost_ref[...])


def _merge(x1, g_pre, w_in_t, o_nsa, w_attn, o_pool, w_pool, w_out, g_post):
    t = x1.shape[0]
    row = lambda i: (i, 0)
    fixed = lambda i: (0, 0)
    once = pl.Buffered(1)
    weights = (_nbytes(w_in_t.shape, F32) + _nbytes((Q_WIDTH, D_MODEL), F32)
               + _nbytes((POOL_WIDTH, D_MODEL), F32) + _nbytes((D_MODEL, D_MODEL), F32))
    blocks = (2 * _nbytes((TOK_TILE, D_MODEL), F32) + _nbytes((TOK_TILE, Q_WIDTH), BF16)
              + _nbytes((TOK_TILE, POOL_WIDTH), BF16))
    return pl.pallas_call(
        _merge_kernel,
        grid=(t // TOK_TILE,),
        in_specs=[
            pl.BlockSpec((TOK_TILE, D_MODEL), row),
            pl.BlockSpec((1, D_MODEL), fixed),
            pl.BlockSpec(w_in_t.shape, fixed, pipeline_mode=once),
            pl.BlockSpec((TOK_TILE, Q_WIDTH), row),
            pl.BlockSpec((Q_WIDTH, D_MODEL), fixed, pipeline_mode=once),
            pl.BlockSpec((TOK_TILE, POOL_WIDTH), row),
            pl.BlockSpec((POOL_WIDTH, D_MODEL), fixed, pipeline_mode=once),
            pl.BlockSpec((D_MODEL, D_MODEL), fixed, pipeline_mode=once),
            pl.BlockSpec((1, D_MODEL), fixed),
        ],
        out_specs=pl.BlockSpec((TOK_TILE, D_MODEL), row),
        out_shape=jax.ShapeDtypeStruct((t, D_MODEL), F32),
        compiler_params=pltpu.CompilerParams(
            dimension_semantics=("arbitrary",), vmem_limit_bytes=_vmem_limit(blocks, weights)),
        name="merge",
    )(x1, g_pre.reshape(1, -1), w_in_t, o_nsa, w_attn, o_pool, w_pool, w_out, g_post.reshape(1, -1))


def _layer(x, tab, tab_cmp, batch, seq_len, g_ffn1_pre, w_ffn1_gate, w_ffn1_up, w_ffn1_down,
           g_ffn1_post, g_mix_pre, w_in, cmp_pe_k, cmp_w1_k, cmp_w2_k, cmp_pe_v, cmp_w1_v, cmp_w2_v,
           w_attn_branch, pool_w, pool_scale, w_pool_branch, w_out, g_mix_post, g_ffn2_pre,
           w_ffn2_gate, w_ffn2_up, w_ffn2_down, g_ffn2_post):
    x1 = _ffn(x, g_ffn1_pre, w_ffn1_gate, w_ffn1_up, w_ffn1_down, g_ffn1_post)
    w_in_t = w_in.T
    q, ks, kw, vst, vwt, cmp_rows, gate, pool_in = _in_proj(x1, g_mix_pre, w_in_t, tab)
    kc, vc = _compress(cmp_rows, cmp_pe_k, cmp_w1_k, cmp_w2_k, cmp_pe_v, cmp_w1_v, cmp_w2_v, tab_cmp)
    n_cmp_pad = seq_len // CMP_STRIDE
    kc = kc.reshape(batch, n_cmp_pad, N_KV * SLOT)
    vc = vc.reshape(batch, n_cmp_pad, N_KV * SLOT)
    o_nsa = _nsa(q, gate, kc, vc, ks, vst, kw, vwt, batch, seq_len)
    o_pool = _pool(pool_in, pool_w, pool_scale, batch, seq_len)
    x2 = _merge(x1, g_mix_pre, w_in_t, o_nsa, w_attn_branch, o_pool, w_pool_branch, w_out,
                g_mix_post)
    return _ffn(x2, g_ffn2_pre, w_ffn2_gate, w_ffn2_up, w_ffn2_down, g_ffn2_post)


def kernel(x, positions, g_ffn1_pre, w_ffn1_gate, w_ffn1_up, w_ffn1_down, g_ffn1_post, g_mix_pre, w_in, cmp_pe_k, cmp_w1_k, cmp_w2_k, cmp_pe_v, cmp_w1_v, cmp_w2_v, w_attn_branch, pool_w, pool_scale, w_pool_branch, w_out, g_mix_post, g_ffn2_pre, w_ffn2_gate, w_ffn2_up, w_ffn2_down, g_ffn2_post):
    batch, seq_len, d_model = x.shape
    assert d_model == D_MODEL and seq_len % TOK_TILE == 0 and seq_len % TQ == 0
    assert TQ == KT and TQ % SEL_LEN == 0 and TOK_TILE % KT == 0 and WINDOW % KT == 0
    assert seq_len % CMP_STRIDE == 0 and CMP_LEN == 2 * CMP_STRIDE
    t = batch * seq_len
    tab = _rope_tables(positions.reshape(t), seq_len)
    n_cmp = (seq_len - CMP_LEN) // CMP_STRIDE + 1
    n_cmp_pad = seq_len // CMP_STRIDE
    pos_cmp = jnp.pad(positions[:, CMP_LEN - 1::CMP_STRIDE], ((0, 0), (0, n_cmp_pad - n_cmp)))
    tab_cmp = _rope_tables(pos_cmp.reshape(batch * n_cmp_pad), batch * n_cmp_pad)
    xf = x.reshape(t, D_MODEL)
    per_layer = (g_ffn1_pre, w_ffn1_gate, w_ffn1_up, w_ffn1_down, g_ffn1_post, g_mix_pre, w_in,
                 cmp_pe_k, cmp_w1_k, cmp_w2_k, cmp_pe_v, cmp_w1_v, cmp_w2_v, w_attn_branch, pool_w,
                 pool_scale, w_pool_branch, w_out, g_mix_post, g_ffn2_pre, w_ffn2_gate, w_ffn2_up,
                 w_ffn2_down, g_ffn2_post)
    for l in range(g_ffn1_pre.shape[0]):
        xf = _layer(xf, tab, tab_cmp, batch, seq_len, *(p[l] for p in per_layer))
    return xf.reshape(batch, seq_len, D_MODEL)
```

```python
import functools
import math

import jax
import jax.numpy as jnp
from jax import lax
from jax.experimental import pallas as pl
from jax.experimental.pallas import tpu as pltpu

F32 = jnp.float32
BF16 = jnp.bfloat16

D_MODEL = 1024
N_HEADS = 16
HEAD_DIM = 64
N_KV = 4
GROUP = N_HEADS // N_KV
ROT_DIM = HEAD_DIM // 4
ROT_HALF = ROT_DIM // 2
ROPE_THETA = 500000.0
CMP_LEN = 32
CMP_STRIDE = 16
CMP_HIDDEN = 2 * HEAD_DIM
SEL_LEN = 64
SEL_SHIFT = SEL_LEN.bit_length() - 1
N_SEL = 8
WINDOW = 512
POOL_WINDOWS = (2, 4, 8, 16)
POOL_WIDTH = D_MODEL // 2
POOL_GROUP = POOL_WIDTH // len(POOL_WINDOWS)
D_FF = 2816
EPS = 1e-6
NEG_INF = -1e30
FORCE_SCORE = 1e4
Q_WIDTH = N_HEADS * HEAD_DIM
KV_WIDTH = N_KV * HEAD_DIM
N_GATES = 3 * N_HEADS
LOG2_E = math.log2(math.e)

LANES = 128
BF16_SUBLANES = 16
V7X_VMEM_BYTES = 64 * 1024 * 1024

SLOT = LANES
BLK_LANE0 = HEAD_DIM
VT_ROWS = HEAD_DIM + BF16_SUBLANES
ONE_ROW = HEAD_DIM
TOK_TILE = 512
TQ = 256
KT = 256
HALF = TQ // 2
CHAIN_COLS = 256
PIPE_DEPTH = 10
FF_CHUNK = 256
IN_SPLIT = 2
FFN_SPLIT = 2
N_ROPE_TAB = 2


def _vmem_limit(block_bytes, scratch_bytes=0):
    need = 2 * block_bytes + scratch_bytes
    return int(min(V7X_VMEM_BYTES - (4 << 20), max(2 * need, 32 << 20)))


def _nbytes(shape, dtype):
    n = 1
    for s in shape:
        n *= s
    return n * jnp.dtype(dtype).itemsize


def _rms(xf, g):
    return xf * lax.rsqrt(jnp.mean(xf * xf, axis=-1, keepdims=True) + EPS) * g


def _dot(a, b):
    return jnp.dot(a, b, preferred_element_type=F32)


def _dot_nt(a, b):
    return lax.dot_general(a, b, (((1,), (1,)), ((), ())), preferred_element_type=F32)


def _rope_trig_kernel(pos_ref, inv_ref, tab_ref):
    tn = pos_ref.shape[1]
    ang = pos_ref[...].astype(F32) * inv_ref[...]
    c, s = jnp.cos(ang), jnp.sin(ang)
    rest = HEAD_DIM - ROT_DIM
    reps = SLOT // HEAD_DIM
    cos_rows = jnp.concatenate([c, c, jnp.ones((rest, tn), F32)] * reps, axis=0)
    sin_rows = jnp.concatenate([-s, s, jnp.zeros((rest, tn), F32)] * reps, axis=0)
    for j in range(tn // LANES):
        cs = slice(j * LANES, (j + 1) * LANES)
        tab_ref[cs, 0:SLOT] = cos_rows[:, cs].T
        tab_ref[cs, SLOT:2 * SLOT] = sin_rows[:, cs].T


def _rope_tables(pos_flat, tile):
    n = pos_flat.shape[0]
    inv = ROPE_THETA ** (-jnp.arange(ROT_HALF, dtype=F32) * (2.0 / ROT_DIM))
    return pl.pallas_call(
        _rope_trig_kernel,
        grid=(n // tile,),
        in_specs=[pl.BlockSpec((1, tile), lambda i: (0, i)),
                  pl.BlockSpec((ROT_HALF, 1), lambda i: (0, 0))],
        out_specs=pl.BlockSpec((tile, N_ROPE_TAB * SLOT), lambda i: (i, 0)),
        out_shape=jax.ShapeDtypeStruct((n, N_ROPE_TAB * SLOT), F32),
        compiler_params=pltpu.CompilerParams(dimension_semantics=("arbitrary",)),
        name="rope_trig",
    )(pos_flat.reshape(1, n), inv.reshape(ROT_HALF, 1))


def _rope_slot(y, tab):
    cos_t = tab[:, 0:SLOT]
    sin_t = tab[:, SLOT:2 * SLOT]
    lane = lax.broadcasted_iota(jnp.int32, y.shape, 1) & (HEAD_DIM - 1)
    up = pltpu.roll(y, SLOT - ROT_HALF, axis=1)
    down = pltpu.roll(y, ROT_HALF, axis=1)
    return y * cos_t + jnp.where(lane < ROT_HALF, up, down) * sin_t


def _ffn_kernel(x_ref, gpre_ref, wg_ref, wu_ref, wd_ref, gpost_ref, o_ref):
    rows = x_ref.shape[0] // FFN_SPLIT
    parts = [slice(k * rows, (k + 1) * rows) for k in range(FFN_SPLIT)]
    hn = [_rms(x_ref[rs, :], gpre_ref[...]) for rs in parts]
    for k, rs in enumerate(parts):
        for c in range(D_FF // FF_CHUNK):
            sl = slice(c * FF_CHUNK, (c + 1) * FF_CHUNK)
            g = _dot(hn[k], wg_ref[:, sl])
            u = _dot(hn[k], wu_ref[:, sl])
            d = _dot(g * jax.nn.sigmoid(g) * u, wd_ref[sl, :])
            if c == 0:
                o_ref[rs, :] = d
            else:
                o_ref[rs, :] += d
        o_ref[rs, :] = x_ref[rs, :] + 0.5 * _rms(o_ref[rs, :], gpost_ref[...])


def _ffn(x, g_pre, w_gate, w_up, w_down, g_post):
    t = x.shape[0]
    tile = FFN_SPLIT * TOK_TILE
    row = lambda i: (i, 0)
    fixed = lambda i: (0, 0)
    once = pl.Buffered(1)
    weights = 3 * _nbytes((D_MODEL, D_FF), F32)
    blocks = 2 * _nbytes((tile, D_MODEL), F32)
    return pl.pallas_call(
        _ffn_kernel,
        grid=(t // tile,),
        in_specs=[
            pl.BlockSpec((tile, D_MODEL), row),
            pl.BlockSpec((1, D_MODEL), fixed),
            pl.BlockSpec((D_MODEL, D_FF), fixed, pipeline_mode=once),
            pl.BlockSpec((D_MODEL, D_FF), fixed, pipeline_mode=once),
            pl.BlockSpec((D_FF, D_MODEL), fixed, pipeline_mode=once),
            pl.BlockSpec((1, D_MODEL), fixed),
        ],
        out_specs=pl.BlockSpec((tile, D_MODEL), row),
        out_shape=jax.ShapeDtypeStruct((t, D_MODEL), F32),
        compiler_params=pltpu.CompilerParams(
            dimension_semantics=("arbitrary",), vmem_limit_bytes=_vmem_limit(blocks, weights)),
        name="ffn",
    )(x, g_pre.reshape(1, -1), w_gate, w_up, w_down, g_post.reshape(1, -1))


OFF_Q = 0
OFF_CMP = OFF_Q + Q_WIDTH
OFF_KS = OFF_CMP + 2 * KV_WIDTH
OFF_VS = OFF_KS + KV_WIDTH
OFF_KW = OFF_VS + KV_WIDTH
OFF_VW = OFF_KW + KV_WIDTH
OFF_GATE = OFF_VW + KV_WIDTH
OFF_POOL = OFF_GATE + N_GATES
OFF_MERGE = OFF_POOL + POOL_WIDTH
W_CMP = 2 * KV_WIDTH
CHUNK_TOK = CMP_STRIDE
VT_SLAB = N_KV * VT_ROWS


def _in_proj_kernel(x_ref, g_ref, w_ref, tab_ref, q_ref, ks_ref, kw_ref, vst_ref,
                    vwt_ref, cmp_ref, gate_ref, pool_ref, cmp_scr):
    rows = x_ref.shape[0] // IN_SPLIT
    wide = 4 * SLOT
    aux = (lax.broadcasted_iota(jnp.int32, (VT_ROWS - HEAD_DIM, KT), 0) == 0).astype(BF16)

    def part(k):
        rs = slice(k * rows, (k + 1) * rows)
        hn = _rms(x_ref[rs, :], g_ref[...])
        tab = tab_ref[rs, :]
        tab_q = tab * (HEAD_DIM ** -0.5 * LOG2_E)
        for c in range(Q_WIDTH // wide):
            y = _dot_nt(hn, w_ref[OFF_Q + c * wide:OFF_Q + (c + 1) * wide, :])
            for j in range(wide // SLOT):
                sl = slice(c * wide + j * SLOT, c * wide + (j + 1) * SLOT)
                q_ref[rs, sl] = _rope_slot(y[:, j * SLOT:(j + 1) * SLOT], tab_q).astype(BF16)
        for k_ref, off in ((ks_ref, OFF_KS), (kw_ref, OFF_KW)):
            y = _dot_nt(hn, w_ref[off:off + KV_WIDTH, :])
            for j in range(KV_WIDTH // SLOT):
                sl = slice(j * SLOT, (j + 1) * SLOT)
                k_ref[rs, sl] = _rope_slot(y[:, sl], tab).astype(BF16)
        y = _dot_nt(hn, w_ref[OFF_CMP:OFF_CMP + W_CMP, :])
        for j in range(W_CMP // LANES):
            cmp_scr[j, rs, :] = y[:, j * LANES:(j + 1) * LANES]
        chunk_rows = rows // CHUNK_TOK
        for l in range(CHUNK_TOK):
            for j in range(W_CMP // LANES):
                c0 = l * W_CMP + j * LANES
                cmp_ref[k * chunk_rows:(k + 1) * chunk_rows, c0:c0 + LANES] = (
                    cmp_scr[j, pl.ds(k * rows + l, chunk_rows, stride=CHUNK_TOK), :].astype(BF16))
        gate_ref[rs, :] = _dot_nt(hn, w_ref[OFF_GATE:OFF_GATE + SLOT, :])
        pool_ref[rs, :] = _dot_nt(hn, w_ref[OFF_POOL:OFF_MERGE, :])
        tiles = rows // KT
        for vt_ref, off in ((vst_ref, OFF_VS), (vwt_ref, OFF_VW)):
            vt = _dot_nt(w_ref[off:off + KV_WIDTH, :], hn).astype(BF16)
            for j in range(tiles):
                for g in range(N_KV):
                    vt_ref[k * tiles + j, g * VT_ROWS:g * VT_ROWS + HEAD_DIM, :] = (
                        vt[g * HEAD_DIM:(g + 1) * HEAD_DIM, j * KT:(j + 1) * KT])
                    vt_ref[k * tiles + j, g * VT_ROWS + HEAD_DIM:(g + 1) * VT_ROWS, :] = aux

    for k in range(IN_SPLIT):
        part(k)


def _in_proj(x1, g_pre, w_in_t, tab):
    t = x1.shape[0]
    row = lambda i: (i, 0)
    fixed = lambda i: (0, 0)
    once = pl.Buffered(1)
    tile = IN_SPLIT * TOK_TILE
    tiles = tile // KT
    chunk_rows = tile // CHUNK_TOK
    slab = pl.BlockSpec((tiles, VT_SLAB, KT), lambda i: (i, 0, 0))
    slab_shape = jax.ShapeDtypeStruct((t // KT, VT_SLAB, KT), BF16)
    out_specs = [
        pl.BlockSpec((tile, Q_WIDTH), row), pl.BlockSpec((tile, KV_WIDTH), row),
        pl.BlockSpec((tile, KV_WIDTH), row), slab, slab,
        pl.BlockSpec((chunk_rows, CHUNK_TOK * W_CMP), row),
        pl.BlockSpec((tile, SLOT), row), pl.BlockSpec((tile, POOL_WIDTH), row)]
    out_shape = [
        jax.ShapeDtypeStruct((t, Q_WIDTH), BF16), jax.ShapeDtypeStruct((t, KV_WIDTH), BF16),
        jax.ShapeDtypeStruct((t, KV_WIDTH), BF16), slab_shape, slab_shape,
        jax.ShapeDtypeStruct((t // CHUNK_TOK, CHUNK_TOK * W_CMP), BF16),
        jax.ShapeDtypeStruct((t, SLOT), F32), jax.ShapeDtypeStruct((t, POOL_WIDTH), F32)]
    weights = _nbytes(w_in_t.shape, F32)
    blocks = (_nbytes((tile, D_MODEL), F32) + _nbytes((tile, N_ROPE_TAB * SLOT), F32)
              + _nbytes((tile, Q_WIDTH + 2 * KV_WIDTH + W_CMP), BF16)
              + 2 * _nbytes((tiles, VT_SLAB, KT), BF16) + _nbytes((tile, SLOT + POOL_WIDTH), F32))
    res = pl.pallas_call(
        _in_proj_kernel,
        grid=(t // tile,),
        in_specs=[
            pl.BlockSpec((tile, D_MODEL), row),
            pl.BlockSpec((1, D_MODEL), fixed),
            pl.BlockSpec(w_in_t.shape, fixed, pipeline_mode=once),
            pl.BlockSpec((tile, N_ROPE_TAB * SLOT), row),
        ],
        out_specs=out_specs,
        out_shape=out_shape,
        scratch_shapes=[pltpu.VMEM((W_CMP // LANES, tile, LANES), F32)],
        compiler_params=pltpu.CompilerParams(
            dimension_semantics=("arbitrary",),
            vmem_limit_bytes=_vmem_limit(blocks, weights + _nbytes((tile, W_CMP), F32))),
        name="in_proj",
    )(x1, g_pre.reshape(1, -1), w_in_t, tab)
    return res


def _compress_kernel(x_ref, w1k_ref, w1v_ref, pek_ref, pev_ref, w1k_raw_ref, w1v_raw_ref,
                     w2k_ref, w2v_ref, tab_ref, kc_ref, vc_ref, acck_ref, accv_ref):
    l = pl.program_id(0)

    @pl.when(l == 0)
    def _():
        acck_ref[...] = jnp.zeros(acck_ref.shape, F32)
        accv_ref[...] = jnp.zeros(accv_ref.shape, F32)

    for acc_ref, w1_ref, c0 in ((acck_ref, w1k_ref, 0), (accv_ref, w1v_ref, KV_WIDTH)):
        for g in range(N_KV):
            acc_ref[g] += _dot(x_ref[:, c0 + g * HEAD_DIM:c0 + (g + 1) * HEAD_DIM], w1_ref[0])

    @pl.when(l == CHUNK_TOK - 1)
    def _():
        rows = acck_ref.shape[1]
        tab = tab_ref[...]
        for acc_ref, pe_ref, w1_raw_ref, w2_ref, o_ref, rope in (
                (acck_ref, pek_ref, w1k_raw_ref, w2k_ref, kc_ref, True),
                (accv_ref, pev_ref, w1v_raw_ref, w2v_ref, vc_ref, False)):
            bias = _dot(pe_ref[...], w1_raw_ref[...])[0:1, :]
            for g in range(N_KV):
                first = acc_ref[g, :, 0:CMP_HIDDEN]
                second = pltpu.roll(acc_ref[g, :, CMP_HIDDEN:2 * CMP_HIDDEN], rows - 1, axis=0)
                hid = jax.nn.gelu(first + second + bias).astype(BF16)
                out = _dot(hid, w2_ref[...])
                if rope:
                    out = _rope_slot(out, tab)
                o_ref[:, g * SLOT:(g + 1) * SLOT] = out.astype(BF16)


def _compress_weights(w1, w2):
    w1 = w1.reshape(2, CHUNK_TOK, HEAD_DIM, CMP_HIDDEN)
    w1_tok = jnp.concatenate([w1[0], w1[1]], axis=-1).astype(BF16)
    return w1_tok, jnp.pad(w2, ((0, 0), (0, SLOT - HEAD_DIM))).astype(BF16)


def _compress(cmp_rows, cmp_pe_k, cmp_w1_k, cmp_w2_k, cmp_pe_v, cmp_w1_v, cmp_w2_v, tab_cmp):
    rows = cmp_rows.shape[0]
    w1k, w2k = _compress_weights(cmp_w1_k, cmp_w2_k)
    w1v, w2v = _compress_weights(cmp_w1_v, cmp_w2_v)
    sub = 8
    pek = jnp.broadcast_to(cmp_pe_k.reshape(1, -1), (sub, CMP_LEN * HEAD_DIM)).astype(BF16)
    pev = jnp.broadcast_to(cmp_pe_v.reshape(1, -1), (sub, CMP_LEN * HEAD_DIM)).astype(BF16)
    fixed = lambda l: (0, 0)
    w_kv_pad = N_KV * SLOT
    blocks = (_nbytes((rows, W_CMP), BF16) + 2 * _nbytes((HEAD_DIM, 2 * CMP_HIDDEN), BF16)
              + 2 * _nbytes((sub + CMP_LEN * HEAD_DIM, CMP_HIDDEN), BF16)
              + 2 * _nbytes((CMP_HIDDEN, SLOT), BF16)
              + _nbytes((rows, N_ROPE_TAB * SLOT), F32) + 2 * _nbytes((rows, w_kv_pad), BF16))
    acc = pltpu.VMEM((N_KV, rows, 2 * CMP_HIDDEN), F32)
    return pl.pallas_call(
        _compress_kernel,
        grid=(CHUNK_TOK,),
        in_specs=[
            pl.BlockSpec((rows, W_CMP), lambda l: (0, l)),
            pl.BlockSpec((1, HEAD_DIM, 2 * CMP_HIDDEN), lambda l: (l, 0, 0)),
            pl.BlockSpec((1, HEAD_DIM, 2 * CMP_HIDDEN), lambda l: (l, 0, 0)),
            pl.BlockSpec((sub, CMP_LEN * HEAD_DIM), fixed),
            pl.BlockSpec((sub, CMP_LEN * HEAD_DIM), fixed),
            pl.BlockSpec((CMP_LEN * HEAD_DIM, CMP_HIDDEN), fixed),
            pl.BlockSpec((CMP_LEN * HEAD_DIM, CMP_HIDDEN), fixed),
            pl.BlockSpec((CMP_HIDDEN, SLOT), fixed),
            pl.BlockSpec((CMP_HIDDEN, SLOT), fixed),
            pl.BlockSpec((rows, N_ROPE_TAB * SLOT), fixed),
        ],
        out_specs=[pl.BlockSpec((rows, w_kv_pad), fixed)] * 2,
        out_shape=[jax.ShapeDtypeStruct((rows, w_kv_pad), BF16)] * 2,
        scratch_shapes=[acc, acc],
        compiler_params=pltpu.CompilerParams(
            dimension_semantics=("arbitrary",),
            vmem_limit_bytes=_vmem_limit(blocks, 2 * _nbytes(acc.shape, F32))),
        name="compress",
    )(cmp_rows, w1k, w1v, pek, pev, cmp_w1_k.astype(BF16), cmp_w1_v.astype(BF16), w2k, w2v, tab_cmp)


def _nsa_kernel(q_ref, gate_ref, kc_ref, vc_ref, ks_ref, vst_ref, kw_ref, vwt_ref, ov_ref,
                o_ref, ksp_ref, kwp_ref, qg_ref, mix_ref, *state_refs):
    i = pl.program_id(1)
    seq = ks_ref.shape[1]
    cols_g = GROUP * TQ
    chains_g = cols_g // CHAIN_COLS
    n_chains = N_KV * chains_g
    n_cmp = kc_ref.shape[1]
    sel_state = (state_refs[0:n_chains], state_refs[n_chains:2 * n_chains])
    win_state = (state_refs[2 * n_chains:3 * n_chains], state_refs[3 * n_chains:4 * n_chains])
    lane_q = lax.broadcasted_iota(jnp.int32, (TQ, SLOT), 1)
    lane_h = lax.broadcasted_iota(jnp.int32, (HALF, SLOT), 1)

    def head_slot(x_ref, rows, j):
        two = x_ref[rows, (j // 2) * SLOT:(j // 2 + 1) * SLOT]
        if j % 2:
            words = pltpu.roll(pltpu.bitcast(two, jnp.uint32), HEAD_DIM, axis=1)
            two = pltpu.bitcast(words, BF16)
        return two

    @pl.when(i == 0)
    def _():
        for c in range(seq // KT):
            rows = pl.ds(c * KT, KT)
            lane = lax.broadcasted_iota(jnp.int32, (KT, SLOT), 1)
            key_blk = (c * KT + lax.broadcasted_iota(jnp.int32, (KT, SLOT), 0)) >> SEL_SHIFT
            onehot = jnp.where(lane - BLK_LANE0 == key_blk, 1.0, 0.0).astype(BF16)
            zeros = jnp.zeros((KT, SLOT), BF16)
            for g in range(N_KV):
                gs = slice(g * SLOT, (g + 1) * SLOT)
                ksp_ref[rows, gs] = jnp.where(lane < HEAD_DIM, head_slot(ks_ref.at[0], rows, g), onehot)
                kwp_ref[rows, gs] = jnp.where(lane < HEAD_DIM, head_slot(kw_ref.at[0], rows, g), zeros)

    def q_rows(r, x):
        return slice((x * GROUP + r) * HALF, (x * GROUP + r + 1) * HALF)

    def chain_half(c):
        return c * CHAIN_COLS // (GROUP * HALF)

    for h in range(N_HEADS):
        g, r = divmod(h, GROUP)
        q_pad = jnp.where(lane_q < HEAD_DIM, head_slot(q_ref, slice(None), h), jnp.zeros((TQ, SLOT), BF16))
        for x in range(2):
            qg_ref[g, q_rows(r, x), :] = q_pad[x * HALF:(x + 1) * HALF]

    gates_t = jax.nn.sigmoid(gate_ref[...]).T
    key_h = lax.broadcasted_iota(jnp.int32, (HALF, CHAIN_COLS), 0)
    qry_h = lax.broadcasted_iota(jnp.int32, (HALF, CHAIN_COLS), 1) & (HALF - 1)
    tri_le, tri_gt = key_h <= qry_h, key_h > qry_h
    diag_plan = ((slice(0, HALF), tri_le, slice(0, HALF)), (slice(0, KT), tri_le, slice(HALF, KT)))
    far_plan = ((slice(0, KT), tri_gt, slice(0, HALF)), (slice(HALF, KT), tri_gt, slice(0, HALF)))

    def gate_row(branch, h):
        c = branch * N_HEADS + h
        return gates_t[c:c + 1, :]

    sub = 8
    cmp_per_sub = sub * SEL_LEN // CMP_STRIDE

    def compressed_scores(ng):
        n_c = ng * cmp_per_sub
        cmp_scores = []
        for h in range(N_HEADS):
            g, r = divmod(h, GROUP)
            q_h = jnp.concatenate([qg_ref[g, q_rows(r, 0), :], qg_ref[g, q_rows(r, 1), :]], axis=0)
            cmp_scores.append(_dot_nt(kc_ref[0, 0:n_c, g * SLOT:(g + 1) * SLOT], q_h))
        return cmp_scores

    def compressed_and_select(ng, start, cmp_scores):
        n_blk, n_c = ng * sub, ng * cmp_per_sub
        t_cmp = start + lax.broadcasted_iota(jnp.int32, (n_c, TQ), 1)
        n_idx = lax.broadcasted_iota(jnp.int32, (n_c, TQ), 0)
        cmp_valid = n_idx * CMP_STRIDE + (CMP_LEN - 1) <= t_cmp
        any_cmp = start + lax.broadcasted_iota(jnp.int32, (1, TQ), 1) >= CMP_LEN - 1
        t_row = start + lax.broadcasted_iota(jnp.int32, (n_blk, TQ), 1)
        blk = lax.broadcasted_iota(jnp.int32, (n_blk, TQ), 0)
        forced = (blk == t_row >> SEL_SHIFT) | (blk == 0)
        causal_blk = blk * SEL_LEN <= t_row
        idx8 = lax.broadcasted_iota(jnp.int32, (sub, TQ), 0)
        pad_c = jnp.zeros((n_cmp - n_c, TQ), F32)
        for g in range(N_KV):
            vct_g = vc_ref[0, :, g * SLOT:(g + 1) * SLOT].astype(F32).T[0:HEAD_DIM].astype(BF16)
            p_sum = jnp.zeros((n_c, TQ), F32)
            for r in range(GROUP):
                h = g * GROUP + r
                s = jnp.where(cmp_valid, cmp_scores[h], NEG_INF)
                p = jnp.exp2(s - jnp.max(s, axis=0, keepdims=True))
                inv = jnp.where(any_cmp, 1.0 / jnp.sum(p, axis=0, keepdims=True), 0.0)
                p = p * inv
                p_sum = p_sum + p
                p_all = jnp.concatenate([p, pad_c], axis=0) if ng * cmp_per_sub < n_cmp else p
                mix_ref[h] = gate_row(0, h) * _dot(vct_g, p_all.astype(BF16))
            p_sum_all = jnp.concatenate([p_sum, pad_c], axis=0) if ng * cmp_per_sub < n_cmp else p_sum
            imp = jnp.dot(ov_ref[0:n_blk, :], p_sum_all, precision=lax.Precision.HIGHEST,
                          preferred_element_type=F32)
            score = jnp.where(causal_blk, jnp.where(forced, FORCE_SCORE, imp), NEG_INF)
            rows8 = [score[k * sub:(k + 1) * sub] for k in range(ng)]
            ranks = [jnp.zeros((sub, TQ), jnp.int32) for _ in rows8]
            for c in range(n_blk):
                other = score[c:c + 1, :]
                for k, mine in enumerate(rows8):
                    if k * sub > c:
                        beats = other >= mine
                    elif (k + 1) * sub - 1 <= c:
                        beats = other > mine
                    else:
                        beats = (other > mine) | ((other == mine) & (idx8 > c - k * sub))
                    ranks[k] = ranks[k] + beats.astype(jnp.int32)
            rank = jnp.concatenate(ranks, axis=0) if ng > 1 else ranks[0]
            bias = jnp.where(causal_blk & (rank < N_SEL), 0.0, NEG_INF)
            bias_t = jnp.concatenate([jnp.zeros((BLK_LANE0, TQ), F32), bias,
                                      jnp.zeros((SLOT - BLK_LANE0 - n_blk, TQ), F32)], axis=0)
            bias_q = bias_t.T.astype(BF16)
            for r in range(GROUP):
                for x in range(2):
                    rs = q_rows(r, x)
                    qg_ref[g, rs, :] = jnp.where(lane_h < HEAD_DIM, qg_ref[g, rs, :],
                                                 bias_q[x * HALF:(x + 1) * HALF])

    ng_now = ((i + 1) * TQ + sub * SEL_LEN - 1) // (sub * SEL_LEN)
    for ng in range(1, seq // (sub * SEL_LEN) + 1):

        @pl.when(ng_now == ng)
        def _(ng=ng):
            compressed_and_select(ng, i * TQ, compressed_scores(ng))

    def flash_init(state):
        for m_ref, acc_ref in zip(*state):
            m_ref[...] = jnp.full(m_ref.shape, NEG_INF, F32)
            acc_ref[...] = jnp.zeros(acc_ref.shape, F32)

    def flash_tiles(tiles):
        jobs = [(tile, c) for tile in tiles for c in range(n_chains)]

        def key_plan(job):
            (_, _, _, plan, _), chain = job
            return (slice(0, KT), None, None) if plan is None else plan[chain_half(chain % chains_g)]

        scores = {}
        for step in range(len(jobs) + PIPE_DEPTH):
            if step < len(jobs):
                (kp_ref, _, kt, _, _), chain = jobs[step]
                g, c = divmod(chain, chains_g)
                keys, mask, rows = key_plan(jobs[step])
                off = pl.multiple_of(kt * KT + keys.start, HALF)
                s = _dot_nt(kp_ref[pl.ds(off, keys.stop - keys.start), g * SLOT:(g + 1) * SLOT],
                            qg_ref[g, c * CHAIN_COLS:(c + 1) * CHAIN_COLS, :])
                if mask is not None:
                    parts = [s[0:rows.start], jnp.where(mask, s[rows], NEG_INF), s[rows.stop:]]
                    s = jnp.concatenate([p for p in parts if p.shape[0]], axis=0)
                scores[step] = s.astype(BF16)
            done = step - PIPE_DEPTH
            if done >= 0:
                (_, vt_ref, kt, _, (m_refs, acc_refs)), chain = jobs[done]
                g = chain // chains_g
                keys = key_plan(jobs[done])[0]
                m_ref, acc_ref = m_refs[chain], acc_refs[chain]
                s = scores.pop(done)
                m_old = m_ref[...]
                packed = [s[r * BF16_SUBLANES:(r + 1) * BF16_SUBLANES]
                          for r in range(s.shape[0] // BF16_SUBLANES)]
                m_tile = functools.reduce(jnp.maximum, packed).astype(F32)
                m_new = jnp.maximum(m_old, jnp.max(m_tile, axis=0, keepdims=True))
                p = jnp.exp2(s - m_new.astype(BF16))
                pv = _dot(vt_ref[kt, g * VT_ROWS:(g + 1) * VT_ROWS, keys], p)
                acc_ref[...] = jnp.exp2(m_old - m_new) * acc_ref[...] + pv
                m_ref[...] = m_new

    def flash_mix(branch, state):
        _, acc_refs = state
        for h in range(N_HEADS):
            g, r = divmod(h, GROUP)
            gate = gate_row(branch, h)
            for x in range(2):
                c, c0 = divmod(q_rows(r, x).start, CHAIN_COLS)
                acc = acc_refs[g * chains_g + c][:, c0:c0 + HALF]
                qs = slice(x * HALF, (x + 1) * HALF)
                scale = gate[:, qs] * (1.0 / acc[ONE_ROW:ONE_ROW + 1, :])
                mix_ref[h, :, qs] = mix_ref[h, :, qs] + scale * acc[0:HEAD_DIM]

    flash_init(sel_state)
    flash_init(win_state)

    def sel_past(kt):
        return (ksp_ref, vst_ref, kt, None, sel_state)

    def sel_pair(j, carry):
        flash_tiles([sel_past(2 * j), sel_past(2 * j + 1)])
        return carry

    lax.fori_loop(0, i >> 1, sel_pair, 0)

    @pl.when(i & 1 == 1)
    def _():
        flash_tiles([sel_past(i - 1)])

    n_back = WINDOW // KT
    sel_diag = (ksp_ref, vst_ref, i, diag_plan, sel_state)
    win_diag = (kwp_ref, vwt_ref, i, diag_plan, win_state)

    def win_back(d):
        return (kwp_ref, vwt_ref, i - d, far_plan if d == n_back else None, win_state)

    for have in range(n_back + 1):
        cond = (i == have) if have < n_back else (i >= have)

        @pl.when(cond)
        def _(have=have):
            flash_tiles([sel_diag] + [win_back(d) for d in range(have, 0, -1)] + [win_diag])

    flash_mix(1, sel_state)
    flash_mix(2, win_state)
    for h2 in range(N_HEADS // 2):
        pair = jnp.concatenate([mix_ref[2 * h2], mix_ref[2 * h2 + 1]], axis=0)
        o_ref[:, h2 * SLOT:(h2 + 1) * SLOT] = pair.T.astype(BF16)


def _overlap(n_cmp_pad, n_slc):
    c0 = jnp.arange(n_cmp_pad) * CMP_STRIDE
    s0 = jnp.arange(n_slc) * SEL_LEN
    ov = jnp.minimum(c0[None, :] + CMP_LEN, s0[:, None] + SEL_LEN) - jnp.maximum(c0[None, :], s0[:, None])
    return jnp.clip(ov, 0).astype(F32) / CMP_LEN


def _nsa(q, gate, kc, vc, ks, vst, kw, vwt, batch, seq_len):
    n_slc = seq_len // SEL_LEN
    n_cmp_pad = kc.shape[1]
    w_kv_pad = N_KV * SLOT
    assert n_slc <= SLOT - BLK_LANE0 and n_slc % 8 == 0
    assert CHAIN_COLS % HALF == 0 and (GROUP * HALF) % CHAIN_COLS == 0 and KT == 2 * HALF
    ov = _overlap(n_cmp_pad, n_slc)
    nq = seq_len // TQ
    n_kt = seq_len // KT
    qrow = lambda b, i: (b * nq + i, 0)
    per_b = lambda b, i: (b, 0, 0)
    fixed = lambda b, i: (0, 0)
    blocks = (_nbytes((TQ, Q_WIDTH), BF16) + _nbytes((TQ, SLOT), F32)
              + 2 * _nbytes((n_cmp_pad, w_kv_pad), BF16) + 2 * _nbytes((seq_len, KV_WIDTH), BF16)
              + 2 * _nbytes((n_kt, VT_SLAB, KT), BF16)
              + _nbytes(ov.shape, F32) + _nbytes((TQ, Q_WIDTH), BF16))
    n_chains = N_KV * GROUP * TQ // CHAIN_COLS
    run_max = [pltpu.VMEM((1, CHAIN_COLS), F32)] * n_chains
    run_acc = [pltpu.VMEM((VT_ROWS, CHAIN_COLS), F32)] * n_chains
    scratch_shapes = (
        [pltpu.VMEM((seq_len, w_kv_pad), BF16),
         pltpu.VMEM((seq_len, w_kv_pad), BF16),
         pltpu.VMEM((N_KV, GROUP * TQ, SLOT), BF16),
         pltpu.VMEM((N_HEADS, HEAD_DIM, TQ), F32)]
        + run_max + run_acc + run_max + run_acc)
    scratch = sum(_nbytes(s.shape, s.dtype) for s in scratch_shapes)
    return pl.pallas_call(
        _nsa_kernel,
        grid=(batch, nq),
        in_specs=[
            pl.BlockSpec((TQ, Q_WIDTH), qrow),
            pl.BlockSpec((TQ, SLOT), qrow),
            pl.BlockSpec((1, n_cmp_pad, w_kv_pad), per_b),
            pl.BlockSpec((1, n_cmp_pad, w_kv_pad), per_b),
            pl.BlockSpec((1, seq_len, KV_WIDTH), per_b),
            pl.BlockSpec((n_kt, VT_SLAB, KT), per_b),
            pl.BlockSpec((1, seq_len, KV_WIDTH), per_b),
            pl.BlockSpec((n_kt, VT_SLAB, KT), per_b),
            pl.BlockSpec(ov.shape, fixed),
        ],
        out_specs=pl.BlockSpec((TQ, Q_WIDTH), qrow),
        out_shape=jax.ShapeDtypeStruct((batch * seq_len, Q_WIDTH), BF16),
        scratch_shapes=scratch_shapes,
        compiler_params=pltpu.CompilerParams(
            dimension_semantics=("arbitrary", "arbitrary"),
            vmem_limit_bytes=_vmem_limit(blocks, scratch)),
        name="nsa",
    )(q, gate, kc.reshape(batch, n_cmp_pad, w_kv_pad), vc.reshape(batch, n_cmp_pad, w_kv_pad),
      ks.reshape(batch, seq_len, KV_WIDTH), vst, kw.reshape(batch, seq_len, KV_WIDTH), vwt, ov)


def _pool_kernel(u_ref, w_ref, scale_ref, o_ref):
    seq = u_ref.shape[0]
    t = lax.broadcasted_iota(jnp.int32, (seq, POOL_GROUP), 0)

    def shifted(x, k):
        return jnp.where(t >= k, pltpu.roll(x, k, axis=0), 0.0)

    for gi, w in enumerate(POOL_WINDOWS):
        sl = slice(gi * POOL_GROUP, (gi + 1) * POOL_GROUP)
        x = u_ref[:, sl]
        wsum = x
        span = 1
        while span < w:
            wsum = wsum + shifted(wsum, span)
            span *= 2
        cnt = jnp.minimum(t + 1, w).astype(F32)
        pooled = (wsum / cnt - x).astype(BF16)
        o_ref[:, sl] = (_dot(pooled, w_ref[gi]) * scale_ref[:, sl]).astype(BF16)


def _pool(pool_in, pool_w, pool_scale, batch, seq_len):
    for w in POOL_WINDOWS:
        assert w & (w - 1) == 0
    blocks = (_nbytes((seq_len, POOL_WIDTH), F32) + _nbytes(pool_w.shape, BF16)
              + _nbytes((seq_len, POOL_WIDTH), BF16))
    return pl.pallas_call(
        _pool_kernel,
        grid=(batch,),
        in_specs=[
            pl.BlockSpec((seq_len, POOL_WIDTH), lambda b: (b, 0)),
            pl.BlockSpec(pool_w.shape, lambda b: (0, 0, 0)),
            pl.BlockSpec((1, POOL_WIDTH), lambda b: (0, 0)),
        ],
        out_specs=pl.BlockSpec((seq_len, POOL_WIDTH), lambda b: (b, 0)),
        out_shape=jax.ShapeDtypeStruct((batch * seq_len, POOL_WIDTH), BF16),
        compiler_params=pltpu.CompilerParams(
            dimension_semantics=("arbitrary",), vmem_limit_bytes=_vmem_limit(blocks)),
        name="pool",
    )(pool_in, pool_w.astype(BF16), pool_scale.reshape(1, -1))


def _merge_kernel(x_ref, gpre_ref, wt_ref, nsa_ref, wa_ref, pool_ref, wp_ref, wo_ref, gpost_ref,
                  o_ref):
    x = x_ref[...]
    hn = _rms(x, gpre_ref[...])
    g_attn = jax.nn.sigmoid(_dot_nt(hn, wt_ref[OFF_MERGE:OFF_MERGE + D_MODEL, :]))
    g_pool = jax.nn.sigmoid(_dot_nt(hn, wt_ref[OFF_MERGE + D_MODEL:OFF_MERGE + 2 * D_MODEL, :]))
    y = (g_attn * _dot(nsa_ref[...].astype(F32), wa_ref[...])
         + g_pool * _dot(pool_ref[...].astype(F32), wp_ref[...]))
    h = _dot(y, wo_ref[...])
    o_ref[...] = x + _rms(h, gpost_ref[...])


def _merge(x1, g_pre, w_in_t, o_nsa, w_attn, o_pool, w_pool, w_out, g_post):
    t = x1.shape[0]
    row = lambda i: (i, 0)
    fixed = lambda i: (0, 0)
    once = pl.Buffered(1)
    weights = (_nbytes(w_in_t.shape, F32) + _nbytes((Q_WIDTH, D_MODEL), F32)
               + _nbytes((POOL_WIDTH, D_MODEL), F32) + _nbytes((D_MODEL, D_MODEL), F32))
    blocks = (2 * _nbytes((TOK_TILE, D_MODEL), F32) + _nbytes((TOK_TILE, Q_WIDTH), BF16)
              + _nbytes((TOK_TILE, POOL_WIDTH), BF16))
    return pl.pallas_call(
        _merge_kernel,
        grid=(t // TOK_TILE,),
        in_specs=[
            pl.BlockSpec((TOK_TILE, D_MODEL), row),
            pl.BlockSpec((1, D_MODEL), fixed),
            pl.BlockSpec(w_in_t.shape, fixed, pipeline_mode=once),
            pl.BlockSpec((TOK_TILE, Q_WIDTH), row),
            pl.BlockSpec((Q_WIDTH, D_MODEL), fixed, pipeline_mode=once),
            pl.BlockSpec((TOK_TILE, POOL_WIDTH), row),
            pl.BlockSpec((POOL_WIDTH, D_MODEL), fixed, pipeline_mode=once),
            pl.BlockSpec((D_MODEL, D_MODEL), fixed, pipeline_mode=once),
            pl.BlockSpec((1, D_MODEL), fixed),
        ],
        out_specs=pl.BlockSpec((TOK_TILE, D_MODEL), row),
        out_shape=jax.ShapeDtypeStruct((t, D_MODEL), F32),
        compiler_params=pltpu.CompilerParams(
            dimension_semantics=("arbitrary",), vmem_limit_bytes=_vmem_limit(blocks, weights)),
        name="merge",
    )(x1, g_pre.reshape(1, -1), w_in_t, o_nsa, w_attn, o_pool, w_pool, w_out, g_post.reshape(1, -1))


def _layer(x, tab, tab_cmp, batch, seq_len, g_ffn1_pre, w_ffn1_gate, w_ffn1_up, w_ffn1_down,
           g_ffn1_post, g_mix_pre, w_in, cmp_pe_k, cmp_w1_k, cmp_w2_k, cmp_pe_v, cmp_w1_v, cmp_w2_v,
           w_attn_branch, pool_w, pool_scale, w_pool_branch, w_out, g_mix_post, g_ffn2_pre,
           w_ffn2_gate, w_ffn2_up, w_ffn2_down, g_ffn2_post):
    x1 = _ffn(x, g_ffn1_pre, w_ffn1_gate, w_ffn1_up, w_ffn1_down, g_ffn1_post)
    w_in_t = w_in.T
    q, ks, kw, vst, vwt, cmp_rows, gate, pool_in = _in_proj(x1, g_mix_pre, w_in_t, tab)
    kc, vc = _compress(cmp_rows, cmp_pe_k, cmp_w1_k, cmp_w2_k, cmp_pe_v, cmp_w1_v, cmp_w2_v, tab_cmp)
    n_cmp_pad = seq_len // CMP_STRIDE
    kc = kc.reshape(batch, n_cmp_pad, N_KV * SLOT)
    vc = vc.reshape(batch, n_cmp_pad, N_KV * SLOT)
    o_nsa = _nsa(q, gate, kc, vc, ks, vst, kw, vwt, batch, seq_len)
    o_pool = _pool(pool_in, pool_w, pool_scale, batch, seq_len)
    x2 = _merge(x1, g_mix_pre, w_in_t, o_nsa, w_attn_branch, o_pool, w_pool_branch, w_out,
                g_mix_post)
    return _ffn(x2, g_ffn2_pre, w_ffn2_gate, w_ffn2_up, w_ffn2_down, g_ffn2_post)


def kernel(x, positions, g_ffn1_pre, w_ffn1_gate, w_ffn1_up, w_ffn1_down, g_ffn1_post, g_mix_pre, w_in, cmp_pe_k, cmp_w1_k, cmp_w2_k, cmp_pe_v, cmp_w1_v, cmp_w2_v, w_attn_branch, pool_w, pool_scale, w_pool_branch, w_out, g_mix_post, g_ffn2_pre, w_ffn2_gate, w_ffn2_up, w_ffn2_down, g_ffn2_post):
    batch, seq_len, d_model = x.shape
    assert d_model == D_MODEL and seq_len % TOK_TILE == 0 and seq_len % TQ == 0
    assert TQ == KT and TQ % SEL_LEN == 0 and TOK_TILE % KT == 0 and WINDOW % KT == 0
    assert seq_len % CMP_STRIDE == 0 and CMP_LEN == 2 * CMP_STRIDE
    t = batch * seq_len
    tab = _rope_tables(positions.reshape(t), seq_len)
    n_cmp = (seq_len - CMP_LEN) // CMP_STRIDE + 1
    n_cmp_pad = seq_len // CMP_STRIDE
    pos_cmp = jnp.pad(positions[:, CMP_LEN - 1::CMP_STRIDE], ((0, 0), (0, n_cmp_pad - n_cmp)))
    tab_cmp = _rope_tables(pos_cmp.reshape(batch * n_cmp_pad), batch * n_cmp_pad)
    xf = x.reshape(t, D_MODEL)
    per_layer = (g_ffn1_pre, w_ffn1_gate, w_ffn1_up, w_ffn1_down, g_ffn1_post, g_mix_pre, w_in,
                 cmp_pe_k, cmp_w1_k, cmp_w2_k, cmp_pe_v, cmp_w1_v, cmp_w2_v, w_attn_branch, pool_w,
                 pool_scale, w_pool_branch, w_out, g_mix_post, g_ffn2_pre, w_ffn2_gate, w_ffn2_up,
                 w_ffn2_down, g_ffn2_post)
    for l in range(g_ffn1_pre.shape[0]):
        xf = _layer(xf, tab, tab_cmp, batch, seq_len, *(p[l] for p in per_layer))
    return xf.reshape(batch, seq_len, D_MODEL)
```

```python
import functools
import math

import jax
import jax.numpy as jnp
from jax import lax
from jax.experimental import pallas as pl
from jax.experimental.pallas import tpu as pltpu

F32 = jnp.float32
BF16 = jnp.bfloat16

D_MODEL = 1024
N_HEADS = 16
HEAD_DIM = 64
N_KV = 4
GROUP = N_HEADS // N_KV
ROT_DIM = HEAD_DIM // 4
ROT_HALF = ROT_DIM // 2
ROPE_THETA = 500000.0
CMP_LEN = 32
CMP_STRIDE = 16
CMP_HIDDEN = 2 * HEAD_DIM
SEL_LEN = 64
SEL_SHIFT = SEL_LEN.bit_length() - 1
N_SEL = 8
WINDOW = 512
POOL_WINDOWS = (2, 4, 8, 16)
POOL_WIDTH = D_MODEL // 2
POOL_GROUP = POOL_WIDTH // len(POOL_WINDOWS)
D_FF = 2816
EPS = 1e-6
NEG_INF = -1e30
FORCE_SCORE = 1e4
Q_WIDTH = N_HEADS * HEAD_DIM
KV_WIDTH = N_KV * HEAD_DIM
N_GATES = 3 * N_HEADS
LOG2_E = math.log2(math.e)

LANES = 128
BF16_SUBLANES = 16
V7X_VMEM_BYTES = 64 * 1024 * 1024
VMEM_COMPILER_RESERVE = 4 * 1024 * 1024
VMEM_MIN_REQUEST = 32 * 1024 * 1024

SLOT = LANES
BLK_LANE0 = HEAD_DIM
VT_ROWS = HEAD_DIM + BF16_SUBLANES
ONE_ROW = HEAD_DIM
TOK_TILE = 512
TQ = 256
KT = 256
HALF = TQ // 2
CHAIN_COLS = 256
PIPE_DEPTH = 10
FF_CHUNK = 256
IN_SPLIT = 2
FFN_SPLIT = 2
MERGE_SPLIT = 2
N_ROPE_TAB = 2


def _vmem_limit(block_bytes, scratch_bytes=0):
    need = 2 * block_bytes + scratch_bytes
    return int(min(V7X_VMEM_BYTES - VMEM_COMPILER_RESERVE, max(2 * need, VMEM_MIN_REQUEST)))


def _nbytes(shape, dtype):
    n = 1
    for s in shape:
        n *= s
    return n * jnp.dtype(dtype).itemsize


def _rms(xf, g):
    return xf * lax.rsqrt(jnp.mean(xf * xf, axis=-1, keepdims=True) + EPS) * g


def _dot(a, b):
    return jnp.dot(a, b, preferred_element_type=F32)


def _dot_nt(a, b):
    return lax.dot_general(a, b, (((1,), (1,)), ((), ())), preferred_element_type=F32)


def _rope_trig_kernel(pos_ref, inv_ref, tab_ref):
    tn = pos_ref.shape[1]
    ang = pos_ref[...].astype(F32) * inv_ref[...]
    c, s = jnp.cos(ang), jnp.sin(ang)
    rest = HEAD_DIM - ROT_DIM
    reps = SLOT // HEAD_DIM
    cos_rows = jnp.concatenate([c, c, jnp.ones((rest, tn), F32)] * reps, axis=0)
    sin_rows = jnp.concatenate([-s, s, jnp.zeros((rest, tn), F32)] * reps, axis=0)
    for j in range(tn // LANES):
        cs = slice(j * LANES, (j + 1) * LANES)
        tab_ref[cs, 0:SLOT] = cos_rows[:, cs].T
        tab_ref[cs, SLOT:2 * SLOT] = sin_rows[:, cs].T


def _rope_tables(pos_flat, tile):
    n = pos_flat.shape[0]
    inv = ROPE_THETA ** (-jnp.arange(ROT_HALF, dtype=F32) * (2.0 / ROT_DIM))
    return pl.pallas_call(
        _rope_trig_kernel,
        grid=(n // tile,),
        in_specs=[pl.BlockSpec((1, tile), lambda i: (0, i)),
                  pl.BlockSpec((ROT_HALF, 1), lambda i: (0, 0))],
        out_specs=pl.BlockSpec((tile, N_ROPE_TAB * SLOT), lambda i: (i, 0)),
        out_shape=jax.ShapeDtypeStruct((n, N_ROPE_TAB * SLOT), F32),
        compiler_params=pltpu.CompilerParams(dimension_semantics=("arbitrary",)),
        name="rope_trig",
    )(pos_flat.reshape(1, n), inv.reshape(ROT_HALF, 1))


def _rope_slot(y, tab):
    cos_t = tab[:, 0:SLOT]
    sin_t = tab[:, SLOT:2 * SLOT]
    lane = lax.broadcasted_iota(jnp.int32, y.shape, 1) & (HEAD_DIM - 1)
    up = pltpu.roll(y, SLOT - ROT_HALF, axis=1)
    down = pltpu.roll(y, ROT_HALF, axis=1)
    return y * cos_t + jnp.where(lane < ROT_HALF, up, down) * sin_t


def _ffn_kernel(x_ref, gpre_ref, wg_ref, wu_ref, wd_ref, gpost_ref, o_ref):
    rows = x_ref.shape[0] // FFN_SPLIT
    parts = [slice(k * rows, (k + 1) * rows) for k in range(FFN_SPLIT)]
    hn = [_rms(x_ref[rs, :], gpre_ref[...]) for rs in parts]
    for k, rs in enumerate(parts):
        for c in range(D_FF // FF_CHUNK):
            sl = slice(c * FF_CHUNK, (c + 1) * FF_CHUNK)
            g = _dot(hn[k], wg_ref[:, sl])
            u = _dot(hn[k], wu_ref[:, sl])
            d = _dot(g * jax.nn.sigmoid(g) * u, wd_ref[sl, :])
            if c == 0:
                o_ref[rs, :] = d
            else:
                o_ref[rs, :] += d
        o_ref[rs, :] = x_ref[rs, :] + 0.5 * _rms(o_ref[rs, :], gpost_ref[...])


def _ffn(x, g_pre, w_gate, w_up, w_down, g_post):
    t = x.shape[0]
    tile = FFN_SPLIT * TOK_TILE
    row = lambda i: (i, 0)
    fixed = lambda i: (0, 0)
    once = pl.Buffered(1)
    weights = 3 * _nbytes((D_MODEL, D_FF), F32)
    blocks = 2 * _nbytes((tile, D_MODEL), F32)
    return pl.pallas_call(
        _ffn_kernel,
        grid=(t // tile,),
        in_specs=[
            pl.BlockSpec((tile, D_MODEL), row),
            pl.BlockSpec((1, D_MODEL), fixed),
            pl.BlockSpec((D_MODEL, D_FF), fixed, pipeline_mode=once),
            pl.BlockSpec((D_MODEL, D_FF), fixed, pipeline_mode=once),
            pl.BlockSpec((D_FF, D_MODEL), fixed, pipeline_mode=once),
            pl.BlockSpec((1, D_MODEL), fixed),
        ],
        out_specs=pl.BlockSpec((tile, D_MODEL), row),
        out_shape=jax.ShapeDtypeStruct((t, D_MODEL), F32),
        compiler_params=pltpu.CompilerParams(
            dimension_semantics=("arbitrary",), vmem_limit_bytes=_vmem_limit(blocks, weights)),
        name="ffn",
    )(x, g_pre.reshape(1, -1), w_gate, w_up, w_down, g_post.reshape(1, -1))


OFF_Q = 0
OFF_CMP = OFF_Q + Q_WIDTH
OFF_KS = OFF_CMP + 2 * KV_WIDTH
OFF_VS = OFF_KS + KV_WIDTH
OFF_KW = OFF_VS + KV_WIDTH
OFF_VW = OFF_KW + KV_WIDTH
OFF_GATE = OFF_VW + KV_WIDTH
OFF_POOL = OFF_GATE + N_GATES
OFF_MERGE = OFF_POOL + POOL_WIDTH
W_CMP = 2 * KV_WIDTH
CHUNK_TOK = CMP_STRIDE
VT_SLAB = N_KV * VT_ROWS


def _in_proj_kernel(x_ref, g_ref, w_ref, tab_ref, q_ref, ks_ref, kw_ref, vst_ref,
                    vwt_ref, cmp_ref, gate_ref, pool_ref, cmp_scr):
    rows = x_ref.shape[0] // IN_SPLIT
    wide = 4 * SLOT
    aux = (lax.broadcasted_iota(jnp.int32, (VT_ROWS - HEAD_DIM, KT), 0) == 0).astype(BF16)

    def part(k):
        rs = slice(k * rows, (k + 1) * rows)
        hn = _rms(x_ref[rs, :], g_ref[...])
        tab = tab_ref[rs, :]
        tab_q = tab * (HEAD_DIM ** -0.5 * LOG2_E)
        for c in range(Q_WIDTH // wide):
            y = _dot_nt(hn, w_ref[OFF_Q + c * wide:OFF_Q + (c + 1) * wide, :])
            for j in range(wide // SLOT):
                sl = slice(c * wide + j * SLOT, c * wide + (j + 1) * SLOT)
                q_ref[rs, sl] = _rope_slot(y[:, j * SLOT:(j + 1) * SLOT], tab_q).astype(BF16)
        for k_ref, off in ((ks_ref, OFF_KS), (kw_ref, OFF_KW)):
            y = _dot_nt(hn, w_ref[off:off + KV_WIDTH, :])
            for j in range(KV_WIDTH // SLOT):
                sl = slice(j * SLOT, (j + 1) * SLOT)
                k_ref[rs, sl] = _rope_slot(y[:, sl], tab).astype(BF16)
        y = _dot_nt(hn, w_ref[OFF_CMP:OFF_CMP + W_CMP, :])
        for j in range(W_CMP // LANES):
            cmp_scr[j, rs, :] = y[:, j * LANES:(j + 1) * LANES]
        chunk_rows = rows // CHUNK_TOK
        for l in range(CHUNK_TOK):
            for j in range(W_CMP // LANES):
                c0 = l * W_CMP + j * LANES
                cmp_ref[k * chunk_rows:(k + 1) * chunk_rows, c0:c0 + LANES] = (
                    cmp_scr[j, pl.ds(k * rows + l, chunk_rows, stride=CHUNK_TOK), :].astype(BF16))
        gate_ref[rs, :] = _dot_nt(hn, w_ref[OFF_GATE:OFF_GATE + SLOT, :])
        pool_ref[rs, :] = _dot_nt(hn, w_ref[OFF_POOL:OFF_MERGE, :])
        tiles = rows // KT
        for vt_ref, off in ((vst_ref, OFF_VS), (vwt_ref, OFF_VW)):
            vt = _dot_nt(w_ref[off:off + KV_WIDTH, :], hn).astype(BF16)
            for j in range(tiles):
                for g in range(N_KV):
                    vt_ref[k * tiles + j, g * VT_ROWS:g * VT_ROWS + HEAD_DIM, :] = (
                        vt[g * HEAD_DIM:(g + 1) * HEAD_DIM, j * KT:(j + 1) * KT])
                    vt_ref[k * tiles + j, g * VT_ROWS + HEAD_DIM:(g + 1) * VT_ROWS, :] = aux

    for k in range(IN_SPLIT):
        part(k)


def _in_proj(x1, g_pre, w_in_t, tab):
    t = x1.shape[0]
    row = lambda i: (i, 0)
    fixed = lambda i: (0, 0)
    once = pl.Buffered(1)
    tile = IN_SPLIT * TOK_TILE
    tiles = tile // KT
    chunk_rows = tile // CHUNK_TOK
    slab = pl.BlockSpec((tiles, VT_SLAB, KT), lambda i: (i, 0, 0))
    slab_shape = jax.ShapeDtypeStruct((t // KT, VT_SLAB, KT), BF16)
    out_specs = [
        pl.BlockSpec((tile, Q_WIDTH), row), pl.BlockSpec((tile, KV_WIDTH), row),
        pl.BlockSpec((tile, KV_WIDTH), row), slab, slab,
        pl.BlockSpec((chunk_rows, CHUNK_TOK * W_CMP), row),
        pl.BlockSpec((tile, SLOT), row), pl.BlockSpec((tile, POOL_WIDTH), row)]
    out_shape = [
        jax.ShapeDtypeStruct((t, Q_WIDTH), BF16), jax.ShapeDtypeStruct((t, KV_WIDTH), BF16),
        jax.ShapeDtypeStruct((t, KV_WIDTH), BF16), slab_shape, slab_shape,
        jax.ShapeDtypeStruct((t // CHUNK_TOK, CHUNK_TOK * W_CMP), BF16),
        jax.ShapeDtypeStruct((t, SLOT), F32), jax.ShapeDtypeStruct((t, POOL_WIDTH), F32)]
    weights = _nbytes((OFF_MERGE, D_MODEL), F32)
    blocks = (_nbytes((tile, D_MODEL), F32) + _nbytes((tile, N_ROPE_TAB * SLOT), F32)
              + _nbytes((tile, Q_WIDTH + 2 * KV_WIDTH + W_CMP), BF16)
              + 2 * _nbytes((tiles, VT_SLAB, KT), BF16) + _nbytes((tile, SLOT + POOL_WIDTH), F32))
    res = pl.pallas_call(
        _in_proj_kernel,
        grid=(t // tile,),
        in_specs=[
            pl.BlockSpec((tile, D_MODEL), row),
            pl.BlockSpec((1, D_MODEL), fixed),
            pl.BlockSpec((pl.Element(OFF_MERGE), pl.Element(D_MODEL)), fixed, pipeline_mode=once),
            pl.BlockSpec((tile, N_ROPE_TAB * SLOT), row),
        ],
        out_specs=out_specs,
        out_shape=out_shape,
        scratch_shapes=[pltpu.VMEM((W_CMP // LANES, tile, LANES), F32)],
        compiler_params=pltpu.CompilerParams(
            dimension_semantics=("arbitrary",),
            vmem_limit_bytes=_vmem_limit(blocks, weights + _nbytes((tile, W_CMP), F32))),
        name="in_proj",
    )(x1, g_pre.reshape(1, -1), w_in_t, tab)
    return res


def _compress_kernel(x_ref, w1k_ref, w1v_ref, pek_ref, pev_ref, w1k_raw_ref, w1v_raw_ref,
                     w2k_ref, w2v_ref, tab_ref, kc_ref, vc_ref, acck_ref, accv_ref):
    l = pl.program_id(0)

    @pl.when(l == 0)
    def _():
        acck_ref[...] = jnp.zeros(acck_ref.shape, F32)
        accv_ref[...] = jnp.zeros(accv_ref.shape, F32)

    for acc_ref, w1_ref, c0 in ((acck_ref, w1k_ref, 0), (accv_ref, w1v_ref, KV_WIDTH)):
        for g in range(N_KV):
            acc_ref[g] += _dot(x_ref[:, c0 + g * HEAD_DIM:c0 + (g + 1) * HEAD_DIM], w1_ref[0])

    @pl.when(l == CHUNK_TOK - 1)
    def _():
        rows = acck_ref.shape[1]
        tab = tab_ref[...]
        for acc_ref, pe_ref, w1_raw_ref, w2_ref, o_ref, rope in (
                (acck_ref, pek_ref, w1k_raw_ref, w2k_ref, kc_ref, True),
                (accv_ref, pev_ref, w1v_raw_ref, w2v_ref, vc_ref, False)):
            bias = _dot(pe_ref[...], w1_raw_ref[...])[0:1, :]
            for g in range(N_KV):
                first = acc_ref[g, :, 0:CMP_HIDDEN]
                second = pltpu.roll(acc_ref[g, :, CMP_HIDDEN:2 * CMP_HIDDEN], rows - 1, axis=0)
                hid = jax.nn.gelu(first + second + bias).astype(BF16)
                out = _dot(hid, w2_ref[...])
                if rope:
                    out = _rope_slot(out, tab)
                o_ref[:, g * SLOT:(g + 1) * SLOT] = out.astype(BF16)


def _compress_weights(w1, w2):
    w1 = w1.reshape(2, CHUNK_TOK, HEAD_DIM, CMP_HIDDEN)
    w1_tok = jnp.concatenate([w1[0], w1[1]], axis=-1).astype(BF16)
    return w1_tok, jnp.pad(w2, ((0, 0), (0, SLOT - HEAD_DIM))).astype(BF16)


def _compress(cmp_rows, cmp_pe_k, cmp_w1_k, cmp_w2_k, cmp_pe_v, cmp_w1_v, cmp_w2_v, tab_cmp):
    rows = cmp_rows.shape[0]
    w1k, w2k = _compress_weights(cmp_w1_k, cmp_w2_k)
    w1v, w2v = _compress_weights(cmp_w1_v, cmp_w2_v)
    sub = 8
    pek = jnp.broadcast_to(cmp_pe_k.reshape(1, -1), (sub, CMP_LEN * HEAD_DIM)).astype(BF16)
    pev = jnp.broadcast_to(cmp_pe_v.reshape(1, -1), (sub, CMP_LEN * HEAD_DIM)).astype(BF16)
    fixed = lambda l: (0, 0)
    w_kv_pad = N_KV * SLOT
    blocks = (_nbytes((rows, W_CMP), BF16) + 2 * _nbytes((HEAD_DIM, 2 * CMP_HIDDEN), BF16)
              + 2 * _nbytes((sub + CMP_LEN * HEAD_DIM, CMP_HIDDEN), BF16)
              + 2 * _nbytes((CMP_HIDDEN, SLOT), BF16)
              + _nbytes((rows, N_ROPE_TAB * SLOT), F32) + 2 * _nbytes((rows, w_kv_pad), BF16))
    acc = pltpu.VMEM((N_KV, rows, 2 * CMP_HIDDEN), F32)
    return pl.pallas_call(
        _compress_kernel,
        grid=(CHUNK_TOK,),
        in_specs=[
            pl.BlockSpec((rows, W_CMP), lambda l: (0, l)),
            pl.BlockSpec((1, HEAD_DIM, 2 * CMP_HIDDEN), lambda l: (l, 0, 0)),
            pl.BlockSpec((1, HEAD_DIM, 2 * CMP_HIDDEN), lambda l: (l, 0, 0)),
            pl.BlockSpec((sub, CMP_LEN * HEAD_DIM), fixed),
            pl.BlockSpec((sub, CMP_LEN * HEAD_DIM), fixed),
            pl.BlockSpec((CMP_LEN * HEAD_DIM, CMP_HIDDEN), fixed),
            pl.BlockSpec((CMP_LEN * HEAD_DIM, CMP_HIDDEN), fixed),
            pl.BlockSpec((CMP_HIDDEN, SLOT), fixed),
            pl.BlockSpec((CMP_HIDDEN, SLOT), fixed),
            pl.BlockSpec((rows, N_ROPE_TAB * SLOT), fixed),
        ],
        out_specs=[pl.BlockSpec((rows, w_kv_pad), fixed)] * 2,
        out_shape=[jax.ShapeDtypeStruct((rows, w_kv_pad), BF16)] * 2,
        scratch_shapes=[acc, acc],
        compiler_params=pltpu.CompilerParams(
            dimension_semantics=("arbitrary",),
            vmem_limit_bytes=_vmem_limit(blocks, 2 * _nbytes(acc.shape, F32))),
        name="compress",
    )(cmp_rows, w1k, w1v, pek, pev, cmp_w1_k.astype(BF16), cmp_w1_v.astype(BF16), w2k, w2v, tab_cmp)


def _nsa_kernel(q_ref, gate_ref, kc_ref, vc_ref, ks_ref, vst_ref, kw_ref, vwt_ref, ov_ref,
                o_ref, ksp_ref, kwp_ref, qg_ref, mix_ref, *state_refs):
    i = pl.program_id(1)
    seq = ks_ref.shape[1]
    cols_g = GROUP * TQ
    chains_g = cols_g // CHAIN_COLS
    n_chains = N_KV * chains_g
    n_cmp = kc_ref.shape[1]
    sel_state = (state_refs[0:n_chains], state_refs[n_chains:2 * n_chains])
    win_state = (state_refs[2 * n_chains:3 * n_chains], state_refs[3 * n_chains:4 * n_chains])
    lane_q = lax.broadcasted_iota(jnp.int32, (TQ, SLOT), 1)
    lane_h = lax.broadcasted_iota(jnp.int32, (HALF, SLOT), 1)

    def head_slot(x_ref, rows, j):
        two = x_ref[rows, (j // 2) * SLOT:(j // 2 + 1) * SLOT]
        if j % 2:
            words = pltpu.roll(pltpu.bitcast(two, jnp.uint32), HEAD_DIM, axis=1)
            two = pltpu.bitcast(words, BF16)
        return two

    @pl.when(i == 0)
    def _():
        for c in range(seq // KT):
            rows = pl.ds(c * KT, KT)
            lane = lax.broadcasted_iota(jnp.int32, (KT, SLOT), 1)
            key_blk = (c * KT + lax.broadcasted_iota(jnp.int32, (KT, SLOT), 0)) >> SEL_SHIFT
            onehot = jnp.where(lane - BLK_LANE0 == key_blk, 1.0, 0.0).astype(BF16)
            zeros = jnp.zeros((KT, SLOT), BF16)
            for g in range(N_KV):
                gs = slice(g * SLOT, (g + 1) * SLOT)
                ksp_ref[rows, gs] = jnp.where(lane < HEAD_DIM, head_slot(ks_ref.at[0], rows, g), onehot)
                kwp_ref[rows, gs] = jnp.where(lane < HEAD_DIM, head_slot(kw_ref.at[0], rows, g), zeros)

    def q_rows(r, x):
        return slice((x * GROUP + r) * HALF, (x * GROUP + r + 1) * HALF)

    def chain_half(c):
        return c * CHAIN_COLS // (GROUP * HALF)

    for h in range(N_HEADS):
        g, r = divmod(h, GROUP)
        q_pad = jnp.where(lane_q < HEAD_DIM, head_slot(q_ref, slice(None), h), jnp.zeros((TQ, SLOT), BF16))
        for x in range(2):
            qg_ref[g, q_rows(r, x), :] = q_pad[x * HALF:(x + 1) * HALF]

    gates_t = jax.nn.sigmoid(gate_ref[...]).T
    key_h = lax.broadcasted_iota(jnp.int32, (HALF, CHAIN_COLS), 0)
    qry_h = lax.broadcasted_iota(jnp.int32, (HALF, CHAIN_COLS), 1) & (HALF - 1)
    tri_le, tri_gt = key_h <= qry_h, key_h > qry_h
    diag_plan = ((slice(0, HALF), tri_le, slice(0, HALF)), (slice(0, KT), tri_le, slice(HALF, KT)))
    far_plan = ((slice(0, KT), tri_gt, slice(0, HALF)), (slice(HALF, KT), tri_gt, slice(0, HALF)))

    def gate_row(branch, h):
        c = branch * N_HEADS + h
        return gates_t[c:c + 1, :]

    sub = 8
    cmp_per_sub = sub * SEL_LEN // CMP_STRIDE

    def compressed_scores(ng):
        n_c = ng * cmp_per_sub
        cmp_scores = []
        for h in range(N_HEADS):
            g, r = divmod(h, GROUP)
            q_h = jnp.concatenate([qg_ref[g, q_rows(r, 0), :], qg_ref[g, q_rows(r, 1), :]], axis=0)
            cmp_scores.append(_dot_nt(kc_ref[0, 0:n_c, g * SLOT:(g + 1) * SLOT], q_h))
        return cmp_scores

    def compressed_and_select(ng, start, cmp_scores):
        n_blk, n_c = ng * sub, ng * cmp_per_sub
        t_cmp = start + lax.broadcasted_iota(jnp.int32, (n_c, TQ), 1)
        n_idx = lax.broadcasted_iota(jnp.int32, (n_c, TQ), 0)
        cmp_valid = n_idx * CMP_STRIDE + (CMP_LEN - 1) <= t_cmp
        any_cmp = start + lax.broadcasted_iota(jnp.int32, (1, TQ), 1) >= CMP_LEN - 1
        t_row = start + lax.broadcasted_iota(jnp.int32, (n_blk, TQ), 1)
        blk = lax.broadcasted_iota(jnp.int32, (n_blk, TQ), 0)
        forced = (blk == t_row >> SEL_SHIFT) | (blk == 0)
        causal_blk = blk * SEL_LEN <= t_row
        idx8 = lax.broadcasted_iota(jnp.int32, (sub, TQ), 0)
        pad_c = jnp.zeros((n_cmp - n_c, TQ), F32)
        for g in range(N_KV):
            vct_g = vc_ref[0, :, g * SLOT:(g + 1) * SLOT].astype(F32).T[0:HEAD_DIM].astype(BF16)
            p_sum = jnp.zeros((n_c, TQ), F32)
            for r in range(GROUP):
                h = g * GROUP + r
                s = jnp.where(cmp_valid, cmp_scores[h], NEG_INF)
                p = jnp.exp2(s - jnp.max(s, axis=0, keepdims=True))
                inv = jnp.where(any_cmp, 1.0 / jnp.sum(p, axis=0, keepdims=True), 0.0)
                p = p * inv
                p_sum = p_sum + p
                p_all = jnp.concatenate([p, pad_c], axis=0) if ng * cmp_per_sub < n_cmp else p
                mix_ref[h] = gate_row(0, h) * _dot(vct_g, p_all.astype(BF16))
            p_sum_all = jnp.concatenate([p_sum, pad_c], axis=0) if ng * cmp_per_sub < n_cmp else p_sum
            imp = jnp.dot(ov_ref[0:n_blk, :], p_sum_all, precision=lax.Precision.HIGHEST,
                          preferred_element_type=F32)
            score = jnp.where(causal_blk, jnp.where(forced, FORCE_SCORE, imp), NEG_INF)
            rows8 = [score[k * sub:(k + 1) * sub] for k in range(ng)]
            ranks = [jnp.zeros((sub, TQ), jnp.int32) for _ in rows8]
            for c in range(n_blk):
                other = score[c:c + 1, :]
                for k, mine in enumerate(rows8):
                    if k * sub > c:
                        beats = other >= mine
                    elif (k + 1) * sub - 1 <= c:
                        beats = other > mine
                    else:
                        beats = (other > mine) | ((other == mine) & (idx8 > c - k * sub))
                    ranks[k] = ranks[k] + beats.astype(jnp.int32)
            rank = jnp.concatenate(ranks, axis=0) if ng > 1 else ranks[0]
            bias = jnp.where(causal_blk & (rank < N_SEL), 0.0, NEG_INF)
            bias_t = jnp.concatenate([jnp.zeros((BLK_LANE0, TQ), F32), bias,
                                      jnp.zeros((SLOT - BLK_LANE0 - n_blk, TQ), F32)], axis=0)
            bias_q = bias_t.T.astype(BF16)
            for r in range(GROUP):
                for x in range(2):
                    rs = q_rows(r, x)
                    qg_ref[g, rs, :] = jnp.where(lane_h < HEAD_DIM, qg_ref[g, rs, :],
                                                 bias_q[x * HALF:(x + 1) * HALF])

    ng_now = ((i + 1) * TQ + sub * SEL_LEN - 1) // (sub * SEL_LEN)
    for ng in range(1, seq // (sub * SEL_LEN) + 1):

        @pl.when(ng_now == ng)
        def _(ng=ng):
            compressed_and_select(ng, i * TQ, compressed_scores(ng))

    def flash_init(state):
        for m_ref, acc_ref in zip(*state):
            m_ref[...] = jnp.full(m_ref.shape, NEG_INF, F32)
            acc_ref[...] = jnp.zeros(acc_ref.shape, F32)

    def flash_tiles(tiles):
        jobs = [(tile, c) for tile in tiles for c in range(n_chains)]

        def key_plan(job):
            (_, _, _, plan, _), chain = job
            return (slice(0, KT), None, None) if plan is None else plan[chain_half(chain % chains_g)]

        scores = {}
        for step in range(len(jobs) + PIPE_DEPTH):
            if step < len(jobs):
                (kp_ref, _, kt, _, _), chain = jobs[step]
                g, c = divmod(chain, chains_g)
                keys, mask, rows = key_plan(jobs[step])
                off = pl.multiple_of(kt * KT + keys.start, HALF)
                s = _dot_nt(kp_ref[pl.ds(off, keys.stop - keys.start), g * SLOT:(g + 1) * SLOT],
                            qg_ref[g, c * CHAIN_COLS:(c + 1) * CHAIN_COLS, :])
                if mask is not None:
                    parts = [s[0:rows.start], jnp.where(mask, s[rows], NEG_INF), s[rows.stop:]]
                    s = jnp.concatenate([p for p in parts if p.shape[0]], axis=0)
                scores[step] = s.astype(BF16)
            done = step - PIPE_DEPTH
            if done >= 0:
                (_, vt_ref, kt, _, (m_refs, acc_refs)), chain = jobs[done]
                g = chain // chains_g
                keys = key_plan(jobs[done])[0]
                m_ref, acc_ref = m_refs[chain], acc_refs[chain]
                s = scores.pop(done)
                m_old = m_ref[...]
                packed = [s[r * BF16_SUBLANES:(r + 1) * BF16_SUBLANES]
                          for r in range(s.shape[0] // BF16_SUBLANES)]
                m_tile = functools.reduce(jnp.maximum, packed).astype(F32)
                m_new = jnp.maximum(m_old, jnp.max(m_tile, axis=0, keepdims=True))
                p = jnp.exp2(s - m_new.astype(BF16))
                pv = _dot(vt_ref[kt, g * VT_ROWS:(g + 1) * VT_ROWS, keys], p)
                acc_ref[...] = jnp.exp2(m_old - m_new) * acc_ref[...] + pv
                m_ref[...] = m_new

    def flash_mix(branch, state):
        _, acc_refs = state
        for h in range(N_HEADS):
            g, r = divmod(h, GROUP)
            gate = gate_row(branch, h)
            for x in range(2):
                c, c0 = divmod(q_rows(r, x).start, CHAIN_COLS)
                acc = acc_refs[g * chains_g + c][:, c0:c0 + HALF]
                qs = slice(x * HALF, (x + 1) * HALF)
                scale = gate[:, qs] * (1.0 / acc[ONE_ROW:ONE_ROW + 1, :])
                mix_ref[h, :, qs] = mix_ref[h, :, qs] + scale * acc[0:HEAD_DIM]

    flash_init(sel_state)
    flash_init(win_state)

    def sel_past(kt):
        return (ksp_ref, vst_ref, kt, None, sel_state)

    def sel_pair(j, carry):
        flash_tiles([sel_past(2 * j), sel_past(2 * j + 1)])
        return carry

    lax.fori_loop(0, i >> 1, sel_pair, 0)

    @pl.when(i & 1 == 1)
    def _():
        flash_tiles([sel_past(i - 1)])

    n_back = WINDOW // KT
    sel_diag = (ksp_ref, vst_ref, i, diag_plan, sel_state)
    win_diag = (kwp_ref, vwt_ref, i, diag_plan, win_state)

    def win_back(d):
        return (kwp_ref, vwt_ref, i - d, far_plan if d == n_back else None, win_state)

    for have in range(n_back + 1):
        cond = (i == have) if have < n_back else (i >= have)

        @pl.when(cond)
        def _(have=have):
            flash_tiles([sel_diag] + [win_back(d) for d in range(have, 0, -1)] + [win_diag])

    flash_mix(1, sel_state)
    flash_mix(2, win_state)
    for h2 in range(N_HEADS // 2):
        pair = jnp.concatenate([mix_ref[2 * h2], mix_ref[2 * h2 + 1]], axis=0)
        o_ref[:, h2 * SLOT:(h2 + 1) * SLOT] = pair.T.astype(BF16)


def _overlap(n_cmp_pad, n_slc):
    c0 = jnp.arange(n_cmp_pad) * CMP_STRIDE
    s0 = jnp.arange(n_slc) * SEL_LEN
    ov = jnp.minimum(c0[None, :] + CMP_LEN, s0[:, None] + SEL_LEN) - jnp.maximum(c0[None, :], s0[:, None])
    return jnp.clip(ov, 0).astype(F32) / CMP_LEN


def _nsa(q, gate, kc, vc, ks, vst, kw, vwt, batch, seq_len):
    n_slc = seq_len // SEL_LEN
    n_cmp_pad = kc.shape[1]
    w_kv_pad = N_KV * SLOT
    assert n_slc <= SLOT - BLK_LANE0 and n_slc % 8 == 0
    assert CHAIN_COLS % HALF == 0 and (GROUP * HALF) % CHAIN_COLS == 0 and KT == 2 * HALF
    ov = _overlap(n_cmp_pad, n_slc)
    nq = seq_len // TQ
    n_kt = seq_len // KT
    qrow = lambda b, i: (b * nq + i, 0)
    per_b = lambda b, i: (b, 0, 0)
    fixed = lambda b, i: (0, 0)
    blocks = (_nbytes((TQ, Q_WIDTH), BF16) + _nbytes((TQ, SLOT), F32)
              + 2 * _nbytes((n_cmp_pad, w_kv_pad), BF16) + 2 * _nbytes((seq_len, KV_WIDTH), BF16)
              + 2 * _nbytes((n_kt, VT_SLAB, KT), BF16)
              + _nbytes(ov.shape, F32) + _nbytes((TQ, Q_WIDTH), BF16))
    n_chains = N_KV * GROUP * TQ // CHAIN_COLS
    run_max = [pltpu.VMEM((1, CHAIN_COLS), F32)] * n_chains
    run_acc = [pltpu.VMEM((VT_ROWS, CHAIN_COLS), F32)] * n_chains
    scratch_shapes = (
        [pltpu.VMEM((seq_len, w_kv_pad), BF16),
         pltpu.VMEM((seq_len, w_kv_pad), BF16),
         pltpu.VMEM((N_KV, GROUP * TQ, SLOT), BF16),
         pltpu.VMEM((N_HEADS, HEAD_DIM, TQ), F32)]
        + run_max + run_acc + run_max + run_acc)
    scratch = sum(_nbytes(s.shape, s.dtype) for s in scratch_shapes)
    return pl.pallas_call(
        _nsa_kernel,
        grid=(batch, nq),
        in_specs=[
            pl.BlockSpec((TQ, Q_WIDTH), qrow),
            pl.BlockSpec((TQ, SLOT), qrow),
            pl.BlockSpec((1, n_cmp_pad, w_kv_pad), per_b),
            pl.BlockSpec((1, n_cmp_pad, w_kv_pad), per_b),
            pl.BlockSpec((1, seq_len, KV_WIDTH), per_b),
            pl.BlockSpec((n_kt, VT_SLAB, KT), per_b),
            pl.BlockSpec((1, seq_len, KV_WIDTH), per_b),
            pl.BlockSpec((n_kt, VT_SLAB, KT), per_b),
            pl.BlockSpec(ov.shape, fixed),
        ],
        out_specs=pl.BlockSpec((TQ, Q_WIDTH), qrow),
        out_shape=jax.ShapeDtypeStruct((batch * seq_len, Q_WIDTH), BF16),
        scratch_shapes=scratch_shapes,
        compiler_params=pltpu.CompilerParams(
            dimension_semantics=("arbitrary", "arbitrary"),
            vmem_limit_bytes=_vmem_limit(blocks, scratch)),
        name="nsa",
    )(q, gate, kc.reshape(batch, n_cmp_pad, w_kv_pad), vc.reshape(batch, n_cmp_pad, w_kv_pad),
      ks.reshape(batch, seq_len, KV_WIDTH), vst, kw.reshape(batch, seq_len, KV_WIDTH), vwt, ov)


def _pool_kernel(u_ref, w_ref, scale_ref, o_ref):
    seq = u_ref.shape[0]
    t = lax.broadcasted_iota(jnp.int32, (seq, POOL_GROUP), 0)

    def shifted(x, k):
        return jnp.where(t >= k, pltpu.roll(x, k, axis=0), 0.0)

    for gi, w in enumerate(POOL_WINDOWS):
        sl = slice(gi * POOL_GROUP, (gi + 1) * POOL_GROUP)
        x = u_ref[:, sl]
        wsum = x
        span = 1
        while span < w:
            wsum = wsum + shifted(wsum, span)
            span *= 2
        cnt = jnp.minimum(t + 1, w).astype(F32)
        pooled = (wsum / cnt - x).astype(BF16)
        o_ref[:, sl] = (_dot(pooled, w_ref[gi]) * scale_ref[:, sl]).astype(BF16)


def _pool(pool_in, pool_w, pool_scale, batch, seq_len):
    for w in POOL_WINDOWS:
        assert w & (w - 1) == 0
    blocks = (_nbytes((seq_len, POOL_WIDTH), F32) + _nbytes(pool_w.shape, BF16)
              + _nbytes((seq_len, POOL_WIDTH), BF16))
    return pl.pallas_call(
        _pool_kernel,
        grid=(batch,),
        in_specs=[
            pl.BlockSpec((seq_len, POOL_WIDTH), lambda b: (b, 0)),
            pl.BlockSpec(pool_w.shape, lambda b: (0, 0, 0)),
            pl.BlockSpec((1, POOL_WIDTH), lambda b: (0, 0)),
        ],
        out_specs=pl.BlockSpec((seq_len, POOL_WIDTH), lambda b: (b, 0)),
        out_shape=jax.ShapeDtypeStruct((batch * seq_len, POOL_WIDTH), BF16),
        compiler_params=pltpu.CompilerParams(
            dimension_semantics=("arbitrary",), vmem_limit_bytes=_vmem_limit(blocks)),
        name="pool",
    )(pool_in, pool_w.astype(BF16), pool_scale.reshape(1, -1))


def _merge_kernel(x_ref, gpre_ref, wt_ref, nsa_ref, wa_ref, pool_ref, wp_ref, wo_ref, gpost_ref,
                  o_ref):
    rows = x_ref.shape[0] // MERGE_SPLIT
    for k in range(MERGE_SPLIT):
        rs = slice(k * rows, (k + 1) * rows)
        x = x_ref[rs, :]
        hn = _rms(x, gpre_ref[...])
        g_attn = jax.nn.sigmoid(_dot_nt(hn, wt_ref[0:D_MODEL, :]))
        g_pool = jax.nn.sigmoid(_dot_nt(hn, wt_ref[D_MODEL:2 * D_MODEL, :]))
        y = (g_attn * _dot(nsa_ref[rs, :].astype(F32), wa_ref[...])
             + g_pool * _dot(pool_ref[rs, :].astype(F32), wp_ref[...]))
        h = _dot(y, wo_ref[...])
        o_ref[rs, :] = x + _rms(h, gpost_ref[...])


def _merge(x1, g_pre, w_in_t, o_nsa, w_attn, o_pool, w_pool, w_out, g_post):
    t = x1.shape[0]
    row = lambda i: (i, 0)
    fixed = lambda i: (0, 0)
    once = pl.Buffered(1)
    tile = MERGE_SPLIT * TOK_TILE
    weights = (_nbytes((2 * D_MODEL, D_MODEL), F32) + _nbytes((Q_WIDTH, D_MODEL), F32)
               + _nbytes((POOL_WIDTH, D_MODEL), F32) + _nbytes((D_MODEL, D_MODEL), F32))
    blocks = (2 * _nbytes((tile, D_MODEL), F32) + _nbytes((tile, Q_WIDTH), BF16)
              + _nbytes((tile, POOL_WIDTH), BF16))
    gate_rows = pl.BlockSpec((pl.Element(2 * D_MODEL), pl.Element(D_MODEL)), lambda i: (OFF_MERGE, 0),
                             pipeline_mode=once)
    return pl.pallas_call(
        _merge_kernel,
        grid=(t // tile,),
        in_specs=[
            pl.BlockSpec((tile, D_MODEL), row),
            pl.BlockSpec((1, D_MODEL), fixed),
            gate_rows,
            pl.BlockSpec((tile, Q_WIDTH), row),
            pl.BlockSpec((Q_WIDTH, D_MODEL), fixed, pipeline_mode=once),
            pl.BlockSpec((tile, POOL_WIDTH), row),
            pl.BlockSpec((POOL_WIDTH, D_MODEL), fixed, pipeline_mode=once),
            pl.BlockSpec((D_MODEL, D_MODEL), fixed, pipeline_mode=once),
            pl.BlockSpec((1, D_MODEL), fixed),
        ],
        out_specs=pl.BlockSpec((tile, D_MODEL), row),
        out_shape=jax.ShapeDtypeStruct((t, D_MODEL), F32),
        compiler_params=pltpu.CompilerParams(
            dimension_semantics=("arbitrary",), vmem_limit_bytes=_vmem_limit(blocks, weights)),
        name="merge",
    )(x1, g_pre.reshape(1, -1), w_in_t, o_nsa, w_attn, o_pool, w_pool, w_out, g_post.reshape(1, -1))


def _layer(x, tab, tab_cmp, batch, seq_len, g_ffn1_pre, w_ffn1_gate, w_ffn1_up, w_ffn1_down,
           g_ffn1_post, g_mix_pre, w_in, cmp_pe_k, cmp_w1_k, cmp_w2_k, cmp_pe_v, cmp_w1_v, cmp_w2_v,
           w_attn_branch, pool_w, pool_scale, w_pool_branch, w_out, g_mix_post, g_ffn2_pre,
           w_ffn2_gate, w_ffn2_up, w_ffn2_down, g_ffn2_post):
    x1 = _ffn(x, g_ffn1_pre, w_ffn1_gate, w_ffn1_up, w_ffn1_down, g_ffn1_post)
    w_in_t = w_in.T
    q, ks, kw, vst, vwt, cmp_rows, gate, pool_in = _in_proj(x1, g_mix_pre, w_in_t, tab)
    kc, vc = _compress(cmp_rows, cmp_pe_k, cmp_w1_k, cmp_w2_k, cmp_pe_v, cmp_w1_v, cmp_w2_v, tab_cmp)
    n_cmp_pad = seq_len // CMP_STRIDE
    kc = kc.reshape(batch, n_cmp_pad, N_KV * SLOT)
    vc = vc.reshape(batch, n_cmp_pad, N_KV * SLOT)
    o_nsa = _nsa(q, gate, kc, vc, ks, vst, kw, vwt, batch, seq_len)
    o_pool = _pool(pool_in, pool_w, pool_scale, batch, seq_len)
    x2 = _merge(x1, g_mix_pre, w_in_t, o_nsa, w_attn_branch, o_pool, w_pool_branch, w_out,
                g_mix_post)
    return _ffn(x2, g_ffn2_pre, w_ffn2_gate, w_ffn2_up, w_ffn2_down, g_ffn2_post)


def kernel(x, positions, g_ffn1_pre, w_ffn1_gate, w_ffn1_up, w_ffn1_down, g_ffn1_post, g_mix_pre, w_in, cmp_pe_k, cmp_w1_k, cmp_w2_k, cmp_pe_v, cmp_w1_v, cmp_w2_v, w_attn_branch, pool_w, pool_scale, w_pool_branch, w_out, g_mix_post, g_ffn2_pre, w_ffn2_gate, w_ffn2_up, w_ffn2_down, g_ffn2_post):
    batch, seq_len, d_model = x.shape
    assert d_model == D_MODEL and seq_len % TOK_TILE == 0 and seq_len % TQ == 0
    assert TQ == KT and TQ % SEL_LEN == 0 and TOK_TILE % KT == 0 and WINDOW % KT == 0
    assert seq_len % CMP_STRIDE == 0 and CMP_LEN == 2 * CMP_STRIDE
    t = batch * seq_len
    tab = _rope_tables(positions.reshape(t), seq_len)
    n_cmp = (seq_len - CMP_LEN) // CMP_STRIDE + 1
    n_cmp_pad = seq_len // CMP_STRIDE
    pos_cmp = jnp.pad(positions[:, CMP_LEN - 1::CMP_STRIDE], ((0, 0), (0, n_cmp_pad - n_cmp)))
    tab_cmp = _rope_tables(pos_cmp.reshape(batch * n_cmp_pad), batch * n_cmp_pad)
    xf = x.reshape(t, D_MODEL)
    per_layer = (g_ffn1_pre, w_ffn1_gate, w_ffn1_up, w_ffn1_down, g_ffn1_post, g_mix_pre, w_in,
                 cmp_pe_k, cmp_w1_k, cmp_w2_k, cmp_pe_v, cmp_w1_v, cmp_w2_v, w_attn_branch, pool_w,
                 pool_scale, w_pool_branch, w_out, g_mix_post, g_ffn2_pre, w_ffn2_gate, w_ffn2_up,
                 w_ffn2_down, g_ffn2_post)
    for l in range(g_ffn1_pre.shape[0]):
        xf = _layer(xf, tab, tab_cmp, batch, seq_len, *(p[l] for p in per_layer))
    return xf.reshape(batch, seq_len, D_MODEL)
```

```python
import functools
import math

import jax
import jax.numpy as jnp
from jax import lax
from jax.experimental import pallas as pl
from jax.experimental.pallas import tpu as pltpu

F32 = jnp.float32
BF16 = jnp.bfloat16

D_MODEL = 1024
N_HEADS = 16
HEAD_DIM = 64
N_KV = 4
GROUP = N_HEADS // N_KV
ROT_DIM = HEAD_DIM // 4
ROT_HALF = ROT_DIM // 2
ROPE_THETA = 500000.0
CMP_LEN = 32
CMP_STRIDE = 16
CMP_HIDDEN = 2 * HEAD_DIM
SEL_LEN = 64
SEL_SHIFT = SEL_LEN.bit_length() - 1
N_SEL = 8
WINDOW = 512
POOL_WINDOWS = (2, 4, 8, 16)
POOL_WIDTH = D_MODEL // 2
POOL_GROUP = POOL_WIDTH // len(POOL_WINDOWS)
D_FF = 2816
EPS = 1e-6
NEG_INF = -1e30
FORCE_SCORE = 1e4
Q_WIDTH = N_HEADS * HEAD_DIM
KV_WIDTH = N_KV * HEAD_DIM
N_GATES = 3 * N_HEADS
LOG2_E = math.log2(math.e)

LANES = 128
BF16_SUBLANES = 16
V7X_VMEM_BYTES = 64 * 1024 * 1024
VMEM_COMPILER_RESERVE = 4 * 1024 * 1024
VMEM_MIN_REQUEST = 32 * 1024 * 1024

SLOT = LANES
BLK_LANE0 = HEAD_DIM
VT_ROWS = HEAD_DIM + BF16_SUBLANES
ONE_ROW = HEAD_DIM
TOK_TILE = 512
TQ = 256
KT = 256
HALF = TQ // 2
CHAIN_COLS = 256
PIPE_DEPTH = 10
FF_CHUNK = 256
IN_SPLIT = 2
MERGE_SPLIT = 2
N_ROPE_TAB = 2


def _vmem_limit(block_bytes, scratch_bytes=0):
    need = 2 * block_bytes + scratch_bytes
    return int(min(V7X_VMEM_BYTES - VMEM_COMPILER_RESERVE, max(2 * need, VMEM_MIN_REQUEST)))


def _nbytes(shape, dtype):
    n = 1
    for s in shape:
        n *= s
    return n * jnp.dtype(dtype).itemsize


def _rms(xf, g):
    return xf * lax.rsqrt(jnp.mean(xf * xf, axis=-1, keepdims=True) + EPS) * g


def _dot(a, b):
    return jnp.dot(a, b, preferred_element_type=F32)


def _dot_nt(a, b):
    return lax.dot_general(a, b, (((1,), (1,)), ((), ())), preferred_element_type=F32)


def _rope_trig_kernel(pos_ref, inv_ref, tab_ref):
    tn = pos_ref.shape[1]
    ang = pos_ref[...].astype(F32) * inv_ref[...]
    c, s = jnp.cos(ang), jnp.sin(ang)
    rest = HEAD_DIM - ROT_DIM
    reps = SLOT // HEAD_DIM
    cos_rows = jnp.concatenate([c, c, jnp.ones((rest, tn), F32)] * reps, axis=0)
    sin_rows = jnp.concatenate([-s, s, jnp.zeros((rest, tn), F32)] * reps, axis=0)
    for j in range(tn // LANES):
        cs = slice(j * LANES, (j + 1) * LANES)
        tab_ref[cs, 0:SLOT] = cos_rows[:, cs].T
        tab_ref[cs, SLOT:2 * SLOT] = sin_rows[:, cs].T


def _rope_tables(pos_flat, tile):
    n = pos_flat.shape[0]
    inv = ROPE_THETA ** (-jnp.arange(ROT_HALF, dtype=F32) * (2.0 / ROT_DIM))
    return pl.pallas_call(
        _rope_trig_kernel,
        grid=(n // tile,),
        in_specs=[pl.BlockSpec((1, tile), lambda i: (0, i)),
                  pl.BlockSpec((ROT_HALF, 1), lambda i: (0, 0))],
        out_specs=pl.BlockSpec((tile, N_ROPE_TAB * SLOT), lambda i: (i, 0)),
        out_shape=jax.ShapeDtypeStruct((n, N_ROPE_TAB * SLOT), F32),
        compiler_params=pltpu.CompilerParams(dimension_semantics=("arbitrary",)),
        name="rope_trig",
    )(pos_flat.reshape(1, n), inv.reshape(ROT_HALF, 1))


def _rope_slot(y, tab):
    cos_t = tab[:, 0:SLOT]
    sin_t = tab[:, SLOT:2 * SLOT]
    lane = lax.broadcasted_iota(jnp.int32, y.shape, 1) & (HEAD_DIM - 1)
    up = pltpu.roll(y, SLOT - ROT_HALF, axis=1)
    down = pltpu.roll(y, ROT_HALF, axis=1)
    return y * cos_t + jnp.where(lane < ROT_HALF, up, down) * sin_t


def _ffn_kernel(x_ref, gpre_ref, wg_hbm, wu_hbm, wd_hbm, gpost_ref, o_ref, wg_ref, wu_ref, wd_ref, sem):
    n_chunks = D_FF // FF_CHUNK

    def chunk_copies(c):
        sl = slice(c * FF_CHUNK, (c + 1) * FF_CHUNK)
        return (pltpu.make_async_copy(wg_hbm.at[:, sl], wg_ref.at[:, sl], sem.at[0, c]),
                pltpu.make_async_copy(wu_hbm.at[:, sl], wu_ref.at[:, sl], sem.at[1, c]),
                pltpu.make_async_copy(wd_hbm.at[sl, :], wd_ref.at[sl, :], sem.at[2, c]))

    def body(first_step):
        x = x_ref[...]
        hn = _rms(x, gpre_ref[...])
        for c in range(n_chunks):
            sl = slice(c * FF_CHUNK, (c + 1) * FF_CHUNK)
            if first_step:
                for copy in chunk_copies(c):
                    copy.wait()
            g = _dot(hn, wg_ref[:, sl])
            u = _dot(hn, wu_ref[:, sl])
            d = _dot(g * jax.nn.sigmoid(g) * u, wd_ref[sl, :])
            if c == 0:
                o_ref[...] = d
            else:
                o_ref[...] += d
        o_ref[...] = x + 0.5 * _rms(o_ref[...], gpost_ref[...])

    @pl.when(pl.program_id(0) == 0)
    def _():
        for c in range(n_chunks):
            for copy in chunk_copies(c):
                copy.start()
        body(True)

    @pl.when(pl.program_id(0) > 0)
    def _():
        body(False)


def _ffn(x, g_pre, w_gate, w_up, w_down, g_post):
    t = x.shape[0]
    row = lambda i: (i, 0)
    fixed = lambda i: (0, 0)
    in_hbm = pl.BlockSpec(memory_space=pl.ANY)
    weights = 3 * _nbytes((D_MODEL, D_FF), F32)
    blocks = 2 * _nbytes((TOK_TILE, D_MODEL), F32)
    return pl.pallas_call(
        _ffn_kernel,
        grid=(t // TOK_TILE,),
        in_specs=[
            pl.BlockSpec((TOK_TILE, D_MODEL), row),
            pl.BlockSpec((1, D_MODEL), fixed),
            in_hbm, in_hbm, in_hbm,
            pl.BlockSpec((1, D_MODEL), fixed),
        ],
        out_specs=pl.BlockSpec((TOK_TILE, D_MODEL), row),
        out_shape=jax.ShapeDtypeStruct((t, D_MODEL), F32),
        scratch_shapes=[pltpu.VMEM((D_MODEL, D_FF), F32), pltpu.VMEM((D_MODEL, D_FF), F32),
                        pltpu.VMEM((D_FF, D_MODEL), F32),
                        pltpu.SemaphoreType.DMA((3, D_FF // FF_CHUNK))],
        compiler_params=pltpu.CompilerParams(
            dimension_semantics=("arbitrary",), vmem_limit_bytes=_vmem_limit(blocks, weights)),
        name="ffn",
    )(x, g_pre.reshape(1, -1), w_gate, w_up, w_down, g_post.reshape(1, -1))


OFF_Q = 0
OFF_CMP = OFF_Q + Q_WIDTH
OFF_KS = OFF_CMP + 2 * KV_WIDTH
OFF_VS = OFF_KS + KV_WIDTH
OFF_KW = OFF_VS + KV_WIDTH
OFF_VW = OFF_KW + KV_WIDTH
OFF_GATE = OFF_VW + KV_WIDTH
OFF_POOL = OFF_GATE + N_GATES
OFF_MERGE = OFF_POOL + POOL_WIDTH
W_CMP = 2 * KV_WIDTH
CHUNK_TOK = CMP_STRIDE
VT_SLAB = N_KV * VT_ROWS


def _in_proj_kernel(x_ref, g_ref, w_ref, tab_ref, q_ref, ks_ref, kw_ref, vst_ref,
                    vwt_ref, cmp_ref, gate_ref, pool_ref, cmp_scr):
    rows = x_ref.shape[0] // IN_SPLIT
    wide = 4 * SLOT
    aux = (lax.broadcasted_iota(jnp.int32, (VT_ROWS - HEAD_DIM, KT), 0) == 0).astype(BF16)

    def part(k):
        rs = slice(k * rows, (k + 1) * rows)
        hn = _rms(x_ref[rs, :], g_ref[...])
        tab = tab_ref[rs, :]
        tab_q = tab * (HEAD_DIM ** -0.5 * LOG2_E)
        for c in range(Q_WIDTH // wide):
            y = _dot_nt(hn, w_ref[OFF_Q + c * wide:OFF_Q + (c + 1) * wide, :])
            for j in range(wide // SLOT):
                sl = slice(c * wide + j * SLOT, c * wide + (j + 1) * SLOT)
                q_ref[rs, sl] = _rope_slot(y[:, j * SLOT:(j + 1) * SLOT], tab_q).astype(BF16)
        for k_ref, off in ((ks_ref, OFF_KS), (kw_ref, OFF_KW)):
            y = _dot_nt(hn, w_ref[off:off + KV_WIDTH, :])
            for j in range(KV_WIDTH // SLOT):
                sl = slice(j * SLOT, (j + 1) * SLOT)
                k_ref[rs, sl] = _rope_slot(y[:, sl], tab).astype(BF16)
        y = _dot_nt(hn, w_ref[OFF_CMP:OFF_CMP + W_CMP, :])
        for j in range(W_CMP // LANES):
            cmp_scr[j, rs, :] = y[:, j * LANES:(j + 1) * LANES]
        chunk_rows = rows // CHUNK_TOK
        for l in range(CHUNK_TOK):
            for j in range(W_CMP // LANES):
                c0 = l * W_CMP + j * LANES
                cmp_ref[k * chunk_rows:(k + 1) * chunk_rows, c0:c0 + LANES] = (
                    cmp_scr[j, pl.ds(k * rows + l, chunk_rows, stride=CHUNK_TOK), :].astype(BF16))
        gate_ref[rs, :] = _dot_nt(hn, w_ref[OFF_GATE:OFF_GATE + SLOT, :])
        pool_ref[rs, :] = _dot_nt(hn, w_ref[OFF_POOL:OFF_MERGE, :])
        tiles = rows // KT
        for vt_ref, off in ((vst_ref, OFF_VS), (vwt_ref, OFF_VW)):
            vt = _dot_nt(w_ref[off:off + KV_WIDTH, :], hn).astype(BF16)
            for j in range(tiles):
                for g in range(N_KV):
                    vt_ref[k * tiles + j, g * VT_ROWS:g * VT_ROWS + HEAD_DIM, :] = (
                        vt[g * HEAD_DIM:(g + 1) * HEAD_DIM, j * KT:(j + 1) * KT])
                    vt_ref[k * tiles + j, g * VT_ROWS + HEAD_DIM:(g + 1) * VT_ROWS, :] = aux

    for k in range(IN_SPLIT):
        part(k)


def _in_proj(x1, g_pre, w_in_t, tab):
    t = x1.shape[0]
    row = lambda i: (i, 0)
    fixed = lambda i: (0, 0)
    once = pl.Buffered(1)
    tile = IN_SPLIT * TOK_TILE
    tiles = tile // KT
    chunk_rows = tile // CHUNK_TOK
    slab = pl.BlockSpec((tiles, VT_SLAB, KT), lambda i: (i, 0, 0))
    slab_shape = jax.ShapeDtypeStruct((t // KT, VT_SLAB, KT), BF16)
    out_specs = [
        pl.BlockSpec((tile, Q_WIDTH), row), pl.BlockSpec((tile, KV_WIDTH), row),
        pl.BlockSpec((tile, KV_WIDTH), row), slab, slab,
        pl.BlockSpec((chunk_rows, CHUNK_TOK * W_CMP), row),
        pl.BlockSpec((tile, SLOT), row), pl.BlockSpec((tile, POOL_WIDTH), row)]
    out_shape = [
        jax.ShapeDtypeStruct((t, Q_WIDTH), BF16), jax.ShapeDtypeStruct((t, KV_WIDTH), BF16),
        jax.ShapeDtypeStruct((t, KV_WIDTH), BF16), slab_shape, slab_shape,
        jax.ShapeDtypeStruct((t // CHUNK_TOK, CHUNK_TOK * W_CMP), BF16),
        jax.ShapeDtypeStruct((t, SLOT), F32), jax.ShapeDtypeStruct((t, POOL_WIDTH), F32)]
    weights = _nbytes((OFF_MERGE, D_MODEL), F32)
    blocks = (_nbytes((tile, D_MODEL), F32) + _nbytes((tile, N_ROPE_TAB * SLOT), F32)
              + _nbytes((tile, Q_WIDTH + 2 * KV_WIDTH + W_CMP), BF16)
              + 2 * _nbytes((tiles, VT_SLAB, KT), BF16) + _nbytes((tile, SLOT + POOL_WIDTH), F32))
    res = pl.pallas_call(
        _in_proj_kernel,
        grid=(t // tile,),
        in_specs=[
            pl.BlockSpec((tile, D_MODEL), row),
            pl.BlockSpec((1, D_MODEL), fixed),
            pl.BlockSpec((pl.Element(OFF_MERGE), pl.Element(D_MODEL)), fixed, pipeline_mode=once),
            pl.BlockSpec((tile, N_ROPE_TAB * SLOT), row),
        ],
        out_specs=out_specs,
        out_shape=out_shape,
        scratch_shapes=[pltpu.VMEM((W_CMP // LANES, tile, LANES), F32)],
        compiler_params=pltpu.CompilerParams(
            dimension_semantics=("arbitrary",),
            vmem_limit_bytes=_vmem_limit(blocks, weights + _nbytes((tile, W_CMP), F32))),
        name="in_proj",
    )(x1, g_pre.reshape(1, -1), w_in_t, tab)
    return res


def _compress_kernel(x_ref, w1k_ref, w1v_ref, pek_ref, pev_ref, w1k_raw_ref, w1v_raw_ref,
                     w2k_ref, w2v_ref, tab_ref, kc_ref, vc_ref, acck_ref, accv_ref):
    l = pl.program_id(0)

    @pl.when(l == 0)
    def _():
        acck_ref[...] = jnp.zeros(acck_ref.shape, F32)
        accv_ref[...] = jnp.zeros(accv_ref.shape, F32)

    for acc_ref, w1_ref, c0 in ((acck_ref, w1k_ref, 0), (accv_ref, w1v_ref, KV_WIDTH)):
        for g in range(N_KV):
            acc_ref[g] += _dot(x_ref[:, c0 + g * HEAD_DIM:c0 + (g + 1) * HEAD_DIM], w1_ref[0])

    @pl.when(l == CHUNK_TOK - 1)
    def _():
        rows = acck_ref.shape[1]
        tab = tab_ref[...]
        for acc_ref, pe_ref, w1_raw_ref, w2_ref, o_ref, rope in (
                (acck_ref, pek_ref, w1k_raw_ref, w2k_ref, kc_ref, True),
                (accv_ref, pev_ref, w1v_raw_ref, w2v_ref, vc_ref, False)):
            bias = _dot(pe_ref[...], w1_raw_ref[...])[0:1, :]
            for g in range(N_KV):
                first = acc_ref[g, :, 0:CMP_HIDDEN]
                second = pltpu.roll(acc_ref[g, :, CMP_HIDDEN:2 * CMP_HIDDEN], rows - 1, axis=0)
                hid = jax.nn.gelu(first + second + bias).astype(BF16)
                out = _dot(hid, w2_ref[...])
                if rope:
                    out = _rope_slot(out, tab)
                o_ref[:, g * SLOT:(g + 1) * SLOT] = out.astype(BF16)


def _compress_weights(w1, w2):
    w1 = w1.reshape(2, CHUNK_TOK, HEAD_DIM, CMP_HIDDEN)
    w1_tok = jnp.concatenate([w1[0], w1[1]], axis=-1).astype(BF16)
    return w1_tok, jnp.pad(w2, ((0, 0), (0, SLOT - HEAD_DIM))).astype(BF16)


def _compress(cmp_rows, cmp_pe_k, cmp_w1_k, cmp_w2_k, cmp_pe_v, cmp_w1_v, cmp_w2_v, tab_cmp):
    rows = cmp_rows.shape[0]
    w1k, w2k = _compress_weights(cmp_w1_k, cmp_w2_k)
    w1v, w2v = _compress_weights(cmp_w1_v, cmp_w2_v)
    sub = 8
    pek = jnp.broadcast_to(cmp_pe_k.reshape(1, -1), (sub, CMP_LEN * HEAD_DIM)).astype(BF16)
    pev = jnp.broadcast_to(cmp_pe_v.reshape(1, -1), (sub, CMP_LEN * HEAD_DIM)).astype(BF16)
    fixed = lambda l: (0, 0)
    w_kv_pad = N_KV * SLOT
    blocks = (_nbytes((rows, W_CMP), BF16) + 2 * _nbytes((HEAD_DIM, 2 * CMP_HIDDEN), BF16)
              + 2 * _nbytes((sub + CMP_LEN * HEAD_DIM, CMP_HIDDEN), BF16)
              + 2 * _nbytes((CMP_HIDDEN, SLOT), BF16)
              + _nbytes((rows, N_ROPE_TAB * SLOT), F32) + 2 * _nbytes((rows, w_kv_pad), BF16))
    acc = pltpu.VMEM((N_KV, rows, 2 * CMP_HIDDEN), F32)
    return pl.pallas_call(
        _compress_kernel,
        grid=(CHUNK_TOK,),
        in_specs=[
            pl.BlockSpec((rows, W_CMP), lambda l: (0, l)),
            pl.BlockSpec((1, HEAD_DIM, 2 * CMP_HIDDEN), lambda l: (l, 0, 0)),
            pl.BlockSpec((1, HEAD_DIM, 2 * CMP_HIDDEN), lambda l: (l, 0, 0)),
            pl.BlockSpec((sub, CMP_LEN * HEAD_DIM), fixed),
            pl.BlockSpec((sub, CMP_LEN * HEAD_DIM), fixed),
            pl.BlockSpec((CMP_LEN * HEAD_DIM, CMP_HIDDEN), fixed),
            pl.BlockSpec((CMP_LEN * HEAD_DIM, CMP_HIDDEN), fixed),
            pl.BlockSpec((CMP_HIDDEN, SLOT), fixed),
            pl.BlockSpec((CMP_HIDDEN, SLOT), fixed),
            pl.BlockSpec((rows, N_ROPE_TAB * SLOT), fixed),
        ],
        out_specs=[pl.BlockSpec((rows, w_kv_pad), fixed)] * 2,
        out_shape=[jax.ShapeDtypeStruct((rows, w_kv_pad), BF16)] * 2,
        scratch_shapes=[acc, acc],
        compiler_params=pltpu.CompilerParams(
            dimension_semantics=("arbitrary",),
            vmem_limit_bytes=_vmem_limit(blocks, 2 * _nbytes(acc.shape, F32))),
        name="compress",
    )(cmp_rows, w1k, w1v, pek, pev, cmp_w1_k.astype(BF16), cmp_w1_v.astype(BF16), w2k, w2v, tab_cmp)


def _nsa_kernel(q_ref, gate_ref, kc_ref, vc_ref, ks_ref, vst_ref, kw_ref, vwt_ref, ov_ref,
                o_ref, ksp_ref, kwp_ref, qg_ref, mix_ref, *state_refs):
    i = pl.program_id(1)
    seq = ks_ref.shape[1]
    cols_g = GROUP * TQ
    chains_g = cols_g // CHAIN_COLS
    n_chains = N_KV * chains_g
    n_cmp = kc_ref.shape[1]
    sel_state = (state_refs[0:n_chains], state_refs[n_chains:2 * n_chains])
    win_state = (state_refs[2 * n_chains:3 * n_chains], state_refs[3 * n_chains:4 * n_chains])
    lane_q = lax.broadcasted_iota(jnp.int32, (TQ, SLOT), 1)
    lane_h = lax.broadcasted_iota(jnp.int32, (HALF, SLOT), 1)

    def head_slot(x_ref, rows, j):
        two = x_ref[rows, (j // 2) * SLOT:(j // 2 + 1) * SLOT]
        if j % 2:
            words = pltpu.roll(pltpu.bitcast(two, jnp.uint32), HEAD_DIM, axis=1)
            two = pltpu.bitcast(words, BF16)
        return two

    @pl.when(i == 0)
    def _():
        for c in range(seq // KT):
            rows = pl.ds(c * KT, KT)
            lane = lax.broadcasted_iota(jnp.int32, (KT, SLOT), 1)
            key_blk = (c * KT + lax.broadcasted_iota(jnp.int32, (KT, SLOT), 0)) >> SEL_SHIFT
            onehot = jnp.where(lane - BLK_LANE0 == key_blk, 1.0, 0.0).astype(BF16)
            zeros = jnp.zeros((KT, SLOT), BF16)
            for g in range(N_KV):
                gs = slice(g * SLOT, (g + 1) * SLOT)
                ksp_ref[rows, gs] = jnp.where(lane < HEAD_DIM, head_slot(ks_ref.at[0], rows, g), onehot)
                kwp_ref[rows, gs] = jnp.where(lane < HEAD_DIM, head_slot(kw_ref.at[0], rows, g), zeros)

    def q_rows(r, x):
        return slice((x * GROUP + r) * HALF, (x * GROUP + r + 1) * HALF)

    def chain_half(c):
        return c * CHAIN_COLS // (GROUP * HALF)

    for h in range(N_HEADS):
        g, r = divmod(h, GROUP)
        q_pad = jnp.where(lane_q < HEAD_DIM, head_slot(q_ref, slice(None), h), jnp.zeros((TQ, SLOT), BF16))
        for x in range(2):
            qg_ref[g, q_rows(r, x), :] = q_pad[x * HALF:(x + 1) * HALF]

    gates_t = jax.nn.sigmoid(gate_ref[...]).T
    key_h = lax.broadcasted_iota(jnp.int32, (HALF, CHAIN_COLS), 0)
    qry_h = lax.broadcasted_iota(jnp.int32, (HALF, CHAIN_COLS), 1) & (HALF - 1)
    tri_le, tri_gt = key_h <= qry_h, key_h > qry_h
    diag_plan = ((slice(0, HALF), tri_le, slice(0, HALF)), (slice(0, KT), tri_le, slice(HALF, KT)))
    far_plan = ((slice(0, KT), tri_gt, slice(0, HALF)), (slice(HALF, KT), tri_gt, slice(0, HALF)))

    def gate_row(branch, h):
        c = branch * N_HEADS + h
        return gates_t[c:c + 1, :]

    sub = 8
    cmp_per_sub = sub * SEL_LEN // CMP_STRIDE

    def compressed_scores(ng):
        n_c = ng * cmp_per_sub
        cmp_scores = []
        for h in range(N_HEADS):
            g, r = divmod(h, GROUP)
            q_h = jnp.concatenate([qg_ref[g, q_rows(r, 0), :], qg_ref[g, q_rows(r, 1), :]], axis=0)
            cmp_scores.append(_dot_nt(kc_ref[0, 0:n_c, g * SLOT:(g + 1) * SLOT], q_h))
        return cmp_scores

    def compressed_and_select(ng, start, cmp_scores):
        n_blk, n_c = ng * sub, ng * cmp_per_sub
        t_cmp = start + lax.broadcasted_iota(jnp.int32, (n_c, TQ), 1)
        n_idx = lax.broadcasted_iota(jnp.int32, (n_c, TQ), 0)
        cmp_valid = n_idx * CMP_STRIDE + (CMP_LEN - 1) <= t_cmp
        any_cmp = start + lax.broadcasted_iota(jnp.int32, (1, TQ), 1) >= CMP_LEN - 1
        t_row = start + lax.broadcasted_iota(jnp.int32, (n_blk, TQ), 1)
        blk = lax.broadcasted_iota(jnp.int32, (n_blk, TQ), 0)
        forced = (blk == t_row >> SEL_SHIFT) | (blk == 0)
        causal_blk = blk * SEL_LEN <= t_row
        idx8 = lax.broadcasted_iota(jnp.int32, (sub, TQ), 0)
        pad_c = jnp.zeros((n_cmp - n_c, TQ), F32)
        for g in range(N_KV):
            vct_g = vc_ref[0, :, g * SLOT:(g + 1) * SLOT].astype(F32).T[0:HEAD_DIM].astype(BF16)
            p_sum = jnp.zeros((n_c, TQ), F32)
            for r in range(GROUP):
                h = g * GROUP + r
                s = jnp.where(cmp_valid, cmp_scores[h], NEG_INF)
                p = jnp.exp2(s - jnp.max(s, axis=0, keepdims=True))
                inv = jnp.where(any_cmp, 1.0 / jnp.sum(p, axis=0, keepdims=True), 0.0)
                p = p * inv
                p_sum = p_sum + p
                p_all = jnp.concatenate([p, pad_c], axis=0) if ng * cmp_per_sub < n_cmp else p
                mix_ref[h] = gate_row(0, h) * _dot(vct_g, p_all.astype(BF16))
            p_sum_all = jnp.concatenate([p_sum, pad_c], axis=0) if ng * cmp_per_sub < n_cmp else p_sum
            imp = jnp.dot(ov_ref[0:n_blk, :], p_sum_all, precision=lax.Precision.HIGHEST,
                          preferred_element_type=F32)
            score = jnp.where(causal_blk, jnp.where(forced, FORCE_SCORE, imp), NEG_INF)
            rows8 = [score[k * sub:(k + 1) * sub] for k in range(ng)]
            ranks = [jnp.zeros((sub, TQ), jnp.int32) for _ in rows8]
            for c in range(n_blk):
                other = score[c:c + 1, :]
                for k, mine in enumerate(rows8):
                    if k * sub > c:
                        beats = other >= mine
                    elif (k + 1) * sub - 1 <= c:
                        beats = other > mine
                    else:
                        beats = (other > mine) | ((other == mine) & (idx8 > c - k * sub))
                    ranks[k] = ranks[k] + beats.astype(jnp.int32)
            rank = jnp.concatenate(ranks, axis=0) if ng > 1 else ranks[0]
            bias = jnp.where(causal_blk & (rank < N_SEL), 0.0, NEG_INF)
            bias_t = jnp.concatenate([jnp.zeros((BLK_LANE0, TQ), F32), bias,
                                      jnp.zeros((SLOT - BLK_LANE0 - n_blk, TQ), F32)], axis=0)
            bias_q = bias_t.T.astype(BF16)
            for r in range(GROUP):
                for x in range(2):
                    rs = q_rows(r, x)
                    qg_ref[g, rs, :] = jnp.where(lane_h < HEAD_DIM, qg_ref[g, rs, :],
                                                 bias_q[x * HALF:(x + 1) * HALF])

    ng_now = ((i + 1) * TQ + sub * SEL_LEN - 1) // (sub * SEL_LEN)
    for ng in range(1, seq // (sub * SEL_LEN) + 1):

        @pl.when(ng_now == ng)
        def _(ng=ng):
            compressed_and_select(ng, i * TQ, compressed_scores(ng))

    def flash_init(state):
        for m_ref, acc_ref in zip(*state):
            m_ref[...] = jnp.full(m_ref.shape, NEG_INF, F32)
            acc_ref[...] = jnp.zeros(acc_ref.shape, F32)

    def flash_tiles(tiles):
        jobs = [(tile, c) for tile in tiles for c in range(n_chains)]

        def key_plan(job):
            (_, _, _, plan, _), chain = job
            return (slice(0, KT), None, None) if plan is None else plan[chain_half(chain % chains_g)]

        scores = {}
        for step in range(len(jobs) + PIPE_DEPTH):
            if step < len(jobs):
                (kp_ref, _, kt, _, _), chain = jobs[step]
                g, c = divmod(chain, chains_g)
                keys, mask, rows = key_plan(jobs[step])
                off = pl.multiple_of(kt * KT + keys.start, HALF)
                s = _dot_nt(kp_ref[pl.ds(off, keys.stop - keys.start), g * SLOT:(g + 1) * SLOT],
                            qg_ref[g, c * CHAIN_COLS:(c + 1) * CHAIN_COLS, :])
                if mask is not None:
                    parts = [s[0:rows.start], jnp.where(mask, s[rows], NEG_INF), s[rows.stop:]]
                    s = jnp.concatenate([p for p in parts if p.shape[0]], axis=0)
                scores[step] = s.astype(BF16)
            done = step - PIPE_DEPTH
            if done >= 0:
                (_, vt_ref, kt, _, (m_refs, acc_refs)), chain = jobs[done]
                g = chain // chains_g
                keys = key_plan(jobs[done])[0]
                m_ref, acc_ref = m_refs[chain], acc_refs[chain]
                s = scores.pop(done)
                m_old = m_ref[...]
                packed = [s[r * BF16_SUBLANES:(r + 1) * BF16_SUBLANES]
                          for r in range(s.shape[0] // BF16_SUBLANES)]
                m_tile = functools.reduce(jnp.maximum, packed).astype(F32)
                m_new = jnp.maximum(m_old, jnp.max(m_tile, axis=0, keepdims=True))
                p = jnp.exp2(s - m_new.astype(BF16))
                pv = _dot(vt_ref[kt, g * VT_ROWS:(g + 1) * VT_ROWS, keys], p)
                acc_ref[...] = jnp.exp2(m_old - m_new) * acc_ref[...] + pv
                m_ref[...] = m_new

    def flash_mix(branch, state):
        _, acc_refs = state
        for h in range(N_HEADS):
            g, r = divmod(h, GROUP)
            gate = gate_row(branch, h)
            for x in range(2):
                c, c0 = divmod(q_rows(r, x).start, CHAIN_COLS)
                acc = acc_refs[g * chains_g + c][:, c0:c0 + HALF]
                qs = slice(x * HALF, (x + 1) * HALF)
                scale = gate[:, qs] * (1.0 / acc[ONE_ROW:ONE_ROW + 1, :])
                mix_ref[h, :, qs] = mix_ref[h, :, qs] + scale * acc[0:HEAD_DIM]

    flash_init(sel_state)
    flash_init(win_state)

    def sel_past(kt):
        return (ksp_ref, vst_ref, kt, None, sel_state)

    def sel_pair(j, carry):
        flash_tiles([sel_past(2 * j), sel_past(2 * j + 1)])
        return carry

    lax.fori_loop(0, i >> 1, sel_pair, 0)

    @pl.when(i & 1 == 1)
    def _():
        flash_tiles([sel_past(i - 1)])

    n_back = WINDOW // KT
    sel_diag = (ksp_ref, vst_ref, i, diag_plan, sel_state)
    win_diag = (kwp_ref, vwt_ref, i, diag_plan, win_state)

    def win_back(d):
        return (kwp_ref, vwt_ref, i - d, far_plan if d == n_back else None, win_state)

    for have in range(n_back + 1):
        cond = (i == have) if have < n_back else (i >= have)

        @pl.when(cond)
        def _(have=have):
            flash_tiles([sel_diag] + [win_back(d) for d in range(have, 0, -1)] + [win_diag])

    flash_mix(1, sel_state)
    flash_mix(2, win_state)
    for h2 in range(N_HEADS // 2):
        pair = jnp.concatenate([mix_ref[2 * h2], mix_ref[2 * h2 + 1]], axis=0)
        o_ref[:, h2 * SLOT:(h2 + 1) * SLOT] = pair.T.astype(BF16)


def _overlap(n_cmp_pad, n_slc):
    c0 = jnp.arange(n_cmp_pad) * CMP_STRIDE
    s0 = jnp.arange(n_slc) * SEL_LEN
    ov = jnp.minimum(c0[None, :] + CMP_LEN, s0[:, None] + SEL_LEN) - jnp.maximum(c0[None, :], s0[:, None])
    return jnp.clip(ov, 0).astype(F32) / CMP_LEN


def _nsa(q, gate, kc, vc, ks, vst, kw, vwt, batch, seq_len):
    n_slc = seq_len // SEL_LEN
    n_cmp_pad = kc.shape[1]
    w_kv_pad = N_KV * SLOT
    assert n_slc <= SLOT - BLK_LANE0 and n_slc % 8 == 0
    assert CHAIN_COLS % HALF == 0 and (GROUP * HALF) % CHAIN_COLS == 0 and KT == 2 * HALF
    ov = _overlap(n_cmp_pad, n_slc)
    nq = seq_len // TQ
    n_kt = seq_len // KT
    qrow = lambda b, i: (b * nq + i, 0)
    per_b = lambda b, i: (b, 0, 0)
    fixed = lambda b, i: (0, 0)
    blocks = (_nbytes((TQ, Q_WIDTH), BF16) + _nbytes((TQ, SLOT), F32)
              + 2 * _nbytes((n_cmp_pad, w_kv_pad), BF16) + 2 * _nbytes((seq_len, KV_WIDTH), BF16)
              + 2 * _nbytes((n_kt, VT_SLAB, KT), BF16)
              + _nbytes(ov.shape, F32) + _nbytes((TQ, Q_WIDTH), BF16))
    n_chains = N_KV * GROUP * TQ // CHAIN_COLS
    run_max = [pltpu.VMEM((1, CHAIN_COLS), F32)] * n_chains
    run_acc = [pltpu.VMEM((VT_ROWS, CHAIN_COLS), F32)] * n_chains
    scratch_shapes = (
        [pltpu.VMEM((seq_len, w_kv_pad), BF16),
         pltpu.VMEM((seq_len, w_kv_pad), BF16),
         pltpu.VMEM((N_KV, GROUP * TQ, SLOT), BF16),
         pltpu.VMEM((N_HEADS, HEAD_DIM, TQ), F32)]
        + run_max + run_acc + run_max + run_acc)
    scratch = sum(_nbytes(s.shape, s.dtype) for s in scratch_shapes)
    return pl.pallas_call(
        _nsa_kernel,
        grid=(batch, nq),
        in_specs=[
            pl.BlockSpec((TQ, Q_WIDTH), qrow),
            pl.BlockSpec((TQ, SLOT), qrow),
            pl.BlockSpec((1, n_cmp_pad, w_kv_pad), per_b),
            pl.BlockSpec((1, n_cmp_pad, w_kv_pad), per_b),
            pl.BlockSpec((1, seq_len, KV_WIDTH), per_b),
            pl.BlockSpec((n_kt, VT_SLAB, KT), per_b),
            pl.BlockSpec((1, seq_len, KV_WIDTH), per_b),
            pl.BlockSpec((n_kt, VT_SLAB, KT), per_b),
            pl.BlockSpec(ov.shape, fixed),
        ],
        out_specs=pl.BlockSpec((TQ, Q_WIDTH), qrow),
        out_shape=jax.ShapeDtypeStruct((batch * seq_len, Q_WIDTH), BF16),
        scratch_shapes=scratch_shapes,
        compiler_params=pltpu.CompilerParams(
            dimension_semantics=("arbitrary", "arbitrary"),
            vmem_limit_bytes=_vmem_limit(blocks, scratch)),
        name="nsa",
    )(q, gate, kc.reshape(batch, n_cmp_pad, w_kv_pad), vc.reshape(batch, n_cmp_pad, w_kv_pad),
      ks.reshape(batch, seq_len, KV_WIDTH), vst, kw.reshape(batch, seq_len, KV_WIDTH), vwt, ov)


def _pool_kernel(u_ref, w_ref, scale_ref, o_ref):
    seq = u_ref.shape[0]
    t = lax.broadcasted_iota(jnp.int32, (seq, POOL_GROUP), 0)

    def shifted(x, k):
        return jnp.where(t >= k, pltpu.roll(x, k, axis=0), 0.0)

    for gi, w in enumerate(POOL_WINDOWS):
        sl = slice(gi * POOL_GROUP, (gi + 1) * POOL_GROUP)
        x = u_ref[:, sl]
        wsum = x
        span = 1
        while span < w:
            wsum = wsum + shifted(wsum, span)
            span *= 2
        cnt = jnp.minimum(t + 1, w).astype(F32)
        pooled = (wsum / cnt - x).astype(BF16)
        o_ref[:, sl] = (_dot(pooled, w_ref[gi]) * scale_ref[:, sl]).astype(BF16)


def _pool(pool_in, pool_w, pool_scale, batch, seq_len):
    for w in POOL_WINDOWS:
        assert w & (w - 1) == 0
    blocks = (_nbytes((seq_len, POOL_WIDTH), F32) + _nbytes(pool_w.shape, BF16)
              + _nbytes((seq_len, POOL_WIDTH), BF16))
    return pl.pallas_call(
        _pool_kernel,
        grid=(batch,),
        in_specs=[
            pl.BlockSpec((seq_len, POOL_WIDTH), lambda b: (b, 0)),
            pl.BlockSpec(pool_w.shape, lambda b: (0, 0, 0)),
            pl.BlockSpec((1, POOL_WIDTH), lambda b: (0, 0)),
        ],
        out_specs=pl.BlockSpec((seq_len, POOL_WIDTH), lambda b: (b, 0)),
        out_shape=jax.ShapeDtypeStruct((batch * seq_len, POOL_WIDTH), BF16),
        compiler_params=pltpu.CompilerParams(
            dimension_semantics=("arbitrary",), vmem_limit_bytes=_vmem_limit(blocks)),
        name="pool",
    )(pool_in, pool_w.astype(BF16), pool_scale.reshape(1, -1))


def _merge_kernel(x_ref, gpre_ref, wt_ref, nsa_ref, wa_ref, pool_ref, wp_ref, wo_ref, gpost_ref,
                  o_ref):
    rows = x_ref.shape[0] // MERGE_SPLIT
    for k in range(MERGE_SPLIT):
        rs = slice(k * rows, (k + 1) * rows)
        x = x_ref[rs, :]
        hn = _rms(x, gpre_ref[...])
        g_attn = jax.nn.sigmoid(_dot_nt(hn, wt_ref[0:D_MODEL, :]))
        g_pool = jax.nn.sigmoid(_dot_nt(hn, wt_ref[D_MODEL:2 * D_MODEL, :]))
        y = (g_attn * _dot(nsa_ref[rs, :].astype(F32), wa_ref[...])
             + g_pool * _dot(pool_ref[rs, :].astype(F32), wp_ref[...]))
        h = _dot(y, wo_ref[...])
        o_ref[rs, :] = x + _rms(h, gpost_ref[...])


def _merge(x1, g_pre, w_in_t, o_nsa, w_attn, o_pool, w_pool, w_out, g_post):
    t = x1.shape[0]
    row = lambda i: (i, 0)
    fixed = lambda i: (0, 0)
    once = pl.Buffered(1)
    tile = MERGE_SPLIT * TOK_TILE
    weights = (_nbytes((2 * D_MODEL, D_MODEL), F32) + _nbytes((Q_WIDTH, D_MODEL), F32)
               + _nbytes((POOL_WIDTH, D_MODEL), F32) + _nbytes((D_MODEL, D_MODEL), F32))
    blocks = (2 * _nbytes((tile, D_MODEL), F32) + _nbytes((tile, Q_WIDTH), BF16)
              + _nbytes((tile, POOL_WIDTH), BF16))
    gate_rows = pl.BlockSpec((pl.Element(2 * D_MODEL), pl.Element(D_MODEL)), lambda i: (OFF_MERGE, 0),
                             pipeline_mode=once)
    return pl.pallas_call(
        _merge_kernel,
        grid=(t // tile,),
        in_specs=[
            pl.BlockSpec((tile, D_MODEL), row),
            pl.BlockSpec((1, D_MODEL), fixed),
            gate_rows,
            pl.BlockSpec((tile, Q_WIDTH), row),
            pl.BlockSpec((Q_WIDTH, D_MODEL), fixed, pipeline_mode=once),
            pl.BlockSpec((tile, POOL_WIDTH), row),
            pl.BlockSpec((POOL_WIDTH, D_MODEL), fixed, pipeline_mode=once),
            pl.BlockSpec((D_MODEL, D_MODEL), fixed, pipeline_mode=once),
            pl.BlockSpec((1, D_MODEL), fixed),
        ],
        out_specs=pl.BlockSpec((tile, D_MODEL), row),
        out_shape=jax.ShapeDtypeStruct((t, D_MODEL), F32),
        compiler_params=pltpu.CompilerParams(
            dimension_semantics=("arbitrary",), vmem_limit_bytes=_vmem_limit(blocks, weights)),
        name="merge",
    )(x1, g_pre.reshape(1, -1), w_in_t, o_nsa, w_attn, o_pool, w_pool, w_out, g_post.reshape(1, -1))


def _layer(x, tab, tab_cmp, batch, seq_len, g_ffn1_pre, w_ffn1_gate, w_ffn1_up, w_ffn1_down,
           g_ffn1_post, g_mix_pre, w_in, cmp_pe_k, cmp_w1_k, cmp_w2_k, cmp_pe_v, cmp_w1_v, cmp_w2_v,
           w_attn_branch, pool_w, pool_scale, w_pool_branch, w_out, g_mix_post, g_ffn2_pre,
           w_ffn2_gate, w_ffn2_up, w_ffn2_down, g_ffn2_post):
    x1 = _ffn(x, g_ffn1_pre, w_ffn1_gate, w_ffn1_up, w_ffn1_down, g_ffn1_post)
    w_in_t = w_in.T
    q, ks, kw, vst, vwt, cmp_rows, gate, pool_in = _in_proj(x1, g_mix_pre, w_in_t, tab)
    kc, vc = _compress(cmp_rows, cmp_pe_k, cmp_w1_k, cmp_w2_k, cmp_pe_v, cmp_w1_v, cmp_w2_v, tab_cmp)
    n_cmp_pad = seq_len // CMP_STRIDE
    kc = kc.reshape(batch, n_cmp_pad, N_KV * SLOT)
    vc = vc.reshape(batch, n_cmp_pad, N_KV * SLOT)
    o_nsa = _nsa(q, gate, kc, vc, ks, vst, kw, vwt, batch, seq_len)
    o_pool = _pool(pool_in, pool_w, pool_scale, batch, seq_len)
    x2 = _merge(x1, g_mix_pre, w_in_t, o_nsa, w_attn_branch, o_pool, w_pool_branch, w_out,
                g_mix_post)
    return _ffn(x2, g_ffn2_pre, w_ffn2_gate, w_ffn2_up, w_ffn2_down, g_ffn2_post)


def kernel(x, positions, g_ffn1_pre, w_ffn1_gate, w_ffn1_up, w_ffn1_down, g_ffn1_post, g_mix_pre, w_in, cmp_pe_k, cmp_w1_k, cmp_w2_k, cmp_pe_v, cmp_w1_v, cmp_w2_v, w_attn_branch, pool_w, pool_scale, w_pool_branch, w_out, g_mix_post, g_ffn2_pre, w_ffn2_gate, w_ffn2_up, w_ffn2_down, g_ffn2_post):
    batch, seq_len, d_model = x.shape
    assert d_model == D_MODEL and seq_len % TOK_TILE == 0 and seq_len % TQ == 0
    assert TQ == KT and TQ % SEL_LEN == 0 and TOK_TILE % KT == 0 and WINDOW % KT == 0
    assert seq_len % CMP_STRIDE == 0 and CMP_LEN == 2 * CMP_STRIDE
    t = batch * seq_len
    tab = _rope_tables(positions.reshape(t), seq_len)
    n_cmp = (seq_len - CMP_LEN) // CMP_STRIDE + 1
    n_cmp_pad = seq_len // CMP_STRIDE
    pos_cmp = jnp.pad(positions[:, CMP_LEN - 1::CMP_STRIDE], ((0, 0), (0, n_cmp_pad - n_cmp)))
    tab_cmp = _rope_tables(pos_cmp.reshape(batch * n_cmp_pad), batch * n_cmp_pad)
    xf = x.reshape(t, D_MODEL)
    per_layer = (g_ffn1_pre, w_ffn1_gate, w_ffn1_up, w_ffn1_down, g_ffn1_post, g_mix_pre, w_in,
                 cmp_pe_k, cmp_w1_k, cmp_w2_k, cmp_pe_v, cmp_w1_v, cmp_w2_v, w_attn_branch, pool_w,
                 pool_scale, w_pool_branch, w_out, g_mix_post, g_ffn2_pre, w_ffn2_gate, w_ffn2_up,
                 w_ffn2_down, g_ffn2_post)
    for l in range(g_ffn1_pre.shape[0]):
        xf = _layer(xf, tab, tab_cmp, batch, seq_len, *(p[l] for p in per_layer))
    return xf.reshape(batch, seq_len, D_MODEL)
```

```python
import functools
import math

import jax
import jax.numpy as jnp
from jax import lax
from jax.experimental import pallas as pl
from jax.experimental.pallas import tpu as pltpu

F32 = jnp.float32
BF16 = jnp.bfloat16

D_MODEL = 1024
N_HEADS = 16
HEAD_DIM = 64
N_KV = 4
GROUP = N_HEADS // N_KV
ROT_DIM = HEAD_DIM // 4
ROT_HALF = ROT_DIM // 2
ROPE_THETA = 500000.0
CMP_LEN = 32
CMP_STRIDE = 16
CMP_HIDDEN = 2 * HEAD_DIM
SEL_LEN = 64
SEL_SHIFT = SEL_LEN.bit_length() - 1
N_SEL = 8
WINDOW = 512
POOL_WINDOWS = (2, 4, 8, 16)
POOL_WIDTH = D_MODEL // 2
POOL_GROUP = POOL_WIDTH // len(POOL_WINDOWS)
D_FF = 2816
EPS = 1e-6
NEG_INF = -1e30
FORCE_SCORE = 1e4
Q_WIDTH = N_HEADS * HEAD_DIM
KV_WIDTH = N_KV * HEAD_DIM
N_GATES = 3 * N_HEADS
LOG2_E = math.log2(math.e)

LANES = 128
BF16_SUBLANES = 16
V7X_VMEM_BYTES = 64 * 1024 * 1024
VMEM_COMPILER_RESERVE = 4 * 1024 * 1024
VMEM_MIN_REQUEST = 32 * 1024 * 1024

SLOT = LANES
BLK_LANE0 = HEAD_DIM
VT_ROWS = HEAD_DIM + BF16_SUBLANES
ONE_ROW = HEAD_DIM
TOK_TILE = 512
TQ = 256
KT = 256
HALF = TQ // 2
NSA_STEP_TILES = 4
CHAIN_COLS = 256
PIPE_DEPTH = 10
FF_CHUNK = 256
IN_SPLIT = 2
MERGE_SPLIT = 2
N_ROPE_TAB = 2


def _vmem_limit(block_bytes, scratch_bytes=0):
    need = 2 * block_bytes + scratch_bytes
    return int(min(V7X_VMEM_BYTES - VMEM_COMPILER_RESERVE, max(2 * need, VMEM_MIN_REQUEST)))


def _nbytes(shape, dtype):
    n = 1
    for s in shape:
        n *= s
    return n * jnp.dtype(dtype).itemsize


def _rms(xf, g):
    return xf * lax.rsqrt(jnp.mean(xf * xf, axis=-1, keepdims=True) + EPS) * g


def _dot(a, b):
    return jnp.dot(a, b, preferred_element_type=F32)


def _dot_nt(a, b):
    return lax.dot_general(a, b, (((1,), (1,)), ((), ())), preferred_element_type=F32)


def _rope_trig_kernel(pos_ref, inv_ref, tab_ref):
    tn = pos_ref.shape[1]
    ang = pos_ref[...].astype(F32) * inv_ref[...]
    c, s = jnp.cos(ang), jnp.sin(ang)
    rest = HEAD_DIM - ROT_DIM
    reps = SLOT // HEAD_DIM
    cos_rows = jnp.concatenate([c, c, jnp.ones((rest, tn), F32)] * reps, axis=0)
    sin_rows = jnp.concatenate([-s, s, jnp.zeros((rest, tn), F32)] * reps, axis=0)
    for j in range(tn // LANES):
        cs = slice(j * LANES, (j + 1) * LANES)
        tab_ref[cs, 0:SLOT] = cos_rows[:, cs].T
        tab_ref[cs, SLOT:2 * SLOT] = sin_rows[:, cs].T


def _rope_tables(pos_flat, tile):
    n = pos_flat.shape[0]
    inv = ROPE_THETA ** (-jnp.arange(ROT_HALF, dtype=F32) * (2.0 / ROT_DIM))
    return pl.pallas_call(
        _rope_trig_kernel,
        grid=(n // tile,),
        in_specs=[pl.BlockSpec((1, tile), lambda i: (0, i)),
                  pl.BlockSpec((ROT_HALF, 1), lambda i: (0, 0))],
        out_specs=pl.BlockSpec((tile, N_ROPE_TAB * SLOT), lambda i: (i, 0)),
        out_shape=jax.ShapeDtypeStruct((n, N_ROPE_TAB * SLOT), F32),
        compiler_params=pltpu.CompilerParams(dimension_semantics=("arbitrary",)),
        name="rope_trig",
    )(pos_flat.reshape(1, n), inv.reshape(ROT_HALF, 1))


def _rope_slot(y, tab):
    cos_t = tab[:, 0:SLOT]
    sin_t = tab[:, SLOT:2 * SLOT]
    lane = lax.broadcasted_iota(jnp.int32, y.shape, 1) & (HEAD_DIM - 1)
    up = pltpu.roll(y, SLOT - ROT_HALF, axis=1)
    down = pltpu.roll(y, ROT_HALF, axis=1)
    return y * cos_t + jnp.where(lane < ROT_HALF, up, down) * sin_t


def _ffn_kernel(x_ref, gpre_ref, wg_hbm, wu_hbm, wd_hbm, gpost_ref, o_ref, wg_ref, wu_ref, wd_ref, sem):
    n_chunks = D_FF // FF_CHUNK

    def chunk_copies(c):
        sl = slice(c * FF_CHUNK, (c + 1) * FF_CHUNK)
        return (pltpu.make_async_copy(wg_hbm.at[:, sl], wg_ref.at[:, sl], sem.at[0, c]),
                pltpu.make_async_copy(wu_hbm.at[:, sl], wu_ref.at[:, sl], sem.at[1, c]),
                pltpu.make_async_copy(wd_hbm.at[sl, :], wd_ref.at[sl, :], sem.at[2, c]))

    def body(first_step):
        x = x_ref[...]
        hn = _rms(x, gpre_ref[...])
        for c in range(n_chunks):
            sl = slice(c * FF_CHUNK, (c + 1) * FF_CHUNK)
            if first_step:
                for copy in chunk_copies(c):
                    copy.wait()
            g = _dot(hn, wg_ref[:, sl])
            u = _dot(hn, wu_ref[:, sl])
            d = _dot(g * jax.nn.sigmoid(g) * u, wd_ref[sl, :])
            if c == 0:
                o_ref[...] = d
            else:
                o_ref[...] += d
        o_ref[...] = x + 0.5 * _rms(o_ref[...], gpost_ref[...])

    @pl.when(pl.program_id(0) == 0)
    def _():
        for c in range(n_chunks):
            for copy in chunk_copies(c):
                copy.start()
        body(True)

    @pl.when(pl.program_id(0) > 0)
    def _():
        body(False)


def _ffn(x, g_pre, w_gate, w_up, w_down, g_post):
    t = x.shape[0]
    row = lambda i: (i, 0)
    fixed = lambda i: (0, 0)
    in_hbm = pl.BlockSpec(memory_space=pl.ANY)
    weights = 3 * _nbytes((D_MODEL, D_FF), F32)
    blocks = 2 * _nbytes((TOK_TILE, D_MODEL), F32)
    return pl.pallas_call(
        _ffn_kernel,
        grid=(t // TOK_TILE,),
        in_specs=[
            pl.BlockSpec((TOK_TILE, D_MODEL), row),
            pl.BlockSpec((1, D_MODEL), fixed),
            in_hbm, in_hbm, in_hbm,
            pl.BlockSpec((1, D_MODEL), fixed),
        ],
        out_specs=pl.BlockSpec((TOK_TILE, D_MODEL), row),
        out_shape=jax.ShapeDtypeStruct((t, D_MODEL), F32),
        scratch_shapes=[pltpu.VMEM((D_MODEL, D_FF), F32), pltpu.VMEM((D_MODEL, D_FF), F32),
                        pltpu.VMEM((D_FF, D_MODEL), F32),
                        pltpu.SemaphoreType.DMA((3, D_FF // FF_CHUNK))],
        compiler_params=pltpu.CompilerParams(
            dimension_semantics=("arbitrary",), vmem_limit_bytes=_vmem_limit(blocks, weights)),
        name="ffn",
    )(x, g_pre.reshape(1, -1), w_gate, w_up, w_down, g_post.reshape(1, -1))


OFF_Q = 0
OFF_CMP = OFF_Q + Q_WIDTH
OFF_KS = OFF_CMP + 2 * KV_WIDTH
OFF_VS = OFF_KS + KV_WIDTH
OFF_KW = OFF_VS + KV_WIDTH
OFF_VW = OFF_KW + KV_WIDTH
OFF_GATE = OFF_VW + KV_WIDTH
OFF_POOL = OFF_GATE + N_GATES
OFF_MERGE = OFF_POOL + POOL_WIDTH
W_CMP = 2 * KV_WIDTH
CHUNK_TOK = CMP_STRIDE
VT_SLAB = N_KV * VT_ROWS


def _in_proj_kernel(x_ref, g_ref, w_ref, tab_ref, q_ref, ks_ref, kw_ref, vst_ref,
                    vwt_ref, cmp_ref, gate_ref, pool_ref, cmp_scr):
    rows = x_ref.shape[0] // IN_SPLIT
    wide = 4 * SLOT
    aux = (lax.broadcasted_iota(jnp.int32, (VT_ROWS - HEAD_DIM, KT), 0) == 0).astype(BF16)

    def part(k):
        rs = slice(k * rows, (k + 1) * rows)
        hn = _rms(x_ref[rs, :], g_ref[...])
        tab = tab_ref[rs, :]
        tab_q = tab * (HEAD_DIM ** -0.5 * LOG2_E)
        for c in range(Q_WIDTH // wide):
            y = _dot_nt(hn, w_ref[OFF_Q + c * wide:OFF_Q + (c + 1) * wide, :])
            for j in range(wide // SLOT):
                sl = slice(c * wide + j * SLOT, c * wide + (j + 1) * SLOT)
                q_ref[rs, sl] = _rope_slot(y[:, j * SLOT:(j + 1) * SLOT], tab_q).astype(BF16)
        for k_ref, off in ((ks_ref, OFF_KS), (kw_ref, OFF_KW)):
            y = _dot_nt(hn, w_ref[off:off + KV_WIDTH, :])
            for j in range(KV_WIDTH // SLOT):
                sl = slice(j * SLOT, (j + 1) * SLOT)
                k_ref[rs, sl] = _rope_slot(y[:, sl], tab).astype(BF16)
        y = _dot_nt(hn, w_ref[OFF_CMP:OFF_CMP + W_CMP, :])
        for j in range(W_CMP // LANES):
            cmp_scr[j, rs, :] = y[:, j * LANES:(j + 1) * LANES]
        chunk_rows = rows // CHUNK_TOK
        for l in range(CHUNK_TOK):
            for j in range(W_CMP // LANES):
                c0 = l * W_CMP + j * LANES
                cmp_ref[k * chunk_rows:(k + 1) * chunk_rows, c0:c0 + LANES] = (
                    cmp_scr[j, pl.ds(k * rows + l, chunk_rows, stride=CHUNK_TOK), :].astype(BF16))
        gate_ref[rs, :] = _dot_nt(hn, w_ref[OFF_GATE:OFF_GATE + SLOT, :])
        pool_ref[rs, :] = _dot_nt(hn, w_ref[OFF_POOL:OFF_MERGE, :])
        tiles = rows // KT
        for vt_ref, off in ((vst_ref, OFF_VS), (vwt_ref, OFF_VW)):
            vt = _dot_nt(w_ref[off:off + KV_WIDTH, :], hn).astype(BF16)
            for j in range(tiles):
                for g in range(N_KV):
                    vt_ref[k * tiles + j, g * VT_ROWS:g * VT_ROWS + HEAD_DIM, :] = (
                        vt[g * HEAD_DIM:(g + 1) * HEAD_DIM, j * KT:(j + 1) * KT])
                    vt_ref[k * tiles + j, g * VT_ROWS + HEAD_DIM:(g + 1) * VT_ROWS, :] = aux

    for k in range(IN_SPLIT):
        part(k)


def _in_proj(x1, g_pre, w_in_t, tab):
    t = x1.shape[0]
    row = lambda i: (i, 0)
    fixed = lambda i: (0, 0)
    once = pl.Buffered(1)
    tile = IN_SPLIT * TOK_TILE
    tiles = tile // KT
    chunk_rows = tile // CHUNK_TOK
    slab = pl.BlockSpec((tiles, VT_SLAB, KT), lambda i: (i, 0, 0))
    slab_shape = jax.ShapeDtypeStruct((t // KT, VT_SLAB, KT), BF16)
    out_specs = [
        pl.BlockSpec((tile, Q_WIDTH), row), pl.BlockSpec((tile, KV_WIDTH), row),
        pl.BlockSpec((tile, KV_WIDTH), row), slab, slab,
        pl.BlockSpec((chunk_rows, CHUNK_TOK * W_CMP), row),
        pl.BlockSpec((tile, SLOT), row), pl.BlockSpec((tile, POOL_WIDTH), row)]
    out_shape = [
        jax.ShapeDtypeStruct((t, Q_WIDTH), BF16), jax.ShapeDtypeStruct((t, KV_WIDTH), BF16),
        jax.ShapeDtypeStruct((t, KV_WIDTH), BF16), slab_shape, slab_shape,
        jax.ShapeDtypeStruct((t // CHUNK_TOK, CHUNK_TOK * W_CMP), BF16),
        jax.ShapeDtypeStruct((t, SLOT), F32), jax.ShapeDtypeStruct((t, POOL_WIDTH), F32)]
    weights = _nbytes((OFF_MERGE, D_MODEL), F32)
    blocks = (_nbytes((tile, D_MODEL), F32) + _nbytes((tile, N_ROPE_TAB * SLOT), F32)
              + _nbytes((tile, Q_WIDTH + 2 * KV_WIDTH + W_CMP), BF16)
              + 2 * _nbytes((tiles, VT_SLAB, KT), BF16) + _nbytes((tile, SLOT + POOL_WIDTH), F32))
    res = pl.pallas_call(
        _in_proj_kernel,
        grid=(t // tile,),
        in_specs=[
            pl.BlockSpec((tile, D_MODEL), row),
            pl.BlockSpec((1, D_MODEL), fixed),
            pl.BlockSpec((pl.Element(OFF_MERGE), pl.Element(D_MODEL)), fixed, pipeline_mode=once),
            pl.BlockSpec((tile, N_ROPE_TAB * SLOT), row),
        ],
        out_specs=out_specs,
        out_shape=out_shape,
        scratch_shapes=[pltpu.VMEM((W_CMP // LANES, tile, LANES), F32)],
        compiler_params=pltpu.CompilerParams(
            dimension_semantics=("arbitrary",),
            vmem_limit_bytes=_vmem_limit(blocks, weights + _nbytes((tile, W_CMP), F32))),
        name="in_proj",
    )(x1, g_pre.reshape(1, -1), w_in_t, tab)
    return res


def _compress_kernel(x_ref, w1k_ref, w1v_ref, pek_ref, pev_ref, w1k_raw_ref, w1v_raw_ref,
                     w2k_ref, w2v_ref, tab_ref, kc_ref, vc_ref, acck_ref, accv_ref):
    l = pl.program_id(0)

    @pl.when(l == 0)
    def _():
        acck_ref[...] = jnp.zeros(acck_ref.shape, F32)
        accv_ref[...] = jnp.zeros(accv_ref.shape, F32)

    for acc_ref, w1_ref, c0 in ((acck_ref, w1k_ref, 0), (accv_ref, w1v_ref, KV_WIDTH)):
        for g in range(N_KV):
            acc_ref[g] += _dot(x_ref[:, c0 + g * HEAD_DIM:c0 + (g + 1) * HEAD_DIM], w1_ref[0])

    @pl.when(l == CHUNK_TOK - 1)
    def _():
        rows = acck_ref.shape[1]
        tab = tab_ref[...]
        for acc_ref, pe_ref, w1_raw_ref, w2_ref, o_ref, rope in (
                (acck_ref, pek_ref, w1k_raw_ref, w2k_ref, kc_ref, True),
                (accv_ref, pev_ref, w1v_raw_ref, w2v_ref, vc_ref, False)):
            bias = _dot(pe_ref[...], w1_raw_ref[...])[0:1, :]
            for g in range(N_KV):
                first = acc_ref[g, :, 0:CMP_HIDDEN]
                second = pltpu.roll(acc_ref[g, :, CMP_HIDDEN:2 * CMP_HIDDEN], rows - 1, axis=0)
                hid = jax.nn.gelu(first + second + bias).astype(BF16)
                out = _dot(hid, w2_ref[...])
                if rope:
                    out = _rope_slot(out, tab)
                o_ref[:, g * SLOT:(g + 1) * SLOT] = out.astype(BF16)


def _compress_weights(w1, w2):
    w1 = w1.reshape(2, CHUNK_TOK, HEAD_DIM, CMP_HIDDEN)
    w1_tok = jnp.concatenate([w1[0], w1[1]], axis=-1).astype(BF16)
    return w1_tok, jnp.pad(w2, ((0, 0), (0, SLOT - HEAD_DIM))).astype(BF16)


def _compress(cmp_rows, cmp_pe_k, cmp_w1_k, cmp_w2_k, cmp_pe_v, cmp_w1_v, cmp_w2_v, tab_cmp):
    rows = cmp_rows.shape[0]
    w1k, w2k = _compress_weights(cmp_w1_k, cmp_w2_k)
    w1v, w2v = _compress_weights(cmp_w1_v, cmp_w2_v)
    sub = 8
    pek = jnp.broadcast_to(cmp_pe_k.reshape(1, -1), (sub, CMP_LEN * HEAD_DIM)).astype(BF16)
    pev = jnp.broadcast_to(cmp_pe_v.reshape(1, -1), (sub, CMP_LEN * HEAD_DIM)).astype(BF16)
    fixed = lambda l: (0, 0)
    w_kv_pad = N_KV * SLOT
    blocks = (_nbytes((rows, W_CMP), BF16) + 2 * _nbytes((HEAD_DIM, 2 * CMP_HIDDEN), BF16)
              + 2 * _nbytes((sub + CMP_LEN * HEAD_DIM, CMP_HIDDEN), BF16)
              + 2 * _nbytes((CMP_HIDDEN, SLOT), BF16)
              + _nbytes((rows, N_ROPE_TAB * SLOT), F32) + 2 * _nbytes((rows, w_kv_pad), BF16))
    acc = pltpu.VMEM((N_KV, rows, 2 * CMP_HIDDEN), F32)
    return pl.pallas_call(
        _compress_kernel,
        grid=(CHUNK_TOK,),
        in_specs=[
            pl.BlockSpec((rows, W_CMP), lambda l: (0, l)),
            pl.BlockSpec((1, HEAD_DIM, 2 * CMP_HIDDEN), lambda l: (l, 0, 0)),
            pl.BlockSpec((1, HEAD_DIM, 2 * CMP_HIDDEN), lambda l: (l, 0, 0)),
            pl.BlockSpec((sub, CMP_LEN * HEAD_DIM), fixed),
            pl.BlockSpec((sub, CMP_LEN * HEAD_DIM), fixed),
            pl.BlockSpec((CMP_LEN * HEAD_DIM, CMP_HIDDEN), fixed),
            pl.BlockSpec((CMP_LEN * HEAD_DIM, CMP_HIDDEN), fixed),
            pl.BlockSpec((CMP_HIDDEN, SLOT), fixed),
            pl.BlockSpec((CMP_HIDDEN, SLOT), fixed),
            pl.BlockSpec((rows, N_ROPE_TAB * SLOT), fixed),
        ],
        out_specs=[pl.BlockSpec((rows, w_kv_pad), fixed)] * 2,
        out_shape=[jax.ShapeDtypeStruct((rows, w_kv_pad), BF16)] * 2,
        scratch_shapes=[acc, acc],
        compiler_params=pltpu.CompilerParams(
            dimension_semantics=("arbitrary",),
            vmem_limit_bytes=_vmem_limit(blocks, 2 * _nbytes(acc.shape, F32))),
        name="compress",
    )(cmp_rows, w1k, w1v, pek, pev, cmp_w1_k.astype(BF16), cmp_w1_v.astype(BF16), w2k, w2v, tab_cmp)


def _nsa_kernel(*refs):
    def tile(j, carry):
        rows = pl.ds(pl.multiple_of(j * TQ, TQ), TQ)
        _nsa_tile(pl.program_id(1) * NSA_STEP_TILES + j, rows, *refs)
        return carry

    lax.fori_loop(0, NSA_STEP_TILES, tile, 0)


def _nsa_tile(i, rows, q_ref, gate_ref, kc_ref, vc_ref, ks_ref, vst_ref, kw_ref, vwt_ref, ov_ref,
              o_ref, ksp_ref, kwp_ref, qg_ref, mix_ref, *state_refs):
    seq = ks_ref.shape[1]
    cols_g = GROUP * TQ
    chains_g = cols_g // CHAIN_COLS
    n_chains = N_KV * chains_g
    n_cmp = kc_ref.shape[1]
    sel_state = (state_refs[0:n_chains], state_refs[n_chains:2 * n_chains])
    win_state = (state_refs[2 * n_chains:3 * n_chains], state_refs[3 * n_chains:4 * n_chains])
    lane_q = lax.broadcasted_iota(jnp.int32, (TQ, SLOT), 1)
    lane_h = lax.broadcasted_iota(jnp.int32, (HALF, SLOT), 1)

    def head_slot(x_ref, rows, j):
        two = x_ref[rows, (j // 2) * SLOT:(j // 2 + 1) * SLOT]
        if j % 2:
            words = pltpu.roll(pltpu.bitcast(two, jnp.uint32), HEAD_DIM, axis=1)
            two = pltpu.bitcast(words, BF16)
        return two

    @pl.when(i == 0)
    def _():
        for c in range(seq // KT):
            rows = pl.ds(c * KT, KT)
            lane = lax.broadcasted_iota(jnp.int32, (KT, SLOT), 1)
            key_blk = (c * KT + lax.broadcasted_iota(jnp.int32, (KT, SLOT), 0)) >> SEL_SHIFT
            onehot = jnp.where(lane - BLK_LANE0 == key_blk, 1.0, 0.0).astype(BF16)
            zeros = jnp.zeros((KT, SLOT), BF16)
            for g in range(N_KV):
                gs = slice(g * SLOT, (g + 1) * SLOT)
                ksp_ref[rows, gs] = jnp.where(lane < HEAD_DIM, head_slot(ks_ref.at[0], rows, g), onehot)
                kwp_ref[rows, gs] = jnp.where(lane < HEAD_DIM, head_slot(kw_ref.at[0], rows, g), zeros)

    def q_rows(r, x):
        return slice((x * GROUP + r) * HALF, (x * GROUP + r + 1) * HALF)

    def chain_half(c):
        return c * CHAIN_COLS // (GROUP * HALF)

    for h in range(N_HEADS):
        g, r = divmod(h, GROUP)
        q_pad = jnp.where(lane_q < HEAD_DIM, head_slot(q_ref, rows, h), jnp.zeros((TQ, SLOT), BF16))
        for x in range(2):
            qg_ref[g, q_rows(r, x), :] = q_pad[x * HALF:(x + 1) * HALF]

    gates_t = jax.nn.sigmoid(gate_ref[rows, :]).T
    key_h = lax.broadcasted_iota(jnp.int32, (HALF, CHAIN_COLS), 0)
    qry_h = lax.broadcasted_iota(jnp.int32, (HALF, CHAIN_COLS), 1) & (HALF - 1)
    tri_le, tri_gt = key_h <= qry_h, key_h > qry_h
    diag_plan = ((slice(0, HALF), tri_le, slice(0, HALF)), (slice(0, KT), tri_le, slice(HALF, KT)))
    far_plan = ((slice(0, KT), tri_gt, slice(0, HALF)), (slice(HALF, KT), tri_gt, slice(0, HALF)))

    def gate_row(branch, h):
        c = branch * N_HEADS + h
        return gates_t[c:c + 1, :]

    sub = 8
    cmp_per_sub = sub * SEL_LEN // CMP_STRIDE

    def compressed_scores(ng):
        n_c = ng * cmp_per_sub
        cmp_scores = []
        for h in range(N_HEADS):
            g, r = divmod(h, GROUP)
            q_h = jnp.concatenate([qg_ref[g, q_rows(r, 0), :], qg_ref[g, q_rows(r, 1), :]], axis=0)
            cmp_scores.append(_dot_nt(kc_ref[0, 0:n_c, g * SLOT:(g + 1) * SLOT], q_h))
        return cmp_scores

    def compressed_and_select(ng, start, cmp_scores):
        n_blk, n_c = ng * sub, ng * cmp_per_sub
        t_cmp = start + lax.broadcasted_iota(jnp.int32, (n_c, TQ), 1)
        n_idx = lax.broadcasted_iota(jnp.int32, (n_c, TQ), 0)
        cmp_valid = n_idx * CMP_STRIDE + (CMP_LEN - 1) <= t_cmp
        any_cmp = start + lax.broadcasted_iota(jnp.int32, (1, TQ), 1) >= CMP_LEN - 1
        t_row = start + lax.broadcasted_iota(jnp.int32, (n_blk, TQ), 1)
        blk = lax.broadcasted_iota(jnp.int32, (n_blk, TQ), 0)
        forced = (blk == t_row >> SEL_SHIFT) | (blk == 0)
        causal_blk = blk * SEL_LEN <= t_row
        idx8 = lax.broadcasted_iota(jnp.int32, (sub, TQ), 0)
        pad_c = jnp.zeros((n_cmp - n_c, TQ), F32)
        for g in range(N_KV):
            vct_g = vc_ref[0, :, g * SLOT:(g + 1) * SLOT].astype(F32).T[0:HEAD_DIM].astype(BF16)
            p_sum = jnp.zeros((n_c, TQ), F32)
            for r in range(GROUP):
                h = g * GROUP + r
                s = jnp.where(cmp_valid, cmp_scores[h], NEG_INF)
                p = jnp.exp2(s - jnp.max(s, axis=0, keepdims=True))
                inv = jnp.where(any_cmp, 1.0 / jnp.sum(p, axis=0, keepdims=True), 0.0)
                p = p * inv
                p_sum = p_sum + p
                p_all = jnp.concatenate([p, pad_c], axis=0) if ng * cmp_per_sub < n_cmp else p
                mix_ref[h] = gate_row(0, h) * _dot(vct_g, p_all.astype(BF16))
            p_sum_all = jnp.concatenate([p_sum, pad_c], axis=0) if ng * cmp_per_sub < n_cmp else p_sum
            imp = jnp.dot(ov_ref[0:n_blk, :], p_sum_all, precision=lax.Precision.HIGHEST,
                          preferred_element_type=F32)
            score = jnp.where(causal_blk, jnp.where(forced, FORCE_SCORE, imp), NEG_INF)
            rows8 = [score[k * sub:(k + 1) * sub] for k in range(ng)]
            ranks = [jnp.zeros((sub, TQ), jnp.int32) for _ in rows8]
            for c in range(n_blk):
                other = score[c:c + 1, :]
                for k, mine in enumerate(rows8):
                    if k * sub > c:
                        beats = other >= mine
                    elif (k + 1) * sub - 1 <= c:
                        beats = other > mine
                    else:
                        beats = (other > mine) | ((other == mine) & (idx8 > c - k * sub))
                    ranks[k] = ranks[k] + beats.astype(jnp.int32)
            rank = jnp.concatenate(ranks, axis=0) if ng > 1 else ranks[0]
            bias = jnp.where(causal_blk & (rank < N_SEL), 0.0, NEG_INF)
            bias_t = jnp.concatenate([jnp.zeros((BLK_LANE0, TQ), F32), bias,
                                      jnp.zeros((SLOT - BLK_LANE0 - n_blk, TQ), F32)], axis=0)
            bias_q = bias_t.T.astype(BF16)
            for r in range(GROUP):
                for x in range(2):
                    rs = q_rows(r, x)
                    qg_ref[g, rs, :] = jnp.where(lane_h < HEAD_DIM, qg_ref[g, rs, :],
                                                 bias_q[x * HALF:(x + 1) * HALF])

    ng_now = ((i + 1) * TQ + sub * SEL_LEN - 1) // (sub * SEL_LEN)
    for ng in range(1, seq // (sub * SEL_LEN) + 1):

        @pl.when(ng_now == ng)
        def _(ng=ng):
            compressed_and_select(ng, i * TQ, compressed_scores(ng))

    def flash_init(state):
        for m_ref, acc_ref in zip(*state):
            m_ref[...] = jnp.full(m_ref.shape, NEG_INF, F32)
            acc_ref[...] = jnp.zeros(acc_ref.shape, F32)

    def flash_tiles(tiles):
        jobs = [(tile, c) for tile in tiles for c in range(n_chains)]

        def key_plan(job):
            (_, _, _, plan, _), chain = job
            return (slice(0, KT), None, None) if plan is None else plan[chain_half(chain % chains_g)]

        scores = {}
        for step in range(len(jobs) + PIPE_DEPTH):
            if step < len(jobs):
                (kp_ref, _, kt, _, _), chain = jobs[step]
                g, c = divmod(chain, chains_g)
                keys, mask, rows = key_plan(jobs[step])
                off = pl.multiple_of(kt * KT + keys.start, HALF)
                s = _dot_nt(kp_ref[pl.ds(off, keys.stop - keys.start), g * SLOT:(g + 1) * SLOT],
                            qg_ref[g, c * CHAIN_COLS:(c + 1) * CHAIN_COLS, :])
                if mask is not None:
                    parts = [s[0:rows.start], jnp.where(mask, s[rows], NEG_INF), s[rows.stop:]]
                    s = jnp.concatenate([p for p in parts if p.shape[0]], axis=0)
                scores[step] = s.astype(BF16)
            done = step - PIPE_DEPTH
            if done >= 0:
                (_, vt_ref, kt, _, (m_refs, acc_refs)), chain = jobs[done]
                g = chain // chains_g
                keys = key_plan(jobs[done])[0]
                m_ref, acc_ref = m_refs[chain], acc_refs[chain]
                s = scores.pop(done)
                m_old = m_ref[...]
                packed = [s[r * BF16_SUBLANES:(r + 1) * BF16_SUBLANES]
                          for r in range(s.shape[0] // BF16_SUBLANES)]
                m_tile = functools.reduce(jnp.maximum, packed).astype(F32)
                m_new = jnp.maximum(m_old, jnp.max(m_tile, axis=0, keepdims=True))
                p = jnp.exp2(s - m_new.astype(BF16))
                pv = _dot(vt_ref[kt, g * VT_ROWS:(g + 1) * VT_ROWS, keys], p)
                acc_ref[...] = jnp.exp2(m_old - m_new) * acc_ref[...] + pv
                m_ref[...] = m_new

    def flash_mix(branch, state):
        _, acc_refs = state
        for h in range(N_HEADS):
            g, r = divmod(h, GROUP)
            gate = gate_row(branch, h)
            for x in range(2):
                c, c0 = divmod(q_rows(r, x).start, CHAIN_COLS)
                acc = acc_refs[g * chains_g + c][:, c0:c0 + HALF]
                qs = slice(x * HALF, (x + 1) * HALF)
                scale = gate[:, qs] * (1.0 / acc[ONE_ROW:ONE_ROW + 1, :])
                mix_ref[h, :, qs] = mix_ref[h, :, qs] + scale * acc[0:HEAD_DIM]

    flash_init(sel_state)
    flash_init(win_state)

    def sel_past(kt):
        return (ksp_ref, vst_ref, kt, None, sel_state)

    def sel_pair(j, carry):
        flash_tiles([sel_past(2 * j), sel_past(2 * j + 1)])
        return carry

    lax.fori_loop(0, i >> 1, sel_pair, 0)

    @pl.when(i & 1 == 1)
    def _():
        flash_tiles([sel_past(i - 1)])

    n_back = WINDOW // KT
    sel_diag = (ksp_ref, vst_ref, i, diag_plan, sel_state)
    win_diag = (kwp_ref, vwt_ref, i, diag_plan, win_state)

    def win_back(d):
        return (kwp_ref, vwt_ref, i - d, far_plan if d == n_back else None, win_state)

    for have in range(n_back + 1):
        cond = (i == have) if have < n_back else (i >= have)

        @pl.when(cond)
        def _(have=have):
            flash_tiles([sel_diag] + [win_back(d) for d in range(have, 0, -1)] + [win_diag])

    flash_mix(1, sel_state)
    flash_mix(2, win_state)
    for h2 in range(N_HEADS // 2):
        pair = jnp.concatenate([mix_ref[2 * h2], mix_ref[2 * h2 + 1]], axis=0)
        o_ref[rows, h2 * SLOT:(h2 + 1) * SLOT] = pair.T.astype(BF16)


def _overlap(n_cmp_pad, n_slc):
    c0 = jnp.arange(n_cmp_pad) * CMP_STRIDE
    s0 = jnp.arange(n_slc) * SEL_LEN
    ov = jnp.minimum(c0[None, :] + CMP_LEN, s0[:, None] + SEL_LEN) - jnp.maximum(c0[None, :], s0[:, None])
    return jnp.clip(ov, 0).astype(F32) / CMP_LEN


def _nsa(q, gate, kc, vc, ks, vst, kw, vwt, batch, seq_len):
    n_slc = seq_len // SEL_LEN
    n_cmp_pad = kc.shape[1]
    w_kv_pad = N_KV * SLOT
    assert n_slc <= SLOT - BLK_LANE0 and n_slc % 8 == 0
    assert CHAIN_COLS % HALF == 0 and (GROUP * HALF) % CHAIN_COLS == 0 and KT == 2 * HALF
    ov = _overlap(n_cmp_pad, n_slc)
    step_rows = NSA_STEP_TILES * TQ
    steps = seq_len // step_rows
    n_kt = seq_len // KT
    qrow = lambda b, i: (b * steps + i, 0)
    per_b = lambda b, i: (b, 0, 0)
    fixed = lambda b, i: (0, 0)
    blocks = (_nbytes((step_rows, Q_WIDTH), BF16) + _nbytes((step_rows, SLOT), F32)
              + 2 * _nbytes((n_cmp_pad, w_kv_pad), BF16) + 2 * _nbytes((seq_len, KV_WIDTH), BF16)
              + 2 * _nbytes((n_kt, VT_SLAB, KT), BF16)
              + _nbytes(ov.shape, F32) + _nbytes((step_rows, Q_WIDTH), BF16))
    n_chains = N_KV * GROUP * TQ // CHAIN_COLS
    run_max = [pltpu.VMEM((1, CHAIN_COLS), F32)] * n_chains
    run_acc = [pltpu.VMEM((VT_ROWS, CHAIN_COLS), F32)] * n_chains
    scratch_shapes = (
        [pltpu.VMEM((seq_len, w_kv_pad), BF16),
         pltpu.VMEM((seq_len, w_kv_pad), BF16),
         pltpu.VMEM((N_KV, GROUP * TQ, SLOT), BF16),
         pltpu.VMEM((N_HEADS, HEAD_DIM, TQ), F32)]
        + run_max + run_acc + run_max + run_acc)
    scratch = sum(_nbytes(s.shape, s.dtype) for s in scratch_shapes)
    return pl.pallas_call(
        _nsa_kernel,
        grid=(batch, steps),
        in_specs=[
            pl.BlockSpec((step_rows, Q_WIDTH), qrow),
            pl.BlockSpec((step_rows, SLOT), qrow),
            pl.BlockSpec((1, n_cmp_pad, w_kv_pad), per_b),
            pl.BlockSpec((1, n_cmp_pad, w_kv_pad), per_b),
            pl.BlockSpec((1, seq_len, KV_WIDTH), per_b),
            pl.BlockSpec((n_kt, VT_SLAB, KT), per_b),
            pl.BlockSpec((1, seq_len, KV_WIDTH), per_b),
            pl.BlockSpec((n_kt, VT_SLAB, KT), per_b),
            pl.BlockSpec(ov.shape, fixed),
        ],
        out_specs=pl.BlockSpec((step_rows, Q_WIDTH), qrow),
        out_shape=jax.ShapeDtypeStruct((batch * seq_len, Q_WIDTH), BF16),
        scratch_shapes=scratch_shapes,
        compiler_params=pltpu.CompilerParams(
            dimension_semantics=("arbitrary", "arbitrary"),
            vmem_limit_bytes=_vmem_limit(blocks, scratch)),
        name="nsa",
    )(q, gate, kc.reshape(batch, n_cmp_pad, w_kv_pad), vc.reshape(batch, n_cmp_pad, w_kv_pad),
      ks.reshape(batch, seq_len, KV_WIDTH), vst, kw.reshape(batch, seq_len, KV_WIDTH), vwt, ov)


def _pool_kernel(u_ref, w_ref, scale_ref, o_ref):
    seq = u_ref.shape[0]
    t = lax.broadcasted_iota(jnp.int32, (seq, POOL_GROUP), 0)

    def shifted(x, k):
        return jnp.where(t >= k, pltpu.roll(x, k, axis=0), 0.0)

    for gi, w in enumerate(POOL_WINDOWS):
        sl = slice(gi * POOL_GROUP, (gi + 1) * POOL_GROUP)
        x = u_ref[:, sl]
        wsum = x
        span = 1
        while span < w:
            wsum = wsum + shifted(wsum, span)
            span *= 2
        cnt = jnp.minimum(t + 1, w).astype(F32)
        pooled = (wsum / cnt - x).astype(BF16)
        o_ref[:, sl] = (_dot(pooled, w_ref[gi]) * scale_ref[:, sl]).astype(BF16)


def _pool(pool_in, pool_w, pool_scale, batch, seq_len):
    for w in POOL_WINDOWS:
        assert w & (w - 1) == 0
    blocks = (_nbytes((seq_len, POOL_WIDTH), F32) + _nbytes(pool_w.shape, BF16)
              + _nbytes((seq_len, POOL_WIDTH), BF16))
    return pl.pallas_call(
        _pool_kernel,
        grid=(batch,),
        in_specs=[
            pl.BlockSpec((seq_len, POOL_WIDTH), lambda b: (b, 0)),
            pl.BlockSpec(pool_w.shape, lambda b: (0, 0, 0)),
            pl.BlockSpec((1, POOL_WIDTH), lambda b: (0, 0)),
        ],
        out_specs=pl.BlockSpec((seq_len, POOL_WIDTH), lambda b: (b, 0)),
        out_shape=jax.ShapeDtypeStruct((batch * seq_len, POOL_WIDTH), BF16),
        compiler_params=pltpu.CompilerParams(
            dimension_semantics=("arbitrary",), vmem_limit_bytes=_vmem_limit(blocks)),
        name="pool",
    )(pool_in, pool_w.astype(BF16), pool_scale.reshape(1, -1))


def _merge_kernel(x_ref, gpre_ref, wt_ref, nsa_ref, wa_ref, pool_ref, wp_ref, wo_ref, gpost_ref,
                  o_ref):
    rows = x_ref.shape[0] // MERGE_SPLIT
    for k in range(MERGE_SPLIT):
        rs = slice(k * rows, (k + 1) * rows)
        x = x_ref[rs, :]
        hn = _rms(x, gpre_ref[...])
        g_attn = jax.nn.sigmoid(_dot_nt(hn, wt_ref[0:D_MODEL, :]))
        g_pool = jax.nn.sigmoid(_dot_nt(hn, wt_ref[D_MODEL:2 * D_MODEL, :]))
        y = (g_attn * _dot(nsa_ref[rs, :].astype(F32), wa_ref[...])
             + g_pool * _dot(pool_ref[rs, :].astype(F32), wp_ref[...]))
        h = _dot(y, wo_ref[...])
        o_ref[rs, :] = x + _rms(h, gpost_ref[...])


def _merge(x1, g_pre, w_in_t, o_nsa, w_attn, o_pool, w_pool, w_out, g_post):
    t = x1.shape[0]
    row = lambda i: (i, 0)
    fixed = lambda i: (0, 0)
    once = pl.Buffered(1)
    tile = MERGE_SPLIT * TOK_TILE
    weights = (_nbytes((2 * D_MODEL, D_MODEL), F32) + _nbytes((Q_WIDTH, D_MODEL), F32)
               + _nbytes((POOL_WIDTH, D_MODEL), F32) + _nbytes((D_MODEL, D_MODEL), F32))
    blocks = (2 * _nbytes((tile, D_MODEL), F32) + _nbytes((tile, Q_WIDTH), BF16)
              + _nbytes((tile, POOL_WIDTH), BF16))
    gate_rows = pl.BlockSpec((pl.Element(2 * D_MODEL), pl.Element(D_MODEL)), lambda i: (OFF_MERGE, 0),
                             pipeline_mode=once)
    return pl.pallas_call(
        _merge_kernel,
        grid=(t // tile,),
        in_specs=[
            pl.BlockSpec((tile, D_MODEL), row),
            pl.BlockSpec((1, D_MODEL), fixed),
            gate_rows,
            pl.BlockSpec((tile, Q_WIDTH), row),
            pl.BlockSpec((Q_WIDTH, D_MODEL), fixed, pipeline_mode=once),
            pl.BlockSpec((tile, POOL_WIDTH), row),
            pl.BlockSpec((POOL_WIDTH, D_MODEL), fixed, pipeline_mode=once),
            pl.BlockSpec((D_MODEL, D_MODEL), fixed, pipeline_mode=once),
            pl.BlockSpec((1, D_MODEL), fixed),
        ],
        out_specs=pl.BlockSpec((tile, D_MODEL), row),
        out_shape=jax.ShapeDtypeStruct((t, D_MODEL), F32),
        compiler_params=pltpu.CompilerParams(
            dimension_semantics=("arbitrary",), vmem_limit_bytes=_vmem_limit(blocks, weights)),
        name="merge",
    )(x1, g_pre.reshape(1, -1), w_in_t, o_nsa, w_attn, o_pool, w_pool, w_out, g_post.reshape(1, -1))


def _layer(x, tab, tab_cmp, batch, seq_len, g_ffn1_pre, w_ffn1_gate, w_ffn1_up, w_ffn1_down,
           g_ffn1_post, g_mix_pre, w_in, cmp_pe_k, cmp_w1_k, cmp_w2_k, cmp_pe_v, cmp_w1_v, cmp_w2_v,
           w_attn_branch, pool_w, pool_scale, w_pool_branch, w_out, g_mix_post, g_ffn2_pre,
           w_ffn2_gate, w_ffn2_up, w_ffn2_down, g_ffn2_post):
    x1 = _ffn(x, g_ffn1_pre, w_ffn1_gate, w_ffn1_up, w_ffn1_down, g_ffn1_post)
    w_in_t = w_in.T
    q, ks, kw, vst, vwt, cmp_rows, gate, pool_in = _in_proj(x1, g_mix_pre, w_in_t, tab)
    kc, vc = _compress(cmp_rows, cmp_pe_k, cmp_w1_k, cmp_w2_k, cmp_pe_v, cmp_w1_v, cmp_w2_v, tab_cmp)
    n_cmp_pad = seq_len // CMP_STRIDE
    kc = kc.reshape(batch, n_cmp_pad, N_KV * SLOT)
    vc = vc.reshape(batch, n_cmp_pad, N_KV * SLOT)
    o_nsa = _nsa(q, gate, kc, vc, ks, vst, kw, vwt, batch, seq_len)
    o_pool = _pool(pool_in, pool_w, pool_scale, batch, seq_len)
    x2 = _merge(x1, g_mix_pre, w_in_t, o_nsa, w_attn_branch, o_pool, w_pool_branch, w_out,
                g_mix_post)
    return _ffn(x2, g_ffn2_pre, w_ffn2_gate, w_ffn2_up, w_ffn2_down, g_ffn2_post)


def kernel(x, positions, g_ffn1_pre, w_ffn1_gate, w_ffn1_up, w_ffn1_down, g_ffn1_post, g_mix_pre, w_in, cmp_pe_k, cmp_w1_k, cmp_w2_k, cmp_pe_v, cmp_w1_v, cmp_w2_v, w_attn_branch, pool_w, pool_scale, w_pool_branch, w_out, g_mix_post, g_ffn2_pre, w_ffn2_gate, w_ffn2_up, w_ffn2_down, g_ffn2_post):
    batch, seq_len, d_model = x.shape
    assert d_model == D_MODEL and seq_len % TOK_TILE == 0 and seq_len % (NSA_STEP_TILES * TQ) == 0
    assert TQ == KT and TQ % SEL_LEN == 0 and TOK_TILE % KT == 0 and WINDOW % KT == 0
    assert seq_len % CMP_STRIDE == 0 and CMP_LEN == 2 * CMP_STRIDE
    t = batch * seq_len
    tab = _rope_tables(positions.reshape(t), seq_len)
    n_cmp = (seq_len - CMP_LEN) // CMP_STRIDE + 1
    n_cmp_pad = seq_len // CMP_STRIDE
    pos_cmp = jnp.pad(positions[:, CMP_LEN - 1::CMP_STRIDE], ((0, 0), (0, n_cmp_pad - n_cmp)))
    tab_cmp = _rope_tables(pos_cmp.reshape(batch * n_cmp_pad), batch * n_cmp_pad)
    xf = x.reshape(t, D_MODEL)
    per_layer = (g_ffn1_pre, w_ffn1_gate, w_ffn1_up, w_ffn1_down, g_ffn1_post, g_mix_pre, w_in,
                 cmp_pe_k, cmp_w1_k, cmp_w2_k, cmp_pe_v, cmp_w1_v, cmp_w2_v, w_attn_branch, pool_w,
                 pool_scale, w_pool_branch, w_out, g_mix_post, g_ffn2_pre, w_ffn2_gate, w_ffn2_up,
                 w_ffn2_down, g_ffn2_post)
    for l in range(g_ffn1_pre.shape[0]):
        xf = _layer(xf, tab, tab_cmp, batch, seq_len, *(p[l] for p in per_layer))
    return xf.reshape(batch, seq_len, D_MODEL)
```

```python
import functools
import math

import jax
import jax.numpy as jnp
from jax import lax
from jax.experimental import pallas as pl
from jax.experimental.pallas import tpu as pltpu

F32 = jnp.float32
BF16 = jnp.bfloat16

D_MODEL = 1024
N_HEADS = 16
HEAD_DIM = 64
N_KV = 4
GROUP = N_HEADS // N_KV
ROT_DIM = HEAD_DIM // 4
ROT_HALF = ROT_DIM // 2
ROPE_THETA = 500000.0
CMP_LEN = 32
CMP_STRIDE = 16
CMP_HIDDEN = 2 * HEAD_DIM
SEL_LEN = 64
SEL_SHIFT = SEL_LEN.bit_length() - 1
N_SEL = 8
WINDOW = 512
POOL_WINDOWS = (2, 4, 8, 16)
POOL_WIDTH = D_MODEL // 2
POOL_GROUP = POOL_WIDTH // len(POOL_WINDOWS)
D_FF = 2816
EPS = 1e-6
NEG_INF = -1e30
FORCE_SCORE = 1e4
Q_WIDTH = N_HEADS * HEAD_DIM
KV_WIDTH = N_KV * HEAD_DIM
N_GATES = 3 * N_HEADS
LOG2_E = math.log2(math.e)

LANES = 128
BF16_SUBLANES = 16
V7X_VMEM_BYTES = 64 * 1024 * 1024
VMEM_COMPILER_RESERVE = 4 * 1024 * 1024
VMEM_MIN_REQUEST = 32 * 1024 * 1024

SLOT = LANES
BLK_LANE0 = HEAD_DIM
VT_ROWS = HEAD_DIM + BF16_SUBLANES
ONE_ROW = HEAD_DIM
TOK_TILE = 512
TQ = 256
KT = 256
HALF = TQ // 2
CHAIN_COLS = 256
PIPE_DEPTH = 10
FF_CHUNK = 256
IN_SPLIT = 2
MERGE_SPLIT = 2
N_ROPE_TAB = 2


def _vmem_limit(block_bytes, scratch_bytes=0):
    need = 2 * block_bytes + scratch_bytes
    return int(min(V7X_VMEM_BYTES - VMEM_COMPILER_RESERVE, max(2 * need, VMEM_MIN_REQUEST)))


def _nbytes(shape, dtype):
    n = 1
    for s in shape:
        n *= s
    return n * jnp.dtype(dtype).itemsize


def _rms(xf, g):
    return xf * lax.rsqrt(jnp.mean(xf * xf, axis=-1, keepdims=True) + EPS) * g


def _dot(a, b):
    return jnp.dot(a, b, preferred_element_type=F32)


def _dot_nt(a, b):
    return lax.dot_general(a, b, (((1,), (1,)), ((), ())), preferred_element_type=F32)


def _rope_trig_kernel(pos_ref, inv_ref, tab_ref):
    tn = pos_ref.shape[1]
    ang = pos_ref[...].astype(F32) * inv_ref[...]
    c, s = jnp.cos(ang), jnp.sin(ang)
    rest = HEAD_DIM - ROT_DIM
    reps = SLOT // HEAD_DIM
    cos_rows = jnp.concatenate([c, c, jnp.ones((rest, tn), F32)] * reps, axis=0)
    sin_rows = jnp.concatenate([-s, s, jnp.zeros((rest, tn), F32)] * reps, axis=0)
    for j in range(tn // LANES):
        cs = slice(j * LANES, (j + 1) * LANES)
        tab_ref[cs, 0:SLOT] = cos_rows[:, cs].T
        tab_ref[cs, SLOT:2 * SLOT] = sin_rows[:, cs].T


def _rope_tables(pos_flat, tile):
    n = pos_flat.shape[0]
    inv = ROPE_THETA ** (-jnp.arange(ROT_HALF, dtype=F32) * (2.0 / ROT_DIM))
    return pl.pallas_call(
        _rope_trig_kernel,
        grid=(n // tile,),
        in_specs=[pl.BlockSpec((1, tile), lambda i: (0, i)),
                  pl.BlockSpec((ROT_HALF, 1), lambda i: (0, 0))],
        out_specs=pl.BlockSpec((tile, N_ROPE_TAB * SLOT), lambda i: (i, 0)),
        out_shape=jax.ShapeDtypeStruct((n, N_ROPE_TAB * SLOT), F32),
        compiler_params=pltpu.CompilerParams(dimension_semantics=("arbitrary",)),
        name="rope_trig",
    )(pos_flat.reshape(1, n), inv.reshape(ROT_HALF, 1))


def _rope_slot(y, tab):
    cos_t = tab[:, 0:SLOT]
    sin_t = tab[:, SLOT:2 * SLOT]
    lane = lax.broadcasted_iota(jnp.int32, y.shape, 1) & (HEAD_DIM - 1)
    up = pltpu.roll(y, SLOT - ROT_HALF, axis=1)
    down = pltpu.roll(y, ROT_HALF, axis=1)
    return y * cos_t + jnp.where(lane < ROT_HALF, up, down) * sin_t


def _ffn_kernel(x_ref, gpre_ref, wg_hbm, wu_hbm, wd_hbm, gpost_ref, o_ref, wg_ref, wu_ref, wd_ref, sem):
    n_chunks = D_FF // FF_CHUNK

    def chunk_copies(c):
        sl = slice(c * FF_CHUNK, (c + 1) * FF_CHUNK)
        return (pltpu.make_async_copy(wg_hbm.at[:, sl], wg_ref.at[:, sl], sem.at[0, c]),
                pltpu.make_async_copy(wu_hbm.at[:, sl], wu_ref.at[:, sl], sem.at[1, c]),
                pltpu.make_async_copy(wd_hbm.at[sl, :], wd_ref.at[sl, :], sem.at[2, c]))

    def body(first_step):
        x = x_ref[...]
        hn = _rms(x, gpre_ref[...])
        for c in range(n_chunks):
            sl = slice(c * FF_CHUNK, (c + 1) * FF_CHUNK)
            if first_step:
                for copy in chunk_copies(c):
                    copy.wait()
            g = _dot(hn, wg_ref[:, sl])
            u = _dot(hn, wu_ref[:, sl])
            d = _dot(g * jax.nn.sigmoid(g) * u, wd_ref[sl, :])
            if c == 0:
                o_ref[...] = d
            else:
                o_ref[...] += d
        o_ref[...] = x + 0.5 * _rms(o_ref[...], gpost_ref[...])

    @pl.when(pl.program_id(0) == 0)
    def _():
        for c in range(n_chunks):
            for copy in chunk_copies(c):
                copy.start()
        body(True)

    @pl.when(pl.program_id(0) > 0)
    def _():
        body(False)


def _ffn(x, g_pre, w_gate, w_up, w_down, g_post):
    t = x.shape[0]
    row = lambda i: (i, 0)
    fixed = lambda i: (0, 0)
    in_hbm = pl.BlockSpec(memory_space=pl.ANY)
    weights = 3 * _nbytes((D_MODEL, D_FF), F32)
    blocks = 2 * _nbytes((TOK_TILE, D_MODEL), F32)
    return pl.pallas_call(
        _ffn_kernel,
        grid=(t // TOK_TILE,),
        in_specs=[
            pl.BlockSpec((TOK_TILE, D_MODEL), row),
            pl.BlockSpec((1, D_MODEL), fixed),
            in_hbm, in_hbm, in_hbm,
            pl.BlockSpec((1, D_MODEL), fixed),
        ],
        out_specs=pl.BlockSpec((TOK_TILE, D_MODEL), row),
        out_shape=jax.ShapeDtypeStruct((t, D_MODEL), F32),
        scratch_shapes=[pltpu.VMEM((D_MODEL, D_FF), F32), pltpu.VMEM((D_MODEL, D_FF), F32),
                        pltpu.VMEM((D_FF, D_MODEL), F32),
                        pltpu.SemaphoreType.DMA((3, D_FF // FF_CHUNK))],
        compiler_params=pltpu.CompilerParams(
            dimension_semantics=("arbitrary",), vmem_limit_bytes=_vmem_limit(blocks, weights)),
        name="ffn",
    )(x, g_pre.reshape(1, -1), w_gate, w_up, w_down, g_post.reshape(1, -1))


OFF_Q = 0
OFF_CMP = OFF_Q + Q_WIDTH
OFF_KS = OFF_CMP + 2 * KV_WIDTH
OFF_VS = OFF_KS + KV_WIDTH
OFF_KW = OFF_VS + KV_WIDTH
OFF_VW = OFF_KW + KV_WIDTH
OFF_GATE = OFF_VW + KV_WIDTH
OFF_POOL = OFF_GATE + N_GATES
OFF_MERGE = OFF_POOL + POOL_WIDTH
W_CMP = 2 * KV_WIDTH
CHUNK_TOK = CMP_STRIDE
VT_SLAB = N_KV * VT_ROWS


def _in_proj_kernel(x_ref, g_ref, w_ref, tab_ref, q_ref, ks_ref, kw_ref, vst_ref,
                    vwt_ref, cmp_ref, gate_ref, pool_ref, cmp_scr):
    rows = x_ref.shape[0] // IN_SPLIT
    wide = 4 * SLOT
    aux = (lax.broadcasted_iota(jnp.int32, (VT_ROWS - HEAD_DIM, KT), 0) == 0).astype(BF16)

    def part(k):
        rs = slice(k * rows, (k + 1) * rows)
        hn = _rms(x_ref[rs, :], g_ref[...])
        tab = tab_ref[rs, :]
        tab_q = tab * (HEAD_DIM ** -0.5 * LOG2_E)
        for c in range(Q_WIDTH // wide):
            y = _dot_nt(hn, w_ref[OFF_Q + c * wide:OFF_Q + (c + 1) * wide, :])
            for j in range(wide // SLOT):
                sl = slice(c * wide + j * SLOT, c * wide + (j + 1) * SLOT)
                q_ref[rs, sl] = _rope_slot(y[:, j * SLOT:(j + 1) * SLOT], tab_q).astype(BF16)
        for k_ref, off in ((ks_ref, OFF_KS), (kw_ref, OFF_KW)):
            y = _dot_nt(hn, w_ref[off:off + KV_WIDTH, :])
            for j in range(KV_WIDTH // SLOT):
                sl = slice(j * SLOT, (j + 1) * SLOT)
                k_ref[rs, sl] = _rope_slot(y[:, sl], tab).astype(BF16)
        y = _dot_nt(hn, w_ref[OFF_CMP:OFF_CMP + W_CMP, :])
        for j in range(W_CMP // LANES):
            cmp_scr[j, rs, :] = y[:, j * LANES:(j + 1) * LANES]
        chunk_rows = rows // CHUNK_TOK
        for l in range(CHUNK_TOK):
            for j in range(W_CMP // LANES):
                c0 = l * W_CMP + j * LANES
                cmp_ref[k * chunk_rows:(k + 1) * chunk_rows, c0:c0 + LANES] = (
                    cmp_scr[j, pl.ds(k * rows + l, chunk_rows, stride=CHUNK_TOK), :].astype(BF16))
        gate_ref[rs, :] = _dot_nt(hn, w_ref[OFF_GATE:OFF_GATE + SLOT, :])
        pool_ref[rs, :] = _dot_nt(hn, w_ref[OFF_POOL:OFF_MERGE, :])
        tiles = rows // KT
        for vt_ref, off in ((vst_ref, OFF_VS), (vwt_ref, OFF_VW)):
            vt = _dot_nt(w_ref[off:off + KV_WIDTH, :], hn).astype(BF16)
            for j in range(tiles):
                for g in range(N_KV):
                    vt_ref[k * tiles + j, g * VT_ROWS:g * VT_ROWS + HEAD_DIM, :] = (
                        vt[g * HEAD_DIM:(g + 1) * HEAD_DIM, j * KT:(j + 1) * KT])
                    vt_ref[k * tiles + j, g * VT_ROWS + HEAD_DIM:(g + 1) * VT_ROWS, :] = aux

    for k in range(IN_SPLIT):
        part(k)


def _in_proj(x1, g_pre, w_in_t, tab):
    t = x1.shape[0]
    row = lambda i: (i, 0)
    fixed = lambda i: (0, 0)
    once = pl.Buffered(1)
    tile = IN_SPLIT * TOK_TILE
    tiles = tile // KT
    chunk_rows = tile // CHUNK_TOK
    slab = pl.BlockSpec((tiles, VT_SLAB, KT), lambda i: (i, 0, 0))
    slab_shape = jax.ShapeDtypeStruct((t // KT, VT_SLAB, KT), BF16)
    out_specs = [
        pl.BlockSpec((tile, Q_WIDTH), row), pl.BlockSpec((tile, KV_WIDTH), row),
        pl.BlockSpec((tile, KV_WIDTH), row), slab, slab,
        pl.BlockSpec((chunk_rows, CHUNK_TOK * W_CMP), row),
        pl.BlockSpec((tile, SLOT), row), pl.BlockSpec((tile, POOL_WIDTH), row)]
    out_shape = [
        jax.ShapeDtypeStruct((t, Q_WIDTH), BF16), jax.ShapeDtypeStruct((t, KV_WIDTH), BF16),
        jax.ShapeDtypeStruct((t, KV_WIDTH), BF16), slab_shape, slab_shape,
        jax.ShapeDtypeStruct((t // CHUNK_TOK, CHUNK_TOK * W_CMP), BF16),
        jax.ShapeDtypeStruct((t, SLOT), F32), jax.ShapeDtypeStruct((t, POOL_WIDTH), F32)]
    weights = _nbytes((OFF_MERGE, D_MODEL), F32)
    blocks = (_nbytes((tile, D_MODEL), F32) + _nbytes((tile, N_ROPE_TAB * SLOT), F32)
              + _nbytes((tile, Q_WIDTH + 2 * KV_WIDTH + W_CMP), BF16)
              + 2 * _nbytes((tiles, VT_SLAB, KT), BF16) + _nbytes((tile, SLOT + POOL_WIDTH), F32))
    res = pl.pallas_call(
        _in_proj_kernel,
        grid=(t // tile,),
        in_specs=[
            pl.BlockSpec((tile, D_MODEL), row),
            pl.BlockSpec((1, D_MODEL), fixed),
            pl.BlockSpec((pl.Element(OFF_MERGE), pl.Element(D_MODEL)), fixed, pipeline_mode=once),
            pl.BlockSpec((tile, N_ROPE_TAB * SLOT), row),
        ],
        out_specs=out_specs,
        out_shape=out_shape,
        scratch_shapes=[pltpu.VMEM((W_CMP // LANES, tile, LANES), F32)],
        compiler_params=pltpu.CompilerParams(
            dimension_semantics=("arbitrary",),
            vmem_limit_bytes=_vmem_limit(blocks, weights + _nbytes((tile, W_CMP), F32))),
        name="in_proj",
    )(x1, g_pre.reshape(1, -1), w_in_t, tab)
    return res


def _compress_kernel(x_ref, w1k_ref, w1v_ref, pek_ref, pev_ref, w1k_raw_ref, w1v_raw_ref,
                     w2k_ref, w2v_ref, tab_ref, kc_ref, vc_ref, acck_ref, accv_ref):
    l = pl.program_id(0)

    @pl.when(l == 0)
    def _():
        acck_ref[...] = jnp.zeros(acck_ref.shape, F32)
        accv_ref[...] = jnp.zeros(accv_ref.shape, F32)

    for acc_ref, w1_ref, c0 in ((acck_ref, w1k_ref, 0), (accv_ref, w1v_ref, KV_WIDTH)):
        for g in range(N_KV):
            acc_ref[g] += _dot(x_ref[:, c0 + g * HEAD_DIM:c0 + (g + 1) * HEAD_DIM], w1_ref[0])

    @pl.when(l == CHUNK_TOK - 1)
    def _():
        rows = acck_ref.shape[1]
        tab = tab_ref[...]
        for acc_ref, pe_ref, w1_raw_ref, w2_ref, o_ref, rope in (
                (acck_ref, pek_ref, w1k_raw_ref, w2k_ref, kc_ref, True),
                (accv_ref, pev_ref, w1v_raw_ref, w2v_ref, vc_ref, False)):
            bias = _dot(pe_ref[...], w1_raw_ref[...])[0:1, :]
            for g in range(N_KV):
                first = acc_ref[g, :, 0:CMP_HIDDEN]
                second = pltpu.roll(acc_ref[g, :, CMP_HIDDEN:2 * CMP_HIDDEN], rows - 1, axis=0)
                hid = jax.nn.gelu(first + second + bias).astype(BF16)
                out = _dot(hid, w2_ref[...])
                if rope:
                    out = _rope_slot(out, tab)
                o_ref[:, g * SLOT:(g + 1) * SLOT] = out.astype(BF16)


def _compress_weights(w1, w2):
    w1 = w1.reshape(2, CHUNK_TOK, HEAD_DIM, CMP_HIDDEN)
    w1_tok = jnp.concatenate([w1[0], w1[1]], axis=-1).astype(BF16)
    return w1_tok, jnp.pad(w2, ((0, 0), (0, SLOT - HEAD_DIM))).astype(BF16)


def _compress(cmp_rows, cmp_pe_k, cmp_w1_k, cmp_w2_k, cmp_pe_v, cmp_w1_v, cmp_w2_v, tab_cmp):
    rows = cmp_rows.shape[0]
    w1k, w2k = _compress_weights(cmp_w1_k, cmp_w2_k)
    w1v, w2v = _compress_weights(cmp_w1_v, cmp_w2_v)
    sub = 8
    pek = jnp.broadcast_to(cmp_pe_k.reshape(1, -1), (sub, CMP_LEN * HEAD_DIM)).astype(BF16)
    pev = jnp.broadcast_to(cmp_pe_v.reshape(1, -1), (sub, CMP_LEN * HEAD_DIM)).astype(BF16)
    fixed = lambda l: (0, 0)
    w_kv_pad = N_KV * SLOT
    blocks = (_nbytes((rows, W_CMP), BF16) + 2 * _nbytes((HEAD_DIM, 2 * CMP_HIDDEN), BF16)
              + 2 * _nbytes((sub + CMP_LEN * HEAD_DIM, CMP_HIDDEN), BF16)
              + 2 * _nbytes((CMP_HIDDEN, SLOT), BF16)
              + _nbytes((rows, N_ROPE_TAB * SLOT), F32) + 2 * _nbytes((rows, w_kv_pad), BF16))
    acc = pltpu.VMEM((N_KV, rows, 2 * CMP_HIDDEN), F32)
    return pl.pallas_call(
        _compress_kernel,
        grid=(CHUNK_TOK,),
        in_specs=[
            pl.BlockSpec((rows, W_CMP), lambda l: (0, l)),
            pl.BlockSpec((1, HEAD_DIM, 2 * CMP_HIDDEN), lambda l: (l, 0, 0)),
            pl.BlockSpec((1, HEAD_DIM, 2 * CMP_HIDDEN), lambda l: (l, 0, 0)),
            pl.BlockSpec((sub, CMP_LEN * HEAD_DIM), fixed),
            pl.BlockSpec((sub, CMP_LEN * HEAD_DIM), fixed),
            pl.BlockSpec((CMP_LEN * HEAD_DIM, CMP_HIDDEN), fixed),
            pl.BlockSpec((CMP_LEN * HEAD_DIM, CMP_HIDDEN), fixed),
            pl.BlockSpec((CMP_HIDDEN, SLOT), fixed),
            pl.BlockSpec((CMP_HIDDEN, SLOT), fixed),
            pl.BlockSpec((rows, N_ROPE_TAB * SLOT), fixed),
        ],
        out_specs=[pl.BlockSpec((rows, w_kv_pad), fixed)] * 2,
        out_shape=[jax.ShapeDtypeStruct((rows, w_kv_pad), BF16)] * 2,
        scratch_shapes=[acc, acc],
        compiler_params=pltpu.CompilerParams(
            dimension_semantics=("arbitrary",),
            vmem_limit_bytes=_vmem_limit(blocks, 2 * _nbytes(acc.shape, F32))),
        name="compress",
    )(cmp_rows, w1k, w1v, pek, pev, cmp_w1_k.astype(BF16), cmp_w1_v.astype(BF16), w2k, w2v, tab_cmp)


def _nsa_kernel(q_ref, gate_ref, kc_ref, vc_ref, ks_ref, vst_ref, kw_ref, vwt_ref, ov_ref,
                o_ref, ksp_ref, kwp_ref, qg_ref, mix_ref, *state_refs):
    i = pl.program_id(1)
    seq = ks_ref.shape[1]
    cols_g = GROUP * TQ
    chains_g = cols_g // CHAIN_COLS
    n_chains = N_KV * chains_g
    n_cmp = kc_ref.shape[1]
    sel_state = (state_refs[0:n_chains], state_refs[n_chains:2 * n_chains])
    win_state = (state_refs[2 * n_chains:3 * n_chains], state_refs[3 * n_chains:4 * n_chains])
    lane_q = lax.broadcasted_iota(jnp.int32, (TQ, SLOT), 1)
    lane_h = lax.broadcasted_iota(jnp.int32, (HALF, SLOT), 1)

    def head_slot(x_ref, rows, j):
        two = x_ref[rows, (j // 2) * SLOT:(j // 2 + 1) * SLOT]
        if j % 2:
            words = pltpu.roll(pltpu.bitcast(two, jnp.uint32), HEAD_DIM, axis=1)
            two = pltpu.bitcast(words, BF16)
        return two

    @pl.when(i == 0)
    def _():
        for c in range(seq // KT):
            rows = pl.ds(c * KT, KT)
            lane = lax.broadcasted_iota(jnp.int32, (KT, SLOT), 1)
            key_blk = (c * KT + lax.broadcasted_iota(jnp.int32, (KT, SLOT), 0)) >> SEL_SHIFT
            onehot = jnp.where(lane - BLK_LANE0 == key_blk, 1.0, 0.0).astype(BF16)
            zeros = jnp.zeros((KT, SLOT), BF16)
            for g in range(N_KV):
                gs = slice(g * SLOT, (g + 1) * SLOT)
                ksp_ref[rows, gs] = jnp.where(lane < HEAD_DIM, head_slot(ks_ref.at[0], rows, g), onehot)
                kwp_ref[rows, gs] = jnp.where(lane < HEAD_DIM, head_slot(kw_ref.at[0], rows, g), zeros)

    def q_rows(r, x):
        return slice((x * GROUP + r) * HALF, (x * GROUP + r + 1) * HALF)

    def chain_half(c):
        return c * CHAIN_COLS // (GROUP * HALF)

    for h in range(N_HEADS):
        g, r = divmod(h, GROUP)
        q_pad = jnp.where(lane_q < HEAD_DIM, head_slot(q_ref, slice(None), h), jnp.zeros((TQ, SLOT), BF16))
        for x in range(2):
            qg_ref[g, q_rows(r, x), :] = q_pad[x * HALF:(x + 1) * HALF]

    gates_t = jax.nn.sigmoid(gate_ref[...]).T
    key_h = lax.broadcasted_iota(jnp.int32, (HALF, CHAIN_COLS), 0)
    qry_h = lax.broadcasted_iota(jnp.int32, (HALF, CHAIN_COLS), 1) & (HALF - 1)
    tri_le, tri_gt = key_h <= qry_h, key_h > qry_h
    diag_plan = ((slice(0, HALF), tri_le, slice(0, HALF)), (slice(0, KT), tri_le, slice(HALF, KT)))
    far_plan = ((slice(0, KT), tri_gt, slice(0, HALF)), (slice(HALF, KT), tri_gt, slice(0, HALF)))

    def gate_row(branch, h):
        c = branch * N_HEADS + h
        return gates_t[c:c + 1, :]

    sub = 8
    cmp_per_sub = sub * SEL_LEN // CMP_STRIDE

    def compressed_scores(ng):
        n_c = ng * cmp_per_sub
        cmp_scores = []
        for h in range(N_HEADS):
            g, r = divmod(h, GROUP)
            q_h = jnp.concatenate([qg_ref[g, q_rows(r, 0), :], qg_ref[g, q_rows(r, 1), :]], axis=0)
            cmp_scores.append(_dot_nt(kc_ref[0, 0:n_c, g * SLOT:(g + 1) * SLOT], q_h))
        return cmp_scores

    def compressed_and_select(ng, start, cmp_scores):
        n_blk, n_c = ng * sub, ng * cmp_per_sub
        t_cmp = start + lax.broadcasted_iota(jnp.int32, (n_c, TQ), 1)
        n_idx = lax.broadcasted_iota(jnp.int32, (n_c, TQ), 0)
        cmp_valid = n_idx * CMP_STRIDE + (CMP_LEN - 1) <= t_cmp
        any_cmp = start + lax.broadcasted_iota(jnp.int32, (1, TQ), 1) >= CMP_LEN - 1
        t_row = start + lax.broadcasted_iota(jnp.int32, (n_blk, TQ), 1)
        blk = lax.broadcasted_iota(jnp.int32, (n_blk, TQ), 0)
        forced = (blk == t_row >> SEL_SHIFT) | (blk == 0)
        causal_blk = blk * SEL_LEN <= t_row
        idx8 = lax.broadcasted_iota(jnp.int32, (sub, TQ), 0)
        pad_c = jnp.zeros((n_cmp - n_c, TQ), F32)
        for g in range(N_KV):
            vct_g = vc_ref[0, :, g * SLOT:(g + 1) * SLOT].astype(F32).T[0:HEAD_DIM].astype(BF16)
            p_sum = jnp.zeros((n_c, TQ), F32)
            for r in range(GROUP):
                h = g * GROUP + r
                s = jnp.where(cmp_valid, cmp_scores[h], NEG_INF)
                p = jnp.exp2(s - jnp.max(s, axis=0, keepdims=True))
                inv = jnp.where(any_cmp, 1.0 / jnp.sum(p, axis=0, keepdims=True), 0.0)
                p = p * inv
                p_sum = p_sum + p
                p_all = jnp.concatenate([p, pad_c], axis=0) if ng * cmp_per_sub < n_cmp else p
                mix_ref[h] = gate_row(0, h) * _dot(vct_g, p_all.astype(BF16))
            p_sum_all = jnp.concatenate([p_sum, pad_c], axis=0) if ng * cmp_per_sub < n_cmp else p_sum
            imp = jnp.dot(ov_ref[0:n_blk, :], p_sum_all, precision=lax.Precision.HIGHEST,
                          preferred_element_type=F32)
            score = jnp.where(causal_blk, jnp.where(forced, FORCE_SCORE, imp), NEG_INF)
            rows8 = [score[k * sub:(k + 1) * sub] for k in range(ng)]
            ranks = [jnp.zeros((sub, TQ), jnp.int32) for _ in rows8]
            for c in range(n_blk):
                other = score[c:c + 1, :]
                for k, mine in enumerate(rows8):
                    if k * sub > c:
                        beats = other >= mine
                    elif (k + 1) * sub - 1 <= c:
                        beats = other > mine
                    else:
                        beats = (other > mine) | ((other == mine) & (idx8 > c - k * sub))
                    ranks[k] = ranks[k] + beats.astype(jnp.int32)
            rank = jnp.concatenate(ranks, axis=0) if ng > 1 else ranks[0]
            bias = jnp.where(causal_blk & (rank < N_SEL), 0.0, NEG_INF)
            bias_t = jnp.concatenate([jnp.zeros((BLK_LANE0, TQ), F32), bias,
                                      jnp.zeros((SLOT - BLK_LANE0 - n_blk, TQ), F32)], axis=0)
            bias_q = bias_t.T.astype(BF16)
            for r in range(GROUP):
                for x in range(2):
                    rs = q_rows(r, x)
                    qg_ref[g, rs, :] = jnp.where(lane_h < HEAD_DIM, qg_ref[g, rs, :],
                                                 bias_q[x * HALF:(x + 1) * HALF])

    ng_now = ((i + 1) * TQ + sub * SEL_LEN - 1) // (sub * SEL_LEN)
    for ng in range(1, seq // (sub * SEL_LEN) + 1):

        @pl.when(ng_now == ng)
        def _(ng=ng):
            compressed_and_select(ng, i * TQ, compressed_scores(ng))

    def flash_init(state):
        for m_ref, acc_ref in zip(*state):
            m_ref[...] = jnp.full(m_ref.shape, NEG_INF, F32)
            acc_ref[...] = jnp.zeros(acc_ref.shape, F32)

    def flash_tiles(tiles):
        jobs = [(tile, c) for tile in tiles for c in range(n_chains)]

        def key_plan(job):
            (_, _, _, plan, _), chain = job
            return (slice(0, KT), None, None) if plan is None else plan[chain_half(chain % chains_g)]

        scores = {}
        for step in range(len(jobs) + PIPE_DEPTH):
            if step < len(jobs):
                (kp_ref, _, kt, _, _), chain = jobs[step]
                g, c = divmod(chain, chains_g)
                keys, mask, rows = key_plan(jobs[step])
                off = pl.multiple_of(kt * KT + keys.start, HALF)
                s = _dot_nt(kp_ref[pl.ds(off, keys.stop - keys.start), g * SLOT:(g + 1) * SLOT],
                            qg_ref[g, c * CHAIN_COLS:(c + 1) * CHAIN_COLS, :])
                if mask is not None:
                    parts = [s[0:rows.start], jnp.where(mask, s[rows], NEG_INF), s[rows.stop:]]
                    s = jnp.concatenate([p for p in parts if p.shape[0]], axis=0)
                scores[step] = s.astype(BF16)
            done = step - PIPE_DEPTH
            if done >= 0:
                (_, vt_ref, kt, _, (m_refs, acc_refs)), chain = jobs[done]
                g = chain // chains_g
                keys = key_plan(jobs[done])[0]
                m_ref, acc_ref = m_refs[chain], acc_refs[chain]
                s = scores.pop(done)
                m_old = m_ref[...]
                packed = [s[r * BF16_SUBLANES:(r + 1) * BF16_SUBLANES]
                          for r in range(s.shape[0] // BF16_SUBLANES)]
                m_tile = functools.reduce(jnp.maximum, packed).astype(F32)
                m_new = jnp.maximum(m_old, jnp.max(m_tile, axis=0, keepdims=True))
                p = jnp.exp2(s - m_new.astype(BF16))
                pv = _dot(vt_ref[kt, g * VT_ROWS:(g + 1) * VT_ROWS, keys], p)
                acc_ref[...] = jnp.exp2(m_old - m_new) * acc_ref[...] + pv
                m_ref[...] = m_new

    def flash_mix(branch, state):
        _, acc_refs = state
        for h in range(N_HEADS):
            g, r = divmod(h, GROUP)
            gate = gate_row(branch, h)
            for x in range(2):
                c, c0 = divmod(q_rows(r, x).start, CHAIN_COLS)
                acc = acc_refs[g * chains_g + c][:, c0:c0 + HALF]
                qs = slice(x * HALF, (x + 1) * HALF)
                scale = gate[:, qs] * (1.0 / acc[ONE_ROW:ONE_ROW + 1, :])
                mix_ref[h, :, qs] = mix_ref[h, :, qs] + scale * acc[0:HEAD_DIM]

    flash_init(sel_state)
    flash_init(win_state)

    def sel_past(kt):
        return (ksp_ref, vst_ref, kt, None, sel_state)

    def sel_pair(j, carry):
        flash_tiles([sel_past(2 * j), sel_past(2 * j + 1)])
        return carry

    lax.fori_loop(0, i >> 1, sel_pair, 0)

    @pl.when(i & 1 == 1)
    def _():
        flash_tiles([sel_past(i - 1)])

    n_back = WINDOW // KT
    sel_diag = (ksp_ref, vst_ref, i, diag_plan, sel_state)
    win_diag = (kwp_ref, vwt_ref, i, diag_plan, win_state)

    def win_back(d):
        return (kwp_ref, vwt_ref, i - d, far_plan if d == n_back else None, win_state)

    for have in range(n_back + 1):
        cond = (i == have) if have < n_back else (i >= have)

        @pl.when(cond)
        def _(have=have):
            flash_tiles([sel_diag] + [win_back(d) for d in range(have, 0, -1)] + [win_diag])

    flash_mix(1, sel_state)
    flash_mix(2, win_state)
    for h2 in range(N_HEADS // 2):
        pair = jnp.concatenate([mix_ref[2 * h2], mix_ref[2 * h2 + 1]], axis=0)
        o_ref[:, h2 * SLOT:(h2 + 1) * SLOT] = pair.T.astype(BF16)


def _overlap(n_cmp_pad, n_slc):
    c0 = jnp.arange(n_cmp_pad) * CMP_STRIDE
    s0 = jnp.arange(n_slc) * SEL_LEN
    ov = jnp.minimum(c0[None, :] + CMP_LEN, s0[:, None] + SEL_LEN) - jnp.maximum(c0[None, :], s0[:, None])
    return jnp.clip(ov, 0).astype(F32) / CMP_LEN


def _nsa(q, gate, kc, vc, ks, vst, kw, vwt, batch, seq_len):
    n_slc = seq_len // SEL_LEN
    n_cmp_pad = kc.shape[1]
    w_kv_pad = N_KV * SLOT
    assert n_slc <= SLOT - BLK_LANE0 and n_slc % 8 == 0
    assert CHAIN_COLS % HALF == 0 and (GROUP * HALF) % CHAIN_COLS == 0 and KT == 2 * HALF
    ov = _overlap(n_cmp_pad, n_slc)
    nq = seq_len // TQ
    n_kt = seq_len // KT
    qrow = lambda b, i: (b * nq + i, 0)
    per_b = lambda b, i: (b, 0, 0)
    fixed = lambda b, i: (0, 0)
    blocks = (_nbytes((TQ, Q_WIDTH), BF16) + _nbytes((TQ, SLOT), F32)
              + 2 * _nbytes((n_cmp_pad, w_kv_pad), BF16) + 2 * _nbytes((seq_len, KV_WIDTH), BF16)
              + 2 * _nbytes((n_kt, VT_SLAB, KT), BF16)
              + _nbytes(ov.shape, F32) + _nbytes((TQ, Q_WIDTH), BF16))
    n_chains = N_KV * GROUP * TQ // CHAIN_COLS
    run_max = [pltpu.VMEM((1, CHAIN_COLS), F32)] * n_chains
    run_acc = [pltpu.VMEM((VT_ROWS, CHAIN_COLS), F32)] * n_chains
    scratch_shapes = (
        [pltpu.VMEM((seq_len, w_kv_pad), BF16),
         pltpu.VMEM((seq_len, w_kv_pad), BF16),
         pltpu.VMEM((N_KV, GROUP * TQ, SLOT), BF16),
         pltpu.VMEM((N_HEADS, HEAD_DIM, TQ), F32)]
        + run_max + run_acc + run_max + run_acc)
    scratch = sum(_nbytes(s.shape, s.dtype) for s in scratch_shapes)
    return pl.pallas_call(
        _nsa_kernel,
        grid=(batch, nq),
        in_specs=[
            pl.BlockSpec((TQ, Q_WIDTH), qrow),
            pl.BlockSpec((TQ, SLOT), qrow),
            pl.BlockSpec((1, n_cmp_pad, w_kv_pad), per_b),
            pl.BlockSpec((1, n_cmp_pad, w_kv_pad), per_b),
            pl.BlockSpec((1, seq_len, KV_WIDTH), per_b),
            pl.BlockSpec((n_kt, VT_SLAB, KT), per_b),
            pl.BlockSpec((1, seq_len, KV_WIDTH), per_b),
            pl.BlockSpec((n_kt, VT_SLAB, KT), per_b),
            pl.BlockSpec(ov.shape, fixed),
        ],
        out_specs=pl.BlockSpec((TQ, Q_WIDTH), qrow),
        out_shape=jax.ShapeDtypeStruct((batch * seq_len, Q_WIDTH), BF16),
        scratch_shapes=scratch_shapes,
        compiler_params=pltpu.CompilerParams(
            dimension_semantics=("arbitrary", "arbitrary"),
            vmem_limit_bytes=_vmem_limit(blocks, scratch)),
        name="nsa",
    )(q, gate, kc.reshape(batch, n_cmp_pad, w_kv_pad), vc.reshape(batch, n_cmp_pad, w_kv_pad),
      ks.reshape(batch, seq_len, KV_WIDTH), vst, kw.reshape(batch, seq_len, KV_WIDTH), vwt, ov)


def _pool_kernel(u_ref, w_ref, scale_ref, o_ref):
    seq = u_ref.shape[0]
    t = lax.broadcasted_iota(jnp.int32, (seq, POOL_GROUP), 0)

    def shifted(x, k):
        return jnp.where(t >= k, pltpu.roll(x, k, axis=0), 0.0)

    for gi, w in enumerate(POOL_WINDOWS):
        sl = slice(gi * POOL_GROUP, (gi + 1) * POOL_GROUP)
        x = u_ref[:, sl]
        wsum = x
        span = 1
        while span < w:
            wsum = wsum + shifted(wsum, span)
            span *= 2
        cnt = jnp.minimum(t + 1, w).astype(F32)
        pooled = (wsum / cnt - x).astype(BF16)
        o_ref[:, sl] = (_dot(pooled, w_ref[gi]) * scale_ref[:, sl]).astype(BF16)


def _pool(pool_in, pool_w, pool_scale, batch, seq_len):
    for w in POOL_WINDOWS:
        assert w & (w - 1) == 0
    blocks = (_nbytes((seq_len, POOL_WIDTH), F32) + _nbytes(pool_w.shape, BF16)
              + _nbytes((seq_len, POOL_WIDTH), BF16))
    return pl.pallas_call(
        _pool_kernel,
        grid=(batch,),
        in_specs=[
            pl.BlockSpec((seq_len, POOL_WIDTH), lambda b: (b, 0)),
            pl.BlockSpec(pool_w.shape, lambda b: (0, 0, 0)),
            pl.BlockSpec((1, POOL_WIDTH), lambda b: (0, 0)),
        ],
        out_specs=pl.BlockSpec((seq_len, POOL_WIDTH), lambda b: (b, 0)),
        out_shape=jax.ShapeDtypeStruct((batch * seq_len, POOL_WIDTH), BF16),
        compiler_params=pltpu.CompilerParams(
            dimension_semantics=("arbitrary",), vmem_limit_bytes=_vmem_limit(blocks)),
        name="pool",
    )(pool_in, pool_w.astype(BF16), pool_scale.reshape(1, -1))


def _merge_kernel(x_ref, gpre_ref, wt_ref, nsa_ref, wa_ref, pool_ref, wp_ref, wo_ref, gpost_ref,
                  o_ref):
    rows = x_ref.shape[0] // MERGE_SPLIT
    for k in range(MERGE_SPLIT):
        rs = slice(k * rows, (k + 1) * rows)
        x = x_ref[rs, :]
        hn = _rms(x, gpre_ref[...])
        g_attn = jax.nn.sigmoid(_dot_nt(hn, wt_ref[0:D_MODEL, :]))
        g_pool = jax.nn.sigmoid(_dot_nt(hn, wt_ref[D_MODEL:2 * D_MODEL, :]))
        y = (g_attn * _dot(nsa_ref[rs, :].astype(F32), wa_ref[...])
             + g_pool * _dot(pool_ref[rs, :].astype(F32), wp_ref[...]))
        h = _dot(y, wo_ref[...])
        o_ref[rs, :] = x + _rms(h, gpost_ref[...])


def _merge(x1, g_pre, w_in_t, o_nsa, w_attn, o_pool, w_pool, w_out, g_post):
    t = x1.shape[0]
    row = lambda i: (i, 0)
    fixed = lambda i: (0, 0)
    once = pl.Buffered(1)
    tile = MERGE_SPLIT * TOK_TILE
    weights = (_nbytes((2 * D_MODEL, D_MODEL), F32) + _nbytes((Q_WIDTH, D_MODEL), F32)
               + _nbytes((POOL_WIDTH, D_MODEL), F32) + _nbytes((D_MODEL, D_MODEL), F32))
    blocks = (2 * _nbytes((tile, D_MODEL), F32) + _nbytes((tile, Q_WIDTH), BF16)
              + _nbytes((tile, POOL_WIDTH), BF16))
    gate_rows = pl.BlockSpec((pl.Element(2 * D_MODEL), pl.Element(D_MODEL)), lambda i: (OFF_MERGE, 0),
                             pipeline_mode=once)
    return pl.pallas_call(
        _merge_kernel,
        grid=(t // tile,),
        in_specs=[
            pl.BlockSpec((tile, D_MODEL), row),
            pl.BlockSpec((1, D_MODEL), fixed),
            gate_rows,
            pl.BlockSpec((tile, Q_WIDTH), row),
            pl.BlockSpec((Q_WIDTH, D_MODEL), fixed, pipeline_mode=once),
            pl.BlockSpec((tile, POOL_WIDTH), row),
            pl.BlockSpec((POOL_WIDTH, D_MODEL), fixed, pipeline_mode=once),
            pl.BlockSpec((D_MODEL, D_MODEL), fixed, pipeline_mode=once),
            pl.BlockSpec((1, D_MODEL), fixed),
        ],
        out_specs=pl.BlockSpec((tile, D_MODEL), row),
        out_shape=jax.ShapeDtypeStruct((t, D_MODEL), F32),
        compiler_params=pltpu.CompilerParams(
            dimension_semantics=("arbitrary",), vmem_limit_bytes=_vmem_limit(blocks, weights)),
        name="merge",
    )(x1, g_pre.reshape(1, -1), w_in_t, o_nsa, w_attn, o_pool, w_pool, w_out, g_post.reshape(1, -1))


def _layer(x, tab, tab_cmp, batch, seq_len, g_ffn1_pre, w_ffn1_gate, w_ffn1_up, w_ffn1_down,
           g_ffn1_post, g_mix_pre, w_in, cmp_pe_k, cmp_w1_k, cmp_w2_k, cmp_pe_v, cmp_w1_v, cmp_w2_v,
           w_attn_branch, pool_w, pool_scale, w_pool_branch, w_out, g_mix_post, g_ffn2_pre,
           w_ffn2_gate, w_ffn2_up, w_ffn2_down, g_ffn2_post):
    x1 = _ffn(x, g_ffn1_pre, w_ffn1_gate, w_ffn1_up, w_ffn1_down, g_ffn1_post)
    w_in_t = w_in.T
    q, ks, kw, vst, vwt, cmp_rows, gate, pool_in = _in_proj(x1, g_mix_pre, w_in_t, tab)
    kc, vc = _compress(cmp_rows, cmp_pe_k, cmp_w1_k, cmp_w2_k, cmp_pe_v, cmp_w1_v, cmp_w2_v, tab_cmp)
    n_cmp_pad = seq_len // CMP_STRIDE
    kc = kc.reshape(batch, n_cmp_pad, N_KV * SLOT)
    vc = vc.reshape(batch, n_cmp_pad, N_KV * SLOT)
    o_nsa = _nsa(q, gate, kc, vc, ks, vst, kw, vwt, batch, seq_len)
    o_pool = _pool(pool_in, pool_w, pool_scale, batch, seq_len)
    x2 = _merge(x1, g_mix_pre, w_in_t, o_nsa, w_attn_branch, o_pool, w_pool_branch, w_out,
                g_mix_post)
    return _ffn(x2, g_ffn2_pre, w_ffn2_gate, w_ffn2_up, w_ffn2_down, g_ffn2_post)


def kernel(x, positions, g_ffn1_pre, w_ffn1_gate, w_ffn1_up, w_ffn1_down, g_ffn1_post, g_mix_pre, w_in, cmp_pe_k, cmp_w1_k, cmp_w2_k, cmp_pe_v, cmp_w1_v, cmp_w2_v, w_attn_branch, pool_w, pool_scale, w_pool_branch, w_out, g_mix_post, g_ffn2_pre, w_ffn2_gate, w_ffn2_up, w_ffn2_down, g_ffn2_post):
    batch, seq_len, d_model = x.shape
    assert d_model == D_MODEL and seq_len % TOK_TILE == 0 and seq_len % TQ == 0
    assert TQ == KT and TQ % SEL_LEN == 0 and TOK_TILE % KT == 0 and WINDOW % KT == 0
    assert seq_len % CMP_STRIDE == 0 and CMP_LEN == 2 * CMP_STRIDE
    t = batch * seq_len
    tab = _rope_tables(positions.reshape(t), seq_len)
    n_cmp = (seq_len - CMP_LEN) // CMP_STRIDE + 1
    n_cmp_pad = seq_len // CMP_STRIDE
    pos_cmp = jnp.pad(positions[:, CMP_LEN - 1::CMP_STRIDE], ((0, 0), (0, n_cmp_pad - n_cmp)))
    tab_cmp = _rope_tables(pos_cmp.reshape(batch * n_cmp_pad), batch * n_cmp_pad)
    xf = x.reshape(t, D_MODEL)
    per_layer = (g_ffn1_pre, w_ffn1_gate, w_ffn1_up, w_ffn1_down, g_ffn1_post, g_mix_pre, w_in,
                 cmp_pe_k, cmp_w1_k, cmp_w2_k, cmp_pe_v, cmp_w1_v, cmp_w2_v, w_attn_branch, pool_w,
                 pool_scale, w_pool_branch, w_out, g_mix_post, g_ffn2_pre, w_ffn2_gate, w_ffn2_up,
                 w_ffn2_down, g_ffn2_post)
    for l in range(g_ffn1_pre.shape[0]):
        xf = _layer(xf, tab, tab_cmp, batch, seq_len, *(p[l] for p in per_layer))
    return xf.reshape(batch, seq_len, D_MODEL)
```

```python
import functools
import math

import jax
import jax.numpy as jnp
from jax import lax
from jax.experimental import pallas as pl
from jax.experimental.pallas import tpu as pltpu

F32 = jnp.float32
BF16 = jnp.bfloat16

D_MODEL = 1024
N_HEADS = 16
HEAD_DIM = 64
N_KV = 4
GROUP = N_HEADS // N_KV
ROT_DIM = HEAD_DIM // 4
ROT_HALF = ROT_DIM // 2
ROPE_THETA = 500000.0
CMP_LEN = 32
CMP_STRIDE = 16
CMP_HIDDEN = 2 * HEAD_DIM
SEL_LEN = 64
SEL_SHIFT = SEL_LEN.bit_length() - 1
N_SEL = 8
WINDOW = 512
POOL_WINDOWS = (2, 4, 8, 16)
POOL_WIDTH = D_MODEL // 2
POOL_GROUP = POOL_WIDTH // len(POOL_WINDOWS)
D_FF = 2816
EPS = 1e-6
NEG_INF = -1e30
FORCE_SCORE = 1e4
Q_WIDTH = N_HEADS * HEAD_DIM
KV_WIDTH = N_KV * HEAD_DIM
N_GATES = 3 * N_HEADS
LOG2_E = math.log2(math.e)

LANES = 128
BF16_SUBLANES = 16
V7X_VMEM_BYTES = 64 * 1024 * 1024
VMEM_COMPILER_RESERVE = 4 * 1024 * 1024
VMEM_MIN_REQUEST = 32 * 1024 * 1024

SLOT = LANES
BLK_LANE0 = HEAD_DIM
VT_ROWS = HEAD_DIM + BF16_SUBLANES
ONE_ROW = HEAD_DIM
TOK_TILE = 512
TQ = 256
KT = 256
HALF = TQ // 2
CHAIN_COLS = 256
PIPE_DEPTH = 10
FF_CHUNK = 256
IN_SPLIT = 2
MERGE_SPLIT = 2
N_ROPE_TAB = 2


def _vmem_limit(block_bytes, scratch_bytes=0):
    need = 2 * block_bytes + scratch_bytes
    return int(min(V7X_VMEM_BYTES - VMEM_COMPILER_RESERVE, max(2 * need, VMEM_MIN_REQUEST)))


def _nbytes(shape, dtype):
    n = 1
    for s in shape:
        n *= s
    return n * jnp.dtype(dtype).itemsize


def _rms(xf, g):
    return xf * lax.rsqrt(jnp.mean(xf * xf, axis=-1, keepdims=True) + EPS) * g


def _dot(a, b):
    return jnp.dot(a, b, preferred_element_type=F32)


def _dot_nt(a, b):
    return lax.dot_general(a, b, (((1,), (1,)), ((), ())), preferred_element_type=F32)


def _rope_trig_kernel(pos_ref, inv_ref, tab_ref):
    tn = pos_ref.shape[1]
    ang = pos_ref[...].astype(F32) * inv_ref[...]
    c, s = jnp.cos(ang), jnp.sin(ang)
    rest = HEAD_DIM - ROT_DIM
    reps = SLOT // HEAD_DIM
    cos_rows = jnp.concatenate([c, c, jnp.ones((rest, tn), F32)] * reps, axis=0)
    sin_rows = jnp.concatenate([-s, s, jnp.zeros((rest, tn), F32)] * reps, axis=0)
    for j in range(tn // LANES):
        cs = slice(j * LANES, (j + 1) * LANES)
        tab_ref[cs, 0:SLOT] = cos_rows[:, cs].T
        tab_ref[cs, SLOT:2 * SLOT] = sin_rows[:, cs].T


def _rope_tables(pos_flat, tile):
    n = pos_flat.shape[0]
    inv = ROPE_THETA ** (-jnp.arange(ROT_HALF, dtype=F32) * (2.0 / ROT_DIM))
    return pl.pallas_call(
        _rope_trig_kernel,
        grid=(n // tile,),
        in_specs=[pl.BlockSpec((1, tile), lambda i: (0, i)),
                  pl.BlockSpec((ROT_HALF, 1), lambda i: (0, 0))],
        out_specs=pl.BlockSpec((tile, N_ROPE_TAB * SLOT), lambda i: (i, 0)),
        out_shape=jax.ShapeDtypeStruct((n, N_ROPE_TAB * SLOT), F32),
        compiler_params=pltpu.CompilerParams(dimension_semantics=("arbitrary",)),
        name="rope_trig",
    )(pos_flat.reshape(1, n), inv.reshape(ROT_HALF, 1))


def _rope_slot(y, tab):
    cos_t = tab[:, 0:SLOT]
    sin_t = tab[:, SLOT:2 * SLOT]
    lane = lax.broadcasted_iota(jnp.int32, y.shape, 1) & (HEAD_DIM - 1)
    up = pltpu.roll(y, SLOT - ROT_HALF, axis=1)
    down = pltpu.roll(y, ROT_HALF, axis=1)
    return y * cos_t + jnp.where(lane < ROT_HALF, up, down) * sin_t


def _ffn_kernel(x_ref, gpre_ref, wg_hbm, wu_hbm, wd_hbm, gpost_ref, o_ref, wg_ref, wu_ref, wd_ref, sem):
    n_chunks = D_FF // FF_CHUNK

    def chunk_copies(c):
        sl = slice(c * FF_CHUNK, (c + 1) * FF_CHUNK)
        return (pltpu.make_async_copy(wg_hbm.at[:, sl], wg_ref.at[:, sl], sem.at[0, c]),
                pltpu.make_async_copy(wu_hbm.at[:, sl], wu_ref.at[:, sl], sem.at[1, c]),
                pltpu.make_async_copy(wd_hbm.at[sl, :], wd_ref.at[sl, :], sem.at[2, c]))

    def body(first_step):
        x = x_ref[...]
        hn = _rms(x, gpre_ref[...])
        for c in range(n_chunks):
            sl = slice(c * FF_CHUNK, (c + 1) * FF_CHUNK)
            if first_step:
                for copy in chunk_copies(c):
                    copy.wait()
            g = _dot(hn, wg_ref[:, sl])
            u = _dot(hn, wu_ref[:, sl])
            d = _dot(g * jax.nn.sigmoid(g) * u, wd_ref[sl, :])
            if c == 0:
                o_ref[...] = d
            else:
                o_ref[...] += d
        o_ref[...] = x + 0.5 * _rms(o_ref[...], gpost_ref[...])

    @pl.when(pl.program_id(0) == 0)
    def _():
        for c in range(n_chunks):
            for copy in chunk_copies(c):
                copy.start()
        body(True)

    @pl.when(pl.program_id(0) > 0)
    def _():
        body(False)


def _ffn(x, g_pre, w_gate, w_up, w_down, g_post):
    t = x.shape[0]
    row = lambda i: (i, 0)
    fixed = lambda i: (0, 0)
    in_hbm = pl.BlockSpec(memory_space=pl.ANY)
    weights = 3 * _nbytes((D_MODEL, D_FF), F32)
    blocks = 2 * _nbytes((TOK_TILE, D_MODEL), F32)
    return pl.pallas_call(
        _ffn_kernel,
        grid=(t // TOK_TILE,),
        in_specs=[
            pl.BlockSpec((TOK_TILE, D_MODEL), row),
            pl.BlockSpec((1, D_MODEL), fixed),
            in_hbm, in_hbm, in_hbm,
            pl.BlockSpec((1, D_MODEL), fixed),
        ],
        out_specs=pl.BlockSpec((TOK_TILE, D_MODEL), row),
        out_shape=jax.ShapeDtypeStruct((t, D_MODEL), F32),
        scratch_shapes=[pltpu.VMEM((D_MODEL, D_FF), F32), pltpu.VMEM((D_MODEL, D_FF), F32),
                        pltpu.VMEM((D_FF, D_MODEL), F32),
                        pltpu.SemaphoreType.DMA((3, D_FF // FF_CHUNK))],
        compiler_params=pltpu.CompilerParams(
            dimension_semantics=("arbitrary",), vmem_limit_bytes=_vmem_limit(blocks, weights)),
        name="ffn",
    )(x, g_pre.reshape(1, -1), w_gate, w_up, w_down, g_post.reshape(1, -1))


OFF_Q = 0
OFF_CMP = OFF_Q + Q_WIDTH
OFF_KS = OFF_CMP + 2 * KV_WIDTH
OFF_VS = OFF_KS + KV_WIDTH
OFF_KW = OFF_VS + KV_WIDTH
OFF_VW = OFF_KW + KV_WIDTH
OFF_GATE = OFF_VW + KV_WIDTH
OFF_POOL = OFF_GATE + N_GATES
OFF_MERGE = OFF_POOL + POOL_WIDTH
W_CMP = 2 * KV_WIDTH
CHUNK_TOK = CMP_STRIDE
VT_SLAB = N_KV * VT_ROWS


def _in_proj_kernel(x_ref, g_ref, w_ref, tab_ref, q_ref, ks_ref, kw_ref, vst_ref,
                    vwt_ref, cmp_ref, gate_ref, pool_ref, cmp_scr):
    rows = x_ref.shape[0] // IN_SPLIT
    wide = 4 * SLOT
    aux = (lax.broadcasted_iota(jnp.int32, (VT_ROWS - HEAD_DIM, KT), 0) == 0).astype(BF16)

    def part(k):
        rs = slice(k * rows, (k + 1) * rows)
        hn = _rms(x_ref[rs, :], g_ref[...])
        tab = tab_ref[rs, :]
        tab_q = tab * (HEAD_DIM ** -0.5 * LOG2_E)
        for c in range(Q_WIDTH // wide):
            y = _dot_nt(hn, w_ref[OFF_Q + c * wide:OFF_Q + (c + 1) * wide, :])
            for j in range(wide // SLOT):
                sl = slice(c * wide + j * SLOT, c * wide + (j + 1) * SLOT)
                q_ref[rs, sl] = _rope_slot(y[:, j * SLOT:(j + 1) * SLOT], tab_q).astype(BF16)
        for k_ref, off in ((ks_ref, OFF_KS), (kw_ref, OFF_KW)):
            y = _dot_nt(hn, w_ref[off:off + KV_WIDTH, :])
            for j in range(KV_WIDTH // SLOT):
                sl = slice(j * SLOT, (j + 1) * SLOT)
                k_ref[rs, sl] = _rope_slot(y[:, sl], tab).astype(BF16)
        y = _dot_nt(hn, w_ref[OFF_CMP:OFF_CMP + W_CMP, :])
        for j in range(W_CMP // LANES):
            cmp_scr[j, rs, :] = y[:, j * LANES:(j + 1) * LANES]
        chunk_rows = rows // CHUNK_TOK
        low = lax.broadcasted_iota(jnp.int32, (chunk_rows, LANES), 1) < HEAD_DIM
        for j in range(W_CMP // LANES):
            for pair in range(CHUNK_TOK // 2):
                tok = [cmp_scr[j, pl.ds(k * rows + 2 * pair + e, chunk_rows, stride=CHUNK_TOK), :]
                       for e in range(2)]
                both = (jnp.where(low, tok[0], pltpu.roll(tok[1], HEAD_DIM, axis=1)),
                        jnp.where(low, pltpu.roll(tok[0], HEAD_DIM, axis=1), tok[1]))
                for e in range(2):
                    c0 = (2 * j + e) * CHUNK_TOK * HEAD_DIM + pair * LANES
                    cmp_ref[k * chunk_rows:(k + 1) * chunk_rows, c0:c0 + LANES] = both[e].astype(BF16)
        gate_ref[rs, :] = _dot_nt(hn, w_ref[OFF_GATE:OFF_GATE + SLOT, :])
        pool_ref[rs, :] = _dot_nt(hn, w_ref[OFF_POOL:OFF_MERGE, :])
        tiles = rows // KT
        for vt_ref, off in ((vst_ref, OFF_VS), (vwt_ref, OFF_VW)):
            vt = _dot_nt(w_ref[off:off + KV_WIDTH, :], hn).astype(BF16)
            for j in range(tiles):
                for g in range(N_KV):
                    vt_ref[k * tiles + j, g * VT_ROWS:g * VT_ROWS + HEAD_DIM, :] = (
                        vt[g * HEAD_DIM:(g + 1) * HEAD_DIM, j * KT:(j + 1) * KT])
                    vt_ref[k * tiles + j, g * VT_ROWS + HEAD_DIM:(g + 1) * VT_ROWS, :] = aux

    for k in range(IN_SPLIT):
        part(k)


def _in_proj(x1, g_pre, w_in_t, tab):
    t = x1.shape[0]
    row = lambda i: (i, 0)
    fixed = lambda i: (0, 0)
    once = pl.Buffered(1)
    tile = IN_SPLIT * TOK_TILE
    tiles = tile // KT
    chunk_rows = tile // CHUNK_TOK
    slab = pl.BlockSpec((tiles, VT_SLAB, KT), lambda i: (i, 0, 0))
    slab_shape = jax.ShapeDtypeStruct((t // KT, VT_SLAB, KT), BF16)
    out_specs = [
        pl.BlockSpec((tile, Q_WIDTH), row), pl.BlockSpec((tile, KV_WIDTH), row),
        pl.BlockSpec((tile, KV_WIDTH), row), slab, slab,
        pl.BlockSpec((chunk_rows, CHUNK_TOK * W_CMP), row),
        pl.BlockSpec((tile, SLOT), row), pl.BlockSpec((tile, POOL_WIDTH), row)]
    out_shape = [
        jax.ShapeDtypeStruct((t, Q_WIDTH), BF16), jax.ShapeDtypeStruct((t, KV_WIDTH), BF16),
        jax.ShapeDtypeStruct((t, KV_WIDTH), BF16), slab_shape, slab_shape,
        jax.ShapeDtypeStruct((t // CHUNK_TOK, CHUNK_TOK * W_CMP), BF16),
        jax.ShapeDtypeStruct((t, SLOT), F32), jax.ShapeDtypeStruct((t, POOL_WIDTH), F32)]
    weights = _nbytes((OFF_MERGE, D_MODEL), F32)
    blocks = (_nbytes((tile, D_MODEL), F32) + _nbytes((tile, N_ROPE_TAB * SLOT), F32)
              + _nbytes((tile, Q_WIDTH + 2 * KV_WIDTH + W_CMP), BF16)
              + 2 * _nbytes((tiles, VT_SLAB, KT), BF16) + _nbytes((tile, SLOT + POOL_WIDTH), F32))
    res = pl.pallas_call(
        _in_proj_kernel,
        grid=(t // tile,),
        in_specs=[
            pl.BlockSpec((tile, D_MODEL), row),
            pl.BlockSpec((1, D_MODEL), fixed),
            pl.BlockSpec((pl.Element(OFF_MERGE), pl.Element(D_MODEL)), fixed, pipeline_mode=once),
            pl.BlockSpec((tile, N_ROPE_TAB * SLOT), row),
        ],
        out_specs=out_specs,
        out_shape=out_shape,
        scratch_shapes=[pltpu.VMEM((W_CMP // LANES, tile, LANES), F32)],
        compiler_params=pltpu.CompilerParams(
            dimension_semantics=("arbitrary",),
            vmem_limit_bytes=_vmem_limit(blocks, weights + _nbytes((tile, W_CMP), F32))),
        name="in_proj",
    )(x1, g_pre.reshape(1, -1), w_in_t, tab)
    return res


def _compress_kernel(x_ref, w1_ref, pe_ref, w1_raw_ref, w2_ref, tab_ref, o_ref):
    rows = x_ref.shape[0]
    acc = _dot(x_ref[...], w1_ref[0])
    bias = _dot(pe_ref[0], w1_raw_ref[0])[0:1, :]
    second = pltpu.roll(acc[:, CMP_HIDDEN:2 * CMP_HIDDEN], rows - 1, axis=0)
    hid = jax.nn.gelu(acc[:, 0:CMP_HIDDEN] + second + bias).astype(BF16)
    out = _dot(hid, w2_ref[0])
    is_key = pl.program_id(0) < N_KV
    o_ref[...] = jnp.where(is_key, _rope_slot(out, tab_ref[...]), out).astype(BF16)


def _compress(cmp_rows, cmp_pe_k, cmp_w1_k, cmp_w2_k, cmp_pe_v, cmp_w1_v, cmp_w2_v, tab_cmp):
    rows = cmp_rows.shape[0]
    chunk_feat = CHUNK_TOK * HEAD_DIM
    sub = 8

    def stacked(f, k, v):
        return jnp.stack([f(k), f(v)]).astype(BF16)

    w1 = stacked(lambda w: jnp.concatenate([w[:chunk_feat], w[chunk_feat:]], axis=1), cmp_w1_k, cmp_w1_v)
    w1_raw = stacked(lambda w: w, cmp_w1_k, cmp_w1_v)
    pe = stacked(lambda p: jnp.broadcast_to(p.reshape(1, -1), (sub, CMP_LEN * HEAD_DIM)), cmp_pe_k, cmp_pe_v)
    w2 = stacked(lambda w: jnp.pad(w, ((0, 0), (0, SLOT - HEAD_DIM))), cmp_w2_k, cmp_w2_v)
    per_tensor = lambda s: (s // N_KV, 0, 0)
    blocks = (_nbytes((rows, chunk_feat), BF16) + _nbytes(w1.shape[1:], BF16) + _nbytes(pe.shape[1:], BF16)
              + _nbytes(w1_raw.shape[1:], BF16) + _nbytes(w2.shape[1:], BF16)
              + _nbytes((rows, N_ROPE_TAB * SLOT), F32) + _nbytes((rows, SLOT), BF16))
    return pl.pallas_call(
        _compress_kernel,
        grid=(2 * N_KV,),
        in_specs=[
            pl.BlockSpec((rows, chunk_feat), lambda s: (0, s)),
            pl.BlockSpec((1,) + w1.shape[1:], per_tensor),
            pl.BlockSpec((1,) + pe.shape[1:], per_tensor),
            pl.BlockSpec((1,) + w1_raw.shape[1:], per_tensor),
            pl.BlockSpec((1,) + w2.shape[1:], per_tensor),
            pl.BlockSpec((rows, N_ROPE_TAB * SLOT), lambda s: (0, 0)),
        ],
        out_specs=pl.BlockSpec((rows, SLOT), lambda s: (0, s)),
        out_shape=jax.ShapeDtypeStruct((rows, 2 * N_KV * SLOT), BF16),
        compiler_params=pltpu.CompilerParams(
            dimension_semantics=("arbitrary",), vmem_limit_bytes=_vmem_limit(blocks)),
        name="compress",
    )(cmp_rows, w1, pe, w1_raw, w2, tab_cmp)


def _nsa_kernel(q_ref, gate_ref, kc_ref, vc_ref, ks_ref, vst_ref, kw_ref, vwt_ref, ov_ref,
                o_ref, ksp_ref, kwp_ref, qg_ref, mix_ref, *state_refs):
    i = pl.program_id(1)
    seq = ks_ref.shape[1]
    cols_g = GROUP * TQ
    chains_g = cols_g // CHAIN_COLS
    n_chains = N_KV * chains_g
    n_cmp = kc_ref.shape[1]
    sel_state = (state_refs[0:n_chains], state_refs[n_chains:2 * n_chains])
    win_state = (state_refs[2 * n_chains:3 * n_chains], state_refs[3 * n_chains:4 * n_chains])
    lane_q = lax.broadcasted_iota(jnp.int32, (TQ, SLOT), 1)
    lane_h = lax.broadcasted_iota(jnp.int32, (HALF, SLOT), 1)

    def head_slot(x_ref, rows, j):
        two = x_ref[rows, (j // 2) * SLOT:(j // 2 + 1) * SLOT]
        if j % 2:
            words = pltpu.roll(pltpu.bitcast(two, jnp.uint32), HEAD_DIM, axis=1)
            two = pltpu.bitcast(words, BF16)
        return two

    @pl.when(i == 0)
    def _():
        for c in range(seq // KT):
            rows = pl.ds(c * KT, KT)
            lane = lax.broadcasted_iota(jnp.int32, (KT, SLOT), 1)
            key_blk = (c * KT + lax.broadcasted_iota(jnp.int32, (KT, SLOT), 0)) >> SEL_SHIFT
            onehot = jnp.where(lane - BLK_LANE0 == key_blk, 1.0, 0.0).astype(BF16)
            zeros = jnp.zeros((KT, SLOT), BF16)
            for g in range(N_KV):
                gs = slice(g * SLOT, (g + 1) * SLOT)
                ksp_ref[rows, gs] = jnp.where(lane < HEAD_DIM, head_slot(ks_ref.at[0], rows, g), onehot)
                kwp_ref[rows, gs] = jnp.where(lane < HEAD_DIM, head_slot(kw_ref.at[0], rows, g), zeros)

    def q_rows(r, x):
        return slice((x * GROUP + r) * HALF, (x * GROUP + r + 1) * HALF)

    def chain_half(c):
        return c * CHAIN_COLS // (GROUP * HALF)

    for h in range(N_HEADS):
        g, r = divmod(h, GROUP)
        q_pad = jnp.where(lane_q < HEAD_DIM, head_slot(q_ref, slice(None), h), jnp.zeros((TQ, SLOT), BF16))
        for x in range(2):
            qg_ref[g, q_rows(r, x), :] = q_pad[x * HALF:(x + 1) * HALF]

    gates_t = jax.nn.sigmoid(gate_ref[...]).T
    key_h = lax.broadcasted_iota(jnp.int32, (HALF, CHAIN_COLS), 0)
    qry_h = lax.broadcasted_iota(jnp.int32, (HALF, CHAIN_COLS), 1) & (HALF - 1)
    tri_le, tri_gt = key_h <= qry_h, key_h > qry_h
    diag_plan = ((slice(0, HALF), tri_le, slice(0, HALF)), (slice(0, KT), tri_le, slice(HALF, KT)))
    far_plan = ((slice(0, KT), tri_gt, slice(0, HALF)), (slice(HALF, KT), tri_gt, slice(0, HALF)))

    def gate_row(branch, h):
        c = branch * N_HEADS + h
        return gates_t[c:c + 1, :]

    sub = 8
    cmp_per_sub = sub * SEL_LEN // CMP_STRIDE

    def compressed_scores(ng):
        n_c = ng * cmp_per_sub
        cmp_scores = []
        for h in range(N_HEADS):
            g, r = divmod(h, GROUP)
            q_h = jnp.concatenate([qg_ref[g, q_rows(r, 0), :], qg_ref[g, q_rows(r, 1), :]], axis=0)
            cmp_scores.append(_dot_nt(kc_ref[0, 0:n_c, g * SLOT:(g + 1) * SLOT], q_h))
        return cmp_scores

    def compressed_and_select(ng, start, cmp_scores):
        n_blk, n_c = ng * sub, ng * cmp_per_sub
        t_cmp = start + lax.broadcasted_iota(jnp.int32, (n_c, TQ), 1)
        n_idx = lax.broadcasted_iota(jnp.int32, (n_c, TQ), 0)
        cmp_valid = n_idx * CMP_STRIDE + (CMP_LEN - 1) <= t_cmp
        any_cmp = start + lax.broadcasted_iota(jnp.int32, (1, TQ), 1) >= CMP_LEN - 1
        t_row = start + lax.broadcasted_iota(jnp.int32, (n_blk, TQ), 1)
        blk = lax.broadcasted_iota(jnp.int32, (n_blk, TQ), 0)
        forced = (blk == t_row >> SEL_SHIFT) | (blk == 0)
        causal_blk = blk * SEL_LEN <= t_row
        idx8 = lax.broadcasted_iota(jnp.int32, (sub, TQ), 0)
        pad_c = jnp.zeros((n_cmp - n_c, TQ), F32)
        for g in range(N_KV):
            vct_g = vc_ref[0, :, g * SLOT:(g + 1) * SLOT].astype(F32).T[0:HEAD_DIM].astype(BF16)
            p_sum = jnp.zeros((n_c, TQ), F32)
            for r in range(GROUP):
                h = g * GROUP + r
                s = jnp.where(cmp_valid, cmp_scores[h], NEG_INF)
                p = jnp.exp2(s - jnp.max(s, axis=0, keepdims=True))
                inv = jnp.where(any_cmp, 1.0 / jnp.sum(p, axis=0, keepdims=True), 0.0)
                p = p * inv
                p_sum = p_sum + p
                p_all = jnp.concatenate([p, pad_c], axis=0) if ng * cmp_per_sub < n_cmp else p
                mix_ref[h] = gate_row(0, h) * _dot(vct_g, p_all.astype(BF16))
            p_sum_all = jnp.concatenate([p_sum, pad_c], axis=0) if ng * cmp_per_sub < n_cmp else p_sum
            imp = jnp.dot(ov_ref[0:n_blk, :], p_sum_all, precision=lax.Precision.HIGHEST,
                          preferred_element_type=F32)
            score = jnp.where(causal_blk, jnp.where(forced, FORCE_SCORE, imp), NEG_INF)
            rows8 = [score[k * sub:(k + 1) * sub] for k in range(ng)]
            ranks = [jnp.zeros((sub, TQ), jnp.int32) for _ in rows8]
            for c in range(n_blk):
                other = score[c:c + 1, :]
                for k, mine in enumerate(rows8):
                    if k * sub > c:
                        beats = other >= mine
                    elif (k + 1) * sub - 1 <= c:
                        beats = other > mine
                    else:
                        beats = (other > mine) | ((other == mine) & (idx8 > c - k * sub))
                    ranks[k] = ranks[k] + beats.astype(jnp.int32)
            rank = jnp.concatenate(ranks, axis=0) if ng > 1 else ranks[0]
            bias = jnp.where(causal_blk & (rank < N_SEL), 0.0, NEG_INF)
            bias_t = jnp.concatenate([jnp.zeros((BLK_LANE0, TQ), F32), bias,
                                      jnp.zeros((SLOT - BLK_LANE0 - n_blk, TQ), F32)], axis=0)
            bias_q = bias_t.T.astype(BF16)
            for r in range(GROUP):
                for x in range(2):
                    rs = q_rows(r, x)
                    qg_ref[g, rs, :] = jnp.where(lane_h < HEAD_DIM, qg_ref[g, rs, :],
                                                 bias_q[x * HALF:(x + 1) * HALF])

    ng_now = ((i + 1) * TQ + sub * SEL_LEN - 1) // (sub * SEL_LEN)
    for ng in range(1, seq // (sub * SEL_LEN) + 1):

        @pl.when(ng_now == ng)
        def _(ng=ng):
            compressed_and_select(ng, i * TQ, compressed_scores(ng))

    def flash_init(state):
        for m_ref, acc_ref in zip(*state):
            m_ref[...] = jnp.full(m_ref.shape, NEG_INF, F32)
            acc_ref[...] = jnp.zeros(acc_ref.shape, F32)

    def flash_tiles(tiles):
        jobs = [(tile, c) for tile in tiles for c in range(n_chains)]

        def key_plan(job):
            (_, _, _, plan, _), chain = job
            return (slice(0, KT), None, None) if plan is None else plan[chain_half(chain % chains_g)]

        scores = {}
        for step in range(len(jobs) + PIPE_DEPTH):
            if step < len(jobs):
                (kp_ref, _, kt, _, _), chain = jobs[step]
                g, c = divmod(chain, chains_g)
                keys, mask, rows = key_plan(jobs[step])
                off = pl.multiple_of(kt * KT + keys.start, HALF)
                s = _dot_nt(kp_ref[pl.ds(off, keys.stop - keys.start), g * SLOT:(g + 1) * SLOT],
                            qg_ref[g, c * CHAIN_COLS:(c + 1) * CHAIN_COLS, :])
                if mask is not None:
                    parts = [s[0:rows.start], jnp.where(mask, s[rows], NEG_INF), s[rows.stop:]]
                    s = jnp.concatenate([p for p in parts if p.shape[0]], axis=0)
                scores[step] = s.astype(BF16)
            done = step - PIPE_DEPTH
            if done >= 0:
                (_, vt_ref, kt, _, (m_refs, acc_refs)), chain = jobs[done]
                g = chain // chains_g
                keys = key_plan(jobs[done])[0]
                m_ref, acc_ref = m_refs[chain], acc_refs[chain]
                s = scores.pop(done)
                m_old = m_ref[...]
                packed = [s[r * BF16_SUBLANES:(r + 1) * BF16_SUBLANES]
                          for r in range(s.shape[0] // BF16_SUBLANES)]
                m_tile = functools.reduce(jnp.maximum, packed).astype(F32)
                m_new = jnp.maximum(m_old, jnp.max(m_tile, axis=0, keepdims=True))
                p = jnp.exp2(s - m_new.astype(BF16))
                pv = _dot(vt_ref[kt, g * VT_ROWS:(g + 1) * VT_ROWS, keys], p)
                acc_ref[...] = jnp.exp2(m_old - m_new) * acc_ref[...] + pv
                m_ref[...] = m_new

    def flash_mix(branch, state):
        _, acc_refs = state
        for h in range(N_HEADS):
            g, r = divmod(h, GROUP)
            gate = gate_row(branch, h)
            for x in range(2):
                c, c0 = divmod(q_rows(r, x).start, CHAIN_COLS)
                acc = acc_refs[g * chains_g + c][:, c0:c0 + HALF]
                qs = slice(x * HALF, (x + 1) * HALF)
                scale = gate[:, qs] * (1.0 / acc[ONE_ROW:ONE_ROW + 1, :])
                mix_ref[h, :, qs] = mix_ref[h, :, qs] + scale * acc[0:HEAD_DIM]

    flash_init(sel_state)
    flash_init(win_state)

    def sel_past(kt):
        return (ksp_ref, vst_ref, kt, None, sel_state)

    def sel_pair(j, carry):
        flash_tiles([sel_past(2 * j), sel_past(2 * j + 1)])
        return carry

    lax.fori_loop(0, i >> 1, sel_pair, 0)

    @pl.when(i & 1 == 1)
    def _():
        flash_tiles([sel_past(i - 1)])

    n_back = WINDOW // KT
    sel_diag = (ksp_ref, vst_ref, i, diag_plan, sel_state)
    win_diag = (kwp_ref, vwt_ref, i, diag_plan, win_state)

    def win_back(d):
        return (kwp_ref, vwt_ref, i - d, far_plan if d == n_back else None, win_state)

    for have in range(n_back + 1):
        cond = (i == have) if have < n_back else (i >= have)

        @pl.when(cond)
        def _(have=have):
            flash_tiles([sel_diag] + [win_back(d) for d in range(have, 0, -1)] + [win_diag])

    flash_mix(1, sel_state)
    flash_mix(2, win_state)
    for h2 in range(N_HEADS // 2):
        pair = jnp.concatenate([mix_ref[2 * h2], mix_ref[2 * h2 + 1]], axis=0)
        o_ref[:, h2 * SLOT:(h2 + 1) * SLOT] = pair.T.astype(BF16)


def _overlap(n_cmp_pad, n_slc):
    c0 = jnp.arange(n_cmp_pad) * CMP_STRIDE
    s0 = jnp.arange(n_slc) * SEL_LEN
    ov = jnp.minimum(c0[None, :] + CMP_LEN, s0[:, None] + SEL_LEN) - jnp.maximum(c0[None, :], s0[:, None])
    return jnp.clip(ov, 0).astype(F32) / CMP_LEN


def _nsa(q, gate, kvc, ks, vst, kw, vwt, batch, seq_len):
    n_slc = seq_len // SEL_LEN
    n_cmp_pad = kvc.shape[0] // batch
    w_kv_pad = N_KV * SLOT
    kvc = kvc.reshape(batch, n_cmp_pad, 2 * w_kv_pad)
    assert n_slc <= SLOT - BLK_LANE0 and n_slc % 8 == 0
    assert CHAIN_COLS % HALF == 0 and (GROUP * HALF) % CHAIN_COLS == 0 and KT == 2 * HALF
    ov = _overlap(n_cmp_pad, n_slc)
    nq = seq_len // TQ
    n_kt = seq_len // KT
    qrow = lambda b, i: (b * nq + i, 0)
    per_b = lambda b, i: (b, 0, 0)
    fixed = lambda b, i: (0, 0)
    blocks = (_nbytes((TQ, Q_WIDTH), BF16) + _nbytes((TQ, SLOT), F32)
              + 2 * _nbytes((n_cmp_pad, w_kv_pad), BF16) + 2 * _nbytes((seq_len, KV_WIDTH), BF16)
              + 2 * _nbytes((n_kt, VT_SLAB, KT), BF16)
              + _nbytes(ov.shape, F32) + _nbytes((TQ, Q_WIDTH), BF16))
    n_chains = N_KV * GROUP * TQ // CHAIN_COLS
    run_max = [pltpu.VMEM((1, CHAIN_COLS), F32)] * n_chains
    run_acc = [pltpu.VMEM((VT_ROWS, CHAIN_COLS), F32)] * n_chains
    scratch_shapes = (
        [pltpu.VMEM((seq_len, w_kv_pad), BF16),
         pltpu.VMEM((seq_len, w_kv_pad), BF16),
         pltpu.VMEM((N_KV, GROUP * TQ, SLOT), BF16),
         pltpu.VMEM((N_HEADS, HEAD_DIM, TQ), F32)]
        + run_max + run_acc + run_max + run_acc)
    scratch = sum(_nbytes(s.shape, s.dtype) for s in scratch_shapes)
    return pl.pallas_call(
        _nsa_kernel,
        grid=(batch, nq),
        in_specs=[
            pl.BlockSpec((TQ, Q_WIDTH), qrow),
            pl.BlockSpec((TQ, SLOT), qrow),
            pl.BlockSpec((1, n_cmp_pad, w_kv_pad), per_b),
            pl.BlockSpec((1, n_cmp_pad, w_kv_pad), lambda b, i: (b, 0, 1)),
            pl.BlockSpec((1, seq_len, KV_WIDTH), per_b),
            pl.BlockSpec((n_kt, VT_SLAB, KT), per_b),
            pl.BlockSpec((1, seq_len, KV_WIDTH), per_b),
            pl.BlockSpec((n_kt, VT_SLAB, KT), per_b),
            pl.BlockSpec(ov.shape, fixed),
        ],
        out_specs=pl.BlockSpec((TQ, Q_WIDTH), qrow),
        out_shape=jax.ShapeDtypeStruct((batch * seq_len, Q_WIDTH), BF16),
        scratch_shapes=scratch_shapes,
        compiler_params=pltpu.CompilerParams(
            dimension_semantics=("arbitrary", "arbitrary"),
            vmem_limit_bytes=_vmem_limit(blocks, scratch)),
        name="nsa",
    )(q, gate, kvc, kvc, ks.reshape(batch, seq_len, KV_WIDTH), vst,
      kw.reshape(batch, seq_len, KV_WIDTH), vwt, ov)


def _pool_kernel(u_ref, w_ref, scale_ref, o_ref):
    seq = u_ref.shape[0]
    t = lax.broadcasted_iota(jnp.int32, (seq, POOL_GROUP), 0)

    def shifted(x, k):
        return jnp.where(t >= k, pltpu.roll(x, k, axis=0), 0.0)

    for gi, w in enumerate(POOL_WINDOWS):
        sl = slice(gi * POOL_GROUP, (gi + 1) * POOL_GROUP)
        x = u_ref[:, sl]
        wsum = x
        span = 1
        while span < w:
            wsum = wsum + shifted(wsum, span)
            span *= 2
        cnt = jnp.minimum(t + 1, w).astype(F32)
        pooled = (wsum / cnt - x).astype(BF16)
        o_ref[:, sl] = (_dot(pooled, w_ref[gi]) * scale_ref[:, sl]).astype(BF16)


def _pool(pool_in, pool_w, pool_scale, batch, seq_len):
    for w in POOL_WINDOWS:
        assert w & (w - 1) == 0
    blocks = (_nbytes((seq_len, POOL_WIDTH), F32) + _nbytes(pool_w.shape, BF16)
              + _nbytes((seq_len, POOL_WIDTH), BF16))
    return pl.pallas_call(
        _pool_kernel,
        grid=(batch,),
        in_specs=[
            pl.BlockSpec((seq_len, POOL_WIDTH), lambda b: (b, 0)),
            pl.BlockSpec(pool_w.shape, lambda b: (0, 0, 0)),
            pl.BlockSpec((1, POOL_WIDTH), lambda b: (0, 0)),
        ],
        out_specs=pl.BlockSpec((seq_len, POOL_WIDTH), lambda b: (b, 0)),
        out_shape=jax.ShapeDtypeStruct((batch * seq_len, POOL_WIDTH), BF16),
        compiler_params=pltpu.CompilerParams(
            dimension_semantics=("arbitrary",), vmem_limit_bytes=_vmem_limit(blocks)),
        name="pool",
    )(pool_in, pool_w.astype(BF16), pool_scale.reshape(1, -1))


def _merge_kernel(x_ref, gpre_ref, wt_ref, nsa_ref, wa_ref, pool_ref, wp_ref, wo_ref, gpost_ref,
                  o_ref):
    rows = x_ref.shape[0] // MERGE_SPLIT
    for k in range(MERGE_SPLIT):
        rs = slice(k * rows, (k + 1) * rows)
        x = x_ref[rs, :]
        hn = _rms(x, gpre_ref[...])
        g_attn = jax.nn.sigmoid(_dot_nt(hn, wt_ref[0:D_MODEL, :]))
        g_pool = jax.nn.sigmoid(_dot_nt(hn, wt_ref[D_MODEL:2 * D_MODEL, :]))
        y = (g_attn * _dot(nsa_ref[rs, :].astype(F32), wa_ref[...])
             + g_pool * _dot(pool_ref[rs, :].astype(F32), wp_ref[...]))
        h = _dot(y, wo_ref[...])
        o_ref[rs, :] = x + _rms(h, gpost_ref[...])


def _merge(x1, g_pre, w_in_t, o_nsa, w_attn, o_pool, w_pool, w_out, g_post):
    t = x1.shape[0]
    row = lambda i: (i, 0)
    fixed = lambda i: (0, 0)
    once = pl.Buffered(1)
    tile = MERGE_SPLIT * TOK_TILE
    weights = (_nbytes((2 * D_MODEL, D_MODEL), F32) + _nbytes((Q_WIDTH, D_MODEL), F32)
               + _nbytes((POOL_WIDTH, D_MODEL), F32) + _nbytes((D_MODEL, D_MODEL), F32))
    blocks = (2 * _nbytes((tile, D_MODEL), F32) + _nbytes((tile, Q_WIDTH), BF16)
              + _nbytes((tile, POOL_WIDTH), BF16))
    gate_rows = pl.BlockSpec((pl.Element(2 * D_MODEL), pl.Element(D_MODEL)), lambda i: (OFF_MERGE, 0),
                             pipeline_mode=once)
    return pl.pallas_call(
        _merge_kernel,
        grid=(t // tile,),
        in_specs=[
            pl.BlockSpec((tile, D_MODEL), row),
            pl.BlockSpec((1, D_MODEL), fixed),
            gate_rows,
            pl.BlockSpec((tile, Q_WIDTH), row),
            pl.BlockSpec((Q_WIDTH, D_MODEL), fixed, pipeline_mode=once),
            pl.BlockSpec((tile, POOL_WIDTH), row),
            pl.BlockSpec((POOL_WIDTH, D_MODEL), fixed, pipeline_mode=once),
            pl.BlockSpec((D_MODEL, D_MODEL), fixed, pipeline_mode=once),
            pl.BlockSpec((1, D_MODEL), fixed),
        ],
        out_specs=pl.BlockSpec((tile, D_MODEL), row),
        out_shape=jax.ShapeDtypeStruct((t, D_MODEL), F32),
        compiler_params=pltpu.CompilerParams(
            dimension_semantics=("arbitrary",), vmem_limit_bytes=_vmem_limit(blocks, weights)),
        name="merge",
    )(x1, g_pre.reshape(1, -1), w_in_t, o_nsa, w_attn, o_pool, w_pool, w_out, g_post.reshape(1, -1))


def _layer(x, tab, tab_cmp, batch, seq_len, g_ffn1_pre, w_ffn1_gate, w_ffn1_up, w_ffn1_down,
           g_ffn1_post, g_mix_pre, w_in, cmp_pe_k, cmp_w1_k, cmp_w2_k, cmp_pe_v, cmp_w1_v, cmp_w2_v,
           w_attn_branch, pool_w, pool_scale, w_pool_branch, w_out, g_mix_post, g_ffn2_pre,
           w_ffn2_gate, w_ffn2_up, w_ffn2_down, g_ffn2_post):
    x1 = _ffn(x, g_ffn1_pre, w_ffn1_gate, w_ffn1_up, w_ffn1_down, g_ffn1_post)
    w_in_t = w_in.T
    q, ks, kw, vst, vwt, cmp_rows, gate, pool_in = _in_proj(x1, g_mix_pre, w_in_t, tab)
    kvc = _compress(cmp_rows, cmp_pe_k, cmp_w1_k, cmp_w2_k, cmp_pe_v, cmp_w1_v, cmp_w2_v, tab_cmp)
    o_nsa = _nsa(q, gate, kvc, ks, vst, kw, vwt, batch, seq_len)
    o_pool = _pool(pool_in, pool_w, pool_scale, batch, seq_len)
    x2 = _merge(x1, g_mix_pre, w_in_t, o_nsa, w_attn_branch, o_pool, w_pool_branch, w_out,
                g_mix_post)
    return _ffn(x2, g_ffn2_pre, w_ffn2_gate, w_ffn2_up, w_ffn2_down, g_ffn2_post)


def kernel(x, positions, g_ffn1_pre, w_ffn1_gate, w_ffn1_up, w_ffn1_down, g_ffn1_post, g_mix_pre, w_in, cmp_pe_k, cmp_w1_k, cmp_w2_k, cmp_pe_v, cmp_w1_v, cmp_w2_v, w_attn_branch, pool_w, pool_scale, w_pool_branch, w_out, g_mix_post, g_ffn2_pre, w_ffn2_gate, w_ffn2_up, w_ffn2_down, g_ffn2_post):
    batch, seq_len, d_model = x.shape
    assert d_model == D_MODEL and seq_len % TOK_TILE == 0 and seq_len % TQ == 0
    assert TQ == KT and TQ % SEL_LEN == 0 and TOK_TILE % KT == 0 and WINDOW % KT == 0
    assert seq_len % CMP_STRIDE == 0 and CMP_LEN == 2 * CMP_STRIDE
    t = batch * seq_len
    tab = _rope_tables(positions.reshape(t), seq_len)
    n_cmp = (seq_len - CMP_LEN) // CMP_STRIDE + 1
    n_cmp_pad = seq_len // CMP_STRIDE
    pos_cmp = jnp.pad(positions[:, CMP_LEN - 1::CMP_STRIDE], ((0, 0), (0, n_cmp_pad - n_cmp)))
    tab_cmp = _rope_tables(pos_cmp.reshape(batch * n_cmp_pad), batch * n_cmp_pad)
    xf = x.reshape(t, D_MODEL)
    per_layer = (g_ffn1_pre, w_ffn1_gate, w_ffn1_up, w_ffn1_down, g_ffn1_post, g_mix_pre, w_in,
                 cmp_pe_k, cmp_w1_k, cmp_w2_k, cmp_pe_v, cmp_w1_v, cmp_w2_v, w_attn_branch, pool_w,
                 pool_scale, w_pool_branch, w_out, g_mix_post, g_ffn2_pre, w_ffn2_gate, w_ffn2_up,
                 w_ffn2_down, g_ffn2_post)
    for l in range(g_ffn1_pre.shape[0]):
        xf = _layer(xf, tab, tab_cmp, batch, seq_len, *(p[l] for p in per_layer))
    return xf.reshape(batch, seq_len, D_MODEL)
```

```python
import functools
import math

import jax
import jax.numpy as jnp
from jax import lax
from jax.experimental import pallas as pl
from jax.experimental.pallas import tpu as pltpu

F32 = jnp.float32
BF16 = jnp.bfloat16

D_MODEL = 1024
N_HEADS = 16
HEAD_DIM = 64
N_KV = 4
GROUP = N_HEADS // N_KV
ROT_DIM = HEAD_DIM // 4
ROT_HALF = ROT_DIM // 2
ROPE_THETA = 500000.0
CMP_LEN = 32
CMP_STRIDE = 16
CMP_HIDDEN = 2 * HEAD_DIM
SEL_LEN = 64
SEL_SHIFT = SEL_LEN.bit_length() - 1
N_SEL = 8
WINDOW = 512
POOL_WINDOWS = (2, 4, 8, 16)
POOL_WIDTH = D_MODEL // 2
POOL_GROUP = POOL_WIDTH // len(POOL_WINDOWS)
D_FF = 2816
EPS = 1e-6
NEG_INF = -1e30
FORCE_SCORE = 1e4
Q_WIDTH = N_HEADS * HEAD_DIM
KV_WIDTH = N_KV * HEAD_DIM
N_GATES = 3 * N_HEADS
LOG2_E = math.log2(math.e)

LANES = 128
BF16_SUBLANES = 16
V7X_VMEM_BYTES = 64 * 1024 * 1024
VMEM_COMPILER_RESERVE = 4 * 1024 * 1024
VMEM_MIN_REQUEST = 32 * 1024 * 1024

SLOT = LANES
BLK_LANE0 = HEAD_DIM
VT_ROWS = HEAD_DIM + BF16_SUBLANES
ONE_ROW = HEAD_DIM
TOK_TILE = 512
TQ = 256
KT = 256
HALF = TQ // 2
CHAIN_COLS = 256
PIPE_DEPTH = 10
FF_CHUNK = 256
IN_SPLIT = 2
MERGE_SPLIT = 2
N_ROPE_TAB = 2


def _vmem_limit(block_bytes, scratch_bytes=0):
    need = 2 * block_bytes + scratch_bytes
    return int(min(V7X_VMEM_BYTES - VMEM_COMPILER_RESERVE, max(2 * need, VMEM_MIN_REQUEST)))


def _nbytes(shape, dtype):
    n = 1
    for s in shape:
        n *= s
    return n * jnp.dtype(dtype).itemsize


def _rms(xf, g):
    return xf * lax.rsqrt(jnp.mean(xf * xf, axis=-1, keepdims=True) + EPS) * g


def _dot(a, b):
    return jnp.dot(a, b, preferred_element_type=F32)


def _dot_nt(a, b):
    return lax.dot_general(a, b, (((1,), (1,)), ((), ())), preferred_element_type=F32)


def _rope_trig_kernel(pos_ref, inv_ref, tab_ref):
    tn = pos_ref.shape[1]
    ang = pos_ref[...].astype(F32) * inv_ref[...]
    c, s = jnp.cos(ang), jnp.sin(ang)
    rest = HEAD_DIM - ROT_DIM
    reps = SLOT // HEAD_DIM
    cos_rows = jnp.concatenate([c, c, jnp.ones((rest, tn), F32)] * reps, axis=0)
    sin_rows = jnp.concatenate([-s, s, jnp.zeros((rest, tn), F32)] * reps, axis=0)
    for j in range(tn // LANES):
        cs = slice(j * LANES, (j + 1) * LANES)
        tab_ref[cs, 0:SLOT] = cos_rows[:, cs].T
        tab_ref[cs, SLOT:2 * SLOT] = sin_rows[:, cs].T


def _rope_tables(pos_flat, tile):
    n = pos_flat.shape[0]
    inv = ROPE_THETA ** (-jnp.arange(ROT_HALF, dtype=F32) * (2.0 / ROT_DIM))
    return pl.pallas_call(
        _rope_trig_kernel,
        grid=(n // tile,),
        in_specs=[pl.BlockSpec((1, tile), lambda i: (0, i)),
                  pl.BlockSpec((ROT_HALF, 1), lambda i: (0, 0))],
        out_specs=pl.BlockSpec((tile, N_ROPE_TAB * SLOT), lambda i: (i, 0)),
        out_shape=jax.ShapeDtypeStruct((n, N_ROPE_TAB * SLOT), F32),
        compiler_params=pltpu.CompilerParams(dimension_semantics=("arbitrary",)),
        name="rope_trig",
    )(pos_flat.reshape(1, n), inv.reshape(ROT_HALF, 1))


def _rope_slot(y, tab):
    cos_t = tab[:, 0:SLOT]
    sin_t = tab[:, SLOT:2 * SLOT]
    lane = lax.broadcasted_iota(jnp.int32, y.shape, 1) & (HEAD_DIM - 1)
    up = pltpu.roll(y, SLOT - ROT_HALF, axis=1)
    down = pltpu.roll(y, ROT_HALF, axis=1)
    return y * cos_t + jnp.where(lane < ROT_HALF, up, down) * sin_t


def _ffn_kernel(x_ref, gpre_ref, wg_hbm, wu_hbm, wd_hbm, gpost_ref, o_ref, wg_ref, wu_ref, wd_ref, sem):
    n_chunks = D_FF // FF_CHUNK

    def chunk_copies(c):
        sl = slice(c * FF_CHUNK, (c + 1) * FF_CHUNK)
        return (pltpu.make_async_copy(wg_hbm.at[:, sl], wg_ref.at[:, sl], sem.at[0, c]),
                pltpu.make_async_copy(wu_hbm.at[:, sl], wu_ref.at[:, sl], sem.at[1, c]),
                pltpu.make_async_copy(wd_hbm.at[sl, :], wd_ref.at[sl, :], sem.at[2, c]))

    def body(first_step):
        x = x_ref[...]
        hn = _rms(x, gpre_ref[...])
        for c in range(n_chunks):
            sl = slice(c * FF_CHUNK, (c + 1) * FF_CHUNK)
            if first_step:
                for copy in chunk_copies(c):
                    copy.wait()
            g = _dot(hn, wg_ref[:, sl])
            u = _dot(hn, wu_ref[:, sl])
            d = _dot(g * jax.nn.sigmoid(g) * u, wd_ref[sl, :])
            if c == 0:
                o_ref[...] = d
            else:
                o_ref[...] += d
        o_ref[...] = x + 0.5 * _rms(o_ref[...], gpost_ref[...])

    @pl.when(pl.program_id(0) == 0)
    def _():
        for c in range(n_chunks):
            for copy in chunk_copies(c):
                copy.start()
        body(True)

    @pl.when(pl.program_id(0) > 0)
    def _():
        body(False)


def _ffn(x, g_pre, w_gate, w_up, w_down, g_post):
    t = x.shape[0]
    row = lambda i: (i, 0)
    fixed = lambda i: (0, 0)
    in_hbm = pl.BlockSpec(memory_space=pl.ANY)
    weights = 3 * _nbytes((D_MODEL, D_FF), F32)
    blocks = 2 * _nbytes((TOK_TILE, D_MODEL), F32)
    return pl.pallas_call(
        _ffn_kernel,
        grid=(t // TOK_TILE,),
        in_specs=[
            pl.BlockSpec((TOK_TILE, D_MODEL), row),
            pl.BlockSpec((1, D_MODEL), fixed),
            in_hbm, in_hbm, in_hbm,
            pl.BlockSpec((1, D_MODEL), fixed),
        ],
        out_specs=pl.BlockSpec((TOK_TILE, D_MODEL), row),
        out_shape=jax.ShapeDtypeStruct((t, D_MODEL), F32),
        scratch_shapes=[pltpu.VMEM((D_MODEL, D_FF), F32), pltpu.VMEM((D_MODEL, D_FF), F32),
                        pltpu.VMEM((D_FF, D_MODEL), F32),
                        pltpu.SemaphoreType.DMA((3, D_FF // FF_CHUNK))],
        compiler_params=pltpu.CompilerParams(
            dimension_semantics=("arbitrary",), vmem_limit_bytes=_vmem_limit(blocks, weights)),
        name="ffn",
    )(x, g_pre.reshape(1, -1), w_gate, w_up, w_down, g_post.reshape(1, -1))


OFF_Q = 0
OFF_CMP = OFF_Q + Q_WIDTH
OFF_KS = OFF_CMP + 2 * KV_WIDTH
OFF_VS = OFF_KS + KV_WIDTH
OFF_KW = OFF_VS + KV_WIDTH
OFF_VW = OFF_KW + KV_WIDTH
OFF_GATE = OFF_VW + KV_WIDTH
OFF_POOL = OFF_GATE + N_GATES
OFF_MERGE = OFF_POOL + POOL_WIDTH
W_CMP = 2 * KV_WIDTH
CHUNK_TOK = CMP_STRIDE
VT_SLAB = N_KV * VT_ROWS


def _in_proj_kernel(x_ref, g_ref, w_ref, tab_ref, q_ref, ks_ref, kw_ref, vst_ref,
                    vwt_ref, cmp_ref, gate_ref, pool_ref, cmp_scr):
    rows = x_ref.shape[0] // IN_SPLIT
    wide = 4 * SLOT
    aux = (lax.broadcasted_iota(jnp.int32, (VT_ROWS - HEAD_DIM, KT), 0) == 0).astype(BF16)

    def part(k):
        rs = slice(k * rows, (k + 1) * rows)
        hn = _rms(x_ref[rs, :], g_ref[...])
        tab = tab_ref[rs, :]
        tab_q = tab * (HEAD_DIM ** -0.5 * LOG2_E)
        for c in range(Q_WIDTH // wide):
            y = _dot_nt(hn, w_ref[OFF_Q + c * wide:OFF_Q + (c + 1) * wide, :])
            for j in range(wide // SLOT):
                sl = slice(c * wide + j * SLOT, c * wide + (j + 1) * SLOT)
                q_ref[rs, sl] = _rope_slot(y[:, j * SLOT:(j + 1) * SLOT], tab_q).astype(BF16)
        for k_ref, off in ((ks_ref, OFF_KS), (kw_ref, OFF_KW)):
            y = _dot_nt(hn, w_ref[off:off + KV_WIDTH, :])
            for j in range(KV_WIDTH // SLOT):
                sl = slice(j * SLOT, (j + 1) * SLOT)
                k_ref[rs, sl] = _rope_slot(y[:, sl], tab).astype(BF16)
        y = _dot_nt(hn, w_ref[OFF_CMP:OFF_CMP + W_CMP, :])
        for j in range(W_CMP // LANES):
            cmp_scr[j, rs, :] = y[:, j * LANES:(j + 1) * LANES]
        chunk_rows = rows // CHUNK_TOK
        low = lax.broadcasted_iota(jnp.int32, (chunk_rows, LANES), 1) < HEAD_DIM
        for j in range(W_CMP // LANES):
            for pair in range(CHUNK_TOK // 2):
                tok = [cmp_scr[j, pl.ds(k * rows + 2 * pair + e, chunk_rows, stride=CHUNK_TOK), :]
                       for e in range(2)]
                both = (jnp.where(low, tok[0], pltpu.roll(tok[1], HEAD_DIM, axis=1)),
                        jnp.where(low, pltpu.roll(tok[0], HEAD_DIM, axis=1), tok[1]))
                for e in range(2):
                    c0 = (2 * j + e) * CHUNK_TOK * HEAD_DIM + pair * LANES
                    cmp_ref[k * chunk_rows:(k + 1) * chunk_rows, c0:c0 + LANES] = both[e].astype(BF16)
        gate_ref[rs, :] = _dot_nt(hn, w_ref[OFF_GATE:OFF_GATE + SLOT, :])
        pool_ref[rs, :] = _dot_nt(hn, w_ref[OFF_POOL:OFF_MERGE, :])
        tiles = rows // KT
        for vt_ref, off in ((vst_ref, OFF_VS), (vwt_ref, OFF_VW)):
            vt = _dot_nt(w_ref[off:off + KV_WIDTH, :], hn).astype(BF16)
            for j in range(tiles):
                for g in range(N_KV):
                    vt_ref[k * tiles + j, g * VT_ROWS:g * VT_ROWS + HEAD_DIM, :] = (
                        vt[g * HEAD_DIM:(g + 1) * HEAD_DIM, j * KT:(j + 1) * KT])
                    vt_ref[k * tiles + j, g * VT_ROWS + HEAD_DIM:(g + 1) * VT_ROWS, :] = aux

    for k in range(IN_SPLIT):
        part(k)


def _in_proj(x1, g_pre, w_in_t, tab):
    t = x1.shape[0]
    row = lambda i: (i, 0)
    fixed = lambda i: (0, 0)
    once = pl.Buffered(1)
    tile = IN_SPLIT * TOK_TILE
    tiles = tile // KT
    chunk_rows = tile // CHUNK_TOK
    slab = pl.BlockSpec((tiles, VT_SLAB, KT), lambda i: (i, 0, 0))
    slab_shape = jax.ShapeDtypeStruct((t // KT, VT_SLAB, KT), BF16)
    out_specs = [
        pl.BlockSpec((tile, Q_WIDTH), row), pl.BlockSpec((tile, KV_WIDTH), row),
        pl.BlockSpec((tile, KV_WIDTH), row), slab, slab,
        pl.BlockSpec((chunk_rows, CHUNK_TOK * W_CMP), row),
        pl.BlockSpec((tile, SLOT), row), pl.BlockSpec((tile, POOL_WIDTH), row)]
    out_shape = [
        jax.ShapeDtypeStruct((t, Q_WIDTH), BF16), jax.ShapeDtypeStruct((t, KV_WIDTH), BF16),
        jax.ShapeDtypeStruct((t, KV_WIDTH), BF16), slab_shape, slab_shape,
        jax.ShapeDtypeStruct((t // CHUNK_TOK, CHUNK_TOK * W_CMP), BF16),
        jax.ShapeDtypeStruct((t, SLOT), F32), jax.ShapeDtypeStruct((t, POOL_WIDTH), F32)]
    weights = _nbytes((OFF_MERGE, D_MODEL), F32)
    blocks = (_nbytes((tile, D_MODEL), F32) + _nbytes((tile, N_ROPE_TAB * SLOT), F32)
              + _nbytes((tile, Q_WIDTH + 2 * KV_WIDTH + W_CMP), BF16)
              + 2 * _nbytes((tiles, VT_SLAB, KT), BF16) + _nbytes((tile, SLOT + POOL_WIDTH), F32))
    res = pl.pallas_call(
        _in_proj_kernel,
        grid=(t // tile,),
        in_specs=[
            pl.BlockSpec((tile, D_MODEL), row),
            pl.BlockSpec((1, D_MODEL), fixed),
            pl.BlockSpec((pl.Element(OFF_MERGE), pl.Element(D_MODEL)), fixed, pipeline_mode=once),
            pl.BlockSpec((tile, N_ROPE_TAB * SLOT), row),
        ],
        out_specs=out_specs,
        out_shape=out_shape,
        scratch_shapes=[pltpu.VMEM((W_CMP // LANES, tile, LANES), F32)],
        compiler_params=pltpu.CompilerParams(
            dimension_semantics=("arbitrary",),
            vmem_limit_bytes=_vmem_limit(blocks, weights + _nbytes((tile, W_CMP), F32))),
        name="in_proj",
    )(x1, g_pre.reshape(1, -1), w_in_t, tab)
    return res


def _compress_kernel(x_ref, w1_ref, pe_ref, w1_raw_ref, w2_ref, tab_ref, o_ref):
    rows = x_ref.shape[0]
    acc = _dot(x_ref[...], w1_ref[0])
    bias = _dot(pe_ref[0], w1_raw_ref[0])[0:1, :]
    second = pltpu.roll(acc[:, CMP_HIDDEN:2 * CMP_HIDDEN], rows - 1, axis=0)
    hid = jax.nn.gelu(acc[:, 0:CMP_HIDDEN] + second + bias).astype(BF16)
    out = _dot(hid, w2_ref[0])
    is_key = pl.program_id(0) < N_KV
    o_ref[...] = jnp.where(is_key, _rope_slot(out, tab_ref[...]), out).astype(BF16)


def _compress(cmp_rows, cmp_pe_k, cmp_w1_k, cmp_w2_k, cmp_pe_v, cmp_w1_v, cmp_w2_v, tab_cmp):
    rows = cmp_rows.shape[0]
    chunk_feat = CHUNK_TOK * HEAD_DIM
    sub = 8

    def stacked(f, k, v):
        return jnp.stack([f(k), f(v)]).astype(BF16)

    w1 = stacked(lambda w: jnp.concatenate([w[:chunk_feat], w[chunk_feat:]], axis=1), cmp_w1_k, cmp_w1_v)
    w1_raw = stacked(lambda w: w, cmp_w1_k, cmp_w1_v)
    pe = stacked(lambda p: jnp.broadcast_to(p.reshape(1, -1), (sub, CMP_LEN * HEAD_DIM)), cmp_pe_k, cmp_pe_v)
    w2 = stacked(lambda w: jnp.pad(w, ((0, 0), (0, SLOT - HEAD_DIM))), cmp_w2_k, cmp_w2_v)
    per_tensor = lambda s: (s // N_KV, 0, 0)
    blocks = (_nbytes((rows, chunk_feat), BF16) + _nbytes(w1.shape[1:], BF16) + _nbytes(pe.shape[1:], BF16)
              + _nbytes(w1_raw.shape[1:], BF16) + _nbytes(w2.shape[1:], BF16)
              + _nbytes((rows, N_ROPE_TAB * SLOT), F32) + _nbytes((rows, SLOT), BF16))
    return pl.pallas_call(
        _compress_kernel,
        grid=(2 * N_KV,),
        in_specs=[
            pl.BlockSpec((rows, chunk_feat), lambda s: (0, s)),
            pl.BlockSpec((1,) + w1.shape[1:], per_tensor),
            pl.BlockSpec((1,) + pe.shape[1:], per_tensor),
            pl.BlockSpec((1,) + w1_raw.shape[1:], per_tensor),
            pl.BlockSpec((1,) + w2.shape[1:], per_tensor),
            pl.BlockSpec((rows, N_ROPE_TAB * SLOT), lambda s: (0, 0)),
        ],
        out_specs=pl.BlockSpec((rows, SLOT), lambda s: (0, s)),
        out_shape=jax.ShapeDtypeStruct((rows, 2 * N_KV * SLOT), BF16),
        compiler_params=pltpu.CompilerParams(
            dimension_semantics=("arbitrary",), vmem_limit_bytes=_vmem_limit(blocks)),
        name="compress",
    )(cmp_rows, w1, pe, w1_raw, w2, tab_cmp)


def _nsa_kernel(q_ref, gate_ref, kc_ref, vc_ref, ks_ref, vst_ref, kw_ref, vwt_ref, ov_ref,
                o_ref, ksp_ref, kwp_ref, qg_ref, mix_ref, *state_refs):
    i = pl.program_id(1)
    seq = ks_ref.shape[1]
    cols_g = GROUP * TQ
    chains_g = cols_g // CHAIN_COLS
    n_chains = N_KV * chains_g
    n_cmp = kc_ref.shape[1]
    sel_state = (state_refs[0:n_chains], state_refs[n_chains:2 * n_chains])
    win_state = (state_refs[2 * n_chains:3 * n_chains], state_refs[3 * n_chains:4 * n_chains])
    lane_q = lax.broadcasted_iota(jnp.int32, (TQ, SLOT), 1)

    def head_slot(x_ref, rows, j):
        two = x_ref[rows, (j // 2) * SLOT:(j // 2 + 1) * SLOT]
        if j % 2:
            words = pltpu.roll(pltpu.bitcast(two, jnp.uint32), HEAD_DIM, axis=1)
            two = pltpu.bitcast(words, BF16)
        return two

    @pl.when(i == 0)
    def _():
        for c in range(seq // KT):
            rows = pl.ds(c * KT, KT)
            lane = lax.broadcasted_iota(jnp.int32, (KT, SLOT), 1)
            key_blk = (c * KT + lax.broadcasted_iota(jnp.int32, (KT, SLOT), 0)) >> SEL_SHIFT
            onehot = jnp.where(lane - BLK_LANE0 == key_blk, 1.0, 0.0).astype(BF16)
            zeros = jnp.zeros((KT, SLOT), BF16)
            for g in range(N_KV):
                gs = slice(g * SLOT, (g + 1) * SLOT)
                ksp_ref[rows, gs] = jnp.where(lane < HEAD_DIM, head_slot(ks_ref.at[0], rows, g), onehot)
                kwp_ref[rows, gs] = jnp.where(lane < HEAD_DIM, head_slot(kw_ref.at[0], rows, g), zeros)

    def q_cols(r, x):
        return slice((x * GROUP + r) * HALF, (x * GROUP + r + 1) * HALF)

    def chain_half(c):
        return c * CHAIN_COLS // (GROUP * HALF)

    for h in range(N_HEADS):
        g, r = divmod(h, GROUP)
        q_pad = jnp.where(lane_q < HEAD_DIM, head_slot(q_ref, slice(None), h), jnp.zeros((TQ, SLOT), BF16))
        q_pad_t = q_pad.astype(F32).T.astype(BF16)
        for x in range(2):
            qg_ref[g, :, q_cols(r, x)] = q_pad_t[:, x * HALF:(x + 1) * HALF]

    gates_t = jax.nn.sigmoid(gate_ref[...]).T
    key_h = lax.broadcasted_iota(jnp.int32, (HALF, CHAIN_COLS), 0)
    qry_h = lax.broadcasted_iota(jnp.int32, (HALF, CHAIN_COLS), 1) & (HALF - 1)
    tri_le, tri_gt = key_h <= qry_h, key_h > qry_h
    diag_plan = ((slice(0, HALF), tri_le, slice(0, HALF)), (slice(0, KT), tri_le, slice(HALF, KT)))
    far_plan = ((slice(0, KT), tri_gt, slice(0, HALF)), (slice(HALF, KT), tri_gt, slice(0, HALF)))

    def gate_row(branch, h):
        c = branch * N_HEADS + h
        return gates_t[c:c + 1, :]

    sub = 8
    cmp_per_sub = sub * SEL_LEN // CMP_STRIDE

    def compressed_scores(ng):
        n_c = ng * cmp_per_sub
        cmp_scores = []
        for h in range(N_HEADS):
            g, r = divmod(h, GROUP)
            q_h = jnp.concatenate([qg_ref[g, :, q_cols(r, 0)], qg_ref[g, :, q_cols(r, 1)]], axis=1)
            cmp_scores.append(_dot(kc_ref[0, 0:n_c, g * SLOT:(g + 1) * SLOT], q_h))
        return cmp_scores

    def compressed_and_select(ng, start, cmp_scores):
        n_blk, n_c = ng * sub, ng * cmp_per_sub
        t_cmp = start + lax.broadcasted_iota(jnp.int32, (n_c, TQ), 1)
        n_idx = lax.broadcasted_iota(jnp.int32, (n_c, TQ), 0)
        cmp_valid = n_idx * CMP_STRIDE + (CMP_LEN - 1) <= t_cmp
        any_cmp = start + lax.broadcasted_iota(jnp.int32, (1, TQ), 1) >= CMP_LEN - 1
        t_row = start + lax.broadcasted_iota(jnp.int32, (n_blk, TQ), 1)
        blk = lax.broadcasted_iota(jnp.int32, (n_blk, TQ), 0)
        forced = (blk == t_row >> SEL_SHIFT) | (blk == 0)
        causal_blk = blk * SEL_LEN <= t_row
        idx8 = lax.broadcasted_iota(jnp.int32, (sub, TQ), 0)
        pad_c = jnp.zeros((n_cmp - n_c, TQ), F32)
        for g in range(N_KV):
            vct_g = vc_ref[0, :, g * SLOT:(g + 1) * SLOT].astype(F32).T[0:HEAD_DIM].astype(BF16)
            p_sum = jnp.zeros((n_c, TQ), F32)
            for r in range(GROUP):
                h = g * GROUP + r
                s = jnp.where(cmp_valid, cmp_scores[h], NEG_INF)
                p = jnp.exp2(s - jnp.max(s, axis=0, keepdims=True))
                inv = jnp.where(any_cmp, 1.0 / jnp.sum(p, axis=0, keepdims=True), 0.0)
                p = p * inv
                p_sum = p_sum + p
                p_all = jnp.concatenate([p, pad_c], axis=0) if ng * cmp_per_sub < n_cmp else p
                mix_ref[h] = gate_row(0, h) * _dot(vct_g, p_all.astype(BF16))
            p_sum_all = jnp.concatenate([p_sum, pad_c], axis=0) if ng * cmp_per_sub < n_cmp else p_sum
            imp = jnp.dot(ov_ref[0:n_blk, :], p_sum_all, precision=lax.Precision.HIGHEST,
                          preferred_element_type=F32)
            score = jnp.where(causal_blk, jnp.where(forced, FORCE_SCORE, imp), NEG_INF)
            rows8 = [score[k * sub:(k + 1) * sub] for k in range(ng)]
            ranks = [jnp.zeros((sub, TQ), jnp.int32) for _ in rows8]
            for c in range(n_blk):
                other = score[c:c + 1, :]
                for k, mine in enumerate(rows8):
                    if k * sub > c:
                        beats = other >= mine
                    elif (k + 1) * sub - 1 <= c:
                        beats = other > mine
                    else:
                        beats = (other > mine) | ((other == mine) & (idx8 > c - k * sub))
                    ranks[k] = ranks[k] + beats.astype(jnp.int32)
            rank = jnp.concatenate(ranks, axis=0) if ng > 1 else ranks[0]
            bias = jnp.where(causal_blk & (rank < N_SEL), 0.0, NEG_INF).astype(BF16)
            for r in range(GROUP):
                for x in range(2):
                    qg_ref[g, BLK_LANE0:BLK_LANE0 + n_blk, q_cols(r, x)] = bias[:, x * HALF:(x + 1) * HALF]

    ng_now = ((i + 1) * TQ + sub * SEL_LEN - 1) // (sub * SEL_LEN)
    for ng in range(1, seq // (sub * SEL_LEN) + 1):

        @pl.when(ng_now == ng)
        def _(ng=ng):
            compressed_and_select(ng, i * TQ, compressed_scores(ng))

    def flash_init(state):
        for m_ref, acc_ref in zip(*state):
            m_ref[...] = jnp.full(m_ref.shape, NEG_INF, F32)
            acc_ref[...] = jnp.zeros(acc_ref.shape, F32)

    def flash_tiles(tiles):
        jobs = [(tile, c) for tile in tiles for c in range(n_chains)]

        def key_plan(job):
            (_, _, _, plan, _), chain = job
            return (slice(0, KT), None, None) if plan is None else plan[chain_half(chain % chains_g)]

        scores = {}
        for step in range(len(jobs) + PIPE_DEPTH):
            if step < len(jobs):
                (kp_ref, _, kt, _, _), chain = jobs[step]
                g, c = divmod(chain, chains_g)
                keys, mask, rows = key_plan(jobs[step])
                off = pl.multiple_of(kt * KT + keys.start, HALF)
                s = _dot(kp_ref[pl.ds(off, keys.stop - keys.start), g * SLOT:(g + 1) * SLOT],
                         qg_ref[g, :, c * CHAIN_COLS:(c + 1) * CHAIN_COLS])
                if mask is not None:
                    parts = [s[0:rows.start], jnp.where(mask, s[rows], NEG_INF), s[rows.stop:]]
                    s = jnp.concatenate([p for p in parts if p.shape[0]], axis=0)
                scores[step] = s.astype(BF16)
            done = step - PIPE_DEPTH
            if done >= 0:
                (_, vt_ref, kt, _, (m_refs, acc_refs)), chain = jobs[done]
                g = chain // chains_g
                keys = key_plan(jobs[done])[0]
                m_ref, acc_ref = m_refs[chain], acc_refs[chain]
                s = scores.pop(done)
                m_old = m_ref[...]
                packed = [s[r * BF16_SUBLANES:(r + 1) * BF16_SUBLANES]
                          for r in range(s.shape[0] // BF16_SUBLANES)]
                m_tile = functools.reduce(jnp.maximum, packed).astype(F32)
                m_new = jnp.maximum(m_old, jnp.max(m_tile, axis=0, keepdims=True))
                p = jnp.exp2(s - m_new.astype(BF16))
                pv = _dot(vt_ref[kt, g * VT_ROWS:(g + 1) * VT_ROWS, keys], p)
                acc_ref[...] = jnp.exp2(m_old - m_new) * acc_ref[...] + pv
                m_ref[...] = m_new

    def flash_mix(branch, state):
        _, acc_refs = state
        for h in range(N_HEADS):
            g, r = divmod(h, GROUP)
            gate = gate_row(branch, h)
            for x in range(2):
                c, c0 = divmod(q_cols(r, x).start, CHAIN_COLS)
                acc = acc_refs[g * chains_g + c][:, c0:c0 + HALF]
                qs = slice(x * HALF, (x + 1) * HALF)
                scale = gate[:, qs] * (1.0 / acc[ONE_ROW:ONE_ROW + 1, :])
                mix_ref[h, :, qs] = mix_ref[h, :, qs] + scale * acc[0:HEAD_DIM]

    flash_init(sel_state)
    flash_init(win_state)

    def sel_past(kt):
        return (ksp_ref, vst_ref, kt, None, sel_state)

    def sel_pair(j, carry):
        flash_tiles([sel_past(2 * j), sel_past(2 * j + 1)])
        return carry

    lax.fori_loop(0, i >> 1, sel_pair, 0)

    @pl.when(i & 1 == 1)
    def _():
        flash_tiles([sel_past(i - 1)])

    n_back = WINDOW // KT
    sel_diag = (ksp_ref, vst_ref, i, diag_plan, sel_state)
    win_diag = (kwp_ref, vwt_ref, i, diag_plan, win_state)

    def win_back(d):
        return (kwp_ref, vwt_ref, i - d, far_plan if d == n_back else None, win_state)

    for have in range(n_back + 1):
        cond = (i == have) if have < n_back else (i >= have)

        @pl.when(cond)
        def _(have=have):
            flash_tiles([sel_diag] + [win_back(d) for d in range(have, 0, -1)] + [win_diag])

    flash_mix(1, sel_state)
    flash_mix(2, win_state)
    for h2 in range(N_HEADS // 2):
        pair = jnp.concatenate([mix_ref[2 * h2], mix_ref[2 * h2 + 1]], axis=0)
        o_ref[:, h2 * SLOT:(h2 + 1) * SLOT] = pair.T.astype(BF16)


def _overlap(n_cmp_pad, n_slc):
    c0 = jnp.arange(n_cmp_pad) * CMP_STRIDE
    s0 = jnp.arange(n_slc) * SEL_LEN
    ov = jnp.minimum(c0[None, :] + CMP_LEN, s0[:, None] + SEL_LEN) - jnp.maximum(c0[None, :], s0[:, None])
    return jnp.clip(ov, 0).astype(F32) / CMP_LEN


def _nsa(q, gate, kvc, ks, vst, kw, vwt, batch, seq_len):
    n_slc = seq_len // SEL_LEN
    n_cmp_pad = kvc.shape[0] // batch
    w_kv_pad = N_KV * SLOT
    kvc = kvc.reshape(batch, n_cmp_pad, 2 * w_kv_pad)
    assert n_slc <= SLOT - BLK_LANE0 and n_slc % 8 == 0
    assert CHAIN_COLS % HALF == 0 and (GROUP * HALF) % CHAIN_COLS == 0 and KT == 2 * HALF
    ov = _overlap(n_cmp_pad, n_slc)
    nq = seq_len // TQ
    n_kt = seq_len // KT
    qrow = lambda b, i: (b * nq + i, 0)
    per_b = lambda b, i: (b, 0, 0)
    fixed = lambda b, i: (0, 0)
    blocks = (_nbytes((TQ, Q_WIDTH), BF16) + _nbytes((TQ, SLOT), F32)
              + 2 * _nbytes((n_cmp_pad, w_kv_pad), BF16) + 2 * _nbytes((seq_len, KV_WIDTH), BF16)
              + 2 * _nbytes((n_kt, VT_SLAB, KT), BF16)
              + _nbytes(ov.shape, F32) + _nbytes((TQ, Q_WIDTH), BF16))
    n_chains = N_KV * GROUP * TQ // CHAIN_COLS
    run_max = [pltpu.VMEM((1, CHAIN_COLS), F32)] * n_chains
    run_acc = [pltpu.VMEM((VT_ROWS, CHAIN_COLS), F32)] * n_chains
    scratch_shapes = (
        [pltpu.VMEM((seq_len, w_kv_pad), BF16),
         pltpu.VMEM((seq_len, w_kv_pad), BF16),
         pltpu.VMEM((N_KV, SLOT, GROUP * TQ), BF16),
         pltpu.VMEM((N_HEADS, HEAD_DIM, TQ), F32)]
        + run_max + run_acc + run_max + run_acc)
    scratch = sum(_nbytes(s.shape, s.dtype) for s in scratch_shapes)
    return pl.pallas_call(
        _nsa_kernel,
        grid=(batch, nq),
        in_specs=[
            pl.BlockSpec((TQ, Q_WIDTH), qrow),
            pl.BlockSpec((TQ, SLOT), qrow),
            pl.BlockSpec((1, n_cmp_pad, w_kv_pad), per_b),
            pl.BlockSpec((1, n_cmp_pad, w_kv_pad), lambda b, i: (b, 0, 1)),
            pl.BlockSpec((1, seq_len, KV_WIDTH), per_b),
            pl.BlockSpec((n_kt, VT_SLAB, KT), per_b),
            pl.BlockSpec((1, seq_len, KV_WIDTH), per_b),
            pl.BlockSpec((n_kt, VT_SLAB, KT), per_b),
            pl.BlockSpec(ov.shape, fixed),
        ],
        out_specs=pl.BlockSpec((TQ, Q_WIDTH), qrow),
        out_shape=jax.ShapeDtypeStruct((batch * seq_len, Q_WIDTH), BF16),
        scratch_shapes=scratch_shapes,
        compiler_params=pltpu.CompilerParams(
            dimension_semantics=("arbitrary", "arbitrary"),
            vmem_limit_bytes=_vmem_limit(blocks, scratch)),
        name="nsa",
    )(q, gate, kvc, kvc, ks.reshape(batch, seq_len, KV_WIDTH), vst,
      kw.reshape(batch, seq_len, KV_WIDTH), vwt, ov)


def _pool_kernel(u_ref, w_ref, scale_ref, o_ref):
    seq = u_ref.shape[0]
    t = lax.broadcasted_iota(jnp.int32, (seq, POOL_GROUP), 0)

    def shifted(x, k):
        return jnp.where(t >= k, pltpu.roll(x, k, axis=0), 0.0)

    for gi, w in enumerate(POOL_WINDOWS):
        sl = slice(gi * POOL_GROUP, (gi + 1) * POOL_GROUP)
        x = u_ref[:, sl]
        wsum = x
        span = 1
        while span < w:
            wsum = wsum + shifted(wsum, span)
            span *= 2
        cnt = jnp.minimum(t + 1, w).astype(F32)
        pooled = (wsum / cnt - x).astype(BF16)
        o_ref[:, sl] = (_dot(pooled, w_ref[gi]) * scale_ref[:, sl]).astype(BF16)


def _pool(pool_in, pool_w, pool_scale, batch, seq_len):
    for w in POOL_WINDOWS:
        assert w & (w - 1) == 0
    blocks = (_nbytes((seq_len, POOL_WIDTH), F32) + _nbytes(pool_w.shape, BF16)
              + _nbytes((seq_len, POOL_WIDTH), BF16))
    return pl.pallas_call(
        _pool_kernel,
        grid=(batch,),
        in_specs=[
            pl.BlockSpec((seq_len, POOL_WIDTH), lambda b: (b, 0)),
            pl.BlockSpec(pool_w.shape, lambda b: (0, 0, 0)),
            pl.BlockSpec((1, POOL_WIDTH), lambda b: (0, 0)),
        ],
        out_specs=pl.BlockSpec((seq_len, POOL_WIDTH), lambda b: (b, 0)),
        out_shape=jax.ShapeDtypeStruct((batch * seq_len, POOL_WIDTH), BF16),
        compiler_params=pltpu.CompilerParams(
            dimension_semantics=("arbitrary",), vmem_limit_bytes=_vmem_limit(blocks)),
        name="pool",
    )(pool_in, pool_w.astype(BF16), pool_scale.reshape(1, -1))


def _merge_kernel(x_ref, gpre_ref, wt_ref, nsa_ref, wa_ref, pool_ref, wp_ref, wo_ref, gpost_ref,
                  o_ref):
    rows = x_ref.shape[0] // MERGE_SPLIT
    for k in range(MERGE_SPLIT):
        rs = slice(k * rows, (k + 1) * rows)
        x = x_ref[rs, :]
        hn = _rms(x, gpre_ref[...])
        g_attn = jax.nn.sigmoid(_dot_nt(hn, wt_ref[0:D_MODEL, :]))
        g_pool = jax.nn.sigmoid(_dot_nt(hn, wt_ref[D_MODEL:2 * D_MODEL, :]))
        y = (g_attn * _dot(nsa_ref[rs, :].astype(F32), wa_ref[...])
             + g_pool * _dot(pool_ref[rs, :].astype(F32), wp_ref[...]))
        h = _dot(y, wo_ref[...])
        o_ref[rs, :] = x + _rms(h, gpost_ref[...])


def _merge(x1, g_pre, w_in_t, o_nsa, w_attn, o_pool, w_pool, w_out, g_post):
    t = x1.shape[0]
    row = lambda i: (i, 0)
    fixed = lambda i: (0, 0)
    once = pl.Buffered(1)
    tile = MERGE_SPLIT * TOK_TILE
    weights = (_nbytes((2 * D_MODEL, D_MODEL), F32) + _nbytes((Q_WIDTH, D_MODEL), F32)
               + _nbytes((POOL_WIDTH, D_MODEL), F32) + _nbytes((D_MODEL, D_MODEL), F32))
    blocks = (2 * _nbytes((tile, D_MODEL), F32) + _nbytes((tile, Q_WIDTH), BF16)
              + _nbytes((tile, POOL_WIDTH), BF16))
    gate_rows = pl.BlockSpec((pl.Element(2 * D_MODEL), pl.Element(D_MODEL)), lambda i: (OFF_MERGE, 0),
                             pipeline_mode=once)
    return pl.pallas_call(
        _merge_kernel,
        grid=(t // tile,),
        in_specs=[
            pl.BlockSpec((tile, D_MODEL), row),
            pl.BlockSpec((1, D_MODEL), fixed),
            gate_rows,
            pl.BlockSpec((tile, Q_WIDTH), row),
            pl.BlockSpec((Q_WIDTH, D_MODEL), fixed, pipeline_mode=once),
            pl.BlockSpec((tile, POOL_WIDTH), row),
            pl.BlockSpec((POOL_WIDTH, D_MODEL), fixed, pipeline_mode=once),
            pl.BlockSpec((D_MODEL, D_MODEL), fixed, pipeline_mode=once),
            pl.BlockSpec((1, D_MODEL), fixed),
        ],
        out_specs=pl.BlockSpec((tile, D_MODEL), row),
        out_shape=jax.ShapeDtypeStruct((t, D_MODEL), F32),
        compiler_params=pltpu.CompilerParams(
            dimension_semantics=("arbitrary",), vmem_limit_bytes=_vmem_limit(blocks, weights)),
        name="merge",
    )(x1, g_pre.reshape(1, -1), w_in_t, o_nsa, w_attn, o_pool, w_pool, w_out, g_post.reshape(1, -1))


def _layer(x, tab, tab_cmp, batch, seq_len, g_ffn1_pre, w_ffn1_gate, w_ffn1_up, w_ffn1_down,
           g_ffn1_post, g_mix_pre, w_in, cmp_pe_k, cmp_w1_k, cmp_w2_k, cmp_pe_v, cmp_w1_v, cmp_w2_v,
           w_attn_branch, pool_w, pool_scale, w_pool_branch, w_out, g_mix_post, g_ffn2_pre,
           w_ffn2_gate, w_ffn2_up, w_ffn2_down, g_ffn2_post):
    x1 = _ffn(x, g_ffn1_pre, w_ffn1_gate, w_ffn1_up, w_ffn1_down, g_ffn1_post)
    w_in_t = w_in.T
    q, ks, kw, vst, vwt, cmp_rows, gate, pool_in = _in_proj(x1, g_mix_pre, w_in_t, tab)
    kvc = _compress(cmp_rows, cmp_pe_k, cmp_w1_k, cmp_w2_k, cmp_pe_v, cmp_w1_v, cmp_w2_v, tab_cmp)
    o_nsa = _nsa(q, gate, kvc, ks, vst, kw, vwt, batch, seq_len)
    o_pool = _pool(pool_in, pool_w, pool_scale, batch, seq_len)
    x2 = _merge(x1, g_mix_pre, w_in_t, o_nsa, w_attn_branch, o_pool, w_pool_branch, w_out,
                g_mix_post)
    return _ffn(x2, g_ffn2_pre, w_ffn2_gate, w_ffn2_up, w_ffn2_down, g_ffn2_post)


def kernel(x, positions, g_ffn1_pre, w_ffn1_gate, w_ffn1_up, w_ffn1_down, g_ffn1_post, g_mix_pre, w_in, cmp_pe_k, cmp_w1_k, cmp_w2_k, cmp_pe_v, cmp_w1_v, cmp_w2_v, w_attn_branch, pool_w, pool_scale, w_pool_branch, w_out, g_mix_post, g_ffn2_pre, w_ffn2_gate, w_ffn2_up, w_ffn2_down, g_ffn2_post):
    batch, seq_len, d_model = x.shape
    assert d_model == D_MODEL and seq_len % TOK_TILE == 0 and seq_len % TQ == 0
    assert TQ == KT and TQ % SEL_LEN == 0 and TOK_TILE % KT == 0 and WINDOW % KT == 0
    assert seq_len % CMP_STRIDE == 0 and CMP_LEN == 2 * CMP_STRIDE
    t = batch * seq_len
    tab = _rope_tables(positions.reshape(t), seq_len)
    n_cmp = (seq_len - CMP_LEN) // CMP_STRIDE + 1
    n_cmp_pad = seq_len // CMP_STRIDE
    pos_cmp = jnp.pad(positions[:, CMP_LEN - 1::CMP_STRIDE], ((0, 0), (0, n_cmp_pad - n_cmp)))
    tab_cmp = _rope_tables(pos_cmp.reshape(batch * n_cmp_pad), batch * n_cmp_pad)
    xf = x.reshape(t, D_MODEL)
    per_layer = (g_ffn1_pre, w_ffn1_gate, w_ffn1_up, w_ffn1_down, g_ffn1_post, g_mix_pre, w_in,
                 cmp_pe_k, cmp_w1_k, cmp_w2_k, cmp_pe_v, cmp_w1_v, cmp_w2_v, w_attn_branch, pool_w,
                 pool_scale, w_pool_branch, w_out, g_mix_post, g_ffn2_pre, w_ffn2_gate, w_ffn2_up,
                 w_ffn2_down, g_ffn2_post)
    for l in range(g_ffn1_pre.shape[0]):
        xf = _layer(xf, tab, tab_cmp, batch, seq_len, *(p[l] for p in per_layer))
    return xf.reshape(batch, seq_len, D_MODEL)
```

```python
import functools
import math

import jax
import jax.numpy as jnp
from jax import lax
from jax.experimental import pallas as pl
from jax.experimental.pallas import tpu as pltpu

F32 = jnp.float32
BF16 = jnp.bfloat16

D_MODEL = 1024
N_HEADS = 16
HEAD_DIM = 64
N_KV = 4
GROUP = N_HEADS // N_KV
ROT_DIM = HEAD_DIM // 4
ROT_HALF = ROT_DIM // 2
ROPE_THETA = 500000.0
CMP_LEN = 32
CMP_STRIDE = 16
CMP_HIDDEN = 2 * HEAD_DIM
SEL_LEN = 64
SEL_SHIFT = SEL_LEN.bit_length() - 1
N_SEL = 8
WINDOW = 512
POOL_WINDOWS = (2, 4, 8, 16)
POOL_WIDTH = D_MODEL // 2
POOL_GROUP = POOL_WIDTH // len(POOL_WINDOWS)
D_FF = 2816
EPS = 1e-6
NEG_INF = -1e30
FORCE_SCORE = 1e4
Q_WIDTH = N_HEADS * HEAD_DIM
KV_WIDTH = N_KV * HEAD_DIM
N_GATES = 3 * N_HEADS
LOG2_E = math.log2(math.e)

LANES = 128
BF16_SUBLANES = 16
V7X_VMEM_BYTES = 64 * 1024 * 1024
VMEM_COMPILER_RESERVE = 4 * 1024 * 1024
VMEM_MIN_REQUEST = 32 * 1024 * 1024

SLOT = LANES
BLK_LANE0 = HEAD_DIM
VT_ROWS = HEAD_DIM + BF16_SUBLANES
ONE_ROW = HEAD_DIM
TOK_TILE = 512
TQ = 256
KT = 256
HALF = TQ // 2
CHAIN_COLS = 256
PIPE_DEPTH = 10
FF_CHUNK = 256
IN_SPLIT = 2
MERGE_SPLIT = 2
N_ROPE_TAB = 2


def _vmem_limit(block_bytes, scratch_bytes=0):
    need = 2 * block_bytes + scratch_bytes
    return int(min(V7X_VMEM_BYTES - VMEM_COMPILER_RESERVE, max(2 * need, VMEM_MIN_REQUEST)))


def _nbytes(shape, dtype):
    n = 1
    for s in shape:
        n *= s
    return n * jnp.dtype(dtype).itemsize


def _rms(xf, g):
    return xf * lax.rsqrt(jnp.mean(xf * xf, axis=-1, keepdims=True) + EPS) * g


def _dot(a, b):
    return jnp.dot(a, b, preferred_element_type=F32)


def _dot_nt(a, b):
    return lax.dot_general(a, b, (((1,), (1,)), ((), ())), preferred_element_type=F32)


def _rope_trig_kernel(pos_ref, inv_ref, tab_ref):
    tn = pos_ref.shape[1]
    ang = pos_ref[...].astype(F32) * inv_ref[...]
    c, s = jnp.cos(ang), jnp.sin(ang)
    rest = HEAD_DIM - ROT_DIM
    reps = SLOT // HEAD_DIM
    cos_rows = jnp.concatenate([c, c, jnp.ones((rest, tn), F32)] * reps, axis=0)
    sin_rows = jnp.concatenate([-s, s, jnp.zeros((rest, tn), F32)] * reps, axis=0)
    for j in range(tn // LANES):
        cs = slice(j * LANES, (j + 1) * LANES)
        tab_ref[cs, 0:SLOT] = cos_rows[:, cs].T
        tab_ref[cs, SLOT:2 * SLOT] = sin_rows[:, cs].T


def _rope_tables(pos_flat, tile):
    n = pos_flat.shape[0]
    inv = ROPE_THETA ** (-jnp.arange(ROT_HALF, dtype=F32) * (2.0 / ROT_DIM))
    return pl.pallas_call(
        _rope_trig_kernel,
        grid=(n // tile,),
        in_specs=[pl.BlockSpec((1, tile), lambda i: (0, i)),
                  pl.BlockSpec((ROT_HALF, 1), lambda i: (0, 0))],
        out_specs=pl.BlockSpec((tile, N_ROPE_TAB * SLOT), lambda i: (i, 0)),
        out_shape=jax.ShapeDtypeStruct((n, N_ROPE_TAB * SLOT), F32),
        compiler_params=pltpu.CompilerParams(dimension_semantics=("arbitrary",)),
        name="rope_trig",
    )(pos_flat.reshape(1, n), inv.reshape(ROT_HALF, 1))


def _rope_slot(y, tab):
    cos_t = tab[:, 0:SLOT]
    sin_t = tab[:, SLOT:2 * SLOT]
    lane = lax.broadcasted_iota(jnp.int32, y.shape, 1) & (HEAD_DIM - 1)
    up = pltpu.roll(y, SLOT - ROT_HALF, axis=1)
    down = pltpu.roll(y, ROT_HALF, axis=1)
    return y * cos_t + jnp.where(lane < ROT_HALF, up, down) * sin_t


def _ffn_kernel(x_ref, gpre_ref, wg_hbm, wu_hbm, wd_hbm, gpost_ref, o_ref, wg_ref, wu_ref, wd_ref, sem):
    n_chunks = D_FF // FF_CHUNK

    def chunk_copies(c):
        sl = slice(c * FF_CHUNK, (c + 1) * FF_CHUNK)
        return (pltpu.make_async_copy(wg_hbm.at[:, sl], wg_ref.at[:, sl], sem.at[0, c]),
                pltpu.make_async_copy(wu_hbm.at[:, sl], wu_ref.at[:, sl], sem.at[1, c]),
                pltpu.make_async_copy(wd_hbm.at[sl, :], wd_ref.at[sl, :], sem.at[2, c]))

    def body(first_step):
        x = x_ref[...]
        hn = _rms(x, gpre_ref[...])
        for c in range(n_chunks):
            sl = slice(c * FF_CHUNK, (c + 1) * FF_CHUNK)
            if first_step:
                for copy in chunk_copies(c):
                    copy.wait()
            g = _dot(hn, wg_ref[:, sl])
            u = _dot(hn, wu_ref[:, sl])
            d = _dot(g * jax.nn.sigmoid(g) * u, wd_ref[sl, :])
            if c == 0:
                o_ref[...] = d
            else:
                o_ref[...] += d
        o_ref[...] = x + 0.5 * _rms(o_ref[...], gpost_ref[...])

    @pl.when(pl.program_id(0) == 0)
    def _():
        for c in range(n_chunks):
            for copy in chunk_copies(c):
                copy.start()
        body(True)

    @pl.when(pl.program_id(0) > 0)
    def _():
        body(False)


def _ffn(x, g_pre, w_gate, w_up, w_down, g_post):
    t = x.shape[0]
    row = lambda i: (i, 0)
    fixed = lambda i: (0, 0)
    in_hbm = pl.BlockSpec(memory_space=pl.ANY)
    weights = 3 * _nbytes((D_MODEL, D_FF), F32)
    blocks = 2 * _nbytes((TOK_TILE, D_MODEL), F32)
    return pl.pallas_call(
        _ffn_kernel,
        grid=(t // TOK_TILE,),
        in_specs=[
            pl.BlockSpec((TOK_TILE, D_MODEL), row),
            pl.BlockSpec((1, D_MODEL), fixed),
            in_hbm, in_hbm, in_hbm,
            pl.BlockSpec((1, D_MODEL), fixed),
        ],
        out_specs=pl.BlockSpec((TOK_TILE, D_MODEL), row),
        out_shape=jax.ShapeDtypeStruct((t, D_MODEL), F32),
        scratch_shapes=[pltpu.VMEM((D_MODEL, D_FF), F32), pltpu.VMEM((D_MODEL, D_FF), F32),
                        pltpu.VMEM((D_FF, D_MODEL), F32),
                        pltpu.SemaphoreType.DMA((3, D_FF // FF_CHUNK))],
        compiler_params=pltpu.CompilerParams(
            dimension_semantics=("arbitrary",), vmem_limit_bytes=_vmem_limit(blocks, weights)),
        name="ffn",
    )(x, g_pre.reshape(1, -1), w_gate, w_up, w_down, g_post.reshape(1, -1))


OFF_Q = 0
OFF_CMP = OFF_Q + Q_WIDTH
OFF_KS = OFF_CMP + 2 * KV_WIDTH
OFF_VS = OFF_KS + KV_WIDTH
OFF_KW = OFF_VS + KV_WIDTH
OFF_VW = OFF_KW + KV_WIDTH
OFF_GATE = OFF_VW + KV_WIDTH
OFF_POOL = OFF_GATE + N_GATES
OFF_MERGE = OFF_POOL + POOL_WIDTH
W_CMP = 2 * KV_WIDTH
CHUNK_TOK = CMP_STRIDE
VT_SLAB = N_KV * VT_ROWS


def _in_proj_kernel(x_ref, g_ref, w_ref, tab_ref, q_ref, ks_ref, kw_ref, vst_ref,
                    vwt_ref, cmp_ref, gate_ref, pool_ref, cmp_scr):
    rows = x_ref.shape[0] // IN_SPLIT
    wide = 4 * SLOT
    aux = (lax.broadcasted_iota(jnp.int32, (VT_ROWS - HEAD_DIM, KT), 0) == 0).astype(BF16)

    def part(k):
        rs = slice(k * rows, (k + 1) * rows)
        hn = _rms(x_ref[rs, :], g_ref[...])
        tab = tab_ref[rs, :]
        tab_q = tab * (HEAD_DIM ** -0.5 * LOG2_E)
        for c in range(Q_WIDTH // wide):
            y = _dot_nt(hn, w_ref[OFF_Q + c * wide:OFF_Q + (c + 1) * wide, :])
            for j in range(wide // SLOT):
                sl = slice(c * wide + j * SLOT, c * wide + (j + 1) * SLOT)
                q_ref[rs, sl] = _rope_slot(y[:, j * SLOT:(j + 1) * SLOT], tab_q).astype(BF16)
        for k_ref, off in ((ks_ref, OFF_KS), (kw_ref, OFF_KW)):
            y = _dot_nt(hn, w_ref[off:off + KV_WIDTH, :])
            for j in range(KV_WIDTH // SLOT):
                sl = slice(j * SLOT, (j + 1) * SLOT)
                k_ref[rs, sl] = _rope_slot(y[:, sl], tab).astype(BF16)
        y = _dot_nt(hn, w_ref[OFF_CMP:OFF_CMP + W_CMP, :])
        for j in range(W_CMP // LANES):
            cmp_scr[j, rs, :] = y[:, j * LANES:(j + 1) * LANES]
        chunk_rows = rows // CHUNK_TOK
        low = lax.broadcasted_iota(jnp.int32, (chunk_rows, LANES), 1) < HEAD_DIM
        for j in range(W_CMP // LANES):
            for pair in range(CHUNK_TOK // 2):
                tok = [cmp_scr[j, pl.ds(k * rows + 2 * pair + e, chunk_rows, stride=CHUNK_TOK), :]
                       for e in range(2)]
                both = (jnp.where(low, tok[0], pltpu.roll(tok[1], HEAD_DIM, axis=1)),
                        jnp.where(low, pltpu.roll(tok[0], HEAD_DIM, axis=1), tok[1]))
                for e in range(2):
                    c0 = (2 * j + e) * CHUNK_TOK * HEAD_DIM + pair * LANES
                    cmp_ref[k * chunk_rows:(k + 1) * chunk_rows, c0:c0 + LANES] = both[e].astype(BF16)
        gate_ref[rs, :] = _dot_nt(hn, w_ref[OFF_GATE:OFF_GATE + SLOT, :])
        pool_ref[rs, :] = _dot_nt(hn, w_ref[OFF_POOL:OFF_MERGE, :])
        tiles = rows // KT
        for vt_ref, off in ((vst_ref, OFF_VS), (vwt_ref, OFF_VW)):
            vt = _dot_nt(w_ref[off:off + KV_WIDTH, :], hn).astype(BF16)
            for j in range(tiles):
                for g in range(N_KV):
                    vt_ref[k * tiles + j, g * VT_ROWS:g * VT_ROWS + HEAD_DIM, :] = (
                        vt[g * HEAD_DIM:(g + 1) * HEAD_DIM, j * KT:(j + 1) * KT])
                    vt_ref[k * tiles + j, g * VT_ROWS + HEAD_DIM:(g + 1) * VT_ROWS, :] = aux

    for k in range(IN_SPLIT):
        part(k)


def _in_proj(x1, g_pre, w_in_t, tab):
    t = x1.shape[0]
    row = lambda i: (i, 0)
    fixed = lambda i: (0, 0)
    once = pl.Buffered(1)
    tile = IN_SPLIT * TOK_TILE
    tiles = tile // KT
    chunk_rows = tile // CHUNK_TOK
    slab = pl.BlockSpec((tiles, VT_SLAB, KT), lambda i: (i, 0, 0))
    slab_shape = jax.ShapeDtypeStruct((t // KT, VT_SLAB, KT), BF16)
    out_specs = [
        pl.BlockSpec((tile, Q_WIDTH), row), pl.BlockSpec((tile, KV_WIDTH), row),
        pl.BlockSpec((tile, KV_WIDTH), row), slab, slab,
        pl.BlockSpec((chunk_rows, CHUNK_TOK * W_CMP), row),
        pl.BlockSpec((tile, SLOT), row), pl.BlockSpec((tile, POOL_WIDTH), row)]
    out_shape = [
        jax.ShapeDtypeStruct((t, Q_WIDTH), BF16), jax.ShapeDtypeStruct((t, KV_WIDTH), BF16),
        jax.ShapeDtypeStruct((t, KV_WIDTH), BF16), slab_shape, slab_shape,
        jax.ShapeDtypeStruct((t // CHUNK_TOK, CHUNK_TOK * W_CMP), BF16),
        jax.ShapeDtypeStruct((t, SLOT), F32), jax.ShapeDtypeStruct((t, POOL_WIDTH), F32)]
    weights = _nbytes((OFF_MERGE, D_MODEL), F32)
    blocks = (_nbytes((tile, D_MODEL), F32) + _nbytes((tile, N_ROPE_TAB * SLOT), F32)
              + _nbytes((tile, Q_WIDTH + 2 * KV_WIDTH + W_CMP), BF16)
              + 2 * _nbytes((tiles, VT_SLAB, KT), BF16) + _nbytes((tile, SLOT + POOL_WIDTH), F32))
    res = pl.pallas_call(
        _in_proj_kernel,
        grid=(t // tile,),
        in_specs=[
            pl.BlockSpec((tile, D_MODEL), row),
            pl.BlockSpec((1, D_MODEL), fixed),
            pl.BlockSpec((pl.Element(OFF_MERGE), pl.Element(D_MODEL)), fixed, pipeline_mode=once),
            pl.BlockSpec((tile, N_ROPE_TAB * SLOT), row),
        ],
        out_specs=out_specs,
        out_shape=out_shape,
        scratch_shapes=[pltpu.VMEM((W_CMP // LANES, tile, LANES), F32)],
        compiler_params=pltpu.CompilerParams(
            dimension_semantics=("arbitrary",),
            vmem_limit_bytes=_vmem_limit(blocks, weights + _nbytes((tile, W_CMP), F32))),
        name="in_proj",
    )(x1, g_pre.reshape(1, -1), w_in_t, tab)
    return res


def _compress_kernel(x_hbm, w1_ref, pe_ref, w1_raw_ref, w2_ref, tab_ref, o_ref, x_buf, sem):
    s = pl.program_id(0)
    _, rows, chunk_feat = x_buf.shape

    def fetch(step):
        cols = pl.ds(pl.multiple_of(step * chunk_feat, chunk_feat), chunk_feat)
        return pltpu.make_async_copy(x_hbm.at[:, cols], x_buf.at[step % 2], sem.at[step % 2])

    @pl.when(s == 0)
    def _():
        fetch(s).start()

    @pl.when(s + 1 < pl.num_programs(0))
    def _():
        fetch(s + 1).start()

    fetch(s).wait()
    acc = _dot(x_buf[s % 2], w1_ref[0])
    bias = _dot(pe_ref[0], w1_raw_ref[0])[0:1, :]
    second = pltpu.roll(acc[:, CMP_HIDDEN:2 * CMP_HIDDEN], rows - 1, axis=0)
    hid = jax.nn.gelu(acc[:, 0:CMP_HIDDEN] + second + bias).astype(BF16)
    out = _dot(hid, w2_ref[0])
    is_key = pl.program_id(0) < N_KV
    o_ref[...] = jnp.where(is_key, _rope_slot(out, tab_ref[...]), out).astype(BF16)


def _compress(cmp_rows, cmp_pe_k, cmp_w1_k, cmp_w2_k, cmp_pe_v, cmp_w1_v, cmp_w2_v, tab_cmp):
    rows = cmp_rows.shape[0]
    chunk_feat = CHUNK_TOK * HEAD_DIM
    sub = 8

    def stacked(f, k, v):
        return jnp.stack([f(k), f(v)]).astype(BF16)

    w1 = stacked(lambda w: jnp.concatenate([w[:chunk_feat], w[chunk_feat:]], axis=1), cmp_w1_k, cmp_w1_v)
    w1_raw = stacked(lambda w: w, cmp_w1_k, cmp_w1_v)
    pe = stacked(lambda p: jnp.broadcast_to(p.reshape(1, -1), (sub, CMP_LEN * HEAD_DIM)), cmp_pe_k, cmp_pe_v)
    w2 = stacked(lambda w: jnp.pad(w, ((0, 0), (0, SLOT - HEAD_DIM))), cmp_w2_k, cmp_w2_v)
    per_tensor = lambda s: (s // N_KV, 0, 0)
    blocks = (_nbytes(w1.shape[1:], BF16) + _nbytes(pe.shape[1:], BF16)
              + _nbytes(w1_raw.shape[1:], BF16) + _nbytes(w2.shape[1:], BF16)
              + _nbytes((rows, N_ROPE_TAB * SLOT), F32) + _nbytes((rows, SLOT), BF16))
    x_slots = (2, rows, chunk_feat)
    return pl.pallas_call(
        _compress_kernel,
        grid=(2 * N_KV,),
        in_specs=[
            pl.BlockSpec(memory_space=pltpu.HBM),
            pl.BlockSpec((1,) + w1.shape[1:], per_tensor),
            pl.BlockSpec((1,) + pe.shape[1:], per_tensor),
            pl.BlockSpec((1,) + w1_raw.shape[1:], per_tensor),
            pl.BlockSpec((1,) + w2.shape[1:], per_tensor),
            pl.BlockSpec((rows, N_ROPE_TAB * SLOT), lambda s: (0, 0)),
        ],
        out_specs=pl.BlockSpec((rows, SLOT), lambda s: (0, s)),
        out_shape=jax.ShapeDtypeStruct((rows, 2 * N_KV * SLOT), BF16),
        scratch_shapes=[pltpu.VMEM(x_slots, BF16), pltpu.SemaphoreType.DMA((2,))],
        compiler_params=pltpu.CompilerParams(
            dimension_semantics=("arbitrary",),
            vmem_limit_bytes=_vmem_limit(blocks, _nbytes(x_slots, BF16))),
        name="compress",
    )(cmp_rows, w1, pe, w1_raw, w2, tab_cmp)


def _nsa_kernel(q_ref, gate_ref, kc_ref, vc_ref, ks_ref, vst_ref, kw_ref, vwt_ref, ov_ref,
                o_ref, ksp_ref, kwp_ref, qg_ref, mix_ref, *state_refs):
    i = pl.program_id(1)
    seq = ks_ref.shape[1]
    cols_g = GROUP * TQ
    chains_g = cols_g // CHAIN_COLS
    n_chains = N_KV * chains_g
    n_cmp = kc_ref.shape[1]
    sel_state = (state_refs[0:n_chains], state_refs[n_chains:2 * n_chains])
    win_state = (state_refs[2 * n_chains:3 * n_chains], state_refs[3 * n_chains:4 * n_chains])
    lane_q = lax.broadcasted_iota(jnp.int32, (TQ, SLOT), 1)

    def head_slot(x_ref, rows, j):
        two = x_ref[rows, (j // 2) * SLOT:(j // 2 + 1) * SLOT]
        if j % 2:
            words = pltpu.roll(pltpu.bitcast(two, jnp.uint32), HEAD_DIM, axis=1)
            two = pltpu.bitcast(words, BF16)
        return two

    @pl.when(i == 0)
    def _():
        for c in range(seq // KT):
            rows = pl.ds(c * KT, KT)
            lane = lax.broadcasted_iota(jnp.int32, (KT, SLOT), 1)
            key_blk = (c * KT + lax.broadcasted_iota(jnp.int32, (KT, SLOT), 0)) >> SEL_SHIFT
            onehot = jnp.where(lane - BLK_LANE0 == key_blk, 1.0, 0.0).astype(BF16)
            zeros = jnp.zeros((KT, SLOT), BF16)
            for g in range(N_KV):
                gs = slice(g * SLOT, (g + 1) * SLOT)
                ksp_ref[rows, gs] = jnp.where(lane < HEAD_DIM, head_slot(ks_ref.at[0], rows, g), onehot)
                kwp_ref[rows, gs] = jnp.where(lane < HEAD_DIM, head_slot(kw_ref.at[0], rows, g), zeros)

    def q_cols(r, x):
        return slice((x * GROUP + r) * HALF, (x * GROUP + r + 1) * HALF)

    def chain_half(c):
        return c * CHAIN_COLS // (GROUP * HALF)

    for h in range(N_HEADS):
        g, r = divmod(h, GROUP)
        q_pad = jnp.where(lane_q < HEAD_DIM, head_slot(q_ref, slice(None), h), jnp.zeros((TQ, SLOT), BF16))
        q_pad_t = q_pad.astype(F32).T.astype(BF16)
        for x in range(2):
            qg_ref[g, :, q_cols(r, x)] = q_pad_t[:, x * HALF:(x + 1) * HALF]

    gates_t = jax.nn.sigmoid(gate_ref[...]).T
    key_h = lax.broadcasted_iota(jnp.int32, (HALF, CHAIN_COLS), 0)
    qry_h = lax.broadcasted_iota(jnp.int32, (HALF, CHAIN_COLS), 1) & (HALF - 1)
    tri_le, tri_gt = key_h <= qry_h, key_h > qry_h
    diag_plan = ((slice(0, HALF), tri_le, slice(0, HALF)), (slice(0, KT), tri_le, slice(HALF, KT)))
    far_plan = ((slice(0, KT), tri_gt, slice(0, HALF)), (slice(HALF, KT), tri_gt, slice(0, HALF)))

    def gate_row(branch, h):
        c = branch * N_HEADS + h
        return gates_t[c:c + 1, :]

    sub = 8
    cmp_per_sub = sub * SEL_LEN // CMP_STRIDE

    def compressed_scores(ng):
        n_c = ng * cmp_per_sub
        cmp_scores = []
        for h in range(N_HEADS):
            g, r = divmod(h, GROUP)
            q_h = jnp.concatenate([qg_ref[g, :, q_cols(r, 0)], qg_ref[g, :, q_cols(r, 1)]], axis=1)
            cmp_scores.append(_dot(kc_ref[0, 0:n_c, g * SLOT:(g + 1) * SLOT], q_h))
        return cmp_scores

    def compressed_and_select(ng, start, cmp_scores):
        n_blk, n_c = ng * sub, ng * cmp_per_sub
        t_cmp = start + lax.broadcasted_iota(jnp.int32, (n_c, TQ), 1)
        n_idx = lax.broadcasted_iota(jnp.int32, (n_c, TQ), 0)
        cmp_valid = n_idx * CMP_STRIDE + (CMP_LEN - 1) <= t_cmp
        any_cmp = start + lax.broadcasted_iota(jnp.int32, (1, TQ), 1) >= CMP_LEN - 1
        t_row = start + lax.broadcasted_iota(jnp.int32, (n_blk, TQ), 1)
        blk = lax.broadcasted_iota(jnp.int32, (n_blk, TQ), 0)
        forced = (blk == t_row >> SEL_SHIFT) | (blk == 0)
        causal_blk = blk * SEL_LEN <= t_row
        idx8 = lax.broadcasted_iota(jnp.int32, (sub, TQ), 0)
        pad_c = jnp.zeros((n_cmp - n_c, TQ), F32)
        for g in range(N_KV):
            vct_g = vc_ref[0, :, g * SLOT:(g + 1) * SLOT].astype(F32).T[0:HEAD_DIM].astype(BF16)
            p_sum = jnp.zeros((n_c, TQ), F32)
            for r in range(GROUP):
                h = g * GROUP + r
                s = jnp.where(cmp_valid, cmp_scores[h], NEG_INF)
                p = jnp.exp2(s - jnp.max(s, axis=0, keepdims=True))
                inv = jnp.where(any_cmp, 1.0 / jnp.sum(p, axis=0, keepdims=True), 0.0)
                p = p * inv
                p_sum = p_sum + p
                p_all = jnp.concatenate([p, pad_c], axis=0) if ng * cmp_per_sub < n_cmp else p
                mix_ref[h] = gate_row(0, h) * _dot(vct_g, p_all.astype(BF16))
            p_sum_all = jnp.concatenate([p_sum, pad_c], axis=0) if ng * cmp_per_sub < n_cmp else p_sum
            imp = jnp.dot(ov_ref[0:n_blk, :], p_sum_all, precision=lax.Precision.HIGHEST,
                          preferred_element_type=F32)
            score = jnp.where(causal_blk, jnp.where(forced, FORCE_SCORE, imp), NEG_INF)
            rows8 = [score[k * sub:(k + 1) * sub] for k in range(ng)]
            ranks = [jnp.zeros((sub, TQ), jnp.int32) for _ in rows8]
            for c in range(n_blk):
                other = score[c:c + 1, :]
                for k, mine in enumerate(rows8):
                    if k * sub > c:
                        beats = other >= mine
                    elif (k + 1) * sub - 1 <= c:
                        beats = other > mine
                    else:
                        beats = (other > mine) | ((other == mine) & (idx8 > c - k * sub))
                    ranks[k] = ranks[k] + beats.astype(jnp.int32)
            rank = jnp.concatenate(ranks, axis=0) if ng > 1 else ranks[0]
            bias = jnp.where(causal_blk & (rank < N_SEL), 0.0, NEG_INF).astype(BF16)
            for r in range(GROUP):
                for x in range(2):
                    qg_ref[g, BLK_LANE0:BLK_LANE0 + n_blk, q_cols(r, x)] = bias[:, x * HALF:(x + 1) * HALF]

    ng_now = ((i + 1) * TQ + sub * SEL_LEN - 1) // (sub * SEL_LEN)
    for ng in range(1, seq // (sub * SEL_LEN) + 1):

        @pl.when(ng_now == ng)
        def _(ng=ng):
            compressed_and_select(ng, i * TQ, compressed_scores(ng))

    def flash_init(state):
        for m_ref, acc_ref in zip(*state):
            m_ref[...] = jnp.full(m_ref.shape, NEG_INF, F32)
            acc_ref[...] = jnp.zeros(acc_ref.shape, F32)

    def flash_tiles(tiles):
        jobs = [(tile, c) for tile in tiles for c in range(n_chains)]

        def key_plan(job):
            (_, _, _, plan, _), chain = job
            return (slice(0, KT), None, None) if plan is None else plan[chain_half(chain % chains_g)]

        scores = {}
        for step in range(len(jobs) + PIPE_DEPTH):
            if step < len(jobs):
                (kp_ref, _, kt, _, _), chain = jobs[step]
                g, c = divmod(chain, chains_g)
                keys, mask, rows = key_plan(jobs[step])
                off = pl.multiple_of(kt * KT + keys.start, HALF)
                s = _dot(kp_ref[pl.ds(off, keys.stop - keys.start), g * SLOT:(g + 1) * SLOT],
                         qg_ref[g, :, c * CHAIN_COLS:(c + 1) * CHAIN_COLS])
                if mask is not None:
                    parts = [s[0:rows.start], jnp.where(mask, s[rows], NEG_INF), s[rows.stop:]]
                    s = jnp.concatenate([p for p in parts if p.shape[0]], axis=0)
                scores[step] = s.astype(BF16)
            done = step - PIPE_DEPTH
            if done >= 0:
                (_, vt_ref, kt, _, (m_refs, acc_refs)), chain = jobs[done]
                g = chain // chains_g
                keys = key_plan(jobs[done])[0]
                m_ref, acc_ref = m_refs[chain], acc_refs[chain]
                s = scores.pop(done)
                m_old = m_ref[...]
                packed = [s[r * BF16_SUBLANES:(r + 1) * BF16_SUBLANES]
                          for r in range(s.shape[0] // BF16_SUBLANES)]
                m_tile = functools.reduce(jnp.maximum, packed).astype(F32)
                m_new = jnp.maximum(m_old, jnp.max(m_tile, axis=0, keepdims=True))
                p = jnp.exp2(s - m_new.astype(BF16))
                pv = _dot(vt_ref[kt, g * VT_ROWS:(g + 1) * VT_ROWS, keys], p)
                acc_ref[...] = jnp.exp2(m_old - m_new) * acc_ref[...] + pv
                m_ref[...] = m_new

    def flash_mix(branch, state):
        _, acc_refs = state
        for h in range(N_HEADS):
            g, r = divmod(h, GROUP)
            gate = gate_row(branch, h)
            for x in range(2):
                c, c0 = divmod(q_cols(r, x).start, CHAIN_COLS)
                acc = acc_refs[g * chains_g + c][:, c0:c0 + HALF]
                qs = slice(x * HALF, (x + 1) * HALF)
                scale = gate[:, qs] * (1.0 / acc[ONE_ROW:ONE_ROW + 1, :])
                mix_ref[h, :, qs] = mix_ref[h, :, qs] + scale * acc[0:HEAD_DIM]

    flash_init(sel_state)
    flash_init(win_state)

    def sel_past(kt):
        return (ksp_ref, vst_ref, kt, None, sel_state)

    def sel_pair(j, carry):
        flash_tiles([sel_past(2 * j), sel_past(2 * j + 1)])
        return carry

    lax.fori_loop(0, i >> 1, sel_pair, 0)

    @pl.when(i & 1 == 1)
    def _():
        flash_tiles([sel_past(i - 1)])

    n_back = WINDOW // KT
    sel_diag = (ksp_ref, vst_ref, i, diag_plan, sel_state)
    win_diag = (kwp_ref, vwt_ref, i, diag_plan, win_state)

    def win_back(d):
        return (kwp_ref, vwt_ref, i - d, far_plan if d == n_back else None, win_state)

    for have in range(n_back + 1):
        cond = (i == have) if have < n_back else (i >= have)

        @pl.when(cond)
        def _(have=have):
            flash_tiles([sel_diag] + [win_back(d) for d in range(have, 0, -1)] + [win_diag])

    flash_mix(1, sel_state)
    flash_mix(2, win_state)
    for h2 in range(N_HEADS // 2):
        pair = jnp.concatenate([mix_ref[2 * h2], mix_ref[2 * h2 + 1]], axis=0)
        o_ref[:, h2 * SLOT:(h2 + 1) * SLOT] = pair.T.astype(BF16)


def _overlap(n_cmp_pad, n_slc):
    c0 = jnp.arange(n_cmp_pad) * CMP_STRIDE
    s0 = jnp.arange(n_slc) * SEL_LEN
    ov = jnp.minimum(c0[None, :] + CMP_LEN, s0[:, None] + SEL_LEN) - jnp.maximum(c0[None, :], s0[:, None])
    return jnp.clip(ov, 0).astype(F32) / CMP_LEN


def _nsa(q, gate, kvc, ks, vst, kw, vwt, batch, seq_len):
    n_slc = seq_len // SEL_LEN
    n_cmp_pad = kvc.shape[0] // batch
    w_kv_pad = N_KV * SLOT
    kvc = kvc.reshape(batch, n_cmp_pad, 2 * w_kv_pad)
    assert n_slc <= SLOT - BLK_LANE0 and n_slc % 8 == 0
    assert CHAIN_COLS % HALF == 0 and (GROUP * HALF) % CHAIN_COLS == 0 and KT == 2 * HALF
    ov = _overlap(n_cmp_pad, n_slc)
    nq = seq_len // TQ
    n_kt = seq_len // KT
    qrow = lambda b, i: (b * nq + i, 0)
    per_b = lambda b, i: (b, 0, 0)
    fixed = lambda b, i: (0, 0)
    blocks = (_nbytes((TQ, Q_WIDTH), BF16) + _nbytes((TQ, SLOT), F32)
              + 2 * _nbytes((n_cmp_pad, w_kv_pad), BF16) + 2 * _nbytes((seq_len, KV_WIDTH), BF16)
              + 2 * _nbytes((n_kt, VT_SLAB, KT), BF16)
              + _nbytes(ov.shape, F32) + _nbytes((TQ, Q_WIDTH), BF16))
    n_chains = N_KV * GROUP * TQ // CHAIN_COLS
    run_max = [pltpu.VMEM((1, CHAIN_COLS), F32)] * n_chains
    run_acc = [pltpu.VMEM((VT_ROWS, CHAIN_COLS), F32)] * n_chains
    scratch_shapes = (
        [pltpu.VMEM((seq_len, w_kv_pad), BF16),
         pltpu.VMEM((seq_len, w_kv_pad), BF16),
         pltpu.VMEM((N_KV, SLOT, GROUP * TQ), BF16),
         pltpu.VMEM((N_HEADS, HEAD_DIM, TQ), F32)]
        + run_max + run_acc + run_max + run_acc)
    scratch = sum(_nbytes(s.shape, s.dtype) for s in scratch_shapes)
    return pl.pallas_call(
        _nsa_kernel,
        grid=(batch, nq),
        in_specs=[
            pl.BlockSpec((TQ, Q_WIDTH), qrow),
            pl.BlockSpec((TQ, SLOT), qrow),
            pl.BlockSpec((1, n_cmp_pad, w_kv_pad), per_b),
            pl.BlockSpec((1, n_cmp_pad, w_kv_pad), lambda b, i: (b, 0, 1)),
            pl.BlockSpec((1, seq_len, KV_WIDTH), per_b),
            pl.BlockSpec((n_kt, VT_SLAB, KT), per_b),
            pl.BlockSpec((1, seq_len, KV_WIDTH), per_b),
            pl.BlockSpec((n_kt, VT_SLAB, KT), per_b),
            pl.BlockSpec(ov.shape, fixed),
        ],
        out_specs=pl.BlockSpec((TQ, Q_WIDTH), qrow),
        out_shape=jax.ShapeDtypeStruct((batch * seq_len, Q_WIDTH), BF16),
        scratch_shapes=scratch_shapes,
        compiler_params=pltpu.CompilerParams(
            dimension_semantics=("arbitrary", "arbitrary"),
            vmem_limit_bytes=_vmem_limit(blocks, scratch)),
        name="nsa",
    )(q, gate, kvc, kvc, ks.reshape(batch, seq_len, KV_WIDTH), vst,
      kw.reshape(batch, seq_len, KV_WIDTH), vwt, ov)


def _pool_kernel(u_ref, w_ref, scale_ref, o_hbm, o_buf, sem):
    b = pl.program_id(0)
    seq = u_ref.shape[0]
    n_steps = o_hbm.shape[0] // seq
    o_ref = o_buf.at[b % 2]

    def store(step):
        rows = pl.ds(pl.multiple_of(step * seq, seq), seq)
        return pltpu.make_async_copy(o_buf.at[step % 2], o_hbm.at[rows], sem.at[step % 2])

    @pl.when(b >= 2)
    def _():
        store(b - 2).wait()

    t = lax.broadcasted_iota(jnp.int32, (seq, POOL_GROUP), 0)

    def shifted(x, k):
        return jnp.where(t >= k, pltpu.roll(x, k, axis=0), 0.0)

    for gi, w in enumerate(POOL_WINDOWS):
        sl = slice(gi * POOL_GROUP, (gi + 1) * POOL_GROUP)
        x = u_ref[:, sl]
        wsum = x
        span = 1
        while span < w:
            wsum = wsum + shifted(wsum, span)
            span *= 2
        cnt = jnp.minimum(t + 1, w).astype(F32)
        pooled = (wsum / cnt - x).astype(BF16)
        o_ref[:, sl] = (_dot(pooled, w_ref[gi]) * scale_ref[:, sl]).astype(BF16)

    store(b).start()

    @pl.when(b == n_steps - 1)
    def _():
        if n_steps > 1:
            store(b - 1).wait()
        store(b).wait()


def _pool(pool_in, pool_w, pool_scale, batch, seq_len):
    for w in POOL_WINDOWS:
        assert w & (w - 1) == 0
    blocks = _nbytes((seq_len, POOL_WIDTH), F32) + _nbytes(pool_w.shape, BF16)
    o_slots = (2, seq_len, POOL_WIDTH)
    return pl.pallas_call(
        _pool_kernel,
        grid=(batch,),
        in_specs=[
            pl.BlockSpec((seq_len, POOL_WIDTH), lambda b: (b, 0)),
            pl.BlockSpec(pool_w.shape, lambda b: (0, 0, 0)),
            pl.BlockSpec((1, POOL_WIDTH), lambda b: (0, 0)),
        ],
        out_specs=pl.BlockSpec(memory_space=pltpu.HBM),
        out_shape=jax.ShapeDtypeStruct((batch * seq_len, POOL_WIDTH), BF16),
        scratch_shapes=[pltpu.VMEM(o_slots, BF16), pltpu.SemaphoreType.DMA((2,))],
        compiler_params=pltpu.CompilerParams(
            dimension_semantics=("arbitrary",),
            vmem_limit_bytes=_vmem_limit(blocks, _nbytes(o_slots, BF16))),
        name="pool",
    )(pool_in, pool_w.astype(BF16), pool_scale.reshape(1, -1))


def _merge_kernel(x_ref, gpre_ref, wt_ref, nsa_ref, wa_ref, pool_ref, wp_ref, wo_ref, gpost_ref,
                  o_ref):
    rows = x_ref.shape[0] // MERGE_SPLIT
    for k in range(MERGE_SPLIT):
        rs = slice(k * rows, (k + 1) * rows)
        x = x_ref[rs, :]
        hn = _rms(x, gpre_ref[...])
        g_attn = jax.nn.sigmoid(_dot_nt(hn, wt_ref[0:D_MODEL, :]))
        g_pool = jax.nn.sigmoid(_dot_nt(hn, wt_ref[D_MODEL:2 * D_MODEL, :]))
        y = (g_attn * _dot(nsa_ref[rs, :].astype(F32), wa_ref[...])
             + g_pool * _dot(pool_ref[rs, :].astype(F32), wp_ref[...]))
        h = _dot(y, wo_ref[...])
        o_ref[rs, :] = x + _rms(h, gpost_ref[...])


def _merge(x1, g_pre, w_in_t, o_nsa, w_attn, o_pool, w_pool, w_out, g_post):
    t = x1.shape[0]
    row = lambda i: (i, 0)
    fixed = lambda i: (0, 0)
    once = pl.Buffered(1)
    tile = MERGE_SPLIT * TOK_TILE
    weights = (_nbytes((2 * D_MODEL, D_MODEL), F32) + _nbytes((Q_WIDTH, D_MODEL), F32)
               + _nbytes((POOL_WIDTH, D_MODEL), F32) + _nbytes((D_MODEL, D_MODEL), F32))
    blocks = (2 * _nbytes((tile, D_MODEL), F32) + _nbytes((tile, Q_WIDTH), BF16)
              + _nbytes((tile, POOL_WIDTH), BF16))
    gate_rows = pl.BlockSpec((pl.Element(2 * D_MODEL), pl.Element(D_MODEL)), lambda i: (OFF_MERGE, 0),
                             pipeline_mode=once)
    return pl.pallas_call(
        _merge_kernel,
        grid=(t // tile,),
        in_specs=[
            pl.BlockSpec((tile, D_MODEL), row),
            pl.BlockSpec((1, D_MODEL), fixed),
            gate_rows,
            pl.BlockSpec((tile, Q_WIDTH), row),
            pl.BlockSpec((Q_WIDTH, D_MODEL), fixed, pipeline_mode=once),
            pl.BlockSpec((tile, POOL_WIDTH), row),
            pl.BlockSpec((POOL_WIDTH, D_MODEL), fixed, pipeline_mode=once),
            pl.BlockSpec((D_MODEL, D_MODEL), fixed, pipeline_mode=once),
            pl.BlockSpec((1, D_MODEL), fixed),
        ],
        out_specs=pl.BlockSpec((tile, D_MODEL), row),
        out_shape=jax.ShapeDtypeStruct((t, D_MODEL), F32),
        compiler_params=pltpu.CompilerParams(
            dimension_semantics=("arbitrary",), vmem_limit_bytes=_vmem_limit(blocks, weights)),
        name="merge",
    )(x1, g_pre.reshape(1, -1), w_in_t, o_nsa, w_attn, o_pool, w_pool, w_out, g_post.reshape(1, -1))


def _layer(x, tab, tab_cmp, batch, seq_len, g_ffn1_pre, w_ffn1_gate, w_ffn1_up, w_ffn1_down,
           g_ffn1_post, g_mix_pre, w_in, cmp_pe_k, cmp_w1_k, cmp_w2_k, cmp_pe_v, cmp_w1_v, cmp_w2_v,
           w_attn_branch, pool_w, pool_scale, w_pool_branch, w_out, g_mix_post, g_ffn2_pre,
           w_ffn2_gate, w_ffn2_up, w_ffn2_down, g_ffn2_post):
    x1 = _ffn(x, g_ffn1_pre, w_ffn1_gate, w_ffn1_up, w_ffn1_down, g_ffn1_post)
    w_in_t = w_in.T
    q, ks, kw, vst, vwt, cmp_rows, gate, pool_in = _in_proj(x1, g_mix_pre, w_in_t, tab)
    kvc = _compress(cmp_rows, cmp_pe_k, cmp_w1_k, cmp_w2_k, cmp_pe_v, cmp_w1_v, cmp_w2_v, tab_cmp)
    o_nsa = _nsa(q, gate, kvc, ks, vst, kw, vwt, batch, seq_len)
    o_pool = _pool(pool_in, pool_w, pool_scale, batch, seq_len)
    x2 = _merge(x1, g_mix_pre, w_in_t, o_nsa, w_attn_branch, o_pool, w_pool_branch, w_out,
                g_mix_post)
    return _ffn(x2, g_ffn2_pre, w_ffn2_gate, w_ffn2_up, w_ffn2_down, g_ffn2_post)


def kernel(x, positions, g_ffn1_pre, w_ffn1_gate, w_ffn1_up, w_ffn1_down, g_ffn1_post, g_mix_pre, w_in, cmp_pe_k, cmp_w1_k, cmp_w2_k, cmp_pe_v, cmp_w1_v, cmp_w2_v, w_attn_branch, pool_w, pool_scale, w_pool_branch, w_out, g_mix_post, g_ffn2_pre, w_ffn2_gate, w_ffn2_up, w_ffn2_down, g_ffn2_post):
    batch, seq_len, d_model = x.shape
    assert d_model == D_MODEL and seq_len % TOK_TILE == 0 and seq_len % TQ == 0
    assert TQ == KT and TQ % SEL_LEN == 0 and TOK_TILE % KT == 0 and WINDOW % KT == 0
    assert seq_len % CMP_STRIDE == 0 and CMP_LEN == 2 * CMP_STRIDE
    t = batch * seq_len
    tab = _rope_tables(positions.reshape(t), seq_len)
    n_cmp = (seq_len - CMP_LEN) // CMP_STRIDE + 1
    n_cmp_pad = seq_len // CMP_STRIDE
    pos_cmp = jnp.pad(positions[:, CMP_LEN - 1::CMP_STRIDE], ((0, 0), (0, n_cmp_pad - n_cmp)))
    tab_cmp = _rope_tables(pos_cmp.reshape(batch * n_cmp_pad), batch * n_cmp_pad)
    xf = x.reshape(t, D_MODEL)
    per_layer = (g_ffn1_pre, w_ffn1_gate, w_ffn1_up, w_ffn1_down, g_ffn1_post, g_mix_pre, w_in,
                 cmp_pe_k, cmp_w1_k, cmp_w2_k, cmp_pe_v, cmp_w1_v, cmp_w2_v, w_attn_branch, pool_w,
                 pool_scale, w_pool_branch, w_out, g_mix_post, g_ffn2_pre, w_ffn2_gate, w_ffn2_up,
                 w_ffn2_down, g_ffn2_post)
    for l in range(g_ffn1_pre.shape[0]):
        xf = _layer(xf, tab, tab_cmp, batch, seq_len, *(p[l] for p in per_layer))
    return xf.reshape(batch, seq_len, D_MODEL)
```

```python
import functools
import math

import jax
import jax.numpy as jnp
from jax import lax
from jax.experimental import pallas as pl
from jax.experimental.pallas import tpu as pltpu

F32 = jnp.float32
BF16 = jnp.bfloat16

D_MODEL = 1024
N_HEADS = 16
HEAD_DIM = 64
N_KV = 4
GROUP = N_HEADS // N_KV
ROT_DIM = HEAD_DIM // 4
ROT_HALF = ROT_DIM // 2
ROPE_THETA = 500000.0
CMP_LEN = 32
CMP_STRIDE = 16
CMP_HIDDEN = 2 * HEAD_DIM
SEL_LEN = 64
SEL_SHIFT = SEL_LEN.bit_length() - 1
N_SEL = 8
WINDOW = 512
POOL_WINDOWS = (2, 4, 8, 16)
POOL_WIDTH = D_MODEL // 2
POOL_GROUP = POOL_WIDTH // len(POOL_WINDOWS)
D_FF = 2816
EPS = 1e-6
NEG_INF = -1e30
FORCE_SCORE = 1e4
Q_WIDTH = N_HEADS * HEAD_DIM
KV_WIDTH = N_KV * HEAD_DIM
BRANCH_WIDTH = Q_WIDTH + POOL_WIDTH
POOL_COL_BLOCK = Q_WIDTH // POOL_WIDTH
assert POOL_COL_BLOCK * POOL_WIDTH == Q_WIDTH
N_GATES = 3 * N_HEADS
LOG2_E = math.log2(math.e)

LANES = 128
BF16_SUBLANES = 16
V7X_VMEM_BYTES = 64 * 1024 * 1024
VMEM_COMPILER_RESERVE = 4 * 1024 * 1024
VMEM_MIN_REQUEST = 32 * 1024 * 1024

SLOT = LANES
BLK_LANE0 = HEAD_DIM
VT_ROWS = HEAD_DIM + BF16_SUBLANES
ONE_ROW = HEAD_DIM
TOK_TILE = 512
TQ = 256
KT = 256
HALF = TQ // 2
CHAIN_COLS = 256
PIPE_DEPTH = 10
FF_CHUNK = 256
IN_SPLIT = 2
MERGE_SPLIT = 2
N_ROPE_TAB = 2


def _vmem_limit(block_bytes, scratch_bytes=0):
    need = 2 * block_bytes + scratch_bytes
    return int(min(V7X_VMEM_BYTES - VMEM_COMPILER_RESERVE, max(2 * need, VMEM_MIN_REQUEST)))


def _nbytes(shape, dtype):
    n = 1
    for s in shape:
        n *= s
    return n * jnp.dtype(dtype).itemsize


def _rms(xf, g):
    return xf * lax.rsqrt(jnp.mean(xf * xf, axis=-1, keepdims=True) + EPS) * g


def _dot(a, b):
    return jnp.dot(a, b, preferred_element_type=F32)


def _dot_nt(a, b):
    return lax.dot_general(a, b, (((1,), (1,)), ((), ())), preferred_element_type=F32)


def _rope_trig_kernel(pos_ref, inv_ref, tab_ref):
    tn = pos_ref.shape[1]
    ang = pos_ref[...].astype(F32) * inv_ref[...]
    c, s = jnp.cos(ang), jnp.sin(ang)
    rest = HEAD_DIM - ROT_DIM
    reps = SLOT // HEAD_DIM
    cos_rows = jnp.concatenate([c, c, jnp.ones((rest, tn), F32)] * reps, axis=0)
    sin_rows = jnp.concatenate([-s, s, jnp.zeros((rest, tn), F32)] * reps, axis=0)
    for j in range(tn // LANES):
        cs = slice(j * LANES, (j + 1) * LANES)
        tab_ref[cs, 0:SLOT] = cos_rows[:, cs].T
        tab_ref[cs, SLOT:2 * SLOT] = sin_rows[:, cs].T


def _rope_tables(pos_flat, tile):
    n = pos_flat.shape[0]
    inv = ROPE_THETA ** (-jnp.arange(ROT_HALF, dtype=F32) * (2.0 / ROT_DIM))
    return pl.pallas_call(
        _rope_trig_kernel,
        grid=(n // tile,),
        in_specs=[pl.BlockSpec((1, tile), lambda i: (0, i)),
                  pl.BlockSpec((ROT_HALF, 1), lambda i: (0, 0))],
        out_specs=pl.BlockSpec((tile, N_ROPE_TAB * SLOT), lambda i: (i, 0)),
        out_shape=jax.ShapeDtypeStruct((n, N_ROPE_TAB * SLOT), F32),
        compiler_params=pltpu.CompilerParams(dimension_semantics=("arbitrary",)),
        name="rope_trig",
    )(pos_flat.reshape(1, n), inv.reshape(ROT_HALF, 1))


def _rope_slot(y, tab):
    cos_t = tab[:, 0:SLOT]
    sin_t = tab[:, SLOT:2 * SLOT]
    lane = lax.broadcasted_iota(jnp.int32, y.shape, 1) & (HEAD_DIM - 1)
    up = pltpu.roll(y, SLOT - ROT_HALF, axis=1)
    down = pltpu.roll(y, ROT_HALF, axis=1)
    return y * cos_t + jnp.where(lane < ROT_HALF, up, down) * sin_t


def _ffn_kernel(x_ref, gpre_ref, wg_hbm, wu_hbm, wd_hbm, gpost_ref, o_ref, wg_ref, wu_ref, wd_ref, sem):
    n_chunks = D_FF // FF_CHUNK

    def chunk_copies(c):
        sl = slice(c * FF_CHUNK, (c + 1) * FF_CHUNK)
        return (pltpu.make_async_copy(wg_hbm.at[:, sl], wg_ref.at[:, sl], sem.at[0, c]),
                pltpu.make_async_copy(wu_hbm.at[:, sl], wu_ref.at[:, sl], sem.at[1, c]),
                pltpu.make_async_copy(wd_hbm.at[sl, :], wd_ref.at[sl, :], sem.at[2, c]))

    def body(first_step):
        x = x_ref[...]
        hn = _rms(x, gpre_ref[...])
        for c in range(n_chunks):
            sl = slice(c * FF_CHUNK, (c + 1) * FF_CHUNK)
            if first_step:
                for copy in chunk_copies(c):
                    copy.wait()
            g = _dot(hn, wg_ref[:, sl])
            u = _dot(hn, wu_ref[:, sl])
            d = _dot(g * jax.nn.sigmoid(g) * u, wd_ref[sl, :])
            if c == 0:
                o_ref[...] = d
            else:
                o_ref[...] += d
        o_ref[...] = x + 0.5 * _rms(o_ref[...], gpost_ref[...])

    @pl.when(pl.program_id(0) == 0)
    def _():
        for c in range(n_chunks):
            for copy in chunk_copies(c):
                copy.start()
        body(True)

    @pl.when(pl.program_id(0) > 0)
    def _():
        body(False)


def _ffn(x, g_pre, w_gate, w_up, w_down, g_post):
    t = x.shape[0]
    row = lambda i: (i, 0)
    fixed = lambda i: (0, 0)
    in_hbm = pl.BlockSpec(memory_space=pl.ANY)
    weights = 3 * _nbytes((D_MODEL, D_FF), F32)
    blocks = 2 * _nbytes((TOK_TILE, D_MODEL), F32)
    return pl.pallas_call(
        _ffn_kernel,
        grid=(t // TOK_TILE,),
        in_specs=[
            pl.BlockSpec((TOK_TILE, D_MODEL), row),
            pl.BlockSpec((1, D_MODEL), fixed),
            in_hbm, in_hbm, in_hbm,
            pl.BlockSpec((1, D_MODEL), fixed),
        ],
        out_specs=pl.BlockSpec((TOK_TILE, D_MODEL), row),
        out_shape=jax.ShapeDtypeStruct((t, D_MODEL), F32),
        scratch_shapes=[pltpu.VMEM((D_MODEL, D_FF), F32), pltpu.VMEM((D_MODEL, D_FF), F32),
                        pltpu.VMEM((D_FF, D_MODEL), F32),
                        pltpu.SemaphoreType.DMA((3, D_FF // FF_CHUNK))],
        compiler_params=pltpu.CompilerParams(
            dimension_semantics=("arbitrary",), vmem_limit_bytes=_vmem_limit(blocks, weights)),
        name="ffn",
    )(x, g_pre.reshape(1, -1), w_gate, w_up, w_down, g_post.reshape(1, -1))


OFF_Q = 0
OFF_CMP = OFF_Q + Q_WIDTH
OFF_KS = OFF_CMP + 2 * KV_WIDTH
OFF_VS = OFF_KS + KV_WIDTH
OFF_KW = OFF_VS + KV_WIDTH
OFF_VW = OFF_KW + KV_WIDTH
OFF_GATE = OFF_VW + KV_WIDTH
OFF_POOL = OFF_GATE + N_GATES
OFF_MERGE = OFF_POOL + POOL_WIDTH
W_CMP = 2 * KV_WIDTH
CHUNK_TOK = CMP_STRIDE
VT_SLAB = N_KV * VT_ROWS


def _in_proj_kernel(x_ref, g_ref, w_ref, tab_ref, q_ref, ks_ref, kw_ref, vst_ref,
                    vwt_ref, cmp_ref, gate_ref, pool_ref, cmp_scr):
    rows = x_ref.shape[0] // IN_SPLIT
    wide = 4 * SLOT
    aux = (lax.broadcasted_iota(jnp.int32, (VT_ROWS - HEAD_DIM, KT), 0) == 0).astype(BF16)

    def part(k):
        rs = slice(k * rows, (k + 1) * rows)
        hn = _rms(x_ref[rs, :], g_ref[...])
        tab = tab_ref[rs, :]
        tab_q = tab * (HEAD_DIM ** -0.5 * LOG2_E)
        for c in range(Q_WIDTH // wide):
            y = _dot_nt(hn, w_ref[OFF_Q + c * wide:OFF_Q + (c + 1) * wide, :])
            for j in range(wide // SLOT):
                sl = slice(c * wide + j * SLOT, c * wide + (j + 1) * SLOT)
                q_ref[rs, sl] = _rope_slot(y[:, j * SLOT:(j + 1) * SLOT], tab_q).astype(BF16)
        for k_ref, off in ((ks_ref, OFF_KS), (kw_ref, OFF_KW)):
            y = _dot_nt(hn, w_ref[off:off + KV_WIDTH, :])
            for j in range(KV_WIDTH // SLOT):
                sl = slice(j * SLOT, (j + 1) * SLOT)
                k_ref[rs, sl] = _rope_slot(y[:, sl], tab).astype(BF16)
        y = _dot_nt(hn, w_ref[OFF_CMP:OFF_CMP + W_CMP, :])
        for j in range(W_CMP // LANES):
            cmp_scr[j, rs, :] = y[:, j * LANES:(j + 1) * LANES]
        chunk_rows = rows // CHUNK_TOK
        low = lax.broadcasted_iota(jnp.int32, (chunk_rows, LANES), 1) < HEAD_DIM
        for j in range(W_CMP // LANES):
            for pair in range(CHUNK_TOK // 2):
                tok = [cmp_scr[j, pl.ds(k * rows + 2 * pair + e, chunk_rows, stride=CHUNK_TOK), :]
                       for e in range(2)]
                both = (jnp.where(low, tok[0], pltpu.roll(tok[1], HEAD_DIM, axis=1)),
                        jnp.where(low, pltpu.roll(tok[0], HEAD_DIM, axis=1), tok[1]))
                for e in range(2):
                    c0 = (2 * j + e) * CHUNK_TOK * HEAD_DIM + pair * LANES
                    cmp_ref[k * chunk_rows:(k + 1) * chunk_rows, c0:c0 + LANES] = both[e].astype(BF16)
        gate_ref[rs, :] = _dot_nt(hn, w_ref[OFF_GATE:OFF_GATE + SLOT, :])
        pool_ref[rs, :] = _dot_nt(hn, w_ref[OFF_POOL:OFF_MERGE, :])
        tiles = rows // KT
        for vt_ref, off in ((vst_ref, OFF_VS), (vwt_ref, OFF_VW)):
            vt = _dot_nt(w_ref[off:off + KV_WIDTH, :], hn).astype(BF16)
            for j in range(tiles):
                for g in range(N_KV):
                    vt_ref[k * tiles + j, g * VT_ROWS:g * VT_ROWS + HEAD_DIM, :] = (
                        vt[g * HEAD_DIM:(g + 1) * HEAD_DIM, j * KT:(j + 1) * KT])
                    vt_ref[k * tiles + j, g * VT_ROWS + HEAD_DIM:(g + 1) * VT_ROWS, :] = aux

    for k in range(IN_SPLIT):
        part(k)


def _in_proj(x1, g_pre, w_in_t, tab):
    t = x1.shape[0]
    row = lambda i: (i, 0)
    fixed = lambda i: (0, 0)
    once = pl.Buffered(1)
    tile = IN_SPLIT * TOK_TILE
    tiles = tile // KT
    chunk_rows = tile // CHUNK_TOK
    slab = pl.BlockSpec((tiles, VT_SLAB, KT), lambda i: (i, 0, 0))
    slab_shape = jax.ShapeDtypeStruct((t // KT, VT_SLAB, KT), BF16)
    out_specs = [
        pl.BlockSpec((tile, Q_WIDTH), row), pl.BlockSpec((tile, KV_WIDTH), row),
        pl.BlockSpec((tile, KV_WIDTH), row), slab, slab,
        pl.BlockSpec((chunk_rows, CHUNK_TOK * W_CMP), row),
        pl.BlockSpec((tile, SLOT), row), pl.BlockSpec((tile, POOL_WIDTH), row)]
    out_shape = [
        jax.ShapeDtypeStruct((t, Q_WIDTH), BF16), jax.ShapeDtypeStruct((t, KV_WIDTH), BF16),
        jax.ShapeDtypeStruct((t, KV_WIDTH), BF16), slab_shape, slab_shape,
        jax.ShapeDtypeStruct((t // CHUNK_TOK, CHUNK_TOK * W_CMP), BF16),
        jax.ShapeDtypeStruct((t, SLOT), F32), jax.ShapeDtypeStruct((t, POOL_WIDTH), F32)]
    weights = _nbytes((OFF_MERGE, D_MODEL), F32)
    blocks = (_nbytes((tile, D_MODEL), F32) + _nbytes((tile, N_ROPE_TAB * SLOT), F32)
              + _nbytes((tile, Q_WIDTH + 2 * KV_WIDTH + W_CMP), BF16)
              + 2 * _nbytes((tiles, VT_SLAB, KT), BF16) + _nbytes((tile, SLOT + POOL_WIDTH), F32))
    res = pl.pallas_call(
        _in_proj_kernel,
        grid=(t // tile,),
        in_specs=[
            pl.BlockSpec((tile, D_MODEL), row),
            pl.BlockSpec((1, D_MODEL), fixed),
            pl.BlockSpec((pl.Element(OFF_MERGE), pl.Element(D_MODEL)), fixed, pipeline_mode=once),
            pl.BlockSpec((tile, N_ROPE_TAB * SLOT), row),
        ],
        out_specs=out_specs,
        out_shape=out_shape,
        scratch_shapes=[pltpu.VMEM((W_CMP // LANES, tile, LANES), F32)],
        compiler_params=pltpu.CompilerParams(
            dimension_semantics=("arbitrary",),
            vmem_limit_bytes=_vmem_limit(blocks, weights + _nbytes((tile, W_CMP), F32))),
        name="in_proj",
    )(x1, g_pre.reshape(1, -1), w_in_t, tab)
    return res


def _compress_kernel(x_ref, w1_ref, pe_ref, w1_raw_ref, w2_ref, tab_ref, o_ref):
    rows = x_ref.shape[0]
    acc = _dot(x_ref[...], w1_ref[0])
    bias = _dot(pe_ref[0], w1_raw_ref[0])[0:1, :]
    second = pltpu.roll(acc[:, CMP_HIDDEN:2 * CMP_HIDDEN], rows - 1, axis=0)
    hid = jax.nn.gelu(acc[:, 0:CMP_HIDDEN] + second + bias).astype(BF16)
    out = _dot(hid, w2_ref[0])
    is_key = pl.program_id(0) < N_KV
    o_ref[...] = jnp.where(is_key, _rope_slot(out, tab_ref[...]), out).astype(BF16)


def _compress(cmp_rows, cmp_pe_k, cmp_w1_k, cmp_w2_k, cmp_pe_v, cmp_w1_v, cmp_w2_v, tab_cmp):
    rows = cmp_rows.shape[0]
    chunk_feat = CHUNK_TOK * HEAD_DIM
    sub = 8

    def stacked(f, k, v):
        return jnp.stack([f(k), f(v)]).astype(BF16)

    w1 = stacked(lambda w: jnp.concatenate([w[:chunk_feat], w[chunk_feat:]], axis=1), cmp_w1_k, cmp_w1_v)
    w1_raw = stacked(lambda w: w, cmp_w1_k, cmp_w1_v)
    pe = stacked(lambda p: jnp.broadcast_to(p.reshape(1, -1), (sub, CMP_LEN * HEAD_DIM)), cmp_pe_k, cmp_pe_v)
    w2 = stacked(lambda w: jnp.pad(w, ((0, 0), (0, SLOT - HEAD_DIM))), cmp_w2_k, cmp_w2_v)
    per_tensor = lambda s: (s // N_KV, 0, 0)
    blocks = (_nbytes((rows, chunk_feat), BF16) + _nbytes(w1.shape[1:], BF16) + _nbytes(pe.shape[1:], BF16)
              + _nbytes(w1_raw.shape[1:], BF16) + _nbytes(w2.shape[1:], BF16)
              + _nbytes((rows, N_ROPE_TAB * SLOT), F32) + _nbytes((rows, SLOT), BF16))
    return pl.pallas_call(
        _compress_kernel,
        grid=(2 * N_KV,),
        in_specs=[
            pl.BlockSpec((rows, chunk_feat), lambda s: (0, s)),
            pl.BlockSpec((1,) + w1.shape[1:], per_tensor),
            pl.BlockSpec((1,) + pe.shape[1:], per_tensor),
            pl.BlockSpec((1,) + w1_raw.shape[1:], per_tensor),
            pl.BlockSpec((1,) + w2.shape[1:], per_tensor),
            pl.BlockSpec((rows, N_ROPE_TAB * SLOT), lambda s: (0, 0)),
        ],
        out_specs=pl.BlockSpec((rows, SLOT), lambda s: (0, s)),
        out_shape=jax.ShapeDtypeStruct((rows, 2 * N_KV * SLOT), BF16),
        compiler_params=pltpu.CompilerParams(
            dimension_semantics=("arbitrary",), vmem_limit_bytes=_vmem_limit(blocks)),
        name="compress",
    )(cmp_rows, w1, pe, w1_raw, w2, tab_cmp)


def _nsa_kernel(q_ref, gate_ref, kc_ref, vc_ref, ks_ref, vst_ref, kw_ref, vwt_ref, ov_ref,
                o_ref, ksp_ref, kwp_ref, qg_ref, mix_ref, *state_refs):
    i = pl.program_id(1)
    seq = ks_ref.shape[1]
    cols_g = GROUP * TQ
    chains_g = cols_g // CHAIN_COLS
    n_chains = N_KV * chains_g
    n_cmp = kc_ref.shape[1]
    sel_state = (state_refs[0:n_chains], state_refs[n_chains:2 * n_chains])
    win_state = (state_refs[2 * n_chains:3 * n_chains], state_refs[3 * n_chains:4 * n_chains])
    lane_q = lax.broadcasted_iota(jnp.int32, (TQ, SLOT), 1)

    def head_slot(x_ref, rows, j):
        two = x_ref[rows, (j // 2) * SLOT:(j // 2 + 1) * SLOT]
        if j % 2:
            words = pltpu.roll(pltpu.bitcast(two, jnp.uint32), HEAD_DIM, axis=1)
            two = pltpu.bitcast(words, BF16)
        return two

    @pl.when(i == 0)
    def _():
        for c in range(seq // KT):
            rows = pl.ds(c * KT, KT)
            lane = lax.broadcasted_iota(jnp.int32, (KT, SLOT), 1)
            key_blk = (c * KT + lax.broadcasted_iota(jnp.int32, (KT, SLOT), 0)) >> SEL_SHIFT
            onehot = jnp.where(lane - BLK_LANE0 == key_blk, 1.0, 0.0).astype(BF16)
            zeros = jnp.zeros((KT, SLOT), BF16)
            for g in range(N_KV):
                gs = slice(g * SLOT, (g + 1) * SLOT)
                ksp_ref[rows, gs] = jnp.where(lane < HEAD_DIM, head_slot(ks_ref.at[0], rows, g), onehot)
                kwp_ref[rows, gs] = jnp.where(lane < HEAD_DIM, head_slot(kw_ref.at[0], rows, g), zeros)

    def q_cols(r, x):
        return slice((x * GROUP + r) * HALF, (x * GROUP + r + 1) * HALF)

    def chain_half(c):
        return c * CHAIN_COLS // (GROUP * HALF)

    for h in range(N_HEADS):
        g, r = divmod(h, GROUP)
        q_pad = jnp.where(lane_q < HEAD_DIM, head_slot(q_ref, slice(None), h), jnp.zeros((TQ, SLOT), BF16))
        q_pad_t = q_pad.astype(F32).T.astype(BF16)
        for x in range(2):
            qg_ref[g, :, q_cols(r, x)] = q_pad_t[:, x * HALF:(x + 1) * HALF]

    gates_t = jax.nn.sigmoid(gate_ref[...]).T
    key_h = lax.broadcasted_iota(jnp.int32, (HALF, CHAIN_COLS), 0)
    qry_h = lax.broadcasted_iota(jnp.int32, (HALF, CHAIN_COLS), 1) & (HALF - 1)
    tri_le, tri_gt = key_h <= qry_h, key_h > qry_h
    diag_plan = ((slice(0, HALF), tri_le, slice(0, HALF)), (slice(0, KT), tri_le, slice(HALF, KT)))
    far_plan = ((slice(0, KT), tri_gt, slice(0, HALF)), (slice(HALF, KT), tri_gt, slice(0, HALF)))

    def gate_row(branch, h):
        c = branch * N_HEADS + h
        return gates_t[c:c + 1, :]

    sub = 8
    cmp_per_sub = sub * SEL_LEN // CMP_STRIDE

    def compressed_scores(ng):
        n_c = ng * cmp_per_sub
        cmp_scores = []
        for h in range(N_HEADS):
            g, r = divmod(h, GROUP)
            q_h = jnp.concatenate([qg_ref[g, :, q_cols(r, 0)], qg_ref[g, :, q_cols(r, 1)]], axis=1)
            cmp_scores.append(_dot(kc_ref[0, 0:n_c, g * SLOT:(g + 1) * SLOT], q_h))
        return cmp_scores

    def compressed_and_select(ng, start, cmp_scores):
        n_blk, n_c = ng * sub, ng * cmp_per_sub
        t_cmp = start + lax.broadcasted_iota(jnp.int32, (n_c, TQ), 1)
        n_idx = lax.broadcasted_iota(jnp.int32, (n_c, TQ), 0)
        cmp_valid = n_idx * CMP_STRIDE + (CMP_LEN - 1) <= t_cmp
        any_cmp = start + lax.broadcasted_iota(jnp.int32, (1, TQ), 1) >= CMP_LEN - 1
        t_row = start + lax.broadcasted_iota(jnp.int32, (n_blk, TQ), 1)
        blk = lax.broadcasted_iota(jnp.int32, (n_blk, TQ), 0)
        forced = (blk == t_row >> SEL_SHIFT) | (blk == 0)
        causal_blk = blk * SEL_LEN <= t_row
        idx8 = lax.broadcasted_iota(jnp.int32, (sub, TQ), 0)
        pad_c = jnp.zeros((n_cmp - n_c, TQ), F32)
        for g in range(N_KV):
            vct_g = vc_ref[0, :, g * SLOT:(g + 1) * SLOT].astype(F32).T[0:HEAD_DIM].astype(BF16)
            p_sum = jnp.zeros((n_c, TQ), F32)
            for r in range(GROUP):
                h = g * GROUP + r
                s = jnp.where(cmp_valid, cmp_scores[h], NEG_INF)
                p = jnp.exp2(s - jnp.max(s, axis=0, keepdims=True))
                inv = jnp.where(any_cmp, 1.0 / jnp.sum(p, axis=0, keepdims=True), 0.0)
                p = p * inv
                p_sum = p_sum + p
                p_all = jnp.concatenate([p, pad_c], axis=0) if ng * cmp_per_sub < n_cmp else p
                mix_ref[h] = gate_row(0, h) * _dot(vct_g, p_all.astype(BF16))
            p_sum_all = jnp.concatenate([p_sum, pad_c], axis=0) if ng * cmp_per_sub < n_cmp else p_sum
            imp = jnp.dot(ov_ref[0:n_blk, :], p_sum_all, precision=lax.Precision.HIGHEST,
                          preferred_element_type=F32)
            score = jnp.where(causal_blk, jnp.where(forced, FORCE_SCORE, imp), NEG_INF)
            rows8 = [score[k * sub:(k + 1) * sub] for k in range(ng)]
            ranks = [jnp.zeros((sub, TQ), jnp.int32) for _ in rows8]
            for c in range(n_blk):
                other = score[c:c + 1, :]
                for k, mine in enumerate(rows8):
                    if k * sub > c:
                        beats = other >= mine
                    elif (k + 1) * sub - 1 <= c:
                        beats = other > mine
                    else:
                        beats = (other > mine) | ((other == mine) & (idx8 > c - k * sub))
                    ranks[k] = ranks[k] + beats.astype(jnp.int32)
            rank = jnp.concatenate(ranks, axis=0) if ng > 1 else ranks[0]
            bias = jnp.where(causal_blk & (rank < N_SEL), 0.0, NEG_INF).astype(BF16)
            for r in range(GROUP):
                for x in range(2):
                    qg_ref[g, BLK_LANE0:BLK_LANE0 + n_blk, q_cols(r, x)] = bias[:, x * HALF:(x + 1) * HALF]

    ng_now = ((i + 1) * TQ + sub * SEL_LEN - 1) // (sub * SEL_LEN)
    for ng in range(1, seq // (sub * SEL_LEN) + 1):

        @pl.when(ng_now == ng)
        def _(ng=ng):
            compressed_and_select(ng, i * TQ, compressed_scores(ng))

    def flash_init(state):
        for m_ref, acc_ref in zip(*state):
            m_ref[...] = jnp.full(m_ref.shape, NEG_INF, F32)
            acc_ref[...] = jnp.zeros(acc_ref.shape, F32)

    def flash_tiles(tiles):
        jobs = [(tile, c) for tile in tiles for c in range(n_chains)]

        def key_plan(job):
            (_, _, _, plan, _), chain = job
            return (slice(0, KT), None, None) if plan is None else plan[chain_half(chain % chains_g)]

        scores = {}
        for step in range(len(jobs) + PIPE_DEPTH):
            if step < len(jobs):
                (kp_ref, _, kt, _, _), chain = jobs[step]
                g, c = divmod(chain, chains_g)
                keys, mask, rows = key_plan(jobs[step])
                off = pl.multiple_of(kt * KT + keys.start, HALF)
                s = _dot(kp_ref[pl.ds(off, keys.stop - keys.start), g * SLOT:(g + 1) * SLOT],
                         qg_ref[g, :, c * CHAIN_COLS:(c + 1) * CHAIN_COLS])
                if mask is not None:
                    parts = [s[0:rows.start], jnp.where(mask, s[rows], NEG_INF), s[rows.stop:]]
                    s = jnp.concatenate([p for p in parts if p.shape[0]], axis=0)
                scores[step] = s.astype(BF16)
            done = step - PIPE_DEPTH
            if done >= 0:
                (_, vt_ref, kt, _, (m_refs, acc_refs)), chain = jobs[done]
                g = chain // chains_g
                keys = key_plan(jobs[done])[0]
                m_ref, acc_ref = m_refs[chain], acc_refs[chain]
                s = scores.pop(done)
                m_old = m_ref[...]
                packed = [s[r * BF16_SUBLANES:(r + 1) * BF16_SUBLANES]
                          for r in range(s.shape[0] // BF16_SUBLANES)]
                m_tile = functools.reduce(jnp.maximum, packed).astype(F32)
                m_new = jnp.maximum(m_old, jnp.max(m_tile, axis=0, keepdims=True))
                p = jnp.exp2(s - m_new.astype(BF16))
                pv = _dot(vt_ref[kt, g * VT_ROWS:(g + 1) * VT_ROWS, keys], p)
                acc_ref[...] = jnp.exp2(m_old - m_new) * acc_ref[...] + pv
                m_ref[...] = m_new

    def flash_mix(branch, state):
        _, acc_refs = state
        for h in range(N_HEADS):
            g, r = divmod(h, GROUP)
            gate = gate_row(branch, h)
            for x in range(2):
                c, c0 = divmod(q_cols(r, x).start, CHAIN_COLS)
                acc = acc_refs[g * chains_g + c][:, c0:c0 + HALF]
                qs = slice(x * HALF, (x + 1) * HALF)
                scale = gate[:, qs] * (1.0 / acc[ONE_ROW:ONE_ROW + 1, :])
                mix_ref[h, :, qs] = mix_ref[h, :, qs] + scale * acc[0:HEAD_DIM]

    flash_init(sel_state)
    flash_init(win_state)

    def sel_past(kt):
        return (ksp_ref, vst_ref, kt, None, sel_state)

    def sel_pair(j, carry):
        flash_tiles([sel_past(2 * j), sel_past(2 * j + 1)])
        return carry

    lax.fori_loop(0, i >> 1, sel_pair, 0)

    @pl.when(i & 1 == 1)
    def _():
        flash_tiles([sel_past(i - 1)])

    n_back = WINDOW // KT
    sel_diag = (ksp_ref, vst_ref, i, diag_plan, sel_state)
    win_diag = (kwp_ref, vwt_ref, i, diag_plan, win_state)

    def win_back(d):
        return (kwp_ref, vwt_ref, i - d, far_plan if d == n_back else None, win_state)

    for have in range(n_back + 1):
        cond = (i == have) if have < n_back else (i >= have)

        @pl.when(cond)
        def _(have=have):
            flash_tiles([sel_diag] + [win_back(d) for d in range(have, 0, -1)] + [win_diag])

    flash_mix(1, sel_state)
    flash_mix(2, win_state)
    for h2 in range(N_HEADS // 2):
        pair = jnp.concatenate([mix_ref[2 * h2], mix_ref[2 * h2 + 1]], axis=0)
        o_ref[:, h2 * SLOT:(h2 + 1) * SLOT] = pair.T.astype(BF16)
    o_ref[:, Q_WIDTH:] = jnp.zeros((TQ, o_ref.shape[1] - Q_WIDTH), BF16)


def _overlap(n_cmp_pad, n_slc):
    c0 = jnp.arange(n_cmp_pad) * CMP_STRIDE
    s0 = jnp.arange(n_slc) * SEL_LEN
    ov = jnp.minimum(c0[None, :] + CMP_LEN, s0[:, None] + SEL_LEN) - jnp.maximum(c0[None, :], s0[:, None])
    return jnp.clip(ov, 0).astype(F32) / CMP_LEN


def _nsa(q, gate, kvc, ks, vst, kw, vwt, batch, seq_len):
    n_slc = seq_len // SEL_LEN
    n_cmp_pad = kvc.shape[0] // batch
    w_kv_pad = N_KV * SLOT
    kvc = kvc.reshape(batch, n_cmp_pad, 2 * w_kv_pad)
    assert n_slc <= SLOT - BLK_LANE0 and n_slc % 8 == 0
    assert CHAIN_COLS % HALF == 0 and (GROUP * HALF) % CHAIN_COLS == 0 and KT == 2 * HALF
    ov = _overlap(n_cmp_pad, n_slc)
    nq = seq_len // TQ
    n_kt = seq_len // KT
    qrow = lambda b, i: (b * nq + i, 0)
    per_b = lambda b, i: (b, 0, 0)
    fixed = lambda b, i: (0, 0)
    blocks = (_nbytes((TQ, Q_WIDTH), BF16) + _nbytes((TQ, SLOT), F32)
              + 2 * _nbytes((n_cmp_pad, w_kv_pad), BF16) + 2 * _nbytes((seq_len, KV_WIDTH), BF16)
              + 2 * _nbytes((n_kt, VT_SLAB, KT), BF16)
              + _nbytes(ov.shape, F32) + _nbytes((TQ, BRANCH_WIDTH), BF16))
    n_chains = N_KV * GROUP * TQ // CHAIN_COLS
    run_max = [pltpu.VMEM((1, CHAIN_COLS), F32)] * n_chains
    run_acc = [pltpu.VMEM((VT_ROWS, CHAIN_COLS), F32)] * n_chains
    scratch_shapes = (
        [pltpu.VMEM((seq_len, w_kv_pad), BF16),
         pltpu.VMEM((seq_len, w_kv_pad), BF16),
         pltpu.VMEM((N_KV, SLOT, GROUP * TQ), BF16),
         pltpu.VMEM((N_HEADS, HEAD_DIM, TQ), F32)]
        + run_max + run_acc + run_max + run_acc)
    scratch = sum(_nbytes(s.shape, s.dtype) for s in scratch_shapes)
    return pl.pallas_call(
        _nsa_kernel,
        grid=(batch, nq),
        in_specs=[
            pl.BlockSpec((TQ, Q_WIDTH), qrow),
            pl.BlockSpec((TQ, SLOT), qrow),
            pl.BlockSpec((1, n_cmp_pad, w_kv_pad), per_b),
            pl.BlockSpec((1, n_cmp_pad, w_kv_pad), lambda b, i: (b, 0, 1)),
            pl.BlockSpec((1, seq_len, KV_WIDTH), per_b),
            pl.BlockSpec((n_kt, VT_SLAB, KT), per_b),
            pl.BlockSpec((1, seq_len, KV_WIDTH), per_b),
            pl.BlockSpec((n_kt, VT_SLAB, KT), per_b),
            pl.BlockSpec(ov.shape, fixed),
        ],
        out_specs=pl.BlockSpec((TQ, BRANCH_WIDTH), qrow),
        out_shape=jax.ShapeDtypeStruct((batch * seq_len, BRANCH_WIDTH), BF16),
        scratch_shapes=scratch_shapes,
        compiler_params=pltpu.CompilerParams(
            dimension_semantics=("arbitrary", "arbitrary"),
            vmem_limit_bytes=_vmem_limit(blocks, scratch)),
        name="nsa",
    )(q, gate, kvc, kvc, ks.reshape(batch, seq_len, KV_WIDTH), vst,
      kw.reshape(batch, seq_len, KV_WIDTH), vwt, ov)


def _pool_kernel(u_ref, w_ref, scale_ref, _branches_hbm, o_ref):
    seq = u_ref.shape[0]
    t = lax.broadcasted_iota(jnp.int32, (seq, POOL_GROUP), 0)

    def shifted(x, k):
        return jnp.where(t >= k, pltpu.roll(x, k, axis=0), 0.0)

    for gi, w in enumerate(POOL_WINDOWS):
        sl = slice(gi * POOL_GROUP, (gi + 1) * POOL_GROUP)
        x = u_ref[:, sl]
        wsum = x
        span = 1
        while span < w:
            wsum = wsum + shifted(wsum, span)
            span *= 2
        cnt = jnp.minimum(t + 1, w).astype(F32)
        pooled = (wsum / cnt - x).astype(BF16)
        o_ref[:, sl] = (_dot(pooled, w_ref[gi]) * scale_ref[:, sl]).astype(BF16)


def _pool(pool_in, pool_w, pool_scale, branches, batch, seq_len):
    for w in POOL_WINDOWS:
        assert w & (w - 1) == 0
    blocks = (_nbytes((seq_len, POOL_WIDTH), F32) + _nbytes(pool_w.shape, BF16)
              + _nbytes((seq_len, POOL_WIDTH), BF16))
    return pl.pallas_call(
        _pool_kernel,
        grid=(batch,),
        in_specs=[
            pl.BlockSpec((seq_len, POOL_WIDTH), lambda b: (b, 0)),
            pl.BlockSpec(pool_w.shape, lambda b: (0, 0, 0)),
            pl.BlockSpec((1, POOL_WIDTH), lambda b: (0, 0)),
            pl.BlockSpec(memory_space=pl.ANY),
        ],
        out_specs=pl.BlockSpec((seq_len, POOL_WIDTH), lambda b: (b, POOL_COL_BLOCK)),
        out_shape=jax.ShapeDtypeStruct(branches.shape, branches.dtype),
        input_output_aliases={3: 0},
        compiler_params=pltpu.CompilerParams(
            dimension_semantics=("arbitrary",), vmem_limit_bytes=_vmem_limit(blocks)),
        name="pool",
    )(pool_in, pool_w.astype(BF16), pool_scale.reshape(1, -1), branches)


def _merge_kernel(x_ref, gpre_ref, wt_ref, nsa_ref, wa_ref, pool_ref, wp_ref, wo_ref, gpost_ref,
                  o_ref):
    rows = x_ref.shape[0] // MERGE_SPLIT
    for k in range(MERGE_SPLIT):
        rs = slice(k * rows, (k + 1) * rows)
        x = x_ref[rs, :]
        hn = _rms(x, gpre_ref[...])
        g_attn = jax.nn.sigmoid(_dot_nt(hn, wt_ref[0:D_MODEL, :]))
        g_pool = jax.nn.sigmoid(_dot_nt(hn, wt_ref[D_MODEL:2 * D_MODEL, :]))
        y = (g_attn * _dot(nsa_ref[rs, :].astype(F32), wa_ref[...])
             + g_pool * _dot(pool_ref[rs, :].astype(F32), wp_ref[...]))
        h = _dot(y, wo_ref[...])
        o_ref[rs, :] = x + _rms(h, gpost_ref[...])


def _merge(x1, g_pre, w_in_t, branches, w_attn, w_pool, w_out, g_post):
    t = x1.shape[0]
    row = lambda i: (i, 0)
    fixed = lambda i: (0, 0)
    once = pl.Buffered(1)
    tile = MERGE_SPLIT * TOK_TILE
    weights = (_nbytes((2 * D_MODEL, D_MODEL), F32) + _nbytes((Q_WIDTH, D_MODEL), F32)
               + _nbytes((POOL_WIDTH, D_MODEL), F32) + _nbytes((D_MODEL, D_MODEL), F32))
    blocks = (2 * _nbytes((tile, D_MODEL), F32) + _nbytes((tile, Q_WIDTH), BF16)
              + _nbytes((tile, POOL_WIDTH), BF16))
    gate_rows = pl.BlockSpec((pl.Element(2 * D_MODEL), pl.Element(D_MODEL)), lambda i: (OFF_MERGE, 0),
                             pipeline_mode=once)
    return pl.pallas_call(
        _merge_kernel,
        grid=(t // tile,),
        in_specs=[
            pl.BlockSpec((tile, D_MODEL), row),
            pl.BlockSpec((1, D_MODEL), fixed),
            gate_rows,
            pl.BlockSpec((tile, Q_WIDTH), row),
            pl.BlockSpec((Q_WIDTH, D_MODEL), fixed, pipeline_mode=once),
            pl.BlockSpec((tile, POOL_WIDTH), lambda i: (i, POOL_COL_BLOCK)),
            pl.BlockSpec((POOL_WIDTH, D_MODEL), fixed, pipeline_mode=once),
            pl.BlockSpec((D_MODEL, D_MODEL), fixed, pipeline_mode=once),
            pl.BlockSpec((1, D_MODEL), fixed),
        ],
        out_specs=pl.BlockSpec((tile, D_MODEL), row),
        out_shape=jax.ShapeDtypeStruct((t, D_MODEL), F32),
        compiler_params=pltpu.CompilerParams(
            dimension_semantics=("arbitrary",), vmem_limit_bytes=_vmem_limit(blocks, weights)),
        name="merge",
    )(x1, g_pre.reshape(1, -1), w_in_t, branches, w_attn, branches, w_pool, w_out, g_post.reshape(1, -1))


def _layer(x, tab, tab_cmp, batch, seq_len, g_ffn1_pre, w_ffn1_gate, w_ffn1_up, w_ffn1_down,
           g_ffn1_post, g_mix_pre, w_in, cmp_pe_k, cmp_w1_k, cmp_w2_k, cmp_pe_v, cmp_w1_v, cmp_w2_v,
           w_attn_branch, pool_w, pool_scale, w_pool_branch, w_out, g_mix_post, g_ffn2_pre,
           w_ffn2_gate, w_ffn2_up, w_ffn2_down, g_ffn2_post):
    x1 = _ffn(x, g_ffn1_pre, w_ffn1_gate, w_ffn1_up, w_ffn1_down, g_ffn1_post)
    w_in_t = w_in.T
    q, ks, kw, vst, vwt, cmp_rows, gate, pool_in = _in_proj(x1, g_mix_pre, w_in_t, tab)
    kvc = _compress(cmp_rows, cmp_pe_k, cmp_w1_k, cmp_w2_k, cmp_pe_v, cmp_w1_v, cmp_w2_v, tab_cmp)
    branches = _nsa(q, gate, kvc, ks, vst, kw, vwt, batch, seq_len)
    branches = _pool(pool_in, pool_w, pool_scale, branches, batch, seq_len)
    x2 = _merge(x1, g_mix_pre, w_in_t, branches, w_attn_branch, w_pool_branch, w_out, g_mix_post)
    return _ffn(x2, g_ffn2_pre, w_ffn2_gate, w_ffn2_up, w_ffn2_down, g_ffn2_post)


def kernel(x, positions, g_ffn1_pre, w_ffn1_gate, w_ffn1_up, w_ffn1_down, g_ffn1_post, g_mix_pre, w_in, cmp_pe_k, cmp_w1_k, cmp_w2_k, cmp_pe_v, cmp_w1_v, cmp_w2_v, w_attn_branch, pool_w, pool_scale, w_pool_branch, w_out, g_mix_post, g_ffn2_pre, w_ffn2_gate, w_ffn2_up, w_ffn2_down, g_ffn2_post):
    batch, seq_len, d_model = x.shape
    assert d_model == D_MODEL and seq_len % TOK_TILE == 0 and seq_len % TQ == 0
    assert TQ == KT and TQ % SEL_LEN == 0 and TOK_TILE % KT == 0 and WINDOW % KT == 0
    assert seq_len % CMP_STRIDE == 0 and CMP_LEN == 2 * CMP_STRIDE
    t = batch * seq_len
    tab = _rope_tables(positions.reshape(t), seq_len)
    n_cmp = (seq_len - CMP_LEN) // CMP_STRIDE + 1
    n_cmp_pad = seq_len // CMP_STRIDE
    pos_cmp = jnp.pad(positions[:, CMP_LEN - 1::CMP_STRIDE], ((0, 0), (0, n_cmp_pad - n_cmp)))
    tab_cmp = _rope_tables(pos_cmp.reshape(batch * n_cmp_pad), batch * n_cmp_pad)
    xf = x.reshape(t, D_MODEL)
    per_layer = (g_ffn1_pre, w_ffn1_gate, w_ffn1_up, w_ffn1_down, g_ffn1_post, g_mix_pre, w_in,
                 cmp_pe_k, cmp_w1_k, cmp_w2_k, cmp_pe_v, cmp_w1_v, cmp_w2_v, w_attn_branch, pool_w,
                 pool_scale, w_pool_branch, w_out, g_mix_post, g_ffn2_pre, w_ffn2_gate, w_ffn2_up,
                 w_ffn2_down, g_ffn2_post)
    for l in range(g_ffn1_pre.shape[0]):
        xf = _layer(xf, tab, tab_cmp, batch, seq_len, *(p[l] for p in per_layer))
    return xf.reshape(batch, seq_len, D_MODEL)
```

```python
import functools
import math

import jax
import jax.numpy as jnp
from jax import lax
from jax.experimental import pallas as pl
from jax.experimental.pallas import tpu as pltpu

F32 = jnp.float32
BF16 = jnp.bfloat16

D_MODEL = 1024
N_HEADS = 16
HEAD_DIM = 64
N_KV = 4
GROUP = N_HEADS // N_KV
ROT_DIM = HEAD_DIM // 4
ROT_HALF = ROT_DIM // 2
ROPE_THETA = 500000.0
CMP_LEN = 32
CMP_STRIDE = 16
CMP_HIDDEN = 2 * HEAD_DIM
SEL_LEN = 64
SEL_SHIFT = SEL_LEN.bit_length() - 1
N_SEL = 8
WINDOW = 512
POOL_WINDOWS = (2, 4, 8, 16)
POOL_WIDTH = D_MODEL // 2
POOL_GROUP = POOL_WIDTH // len(POOL_WINDOWS)
D_FF = 2816
EPS = 1e-6
NEG_INF = -1e30
FORCE_SCORE = 1e4
Q_WIDTH = N_HEADS * HEAD_DIM
KV_WIDTH = N_KV * HEAD_DIM
BRANCH_WIDTH = Q_WIDTH + POOL_WIDTH
POOL_COL_BLOCK = Q_WIDTH // POOL_WIDTH
assert POOL_COL_BLOCK * POOL_WIDTH == Q_WIDTH
N_GATES = 3 * N_HEADS
LOG2_E = math.log2(math.e)

LANES = 128
BF16_SUBLANES = 16
V7X_VMEM_BYTES = 64 * 1024 * 1024
VMEM_COMPILER_RESERVE = 4 * 1024 * 1024
VMEM_MIN_REQUEST = 32 * 1024 * 1024

SLOT = LANES
BLK_LANE0 = HEAD_DIM
VT_ROWS = HEAD_DIM + BF16_SUBLANES
ONE_ROW = HEAD_DIM
TOK_TILE = 512
TQ = 256
KT = 256
HALF = TQ // 2
CHAIN_COLS = 256
PIPE_DEPTH = 10
FF_CHUNK = 256
IN_SPLIT = 2
MERGE_SPLIT = 2
N_ROPE_TAB = 2


def _vmem_limit(block_bytes, scratch_bytes=0):
    need = 2 * block_bytes + scratch_bytes
    return int(min(V7X_VMEM_BYTES - VMEM_COMPILER_RESERVE, max(2 * need, VMEM_MIN_REQUEST)))


def _nbytes(shape, dtype):
    n = 1
    for s in shape:
        n *= s
    return n * jnp.dtype(dtype).itemsize


def _rms(xf, g):
    return xf * lax.rsqrt(jnp.mean(xf * xf, axis=-1, keepdims=True) + EPS) * g


def _dot(a, b):
    return jnp.dot(a, b, preferred_element_type=F32)


def _dot_nt(a, b):
    return lax.dot_general(a, b, (((1,), (1,)), ((), ())), preferred_element_type=F32)


def _rope_trig_kernel(pos_ref, inv_ref, tab_ref):
    tn = pos_ref.shape[1]
    ang = pos_ref[...].astype(F32) * inv_ref[...]
    c, s = jnp.cos(ang), jnp.sin(ang)
    rest = HEAD_DIM - ROT_DIM
    reps = SLOT // HEAD_DIM
    cos_rows = jnp.concatenate([c, c, jnp.ones((rest, tn), F32)] * reps, axis=0)
    sin_rows = jnp.concatenate([-s, s, jnp.zeros((rest, tn), F32)] * reps, axis=0)
    for j in range(tn // LANES):
        cs = slice(j * LANES, (j + 1) * LANES)
        tab_ref[cs, 0:SLOT] = cos_rows[:, cs].T
        tab_ref[cs, SLOT:2 * SLOT] = sin_rows[:, cs].T


def _rope_tables(pos_flat, tile):
    n = pos_flat.shape[0]
    inv = ROPE_THETA ** (-jnp.arange(ROT_HALF, dtype=F32) * (2.0 / ROT_DIM))
    return pl.pallas_call(
        _rope_trig_kernel,
        grid=(n // tile,),
        in_specs=[pl.BlockSpec((1, tile), lambda i: (0, i)),
                  pl.BlockSpec((ROT_HALF, 1), lambda i: (0, 0))],
        out_specs=pl.BlockSpec((tile, N_ROPE_TAB * SLOT), lambda i: (i, 0)),
        out_shape=jax.ShapeDtypeStruct((n, N_ROPE_TAB * SLOT), F32),
        compiler_params=pltpu.CompilerParams(dimension_semantics=("arbitrary",)),
        name="rope_trig",
    )(pos_flat.reshape(1, n), inv.reshape(ROT_HALF, 1))


def _rope_slot(y, tab):
    cos_t = tab[:, 0:SLOT]
    sin_t = tab[:, SLOT:2 * SLOT]
    lane = lax.broadcasted_iota(jnp.int32, y.shape, 1) & (HEAD_DIM - 1)
    up = pltpu.roll(y, SLOT - ROT_HALF, axis=1)
    down = pltpu.roll(y, ROT_HALF, axis=1)
    return y * cos_t + jnp.where(lane < ROT_HALF, up, down) * sin_t


def _ffn_kernel(x_ref, gpre_ref, wg_hbm, wu_hbm, wd_hbm, gpost_ref, o_ref, wg_ref, wu_ref, wd_ref, sem):
    n_chunks = D_FF // FF_CHUNK

    def chunk_copies(c):
        sl = slice(c * FF_CHUNK, (c + 1) * FF_CHUNK)
        return (pltpu.make_async_copy(wg_hbm.at[:, sl], wg_ref.at[:, sl], sem.at[0, c]),
                pltpu.make_async_copy(wu_hbm.at[:, sl], wu_ref.at[:, sl], sem.at[1, c]),
                pltpu.make_async_copy(wd_hbm.at[sl, :], wd_ref.at[sl, :], sem.at[2, c]))

    def body(first_step):
        x = x_ref[...]
        hn = _rms(x, gpre_ref[...])
        for c in range(n_chunks):
            sl = slice(c * FF_CHUNK, (c + 1) * FF_CHUNK)
            if first_step:
                for copy in chunk_copies(c):
                    copy.wait()
            g = _dot(hn, wg_ref[:, sl])
            u = _dot(hn, wu_ref[:, sl])
            d = _dot(g * jax.nn.sigmoid(g) * u, wd_ref[sl, :])
            if c == 0:
                o_ref[...] = d
            else:
                o_ref[...] += d
        o_ref[...] = x + 0.5 * _rms(o_ref[...], gpost_ref[...])

    @pl.when(pl.program_id(0) == 0)
    def _():
        for c in range(n_chunks):
            for copy in chunk_copies(c):
                copy.start()
        body(True)

    @pl.when(pl.program_id(0) > 0)
    def _():
        body(False)


def _ffn(x, g_pre, w_gate, w_up, w_down, g_post):
    t = x.shape[0]
    row = lambda i: (i, 0)
    fixed = lambda i: (0, 0)
    in_hbm = pl.BlockSpec(memory_space=pl.ANY)
    weights = 3 * _nbytes((D_MODEL, D_FF), F32)
    blocks = 2 * _nbytes((TOK_TILE, D_MODEL), F32)
    return pl.pallas_call(
        _ffn_kernel,
        grid=(t // TOK_TILE,),
        in_specs=[
            pl.BlockSpec((TOK_TILE, D_MODEL), row),
            pl.BlockSpec((1, D_MODEL), fixed),
            in_hbm, in_hbm, in_hbm,
            pl.BlockSpec((1, D_MODEL), fixed),
        ],
        out_specs=pl.BlockSpec((TOK_TILE, D_MODEL), row),
        out_shape=jax.ShapeDtypeStruct((t, D_MODEL), F32),
        scratch_shapes=[pltpu.VMEM((D_MODEL, D_FF), F32), pltpu.VMEM((D_MODEL, D_FF), F32),
                        pltpu.VMEM((D_FF, D_MODEL), F32),
                        pltpu.SemaphoreType.DMA((3, D_FF // FF_CHUNK))],
        compiler_params=pltpu.CompilerParams(
            dimension_semantics=("arbitrary",), vmem_limit_bytes=_vmem_limit(blocks, weights)),
        name="ffn",
    )(x, g_pre.reshape(1, -1), w_gate, w_up, w_down, g_post.reshape(1, -1))


OFF_Q = 0
OFF_CMP = OFF_Q + Q_WIDTH
OFF_KS = OFF_CMP + 2 * KV_WIDTH
OFF_VS = OFF_KS + KV_WIDTH
OFF_KW = OFF_VS + KV_WIDTH
OFF_VW = OFF_KW + KV_WIDTH
OFF_GATE = OFF_VW + KV_WIDTH
OFF_POOL = OFF_GATE + N_GATES
OFF_MERGE = OFF_POOL + POOL_WIDTH
W_CMP = 2 * KV_WIDTH
CHUNK_TOK = CMP_STRIDE
VT_SLAB = N_KV * VT_ROWS


def _in_proj_kernel(x_ref, g_ref, w_ref, tab_ref, q_ref, ks_ref, kw_ref, vst_ref,
                    vwt_ref, cmp_ref, gate_ref, pool_ref, cmp_scr):
    rows = x_ref.shape[0] // IN_SPLIT
    wide = 4 * SLOT
    aux = (lax.broadcasted_iota(jnp.int32, (VT_ROWS - HEAD_DIM, KT), 0) == 0).astype(BF16)

    def part(k):
        rs = slice(k * rows, (k + 1) * rows)
        hn = _rms(x_ref[rs, :], g_ref[...])
        tab = tab_ref[rs, :]
        tab_q = tab * (HEAD_DIM ** -0.5 * LOG2_E)
        for c in range(Q_WIDTH // wide):
            y = _dot_nt(hn, w_ref[OFF_Q + c * wide:OFF_Q + (c + 1) * wide, :])
            for j in range(wide // SLOT):
                sl = slice(c * wide + j * SLOT, c * wide + (j + 1) * SLOT)
                q_ref[rs, sl] = _rope_slot(y[:, j * SLOT:(j + 1) * SLOT], tab_q).astype(BF16)
        for k_ref, off in ((ks_ref, OFF_KS), (kw_ref, OFF_KW)):
            y = _dot_nt(hn, w_ref[off:off + KV_WIDTH, :])
            for j in range(KV_WIDTH // SLOT):
                sl = slice(j * SLOT, (j + 1) * SLOT)
                k_ref[rs, sl] = _rope_slot(y[:, sl], tab).astype(BF16)
        y = _dot_nt(hn, w_ref[OFF_CMP:OFF_CMP + W_CMP, :])
        for j in range(W_CMP // LANES):
            cmp_scr[j, rs, :] = y[:, j * LANES:(j + 1) * LANES]
        chunk_rows = rows // CHUNK_TOK
        low = lax.broadcasted_iota(jnp.int32, (chunk_rows, LANES), 1) < HEAD_DIM
        for j in range(W_CMP // LANES):
            for pair in range(CHUNK_TOK // 2):
                tok = [cmp_scr[j, pl.ds(k * rows + 2 * pair + e, chunk_rows, stride=CHUNK_TOK), :]
                       for e in range(2)]
                both = (jnp.where(low, tok[0], pltpu.roll(tok[1], HEAD_DIM, axis=1)),
                        jnp.where(low, pltpu.roll(tok[0], HEAD_DIM, axis=1), tok[1]))
                for e in range(2):
                    c0 = (2 * j + e) * CHUNK_TOK * HEAD_DIM + pair * LANES
                    cmp_ref[k * chunk_rows:(k + 1) * chunk_rows, c0:c0 + LANES] = both[e].astype(BF16)
        gate_ref[rs, :] = _dot_nt(hn, w_ref[OFF_GATE:OFF_GATE + SLOT, :])
        pool_ref[rs, :] = _dot_nt(hn, w_ref[OFF_POOL:OFF_MERGE, :])
        tiles = rows // KT
        for vt_ref, off in ((vst_ref, OFF_VS), (vwt_ref, OFF_VW)):
            vt = _dot_nt(w_ref[off:off + KV_WIDTH, :], hn).astype(BF16)
            for j in range(tiles):
                for g in range(N_KV):
                    vt_ref[k * tiles + j, g * VT_ROWS:g * VT_ROWS + HEAD_DIM, :] = (
                        vt[g * HEAD_DIM:(g + 1) * HEAD_DIM, j * KT:(j + 1) * KT])
                    vt_ref[k * tiles + j, g * VT_ROWS + HEAD_DIM:(g + 1) * VT_ROWS, :] = aux

    for k in range(IN_SPLIT):
        part(k)


def _in_proj(x1, g_pre, w_in_t, tab):
    t = x1.shape[0]
    row = lambda i: (i, 0)
    fixed = lambda i: (0, 0)
    once = pl.Buffered(1)
    tile = IN_SPLIT * TOK_TILE
    tiles = tile // KT
    chunk_rows = tile // CHUNK_TOK
    slab = pl.BlockSpec((tiles, VT_SLAB, KT), lambda i: (i, 0, 0))
    slab_shape = jax.ShapeDtypeStruct((t // KT, VT_SLAB, KT), BF16)
    out_specs = [
        pl.BlockSpec((tile, Q_WIDTH), row), pl.BlockSpec((tile, KV_WIDTH), row),
        pl.BlockSpec((tile, KV_WIDTH), row), slab, slab,
        pl.BlockSpec((chunk_rows, CHUNK_TOK * W_CMP), row),
        pl.BlockSpec((tile, SLOT), row), pl.BlockSpec((tile, POOL_WIDTH), row)]
    out_shape = [
        jax.ShapeDtypeStruct((t, Q_WIDTH), BF16), jax.ShapeDtypeStruct((t, KV_WIDTH), BF16),
        jax.ShapeDtypeStruct((t, KV_WIDTH), BF16), slab_shape, slab_shape,
        jax.ShapeDtypeStruct((t // CHUNK_TOK, CHUNK_TOK * W_CMP), BF16),
        jax.ShapeDtypeStruct((t, SLOT), F32), jax.ShapeDtypeStruct((t, POOL_WIDTH), F32)]
    weights = _nbytes((OFF_MERGE, D_MODEL), F32)
    blocks = (_nbytes((tile, D_MODEL), F32) + _nbytes((tile, N_ROPE_TAB * SLOT), F32)
              + _nbytes((tile, Q_WIDTH + 2 * KV_WIDTH + W_CMP), BF16)
              + 2 * _nbytes((tiles, VT_SLAB, KT), BF16) + _nbytes((tile, SLOT + POOL_WIDTH), F32))
    res = pl.pallas_call(
        _in_proj_kernel,
        grid=(t // tile,),
        in_specs=[
            pl.BlockSpec((tile, D_MODEL), row),
            pl.BlockSpec((1, D_MODEL), fixed),
            pl.BlockSpec((pl.Element(OFF_MERGE), pl.Element(D_MODEL)), fixed, pipeline_mode=once),
            pl.BlockSpec((tile, N_ROPE_TAB * SLOT), row),
        ],
        out_specs=out_specs,
        out_shape=out_shape,
        scratch_shapes=[pltpu.VMEM((W_CMP // LANES, tile, LANES), F32)],
        compiler_params=pltpu.CompilerParams(
            dimension_semantics=("arbitrary",),
            vmem_limit_bytes=_vmem_limit(blocks, weights + _nbytes((tile, W_CMP), F32))),
        name="in_proj",
    )(x1, g_pre.reshape(1, -1), w_in_t, tab)
    return res


def _compress_kernel(x_ref, w1_ref, pe_ref, w1_raw_ref, w2_ref, tab_ref, o_ref):
    rows = x_ref.shape[0]
    acc = _dot(x_ref[...], w1_ref[0])
    bias = _dot(pe_ref[0], w1_raw_ref[0])[0:1, :]
    second = pltpu.roll(acc[:, CMP_HIDDEN:2 * CMP_HIDDEN], rows - 1, axis=0)
    hid = jax.nn.gelu(acc[:, 0:CMP_HIDDEN] + second + bias).astype(BF16)
    out = _dot(hid, w2_ref[0])
    is_key = pl.program_id(0) < N_KV
    o_ref[...] = jnp.where(is_key, _rope_slot(out, tab_ref[...]), out).astype(BF16)


def _compress(cmp_rows, cmp_pe_k, cmp_w1_k, cmp_w2_k, cmp_pe_v, cmp_w1_v, cmp_w2_v, tab_cmp):
    rows = cmp_rows.shape[0]
    chunk_feat = CHUNK_TOK * HEAD_DIM
    sub = 8

    def stacked(f, k, v):
        return jnp.stack([f(k), f(v)]).astype(BF16)

    w1 = stacked(lambda w: jnp.concatenate([w[:chunk_feat], w[chunk_feat:]], axis=1), cmp_w1_k, cmp_w1_v)
    w1_raw = stacked(lambda w: w, cmp_w1_k, cmp_w1_v)
    pe = stacked(lambda p: jnp.broadcast_to(p.reshape(1, -1), (sub, CMP_LEN * HEAD_DIM)), cmp_pe_k, cmp_pe_v)
    w2 = stacked(lambda w: jnp.pad(w, ((0, 0), (0, SLOT - HEAD_DIM))), cmp_w2_k, cmp_w2_v)
    per_tensor = lambda s: (s // N_KV, 0, 0)
    return pl.pallas_call(
        _compress_kernel,
        grid=(2 * N_KV,),
        in_specs=[
            pl.BlockSpec((rows, chunk_feat), lambda s: (0, s)),
            pl.BlockSpec((1,) + w1.shape[1:], per_tensor),
            pl.BlockSpec((1,) + pe.shape[1:], per_tensor),
            pl.BlockSpec((1,) + w1_raw.shape[1:], per_tensor),
            pl.BlockSpec((1,) + w2.shape[1:], per_tensor),
            pl.BlockSpec((rows, N_ROPE_TAB * SLOT), lambda s: (0, 0)),
        ],
        out_specs=pl.BlockSpec((rows, SLOT), lambda s: (0, s)),
        out_shape=jax.ShapeDtypeStruct((rows, 2 * N_KV * SLOT), BF16),
        compiler_params=pltpu.CompilerParams(
            dimension_semantics=("arbitrary",),
            vmem_limit_bytes=V7X_VMEM_BYTES - VMEM_COMPILER_RESERVE),
        name="compress",
    )(cmp_rows, w1, pe, w1_raw, w2, tab_cmp)


def _nsa_kernel(q_ref, gate_ref, kc_ref, vc_ref, ks_ref, vst_ref, kw_ref, vwt_ref, ov_ref,
                o_ref, ksp_ref, kwp_ref, qg_ref, mix_ref, *state_refs):
    i = pl.program_id(1)
    seq = ks_ref.shape[1]
    cols_g = GROUP * TQ
    chains_g = cols_g // CHAIN_COLS
    n_chains = N_KV * chains_g
    n_cmp = kc_ref.shape[1]
    sel_state = (state_refs[0:n_chains], state_refs[n_chains:2 * n_chains])
    win_state = (state_refs[2 * n_chains:3 * n_chains], state_refs[3 * n_chains:4 * n_chains])
    lane_q = lax.broadcasted_iota(jnp.int32, (TQ, SLOT), 1)

    def head_slot(x_ref, rows, j):
        two = x_ref[rows, (j // 2) * SLOT:(j // 2 + 1) * SLOT]
        if j % 2:
            words = pltpu.roll(pltpu.bitcast(two, jnp.uint32), HEAD_DIM, axis=1)
            two = pltpu.bitcast(words, BF16)
        return two

    @pl.when(i == 0)
    def _():
        for c in range(seq // KT):
            rows = pl.ds(c * KT, KT)
            lane = lax.broadcasted_iota(jnp.int32, (KT, SLOT), 1)
            key_blk = (c * KT + lax.broadcasted_iota(jnp.int32, (KT, SLOT), 0)) >> SEL_SHIFT
            onehot = jnp.where(lane - BLK_LANE0 == key_blk, 1.0, 0.0).astype(BF16)
            zeros = jnp.zeros((KT, SLOT), BF16)
            for g in range(N_KV):
                gs = slice(g * SLOT, (g + 1) * SLOT)
                ksp_ref[rows, gs] = jnp.where(lane < HEAD_DIM, head_slot(ks_ref.at[0], rows, g), onehot)
                kwp_ref[rows, gs] = jnp.where(lane < HEAD_DIM, head_slot(kw_ref.at[0], rows, g), zeros)

    def q_cols(r, x):
        return slice((x * GROUP + r) * HALF, (x * GROUP + r + 1) * HALF)

    def chain_half(c):
        return c * CHAIN_COLS // (GROUP * HALF)

    for h in range(N_HEADS):
        g, r = divmod(h, GROUP)
        q_pad = jnp.where(lane_q < HEAD_DIM, head_slot(q_ref, slice(None), h), jnp.zeros((TQ, SLOT), BF16))
        q_pad_t = q_pad.astype(F32).T.astype(BF16)
        for x in range(2):
            qg_ref[g, :, q_cols(r, x)] = q_pad_t[:, x * HALF:(x + 1) * HALF]

    gates_t = jax.nn.sigmoid(gate_ref[...]).T
    key_h = lax.broadcasted_iota(jnp.int32, (HALF, CHAIN_COLS), 0)
    qry_h = lax.broadcasted_iota(jnp.int32, (HALF, CHAIN_COLS), 1) & (HALF - 1)
    tri_le, tri_gt = key_h <= qry_h, key_h > qry_h
    diag_plan = ((slice(0, HALF), tri_le, slice(0, HALF)), (slice(0, KT), tri_le, slice(HALF, KT)))
    far_plan = ((slice(0, KT), tri_gt, slice(0, HALF)), (slice(HALF, KT), tri_gt, slice(0, HALF)))

    def gate_row(branch, h):
        c = branch * N_HEADS + h
        return gates_t[c:c + 1, :]

    sub = 8
    cmp_per_sub = sub * SEL_LEN // CMP_STRIDE

    def compressed_scores(ng):
        n_c = ng * cmp_per_sub
        cmp_scores = []
        for h in range(N_HEADS):
            g, r = divmod(h, GROUP)
            q_h = jnp.concatenate([qg_ref[g, :, q_cols(r, 0)], qg_ref[g, :, q_cols(r, 1)]], axis=1)
            cmp_scores.append(_dot(kc_ref[0, 0:n_c, g * SLOT:(g + 1) * SLOT], q_h))
        return cmp_scores

    def compressed_and_select(ng, start, cmp_scores):
        n_blk, n_c = ng * sub, ng * cmp_per_sub
        t_cmp = start + lax.broadcasted_iota(jnp.int32, (n_c, TQ), 1)
        n_idx = lax.broadcasted_iota(jnp.int32, (n_c, TQ), 0)
        cmp_valid = n_idx * CMP_STRIDE + (CMP_LEN - 1) <= t_cmp
        any_cmp = start + lax.broadcasted_iota(jnp.int32, (1, TQ), 1) >= CMP_LEN - 1
        t_row = start + lax.broadcasted_iota(jnp.int32, (n_blk, TQ), 1)
        blk = lax.broadcasted_iota(jnp.int32, (n_blk, TQ), 0)
        forced = (blk == t_row >> SEL_SHIFT) | (blk == 0)
        causal_blk = blk * SEL_LEN <= t_row
        idx8 = lax.broadcasted_iota(jnp.int32, (sub, TQ), 0)
        pad_c = jnp.zeros((n_cmp - n_c, TQ), F32)
        for g in range(N_KV):
            vct_g = vc_ref[0, :, g * SLOT:(g + 1) * SLOT].astype(F32).T[0:HEAD_DIM].astype(BF16)
            p_sum = jnp.zeros((n_c, TQ), F32)
            for r in range(GROUP):
                h = g * GROUP + r
                s = jnp.where(cmp_valid, cmp_scores[h], NEG_INF)
                p = jnp.exp2(s - jnp.max(s, axis=0, keepdims=True))
                inv = jnp.where(any_cmp, 1.0 / jnp.sum(p, axis=0, keepdims=True), 0.0)
                p = p * inv
                p_sum = p_sum + p
                p_all = jnp.concatenate([p, pad_c], axis=0) if ng * cmp_per_sub < n_cmp else p
                mix_ref[h] = gate_row(0, h) * _dot(vct_g, p_all.astype(BF16))
            p_sum_all = jnp.concatenate([p_sum, pad_c], axis=0) if ng * cmp_per_sub < n_cmp else p_sum
            imp = jnp.dot(ov_ref[0:n_blk, :], p_sum_all, precision=lax.Precision.HIGHEST,
                          preferred_element_type=F32)
            score = jnp.where(causal_blk, jnp.where(forced, FORCE_SCORE, imp), NEG_INF)
            rows8 = [score[k * sub:(k + 1) * sub] for k in range(ng)]
            ranks = [jnp.zeros((sub, TQ), jnp.int32) for _ in rows8]
            for c in range(n_blk):
                other = score[c:c + 1, :]
                for k, mine in enumerate(rows8):
                    if k * sub > c:
                        beats = other >= mine
                    elif (k + 1) * sub - 1 <= c:
                        beats = other > mine
                    else:
                        beats = (other > mine) | ((other == mine) & (idx8 > c - k * sub))
                    ranks[k] = ranks[k] + beats.astype(jnp.int32)
            rank = jnp.concatenate(ranks, axis=0) if ng > 1 else ranks[0]
            bias = jnp.where(causal_blk & (rank < N_SEL), 0.0, NEG_INF).astype(BF16)
            for r in range(GROUP):
                for x in range(2):
                    qg_ref[g, BLK_LANE0:BLK_LANE0 + n_blk, q_cols(r, x)] = bias[:, x * HALF:(x + 1) * HALF]

    ng_now = ((i + 1) * TQ + sub * SEL_LEN - 1) // (sub * SEL_LEN)
    for ng in range(1, seq // (sub * SEL_LEN) + 1):

        @pl.when(ng_now == ng)
        def _(ng=ng):
            compressed_and_select(ng, i * TQ, compressed_scores(ng))

    def flash_init(state):
        for m_ref, acc_ref in zip(*state):
            m_ref[...] = jnp.full(m_ref.shape, NEG_INF, F32)
            acc_ref[...] = jnp.zeros(acc_ref.shape, F32)

    def flash_tiles(tiles):
        jobs = [(tile, c) for tile in tiles for c in range(n_chains)]

        def key_plan(job):
            (_, _, _, plan, _), chain = job
            return (slice(0, KT), None, None) if plan is None else plan[chain_half(chain % chains_g)]

        scores = {}
        for step in range(len(jobs) + PIPE_DEPTH):
            if step < len(jobs):
                (kp_ref, _, kt, _, _), chain = jobs[step]
                g, c = divmod(chain, chains_g)
                keys, mask, rows = key_plan(jobs[step])
                off = pl.multiple_of(kt * KT + keys.start, HALF)
                s = _dot(kp_ref[pl.ds(off, keys.stop - keys.start), g * SLOT:(g + 1) * SLOT],
                         qg_ref[g, :, c * CHAIN_COLS:(c + 1) * CHAIN_COLS])
                if mask is not None:
                    parts = [s[0:rows.start], jnp.where(mask, s[rows], NEG_INF), s[rows.stop:]]
                    s = jnp.concatenate([p for p in parts if p.shape[0]], axis=0)
                scores[step] = s.astype(BF16)
            done = step - PIPE_DEPTH
            if done >= 0:
                (_, vt_ref, kt, _, (m_refs, acc_refs)), chain = jobs[done]
                g = chain // chains_g
                keys = key_plan(jobs[done])[0]
                m_ref, acc_ref = m_refs[chain], acc_refs[chain]
                s = scores.pop(done)
                m_old = m_ref[...]
                packed = [s[r * BF16_SUBLANES:(r + 1) * BF16_SUBLANES]
                          for r in range(s.shape[0] // BF16_SUBLANES)]
                m_tile = functools.reduce(jnp.maximum, packed).astype(F32)
                m_new = jnp.maximum(m_old, jnp.max(m_tile, axis=0, keepdims=True))
                p = jnp.exp2(s - m_new.astype(BF16))
                pv = _dot(vt_ref[kt, g * VT_ROWS:(g + 1) * VT_ROWS, keys], p)
                acc_ref[...] = jnp.exp2(m_old - m_new) * acc_ref[...] + pv
                m_ref[...] = m_new

    def flash_mix(branch, state):
        _, acc_refs = state
        for h in range(N_HEADS):
            g, r = divmod(h, GROUP)
            gate = gate_row(branch, h)
            for x in range(2):
                c, c0 = divmod(q_cols(r, x).start, CHAIN_COLS)
                acc = acc_refs[g * chains_g + c][:, c0:c0 + HALF]
                qs = slice(x * HALF, (x + 1) * HALF)
                scale = gate[:, qs] * (1.0 / acc[ONE_ROW:ONE_ROW + 1, :])
                mix_ref[h, :, qs] = mix_ref[h, :, qs] + scale * acc[0:HEAD_DIM]

    flash_init(sel_state)
    flash_init(win_state)

    def sel_past(kt):
        return (ksp_ref, vst_ref, kt, None, sel_state)

    def sel_pair(j, carry):
        flash_tiles([sel_past(2 * j), sel_past(2 * j + 1)])
        return carry

    lax.fori_loop(0, i >> 1, sel_pair, 0)

    @pl.when(i & 1 == 1)
    def _():
        flash_tiles([sel_past(i - 1)])

    n_back = WINDOW // KT
    sel_diag = (ksp_ref, vst_ref, i, diag_plan, sel_state)
    win_diag = (kwp_ref, vwt_ref, i, diag_plan, win_state)

    def win_back(d):
        return (kwp_ref, vwt_ref, i - d, far_plan if d == n_back else None, win_state)

    for have in range(n_back + 1):
        cond = (i == have) if have < n_back else (i >= have)

        @pl.when(cond)
        def _(have=have):
            flash_tiles([sel_diag] + [win_back(d) for d in range(have, 0, -1)] + [win_diag])

    flash_mix(1, sel_state)
    flash_mix(2, win_state)
    for h2 in range(N_HEADS // 2):
        pair = jnp.concatenate([mix_ref[2 * h2], mix_ref[2 * h2 + 1]], axis=0)
        o_ref[:, h2 * SLOT:(h2 + 1) * SLOT] = pair.T.astype(BF16)
    o_ref[:, Q_WIDTH:] = jnp.zeros((TQ, o_ref.shape[1] - Q_WIDTH), BF16)


def _overlap(n_cmp_pad, n_slc):
    c0 = jnp.arange(n_cmp_pad) * CMP_STRIDE
    s0 = jnp.arange(n_slc) * SEL_LEN
    ov = jnp.minimum(c0[None, :] + CMP_LEN, s0[:, None] + SEL_LEN) - jnp.maximum(c0[None, :], s0[:, None])
    return jnp.clip(ov, 0).astype(F32) / CMP_LEN


def _nsa(q, gate, kvc, ks, vst, kw, vwt, batch, seq_len):
    n_slc = seq_len // SEL_LEN
    n_cmp_pad = kvc.shape[0] // batch
    w_kv_pad = N_KV * SLOT
    kvc = kvc.reshape(batch, n_cmp_pad, 2 * w_kv_pad)
    assert n_slc <= SLOT - BLK_LANE0 and n_slc % 8 == 0
    assert CHAIN_COLS % HALF == 0 and (GROUP * HALF) % CHAIN_COLS == 0 and KT == 2 * HALF
    ov = _overlap(n_cmp_pad, n_slc)
    nq = seq_len // TQ
    n_kt = seq_len // KT
    qrow = lambda b, i: (b * nq + i, 0)
    per_b = lambda b, i: (b, 0, 0)
    fixed = lambda b, i: (0, 0)
    blocks = (_nbytes((TQ, Q_WIDTH), BF16) + _nbytes((TQ, SLOT), F32)
              + 2 * _nbytes((n_cmp_pad, w_kv_pad), BF16) + 2 * _nbytes((seq_len, KV_WIDTH), BF16)
              + 2 * _nbytes((n_kt, VT_SLAB, KT), BF16)
              + _nbytes(ov.shape, F32) + _nbytes((TQ, BRANCH_WIDTH), BF16))
    n_chains = N_KV * GROUP * TQ // CHAIN_COLS
    run_max = [pltpu.VMEM((1, CHAIN_COLS), F32)] * n_chains
    run_acc = [pltpu.VMEM((VT_ROWS, CHAIN_COLS), F32)] * n_chains
    scratch_shapes = (
        [pltpu.VMEM((seq_len, w_kv_pad), BF16),
         pltpu.VMEM((seq_len, w_kv_pad), BF16),
         pltpu.VMEM((N_KV, SLOT, GROUP * TQ), BF16),
         pltpu.VMEM((N_HEADS, HEAD_DIM, TQ), F32)]
        + run_max + run_acc + run_max + run_acc)
    scratch = sum(_nbytes(s.shape, s.dtype) for s in scratch_shapes)
    return pl.pallas_call(
        _nsa_kernel,
        grid=(batch, nq),
        in_specs=[
            pl.BlockSpec((TQ, Q_WIDTH), qrow),
            pl.BlockSpec((TQ, SLOT), qrow),
            pl.BlockSpec((1, n_cmp_pad, w_kv_pad), per_b),
            pl.BlockSpec((1, n_cmp_pad, w_kv_pad), lambda b, i: (b, 0, 1)),
            pl.BlockSpec((1, seq_len, KV_WIDTH), per_b),
            pl.BlockSpec((n_kt, VT_SLAB, KT), per_b),
            pl.BlockSpec((1, seq_len, KV_WIDTH), per_b),
            pl.BlockSpec((n_kt, VT_SLAB, KT), per_b),
            pl.BlockSpec(ov.shape, fixed),
        ],
        out_specs=pl.BlockSpec((TQ, BRANCH_WIDTH), qrow),
        out_shape=jax.ShapeDtypeStruct((batch * seq_len, BRANCH_WIDTH), BF16),
        scratch_shapes=scratch_shapes,
        compiler_params=pltpu.CompilerParams(
            dimension_semantics=("arbitrary", "arbitrary"),
            vmem_limit_bytes=_vmem_limit(blocks, scratch)),
        name="nsa",
    )(q, gate, kvc, kvc, ks.reshape(batch, seq_len, KV_WIDTH), vst,
      kw.reshape(batch, seq_len, KV_WIDTH), vwt, ov)


def _pool_kernel(u_ref, w_ref, scale_ref, _branches_hbm, o_ref):
    seq = u_ref.shape[0]
    t = lax.broadcasted_iota(jnp.int32, (seq, POOL_GROUP), 0)

    def shifted(x, k):
        return jnp.where(t >= k, pltpu.roll(x, k, axis=0), 0.0)

    for gi, w in enumerate(POOL_WINDOWS):
        sl = slice(gi * POOL_GROUP, (gi + 1) * POOL_GROUP)
        x = u_ref[:, sl]
        wsum = x
        span = 1
        while span < w:
            wsum = wsum + shifted(wsum, span)
            span *= 2
        cnt = jnp.minimum(t + 1, w).astype(F32)
        pooled = (wsum / cnt - x).astype(BF16)
        o_ref[:, sl] = (_dot(pooled, w_ref[gi]) * scale_ref[:, sl]).astype(BF16)


def _pool(pool_in, pool_w, pool_scale, branches, batch, seq_len):
    for w in POOL_WINDOWS:
        assert w & (w - 1) == 0
    blocks = (_nbytes((seq_len, POOL_WIDTH), F32) + _nbytes(pool_w.shape, BF16)
              + _nbytes((seq_len, POOL_WIDTH), BF16))
    return pl.pallas_call(
        _pool_kernel,
        grid=(batch,),
        in_specs=[
            pl.BlockSpec((seq_len, POOL_WIDTH), lambda b: (b, 0)),
            pl.BlockSpec(pool_w.shape, lambda b: (0, 0, 0)),
            pl.BlockSpec((1, POOL_WIDTH), lambda b: (0, 0)),
            pl.BlockSpec(memory_space=pl.ANY),
        ],
        out_specs=pl.BlockSpec((seq_len, POOL_WIDTH), lambda b: (b, POOL_COL_BLOCK)),
        out_shape=jax.ShapeDtypeStruct(branches.shape, branches.dtype),
        input_output_aliases={3: 0},
        compiler_params=pltpu.CompilerParams(
            dimension_semantics=("arbitrary",), vmem_limit_bytes=_vmem_limit(blocks)),
        name="pool",
    )(pool_in, pool_w.astype(BF16), pool_scale.reshape(1, -1), branches)


def _merge_kernel(x_ref, gpre_ref, wt_ref, nsa_ref, wa_ref, pool_ref, wp_ref, wo_ref, gpost_ref,
                  o_ref):
    rows = x_ref.shape[0] // MERGE_SPLIT
    for k in range(MERGE_SPLIT):
        rs = slice(k * rows, (k + 1) * rows)
        x = x_ref[rs, :]
        hn = _rms(x, gpre_ref[...])
        g_attn = jax.nn.sigmoid(_dot_nt(hn, wt_ref[0:D_MODEL, :]))
        g_pool = jax.nn.sigmoid(_dot_nt(hn, wt_ref[D_MODEL:2 * D_MODEL, :]))
        y = (g_attn * _dot(nsa_ref[rs, :].astype(F32), wa_ref[...])
             + g_pool * _dot(pool_ref[rs, :].astype(F32), wp_ref[...]))
        h = _dot(y, wo_ref[...])
        o_ref[rs, :] = x + _rms(h, gpost_ref[...])


def _merge(x1, g_pre, w_in_t, branches, w_attn, w_pool, w_out, g_post):
    t = x1.shape[0]
    row = lambda i: (i, 0)
    fixed = lambda i: (0, 0)
    once = pl.Buffered(1)
    tile = MERGE_SPLIT * TOK_TILE
    weights = (_nbytes((2 * D_MODEL, D_MODEL), F32) + _nbytes((Q_WIDTH, D_MODEL), F32)
               + _nbytes((POOL_WIDTH, D_MODEL), F32) + _nbytes((D_MODEL, D_MODEL), F32))
    blocks = (2 * _nbytes((tile, D_MODEL), F32) + _nbytes((tile, Q_WIDTH), BF16)
              + _nbytes((tile, POOL_WIDTH), BF16))
    gate_rows = pl.BlockSpec((pl.Element(2 * D_MODEL), pl.Element(D_MODEL)), lambda i: (OFF_MERGE, 0),
                             pipeline_mode=once)
    return pl.pallas_call(
        _merge_kernel,
        grid=(t // tile,),
        in_specs=[
            pl.BlockSpec((tile, D_MODEL), row),
            pl.BlockSpec((1, D_MODEL), fixed),
            gate_rows,
            pl.BlockSpec((tile, Q_WIDTH), row),
            pl.BlockSpec((Q_WIDTH, D_MODEL), fixed, pipeline_mode=once),
            pl.BlockSpec((tile, POOL_WIDTH), lambda i: (i, POOL_COL_BLOCK)),
            pl.BlockSpec((POOL_WIDTH, D_MODEL), fixed, pipeline_mode=once),
            pl.BlockSpec((D_MODEL, D_MODEL), fixed, pipeline_mode=once),
            pl.BlockSpec((1, D_MODEL), fixed),
        ],
        out_specs=pl.BlockSpec((tile, D_MODEL), row),
        out_shape=jax.ShapeDtypeStruct((t, D_MODEL), F32),
        compiler_params=pltpu.CompilerParams(
            dimension_semantics=("arbitrary",), vmem_limit_bytes=_vmem_limit(blocks, weights)),
        name="merge",
    )(x1, g_pre.reshape(1, -1), w_in_t, branches, w_attn, branches, w_pool, w_out, g_post.reshape(1, -1))


def _layer(x, tab, tab_cmp, batch, seq_len, g_ffn1_pre, w_ffn1_gate, w_ffn1_up, w_ffn1_down,
           g_ffn1_post, g_mix_pre, w_in, cmp_pe_k, cmp_w1_k, cmp_w2_k, cmp_pe_v, cmp_w1_v, cmp_w2_v,
           w_attn_branch, pool_w, pool_scale, w_pool_branch, w_out, g_mix_post, g_ffn2_pre,
           w_ffn2_gate, w_ffn2_up, w_ffn2_down, g_ffn2_post):
    x1 = _ffn(x, g_ffn1_pre, w_ffn1_gate, w_ffn1_up, w_ffn1_down, g_ffn1_post)
    w_in_t = w_in.T
    q, ks, kw, vst, vwt, cmp_rows, gate, pool_in = _in_proj(x1, g_mix_pre, w_in_t, tab)
    kvc = _compress(cmp_rows, cmp_pe_k, cmp_w1_k, cmp_w2_k, cmp_pe_v, cmp_w1_v, cmp_w2_v, tab_cmp)
    branches = _nsa(q, gate, kvc, ks, vst, kw, vwt, batch, seq_len)
    branches = _pool(pool_in, pool_w, pool_scale, branches, batch, seq_len)
    x2 = _merge(x1, g_mix_pre, w_in_t, branches, w_attn_branch, w_pool_branch, w_out, g_mix_post)
    return _ffn(x2, g_ffn2_pre, w_ffn2_gate, w_ffn2_up, w_ffn2_down, g_ffn2_post)


def kernel(x, positions, g_ffn1_pre, w_ffn1_gate, w_ffn1_up, w_ffn1_down, g_ffn1_post, g_mix_pre, w_in, cmp_pe_k, cmp_w1_k, cmp_w2_k, cmp_pe_v, cmp_w1_v, cmp_w2_v, w_attn_branch, pool_w, pool_scale, w_pool_branch, w_out, g_mix_post, g_ffn2_pre, w_ffn2_gate, w_ffn2_up, w_ffn2_down, g_ffn2_post):
    batch, seq_len, d_model = x.shape
    assert d_model == D_MODEL and seq_len % TOK_TILE == 0 and seq_len % TQ == 0
    assert TQ == KT and TQ % SEL_LEN == 0 and TOK_TILE % KT == 0 and WINDOW % KT == 0
    assert seq_len % CMP_STRIDE == 0 and CMP_LEN == 2 * CMP_STRIDE
    t = batch * seq_len
    tab = _rope_tables(positions.reshape(t), seq_len)
    n_cmp = (seq_len - CMP_LEN) // CMP_STRIDE + 1
    n_cmp_pad = seq_len // CMP_STRIDE
    pos_cmp = jnp.pad(positions[:, CMP_LEN - 1::CMP_STRIDE], ((0, 0), (0, n_cmp_pad - n_cmp)))
    tab_cmp = _rope_tables(pos_cmp.reshape(batch * n_cmp_pad), batch * n_cmp_pad)
    xf = x.reshape(t, D_MODEL)
    per_layer = (g_ffn1_pre, w_ffn1_gate, w_ffn1_up, w_ffn1_down, g_ffn1_post, g_mix_pre, w_in,
                 cmp_pe_k, cmp_w1_k, cmp_w2_k, cmp_pe_v, cmp_w1_v, cmp_w2_v, w_attn_branch, pool_w,
                 pool_scale, w_pool_branch, w_out, g_mix_post, g_ffn2_pre, w_ffn2_gate, w_ffn2_up,
                 w_ffn2_down, g_ffn2_post)
    for l in range(g_ffn1_pre.shape[0]):
        xf = _layer(xf, tab, tab_cmp, batch, seq_len, *(p[l] for p in per_layer))
    return xf.reshape(batch, seq_len, D_MODEL)
```

```python
import functools
import math

import jax
import jax.numpy as jnp
from jax import lax
from jax.experimental import pallas as pl
from jax.experimental.pallas import tpu as pltpu

F32 = jnp.float32
BF16 = jnp.bfloat16

D_MODEL = 1024
N_HEADS = 16
HEAD_DIM = 64
N_KV = 4
GROUP = N_HEADS // N_KV
ROT_DIM = HEAD_DIM // 4
ROT_HALF = ROT_DIM // 2
ROPE_THETA = 500000.0
CMP_LEN = 32
CMP_STRIDE = 16
CMP_HIDDEN = 2 * HEAD_DIM
SEL_LEN = 64
SEL_SHIFT = SEL_LEN.bit_length() - 1
N_SEL = 8
WINDOW = 512
POOL_WINDOWS = (2, 4, 8, 16)
POOL_WIDTH = D_MODEL // 2
POOL_GROUP = POOL_WIDTH // len(POOL_WINDOWS)
D_FF = 2816
EPS = 1e-6
NEG_INF = -1e30
FORCE_SCORE = 1e4
Q_WIDTH = N_HEADS * HEAD_DIM
KV_WIDTH = N_KV * HEAD_DIM
BRANCH_WIDTH = Q_WIDTH + POOL_WIDTH
POOL_COL_BLOCK = Q_WIDTH // POOL_WIDTH
assert POOL_COL_BLOCK * POOL_WIDTH == Q_WIDTH
N_GATES = 3 * N_HEADS
LOG2_E = math.log2(math.e)

LANES = 128
BF16_SUBLANES = 16
V7X_VMEM_BYTES = 64 * 1024 * 1024
VMEM_COMPILER_RESERVE = 4 * 1024 * 1024
VMEM_MIN_REQUEST = 32 * 1024 * 1024

SLOT = LANES
BLK_LANE0 = HEAD_DIM
VT_ROWS = HEAD_DIM + BF16_SUBLANES
ONE_ROW = HEAD_DIM
TOK_TILE = 512
TQ = 256
KT = 256
HALF = TQ // 2
CHAIN_COLS = 256
PIPE_DEPTH = 10
FF_CHUNK = 256
IN_SPLIT = 2
MERGE_SPLIT = 2
N_ROPE_TAB = 2


def _vmem_limit(block_bytes, scratch_bytes=0):
    need = 2 * block_bytes + scratch_bytes
    return int(min(V7X_VMEM_BYTES - VMEM_COMPILER_RESERVE, max(2 * need, VMEM_MIN_REQUEST)))


def _nbytes(shape, dtype):
    n = 1
    for s in shape:
        n *= s
    return n * jnp.dtype(dtype).itemsize


def _rms(xf, g):
    return xf * lax.rsqrt(jnp.mean(xf * xf, axis=-1, keepdims=True) + EPS) * g


def _dot(a, b):
    return jnp.dot(a, b, preferred_element_type=F32)


def _dot_nt(a, b):
    return lax.dot_general(a, b, (((1,), (1,)), ((), ())), preferred_element_type=F32)


def _rope_trig_kernel(pos_ref, inv_ref, tab_ref):
    tn = pos_ref.shape[1]
    ang = pos_ref[...].astype(F32) * inv_ref[...]
    c, s = jnp.cos(ang), jnp.sin(ang)
    rest = HEAD_DIM - ROT_DIM
    reps = SLOT // HEAD_DIM
    cos_rows = jnp.concatenate([c, c, jnp.ones((rest, tn), F32)] * reps, axis=0)
    sin_rows = jnp.concatenate([-s, s, jnp.zeros((rest, tn), F32)] * reps, axis=0)
    for j in range(tn // LANES):
        cs = slice(j * LANES, (j + 1) * LANES)
        tab_ref[cs, 0:SLOT] = cos_rows[:, cs].T
        tab_ref[cs, SLOT:2 * SLOT] = sin_rows[:, cs].T


def _rope_tables(pos_flat, tile):
    n = pos_flat.shape[0]
    inv = ROPE_THETA ** (-jnp.arange(ROT_HALF, dtype=F32) * (2.0 / ROT_DIM))
    return pl.pallas_call(
        _rope_trig_kernel,
        grid=(n // tile,),
        in_specs=[pl.BlockSpec((1, tile), lambda i: (0, i)),
                  pl.BlockSpec((ROT_HALF, 1), lambda i: (0, 0))],
        out_specs=pl.BlockSpec((tile, N_ROPE_TAB * SLOT), lambda i: (i, 0)),
        out_shape=jax.ShapeDtypeStruct((n, N_ROPE_TAB * SLOT), F32),
        compiler_params=pltpu.CompilerParams(dimension_semantics=("arbitrary",)),
        name="rope_trig",
    )(pos_flat.reshape(1, n), inv.reshape(ROT_HALF, 1))


def _rope_slot(y, tab):
    cos_t = tab[:, 0:SLOT]
    sin_t = tab[:, SLOT:2 * SLOT]
    lane = lax.broadcasted_iota(jnp.int32, y.shape, 1) & (HEAD_DIM - 1)
    up = pltpu.roll(y, SLOT - ROT_HALF, axis=1)
    down = pltpu.roll(y, ROT_HALF, axis=1)
    return y * cos_t + jnp.where(lane < ROT_HALF, up, down) * sin_t


def _ffn_kernel(x_ref, gpre_ref, wg_hbm, wu_hbm, wd_hbm, gpost_ref, o_ref, wg_ref, wu_ref, wd_ref, sem):
    n_chunks = D_FF // FF_CHUNK

    def chunk_copies(c):
        sl = slice(c * FF_CHUNK, (c + 1) * FF_CHUNK)
        return (pltpu.make_async_copy(wg_hbm.at[:, sl], wg_ref.at[:, sl], sem.at[0, c]),
                pltpu.make_async_copy(wu_hbm.at[:, sl], wu_ref.at[:, sl], sem.at[1, c]),
                pltpu.make_async_copy(wd_hbm.at[sl, :], wd_ref.at[sl, :], sem.at[2, c]))

    def body(first_step):
        x = x_ref[...]
        hn = _rms(x, gpre_ref[...])
        for c in range(n_chunks):
            sl = slice(c * FF_CHUNK, (c + 1) * FF_CHUNK)
            if first_step:
                for copy in chunk_copies(c):
                    copy.wait()
            g = _dot(hn, wg_ref[:, sl])
            u = _dot(hn, wu_ref[:, sl])
            d = _dot(g * jax.nn.sigmoid(g) * u, wd_ref[sl, :])
            if c == 0:
                o_ref[...] = d
            else:
                o_ref[...] += d
        o_ref[...] = x + 0.5 * _rms(o_ref[...], gpost_ref[...])

    @pl.when(pl.program_id(0) == 0)
    def _():
        for c in range(n_chunks):
            for copy in chunk_copies(c):
                copy.start()
        body(True)

    @pl.when(pl.program_id(0) > 0)
    def _():
        body(False)


def _ffn(x, g_pre, w_gate, w_up, w_down, g_post):
    t = x.shape[0]
    row = lambda i: (i, 0)
    fixed = lambda i: (0, 0)
    in_hbm = pl.BlockSpec(memory_space=pl.ANY)
    weights = 3 * _nbytes((D_MODEL, D_FF), F32)
    blocks = 2 * _nbytes((TOK_TILE, D_MODEL), F32)
    return pl.pallas_call(
        _ffn_kernel,
        grid=(t // TOK_TILE,),
        in_specs=[
            pl.BlockSpec((TOK_TILE, D_MODEL), row),
            pl.BlockSpec((1, D_MODEL), fixed),
            in_hbm, in_hbm, in_hbm,
            pl.BlockSpec((1, D_MODEL), fixed),
        ],
        out_specs=pl.BlockSpec((TOK_TILE, D_MODEL), row),
        out_shape=jax.ShapeDtypeStruct((t, D_MODEL), F32),
        scratch_shapes=[pltpu.VMEM((D_MODEL, D_FF), F32), pltpu.VMEM((D_MODEL, D_FF), F32),
                        pltpu.VMEM((D_FF, D_MODEL), F32),
                        pltpu.SemaphoreType.DMA((3, D_FF // FF_CHUNK))],
        compiler_params=pltpu.CompilerParams(
            dimension_semantics=("arbitrary",), vmem_limit_bytes=_vmem_limit(blocks, weights)),
        name="ffn",
    )(x, g_pre.reshape(1, -1), w_gate, w_up, w_down, g_post.reshape(1, -1))


OFF_Q = 0
OFF_CMP = OFF_Q + Q_WIDTH
OFF_KS = OFF_CMP + 2 * KV_WIDTH
OFF_VS = OFF_KS + KV_WIDTH
OFF_KW = OFF_VS + KV_WIDTH
OFF_VW = OFF_KW + KV_WIDTH
OFF_GATE = OFF_VW + KV_WIDTH
OFF_POOL = OFF_GATE + N_GATES
OFF_MERGE = OFF_POOL + POOL_WIDTH
W_CMP = 2 * KV_WIDTH
CHUNK_TOK = CMP_STRIDE
VT_SLAB = N_KV * VT_ROWS


def _in_proj_kernel(x_ref, g_ref, w_ref, tab_ref, q_ref, ks_ref, kw_ref, vst_ref,
                    vwt_ref, cmp_ref, gate_ref, pool_ref, cmp_scr):
    rows = x_ref.shape[0] // IN_SPLIT
    wide = 4 * SLOT
    aux = (lax.broadcasted_iota(jnp.int32, (VT_ROWS - HEAD_DIM, KT), 0) == 0).astype(BF16)

    def part(k):
        rs = slice(k * rows, (k + 1) * rows)
        hn = _rms(x_ref[rs, :], g_ref[...])
        tab = tab_ref[rs, :]
        tab_q = tab * (HEAD_DIM ** -0.5 * LOG2_E)
        for c in range(Q_WIDTH // wide):
            y = _dot_nt(hn, w_ref[OFF_Q + c * wide:OFF_Q + (c + 1) * wide, :])
            for j in range(wide // SLOT):
                sl = slice(c * wide + j * SLOT, c * wide + (j + 1) * SLOT)
                q_ref[rs, sl] = _rope_slot(y[:, j * SLOT:(j + 1) * SLOT], tab_q).astype(BF16)
        for k_ref, off in ((ks_ref, OFF_KS), (kw_ref, OFF_KW)):
            y = _dot_nt(hn, w_ref[off:off + KV_WIDTH, :])
            for j in range(KV_WIDTH // SLOT):
                sl = slice(j * SLOT, (j + 1) * SLOT)
                k_ref[rs, sl] = _rope_slot(y[:, sl], tab).astype(BF16)
        y = _dot_nt(hn, w_ref[OFF_CMP:OFF_CMP + W_CMP, :])
        for j in range(W_CMP // LANES):
            cmp_scr[j, rs, :] = y[:, j * LANES:(j + 1) * LANES]
        chunk_rows = rows // CHUNK_TOK
        low = lax.broadcasted_iota(jnp.int32, (chunk_rows, LANES), 1) < HEAD_DIM
        for j in range(W_CMP // LANES):
            for pair in range(CHUNK_TOK // 2):
                tok = [cmp_scr[j, pl.ds(k * rows + 2 * pair + e, chunk_rows, stride=CHUNK_TOK), :]
                       for e in range(2)]
                both = (jnp.where(low, tok[0], pltpu.roll(tok[1], HEAD_DIM, axis=1)),
                        jnp.where(low, pltpu.roll(tok[0], HEAD_DIM, axis=1), tok[1]))
                for e in range(2):
                    c0 = (2 * j + e) * CHUNK_TOK * HEAD_DIM + pair * LANES
                    cmp_ref[k * chunk_rows:(k + 1) * chunk_rows, c0:c0 + LANES] = both[e].astype(BF16)
        gate_ref[rs, :] = _dot_nt(hn, w_ref[OFF_GATE:OFF_GATE + SLOT, :])
        pool_ref[rs, :] = _dot_nt(hn, w_ref[OFF_POOL:OFF_MERGE, :])
        tiles = rows // KT
        for vt_ref, off in ((vst_ref, OFF_VS), (vwt_ref, OFF_VW)):
            vt = _dot_nt(w_ref[off:off + KV_WIDTH, :], hn).astype(BF16)
            for j in range(tiles):
                for g in range(N_KV):
                    vt_ref[k * tiles + j, g * VT_ROWS:g * VT_ROWS + HEAD_DIM, :] = (
                        vt[g * HEAD_DIM:(g + 1) * HEAD_DIM, j * KT:(j + 1) * KT])
                    vt_ref[k * tiles + j, g * VT_ROWS + HEAD_DIM:(g + 1) * VT_ROWS, :] = aux

    for k in range(IN_SPLIT):
        part(k)


def _in_proj(x1, g_pre, w_in_t, tab):
    t = x1.shape[0]
    row = lambda i: (i, 0)
    fixed = lambda i: (0, 0)
    once = pl.Buffered(1)
    tile = IN_SPLIT * TOK_TILE
    tiles = tile // KT
    chunk_rows = tile // CHUNK_TOK
    slab = pl.BlockSpec((tiles, VT_SLAB, KT), lambda i: (i, 0, 0))
    slab_shape = jax.ShapeDtypeStruct((t // KT, VT_SLAB, KT), BF16)
    out_specs = [
        pl.BlockSpec((tile, Q_WIDTH), row), pl.BlockSpec((tile, KV_WIDTH), row),
        pl.BlockSpec((tile, KV_WIDTH), row), slab, slab,
        pl.BlockSpec((chunk_rows, CHUNK_TOK * W_CMP), row),
        pl.BlockSpec((tile, SLOT), row), pl.BlockSpec((tile, POOL_WIDTH), row)]
    out_shape = [
        jax.ShapeDtypeStruct((t, Q_WIDTH), BF16), jax.ShapeDtypeStruct((t, KV_WIDTH), BF16),
        jax.ShapeDtypeStruct((t, KV_WIDTH), BF16), slab_shape, slab_shape,
        jax.ShapeDtypeStruct((t // CHUNK_TOK, CHUNK_TOK * W_CMP), BF16),
        jax.ShapeDtypeStruct((t, SLOT), F32), jax.ShapeDtypeStruct((t, POOL_WIDTH), F32)]
    weights = _nbytes((OFF_MERGE, D_MODEL), F32)
    blocks = (_nbytes((tile, D_MODEL), F32) + _nbytes((tile, N_ROPE_TAB * SLOT), F32)
              + _nbytes((tile, Q_WIDTH + 2 * KV_WIDTH + W_CMP), BF16)
              + 2 * _nbytes((tiles, VT_SLAB, KT), BF16) + _nbytes((tile, SLOT + POOL_WIDTH), F32))
    res = pl.pallas_call(
        _in_proj_kernel,
        grid=(t // tile,),
        in_specs=[
            pl.BlockSpec((tile, D_MODEL), row),
            pl.BlockSpec((1, D_MODEL), fixed),
            pl.BlockSpec((pl.Element(OFF_MERGE), pl.Element(D_MODEL)), fixed, pipeline_mode=once),
            pl.BlockSpec((tile, N_ROPE_TAB * SLOT), row),
        ],
        out_specs=out_specs,
        out_shape=out_shape,
        scratch_shapes=[pltpu.VMEM((W_CMP // LANES, tile, LANES), F32)],
        compiler_params=pltpu.CompilerParams(
            dimension_semantics=("arbitrary",),
            vmem_limit_bytes=_vmem_limit(blocks, weights + _nbytes((tile, W_CMP), F32))),
        name="in_proj",
    )(x1, g_pre.reshape(1, -1), w_in_t, tab)
    return res


def _compress_kernel(x_ref, w1_ref, pe_ref, w1_raw_ref, w2_ref, tab_ref, o_ref):
    rows = x_ref.shape[0]
    acc = _dot(x_ref[...], w1_ref[0])
    bias = _dot(pe_ref[0], w1_raw_ref[0])[0:1, :]
    second = pltpu.roll(acc[:, CMP_HIDDEN:2 * CMP_HIDDEN], rows - 1, axis=0)
    hid = jax.nn.gelu(acc[:, 0:CMP_HIDDEN] + second + bias).astype(BF16)
    out = _dot(hid, w2_ref[0])
    is_key = pl.program_id(0) < N_KV
    o_ref[...] = jnp.where(is_key, _rope_slot(out, tab_ref[...]), out).astype(BF16)


def _compress(cmp_rows, cmp_pe_k, cmp_w1_k, cmp_w2_k, cmp_pe_v, cmp_w1_v, cmp_w2_v, tab_cmp):
    rows = cmp_rows.shape[0]
    chunk_feat = CHUNK_TOK * HEAD_DIM
    sub = 8

    def stacked(f, k, v):
        return jnp.stack([f(k), f(v)]).astype(BF16)

    w1 = stacked(lambda w: jnp.concatenate([w[:chunk_feat], w[chunk_feat:]], axis=1), cmp_w1_k, cmp_w1_v)
    w1_raw = stacked(lambda w: w, cmp_w1_k, cmp_w1_v)
    pe = stacked(lambda p: jnp.broadcast_to(p.reshape(1, -1), (sub, CMP_LEN * HEAD_DIM)), cmp_pe_k, cmp_pe_v)
    w2 = stacked(lambda w: jnp.pad(w, ((0, 0), (0, SLOT - HEAD_DIM))), cmp_w2_k, cmp_w2_v)
    per_tensor = lambda s: (s // N_KV, 0, 0)
    return pl.pallas_call(
        _compress_kernel,
        grid=(2 * N_KV,),
        in_specs=[
            pl.BlockSpec((rows, chunk_feat), lambda s: (0, s)),
            pl.BlockSpec((1,) + w1.shape[1:], per_tensor),
            pl.BlockSpec((1,) + pe.shape[1:], per_tensor),
            pl.BlockSpec((1,) + w1_raw.shape[1:], per_tensor),
            pl.BlockSpec((1,) + w2.shape[1:], per_tensor),
            pl.BlockSpec((rows, N_ROPE_TAB * SLOT), lambda s: (0, 0)),
        ],
        out_specs=pl.BlockSpec((rows, SLOT), lambda s: (0, s)),
        out_shape=jax.ShapeDtypeStruct((rows, 2 * N_KV * SLOT), BF16),
        compiler_params=pltpu.CompilerParams(
            dimension_semantics=("arbitrary",),
            vmem_limit_bytes=V7X_VMEM_BYTES - VMEM_COMPILER_RESERVE),
        name="compress",
    )(cmp_rows, w1, pe, w1_raw, w2, tab_cmp)


def _nsa_kernel(q_ref, gate_ref, kc_ref, vc_ref, ks_ref, vst_ref, kw_ref, vwt_ref, ov_ref,
                o_ref, ksp_ref, kwp_ref, qg_ref, mix_ref, *state_refs):
    i = pl.program_id(1)
    seq = ks_ref.shape[1]
    cols_g = GROUP * TQ
    chains_g = cols_g // CHAIN_COLS
    n_chains = N_KV * chains_g
    n_cmp = kc_ref.shape[1]
    sel_state = (state_refs[0:n_chains], state_refs[n_chains:2 * n_chains])
    win_state = (state_refs[2 * n_chains:3 * n_chains], state_refs[3 * n_chains:4 * n_chains])
    lane_q = lax.broadcasted_iota(jnp.int32, (TQ, SLOT), 1)

    def head_slot(x_ref, rows, j):
        two = x_ref[rows, (j // 2) * SLOT:(j // 2 + 1) * SLOT]
        if j % 2:
            words = pltpu.roll(pltpu.bitcast(two, jnp.uint32), HEAD_DIM, axis=1)
            two = pltpu.bitcast(words, BF16)
        return two

    @pl.when(i == 0)
    def _():
        for c in range(seq // KT):
            rows = pl.ds(c * KT, KT)
            lane = lax.broadcasted_iota(jnp.int32, (KT, SLOT), 1)
            key_blk = (c * KT + lax.broadcasted_iota(jnp.int32, (KT, SLOT), 0)) >> SEL_SHIFT
            onehot = jnp.where(lane - BLK_LANE0 == key_blk, 1.0, 0.0).astype(BF16)
            zeros = jnp.zeros((KT, SLOT), BF16)
            for g in range(N_KV):
                gs = slice(g * SLOT, (g + 1) * SLOT)
                ksp_ref[rows, gs] = jnp.where(lane < HEAD_DIM, head_slot(ks_ref.at[0], rows, g), onehot)
                kwp_ref[rows, gs] = jnp.where(lane < HEAD_DIM, head_slot(kw_ref.at[0], rows, g), zeros)

    def q_cols(r, x):
        return slice((x * GROUP + r) * HALF, (x * GROUP + r + 1) * HALF)

    def chain_half(c):
        return c * CHAIN_COLS // (GROUP * HALF)

    for h in range(N_HEADS):
        g, r = divmod(h, GROUP)
        q_pad = jnp.where(lane_q < HEAD_DIM, head_slot(q_ref, slice(None), h), jnp.zeros((TQ, SLOT), BF16))
        q_pad_t = q_pad.astype(F32).T.astype(BF16)
        for x in range(2):
            qg_ref[g, :, q_cols(r, x)] = q_pad_t[:, x * HALF:(x + 1) * HALF]

    gates_t = jax.nn.sigmoid(gate_ref[...]).T
    key_h = lax.broadcasted_iota(jnp.int32, (HALF, CHAIN_COLS), 0)
    qry_h = lax.broadcasted_iota(jnp.int32, (HALF, CHAIN_COLS), 1) & (HALF - 1)
    tri_le, tri_gt = key_h <= qry_h, key_h > qry_h
    diag_plan = ((slice(0, HALF), tri_le, slice(0, HALF)), (slice(0, KT), tri_le, slice(HALF, KT)))
    far_plan = ((slice(0, KT), tri_gt, slice(0, HALF)), (slice(HALF, KT), tri_gt, slice(0, HALF)))

    def gate_row(branch, h):
        c = branch * N_HEADS + h
        return gates_t[c:c + 1, :]

    sub = 8
    cmp_per_sub = sub * SEL_LEN // CMP_STRIDE

    def compressed_scores(ng):
        n_c = ng * cmp_per_sub
        cmp_scores = []
        for h in range(N_HEADS):
            g, r = divmod(h, GROUP)
            q_h = jnp.concatenate([qg_ref[g, :, q_cols(r, 0)], qg_ref[g, :, q_cols(r, 1)]], axis=1)
            cmp_scores.append(_dot(kc_ref[0, 0:n_c, g * SLOT:(g + 1) * SLOT], q_h))
        return cmp_scores

    def compressed_and_select(ng, start, cmp_scores):
        n_blk, n_c = ng * sub, ng * cmp_per_sub
        t_cmp = start + lax.broadcasted_iota(jnp.int32, (n_c, TQ), 1)
        n_idx = lax.broadcasted_iota(jnp.int32, (n_c, TQ), 0)
        cmp_valid = n_idx * CMP_STRIDE + (CMP_LEN - 1) <= t_cmp
        any_cmp = start + lax.broadcasted_iota(jnp.int32, (1, TQ), 1) >= CMP_LEN - 1
        t_row = start + lax.broadcasted_iota(jnp.int32, (n_blk, TQ), 1)
        blk = lax.broadcasted_iota(jnp.int32, (n_blk, TQ), 0)
        forced = (blk == t_row >> SEL_SHIFT) | (blk == 0)
        causal_blk = blk * SEL_LEN <= t_row
        idx8 = lax.broadcasted_iota(jnp.int32, (sub, TQ), 0)
        pad_c = jnp.zeros((n_cmp - n_c, TQ), F32)
        for g in range(N_KV):
            vct_g = vc_ref[0, :, g * SLOT:(g + 1) * SLOT].astype(F32).T[0:HEAD_DIM].astype(BF16)
            p_sum = jnp.zeros((n_c, TQ), F32)
            for r in range(GROUP):
                h = g * GROUP + r
                s = jnp.where(cmp_valid, cmp_scores[h], NEG_INF)
                p = jnp.exp2(s - jnp.max(s, axis=0, keepdims=True))
                inv = jnp.where(any_cmp, 1.0 / jnp.sum(p, axis=0, keepdims=True), 0.0)
                p = p * inv
                p_sum = p_sum + p
                p_all = jnp.concatenate([p, pad_c], axis=0) if ng * cmp_per_sub < n_cmp else p
                mix_ref[h] = gate_row(0, h) * _dot(vct_g, p_all.astype(BF16))
            p_sum_all = jnp.concatenate([p_sum, pad_c], axis=0) if ng * cmp_per_sub < n_cmp else p_sum
            imp = jnp.dot(ov_ref[0:n_blk, :], p_sum_all, precision=lax.Precision.HIGHEST,
                          preferred_element_type=F32)
            score = jnp.where(causal_blk, jnp.where(forced, FORCE_SCORE, imp), NEG_INF)
            rows8 = [score[k * sub:(k + 1) * sub] for k in range(ng)]
            ranks = [jnp.zeros((sub, TQ), jnp.int32) for _ in rows8]
            for c in range(n_blk):
                other = score[c:c + 1, :]
                for k, mine in enumerate(rows8):
                    if k * sub > c:
                        beats = other >= mine
                    elif (k + 1) * sub - 1 <= c:
                        beats = other > mine
                    else:
                        beats = (other > mine) | ((other == mine) & (idx8 > c - k * sub))
                    ranks[k] = ranks[k] + beats.astype(jnp.int32)
            rank = jnp.concatenate(ranks, axis=0) if ng > 1 else ranks[0]
            bias = jnp.where(causal_blk & (rank < N_SEL), 0.0, NEG_INF).astype(BF16)
            for r in range(GROUP):
                for x in range(2):
                    qg_ref[g, BLK_LANE0:BLK_LANE0 + n_blk, q_cols(r, x)] = bias[:, x * HALF:(x + 1) * HALF]

    ng_now = ((i + 1) * TQ + sub * SEL_LEN - 1) // (sub * SEL_LEN)
    for ng in range(1, seq // (sub * SEL_LEN) + 1):

        @pl.when(ng_now == ng)
        def _(ng=ng):
            compressed_and_select(ng, i * TQ, compressed_scores(ng))

    def flash_init(state):
        for m_ref, acc_ref in zip(*state):
            m_ref[...] = jnp.full(m_ref.shape, NEG_INF, F32)
            acc_ref[...] = jnp.zeros(acc_ref.shape, F32)

    def flash_tiles(tiles):
        jobs = [(tile, c) for tile in tiles for c in range(n_chains)]

        def key_plan(job):
            (_, _, _, plan, _), chain = job
            return (slice(0, KT), None, None) if plan is None else plan[chain_half(chain % chains_g)]

        scores = {}
        for step in range(len(jobs) + PIPE_DEPTH):
            if step < len(jobs):
                (kp_ref, _, kt, _, _), chain = jobs[step]
                g, c = divmod(chain, chains_g)
                keys, mask, rows = key_plan(jobs[step])
                off = pl.multiple_of(kt * KT + keys.start, HALF)
                s = _dot(kp_ref[pl.ds(off, keys.stop - keys.start), g * SLOT:(g + 1) * SLOT],
                         qg_ref[g, :, c * CHAIN_COLS:(c + 1) * CHAIN_COLS])
                if mask is not None:
                    parts = [s[0:rows.start], jnp.where(mask, s[rows], NEG_INF), s[rows.stop:]]
                    s = jnp.concatenate([p for p in parts if p.shape[0]], axis=0)
                scores[step] = s.astype(BF16)
            done = step - PIPE_DEPTH
            if done >= 0:
                (_, vt_ref, kt, _, (m_refs, acc_refs)), chain = jobs[done]
                g = chain // chains_g
                keys = key_plan(jobs[done])[0]
                m_ref, acc_ref = m_refs[chain], acc_refs[chain]
                s = scores.pop(done)
                m_old = m_ref[...]
                packed = [s[r * BF16_SUBLANES:(r + 1) * BF16_SUBLANES]
                          for r in range(s.shape[0] // BF16_SUBLANES)]
                m_tile = functools.reduce(jnp.maximum, packed).astype(F32)
                m_new = jnp.maximum(m_old, jnp.max(m_tile, axis=0, keepdims=True))
                p = jnp.exp2(s - m_new.astype(BF16))
                pv = _dot(vt_ref[kt, g * VT_ROWS:(g + 1) * VT_ROWS, keys], p)
                acc_ref[...] = jnp.exp2(m_old - m_new) * acc_ref[...] + pv
                m_ref[...] = m_new

    def flash_mix(branch, state):
        _, acc_refs = state
        for h in range(N_HEADS):
            g, r = divmod(h, GROUP)
            gate = gate_row(branch, h)
            for x in range(2):
                c, c0 = divmod(q_cols(r, x).start, CHAIN_COLS)
                acc = acc_refs[g * chains_g + c][:, c0:c0 + HALF]
                qs = slice(x * HALF, (x + 1) * HALF)
                scale = gate[:, qs] * (1.0 / acc[ONE_ROW:ONE_ROW + 1, :])
                mix_ref[h, :, qs] = mix_ref[h, :, qs] + scale * acc[0:HEAD_DIM]

    flash_init(sel_state)
    flash_init(win_state)

    def sel_past(kt):
        return (ksp_ref, vst_ref, kt, None, sel_state)

    def sel_pair(j, carry):
        flash_tiles([sel_past(2 * j), sel_past(2 * j + 1)])
        return carry

    lax.fori_loop(0, i >> 1, sel_pair, 0)

    @pl.when(i & 1 == 1)
    def _():
        flash_tiles([sel_past(i - 1)])

    n_back = WINDOW // KT
    sel_diag = (ksp_ref, vst_ref, i, diag_plan, sel_state)
    win_diag = (kwp_ref, vwt_ref, i, diag_plan, win_state)

    def win_back(d):
        return (kwp_ref, vwt_ref, i - d, far_plan if d == n_back else None, win_state)

    for have in range(n_back + 1):
        cond = (i == have) if have < n_back else (i >= have)

        @pl.when(cond)
        def _(have=have):
            flash_tiles([sel_diag] + [win_back(d) for d in range(have, 0, -1)] + [win_diag])

    flash_mix(1, sel_state)
    flash_mix(2, win_state)
    for h2 in range(N_HEADS // 2):
        pair = jnp.concatenate([mix_ref[2 * h2], mix_ref[2 * h2 + 1]], axis=0)
        o_ref[:, h2 * SLOT:(h2 + 1) * SLOT] = pair.T.astype(BF16)
    o_ref[:, Q_WIDTH:] = jnp.zeros((TQ, o_ref.shape[1] - Q_WIDTH), BF16)


def _overlap(n_cmp_pad, n_slc):
    c0 = jnp.arange(n_cmp_pad) * CMP_STRIDE
    s0 = jnp.arange(n_slc) * SEL_LEN
    ov = jnp.minimum(c0[None, :] + CMP_LEN, s0[:, None] + SEL_LEN) - jnp.maximum(c0[None, :], s0[:, None])
    return jnp.clip(ov, 0).astype(F32) / CMP_LEN


def _nsa(q, gate, kvc, ks, vst, kw, vwt, batch, seq_len):
    n_slc = seq_len // SEL_LEN
    n_cmp_pad = kvc.shape[0] // batch
    w_kv_pad = N_KV * SLOT
    kvc = kvc.reshape(batch, n_cmp_pad, 2 * w_kv_pad)
    assert n_slc <= SLOT - BLK_LANE0 and n_slc % 8 == 0
    assert CHAIN_COLS % HALF == 0 and (GROUP * HALF) % CHAIN_COLS == 0 and KT == 2 * HALF
    ov = _overlap(n_cmp_pad, n_slc)
    nq = seq_len // TQ
    n_kt = seq_len // KT
    qrow = lambda b, i: (b * nq + i, 0)
    per_b = lambda b, i: (b, 0, 0)
    fixed = lambda b, i: (0, 0)
    n_chains = N_KV * GROUP * TQ // CHAIN_COLS
    run_max = [pltpu.VMEM((1, CHAIN_COLS), F32)] * n_chains
    run_acc = [pltpu.VMEM((VT_ROWS, CHAIN_COLS), F32)] * n_chains
    scratch_shapes = (
        [pltpu.VMEM((seq_len, w_kv_pad), BF16),
         pltpu.VMEM((seq_len, w_kv_pad), BF16),
         pltpu.VMEM((N_KV, SLOT, GROUP * TQ), BF16),
         pltpu.VMEM((N_HEADS, HEAD_DIM, TQ), F32)]
        + run_max + run_acc + run_max + run_acc)
    return pl.pallas_call(
        _nsa_kernel,
        grid=(batch, nq),
        in_specs=[
            pl.BlockSpec((TQ, Q_WIDTH), qrow),
            pl.BlockSpec((TQ, SLOT), qrow),
            pl.BlockSpec((1, n_cmp_pad, w_kv_pad), per_b),
            pl.BlockSpec((1, n_cmp_pad, w_kv_pad), lambda b, i: (b, 0, 1)),
            pl.BlockSpec((1, seq_len, KV_WIDTH), per_b),
            pl.BlockSpec((n_kt, VT_SLAB, KT), per_b),
            pl.BlockSpec((1, seq_len, KV_WIDTH), per_b),
            pl.BlockSpec((n_kt, VT_SLAB, KT), per_b),
            pl.BlockSpec(ov.shape, fixed),
        ],
        out_specs=pl.BlockSpec((TQ, BRANCH_WIDTH), qrow),
        out_shape=jax.ShapeDtypeStruct((batch * seq_len, BRANCH_WIDTH), BF16),
        scratch_shapes=scratch_shapes,
        compiler_params=pltpu.CompilerParams(
            dimension_semantics=("arbitrary", "arbitrary"),
            vmem_limit_bytes=V7X_VMEM_BYTES - VMEM_COMPILER_RESERVE),
        name="nsa",
    )(q, gate, kvc, kvc, ks.reshape(batch, seq_len, KV_WIDTH), vst,
      kw.reshape(batch, seq_len, KV_WIDTH), vwt, ov)


def _pool_kernel(u_ref, w_ref, scale_ref, _branches_hbm, o_ref):
    seq = u_ref.shape[0]
    t = lax.broadcasted_iota(jnp.int32, (seq, POOL_GROUP), 0)

    def shifted(x, k):
        return jnp.where(t >= k, pltpu.roll(x, k, axis=0), 0.0)

    for gi, w in enumerate(POOL_WINDOWS):
        sl = slice(gi * POOL_GROUP, (gi + 1) * POOL_GROUP)
        x = u_ref[:, sl]
        wsum = x
        span = 1
        while span < w:
            wsum = wsum + shifted(wsum, span)
            span *= 2
        cnt = jnp.minimum(t + 1, w).astype(F32)
        pooled = (wsum / cnt - x).astype(BF16)
        o_ref[:, sl] = (_dot(pooled, w_ref[gi]) * scale_ref[:, sl]).astype(BF16)


def _pool(pool_in, pool_w, pool_scale, branches, batch, seq_len):
    for w in POOL_WINDOWS:
        assert w & (w - 1) == 0
    blocks = (_nbytes((seq_len, POOL_WIDTH), F32) + _nbytes(pool_w.shape, BF16)
              + _nbytes((seq_len, POOL_WIDTH), BF16))
    return pl.pallas_call(
        _pool_kernel,
        grid=(batch,),
        in_specs=[
            pl.BlockSpec((seq_len, POOL_WIDTH), lambda b: (b, 0)),
            pl.BlockSpec(pool_w.shape, lambda b: (0, 0, 0)),
            pl.BlockSpec((1, POOL_WIDTH), lambda b: (0, 0)),
            pl.BlockSpec(memory_space=pl.ANY),
        ],
        out_specs=pl.BlockSpec((seq_len, POOL_WIDTH), lambda b: (b, POOL_COL_BLOCK)),
        out_shape=jax.ShapeDtypeStruct(branches.shape, branches.dtype),
        input_output_aliases={3: 0},
        compiler_params=pltpu.CompilerParams(
            dimension_semantics=("arbitrary",), vmem_limit_bytes=_vmem_limit(blocks)),
        name="pool",
    )(pool_in, pool_w.astype(BF16), pool_scale.reshape(1, -1), branches)


def _merge_kernel(x_ref, gpre_ref, wt_ref, nsa_ref, wa_ref, pool_ref, wp_ref, wo_ref, gpost_ref,
                  o_ref):
    rows = x_ref.shape[0] // MERGE_SPLIT
    for k in range(MERGE_SPLIT):
        rs = slice(k * rows, (k + 1) * rows)
        x = x_ref[rs, :]
        hn = _rms(x, gpre_ref[...])
        g_attn = jax.nn.sigmoid(_dot_nt(hn, wt_ref[0:D_MODEL, :]))
        g_pool = jax.nn.sigmoid(_dot_nt(hn, wt_ref[D_MODEL:2 * D_MODEL, :]))
        y = (g_attn * _dot(nsa_ref[rs, :].astype(F32), wa_ref[...])
             + g_pool * _dot(pool_ref[rs, :].astype(F32), wp_ref[...]))
        h = _dot(y, wo_ref[...])
        o_ref[rs, :] = x + _rms(h, gpost_ref[...])


def _merge(x1, g_pre, w_in_t, branches, w_attn, w_pool, w_out, g_post):
    t = x1.shape[0]
    row = lambda i: (i, 0)
    fixed = lambda i: (0, 0)
    once = pl.Buffered(1)
    tile = MERGE_SPLIT * TOK_TILE
    weights = (_nbytes((2 * D_MODEL, D_MODEL), F32) + _nbytes((Q_WIDTH, D_MODEL), F32)
               + _nbytes((POOL_WIDTH, D_MODEL), F32) + _nbytes((D_MODEL, D_MODEL), F32))
    blocks = (2 * _nbytes((tile, D_MODEL), F32) + _nbytes((tile, Q_WIDTH), BF16)
              + _nbytes((tile, POOL_WIDTH), BF16))
    gate_rows = pl.BlockSpec((pl.Element(2 * D_MODEL), pl.Element(D_MODEL)), lambda i: (OFF_MERGE, 0),
                             pipeline_mode=once)
    return pl.pallas_call(
        _merge_kernel,
        grid=(t // tile,),
        in_specs=[
            pl.BlockSpec((tile, D_MODEL), row),
            pl.BlockSpec((1, D_MODEL), fixed),
            gate_rows,
            pl.BlockSpec((tile, Q_WIDTH), row),
            pl.BlockSpec((Q_WIDTH, D_MODEL), fixed, pipeline_mode=once),
            pl.BlockSpec((tile, POOL_WIDTH), lambda i: (i, POOL_COL_BLOCK)),
            pl.BlockSpec((POOL_WIDTH, D_MODEL), fixed, pipeline_mode=once),
            pl.BlockSpec((D_MODEL, D_MODEL), fixed, pipeline_mode=once),
            pl.BlockSpec((1, D_MODEL), fixed),
        ],
        out_specs=pl.BlockSpec((tile, D_MODEL), row),
        out_shape=jax.ShapeDtypeStruct((t, D_MODEL), F32),
        compiler_params=pltpu.CompilerParams(
            dimension_semantics=("arbitrary",), vmem_limit_bytes=_vmem_limit(blocks, weights)),
        name="merge",
    )(x1, g_pre.reshape(1, -1), w_in_t, branches, w_attn, branches, w_pool, w_out, g_post.reshape(1, -1))


def _layer(x, tab, tab_cmp, batch, seq_len, g_ffn1_pre, w_ffn1_gate, w_ffn1_up, w_ffn1_down,
           g_ffn1_post, g_mix_pre, w_in, cmp_pe_k, cmp_w1_k, cmp_w2_k, cmp_pe_v, cmp_w1_v, cmp_w2_v,
           w_attn_branch, pool_w, pool_scale, w_pool_branch, w_out, g_mix_post, g_ffn2_pre,
           w_ffn2_gate, w_ffn2_up, w_ffn2_down, g_ffn2_post):
    x1 = _ffn(x, g_ffn1_pre, w_ffn1_gate, w_ffn1_up, w_ffn1_down, g_ffn1_post)
    w_in_t = w_in.T
    q, ks, kw, vst, vwt, cmp_rows, gate, pool_in = _in_proj(x1, g_mix_pre, w_in_t, tab)
    kvc = _compress(cmp_rows, cmp_pe_k, cmp_w1_k, cmp_w2_k, cmp_pe_v, cmp_w1_v, cmp_w2_v, tab_cmp)
    branches = _nsa(q, gate, kvc, ks, vst, kw, vwt, batch, seq_len)
    branches = _pool(pool_in, pool_w, pool_scale, branches, batch, seq_len)
    x2 = _merge(x1, g_mix_pre, w_in_t, branches, w_attn_branch, w_pool_branch, w_out, g_mix_post)
    return _ffn(x2, g_ffn2_pre, w_ffn2_gate, w_ffn2_up, w_ffn2_down, g_ffn2_post)


def kernel(x, positions, g_ffn1_pre, w_ffn1_gate, w_ffn1_up, w_ffn1_down, g_ffn1_post, g_mix_pre, w_in, cmp_pe_k, cmp_w1_k, cmp_w2_k, cmp_pe_v, cmp_w1_v, cmp_w2_v, w_attn_branch, pool_w, pool_scale, w_pool_branch, w_out, g_mix_post, g_ffn2_pre, w_ffn2_gate, w_ffn2_up, w_ffn2_down, g_ffn2_post):
    batch, seq_len, d_model = x.shape
    assert d_model == D_MODEL and seq_len % TOK_TILE == 0 and seq_len % TQ == 0
    assert TQ == KT and TQ % SEL_LEN == 0 and TOK_TILE % KT == 0 and WINDOW % KT == 0
    assert seq_len % CMP_STRIDE == 0 and CMP_LEN == 2 * CMP_STRIDE
    t = batch * seq_len
    tab = _rope_tables(positions.reshape(t), seq_len)
    n_cmp = (seq_len - CMP_LEN) // CMP_STRIDE + 1
    n_cmp_pad = seq_len // CMP_STRIDE
    pos_cmp = jnp.pad(positions[:, CMP_LEN - 1::CMP_STRIDE], ((0, 0), (0, n_cmp_pad - n_cmp)))
    tab_cmp = _rope_tables(pos_cmp.reshape(batch * n_cmp_pad), batch * n_cmp_pad)
    xf = x.reshape(t, D_MODEL)
    per_layer = (g_ffn1_pre, w_ffn1_gate, w_ffn1_up, w_ffn1_down, g_ffn1_post, g_mix_pre, w_in,
                 cmp_pe_k, cmp_w1_k, cmp_w2_k, cmp_pe_v, cmp_w1_v, cmp_w2_v, w_attn_branch, pool_w,
                 pool_scale, w_pool_branch, w_out, g_mix_post, g_ffn2_pre, w_ffn2_gate, w_ffn2_up,
                 w_ffn2_down, g_ffn2_post)
    for l in range(g_ffn1_pre.shape[0]):
        xf = _layer(xf, tab, tab_cmp, batch, seq_len, *(p[l] for p in per_layer))
    return xf.reshape(batch, seq_len, D_MODEL)
```

```python
import functools
import math

import jax
import jax.numpy as jnp
from jax import lax
from jax.experimental import pallas as pl
from jax.experimental.pallas import tpu as pltpu

F32 = jnp.float32
BF16 = jnp.bfloat16

D_MODEL = 1024
N_HEADS = 16
HEAD_DIM = 64
N_KV = 4
GROUP = N_HEADS // N_KV
ROT_DIM = HEAD_DIM // 4
ROT_HALF = ROT_DIM // 2
ROPE_THETA = 500000.0
CMP_LEN = 32
CMP_STRIDE = 16
CMP_HIDDEN = 2 * HEAD_DIM
SEL_LEN = 64
SEL_SHIFT = SEL_LEN.bit_length() - 1
N_SEL = 8
WINDOW = 512
POOL_WINDOWS = (2, 4, 8, 16)
POOL_WIDTH = D_MODEL // 2
POOL_GROUP = POOL_WIDTH // len(POOL_WINDOWS)
D_FF = 2816
EPS = 1e-6
NEG_INF = -1e30
FORCE_SCORE = 1e4
Q_WIDTH = N_HEADS * HEAD_DIM
KV_WIDTH = N_KV * HEAD_DIM
BRANCH_WIDTH = Q_WIDTH + POOL_WIDTH
POOL_COL_BLOCK = Q_WIDTH // POOL_WIDTH
assert POOL_COL_BLOCK * POOL_WIDTH == Q_WIDTH
N_GATES = 3 * N_HEADS
LOG2_E = math.log2(math.e)

LANES = 128
BF16_SUBLANES = 16
V7X_VMEM_BYTES = 64 * 1024 * 1024
VMEM_COMPILER_RESERVE = 4 * 1024 * 1024
VMEM_MIN_REQUEST = 32 * 1024 * 1024

SLOT = LANES
BLK_LANE0 = HEAD_DIM
VT_ROWS = HEAD_DIM + BF16_SUBLANES
ONE_ROW = HEAD_DIM
TOK_TILE = 512
TQ = 256
KT = 256
HALF = TQ // 2
CHAIN_COLS = 256
PIPE_DEPTH = 10
FF_CHUNK = 256
IN_SPLIT = 2
MERGE_SPLIT = 2
N_ROPE_TAB = 2


def _vmem_limit(block_bytes, scratch_bytes=0):
    need = 2 * block_bytes + scratch_bytes
    return int(min(V7X_VMEM_BYTES - VMEM_COMPILER_RESERVE, max(2 * need, VMEM_MIN_REQUEST)))


def _nbytes(shape, dtype):
    n = 1
    for s in shape:
        n *= s
    return n * jnp.dtype(dtype).itemsize


def _rms(xf, g):
    return xf * lax.rsqrt(jnp.mean(xf * xf, axis=-1, keepdims=True) + EPS) * g


def _dot(a, b):
    return jnp.dot(a, b, preferred_element_type=F32)


def _dot_nt(a, b):
    return lax.dot_general(a, b, (((1,), (1,)), ((), ())), preferred_element_type=F32)


def _rope_trig_kernel(pos_ref, inv_ref, tab_ref):
    tn = pos_ref.shape[1]
    ang = pos_ref[...].astype(F32) * inv_ref[...]
    c, s = jnp.cos(ang), jnp.sin(ang)
    rest = HEAD_DIM - ROT_DIM
    reps = SLOT // HEAD_DIM
    cos_rows = jnp.concatenate([c, c, jnp.ones((rest, tn), F32)] * reps, axis=0)
    sin_rows = jnp.concatenate([-s, s, jnp.zeros((rest, tn), F32)] * reps, axis=0)
    for j in range(tn // LANES):
        cs = slice(j * LANES, (j + 1) * LANES)
        tab_ref[cs, 0:SLOT] = cos_rows[:, cs].T
        tab_ref[cs, SLOT:2 * SLOT] = sin_rows[:, cs].T


def _rope_tables(pos_flat, tile):
    n = pos_flat.shape[0]
    inv = ROPE_THETA ** (-jnp.arange(ROT_HALF, dtype=F32) * (2.0 / ROT_DIM))
    return pl.pallas_call(
        _rope_trig_kernel,
        grid=(n // tile,),
        in_specs=[pl.BlockSpec((1, tile), lambda i: (0, i)),
                  pl.BlockSpec((ROT_HALF, 1), lambda i: (0, 0))],
        out_specs=pl.BlockSpec((tile, N_ROPE_TAB * SLOT), lambda i: (i, 0)),
        out_shape=jax.ShapeDtypeStruct((n, N_ROPE_TAB * SLOT), F32),
        compiler_params=pltpu.CompilerParams(dimension_semantics=("arbitrary",)),
        name="rope_trig",
    )(pos_flat.reshape(1, n), inv.reshape(ROT_HALF, 1))


def _rope_slot(y, tab):
    cos_t = tab[:, 0:SLOT]
    sin_t = tab[:, SLOT:2 * SLOT]
    lane = lax.broadcasted_iota(jnp.int32, y.shape, 1) & (HEAD_DIM - 1)
    up = pltpu.roll(y, SLOT - ROT_HALF, axis=1)
    down = pltpu.roll(y, ROT_HALF, axis=1)
    return y * cos_t + jnp.where(lane < ROT_HALF, up, down) * sin_t


def _ffn_kernel(x_ref, gpre_ref, wg_hbm, wu_hbm, wd_hbm, gpost_ref, o_ref, wg_ref, wu_ref, wd_ref, sem):
    n_chunks = D_FF // FF_CHUNK

    def chunk_copies(c):
        sl = slice(c * FF_CHUNK, (c + 1) * FF_CHUNK)
        return (pltpu.make_async_copy(wg_hbm.at[:, sl], wg_ref.at[:, sl], sem.at[0, c]),
                pltpu.make_async_copy(wu_hbm.at[:, sl], wu_ref.at[:, sl], sem.at[1, c]),
                pltpu.make_async_copy(wd_hbm.at[sl, :], wd_ref.at[sl, :], sem.at[2, c]))

    def body(first_step):
        x = x_ref[...]
        hn = _rms(x, gpre_ref[...])
        for c in range(n_chunks):
            sl = slice(c * FF_CHUNK, (c + 1) * FF_CHUNK)
            if first_step:
                for copy in chunk_copies(c):
                    copy.wait()
            g = _dot(hn, wg_ref[:, sl])
            u = _dot(hn, wu_ref[:, sl])
            d = _dot(g * jax.nn.sigmoid(g) * u, wd_ref[sl, :])
            if c == 0:
                o_ref[...] = d
            else:
                o_ref[...] += d
        o_ref[...] = x + 0.5 * _rms(o_ref[...], gpost_ref[...])

    @pl.when(pl.program_id(0) == 0)
    def _():
        for c in range(n_chunks):
            for copy in chunk_copies(c):
                copy.start()
        body(True)

    @pl.when(pl.program_id(0) > 0)
    def _():
        body(False)


def _ffn(x, g_pre, w_gate, w_up, w_down, g_post):
    t = x.shape[0]
    row = lambda i: (i, 0)
    fixed = lambda i: (0, 0)
    in_hbm = pl.BlockSpec(memory_space=pl.ANY)
    weights = 3 * _nbytes((D_MODEL, D_FF), F32)
    blocks = 2 * _nbytes((TOK_TILE, D_MODEL), F32)
    return pl.pallas_call(
        _ffn_kernel,
        grid=(t // TOK_TILE,),
        in_specs=[
            pl.BlockSpec((TOK_TILE, D_MODEL), row),
            pl.BlockSpec((1, D_MODEL), fixed),
            in_hbm, in_hbm, in_hbm,
            pl.BlockSpec((1, D_MODEL), fixed),
        ],
        out_specs=pl.BlockSpec((TOK_TILE, D_MODEL), row),
        out_shape=jax.ShapeDtypeStruct((t, D_MODEL), F32),
        scratch_shapes=[pltpu.VMEM((D_MODEL, D_FF), F32), pltpu.VMEM((D_MODEL, D_FF), F32),
                        pltpu.VMEM((D_FF, D_MODEL), F32),
                        pltpu.SemaphoreType.DMA((3, D_FF // FF_CHUNK))],
        compiler_params=pltpu.CompilerParams(
            dimension_semantics=("arbitrary",), vmem_limit_bytes=_vmem_limit(blocks, weights)),
        name="ffn",
    )(x, g_pre.reshape(1, -1), w_gate, w_up, w_down, g_post.reshape(1, -1))


OFF_Q = 0
OFF_CMP = OFF_Q + Q_WIDTH
OFF_KS = OFF_CMP + 2 * KV_WIDTH
OFF_VS = OFF_KS + KV_WIDTH
OFF_KW = OFF_VS + KV_WIDTH
OFF_VW = OFF_KW + KV_WIDTH
OFF_GATE = OFF_VW + KV_WIDTH
OFF_POOL = OFF_GATE + N_GATES
OFF_MERGE = OFF_POOL + POOL_WIDTH
W_CMP = 2 * KV_WIDTH
CHUNK_TOK = CMP_STRIDE
VT_SLAB = N_KV * VT_ROWS


def _in_proj_kernel(x_ref, g_ref, w_ref, tab_ref, q_ref, ks_ref, kw_ref, vst_ref,
                    vwt_ref, cmp_ref, gate_ref, pool_ref, cmp_scr):
    rows = x_ref.shape[0] // IN_SPLIT
    wide = 4 * SLOT
    aux = (lax.broadcasted_iota(jnp.int32, (VT_ROWS - HEAD_DIM, KT), 0) == 0).astype(BF16)

    def part(k):
        rs = slice(k * rows, (k + 1) * rows)
        hn = _rms(x_ref[rs, :], g_ref[...])
        tab = tab_ref[rs, :]
        tab_q = tab * (HEAD_DIM ** -0.5 * LOG2_E)
        for c in range(Q_WIDTH // wide):
            y = _dot_nt(hn, w_ref[OFF_Q + c * wide:OFF_Q + (c + 1) * wide, :])
            for j in range(wide // SLOT):
                sl = slice(c * wide + j * SLOT, c * wide + (j + 1) * SLOT)
                q_ref[rs, sl] = _rope_slot(y[:, j * SLOT:(j + 1) * SLOT], tab_q).astype(BF16)
        for k_ref, off in ((ks_ref, OFF_KS), (kw_ref, OFF_KW)):
            y = _dot_nt(hn, w_ref[off:off + KV_WIDTH, :])
            for j in range(KV_WIDTH // SLOT):
                sl = slice(j * SLOT, (j + 1) * SLOT)
                k_ref[rs, sl] = _rope_slot(y[:, sl], tab).astype(BF16)
        y = _dot_nt(hn, w_ref[OFF_CMP:OFF_CMP + W_CMP, :])
        for j in range(W_CMP // LANES):
            cmp_scr[j, rs, :] = y[:, j * LANES:(j + 1) * LANES]
        chunk_rows = rows // CHUNK_TOK
        low = lax.broadcasted_iota(jnp.int32, (chunk_rows, LANES), 1) < HEAD_DIM
        for j in range(W_CMP // LANES):
            for pair in range(CHUNK_TOK // 2):
                tok = [cmp_scr[j, pl.ds(k * rows + 2 * pair + e, chunk_rows, stride=CHUNK_TOK), :]
                       for e in range(2)]
                both = (jnp.where(low, tok[0], pltpu.roll(tok[1], HEAD_DIM, axis=1)),
                        jnp.where(low, pltpu.roll(tok[0], HEAD_DIM, axis=1), tok[1]))
                for e in range(2):
                    c0 = (2 * j + e) * CHUNK_TOK * HEAD_DIM + pair * LANES
                    cmp_ref[k * chunk_rows:(k + 1) * chunk_rows, c0:c0 + LANES] = both[e].astype(BF16)
        gate_ref[rs, :] = _dot_nt(hn, w_ref[OFF_GATE:OFF_GATE + SLOT, :])
        pool_ref[rs, :] = _dot_nt(hn, w_ref[OFF_POOL:OFF_MERGE, :])
        tiles = rows // KT
        for vt_ref, off in ((vst_ref, OFF_VS), (vwt_ref, OFF_VW)):
            vt = _dot_nt(w_ref[off:off + KV_WIDTH, :], hn).astype(BF16)
            for j in range(tiles):
                for g in range(N_KV):
                    vt_ref[k * tiles + j, g * VT_ROWS:g * VT_ROWS + HEAD_DIM, :] = (
                        vt[g * HEAD_DIM:(g + 1) * HEAD_DIM, j * KT:(j + 1) * KT])
                    vt_ref[k * tiles + j, g * VT_ROWS + HEAD_DIM:(g + 1) * VT_ROWS, :] = aux

    for k in range(IN_SPLIT):
        part(k)


def _in_proj(x1, g_pre, w_in_t, tab):
    t = x1.shape[0]
    row = lambda i: (i, 0)
    fixed = lambda i: (0, 0)
    once = pl.Buffered(1)
    tile = IN_SPLIT * TOK_TILE
    tiles = tile // KT
    chunk_rows = tile // CHUNK_TOK
    slab = pl.BlockSpec((tiles, VT_SLAB, KT), lambda i: (i, 0, 0))
    slab_shape = jax.ShapeDtypeStruct((t // KT, VT_SLAB, KT), BF16)
    out_specs = [
        pl.BlockSpec((tile, Q_WIDTH), row), pl.BlockSpec((tile, KV_WIDTH), row),
        pl.BlockSpec((tile, KV_WIDTH), row), slab, slab,
        pl.BlockSpec((chunk_rows, CHUNK_TOK * W_CMP), row),
        pl.BlockSpec((tile, SLOT), row), pl.BlockSpec((tile, POOL_WIDTH), row)]
    out_shape = [
        jax.ShapeDtypeStruct((t, Q_WIDTH), BF16), jax.ShapeDtypeStruct((t, KV_WIDTH), BF16),
        jax.ShapeDtypeStruct((t, KV_WIDTH), BF16), slab_shape, slab_shape,
        jax.ShapeDtypeStruct((t // CHUNK_TOK, CHUNK_TOK * W_CMP), BF16),
        jax.ShapeDtypeStruct((t, SLOT), F32), jax.ShapeDtypeStruct((t, POOL_WIDTH), F32)]
    weights = _nbytes((OFF_MERGE, D_MODEL), F32)
    blocks = (_nbytes((tile, D_MODEL), F32) + _nbytes((tile, N_ROPE_TAB * SLOT), F32)
              + _nbytes((tile, Q_WIDTH + 2 * KV_WIDTH + W_CMP), BF16)
              + 2 * _nbytes((tiles, VT_SLAB, KT), BF16) + _nbytes((tile, SLOT + POOL_WIDTH), F32))
    res = pl.pallas_call(
        _in_proj_kernel,
        grid=(t // tile,),
        in_specs=[
            pl.BlockSpec((tile, D_MODEL), row),
            pl.BlockSpec((1, D_MODEL), fixed),
            pl.BlockSpec((pl.Element(OFF_MERGE), pl.Element(D_MODEL)), fixed, pipeline_mode=once),
            pl.BlockSpec((tile, N_ROPE_TAB * SLOT), row),
        ],
        out_specs=out_specs,
        out_shape=out_shape,
        scratch_shapes=[pltpu.VMEM((W_CMP // LANES, tile, LANES), F32)],
        compiler_params=pltpu.CompilerParams(
            dimension_semantics=("arbitrary",),
            vmem_limit_bytes=_vmem_limit(blocks, weights + _nbytes((tile, W_CMP), F32))),
        name="in_proj",
    )(x1, g_pre.reshape(1, -1), w_in_t, tab)
    return res


def _compress_kernel(x_ref, w1_ref, pe_ref, w1_raw_ref, w2_ref, tab_ref, o_ref):
    rows = x_ref.shape[0]
    acc = _dot(x_ref[...], w1_ref[0])
    bias = _dot(pe_ref[0], w1_raw_ref[0])[0:1, :]
    second = pltpu.roll(acc[:, CMP_HIDDEN:2 * CMP_HIDDEN], rows - 1, axis=0)
    hid = jax.nn.gelu(acc[:, 0:CMP_HIDDEN] + second + bias).astype(BF16)
    out = _dot(hid, w2_ref[0])
    is_key = pl.program_id(0) < N_KV
    tab = pltpu.roll(tab_ref[...], rows - 1, axis=0)
    o_ref[...] = jnp.where(is_key, _rope_slot(out, tab), out).astype(BF16)


def _compress(cmp_rows, cmp_pe_k, cmp_w1_k, cmp_w2_k, cmp_pe_v, cmp_w1_v, cmp_w2_v, tab):
    rows = cmp_rows.shape[0]
    chunk_feat = CHUNK_TOK * HEAD_DIM
    sub = 8

    def stacked(f, k, v):
        return jnp.stack([f(k), f(v)]).astype(BF16)

    w1 = stacked(lambda w: jnp.concatenate([w[:chunk_feat], w[chunk_feat:]], axis=1), cmp_w1_k, cmp_w1_v)
    w1_raw = stacked(lambda w: w, cmp_w1_k, cmp_w1_v)
    pe = stacked(lambda p: jnp.broadcast_to(p.reshape(1, -1), (sub, CMP_LEN * HEAD_DIM)), cmp_pe_k, cmp_pe_v)
    w2 = stacked(lambda w: jnp.pad(w, ((0, 0), (0, SLOT - HEAD_DIM))), cmp_w2_k, cmp_w2_v)
    per_tensor = lambda s: (s // N_KV, 0, 0)
    return pl.pallas_call(
        _compress_kernel,
        grid=(2 * N_KV,),
        in_specs=[
            pl.BlockSpec((rows, chunk_feat), lambda s: (0, s)),
            pl.BlockSpec((1,) + w1.shape[1:], per_tensor),
            pl.BlockSpec((1,) + pe.shape[1:], per_tensor),
            pl.BlockSpec((1,) + w1_raw.shape[1:], per_tensor),
            pl.BlockSpec((1,) + w2.shape[1:], per_tensor),
            pl.BlockSpec((rows, N_ROPE_TAB * SLOT), lambda s: (0, CHUNK_TOK - 1)),
        ],
        out_specs=pl.BlockSpec((rows, SLOT), lambda s: (0, s)),
        out_shape=jax.ShapeDtypeStruct((rows, 2 * N_KV * SLOT), BF16),
        compiler_params=pltpu.CompilerParams(
            dimension_semantics=("arbitrary",),
            vmem_limit_bytes=V7X_VMEM_BYTES - VMEM_COMPILER_RESERVE),
        name="compress",
    )(cmp_rows, w1, pe, w1_raw, w2, tab.reshape(rows, CHUNK_TOK * N_ROPE_TAB * SLOT))


def _nsa_kernel(q_ref, gate_ref, kc_ref, vc_ref, ks_ref, vst_ref, kw_ref, vwt_ref, ov_ref,
                o_ref, ksp_ref, kwp_ref, qg_ref, mix_ref, *state_refs):
    i = pl.program_id(1)
    seq = ks_ref.shape[1]
    cols_g = GROUP * TQ
    chains_g = cols_g // CHAIN_COLS
    n_chains = N_KV * chains_g
    n_cmp = kc_ref.shape[1]
    sel_state = (state_refs[0:n_chains], state_refs[n_chains:2 * n_chains])
    win_state = (state_refs[2 * n_chains:3 * n_chains], state_refs[3 * n_chains:4 * n_chains])
    lane_q = lax.broadcasted_iota(jnp.int32, (TQ, SLOT), 1)

    def head_slot(x_ref, rows, j):
        two = x_ref[rows, (j // 2) * SLOT:(j // 2 + 1) * SLOT]
        if j % 2:
            words = pltpu.roll(pltpu.bitcast(two, jnp.uint32), HEAD_DIM, axis=1)
            two = pltpu.bitcast(words, BF16)
        return two

    @pl.when(i == 0)
    def _():
        for c in range(seq // KT):
            rows = pl.ds(c * KT, KT)
            lane = lax.broadcasted_iota(jnp.int32, (KT, SLOT), 1)
            key_blk = (c * KT + lax.broadcasted_iota(jnp.int32, (KT, SLOT), 0)) >> SEL_SHIFT
            onehot = jnp.where(lane - BLK_LANE0 == key_blk, 1.0, 0.0).astype(BF16)
            zeros = jnp.zeros((KT, SLOT), BF16)
            for g in range(N_KV):
                gs = slice(g * SLOT, (g + 1) * SLOT)
                ksp_ref[rows, gs] = jnp.where(lane < HEAD_DIM, head_slot(ks_ref.at[0], rows, g), onehot)
                kwp_ref[rows, gs] = jnp.where(lane < HEAD_DIM, head_slot(kw_ref.at[0], rows, g), zeros)

    def q_cols(r, x):
        return slice((x * GROUP + r) * HALF, (x * GROUP + r + 1) * HALF)

    def chain_half(c):
        return c * CHAIN_COLS // (GROUP * HALF)

    for h in range(N_HEADS):
        g, r = divmod(h, GROUP)
        q_pad = jnp.where(lane_q < HEAD_DIM, head_slot(q_ref, slice(None), h), jnp.zeros((TQ, SLOT), BF16))
        q_pad_t = q_pad.astype(F32).T.astype(BF16)
        for x in range(2):
            qg_ref[g, :, q_cols(r, x)] = q_pad_t[:, x * HALF:(x + 1) * HALF]

    gates_t = jax.nn.sigmoid(gate_ref[...]).T
    key_h = lax.broadcasted_iota(jnp.int32, (HALF, CHAIN_COLS), 0)
    qry_h = lax.broadcasted_iota(jnp.int32, (HALF, CHAIN_COLS), 1) & (HALF - 1)
    tri_le, tri_gt = key_h <= qry_h, key_h > qry_h
    diag_plan = ((slice(0, HALF), tri_le, slice(0, HALF)), (slice(0, KT), tri_le, slice(HALF, KT)))
    far_plan = ((slice(0, KT), tri_gt, slice(0, HALF)), (slice(HALF, KT), tri_gt, slice(0, HALF)))

    def gate_row(branch, h):
        c = branch * N_HEADS + h
        return gates_t[c:c + 1, :]

    sub = 8
    cmp_per_sub = sub * SEL_LEN // CMP_STRIDE

    def compressed_scores(ng):
        n_c = ng * cmp_per_sub
        cmp_scores = []
        for h in range(N_HEADS):
            g, r = divmod(h, GROUP)
            q_h = jnp.concatenate([qg_ref[g, :, q_cols(r, 0)], qg_ref[g, :, q_cols(r, 1)]], axis=1)
            cmp_scores.append(_dot(kc_ref[0, 0:n_c, g * SLOT:(g + 1) * SLOT], q_h))
        return cmp_scores

    def compressed_and_select(ng, start, cmp_scores):
        n_blk, n_c = ng * sub, ng * cmp_per_sub
        t_cmp = start + lax.broadcasted_iota(jnp.int32, (n_c, TQ), 1)
        n_idx = lax.broadcasted_iota(jnp.int32, (n_c, TQ), 0)
        cmp_valid = n_idx * CMP_STRIDE + (CMP_LEN - 1) <= t_cmp
        any_cmp = start + lax.broadcasted_iota(jnp.int32, (1, TQ), 1) >= CMP_LEN - 1
        t_row = start + lax.broadcasted_iota(jnp.int32, (n_blk, TQ), 1)
        blk = lax.broadcasted_iota(jnp.int32, (n_blk, TQ), 0)
        forced = (blk == t_row >> SEL_SHIFT) | (blk == 0)
        causal_blk = blk * SEL_LEN <= t_row
        idx8 = lax.broadcasted_iota(jnp.int32, (sub, TQ), 0)
        pad_c = jnp.zeros((n_cmp - n_c, TQ), F32)
        for g in range(N_KV):
            vct_g = vc_ref[0, :, g * SLOT:(g + 1) * SLOT].astype(F32).T[0:HEAD_DIM].astype(BF16)
            p_sum = jnp.zeros((n_c, TQ), F32)
            for r in range(GROUP):
                h = g * GROUP + r
                s = jnp.where(cmp_valid, cmp_scores[h], NEG_INF)
                p = jnp.exp2(s - jnp.max(s, axis=0, keepdims=True))
                inv = jnp.where(any_cmp, 1.0 / jnp.sum(p, axis=0, keepdims=True), 0.0)
                p = p * inv
                p_sum = p_sum + p
                p_all = jnp.concatenate([p, pad_c], axis=0) if ng * cmp_per_sub < n_cmp else p
                mix_ref[h] = gate_row(0, h) * _dot(vct_g, p_all.astype(BF16))
            p_sum_all = jnp.concatenate([p_sum, pad_c], axis=0) if ng * cmp_per_sub < n_cmp else p_sum
            imp = jnp.dot(ov_ref[0:n_blk, :], p_sum_all, precision=lax.Precision.HIGHEST,
                          preferred_element_type=F32)
            score = jnp.where(causal_blk, jnp.where(forced, FORCE_SCORE, imp), NEG_INF)
            rows8 = [score[k * sub:(k + 1) * sub] for k in range(ng)]
            ranks = [jnp.zeros((sub, TQ), jnp.int32) for _ in rows8]
            for c in range(n_blk):
                other = score[c:c + 1, :]
                for k, mine in enumerate(rows8):
                    if k * sub > c:
                        beats = other >= mine
                    elif (k + 1) * sub - 1 <= c:
                        beats = other > mine
                    else:
                        beats = (other > mine) | ((other == mine) & (idx8 > c - k * sub))
                    ranks[k] = ranks[k] + beats.astype(jnp.int32)
            rank = jnp.concatenate(ranks, axis=0) if ng > 1 else ranks[0]
            bias = jnp.where(causal_blk & (rank < N_SEL), 0.0, NEG_INF).astype(BF16)
            for r in range(GROUP):
                for x in range(2):
                    qg_ref[g, BLK_LANE0:BLK_LANE0 + n_blk, q_cols(r, x)] = bias[:, x * HALF:(x + 1) * HALF]

    ng_now = ((i + 1) * TQ + sub * SEL_LEN - 1) // (sub * SEL_LEN)
    for ng in range(1, seq // (sub * SEL_LEN) + 1):

        @pl.when(ng_now == ng)
        def _(ng=ng):
            compressed_and_select(ng, i * TQ, compressed_scores(ng))

    def flash_init(state):
        for m_ref, acc_ref in zip(*state):
            m_ref[...] = jnp.full(m_ref.shape, NEG_INF, F32)
            acc_ref[...] = jnp.zeros(acc_ref.shape, F32)

    def flash_tiles(tiles):
        jobs = [(tile, c) for tile in tiles for c in range(n_chains)]

        def key_plan(job):
            (_, _, _, plan, _), chain = job
            return (slice(0, KT), None, None) if plan is None else plan[chain_half(chain % chains_g)]

        scores = {}
        for step in range(len(jobs) + PIPE_DEPTH):
            if step < len(jobs):
                (kp_ref, _, kt, _, _), chain = jobs[step]
                g, c = divmod(chain, chains_g)
                keys, mask, rows = key_plan(jobs[step])
                off = pl.multiple_of(kt * KT + keys.start, HALF)
                s = _dot(kp_ref[pl.ds(off, keys.stop - keys.start), g * SLOT:(g + 1) * SLOT],
                         qg_ref[g, :, c * CHAIN_COLS:(c + 1) * CHAIN_COLS])
                if mask is not None:
                    parts = [s[0:rows.start], jnp.where(mask, s[rows], NEG_INF), s[rows.stop:]]
                    s = jnp.concatenate([p for p in parts if p.shape[0]], axis=0)
                scores[step] = s.astype(BF16)
            done = step - PIPE_DEPTH
            if done >= 0:
                (_, vt_ref, kt, _, (m_refs, acc_refs)), chain = jobs[done]
                g = chain // chains_g
                keys = key_plan(jobs[done])[0]
                m_ref, acc_ref = m_refs[chain], acc_refs[chain]
                s = scores.pop(done)
                m_old = m_ref[...]
                packed = [s[r * BF16_SUBLANES:(r + 1) * BF16_SUBLANES]
                          for r in range(s.shape[0] // BF16_SUBLANES)]
                m_tile = functools.reduce(jnp.maximum, packed).astype(F32)
                m_new = jnp.maximum(m_old, jnp.max(m_tile, axis=0, keepdims=True))
                p = jnp.exp2(s - m_new.astype(BF16))
                pv = _dot(vt_ref[kt, g * VT_ROWS:(g + 1) * VT_ROWS, keys], p)
                acc_ref[...] = jnp.exp2(m_old - m_new) * acc_ref[...] + pv
                m_ref[...] = m_new

    def flash_mix(branch, state):
        _, acc_refs = state
        for h in range(N_HEADS):
            g, r = divmod(h, GROUP)
            gate = gate_row(branch, h)
            for x in range(2):
                c, c0 = divmod(q_cols(r, x).start, CHAIN_COLS)
                acc = acc_refs[g * chains_g + c][:, c0:c0 + HALF]
                qs = slice(x * HALF, (x + 1) * HALF)
                scale = gate[:, qs] * (1.0 / acc[ONE_ROW:ONE_ROW + 1, :])
                mix_ref[h, :, qs] = mix_ref[h, :, qs] + scale * acc[0:HEAD_DIM]

    flash_init(sel_state)
    flash_init(win_state)

    def sel_past(kt):
        return (ksp_ref, vst_ref, kt, None, sel_state)

    def sel_pair(j, carry):
        flash_tiles([sel_past(2 * j), sel_past(2 * j + 1)])
        return carry

    lax.fori_loop(0, i >> 1, sel_pair, 0)

    @pl.when(i & 1 == 1)
    def _():
        flash_tiles([sel_past(i - 1)])

    n_back = WINDOW // KT
    sel_diag = (ksp_ref, vst_ref, i, diag_plan, sel_state)
    win_diag = (kwp_ref, vwt_ref, i, diag_plan, win_state)

    def win_back(d):
        return (kwp_ref, vwt_ref, i - d, far_plan if d == n_back else None, win_state)

    for have in range(n_back + 1):
        cond = (i == have) if have < n_back else (i >= have)

        @pl.when(cond)
        def _(have=have):
            flash_tiles([sel_diag] + [win_back(d) for d in range(have, 0, -1)] + [win_diag])

    flash_mix(1, sel_state)
    flash_mix(2, win_state)
    for h2 in range(N_HEADS // 2):
        pair = jnp.concatenate([mix_ref[2 * h2], mix_ref[2 * h2 + 1]], axis=0)
        o_ref[:, h2 * SLOT:(h2 + 1) * SLOT] = pair.T.astype(BF16)
    o_ref[:, Q_WIDTH:] = jnp.zeros((TQ, o_ref.shape[1] - Q_WIDTH), BF16)


def _overlap(n_cmp_pad, n_slc):
    c0 = jnp.arange(n_cmp_pad) * CMP_STRIDE
    s0 = jnp.arange(n_slc) * SEL_LEN
    ov = jnp.minimum(c0[None, :] + CMP_LEN, s0[:, None] + SEL_LEN) - jnp.maximum(c0[None, :], s0[:, None])
    return jnp.clip(ov, 0).astype(F32) / CMP_LEN


def _nsa(q, gate, kvc, ks, vst, kw, vwt, batch, seq_len):
    n_slc = seq_len // SEL_LEN
    n_cmp_pad = kvc.shape[0] // batch
    w_kv_pad = N_KV * SLOT
    kvc = kvc.reshape(batch, n_cmp_pad, 2 * w_kv_pad)
    assert n_slc <= SLOT - BLK_LANE0 and n_slc % 8 == 0
    assert CHAIN_COLS % HALF == 0 and (GROUP * HALF) % CHAIN_COLS == 0 and KT == 2 * HALF
    ov = _overlap(n_cmp_pad, n_slc)
    nq = seq_len // TQ
    n_kt = seq_len // KT
    qrow = lambda b, i: (b * nq + i, 0)
    per_b = lambda b, i: (b, 0, 0)
    fixed = lambda b, i: (0, 0)
    n_chains = N_KV * GROUP * TQ // CHAIN_COLS
    run_max = [pltpu.VMEM((1, CHAIN_COLS), F32)] * n_chains
    run_acc = [pltpu.VMEM((VT_ROWS, CHAIN_COLS), F32)] * n_chains
    scratch_shapes = (
        [pltpu.VMEM((seq_len, w_kv_pad), BF16),
         pltpu.VMEM((seq_len, w_kv_pad), BF16),
         pltpu.VMEM((N_KV, SLOT, GROUP * TQ), BF16),
         pltpu.VMEM((N_HEADS, HEAD_DIM, TQ), F32)]
        + run_max + run_acc + run_max + run_acc)
    return pl.pallas_call(
        _nsa_kernel,
        grid=(batch, nq),
        in_specs=[
            pl.BlockSpec((TQ, Q_WIDTH), qrow),
            pl.BlockSpec((TQ, SLOT), qrow),
            pl.BlockSpec((1, n_cmp_pad, w_kv_pad), per_b),
            pl.BlockSpec((1, n_cmp_pad, w_kv_pad), lambda b, i: (b, 0, 1)),
            pl.BlockSpec((1, seq_len, KV_WIDTH), per_b),
            pl.BlockSpec((n_kt, VT_SLAB, KT), per_b),
            pl.BlockSpec((1, seq_len, KV_WIDTH), per_b),
            pl.BlockSpec((n_kt, VT_SLAB, KT), per_b),
            pl.BlockSpec(ov.shape, fixed),
        ],
        out_specs=pl.BlockSpec((TQ, BRANCH_WIDTH), qrow),
        out_shape=jax.ShapeDtypeStruct((batch * seq_len, BRANCH_WIDTH), BF16),
        scratch_shapes=scratch_shapes,
        compiler_params=pltpu.CompilerParams(
            dimension_semantics=("arbitrary", "arbitrary"),
            vmem_limit_bytes=V7X_VMEM_BYTES - VMEM_COMPILER_RESERVE),
        name="nsa",
    )(q, gate, kvc, kvc, ks.reshape(batch, seq_len, KV_WIDTH), vst,
      kw.reshape(batch, seq_len, KV_WIDTH), vwt, ov)


def _pool_kernel(u_ref, w_ref, scale_ref, _branches_hbm, o_ref):
    seq = u_ref.shape[0]
    t = lax.broadcasted_iota(jnp.int32, (seq, POOL_GROUP), 0)

    def shifted(x, k):
        return jnp.where(t >= k, pltpu.roll(x, k, axis=0), 0.0)

    for gi, w in enumerate(POOL_WINDOWS):
        sl = slice(gi * POOL_GROUP, (gi + 1) * POOL_GROUP)
        x = u_ref[:, sl]
        wsum = x
        span = 1
        while span < w:
            wsum = wsum + shifted(wsum, span)
            span *= 2
        cnt = jnp.minimum(t + 1, w).astype(F32)
        pooled = (wsum / cnt - x).astype(BF16)
        o_ref[:, sl] = (_dot(pooled, w_ref[gi]) * scale_ref[:, sl]).astype(BF16)


def _pool(pool_in, pool_w, pool_scale, branches, batch, seq_len):
    for w in POOL_WINDOWS:
        assert w & (w - 1) == 0
    blocks = (_nbytes((seq_len, POOL_WIDTH), F32) + _nbytes(pool_w.shape, BF16)
              + _nbytes((seq_len, POOL_WIDTH), BF16))
    return pl.pallas_call(
        _pool_kernel,
        grid=(batch,),
        in_specs=[
            pl.BlockSpec((seq_len, POOL_WIDTH), lambda b: (b, 0)),
            pl.BlockSpec(pool_w.shape, lambda b: (0, 0, 0)),
            pl.BlockSpec((1, POOL_WIDTH), lambda b: (0, 0)),
            pl.BlockSpec(memory_space=pl.ANY),
        ],
        out_specs=pl.BlockSpec((seq_len, POOL_WIDTH), lambda b: (b, POOL_COL_BLOCK)),
        out_shape=jax.ShapeDtypeStruct(branches.shape, branches.dtype),
        input_output_aliases={3: 0},
        compiler_params=pltpu.CompilerParams(
            dimension_semantics=("arbitrary",), vmem_limit_bytes=_vmem_limit(blocks)),
        name="pool",
    )(pool_in, pool_w.astype(BF16), pool_scale.reshape(1, -1), branches)


def _merge_kernel(x_ref, gpre_ref, wt_ref, nsa_ref, wa_ref, pool_ref, wp_ref, wo_ref, gpost_ref,
                  o_ref):
    rows = x_ref.shape[0] // MERGE_SPLIT
    for k in range(MERGE_SPLIT):
        rs = slice(k * rows, (k + 1) * rows)
        x = x_ref[rs, :]
        hn = _rms(x, gpre_ref[...])
        g_attn = jax.nn.sigmoid(_dot_nt(hn, wt_ref[0:D_MODEL, :]))
        g_pool = jax.nn.sigmoid(_dot_nt(hn, wt_ref[D_MODEL:2 * D_MODEL, :]))
        y = (g_attn * _dot(nsa_ref[rs, :].astype(F32), wa_ref[...])
             + g_pool * _dot(pool_ref[rs, :].astype(F32), wp_ref[...]))
        h = _dot(y, wo_ref[...])
        o_ref[rs, :] = x + _rms(h, gpost_ref[...])


def _merge(x1, g_pre, w_in_t, branches, w_attn, w_pool, w_out, g_post):
    t = x1.shape[0]
    row = lambda i: (i, 0)
    fixed = lambda i: (0, 0)
    once = pl.Buffered(1)
    tile = MERGE_SPLIT * TOK_TILE
    weights = (_nbytes((2 * D_MODEL, D_MODEL), F32) + _nbytes((Q_WIDTH, D_MODEL), F32)
               + _nbytes((POOL_WIDTH, D_MODEL), F32) + _nbytes((D_MODEL, D_MODEL), F32))
    blocks = (2 * _nbytes((tile, D_MODEL), F32) + _nbytes((tile, Q_WIDTH), BF16)
              + _nbytes((tile, POOL_WIDTH), BF16))
    gate_rows = pl.BlockSpec((pl.Element(2 * D_MODEL), pl.Element(D_MODEL)), lambda i: (OFF_MERGE, 0),
                             pipeline_mode=once)
    return pl.pallas_call(
        _merge_kernel,
        grid=(t // tile,),
        in_specs=[
            pl.BlockSpec((tile, D_MODEL), row),
            pl.BlockSpec((1, D_MODEL), fixed),
            gate_rows,
            pl.BlockSpec((tile, Q_WIDTH), row),
            pl.BlockSpec((Q_WIDTH, D_MODEL), fixed, pipeline_mode=once),
            pl.BlockSpec((tile, POOL_WIDTH), lambda i: (i, POOL_COL_BLOCK)),
            pl.BlockSpec((POOL_WIDTH, D_MODEL), fixed, pipeline_mode=once),
            pl.BlockSpec((D_MODEL, D_MODEL), fixed, pipeline_mode=once),
            pl.BlockSpec((1, D_MODEL), fixed),
        ],
        out_specs=pl.BlockSpec((tile, D_MODEL), row),
        out_shape=jax.ShapeDtypeStruct((t, D_MODEL), F32),
        compiler_params=pltpu.CompilerParams(
            dimension_semantics=("arbitrary",), vmem_limit_bytes=_vmem_limit(blocks, weights)),
        name="merge",
    )(x1, g_pre.reshape(1, -1), w_in_t, branches, w_attn, branches, w_pool, w_out, g_post.reshape(1, -1))


def _layer(x, tab, batch, seq_len, g_ffn1_pre, w_ffn1_gate, w_ffn1_up, w_ffn1_down,
           g_ffn1_post, g_mix_pre, w_in, cmp_pe_k, cmp_w1_k, cmp_w2_k, cmp_pe_v, cmp_w1_v, cmp_w2_v,
           w_attn_branch, pool_w, pool_scale, w_pool_branch, w_out, g_mix_post, g_ffn2_pre,
           w_ffn2_gate, w_ffn2_up, w_ffn2_down, g_ffn2_post):
    x1 = _ffn(x, g_ffn1_pre, w_ffn1_gate, w_ffn1_up, w_ffn1_down, g_ffn1_post)
    w_in_t = w_in.T
    q, ks, kw, vst, vwt, cmp_rows, gate, pool_in = _in_proj(x1, g_mix_pre, w_in_t, tab)
    kvc = _compress(cmp_rows, cmp_pe_k, cmp_w1_k, cmp_w2_k, cmp_pe_v, cmp_w1_v, cmp_w2_v, tab)
    branches = _nsa(q, gate, kvc, ks, vst, kw, vwt, batch, seq_len)
    branches = _pool(pool_in, pool_w, pool_scale, branches, batch, seq_len)
    x2 = _merge(x1, g_mix_pre, w_in_t, branches, w_attn_branch, w_pool_branch, w_out, g_mix_post)
    return _ffn(x2, g_ffn2_pre, w_ffn2_gate, w_ffn2_up, w_ffn2_down, g_ffn2_post)


def kernel(x, positions, g_ffn1_pre, w_ffn1_gate, w_ffn1_up, w_ffn1_down, g_ffn1_post, g_mix_pre, w_in, cmp_pe_k, cmp_w1_k, cmp_w2_k, cmp_pe_v, cmp_w1_v, cmp_w2_v, w_attn_branch, pool_w, pool_scale, w_pool_branch, w_out, g_mix_post, g_ffn2_pre, w_ffn2_gate, w_ffn2_up, w_ffn2_down, g_ffn2_post):
    batch, seq_len, d_model = x.shape
    assert d_model == D_MODEL and seq_len % TOK_TILE == 0 and seq_len % TQ == 0
    assert TQ == KT and TQ % SEL_LEN == 0 and TOK_TILE % KT == 0 and WINDOW % KT == 0
    assert seq_len % CMP_STRIDE == 0 and CMP_LEN == 2 * CMP_STRIDE
    t = batch * seq_len
    tab = _rope_tables(positions.reshape(t), seq_len)
    xf = x.reshape(t, D_MODEL)
    per_layer = (g_ffn1_pre, w_ffn1_gate, w_ffn1_up, w_ffn1_down, g_ffn1_post, g_mix_pre, w_in,
                 cmp_pe_k, cmp_w1_k, cmp_w2_k, cmp_pe_v, cmp_w1_v, cmp_w2_v, w_attn_branch, pool_w,
                 pool_scale, w_pool_branch, w_out, g_mix_post, g_ffn2_pre, w_ffn2_gate, w_ffn2_up,
                 w_ffn2_down, g_ffn2_post)
    for l in range(g_ffn1_pre.shape[0]):
        xf = _layer(xf, tab, batch, seq_len, *(p[l] for p in per_layer))
    return xf.reshape(batch, seq_len, D_MODEL)
```
